```python
import jax, jax.numpy as jnp
from jax import lax
import numpy as np

D_MODEL = 1024
BATCH = 1
SEQ = 16384
DEPTH = 1
DEC_BATCH = 16
DEC_SEQ = 32
PAST_LEN = 2048

CHUNK = 64
EPS = 1e-6
MLA_HEADS = 4
Q_LORA = 384
KV_LORA = 256
NOPE_DIM = 128
ROPE_DIM = 64
V_DIM = 128
QK_DIM = NOPE_DIM + ROPE_DIM
ROPE_THETA = 10000.0
Q_BLOCK = 128
GLA_HEADS = 4
GLA_DK = 64
GLA_DV = 128
GATE_RANK = 16
GATE_NORM = 16.0
MLA_OUT = MLA_HEADS * V_DIM
GLA_OUT = GLA_HEADS * GLA_DV
MIX_WIDTH = MLA_OUT + GLA_OUT
IN_SIZES = (Q_LORA, KV_LORA, ROPE_DIM, GLA_HEADS * GLA_DK, GLA_HEADS * GLA_DK, GLA_OUT, GATE_RANK, GLA_OUT)
IN_WIDTH = sum(IN_SIZES)
N_EXPERTS = 32
TOP_K = 4
D_FF = D_MODEL
SWIGLU_LIMIT = 7.0
SWIGLU_ALPHA = 1.702
EXPERT_BLOCK = 128

kernel_name = 'hybrid_mla_gla_moe_streaming_step'


def rms_norm(x, g):
    xf = x.astype(jnp.float32)
    y = xf * lax.rsqrt(jnp.mean(xf * xf, axis=-1, keepdims=True) + EPS)
    return (y * g.astype(jnp.float32)).astype(x.dtype)


def rope(x, pos):
    half = ROPE_DIM // 2
    inv = ROPE_THETA ** (-jnp.arange(half, dtype=jnp.float32) / half)
    ang = pos.astype(jnp.float32)[:, None] * inv[None, :]
    cos = jnp.cos(ang)[:, None, :]
    sin = jnp.sin(ang)[:, None, :]
    xf = x.astype(jnp.float32)
    x1, x2 = xf[..., :half], xf[..., half:]
    return jnp.concatenate([x1 * cos - x2 * sin, x2 * cos + x1 * sin], axis=-1).astype(x.dtype)


def mla_expand(lat, kr, w_ukv, g_k):
    B, S, _ = lat.shape
    kv = (lat @ w_ukv).reshape(B, S, MLA_HEADS, NOPE_DIM + V_DIM)
    k_nope, v = kv[..., :NOPE_DIM], kv[..., NOPE_DIM:]
    k = jnp.concatenate([k_nope, jnp.broadcast_to(kr[:, :, None, :], (B, S, MLA_HEADS, ROPE_DIM))], axis=-1)
    return rms_norm(k, g_k), v


def block_causal_attention(q, k, v, q_pos, k_pos):
    B, Sq, H, D = q.shape
    scale = D ** -0.5
    k_chunk = k_pos // CHUNK

    def attend(qb, qpb):
        s = jnp.einsum('bqhd,bkhd->bhqk', qb, k).astype(jnp.float32) * scale
        mask = k_chunk[None, :] <= (qpb // CHUNK)[:, None]
        s = jnp.where(mask[None, None], s, jnp.float32(-1e30))
        p = jax.nn.softmax(s, axis=-1).astype(v.dtype)
        return jnp.einsum('bhqk,bkhd->bqhd', p, v)

    if Sq <= Q_BLOCK:
        return attend(q, q_pos)
    nb = Sq // Q_BLOCK
    qb = q.reshape(B, nb, Q_BLOCK, H, D).transpose(1, 0, 2, 3, 4)
    out = lax.map(lambda a: attend(a[0], a[1]), (qb, q_pos.reshape(nb, Q_BLOCK)))
    return out.transpose(1, 0, 2, 3, 4).reshape(B, Sq, H, v.shape[-1])


def gla_recurrence(q, k, v, g, s0):
    B, L, H, DK = q.shape
    DV = v.shape[-1]
    C = min(CHUNK, L)
    N = L // C

    def to_blocks(t):
        return t.astype(jnp.float32).reshape(B, N, C, H, t.shape[-1]).transpose(1, 0, 2, 3, 4)

    causal = jnp.tril(jnp.ones((C, C), dtype=bool))[None, :, :, None, None]

    def step(S, inp):
        qc, kc, vc, gc = inp
        b = jnp.cumsum(gc, axis=1)
        decay = jnp.exp(jnp.where(causal, b[:, :, None] - b[:, None, :], -jnp.inf))
        A = jnp.sum(qc[:, :, None] * kc[:, None, :] * decay, axis=-1)
        o = jnp.einsum('bijh,bjhv->bihv', A, vc) + jnp.einsum('bihd,bhdv->bihv', qc * jnp.exp(b), S)
        bl = b[:, -1]
        S = jnp.exp(bl)[..., None] * S + jnp.einsum('bjhd,bjhv->bhdv', kc * jnp.exp(bl[:, None] - b), vc)
        return S, o

    S, o = lax.scan(step, s0.astype(jnp.float32), (to_blocks(q), to_blocks(k), to_blocks(v), to_blocks(g)))
    o = o.transpose(1, 0, 2, 3, 4).reshape(B, L, H, DV)
    return o, S


def moe(h, w_router, b_router, w_gu, b_gu, w_down, b_down):
    B, S, D = h.shape
    T = B * S
    xt = h.reshape(T, D)
    logits = (xt @ w_router + b_router).astype(jnp.float32)
    top_val, top_idx = lax.top_k(logits, TOP_K)
    wts = jax.nn.softmax(top_val, axis=-1)
    M = T * TOP_K
    e_flat = top_idx.reshape(M)
    tok_flat = jnp.arange(M, dtype=jnp.int32) // TOP_K
    order = jnp.argsort(e_flat, stable=True)
    e_sorted = e_flat[order]
    tok_sorted = tok_flat[order]
    w_sorted = wts.reshape(M)[order]
    counts = jnp.bincount(e_flat, length=N_EXPERTS)
    starts = jnp.cumsum(counts) - counts
    padded = ((counts + EXPERT_BLOCK - 1) // EXPERT_BLOCK) * EXPERT_BLOCK
    pends = jnp.cumsum(padded)
    pstarts = pends - padded
    dest = pstarts[e_sorted] + (jnp.arange(M, dtype=jnp.int32) - starts[e_sorted])
    nblk = -(-M // EXPERT_BLOCK) + N_EXPERTS
    xpad = jnp.zeros((nblk * EXPERT_BLOCK, D), xt.dtype).at[dest].set(xt[tok_sorted])
    blk_e = jnp.clip(jnp.searchsorted(pends, jnp.arange(nblk) * EXPERT_BLOCK, side='right'), 0, N_EXPERTS - 1)

    def expert_block(args):
        xb, e = args
        gu = xb @ w_gu[e] + b_gu[e]
        gate = jnp.minimum(gu[:, :D_FF], SWIGLU_LIMIT)
        up = jnp.clip(gu[:, D_FF:], -SWIGLU_LIMIT, SWIGLU_LIMIT)
        glu = gate * jax.nn.sigmoid(gate * SWIGLU_ALPHA)
        return ((up + 1.0) * glu) @ w_down[e] + b_down[e]

    ypad = lax.map(expert_block, (xpad.reshape(nblk, EXPERT_BLOCK, D), blk_e)).reshape(nblk * EXPERT_BLOCK, D)
    y_rows = ypad[dest] * w_sorted[:, None].astype(ypad.dtype)
    out = jnp.zeros((T, D), h.dtype).at[tok_sorted].add(y_rows)
    return out.reshape(B, S, D)


def trunk_layer(x, c, pos, past_lat, past_kr, gla_s0, lp):
    B, S, _ = x.shape
    mod = jax.nn.silu(c) @ lp['w_ada'] + lp['b_ada']
    sh_a, sc_a, gt_a, sh_f, sc_f, gt_f = [m[:, None, :] for m in jnp.split(mod, 6, axis=-1)]

    h = rms_norm(x, lp['g_norm_mix']) * (1.0 + sc_a) + sh_a
    proj = h @ lp['w_in']
    split_pts = [int(s) for s in np.cumsum(IN_SIZES)[:-1]]
    q_a, kv_a, kr_raw, gq, gk, gv, g_lr, g_r = jnp.split(proj, split_pts, axis=-1)

    q = (rms_norm(q_a, lp['g_q_a']) @ lp['w_uq']).reshape(B, S, MLA_HEADS, QK_DIM)
    q = jnp.concatenate([q[..., :NOPE_DIM], rope(q[..., NOPE_DIM:], pos)], axis=-1)
    q = rms_norm(q, lp['g_qk_q'])
    lat_new = rms_norm(kv_a, lp['g_kv_a'])
    kr_new = rope(kr_raw[:, :, None, :], pos)[:, :, 0]
    if past_lat is None:
        lat_all, kr_all, k_pos = lat_new, kr_new, pos
    else:
        lat_all = jnp.concatenate([past_lat, lat_new], axis=1)
        kr_all = jnp.concatenate([past_kr, kr_new], axis=1)
        k_pos = jnp.arange(lat_all.shape[1], dtype=jnp.int32)
    k, v = mla_expand(lat_all, kr_all, lp['w_ukv'], lp['g_qk_k'])
    o_mla = block_causal_attention(q, k, v, pos, k_pos).reshape(B, S, MLA_OUT)

    gq = gq.reshape(B, S, GLA_HEADS, GLA_DK) * (GLA_DK ** -0.5)
    gk = gk.reshape(B, S, GLA_HEADS, GLA_DK)
    gv = gv.reshape(B, S, GLA_HEADS, GLA_DV)
    glog = jax.nn.log_sigmoid((g_lr @ lp['w_g2'] + lp['b_g2']).astype(jnp.float32)) / GATE_NORM
    glog = glog.reshape(B, S, GLA_HEADS, GLA_DK)
    o_gla, s_new = gla_recurrence(gq, gk, gv, glog, gla_s0)
    o_gla = rms_norm(o_gla.astype(x.dtype), lp['g_gla_out']) * jax.nn.silu(g_r.reshape(B, S, GLA_HEADS, GLA_DV))

    mix = jnp.concatenate([o_mla, o_gla.reshape(B, S, GLA_OUT)], axis=-1) @ lp['w_o']
    x = x + gt_a * mix

    h = rms_norm(x, lp['g_norm_ffn']) * (1.0 + sc_f) + sh_f
    x = x + gt_f * moe(h, lp['w_router'], lp['b_router'], lp['w_gu'], lp['b_gu'], lp['w_down'], lp['b_down'])
    return x, lat_new, kr_new, s_new


def setup_inputs(seed: int = 0) -> dict:
    key = jax.random.key(seed)
    ks = jax.random.split(key, 32)
    f32 = jnp.float32

    def nrm(k, shape, scale):
        return jax.random.normal(k, shape, f32) * scale

    def gain(k, n):
        return 1.0 + 0.02 * jax.random.normal(k, (DEPTH, n), f32)

    return {
        'x_prompt': nrm(ks[0], (BATCH, SEQ, D_MODEL), 1.0),
        'x_sample': nrm(ks[1], (DEC_BATCH, DEC_SEQ, D_MODEL), 1.0),
        'cache_mla_latent': nrm(ks[2], (DEPTH, DEC_BATCH, PAST_LEN, KV_LORA), 1.0),
        'cache_mla_krope': nrm(ks[3], (DEPTH, DEC_BATCH, PAST_LEN, ROPE_DIM), 1.0),
        'state_gla': nrm(ks[4], (DEPTH, DEC_BATCH, GLA_HEADS, GLA_DK, GLA_DV), 0.5),
        'c_prompt': nrm(ks[5], (BATCH, D_MODEL), 1.0),
        'c_sample': nrm(ks[6], (DEC_BATCH, D_MODEL), 1.0),
        'w_ada': nrm(ks[7], (DEPTH, D_MODEL, 6 * D_MODEL), 0.5 * D_MODEL ** -0.5),
        'b_ada': nrm(ks[8], (DEPTH, 6 * D_MODEL), 0.02),
        'g_norm_mix': gain(ks[9], D_MODEL),
        'w_in': nrm(ks[10], (DEPTH, D_MODEL, IN_WIDTH), D_MODEL ** -0.5),
        'g_q_a': gain(ks[11], Q_LORA),
        'w_uq': nrm(ks[12], (DEPTH, Q_LORA, MLA_HEADS * QK_DIM), Q_LORA ** -0.5),
        'g_kv_a': gain(ks[13], KV_LORA),
        'w_ukv': nrm(ks[14], (DEPTH, KV_LORA, MLA_HEADS * (NOPE_DIM + V_DIM)), KV_LORA ** -0.5),
        'g_qk_q': gain(ks[15], QK_DIM),
        'g_qk_k': gain(ks[16], QK_DIM),
        'w_g2': nrm(ks[17], (DEPTH, GATE_RANK, GLA_HEADS * GLA_DK), GATE_RANK ** -0.5),
        'b_g2': nrm(ks[18], (DEPTH, GLA_HEADS * GLA_DK), 0.1),
        'g_gla_out': gain(ks[19], GLA_DV),
        'w_o': nrm(ks[20], (DEPTH, MIX_WIDTH, D_MODEL), MIX_WIDTH ** -0.5),
        'g_norm_ffn': gain(ks[21], D_MODEL),
        'w_router': nrm(ks[22], (DEPTH, D_MODEL, N_EXPERTS), D_MODEL ** -0.5),
        'b_router': nrm(ks[23], (DEPTH, N_EXPERTS), 0.01),
        'w_gu': nrm(ks[24], (DEPTH, N_EXPERTS, D_MODEL, 2 * D_FF), D_MODEL ** -0.5),
        'b_gu': nrm(ks[25], (DEPTH, N_EXPERTS, 2 * D_FF), 0.01),
        'w_down': nrm(ks[26], (DEPTH, N_EXPERTS, D_FF, D_MODEL), D_FF ** -0.5),
        'b_down': nrm(ks[27], (DEPTH, N_EXPERTS, D_MODEL), 0.01),
    }


def reference(x_prompt, x_sample, cache_mla_latent, cache_mla_krope, state_gla, c_prompt, c_sample,
              w_ada, b_ada, g_norm_mix, w_in, g_q_a, w_uq, g_kv_a, w_ukv, g_qk_q, g_qk_k,
              w_g2, b_g2, g_gla_out, w_o, g_norm_ffn, w_router, b_router, w_gu, b_gu, w_down, b_down):
    n_prompt = x_prompt.shape[1]
    n_sample = x_sample.shape[1]
    past = cache_mla_latent.shape[2]
    pos_p = jnp.arange(n_prompt, dtype=jnp.int32)
    pos_s = past + jnp.arange(n_sample, dtype=jnp.int32)
    s0_p = jnp.zeros((x_prompt.shape[0], GLA_HEADS, GLA_DK, GLA_DV), jnp.float32)

    xp, xs = x_prompt, x_sample
    lat_p, kr_p, st_p, lat_s, kr_s, st_s = [], [], [], [], [], []
    for l in range(DEPTH):
        lp = {
            'w_ada': w_ada[l], 'b_ada': b_ada[l], 'g_norm_mix': g_norm_mix[l], 'w_in': w_in[l],
            'g_q_a': g_q_a[l], 'w_uq': w_uq[l], 'g_kv_a': g_kv_a[l], 'w_ukv': w_ukv[l],
            'g_qk_q': g_qk_q[l], 'g_qk_k': g_qk_k[l], 'w_g2': w_g2[l], 'b_g2': b_g2[l],
            'g_gla_out': g_gla_out[l], 'w_o': w_o[l], 'g_norm_ffn': g_norm_ffn[l],
            'w_router': w_router[l], 'b_router': b_router[l], 'w_gu': w_gu[l], 'b_gu': b_gu[l],
            'w_down': w_down[l], 'b_down': b_down[l],
        }
        xp, a, b, c = trunk_layer(xp, c_prompt, pos_p, None, None, s0_p, lp)
        lat_p.append(a); kr_p.append(b); st_p.append(c)
        xs, a, b, c = trunk_layer(xs, c_sample, pos_s, cache_mla_latent[l], cache_mla_krope[l], state_gla[l], lp)
        lat_s.append(a); kr_s.append(b); st_s.append(c)

    return (xp, xs, jnp.stack(lat_p), jnp.stack(kr_p), jnp.stack(st_p),
            jnp.stack(lat_s), jnp.stack(kr_s), jnp.stack(st_s))
```

```python
import functools

import numpy as np
import jax
import jax.numpy as jnp
from jax import lax
from jax.experimental import pallas as pl
from jax.experimental.pallas import tpu as pltpu

F32 = jnp.float32
BF16 = jnp.bfloat16
I32 = jnp.int32

CHUNK = 64
EPS = 1e-6
HEADS = 4
Q_LORA = 384
KV_LORA = 256
NOPE = 128
ROPE = 64
HALF = ROPE // 2
V_DIM = 128
QK = NOPE + ROPE
QK_PAD = 256
ROPE_THETA = 10000.0
GLA_DK = 64
GLA_DV = 128
GATE_RANK = 16
GATE_NORM = 16.0
N_EXPERTS = 32
TOP_K = 4
SWIGLU_LIMIT = 7.0
SWIGLU_ALPHA = 1.702
NEG = -1e30

LANES = 128
SUBLANES = 8
ROW_SLABS = 8
EXPERT_ROWS = 256
VMEM_BIG = 56 * 1024 * 1024


def _cparams(sem, vmem=None):
    return pltpu.CompilerParams(dimension_semantics=sem, vmem_limit_bytes=vmem)


def _nt(a, b):
    return lax.dot_general(a, b, (((1,), (1,)), ((), ())), preferred_element_type=F32)


def _rms(x, width):
    return lax.rsqrt(jnp.sum(x * x, axis=-1, keepdims=True) * (1.0 / width) + EPS)


def _ada_body(c_ref, w_ref, b_ref, o_ref):
    c = c_ref[...]
    s = (c * jax.nn.sigmoid(c)).astype(BF16)
    o_ref[...] = jnp.dot(s, w_ref[...].astype(BF16), preferred_element_type=F32) + b_ref[...]


def _ada(c, w_ada, b_ada):
    r, d = c.shape
    n = w_ada.shape[1]
    tn = 1536 if n % 1536 == 0 else n
    return pl.pallas_call(
        _ada_body,
        grid=(n // tn,),
        in_specs=[pl.BlockSpec((r, d), lambda j: (0, 0)),
                  pl.BlockSpec((d, tn), lambda j: (0, j)),
                  pl.BlockSpec((1, tn), lambda j: (0, j))],
        out_specs=pl.BlockSpec((r, tn), lambda j: (0, j)),
        out_shape=jax.ShapeDtypeStruct((r, n), F32),
        compiler_params=_cparams(("arbitrary",), 40 * 1024 * 1024),
        name="ada",
    )(c, w_ada, b_ada.reshape(1, n))


_SEG = dict(qa=(0, 384), kva=(384, 640), kr=(640, 768), gq=(768, 1024), gk=(1024, 1280),
            gv=(1280, 1792), gr=(1792, 2304), glr=(2304, 2432))
_W1_COLS = 2432


def _proj_body(pos0, ts, x_ref, sh_ref, sc_ref, gmix_ref, w1_ref, gqa_ref, wuq_ref, gkv_ref, gqk_ref,
               rope_ref, wg2_ref, bg2_ref,
               q_ref, lat_ref, kr_ref, gq_o, gk_o, gv_o, gl_o, gr_o):
    i = pl.program_id(1)
    x = x_ref[...]
    d = x.shape[-1]
    h = (x * _rms(x, d) * gmix_ref[...]) * (1.0 + sc_ref[...]) + sh_ref[...]
    hb = h.astype(BF16)

    def seg(name):
        a, b = _SEG[name]
        return jnp.dot(hb, w1_ref[:, a:b], preferred_element_type=F32)

    pos = (pos0 + i * ts + lax.broadcasted_iota(I32, (ts, LANES), 0)).astype(F32)
    ang = pos * rope_ref[0:1, :]
    cos = jnp.cos(ang)
    sin = jnp.sin(ang) * rope_ref[1:2, :]
    lane = lax.broadcasted_iota(I32, (ts, LANES), 1)
    first_half = (lane & HALF) == 0
    low64 = lane < ROPE

    def rope(v):
        partner = jnp.where(first_half, pltpu.roll(v, LANES - HALF, 1), pltpu.roll(v, HALF, 1))
        return v * cos + partner * sin

    qa = seg("qa")
    qn = (qa * _rms(qa, Q_LORA) * gqa_ref[...]).astype(BF16)
    qf = jnp.dot(qn, wuq_ref[...], preferred_element_type=F32)
    rope_blocks = (rope(qf[:, 4 * NOPE:4 * NOPE + LANES]), rope(qf[:, 4 * NOPE + LANES:4 * NOPE + 2 * LANES]))
    for hd in range(HEADS):
        nope = qf[:, NOPE * hd:NOPE * (hd + 1)]
        blk = rope_blocks[hd // 2]
        if hd % 2:
            blk = pltpu.roll(blk, ROPE, 1)
        blk = jnp.where(low64, blk, 0.0)
        ss = jnp.sum(nope * nope, axis=-1, keepdims=True) + jnp.sum(blk * blk, axis=-1, keepdims=True)
        scl = lax.rsqrt(ss * (1.0 / QK) + EPS) * (QK ** -0.5)
        q_ref[hd, :, 0:NOPE] = (nope * scl * gqk_ref[0:1, :]).astype(BF16)
        q_ref[hd, :, NOPE:QK_PAD] = (blk * scl * gqk_ref[1:2, :]).astype(BF16)

    kva = seg("kva")
    lat_ref[...] = kva * _rms(kva, KV_LORA) * gkv_ref[...]
    kr_ref[...] = rope(seg("kr"))[:, 0:ROPE]

    gq_o[...] = seg("gq") * (GLA_DK ** -0.5)
    gk_o[...] = seg("gk")
    gv_o[...] = seg("gv").astype(BF16)
    gr_o[...] = seg("gr")
    z = jnp.dot(seg("glr").astype(BF16), wg2_ref[...], preferred_element_type=F32) + bg2_ref[...]
    gl_o[...] = (jnp.minimum(z, 0.0) - jnp.log1p(jnp.exp(-jnp.abs(z)))) * (1.0 / GATE_NORM)


def _proj(x, shift, scale, pos0, wts):
    b, s, d = x.shape
    ts = min(s, 512)
    row = lambda a: pl.BlockSpec(a.shape, lambda bi, i: (0,) * a.ndim)
    tok = lambda w: pl.BlockSpec((None, ts, w), lambda bi, i: (bi, i, 0))
    mod = pl.BlockSpec((None, 1, d), lambda bi, i: (bi, 0, 0))
    small = [wts["g_mix"], wts["w1"], wts["g_qa"], wts["w_uq"], wts["g_kv"], wts["g_qk_q"], wts["rope"],
             wts["w_g2"], wts["b_g2"]]
    out_shape = (
        jax.ShapeDtypeStruct((b, HEADS, s, QK_PAD), BF16),
        jax.ShapeDtypeStruct((b, s, KV_LORA), F32),
        jax.ShapeDtypeStruct((b, s, ROPE), F32),
        jax.ShapeDtypeStruct((b, s, HEADS * GLA_DK), F32),
        jax.ShapeDtypeStruct((b, s, HEADS * GLA_DK), F32),
        jax.ShapeDtypeStruct((b, s, HEADS * GLA_DV), BF16),
        jax.ShapeDtypeStruct((b, s, HEADS * GLA_DK), F32),
        jax.ShapeDtypeStruct((b, s, HEADS * GLA_DV), F32),
    )
    out_specs = (
        pl.BlockSpec((None, HEADS, ts, QK_PAD), lambda bi, i: (bi, 0, i, 0)),
        tok(KV_LORA), tok(ROPE), tok(HEADS * GLA_DK), tok(HEADS * GLA_DK), tok(HEADS * GLA_DV),
        tok(HEADS * GLA_DK), tok(HEADS * GLA_DV),
    )
    return pl.pallas_call(
        functools.partial(_proj_body, pos0, ts),
        grid=(b, s // ts),
        in_specs=[tok(d), mod, mod] + [row(a) for a in small],
        out_specs=out_specs,
        out_shape=out_shape,
        compiler_params=_cparams(("arbitrary", "arbitrary"), VMEM_BIG),
        name="proj",
    )(x, shift, scale, *small)


def _kv_body(lat_ref, kr_ref, w_ref, gk_ref, k_ref, v_ref):
    kv = jnp.dot(lat_ref[...].astype(BF16), w_ref[...], preferred_element_type=F32)
    kr = kr_ref[...]
    kr_ss = jnp.sum(kr * kr, axis=-1, keepdims=True)
    for hd in range(HEADS):
        kn = kv[:, NOPE * hd:NOPE * (hd + 1)]
        scl = lax.rsqrt((jnp.sum(kn * kn, axis=-1, keepdims=True) + kr_ss) * (1.0 / QK) + EPS)
        k_ref[hd, :, 0:NOPE] = (kn * scl * gk_ref[0:1, :]).astype(BF16)
        k_ref[hd, :, NOPE:QK] = (kr * scl * gk_ref[1:2, 0:ROPE]).astype(BF16)
        k_ref[hd, :, QK:QK_PAD] = jnp.zeros((kr.shape[0], QK_PAD - QK), BF16)
        v_ref[hd] = kv[:, HEADS * NOPE + V_DIM * hd:HEADS * NOPE + V_DIM * (hd + 1)].astype(BF16)


def _kv(lat, kr, w_ukv, g_qk_k):
    b, s, _ = lat.shape
    ts = min(s, 512)
    return pl.pallas_call(
        _kv_body,
        grid=(b, s // ts),
        in_specs=[pl.BlockSpec((None, ts, KV_LORA), lambda bi, i: (bi, i, 0)),
                  pl.BlockSpec((None, ts, ROPE), lambda bi, i: (bi, i, 0)),
                  pl.BlockSpec(w_ukv.shape, lambda bi, i: (0, 0)),
                  pl.BlockSpec(g_qk_k.shape, lambda bi, i: (0, 0))],
        out_specs=(pl.BlockSpec((None, HEADS, ts, QK_PAD), lambda bi, i: (bi, 0, i, 0)),
                   pl.BlockSpec((None, HEADS, ts, V_DIM), lambda bi, i: (bi, 0, i, 0))),
        out_shape=(jax.ShapeDtypeStruct((b, HEADS, s, QK_PAD), BF16),
                   jax.ShapeDtypeStruct((b, HEADS, s, V_DIM), BF16)),
        compiler_params=_cparams(("arbitrary", "arbitrary")),
        name="kv",
    )(lat, kr, w_ukv, g_qk_k)


def _softmax_step(carry, s, vt):
    m, l, acc = carry
    m_new = jnp.maximum(m, jnp.max(s, axis=-1, keepdims=True))
    alpha = jnp.exp(m - m_new)
    p = jnp.exp(s - m_new)
    l = alpha * l + jnp.sum(p, axis=-1, keepdims=True)
    acc = alpha * acc + jnp.dot(p.astype(BF16), vt, preferred_element_type=F32)
    return m_new, l, acc


def _chunk_mask(tq, tk, q0, k0):
    qc = (q0 + lax.broadcasted_iota(I32, (tq, tk), 0)) // CHUNK
    kc = (k0 + lax.broadcasted_iota(I32, (tq, tk), 1)) // CHUNK
    return kc <= qc


def _attn_prompt_body(t, q_ref, k_ref, v_ref, o_ref):
    i = pl.program_id(2)
    q = q_ref[...]

    def tile(j):
        start = pl.multiple_of(j * t, t)
        return k_ref[pl.ds(start, t), :], v_ref[pl.ds(start, t), :]

    def body(j, carry):
        kt, vt = tile(j)
        return _softmax_step(carry, _nt(q, kt), vt)

    init = (jnp.full((t, 1), NEG, F32), jnp.zeros((t, 1), F32), jnp.zeros((t, V_DIM), F32))
    carry = lax.fori_loop(0, i, body, init)
    kt, vt = tile(i)
    s = jnp.where(_chunk_mask(t, t, 0, 0), _nt(q, kt), NEG)
    _, l, acc = _softmax_step(carry, s, vt)
    o_ref[...] = (acc / l).astype(BF16)


def _attn_prompt(q, k, v):
    b, _, s, _ = q.shape
    t = min(s, 512)
    return pl.pallas_call(
        functools.partial(_attn_prompt_body, t),
        grid=(b, HEADS, s // t),
        in_specs=[pl.BlockSpec((None, None, t, QK_PAD), lambda bi, h, i: (bi, h, i, 0)),
                  pl.BlockSpec((None, None, s, QK_PAD), lambda bi, h, i: (bi, h, 0, 0)),
                  pl.BlockSpec((None, None, s, V_DIM), lambda bi, h, i: (bi, h, 0, 0))],
        out_specs=pl.BlockSpec((None, t, V_DIM), lambda bi, h, i: (bi, i, h)),
        out_shape=jax.ShapeDtypeStruct((b, s, HEADS * V_DIM), BF16),
        compiler_params=_cparams(("arbitrary", "arbitrary", "arbitrary"), VMEM_BIG),
        name="attn_prompt",
    )(q, k, v)


def _attn_sample_body(past, sq, q_ref, kp_ref, vp_ref, kn_ref, vn_ref, o_ref):
    q = q_ref[...]
    init = (jnp.full((sq, 1), NEG, F32), jnp.zeros((sq, 1), F32), jnp.zeros((sq, V_DIM), F32))
    carry = _softmax_step(init, _nt(q, kp_ref[...]), vp_ref[...])
    s = jnp.where(_chunk_mask(sq, sq, past, past), _nt(q, kn_ref[...]), NEG)
    _, l, acc = _softmax_step(carry, s, vn_ref[...])
    o_ref[...] = (acc / l).astype(BF16)


def _attn_sample(q, kp, vp, kn, vn):
    b, _, sq, _ = q.shape
    past = kp.shape[2]
    head = lambda n, w: pl.BlockSpec((None, None, n, w), lambda bi, h: (bi, h, 0, 0))
    return pl.pallas_call(
        functools.partial(_attn_sample_body, past, sq),
        grid=(b, HEADS),
        in_specs=[head(sq, QK_PAD), head(past, QK_PAD), head(past, V_DIM), head(sq, QK_PAD), head(sq, V_DIM)],
        out_specs=pl.BlockSpec((None, sq, V_DIM), lambda bi, h: (bi, 0, h)),
        out_shape=jax.ShapeDtypeStruct((b, sq, HEADS * V_DIM), BF16),
        compiler_params=_cparams(("arbitrary", "arbitrary")),
        name="attn_sample",
    )(q, kp, vp, kn, vn)


def _gla_masks(c):
    levels = int(np.log2(c))
    idx = np.arange(c)
    le = idx[None, :] <= idx[:, None]
    gt = idx[None, :] > idx[:, None]
    blocks = [le, gt]
    for l in range(levels):
        n = c >> l
        bot = (idx % n) >= n // 2
        same = (idx // n)[:, None] == (idx // n)[None, :]
        blocks.append(same & bot[:, None] & bot[None, :] & le)
        blocks.append(same & ~bot[:, None] & ~bot[None, :] & gt)
    return np.concatenate(blocks, axis=0).astype(np.float32), levels


def _gla_body(c, n_chunks, levels, mall_ref, q_ref, k_ref, g_ref, v_ref, r_ref, s0_ref, gout_ref,
              o_ref, sfin_ref, st_scr):
    it = pl.program_id(1)

    @pl.when(it == 0)
    def _():
        st_scr[...] = s0_ref[...]

    lane = lax.broadcasted_iota(I32, (c, LANES), 1)
    head_lanes = (lane < GLA_DK, lane >= GLA_DK)
    st_lane_lo = lax.broadcasted_iota(I32, (GLA_DV, LANES), 1) < GLA_DK
    row = lax.broadcasted_iota(I32, (c, 1), 0)
    ri = lax.broadcasted_iota(I32, (c, c), 0)
    ci = lax.broadcasted_iota(I32, (c, c), 1)
    mall = mall_ref[...]

    for ch in range(n_chunks):
        rows = slice(ch * c, (ch + 1) * c)
        for p in range(HEADS // 2):
            ls = slice(LANES * p, LANES * (p + 1))
            g = g_ref[rows, ls]
            q = q_ref[rows, ls]
            k = k_ref[rows, ls]
            g_hi = g.astype(BF16)
            g_lo = (g - g_hi.astype(F32)).astype(BF16)
            e2 = jnp.dot(mall, jnp.concatenate([g_hi, g_lo], axis=1), preferred_element_type=F32)
            e = e2[:, 0:LANES] + e2[:, LANES:2 * LANES]
            eb = jnp.exp(e[0:c])
            qb = q * eb
            kd = (k * jnp.exp(e[c:2 * c])).astype(BF16)
            d_last = eb[c - 1:c, :]
            st = st_scr[p]
            st_b = st.astype(BF16)
            qs, ks = [q], [k.astype(BF16)]
            for l in range(levels):
                bottom = (row & (c >> (l + 1))) != 0
                qs.append(jnp.where(bottom, q * jnp.exp(e[(2 + 2 * l) * c:(3 + 2 * l) * c]), 0.0))
                ks.append(jnp.where(bottom, 0.0, k * jnp.exp(e[(3 + 2 * l) * c:(4 + 2 * l) * c])).astype(BF16))
            upd = []
            for hh in range(2):
                hd = 2 * p + hh
                sel = head_lanes[hh]
                a = jnp.where(ri == ci, _nt(jnp.where(sel, qs[0], 0.0).astype(BF16), ks[0]), 0.0)
                for l in range(levels):
                    pr = _nt(jnp.where(sel, qs[l + 1], 0.0).astype(BF16), ks[l + 1])
                    if l > 0:
                        pr = jnp.where((ri ^ ci) < (c >> l), pr, 0.0)
                    a = a + pr
                vh = v_ref[rows, GLA_DV * hd:GLA_DV * (hd + 1)]
                o = jnp.dot(a.astype(BF16), vh, preferred_element_type=F32)
                o = o + _nt(jnp.where(sel, qb, 0.0).astype(BF16), st_b)
                on = o * _rms(o, GLA_DV) * gout_ref[...]
                r = r_ref[rows, GLA_DV * hd:GLA_DV * (hd + 1)]
                o_ref[rows, GLA_DV * hd:GLA_DV * (hd + 1)] = (on * (r * jax.nn.sigmoid(r))).astype(BF16)
                upd.append(lax.dot_general(vh, kd, (((0,), (0,)), ((), ())), preferred_element_type=F32))
            st_scr[p] = st * d_last + jnp.where(st_lane_lo, upd[0], upd[1])

    @pl.when(it == pl.num_programs(1) - 1)
    def _():
        sfin_ref[...] = st_scr[...]


def _gla(gq, gk, gl, gv, gr, s0, g_out):
    b, s, _ = gq.shape
    c = min(CHUNK, s)
    tile = min(s, 4 * c)
    masks, levels = _gla_masks(c)
    mall = jnp.asarray(masks, BF16)
    tok = lambda w: pl.BlockSpec((None, tile, w), lambda bi, i: (bi, i, 0))
    st_spec = pl.BlockSpec((None, HEADS // 2, GLA_DV, LANES), lambda bi, i: (bi, 0, 0, 0))
    return pl.pallas_call(
        functools.partial(_gla_body, c, tile // c, levels),
        grid=(b, s // tile),
        in_specs=[pl.BlockSpec(mall.shape, lambda bi, i: (0, 0)),
                  tok(HEADS * GLA_DK), tok(HEADS * GLA_DK), tok(HEADS * GLA_DK), tok(HEADS * GLA_DV),
                  tok(HEADS * GLA_DV), st_spec, pl.BlockSpec(g_out.shape, lambda bi, i: (0, 0))],
        out_specs=(tok(HEADS * GLA_DV), st_spec),
        out_shape=(jax.ShapeDtypeStruct((b, s, HEADS * GLA_DV), BF16),
                   jax.ShapeDtypeStruct((b, HEADS // 2, GLA_DV, LANES), F32)),
        scratch_shapes=[pltpu.VMEM((HEADS // 2, GLA_DV, LANES), F32)],
        compiler_params=_cparams(("arbitrary", "arbitrary")),
        name="gla",
    )(mall, gq, gk, gl, gv, gr, s0, g_out)


def _state_to_pairs(s):
    b = s.shape[0]
    s = s.reshape(b, HEADS // 2, 2, GLA_DK, GLA_DV)
    return jnp.transpose(s, (0, 1, 4, 2, 3)).reshape(b, HEADS // 2, GLA_DV, 2 * GLA_DK)


def _state_from_pairs(s):
    b = s.shape[0]
    s = s.reshape(b, HEADS // 2, GLA_DV, 2, GLA_DK)
    return jnp.transpose(s, (0, 1, 3, 4, 2)).reshape(b, HEADS, GLA_DK, GLA_DV)


def _post_body(x_ref, om_ref, og_ref, gt_ref, sc_ref, sh_ref, wo_ref, gffn_ref, wr_ref, br_ref,
               x2_ref, h_ref, idx_ref, wt_ref):
    half = om_ref.shape[-1]
    mix = (jnp.dot(om_ref[...], wo_ref[0:half, :], preferred_element_type=F32)
           + jnp.dot(og_ref[...], wo_ref[half:2 * half, :], preferred_element_type=F32))
    x2 = x_ref[...] + gt_ref[...] * mix
    x2_ref[...] = x2
    d = x2.shape[-1]
    h = (x2 * _rms(x2, d) * gffn_ref[...]) * (1.0 + sc_ref[...]) + sh_ref[...]
    for s in range(ROW_SLABS):
        h_ref[:, s, :] = h[:, LANES * s:LANES * (s + 1)]
    h_hi = h.astype(BF16)
    h_lo = (h - h_hi.astype(F32)).astype(BF16)
    logits = _nt(wr_ref[0], h_hi) + _nt(wr_ref[0], h_lo) + _nt(wr_ref[1], h_hi) + br_ref[...]
    n_exp, tm = logits.shape
    eid = lax.broadcasted_iota(I32, (n_exp, tm), 0)
    vals, tops, ids = logits, [], []
    for _ in range(TOP_K):
        m = jnp.max(vals, axis=0, keepdims=True)
        sel = jnp.min(jnp.where(vals == m, eid, n_exp), axis=0, keepdims=True)
        tops.append(m)
        ids.append(sel)
        vals = jnp.where(eid == sel, -jnp.inf, vals)
    es = [jnp.exp(t - tops[0]) for t in tops]
    tot = es[0] + es[1] + es[2] + es[3]
    idx_ref[...] = jnp.concatenate(ids, axis=0)
    wt_ref[...] = jnp.concatenate([e / tot for e in es], axis=0)


def _post(x, om, og, gate, scale, shift, w_o, g_ffn, w_r2, b_r):
    t, d = x.shape
    tm = min(t, 512)
    per_tok = gate.shape[0] == t
    mod = pl.BlockSpec((tm, d), lambda i: (i, 0)) if per_tok else pl.BlockSpec((1, d), lambda i: (0, 0))
    tok = lambda w: pl.BlockSpec((tm, w), lambda i: (i, 0))
    full = lambda a: pl.BlockSpec(a.shape, lambda i: (0,) * a.ndim)
    return pl.pallas_call(
        _post_body,
        grid=(t // tm,),
        in_specs=[tok(d), tok(om.shape[1]), tok(og.shape[1]), mod, mod, mod, full(w_o), full(g_ffn), full(w_r2),
                  full(b_r)],
        out_specs=(tok(d), pl.BlockSpec((tm, ROW_SLABS, LANES), lambda i: (i, 0, 0)),
                   pl.BlockSpec((TOP_K, tm), lambda i: (0, i)), pl.BlockSpec((TOP_K, tm), lambda i: (0, i))),
        out_shape=(jax.ShapeDtypeStruct((t, d), F32), jax.ShapeDtypeStruct((t, ROW_SLABS, LANES), F32),
                   jax.ShapeDtypeStruct((TOP_K, t), I32), jax.ShapeDtypeStruct((TOP_K, t), F32)),
        compiler_params=_cparams(("arbitrary",), 40 * 1024 * 1024),
        name="post",
    )(x, om, og, gate, scale, shift, w_o, g_ffn, w_r2, b_r)


def _rank_body(idx_ref, pos_ref, cnt_ref, run_scr):
    i = pl.program_id(0)

    @pl.when(i == 0)
    def _():
        run_scr[...] = jnp.zeros_like(run_scr)

    idx = idx_ref[...]
    tg = idx.shape[1]
    eid = lax.broadcasted_iota(I32, (N_EXPERTS, tg), 0)
    hits = [eid == idx[k:k + 1, :] for k in range(TOP_K)]
    member = jnp.zeros((N_EXPERTS, tg), F32)
    for hk in hits:
        member = member + jnp.where(hk, 1.0, 0.0)
    before = (lax.broadcasted_iota(I32, (tg, tg), 0) < lax.broadcasted_iota(I32, (tg, tg), 1))
    prefix = jnp.dot(member.astype(BF16), jnp.where(before, 1.0, 0.0).astype(BF16),
                     preferred_element_type=F32) + run_scr[:, 0:1]
    pos_ref[...] = jnp.concatenate(
        [jnp.sum(jnp.where(hk, prefix, 0.0), axis=0, keepdims=True) for hk in hits], axis=0).astype(I32)
    run_scr[...] = run_scr[...] + jnp.sum(member, axis=1, keepdims=True)

    @pl.when(i == pl.num_programs(0) - 1)
    def _():
        cnt_ref[...] = run_scr[...].astype(I32)


def _rank(idx):
    t = idx.shape[1]
    tg = min(t, 512)
    return pl.pallas_call(
        _rank_body,
        grid=(t // tg,),
        in_specs=[pl.BlockSpec((TOP_K, tg), lambda i: (0, i))],
        out_specs=(pl.BlockSpec((TOP_K, tg), lambda i: (0, i)), pl.BlockSpec((N_EXPERTS, LANES), lambda i: (0, 0))),
        out_shape=(jax.ShapeDtypeStruct((TOP_K, t), I32), jax.ShapeDtypeStruct((N_EXPERTS, LANES), I32)),
        scratch_shapes=[pltpu.VMEM((N_EXPERTS, LANES), F32)],
        compiler_params=_cparams(("arbitrary",)),
        name="rank",
    )(idx)


def _row_copy(src, dst, sem):
    return pltpu.make_async_copy(src, dst, sem)


def _scatter_body(n_rows, dest_ref, h_ref, xin_ref, xout_ref, sem):
    del xin_ref

    def issue(r, carry):
        _row_copy(h_ref.at[r // TOP_K], xout_ref.at[dest_ref[0, r]], sem).start()
        return carry

    def drain(r, carry):
        _row_copy(h_ref.at[0], xout_ref.at[0], sem).wait()
        return carry

    lax.fori_loop(0, n_rows, issue, 0)
    lax.fori_loop(0, n_rows, drain, 0)


def _scatter(dest, h3, xpad):
    t = h3.shape[0]
    th = min(t, 256)
    n_rows = th * TOP_K
    dest3 = dest.reshape(t // th, 1, n_rows)
    return pl.pallas_call(
        functools.partial(_scatter_body, n_rows),
        grid=(t // th,),
        in_specs=[pl.BlockSpec((None, 1, n_rows), lambda i: (i, 0, 0), memory_space=pltpu.SMEM),
                  pl.BlockSpec((th, ROW_SLABS, LANES), lambda i: (i, 0, 0)),
                  pl.BlockSpec(memory_space=pl.ANY)],
        out_specs=pl.BlockSpec(memory_space=pl.ANY),
        out_shape=jax.ShapeDtypeStruct(xpad.shape, xpad.dtype),
        scratch_shapes=[pltpu.SemaphoreType.DMA(())],
        input_output_aliases={2: 0},
        compiler_params=_cparams(("arbitrary",)),
        name="scatter",
    )(dest3, h3, xpad)


def _experts_body(be_ref, nv_ref, x_ref, wgu_ref, bgu_ref, wd_ref, bd_ref, y_ref, wgu_s, wd_s):
    b = pl.program_id(0)
    e = be_ref[b]
    prev = be_ref[jnp.maximum(b - 1, 0)]
    valid = b < nv_ref[0]
    d_ff = wd_ref.shape[0]

    @pl.when(valid & ((b == 0) | (e != prev)))
    def _():
        wgu_s[...] = wgu_ref[...].astype(BF16)
        wd_s[...] = wd_ref[...].astype(BF16)

    @pl.when(valid)
    def _():
        x = jnp.concatenate([x_ref[:, s, :] for s in range(ROW_SLABS)], axis=-1).astype(BF16)
        gu = jnp.dot(x, wgu_s[...], preferred_element_type=F32) + bgu_ref[...]
        gate = jnp.minimum(gu[:, 0:d_ff], SWIGLU_LIMIT)
        up = jnp.clip(gu[:, d_ff:2 * d_ff], -SWIGLU_LIMIT, SWIGLU_LIMIT)
        act = ((up + 1.0) * (gate * jax.nn.sigmoid(gate * SWIGLU_ALPHA))).astype(BF16)
        y = jnp.dot(act, wd_s[...], preferred_element_type=F32) + bd_ref[...]
        for s in range(ROW_SLABS):
            y_ref[:, s, :] = y[:, LANES * s:LANES * (s + 1)]

    @pl.when(jnp.logical_not(valid))
    def _():
        y_ref[...] = jnp.zeros_like(y_ref)


def _experts(blk_e, n_valid, xpad, w_gu, b_gu, w_down, b_down):
    m = xpad.shape[0]
    nb = m // EXPERT_ROWS
    n_exp, d, f2 = w_gu.shape
    d_ff = w_down.shape[1]
    last = lambda b, be, nv: jnp.minimum(b, nv[0] - 1)
    grid_spec = pltpu.PrefetchScalarGridSpec(
        num_scalar_prefetch=2,
        grid=(nb,),
        in_specs=[pl.BlockSpec((EXPERT_ROWS, ROW_SLABS, LANES), lambda b, be, nv: (last(b, be, nv), 0, 0)),
                  pl.BlockSpec((None, d, f2), lambda b, be, nv: (be[last(b, be, nv)], 0, 0)),
                  pl.BlockSpec((None, 1, f2), lambda b, be, nv: (be[last(b, be, nv)], 0, 0)),
                  pl.BlockSpec((None, d_ff, d), lambda b, be, nv: (be[last(b, be, nv)], 0, 0)),
                  pl.BlockSpec((None, 1, d), lambda b, be, nv: (be[last(b, be, nv)], 0, 0))],
        out_specs=pl.BlockSpec((EXPERT_ROWS, ROW_SLABS, LANES), lambda b, be, nv: (b, 0, 0)),
        scratch_shapes=[pltpu.VMEM((d, f2), BF16), pltpu.VMEM((d_ff, d), BF16)],
    )
    return pl.pallas_call(
        _experts_body,
        grid_spec=grid_spec,
        out_shape=jax.ShapeDtypeStruct(xpad.shape, F32),
        compiler_params=_cparams(("arbitrary",), VMEM_BIG),
        name="experts",
    )(blk_e, n_valid, xpad, w_gu, b_gu.reshape(n_exp, 1, f2), w_down, b_down.reshape(n_exp, 1, d))


def _combine_body(tj, dest_ref, wt_ref, x2_ref, gt_ref, y_ref, o_ref, buf, sem):
    n_rows = tj * TOP_K

    def issue(r, carry):
        _row_copy(y_ref.at[dest_ref[0, r]], buf.at[r % TOP_K, r // TOP_K], sem).start()
        return carry

    def drain(r, carry):
        _row_copy(y_ref.at[0], buf.at[0, 0], sem).wait()
        return carry

    lax.fori_loop(0, n_rows, issue, 0)
    lax.fori_loop(0, n_rows, drain, 0)
    w = wt_ref[...]
    for s in range(ROW_SLABS):
        acc = w[:, 0:1] * buf[0, :, s, :]
        for k in range(1, TOP_K):
            acc = acc + w[:, k:k + 1] * buf[k, :, s, :]
        cols = slice(LANES * s, LANES * (s + 1))
        o_ref[:, cols] = x2_ref[:, cols] + gt_ref[:, cols] * acc


def _combine(dest, wts, x2, gate, ypad):
    t, d = x2.shape
    tj = min(t, 256)
    n_rows = tj * TOP_K
    dest3 = dest.reshape(t // tj, 1, n_rows)
    per_tok = gate.shape[0] == t
    mod = pl.BlockSpec((tj, d), lambda i: (i, 0)) if per_tok else pl.BlockSpec((1, d), lambda i: (0, 0))
    return pl.pallas_call(
        functools.partial(_combine_body, tj),
        grid=(t // tj,),
        in_specs=[pl.BlockSpec((None, 1, n_rows), lambda i: (i, 0, 0), memory_space=pltpu.SMEM),
                  pl.BlockSpec((tj, TOP_K), lambda i: (i, 0)),
                  pl.BlockSpec((tj, d), lambda i: (i, 0)),
                  mod,
                  pl.BlockSpec(memory_space=pl.ANY)],
        out_specs=pl.BlockSpec((tj, d), lambda i: (i, 0)),
        out_shape=jax.ShapeDtypeStruct((t, d), F32),
        scratch_shapes=[pltpu.VMEM((TOP_K, tj, ROW_SLABS, LANES), F32), pltpu.SemaphoreType.DMA(())],
        compiler_params=_cparams(("arbitrary",)),
        name="combine",
    )(dest3, wts, x2, gate, ypad)


def _prep_weights(g_norm_mix, w_in, g_q_a, w_uq, g_kv_a, w_ukv, g_qk_q, g_qk_k, w_g2, b_g2, g_gla_out, w_o,
                  g_norm_ffn, w_router, b_router):
    d = w_in.shape[0]
    o_qa, o_kva, o_kr = 0, Q_LORA, Q_LORA + KV_LORA
    o_gq = o_kr + ROPE
    o_gk = o_gq + HEADS * GLA_DK
    o_gv = o_gk + HEADS * GLA_DK
    o_glr = o_gv + HEADS * GLA_DV
    o_gr = o_glr + GATE_RANK
    kr_cols = w_in[:, o_kr:o_kr + ROPE]
    w1 = jnp.concatenate([
        w_in[:, o_qa:o_kva], w_in[:, o_kva:o_kr], kr_cols, kr_cols, w_in[:, o_gq:o_gk], w_in[:, o_gk:o_gv],
        w_in[:, o_gv:o_glr], w_in[:, o_gr:o_gr + HEADS * GLA_DV], w_in[:, o_glr:o_gr],
        jnp.zeros((d, LANES - GATE_RANK), w_in.dtype)], axis=1).astype(BF16)
    assert w1.shape[1] == _W1_COLS
    head = np.arange(HEADS)[:, None] * QK
    nope_cols = (head + np.arange(NOPE)[None, :]).reshape(-1)
    rope_cols = (head + NOPE + np.arange(ROPE)[None, :]).reshape(-1)
    wuq = w_uq[:, np.concatenate([nope_cols, rope_cols])].astype(BF16)
    kv_head = np.arange(HEADS)[:, None] * (NOPE + V_DIM)
    k_cols = (kv_head + np.arange(NOPE)[None, :]).reshape(-1)
    v_cols = (kv_head + NOPE + np.arange(V_DIM)[None, :]).reshape(-1)
    wukv = w_ukv[:, np.concatenate([k_cols, v_cols])].astype(BF16)
    pad_rope = lambda g: jnp.stack([g[0:NOPE], jnp.concatenate([g[NOPE:QK], jnp.zeros((QK_PAD - QK,), g.dtype)])])
    inv = ROPE_THETA ** (-jnp.arange(HALF, dtype=F32) / HALF)
    sign = jnp.concatenate([-jnp.ones((HALF,), F32), jnp.ones((HALF,), F32)])
    rope_tab = jnp.stack([jnp.tile(inv, LANES // HALF), jnp.tile(sign, LANES // ROPE)])
    wg2 = jnp.concatenate([w_g2, jnp.zeros((LANES - GATE_RANK, w_g2.shape[1]), w_g2.dtype)], axis=0).astype(BF16)
    wr_t = w_router.T
    wr_hi = wr_t.astype(BF16)
    wr_lo = (wr_t - wr_hi.astype(F32)).astype(BF16)
    return dict(
        g_mix=g_norm_mix.reshape(1, d), w1=w1, g_qa=g_q_a.reshape(1, -1), w_uq=wuq, g_kv=g_kv_a.reshape(1, -1),
        w_ukv=wukv, g_qk_q=pad_rope(g_qk_q), g_qk_k=pad_rope(g_qk_k), rope=rope_tab, w_g2=wg2,
        b_g2=b_g2.reshape(1, -1), g_out=g_gla_out.reshape(1, -1), w_o=w_o.astype(BF16),
        g_ffn=g_norm_ffn.reshape(1, d), w_r2=jnp.stack([wr_hi, wr_lo]), b_r=b_router.reshape(-1, 1))


def _mixer(x, mod, pos0, past_lat, past_kr, s0_pairs, wts):
    b, s, d = x.shape
    q, lat, kr, gq, gk, gv, gl, gr = _proj(x, mod[:, 0:1], mod[:, 1:2], pos0, wts)
    k_new, v_new = _kv(lat, kr, wts["w_ukv"], wts["g_qk_k"])
    if past_lat is None:
        o_mla = _attn_prompt(q, k_new, v_new)
    else:
        k_past, v_past = _kv(past_lat, past_kr, wts["w_ukv"], wts["g_qk_k"])
        o_mla = _attn_sample(q, k_past, v_past, k_new, v_new)
    o_gla, s_fin = _gla(gq, gk, gl, gv, gr, s0_pairs, wts["g_out"])
    t = b * s
    if b == 1:
        rows = lambda j: mod[0, j:j + 1]
    else:
        rows = lambda j: jnp.broadcast_to(mod[:, j:j + 1], (b, s, d)).reshape(t, d)
    x2, h3, idx, wt = _post(x.reshape(t, d), o_mla.reshape(t, -1), o_gla.reshape(t, -1), rows(2), rows(4), rows(3),
                            wts["w_o"], wts["g_ffn"], wts["w_r2"], wts["b_r"])
    return dict(x2=x2, h3=h3, idx=idx, wt=wt, gate_f=rows(5), lat=lat, kr=kr, s_fin=s_fin)


def kernel(x_prompt, x_sample, cache_mla_latent, cache_mla_krope, state_gla, c_prompt, c_sample, w_ada, b_ada, g_norm_mix, w_in, g_q_a, w_uq, g_kv_a, w_ukv, g_qk_q, g_qk_k, w_g2, b_g2, g_gla_out, w_o, g_norm_ffn, w_router, b_router, w_gu, b_gu, w_down, b_down):
    depth = w_ada.shape[0]
    assert depth == 1, "single-layer step"
    bp, sp, d = x_prompt.shape
    bs, ss, _ = x_sample.shape
    past = cache_mla_latent.shape[2]
    wts = _prep_weights(g_norm_mix[0], w_in[0], g_q_a[0], w_uq[0], g_kv_a[0], w_ukv[0], g_qk_q[0], g_qk_k[0],
                        w_g2[0], b_g2[0], g_gla_out[0], w_o[0], g_norm_ffn[0], w_router[0], b_router[0])

    mod = _ada(jnp.concatenate([c_prompt, c_sample], axis=0), w_ada[0], b_ada[0]).reshape(bp + bs, 6, d)
    zero_state = jnp.zeros((bp, HEADS // 2, GLA_DV, LANES), F32)
    pr = _mixer(x_prompt, mod[:bp], 0, None, None, zero_state, wts)
    sa = _mixer(x_sample, mod[bp:], past, cache_mla_latent[0], cache_mla_krope[0], _state_to_pairs(state_gla[0]),
                wts)

    idx = jnp.concatenate([pr["idx"], sa["idx"]], axis=1)
    pos, cnt = _rank(idx)
    counts = cnt[:, 0]
    padded = ((counts + EXPERT_ROWS - 1) // EXPERT_ROWS) * EXPERT_ROWS
    ends = jnp.cumsum(padded)
    starts = ends - padded
    dest = (starts[idx] + pos).T
    t_all = idx.shape[1]
    n_blocks = -(-(t_all * TOP_K) // EXPERT_ROWS) + N_EXPERTS
    blk_e = jnp.clip(jnp.searchsorted(ends, jnp.arange(n_blocks, dtype=I32) * EXPERT_ROWS, side="right"),
                     0, N_EXPERTS - 1).astype(I32)
    n_valid = (ends[-1:] // EXPERT_ROWS).astype(I32)

    tp = bp * sp
    xpad = jnp.zeros((n_blocks * EXPERT_ROWS, ROW_SLABS, LANES), F32)
    xpad = _scatter(dest[:tp], pr["h3"], xpad)
    xpad = _scatter(dest[tp:], sa["h3"], xpad)
    ypad = _experts(blk_e, n_valid, xpad, w_gu[0], b_gu[0], w_down[0], b_down[0])
    wt = jnp.concatenate([pr["wt"], sa["wt"]], axis=1).T
    y_p = _combine(dest[:tp], wt[:tp], pr["x2"], pr["gate_f"], ypad).reshape(bp, sp, d)
    y_s = _combine(dest[tp:], wt[tp:], sa["x2"], sa["gate_f"], ypad).reshape(bs, ss, d)

    return (y_p, y_s,
            pr["lat"][None], pr["kr"][None], _state_from_pairs(pr["s_fin"])[None],
            sa["lat"][None], sa["kr"][None], _state_from_pairs(sa["s_fin"])[None])
```

```python
import functools

import numpy as np
import jax
import jax.numpy as jnp
from jax import lax
from jax.experimental import pallas as pl
from jax.experimental.pallas import tpu as pltpu

F32 = jnp.float32
BF16 = jnp.bfloat16
I32 = jnp.int32

CHUNK = 64
EPS = 1e-6
HEADS = 4
Q_LORA = 384
KV_LORA = 256
NOPE = 128
ROPE = 64
HALF = ROPE // 2
V_DIM = 128
QK = NOPE + ROPE
QK_PAD = 256
ROPE_THETA = 10000.0
GLA_DK = 64
GLA_DV = 128
GATE_RANK = 16
GATE_NORM = 16.0
N_EXPERTS = 32
TOP_K = 4
SWIGLU_LIMIT = 7.0
SWIGLU_ALPHA = 1.702
NEG = -1e30

LANES = 128
BF16_ROWS = 16
ROUTE_TILE = 512
RUN_CHUNK = BF16_ROWS
SORT_ROWS = 512
EXPERT_ROWS = 256
VMEM_BIG = 56 * 1024 * 1024


def _cparams(sem, vmem=None):
    return pltpu.CompilerParams(dimension_semantics=sem, vmem_limit_bytes=vmem)


def _nt(a, b):
    return lax.dot_general(a, b, (((1,), (1,)), ((), ())), preferred_element_type=F32)


def _rms(x, width):
    return lax.rsqrt(jnp.sum(x * x, axis=-1, keepdims=True) * (1.0 / width) + EPS)


def _round_up(x, m):
    return ((x + m - 1) // m) * m


def _ada_body(c_ref, w_ref, b_ref, o_ref):
    c = c_ref[...]
    s = (c * jax.nn.sigmoid(c)).astype(BF16)
    o_ref[...] = jnp.dot(s, w_ref[...].astype(BF16), preferred_element_type=F32) + b_ref[...]


def _ada(c, w_ada, b_ada):
    r, d = c.shape
    n = w_ada.shape[1]
    tn = 1536 if n % 1536 == 0 else n
    return pl.pallas_call(
        _ada_body,
        grid=(n // tn,),
        in_specs=[pl.BlockSpec((r, d), lambda j: (0, 0)),
                  pl.BlockSpec((d, tn), lambda j: (0, j)),
                  pl.BlockSpec((1, tn), lambda j: (0, j))],
        out_specs=pl.BlockSpec((r, tn), lambda j: (0, j)),
        out_shape=jax.ShapeDtypeStruct((r, n), F32),
        compiler_params=_cparams(("arbitrary",), 40 * 1024 * 1024),
        name="ada",
    )(c, w_ada, b_ada.reshape(1, n))


_SEG = dict(qa=(0, 384), kva=(384, 640), kr=(640, 768), gq=(768, 1024), gk=(1024, 1280),
            gv=(1280, 1792), gr=(1792, 2304), glr=(2304, 2432))
_W1_COLS = 2432


def _proj_body(pos0, ts, x_ref, sh_ref, sc_ref, gmix_ref, w1_ref, gqa_ref, wuq_ref, gkv_ref, gqk_ref,
               rope_ref, wg2_ref, bg2_ref,
               q_ref, lat_ref, kr_ref, gq_o, gk_o, gv_o, gl_o, gr_o):
    i = pl.program_id(1)
    x = x_ref[...]
    d = x.shape[-1]
    h = (x * _rms(x, d) * gmix_ref[...]) * (1.0 + sc_ref[...]) + sh_ref[...]
    hb = h.astype(BF16)

    def seg(name):
        a, b = _SEG[name]
        return jnp.dot(hb, w1_ref[:, a:b], preferred_element_type=F32)

    pos = (pos0 + i * ts + lax.broadcasted_iota(I32, (ts, LANES), 0)).astype(F32)
    ang = pos * rope_ref[0:1, :]
    cos = jnp.cos(ang)
    sin = jnp.sin(ang) * rope_ref[1:2, :]
    lane = lax.broadcasted_iota(I32, (ts, LANES), 1)
    first_half = (lane & HALF) == 0
    low64 = lane < ROPE

    def rope(v):
        partner = jnp.where(first_half, pltpu.roll(v, LANES - HALF, 1), pltpu.roll(v, HALF, 1))
        return v * cos + partner * sin

    qa = seg("qa")
    qn = (qa * _rms(qa, Q_LORA) * gqa_ref[...]).astype(BF16)
    qf = jnp.dot(qn, wuq_ref[...], preferred_element_type=F32)
    rope_blocks = (rope(qf[:, 4 * NOPE:4 * NOPE + LANES]), rope(qf[:, 4 * NOPE + LANES:4 * NOPE + 2 * LANES]))
    for hd in range(HEADS):
        nope = qf[:, NOPE * hd:NOPE * (hd + 1)]
        blk = rope_blocks[hd // 2]
        if hd % 2:
            blk = pltpu.roll(blk, ROPE, 1)
        blk = jnp.where(low64, blk, 0.0)
        ss = jnp.sum(nope * nope, axis=-1, keepdims=True) + jnp.sum(blk * blk, axis=-1, keepdims=True)
        scl = lax.rsqrt(ss * (1.0 / QK) + EPS) * (QK ** -0.5)
        q_ref[hd, :, 0:NOPE] = (nope * scl * gqk_ref[0:1, :]).astype(BF16)
        q_ref[hd, :, NOPE:QK_PAD] = (blk * scl * gqk_ref[1:2, :]).astype(BF16)

    kva = seg("kva")
    lat_ref[...] = kva * _rms(kva, KV_LORA) * gkv_ref[...]
    kr_ref[...] = rope(seg("kr"))[:, 0:ROPE]

    gq_o[...] = seg("gq") * (GLA_DK ** -0.5)
    gk_o[...] = seg("gk")
    gv_o[...] = seg("gv").astype(BF16)
    gr_o[...] = seg("gr")
    z = jnp.dot(seg("glr").astype(BF16), wg2_ref[...], preferred_element_type=F32) + bg2_ref[...]
    gl_o[...] = (jnp.minimum(z, 0.0) - jnp.log1p(jnp.exp(-jnp.abs(z)))) * (1.0 / GATE_NORM)


def _proj(x, shift, scale, pos0, wts):
    b, s, d = x.shape
    ts = min(s, 512)
    row = lambda a: pl.BlockSpec(a.shape, lambda bi, i: (0,) * a.ndim)
    tok = lambda w: pl.BlockSpec((None, ts, w), lambda bi, i: (bi, i, 0))
    mod = pl.BlockSpec((None, 1, d), lambda bi, i: (bi, 0, 0))
    small = [wts["g_mix"], wts["w1"], wts["g_qa"], wts["w_uq"], wts["g_kv"], wts["g_qk_q"], wts["rope"],
             wts["w_g2"], wts["b_g2"]]
    out_shape = (
        jax.ShapeDtypeStruct((b, HEADS, s, QK_PAD), BF16),
        jax.ShapeDtypeStruct((b, s, KV_LORA), F32),
        jax.ShapeDtypeStruct((b, s, ROPE), F32),
        jax.ShapeDtypeStruct((b, s, HEADS * GLA_DK), F32),
        jax.ShapeDtypeStruct((b, s, HEADS * GLA_DK), F32),
        jax.ShapeDtypeStruct((b, s, HEADS * GLA_DV), BF16),
        jax.ShapeDtypeStruct((b, s, HEADS * GLA_DK), F32),
        jax.ShapeDtypeStruct((b, s, HEADS * GLA_DV), F32),
    )
    out_specs = (
        pl.BlockSpec((None, HEADS, ts, QK_PAD), lambda bi, i: (bi, 0, i, 0)),
        tok(KV_LORA), tok(ROPE), tok(HEADS * GLA_DK), tok(HEADS * GLA_DK), tok(HEADS * GLA_DV),
        tok(HEADS * GLA_DK), tok(HEADS * GLA_DV),
    )
    return pl.pallas_call(
        functools.partial(_proj_body, pos0, ts),
        grid=(b, s // ts),
        in_specs=[tok(d), mod, mod] + [row(a) for a in small],
        out_specs=out_specs,
        out_shape=out_shape,
        compiler_params=_cparams(("arbitrary", "arbitrary"), VMEM_BIG),
        name="proj",
    )(x, shift, scale, *small)


def _kv_body(lat_ref, kr_ref, w_ref, gk_ref, k_ref, v_ref):
    kv = jnp.dot(lat_ref[...].astype(BF16), w_ref[...], preferred_element_type=F32)
    kr = kr_ref[...]
    kr_ss = jnp.sum(kr * kr, axis=-1, keepdims=True)
    for hd in range(HEADS):
        kn = kv[:, NOPE * hd:NOPE * (hd + 1)]
        scl = lax.rsqrt((jnp.sum(kn * kn, axis=-1, keepdims=True) + kr_ss) * (1.0 / QK) + EPS)
        k_ref[hd, :, 0:NOPE] = (kn * scl * gk_ref[0:1, :]).astype(BF16)
        k_ref[hd, :, NOPE:QK] = (kr * scl * gk_ref[1:2, 0:ROPE]).astype(BF16)
        k_ref[hd, :, QK:QK_PAD] = jnp.zeros((kr.shape[0], QK_PAD - QK), BF16)
        v_ref[hd] = kv[:, HEADS * NOPE + V_DIM * hd:HEADS * NOPE + V_DIM * (hd + 1)].astype(BF16)


def _kv(lat, kr, w_ukv, g_qk_k):
    b, s, _ = lat.shape
    ts = min(s, 512)
    return pl.pallas_call(
        _kv_body,
        grid=(b, s // ts),
        in_specs=[pl.BlockSpec((None, ts, KV_LORA), lambda bi, i: (bi, i, 0)),
                  pl.BlockSpec((None, ts, ROPE), lambda bi, i: (bi, i, 0)),
                  pl.BlockSpec(w_ukv.shape, lambda bi, i: (0, 0)),
                  pl.BlockSpec(g_qk_k.shape, lambda bi, i: (0, 0))],
        out_specs=(pl.BlockSpec((None, HEADS, ts, QK_PAD), lambda bi, i: (bi, 0, i, 0)),
                   pl.BlockSpec((None, HEADS, ts, V_DIM), lambda bi, i: (bi, 0, i, 0))),
        out_shape=(jax.ShapeDtypeStruct((b, HEADS, s, QK_PAD), BF16),
                   jax.ShapeDtypeStruct((b, HEADS, s, V_DIM), BF16)),
        compiler_params=_cparams(("arbitrary", "arbitrary")),
        name="kv",
    )(lat, kr, w_ukv, g_qk_k)


def _softmax_step(carry, s, vt):
    m, l, acc = carry
    m_new = jnp.maximum(m, jnp.max(s, axis=-1, keepdims=True))
    alpha = jnp.exp(m - m_new)
    p = jnp.exp(s - m_new)
    l = alpha * l + jnp.sum(p, axis=-1, keepdims=True)
    acc = alpha * acc + jnp.dot(p.astype(BF16), vt, preferred_element_type=F32)
    return m_new, l, acc


def _chunk_mask(tq, tk, q0, k0):
    qc = (q0 + lax.broadcasted_iota(I32, (tq, tk), 0)) // CHUNK
    kc = (k0 + lax.broadcasted_iota(I32, (tq, tk), 1)) // CHUNK
    return kc <= qc


def _attn_prompt_body(t, q_ref, k_ref, v_ref, o_ref):
    i = pl.program_id(2)
    q = q_ref[...]

    def tile(j):
        start = pl.multiple_of(j * t, t)
        return k_ref[pl.ds(start, t), :], v_ref[pl.ds(start, t), :]

    def body(j, carry):
        kt, vt = tile(j)
        return _softmax_step(carry, _nt(q, kt), vt)

    init = (jnp.full((t, 1), NEG, F32), jnp.zeros((t, 1), F32), jnp.zeros((t, V_DIM), F32))
    carry = lax.fori_loop(0, i, body, init)
    kt, vt = tile(i)
    s = jnp.where(_chunk_mask(t, t, 0, 0), _nt(q, kt), NEG)
    _, l, acc = _softmax_step(carry, s, vt)
    o_ref[...] = (acc / l).astype(BF16)


def _attn_prompt(q, k, v):
    b, _, s, _ = q.shape
    t = min(s, 512)
    return pl.pallas_call(
        functools.partial(_attn_prompt_body, t),
        grid=(b, HEADS, s // t),
        in_specs=[pl.BlockSpec((None, None, t, QK_PAD), lambda bi, h, i: (bi, h, i, 0)),
                  pl.BlockSpec((None, None, s, QK_PAD), lambda bi, h, i: (bi, h, 0, 0)),
                  pl.BlockSpec((None, None, s, V_DIM), lambda bi, h, i: (bi, h, 0, 0))],
        out_specs=pl.BlockSpec((None, t, V_DIM), lambda bi, h, i: (bi, i, h)),
        out_shape=jax.ShapeDtypeStruct((b, s, HEADS * V_DIM), BF16),
        compiler_params=_cparams(("arbitrary", "arbitrary", "arbitrary"), VMEM_BIG),
        name="attn_prompt",
    )(q, k, v)


def _attn_sample_body(past, sq, q_ref, kp_ref, vp_ref, kn_ref, vn_ref, o_ref):
    q = q_ref[...]
    init = (jnp.full((sq, 1), NEG, F32), jnp.zeros((sq, 1), F32), jnp.zeros((sq, V_DIM), F32))
    carry = _softmax_step(init, _nt(q, kp_ref[...]), vp_ref[...])
    s = jnp.where(_chunk_mask(sq, sq, past, past), _nt(q, kn_ref[...]), NEG)
    _, l, acc = _softmax_step(carry, s, vn_ref[...])
    o_ref[...] = (acc / l).astype(BF16)


def _attn_sample(q, kp, vp, kn, vn):
    b, _, sq, _ = q.shape
    past = kp.shape[2]
    head = lambda n, w: pl.BlockSpec((None, None, n, w), lambda bi, h: (bi, h, 0, 0))
    return pl.pallas_call(
        functools.partial(_attn_sample_body, past, sq),
        grid=(b, HEADS),
        in_specs=[head(sq, QK_PAD), head(past, QK_PAD), head(past, V_DIM), head(sq, QK_PAD), head(sq, V_DIM)],
        out_specs=pl.BlockSpec((None, sq, V_DIM), lambda bi, h: (bi, 0, h)),
        out_shape=jax.ShapeDtypeStruct((b, sq, HEADS * V_DIM), BF16),
        compiler_params=_cparams(("arbitrary", "arbitrary")),
        name="attn_sample",
    )(q, kp, vp, kn, vn)


def _gla_masks(c):
    levels = int(np.log2(c))
    idx = np.arange(c)
    le = idx[None, :] <= idx[:, None]
    gt = idx[None, :] > idx[:, None]
    blocks = [le, gt]
    for l in range(levels):
        n = c >> l
        bot = (idx % n) >= n // 2
        same = (idx // n)[:, None] == (idx // n)[None, :]
        blocks.append(same & bot[:, None] & bot[None, :] & le)
        blocks.append(same & ~bot[:, None] & ~bot[None, :] & gt)
    return np.concatenate(blocks, axis=0).astype(np.float32), levels


def _gla_body(c, n_chunks, levels, mall_ref, q_ref, k_ref, g_ref, v_ref, r_ref, s0_ref, gout_ref,
              o_ref, sfin_ref, st_scr):
    it = pl.program_id(1)

    @pl.when(it == 0)
    def _():
        st_scr[...] = s0_ref[...]

    lane = lax.broadcasted_iota(I32, (c, LANES), 1)
    head_lanes = (lane < GLA_DK, lane >= GLA_DK)
    st_lane_lo = lax.broadcasted_iota(I32, (GLA_DV, LANES), 1) < GLA_DK
    row = lax.broadcasted_iota(I32, (c, 1), 0)
    ri = lax.broadcasted_iota(I32, (c, c), 0)
    ci = lax.broadcasted_iota(I32, (c, c), 1)
    mall = mall_ref[...]

    for ch in range(n_chunks):
        rows = slice(ch * c, (ch + 1) * c)
        for p in range(HEADS // 2):
            ls = slice(LANES * p, LANES * (p + 1))
            g = g_ref[rows, ls]
            q = q_ref[rows, ls]
            k = k_ref[rows, ls]
            g_hi = g.astype(BF16)
            g_lo = (g - g_hi.astype(F32)).astype(BF16)
            e2 = jnp.dot(mall, jnp.concatenate([g_hi, g_lo], axis=1), preferred_element_type=F32)
            e = e2[:, 0:LANES] + e2[:, LANES:2 * LANES]
            eb = jnp.exp(e[0:c])
            qb = q * eb
            kd = (k * jnp.exp(e[c:2 * c])).astype(BF16)
            d_last = eb[c - 1:c, :]
            st = st_scr[p]
            st_b = st.astype(BF16)
            qs, ks = [q], [k.astype(BF16)]
            for l in range(levels):
                bottom = (row & (c >> (l + 1))) != 0
                qs.append(jnp.where(bottom, q * jnp.exp(e[(2 + 2 * l) * c:(3 + 2 * l) * c]), 0.0))
                ks.append(jnp.where(bottom, 0.0, k * jnp.exp(e[(3 + 2 * l) * c:(4 + 2 * l) * c])).astype(BF16))
            upd = []
            for hh in range(2):
                hd = 2 * p + hh
                sel = head_lanes[hh]
                a = jnp.where(ri == ci, _nt(jnp.where(sel, qs[0], 0.0).astype(BF16), ks[0]), 0.0)
                for l in range(levels):
                    pr = _nt(jnp.where(sel, qs[l + 1], 0.0).astype(BF16), ks[l + 1])
                    if l > 0:
                        pr = jnp.where((ri ^ ci) < (c >> l), pr, 0.0)
                    a = a + pr
                vh = v_ref[rows, GLA_DV * hd:GLA_DV * (hd + 1)]
                o = jnp.dot(a.astype(BF16), vh, preferred_element_type=F32)
                o = o + _nt(jnp.where(sel, qb, 0.0).astype(BF16), st_b)
                on = o * _rms(o, GLA_DV) * gout_ref[...]
                r = r_ref[rows, GLA_DV * hd:GLA_DV * (hd + 1)]
                o_ref[rows, GLA_DV * hd:GLA_DV * (hd + 1)] = (on * (r * jax.nn.sigmoid(r))).astype(BF16)
                upd.append(lax.dot_general(vh, kd, (((0,), (0,)), ((), ())), preferred_element_type=F32))
            st_scr[p] = st * d_last + jnp.where(st_lane_lo, upd[0], upd[1])

    @pl.when(it == pl.num_programs(1) - 1)
    def _():
        sfin_ref[...] = st_scr[...]


def _gla(gq, gk, gl, gv, gr, s0, g_out):
    b, s, _ = gq.shape
    c = min(CHUNK, s)
    tile = min(s, 4 * c)
    masks, levels = _gla_masks(c)
    mall = jnp.asarray(masks, BF16)
    tok = lambda w: pl.BlockSpec((None, tile, w), lambda bi, i: (bi, i, 0))
    st_spec = pl.BlockSpec((None, HEADS // 2, GLA_DV, LANES), lambda bi, i: (bi, 0, 0, 0))
    return pl.pallas_call(
        functools.partial(_gla_body, c, tile // c, levels),
        grid=(b, s // tile),
        in_specs=[pl.BlockSpec(mall.shape, lambda bi, i: (0, 0)),
                  tok(HEADS * GLA_DK), tok(HEADS * GLA_DK), tok(HEADS * GLA_DK), tok(HEADS * GLA_DV),
                  tok(HEADS * GLA_DV), st_spec, pl.BlockSpec(g_out.shape, lambda bi, i: (0, 0))],
        out_specs=(tok(HEADS * GLA_DV), st_spec),
        out_shape=(jax.ShapeDtypeStruct((b, s, HEADS * GLA_DV), BF16),
                   jax.ShapeDtypeStruct((b, HEADS // 2, GLA_DV, LANES), F32)),
        scratch_shapes=[pltpu.VMEM((HEADS // 2, GLA_DV, LANES), F32)],
        compiler_params=_cparams(("arbitrary", "arbitrary")),
        name="gla",
    )(mall, gq, gk, gl, gv, gr, s0, g_out)


def _state_to_pairs(s):
    b = s.shape[0]
    s = s.reshape(b, HEADS // 2, 2, GLA_DK, GLA_DV)
    return jnp.transpose(s, (0, 1, 4, 2, 3)).reshape(b, HEADS // 2, GLA_DV, 2 * GLA_DK)


def _state_from_pairs(s):
    b = s.shape[0]
    s = s.reshape(b, HEADS // 2, GLA_DV, 2, GLA_DK)
    return jnp.transpose(s, (0, 1, 3, 4, 2)).reshape(b, HEADS, GLA_DK, GLA_DV)


def _post_body(x_ref, om_ref, og_ref, gt_ref, sc_ref, sh_ref, wo_ref, gffn_ref, wr_ref, br_ref,
               x2_ref, h_ref, idx_ref, wt_ref):
    half = om_ref.shape[-1]
    mix = (jnp.dot(om_ref[...], wo_ref[0:half, :], preferred_element_type=F32)
           + jnp.dot(og_ref[...], wo_ref[half:2 * half, :], preferred_element_type=F32))
    x2 = x_ref[...] + gt_ref[...] * mix
    x2_ref[...] = x2
    d = x2.shape[-1]
    h = (x2 * _rms(x2, d) * gffn_ref[...]) * (1.0 + sc_ref[...]) + sh_ref[...]
    h_hi = h.astype(BF16)
    h_ref[...] = h_hi
    h_lo = (h - h_hi.astype(F32)).astype(BF16)
    logits = _nt(wr_ref[0], h_hi) + _nt(wr_ref[0], h_lo) + _nt(wr_ref[1], h_hi) + br_ref[...]
    n_exp, tm = logits.shape
    eid = lax.broadcasted_iota(I32, (n_exp, tm), 0)
    vals, tops, ids = logits, [], []
    for _ in range(TOP_K):
        m = jnp.max(vals, axis=0, keepdims=True)
        sel = jnp.min(jnp.where(vals == m, eid, n_exp), axis=0, keepdims=True)
        tops.append(m)
        ids.append(sel)
        vals = jnp.where(eid == sel, -jnp.inf, vals)
    es = [jnp.exp(t - tops[0]) for t in tops]
    tot = es[0] + es[1] + es[2] + es[3]
    idx_ref[...] = jnp.concatenate(ids, axis=0)
    wt_ref[...] = jnp.concatenate([e / tot for e in es], axis=0)


def _post(x, om, og, gate, scale, shift, w_o, g_ffn, w_r2, b_r):
    t, d = x.shape
    tm = min(t, 512)
    per_tok = gate.shape[0] == t
    mod = pl.BlockSpec((tm, d), lambda i: (i, 0)) if per_tok else pl.BlockSpec((1, d), lambda i: (0, 0))
    tok = lambda w: pl.BlockSpec((tm, w), lambda i: (i, 0))
    full = lambda a: pl.BlockSpec(a.shape, lambda i: (0,) * a.ndim)
    return pl.pallas_call(
        _post_body,
        grid=(t // tm,),
        in_specs=[tok(d), tok(om.shape[1]), tok(og.shape[1]), mod, mod, mod, full(w_o), full(g_ffn), full(w_r2),
                  full(b_r)],
        out_specs=(tok(d), tok(d),
                   pl.BlockSpec((TOP_K, tm), lambda i: (0, i)), pl.BlockSpec((TOP_K, tm), lambda i: (0, i))),
        out_shape=(jax.ShapeDtypeStruct((t, d), F32), jax.ShapeDtypeStruct((t, d), BF16),
                   jax.ShapeDtypeStruct((TOP_K, t), I32), jax.ShapeDtypeStruct((TOP_K, t), F32)),
        compiler_params=_cparams(("arbitrary",), 40 * 1024 * 1024),
        name="post",
    )(x, om, og, gate, scale, shift, w_o, g_ffn, w_r2, b_r)


def _rank_body(idx_ref, rank_ref, cnt_ref):
    idx = idx_ref[...]
    tg = idx.shape[1]
    eid = lax.broadcasted_iota(I32, (N_EXPERTS, tg), 0)
    hits = [eid == idx[k:k + 1, :] for k in range(TOP_K)]
    member = jnp.zeros((N_EXPERTS, tg), F32)
    for hk in hits:
        member = member + jnp.where(hk, 1.0, 0.0)
    before = (lax.broadcasted_iota(I32, (tg, tg), 0) < lax.broadcasted_iota(I32, (tg, tg), 1))
    prefix = jnp.dot(member.astype(BF16), jnp.where(before, 1.0, 0.0).astype(BF16), preferred_element_type=F32)
    rank_ref[...] = jnp.concatenate(
        [jnp.sum(jnp.where(hk, prefix, 0.0), axis=0, keepdims=True) for hk in hits], axis=0).astype(I32)
    cnt_ref[...] = jnp.broadcast_to(jnp.sum(member, axis=1, keepdims=True), (N_EXPERTS, LANES)).astype(I32)


def _rank(idx):
    t = idx.shape[1]
    nt = t // ROUTE_TILE
    return pl.pallas_call(
        _rank_body,
        grid=(nt,),
        in_specs=[pl.BlockSpec((TOP_K, ROUTE_TILE), lambda i: (0, i))],
        out_specs=(pl.BlockSpec((TOP_K, ROUTE_TILE), lambda i: (0, i)),
                   pl.BlockSpec((None, N_EXPERTS, LANES), lambda i: (i, 0, 0))),
        out_shape=(jax.ShapeDtypeStruct((TOP_K, t), I32), jax.ShapeDtypeStruct((nt, N_EXPERTS, LANES), I32)),
        compiler_params=_cparams(("arbitrary",)),
        name="rank",
    )(idx)


def _route_tables(idx, lrank, cnt3):
    nt = cnt3.shape[0]
    t = idx.shape[1]
    cnt = cnt3[:, :, 0]
    run = _round_up(cnt, RUN_CHUNK)
    lo_end = jnp.cumsum(run, axis=1)
    lo = lo_end - run
    n_chunks = lo_end[:, -1] // RUN_CHUNK
    region = _round_up(jnp.sum(run, axis=0), EXPERT_ROWS)
    g_end = jnp.cumsum(region)
    run_dest = (g_end - region)[None, :] + jnp.cumsum(run, axis=0) - run
    max_rows = TOP_K * ROUTE_TILE + N_EXPERTS * (RUN_CHUNK - 1)
    n_tab = _round_up(max_rows, RUN_CHUNK) // RUN_CHUNK
    c_start = jnp.arange(n_tab, dtype=I32) * RUN_CHUNK
    e_of_c = jnp.minimum(jnp.sum(lo_end[:, None, :] <= c_start[None, :, None], axis=2), N_EXPERTS - 1)
    pick = e_of_c[:, :, None] == jnp.arange(N_EXPERTS, dtype=I32)[None, None, :]
    chunk_dest = jnp.sum(jnp.where(pick, (run_dest - lo)[:, None, :], 0), axis=2) + c_start[None, :]
    table = jnp.concatenate([chunk_dest, n_chunks[:, None]], axis=1).astype(I32).reshape(nt, 1, n_tab + 1)
    eid = jnp.arange(N_EXPERTS, dtype=I32)[:, None]
    lo_tok = jnp.repeat(lo.T, ROUTE_TILE, axis=1)
    lpos = jnp.stack([jnp.sum(jnp.where(idx[k][None, :] == eid, lo_tok, 0), axis=0) for k in range(TOP_K)])
    lpos = (lpos + lrank).astype(I32)
    n_blocks = _round_up(t * TOP_K + nt * N_EXPERTS * (RUN_CHUNK - 1), EXPERT_ROWS) // EXPERT_ROWS + N_EXPERTS
    b_start = jnp.arange(n_blocks, dtype=I32) * EXPERT_ROWS
    blk_e = jnp.minimum(jnp.sum(g_end[None, :] <= b_start[:, None], axis=1), N_EXPERTS - 1).astype(I32)
    n_valid = (g_end[-1:] // EXPERT_ROWS).astype(I32)
    tail = jnp.concatenate([jnp.where(region > 0, g_end - EXPERT_ROWS, -1), n_valid]).astype(I32)
    tail = tail.reshape(1, N_EXPERTS + 1)
    return dict(table=table, lpos=lpos, blk_e=blk_e, n_valid=n_valid, tail=tail, n_blocks=n_blocks)


def _chunk_copy(src, dst, sem):
    return pltpu.make_async_copy(src, dst, sem)


def _scatter_body(n_tab, nt_a, n_blocks, tab_ref, tail_ref, lpos_ref, ha_ref, hb_ref, xout_ref, sorted_scr, zero_scr,
                  sem):
    i = pl.program_id(0)
    n_chunks = tab_ref[0, n_tab]
    tt = ha_ref.shape[0]

    @pl.when(i == 0)
    def _():
        zero_scr[...] = jnp.zeros_like(zero_scr)
        n_valid = tail_ref[0, N_EXPERTS]

        def block(start):
            return xout_ref.at[pl.ds(pl.multiple_of(start, EXPERT_ROWS), EXPERT_ROWS)]

        for e in range(N_EXPERTS):
            @pl.when(tail_ref[0, e] >= 0)
            def _():
                _chunk_copy(zero_scr, block(tail_ref[0, e]), sem).start()

        def fill(b, carry):
            _chunk_copy(zero_scr, block(b * EXPERT_ROWS), sem).start()
            return carry

        def fill_done(b, carry):
            _chunk_copy(zero_scr, block(0), sem).wait()
            return carry

        lax.fori_loop(n_valid, n_blocks, fill, 0)
        for e in range(N_EXPERTS):
            @pl.when(tail_ref[0, e] >= 0)
            def _():
                _chunk_copy(zero_scr, block(0), sem).wait()
        lax.fori_loop(n_valid, n_blocks, fill_done, 0)

    lp = lpos_ref[...]
    h = jnp.where(i < nt_a, ha_ref[...], hb_ref[...])

    def sort_block(rb, carry):
        r0 = pl.multiple_of(rb * SORT_ROWS, SORT_ROWS)
        rid = r0 + lax.broadcasted_iota(I32, (SORT_ROWS, tt), 0)
        onehot = jnp.zeros((SORT_ROWS, tt), F32)
        for k in range(TOP_K):
            onehot = onehot + jnp.where(lp[k:k + 1, :] == rid, 1.0, 0.0)
        sorted_scr[pl.ds(r0, SORT_ROWS), :] = jnp.dot(onehot.astype(BF16), h, preferred_element_type=F32).astype(BF16)
        return carry

    lax.fori_loop(0, (n_chunks * RUN_CHUNK + SORT_ROWS - 1) // SORT_ROWS, sort_block, 0)

    def issue(c, carry):
        src = sorted_scr.at[pl.ds(pl.multiple_of(c * RUN_CHUNK, RUN_CHUNK), RUN_CHUNK)]
        dst = xout_ref.at[pl.ds(pl.multiple_of(tab_ref[0, c], RUN_CHUNK), RUN_CHUNK)]
        _chunk_copy(src, dst, sem).start()
        return carry

    def drain(c, carry):
        _chunk_copy(sorted_scr.at[pl.ds(0, RUN_CHUNK)], xout_ref.at[pl.ds(0, RUN_CHUNK)], sem).wait()
        return carry

    lax.fori_loop(0, n_chunks, issue, 0)
    lax.fori_loop(0, n_chunks, drain, 0)


def _scatter(table, tail, lpos, h_a, h_b, n_blocks):
    d = h_a.shape[1]
    nt_a, nt_b = h_a.shape[0] // ROUTE_TILE, h_b.shape[0] // ROUTE_TILE
    n_tab = table.shape[2] - 1
    sort_cap = _round_up(n_tab * RUN_CHUNK, SORT_ROWS)
    return pl.pallas_call(
        functools.partial(_scatter_body, n_tab, nt_a, n_blocks),
        grid=(nt_a + nt_b,),
        in_specs=[pl.BlockSpec((None, 1, n_tab + 1), lambda i: (i, 0, 0), memory_space=pltpu.SMEM),
                  pl.BlockSpec((1, N_EXPERTS + 1), lambda i: (0, 0), memory_space=pltpu.SMEM),
                  pl.BlockSpec((TOP_K, ROUTE_TILE), lambda i: (0, i)),
                  pl.BlockSpec((ROUTE_TILE, d), lambda i: (jnp.minimum(i, nt_a - 1), 0)),
                  pl.BlockSpec((ROUTE_TILE, d), lambda i: (jnp.maximum(i - nt_a, 0), 0))],
        out_specs=pl.BlockSpec(memory_space=pl.ANY),
        out_shape=jax.ShapeDtypeStruct((n_blocks * EXPERT_ROWS, d), BF16),
        scratch_shapes=[pltpu.VMEM((sort_cap, d), BF16), pltpu.VMEM((EXPERT_ROWS, d), BF16),
                        pltpu.SemaphoreType.DMA(())],
        compiler_params=_cparams(("arbitrary",), 40 * 1024 * 1024),
        name="scatter",
    )(table, tail, lpos, h_a, h_b)


def _experts_body(be_ref, nv_ref, x_ref, wgu_ref, bgu_ref, wd_ref, bd_ref, y_ref, wgu_s, wd_s):
    b = pl.program_id(0)
    e = be_ref[b]
    prev = be_ref[jnp.maximum(b - 1, 0)]
    valid = b < nv_ref[0]
    d_ff = wd_ref.shape[0]

    @pl.when(valid & ((b == 0) | (e != prev)))
    def _():
        wgu_s[...] = wgu_ref[...].astype(BF16)
        wd_s[...] = wd_ref[...].astype(BF16)

    @pl.when(valid)
    def _():
        gu = jnp.dot(x_ref[...], wgu_s[...], preferred_element_type=F32) + bgu_ref[...]
        gate = jnp.minimum(gu[:, 0:d_ff], SWIGLU_LIMIT)
        up = jnp.clip(gu[:, d_ff:2 * d_ff], -SWIGLU_LIMIT, SWIGLU_LIMIT)
        act = ((up + 1.0) * (gate * jax.nn.sigmoid(gate * SWIGLU_ALPHA))).astype(BF16)
        y_ref[...] = (jnp.dot(act, wd_s[...], preferred_element_type=F32) + bd_ref[...]).astype(BF16)

    @pl.when(jnp.logical_not(valid))
    def _():
        y_ref[...] = jnp.zeros_like(y_ref)


def _experts(blk_e, n_valid, xpad, w_gu, b_gu, w_down, b_down):
    m, d = xpad.shape
    nb = m // EXPERT_ROWS
    n_exp, _, f2 = w_gu.shape
    d_ff = w_down.shape[1]
    last = lambda b, be, nv: jnp.minimum(b, nv[0] - 1)
    grid_spec = pltpu.PrefetchScalarGridSpec(
        num_scalar_prefetch=2,
        grid=(nb,),
        in_specs=[pl.BlockSpec((EXPERT_ROWS, d), lambda b, be, nv: (last(b, be, nv), 0)),
                  pl.BlockSpec((None, d, f2), lambda b, be, nv: (be[last(b, be, nv)], 0, 0)),
                  pl.BlockSpec((None, 1, f2), lambda b, be, nv: (be[last(b, be, nv)], 0, 0)),
                  pl.BlockSpec((None, d_ff, d), lambda b, be, nv: (be[last(b, be, nv)], 0, 0)),
                  pl.BlockSpec((None, 1, d), lambda b, be, nv: (be[last(b, be, nv)], 0, 0))],
        out_specs=pl.BlockSpec((EXPERT_ROWS, d), lambda b, be, nv: (b, 0)),
        scratch_shapes=[pltpu.VMEM((d, f2), BF16), pltpu.VMEM((d_ff, d), BF16)],
    )
    return pl.pallas_call(
        _experts_body,
        grid_spec=grid_spec,
        out_shape=jax.ShapeDtypeStruct((m, d), BF16),
        compiler_params=_cparams(("arbitrary",), VMEM_BIG),
        name="experts",
    )(blk_e, n_valid, xpad, w_gu, b_gu.reshape(n_exp, 1, f2), w_down, b_down.reshape(n_exp, 1, d))


def _combine_body(n_tab, tab_ref, lpos_ref, wt_ref, x2_ref, gt_ref, y_ref, o_ref, ysort_scr, sem):
    i = pl.program_id(0)
    n_chunks = tab_ref[0, n_tab]
    tt = x2_ref.shape[0]

    @pl.when(i == 0)
    def _():
        ysort_scr[...] = jnp.zeros_like(ysort_scr)

    def issue(c, carry):
        src = y_ref.at[pl.ds(pl.multiple_of(tab_ref[0, c], RUN_CHUNK), RUN_CHUNK)]
        dst = ysort_scr.at[pl.ds(pl.multiple_of(c * RUN_CHUNK, RUN_CHUNK), RUN_CHUNK)]
        _chunk_copy(src, dst, sem).start()
        return carry

    def drain(c, carry):
        _chunk_copy(y_ref.at[pl.ds(0, RUN_CHUNK)], ysort_scr.at[pl.ds(0, RUN_CHUNK)], sem).wait()
        return carry

    lax.fori_loop(0, n_chunks, issue, 0)
    lax.fori_loop(0, n_chunks, drain, 0)

    lp = lpos_ref[...]
    w = wt_ref[...]
    o_ref[...] = x2_ref[...]
    gt = gt_ref[...]

    def gather_block(rb, carry):
        r0 = pl.multiple_of(rb * SORT_ROWS, SORT_ROWS)
        rid = r0 + lax.broadcasted_iota(I32, (tt, SORT_ROWS), 1)
        pw = jnp.zeros((tt, SORT_ROWS), F32)
        for k in range(TOP_K):
            pw = pw + jnp.where(lp[:, k:k + 1] == rid, w[:, k:k + 1], 0.0)
        pw_hi = pw.astype(BF16)
        pw_lo = (pw - pw_hi.astype(F32)).astype(BF16)
        ys = ysort_scr[pl.ds(r0, SORT_ROWS), :]
        part = jnp.dot(pw_hi, ys, preferred_element_type=F32) + jnp.dot(pw_lo, ys, preferred_element_type=F32)
        o_ref[...] = o_ref[...] + gt * part
        return carry

    lax.fori_loop(0, (n_chunks * RUN_CHUNK + SORT_ROWS - 1) // SORT_ROWS, gather_block, 0)


def _combine(table, lpos_t, wts, x2, gate, ypad):
    t, d = x2.shape
    nt = t // ROUTE_TILE
    n_tab = table.shape[2] - 1
    per_tok = gate.shape[0] == t
    mod = pl.BlockSpec((ROUTE_TILE, d), lambda i: (i, 0)) if per_tok else pl.BlockSpec((1, d), lambda i: (0, 0))
    sort_cap = _round_up(n_tab * RUN_CHUNK, SORT_ROWS)
    return pl.pallas_call(
        functools.partial(_combine_body, n_tab),
        grid=(nt,),
        in_specs=[pl.BlockSpec((None, 1, n_tab + 1), lambda i: (i, 0, 0), memory_space=pltpu.SMEM),
                  pl.BlockSpec((ROUTE_TILE, TOP_K), lambda i: (i, 0)),
                  pl.BlockSpec((ROUTE_TILE, TOP_K), lambda i: (i, 0)),
                  pl.BlockSpec((ROUTE_TILE, d), lambda i: (i, 0)),
                  mod,
                  pl.BlockSpec(memory_space=pl.ANY)],
        out_specs=pl.BlockSpec((ROUTE_TILE, d), lambda i: (i, 0)),
        out_shape=jax.ShapeDtypeStruct((t, d), F32),
        scratch_shapes=[pltpu.VMEM((sort_cap, d), BF16), pltpu.SemaphoreType.DMA(())],
        compiler_params=_cparams(("arbitrary",), 40 * 1024 * 1024),
        name="combine",
    )(table, lpos_t, wts, x2, gate, ypad)


def _prep_weights(g_norm_mix, w_in, g_q_a, w_uq, g_kv_a, w_ukv, g_qk_q, g_qk_k, w_g2, b_g2, g_gla_out, w_o,
                  g_norm_ffn, w_router, b_router):
    d = w_in.shape[0]
    o_qa, o_kva, o_kr = 0, Q_LORA, Q_LORA + KV_LORA
    o_gq = o_kr + ROPE
    o_gk = o_gq + HEADS * GLA_DK
    o_gv = o_gk + HEADS * GLA_DK
    o_glr = o_gv + HEADS * GLA_DV
    o_gr = o_glr + GATE_RANK
    kr_cols = w_in[:, o_kr:o_kr + ROPE]
    w1 = jnp.concatenate([
        w_in[:, o_qa:o_kva], w_in[:, o_kva:o_kr], kr_cols, kr_cols, w_in[:, o_gq:o_gk], w_in[:, o_gk:o_gv],
        w_in[:, o_gv:o_glr], w_in[:, o_gr:o_gr + HEADS * GLA_DV], w_in[:, o_glr:o_gr],
        jnp.zeros((d, LANES - GATE_RANK), w_in.dtype)], axis=1).astype(BF16)
    assert w1.shape[1] == _W1_COLS
    wq = w_uq.reshape(Q_LORA, HEADS, QK)
    wuq = jnp.concatenate([wq[:, :, 0:NOPE].reshape(Q_LORA, HEADS * NOPE),
                           wq[:, :, NOPE:QK].reshape(Q_LORA, HEADS * ROPE)], axis=1).astype(BF16)
    wkv = w_ukv.reshape(KV_LORA, HEADS, NOPE + V_DIM)
    wukv = jnp.concatenate([wkv[:, :, 0:NOPE].reshape(KV_LORA, HEADS * NOPE),
                            wkv[:, :, NOPE:].reshape(KV_LORA, HEADS * V_DIM)], axis=1).astype(BF16)
    pad_rope = lambda g: jnp.stack([g[0:NOPE], jnp.concatenate([g[NOPE:QK], jnp.zeros((QK_PAD - QK,), g.dtype)])])
    inv = ROPE_THETA ** (-jnp.arange(HALF, dtype=F32) / HALF)
    sign = jnp.concatenate([-jnp.ones((HALF,), F32), jnp.ones((HALF,), F32)])
    rope_tab = jnp.stack([jnp.tile(inv, LANES // HALF), jnp.tile(sign, LANES // ROPE)])
    wg2 = jnp.concatenate([w_g2, jnp.zeros((LANES - GATE_RANK, w_g2.shape[1]), w_g2.dtype)], axis=0).astype(BF16)
    wr_t = w_router.T
    wr_hi = wr_t.astype(BF16)
    wr_lo = (wr_t - wr_hi.astype(F32)).astype(BF16)
    return dict(
        g_mix=g_norm_mix.reshape(1, d), w1=w1, g_qa=g_q_a.reshape(1, -1), w_uq=wuq, g_kv=g_kv_a.reshape(1, -1),
        w_ukv=wukv, g_qk_q=pad_rope(g_qk_q), g_qk_k=pad_rope(g_qk_k), rope=rope_tab, w_g2=wg2,
        b_g2=b_g2.reshape(1, -1), g_out=g_gla_out.reshape(1, -1), w_o=w_o.astype(BF16),
        g_ffn=g_norm_ffn.reshape(1, d), w_r2=jnp.stack([wr_hi, wr_lo]), b_r=b_router.reshape(-1, 1))


def _mixer(x, mod, pos0, past_lat, past_kr, s0_pairs, wts):
    b, s, d = x.shape
    q, lat, kr, gq, gk, gv, gl, gr = _proj(x, mod[:, 0:1], mod[:, 1:2], pos0, wts)
    k_new, v_new = _kv(lat, kr, wts["w_ukv"], wts["g_qk_k"])
    if past_lat is None:
        o_mla = _attn_prompt(q, k_new, v_new)
    else:
        k_past, v_past = _kv(past_lat, past_kr, wts["w_ukv"], wts["g_qk_k"])
        o_mla = _attn_sample(q, k_past, v_past, k_new, v_new)
    o_gla, s_fin = _gla(gq, gk, gl, gv, gr, s0_pairs, wts["g_out"])
    t = b * s
    if b == 1:
        rows = lambda j: mod[0, j:j + 1]
    else:
        rows = lambda j: jnp.broadcast_to(mod[:, j:j + 1], (b, s, d)).reshape(t, d)
    x2, h2, idx, wt = _post(x.reshape(t, d), o_mla.reshape(t, -1), o_gla.reshape(t, -1), rows(2), rows(4), rows(3),
                            wts["w_o"], wts["g_ffn"], wts["w_r2"], wts["b_r"])
    return dict(x2=x2, h2=h2, idx=idx, wt=wt, gate_f=rows(5), lat=lat, kr=kr, s_fin=s_fin)


def kernel(x_prompt, x_sample, cache_mla_latent, cache_mla_krope, state_gla, c_prompt, c_sample, w_ada, b_ada, g_norm_mix, w_in, g_q_a, w_uq, g_kv_a, w_ukv, g_qk_q, g_qk_k, w_g2, b_g2, g_gla_out, w_o, g_norm_ffn, w_router, b_router, w_gu, b_gu, w_down, b_down):
    depth = w_ada.shape[0]
    assert depth == 1, "single-layer step"
    bp, sp, d = x_prompt.shape
    bs, ss, _ = x_sample.shape
    tp, tsm = bp * sp, bs * ss
    assert tp % ROUTE_TILE == 0 and tsm % ROUTE_TILE == 0, "token counts must be whole routing tiles"
    past = cache_mla_latent.shape[2]
    wts = _prep_weights(g_norm_mix[0], w_in[0], g_q_a[0], w_uq[0], g_kv_a[0], w_ukv[0], g_qk_q[0], g_qk_k[0],
                        w_g2[0], b_g2[0], g_gla_out[0], w_o[0], g_norm_ffn[0], w_router[0], b_router[0])

    mod = _ada(jnp.concatenate([c_prompt, c_sample], axis=0), w_ada[0], b_ada[0]).reshape(bp + bs, 6, d)
    zero_state = jnp.zeros((bp, HEADS // 2, GLA_DV, LANES), F32)
    pr = _mixer(x_prompt, mod[:bp], 0, None, None, zero_state, wts)
    sa = _mixer(x_sample, mod[bp:], past, cache_mla_latent[0], cache_mla_krope[0], _state_to_pairs(state_gla[0]),
                wts)

    idx = jnp.concatenate([pr["idx"], sa["idx"]], axis=1)
    lrank, cnt3 = _rank(idx)
    rt = _route_tables(idx, lrank, cnt3)
    ntp = tp // ROUTE_TILE
    lpos, table = rt["lpos"], rt["table"]
    xpad = _scatter(table, rt["tail"], lpos, pr["h2"], sa["h2"], rt["n_blocks"])
    ypad = _experts(rt["blk_e"], rt["n_valid"], xpad, w_gu[0], b_gu[0], w_down[0], b_down[0])
    lpos_t = lpos.T
    y_p = _combine(table[:ntp], lpos_t[:tp], pr["wt"].T, pr["x2"], pr["gate_f"], ypad).reshape(bp, sp, d)
    y_s = _combine(table[ntp:], lpos_t[tp:], sa["wt"].T, sa["x2"], sa["gate_f"], ypad).reshape(bs, ss, d)

    return (y_p, y_s,
            pr["lat"][None], pr["kr"][None], _state_from_pairs(pr["s_fin"])[None],
            sa["lat"][None], sa["kr"][None], _state_from_pairs(sa["s_fin"])[None])
```

```python
import functools

import numpy as np
import jax
import jax.numpy as jnp
from jax import lax
from jax.experimental import pallas as pl
from jax.experimental.pallas import tpu as pltpu

F32 = jnp.float32
BF16 = jnp.bfloat16
I32 = jnp.int32

CHUNK = 64
EPS = 1e-6
HEADS = 4
Q_LORA = 384
KV_LORA = 256
NOPE = 128
ROPE = 64
HALF = ROPE // 2
V_DIM = 128
QK = NOPE + ROPE
QK_PAD = 256
ROPE_THETA = 10000.0
GLA_DK = 64
GLA_DV = 128
GATE_RANK = 16
GATE_NORM = 16.0
N_EXPERTS = 32
TOP_K = 4
SWIGLU_LIMIT = 7.0
SWIGLU_ALPHA = 1.702
NEG = -1e30
LOG2_E = 1.4426950408889634

LANES = 128
BF16_ROWS = 16
ROUTE_TILE = 512
RUN_CHUNK = BF16_ROWS
SORT_ROWS = 512
EXPERT_ROWS = 256
ATTN_TILE = 1024
VMEM_BIG = 56 * 1024 * 1024


def _cparams(sem, vmem=None):
    return pltpu.CompilerParams(dimension_semantics=sem, vmem_limit_bytes=vmem)


def _nt(a, b):
    return lax.dot_general(a, b, (((1,), (1,)), ((), ())), preferred_element_type=F32)


def _rms(x, width):
    return lax.rsqrt(jnp.sum(x * x, axis=-1, keepdims=True) * (1.0 / width) + EPS)


def _round_up(x, m):
    return ((x + m - 1) // m) * m


def _ada_body(c_ref, w_ref, b_ref, o_ref):
    c = c_ref[...]
    s = (c * jax.nn.sigmoid(c)).astype(BF16)
    o_ref[...] = jnp.dot(s, w_ref[...].astype(BF16), preferred_element_type=F32) + b_ref[...]


def _ada(c, w_ada, b_ada):
    r, d = c.shape
    n = w_ada.shape[1]
    tn = 1536 if n % 1536 == 0 else n
    return pl.pallas_call(
        _ada_body,
        grid=(n // tn,),
        in_specs=[pl.BlockSpec((r, d), lambda j: (0, 0)),
                  pl.BlockSpec((d, tn), lambda j: (0, j)),
                  pl.BlockSpec((1, tn), lambda j: (0, j))],
        out_specs=pl.BlockSpec((r, tn), lambda j: (0, j)),
        out_shape=jax.ShapeDtypeStruct((r, n), F32),
        compiler_params=_cparams(("arbitrary",), 40 * 1024 * 1024),
        name="ada",
    )(c, w_ada, b_ada.reshape(1, n))


_SEG = dict(qa=(0, 384), kva=(384, 640), kr=(640, 768), gq=(768, 1024), gk=(1024, 1280),
            gv=(1280, 1792), gr=(1792, 2304), glr=(2304, 2432))
_W1_COLS = 2432


def _proj_body(pos0, ts, x_ref, sh_ref, sc_ref, gmix_ref, w1_ref, gqa_ref, wuq_ref, gkv_ref, gqk_ref,
               rope_ref, wg2_ref, bg2_ref,
               q_ref, lat_ref, kr_ref, gq_o, gk_o, gv_o, gl_o, gr_o):
    i = pl.program_id(1)
    x = x_ref[...]
    d = x.shape[-1]
    h = (x * _rms(x, d) * gmix_ref[...]) * (1.0 + sc_ref[...]) + sh_ref[...]
    hb = h.astype(BF16)

    def seg(name):
        a, b = _SEG[name]
        return jnp.dot(hb, w1_ref[:, a:b], preferred_element_type=F32)

    pos = (pos0 + i * ts + lax.broadcasted_iota(I32, (ts, LANES), 0)).astype(F32)
    ang = pos * rope_ref[0:1, :]
    cos = jnp.cos(ang)
    sin = jnp.sin(ang) * rope_ref[1:2, :]
    lane = lax.broadcasted_iota(I32, (ts, LANES), 1)
    first_half = (lane & HALF) == 0
    low64 = lane < ROPE

    def rope(v):
        partner = jnp.where(first_half, pltpu.roll(v, LANES - HALF, 1), pltpu.roll(v, HALF, 1))
        return v * cos + partner * sin

    qa = seg("qa")
    qn = (qa * _rms(qa, Q_LORA) * gqa_ref[...]).astype(BF16)
    qf = jnp.dot(qn, wuq_ref[...], preferred_element_type=F32)
    rope_blocks = (rope(qf[:, 4 * NOPE:4 * NOPE + LANES]), rope(qf[:, 4 * NOPE + LANES:4 * NOPE + 2 * LANES]))
    for hd in range(HEADS):
        nope = qf[:, NOPE * hd:NOPE * (hd + 1)]
        blk = rope_blocks[hd // 2]
        if hd % 2:
            blk = pltpu.roll(blk, ROPE, 1)
        blk = jnp.where(low64, blk, 0.0)
        ss = jnp.sum(nope * nope, axis=-1, keepdims=True) + jnp.sum(blk * blk, axis=-1, keepdims=True)
        scl = lax.rsqrt(ss * (1.0 / QK) + EPS) * (QK ** -0.5 * LOG2_E)
        q_ref[hd, :, 0:NOPE] = (nope * scl * gqk_ref[0:1, :]).astype(BF16)
        q_ref[hd, :, NOPE:QK_PAD] = (blk * scl * gqk_ref[1:2, :]).astype(BF16)

    kva = seg("kva")
    lat_ref[...] = kva * _rms(kva, KV_LORA) * gkv_ref[...]
    kr_ref[...] = rope(seg("kr"))[:, 0:ROPE]

    gq_o[...] = seg("gq") * (GLA_DK ** -0.5)
    gk_o[...] = seg("gk")
    gv_o[...] = seg("gv").astype(BF16)
    gr_o[...] = seg("gr")
    z = jnp.dot(seg("glr").astype(BF16), wg2_ref[...], preferred_element_type=F32) + bg2_ref[...]
    gl_o[...] = (jnp.minimum(z, 0.0) - jnp.log1p(jnp.exp(-jnp.abs(z)))) * (1.0 / GATE_NORM)


def _proj(x, shift, scale, pos0, wts):
    b, s, d = x.shape
    ts = min(s, 512)
    row = lambda a: pl.BlockSpec(a.shape, lambda bi, i: (0,) * a.ndim)
    tok = lambda w: pl.BlockSpec((None, ts, w), lambda bi, i: (bi, i, 0))
    mod = pl.BlockSpec((None, 1, d), lambda bi, i: (bi, 0, 0))
    small = [wts["g_mix"], wts["w1"], wts["g_qa"], wts["w_uq"], wts["g_kv"], wts["g_qk_q"], wts["rope"],
             wts["w_g2"], wts["b_g2"]]
    out_shape = (
        jax.ShapeDtypeStruct((b, HEADS, s, QK_PAD), BF16),
        jax.ShapeDtypeStruct((b, s, KV_LORA), F32),
        jax.ShapeDtypeStruct((b, s, ROPE), F32),
        jax.ShapeDtypeStruct((b, s, HEADS * GLA_DK), F32),
        jax.ShapeDtypeStruct((b, s, HEADS * GLA_DK), F32),
        jax.ShapeDtypeStruct((b, s, HEADS * GLA_DV), BF16),
        jax.ShapeDtypeStruct((b, s, HEADS * GLA_DK), F32),
        jax.ShapeDtypeStruct((b, s, HEADS * GLA_DV), F32),
    )
    out_specs = (
        pl.BlockSpec((None, HEADS, ts, QK_PAD), lambda bi, i: (bi, 0, i, 0)),
        tok(KV_LORA), tok(ROPE), tok(HEADS * GLA_DK), tok(HEADS * GLA_DK), tok(HEADS * GLA_DV),
        tok(HEADS * GLA_DK), tok(HEADS * GLA_DV),
    )
    return pl.pallas_call(
        functools.partial(_proj_body, pos0, ts),
        grid=(b, s // ts),
        in_specs=[tok(d), mod, mod] + [row(a) for a in small],
        out_specs=out_specs,
        out_shape=out_shape,
        compiler_params=_cparams(("arbitrary", "arbitrary"), VMEM_BIG),
        name="proj",
    )(x, shift, scale, *small)


def _kv_body(v_transposed, lat_ref, kr_ref, wk_ref, wv_ref, gk_ref, k_ref, v_ref):
    lat = lat_ref[...].astype(BF16)
    kn_all = jnp.dot(lat, wk_ref[...], preferred_element_type=F32)
    kr = kr_ref[...]
    kr_ss = jnp.sum(kr * kr, axis=-1, keepdims=True)
    for hd in range(HEADS):
        kn = kn_all[:, NOPE * hd:NOPE * (hd + 1)]
        scl = lax.rsqrt((jnp.sum(kn * kn, axis=-1, keepdims=True) + kr_ss) * (1.0 / QK) + EPS)
        k_ref[hd, :, 0:NOPE] = (kn * scl * gk_ref[0:1, :]).astype(BF16)
        k_ref[hd, :, NOPE:QK] = (kr * scl * gk_ref[1:2, 0:ROPE]).astype(BF16)
        k_ref[hd, :, QK:QK_PAD] = jnp.zeros((kr.shape[0], QK_PAD - QK), BF16)
    if v_transposed:
        v_t = _nt(wv_ref[...], lat)
        for hd in range(HEADS):
            v_ref[hd] = v_t[V_DIM * hd:V_DIM * (hd + 1), :].astype(BF16)
    else:
        v_all = _nt(lat, wv_ref[...])
        for hd in range(HEADS):
            v_ref[hd] = v_all[:, V_DIM * hd:V_DIM * (hd + 1)].astype(BF16)


def _kv(lat, kr, w_uk, w_uv_t, g_qk_k, v_transposed):
    b, s, _ = lat.shape
    ts = min(s, ATTN_TILE)
    if v_transposed:
        v_spec = pl.BlockSpec((None, HEADS, None, V_DIM, ts), lambda bi, i: (bi, 0, i, 0, 0))
        v_shape = jax.ShapeDtypeStruct((b, HEADS, s // ts, V_DIM, ts), BF16)
    else:
        v_spec = pl.BlockSpec((None, HEADS, ts, V_DIM), lambda bi, i: (bi, 0, i, 0))
        v_shape = jax.ShapeDtypeStruct((b, HEADS, s, V_DIM), BF16)
    return pl.pallas_call(
        functools.partial(_kv_body, v_transposed),
        grid=(b, s // ts),
        in_specs=[pl.BlockSpec((None, ts, KV_LORA), lambda bi, i: (bi, i, 0)),
                  pl.BlockSpec((None, ts, ROPE), lambda bi, i: (bi, i, 0)),
                  pl.BlockSpec(w_uk.shape, lambda bi, i: (0, 0)),
                  pl.BlockSpec(w_uv_t.shape, lambda bi, i: (0, 0)),
                  pl.BlockSpec(g_qk_k.shape, lambda bi, i: (0, 0))],
        out_specs=(pl.BlockSpec((None, HEADS, ts, QK_PAD), lambda bi, i: (bi, 0, i, 0)), v_spec),
        out_shape=(jax.ShapeDtypeStruct((b, HEADS, s, QK_PAD), BF16), v_shape),
        compiler_params=_cparams(("arbitrary", "arbitrary")),
        name="kv",
    )(lat, kr, w_uk, w_uv_t, g_qk_k)


def _softmax_step(carry, s, vt):
    m, l, acc = carry
    m_new = jnp.maximum(m, jnp.max(s, axis=-1, keepdims=True))
    alpha = jnp.exp2(m - m_new)
    p = jnp.exp2(s - m_new)
    l = alpha * l + jnp.sum(p, axis=-1, keepdims=True)
    acc = alpha * acc + jnp.dot(p.astype(BF16), vt, preferred_element_type=F32)
    return m_new, l, acc


def _chunk_mask(tq, tk, q0, k0):
    qc = (q0 + lax.broadcasted_iota(I32, (tq, tk), 0)) // CHUNK
    kc = (k0 + lax.broadcasted_iota(I32, (tq, tk), 1)) // CHUNK
    return kc <= qc


def _attn_prompt_body(t, q_ref, k_ref, vt_ref, o_ref, s_a, s_b):
    i = pl.program_id(2)
    q = q_ref[...]

    def scores(j, buf):
        buf[...] = _nt(k_ref[pl.ds(pl.multiple_of(j * t, t), t), :], q)

    def consume(j, buf, carry, masked=False):
        m, l, acc = carry
        s = buf[...]
        if masked:
            visible = (lax.broadcasted_iota(I32, (t, t), 0) // CHUNK) <= (lax.broadcasted_iota(I32, (t, t), 1) // CHUNK)
            s = jnp.where(visible, s, NEG)
        m_new = jnp.maximum(m, jnp.max(s, axis=0, keepdims=True))
        alpha = jnp.exp2(m - m_new)
        p = jnp.exp2(s - m_new)
        l = alpha * l + jnp.sum(p, axis=0, keepdims=True)
        acc = alpha * acc + jnp.dot(vt_ref[j], p.astype(BF16), preferred_element_type=F32)
        return m_new, l, acc

    def pair(pp, carry):
        j = 2 * pp
        scores(j + 1, s_b)
        carry = consume(j, s_a, carry)
        scores(j + 2, s_a)
        return consume(j + 1, s_b, carry)

    def odd_tail(carry):
        scores(i, s_b)
        carry = consume(i - 1, s_a, carry)
        return consume(i, s_b, carry, masked=True)

    scores(0, s_a)
    carry = (jnp.full((1, t), NEG, F32), jnp.zeros((1, t), F32), jnp.zeros((V_DIM, t), F32))
    carry = lax.fori_loop(0, i // 2, pair, carry)
    _, l, acc = lax.cond(i % 2 == 1, odd_tail, lambda c: consume(i, s_a, c, masked=True), carry)
    o_ref[...] = (acc / l).T.astype(BF16)


def _attn_prompt(q, k, v_t):
    b, _, s, _ = q.shape
    t = v_t.shape[-1]
    return pl.pallas_call(
        functools.partial(_attn_prompt_body, t),
        grid=(b, HEADS, s // t),
        in_specs=[pl.BlockSpec((None, None, t, QK_PAD), lambda bi, h, i: (bi, h, i, 0)),
                  pl.BlockSpec((None, None, s, QK_PAD), lambda bi, h, i: (bi, h, 0, 0)),
                  pl.BlockSpec((None, None, s // t, V_DIM, t), lambda bi, h, i: (bi, h, 0, 0, 0))],
        out_specs=pl.BlockSpec((None, t, V_DIM), lambda bi, h, i: (bi, i, h)),
        out_shape=jax.ShapeDtypeStruct((b, s, HEADS * V_DIM), BF16),
        scratch_shapes=[pltpu.VMEM((t, t), F32), pltpu.VMEM((t, t), F32)],
        compiler_params=_cparams(("arbitrary", "arbitrary", "arbitrary"), VMEM_BIG),
        name="attn_prompt",
    )(q, k, v_t)


def _attn_sample_body(past, sq, q_ref, kp_ref, vp_ref, kn_ref, vn_ref, o_ref):
    q = q_ref[...]
    init = (jnp.full((sq, 1), NEG, F32), jnp.zeros((sq, 1), F32), jnp.zeros((sq, V_DIM), F32))
    carry = _softmax_step(init, _nt(q, kp_ref[...]), vp_ref[...])
    s = jnp.where(_chunk_mask(sq, sq, past, past), _nt(q, kn_ref[...]), NEG)
    _, l, acc = _softmax_step(carry, s, vn_ref[...])
    o_ref[...] = (acc / l).astype(BF16)


def _attn_sample(q, kp, vp, kn, vn):
    b, _, sq, _ = q.shape
    past = kp.shape[2]
    head = lambda n, w: pl.BlockSpec((None, None, n, w), lambda bi, h: (bi, h, 0, 0))
    return pl.pallas_call(
        functools.partial(_attn_sample_body, past, sq),
        grid=(b, HEADS),
        in_specs=[head(sq, QK_PAD), head(past, QK_PAD), head(past, V_DIM), head(sq, QK_PAD), head(sq, V_DIM)],
        out_specs=pl.BlockSpec((None, sq, V_DIM), lambda bi, h: (bi, 0, h)),
        out_shape=jax.ShapeDtypeStruct((b, sq, HEADS * V_DIM), BF16),
        compiler_params=_cparams(("arbitrary", "arbitrary")),
        name="attn_sample",
    )(q, kp, vp, kn, vn)


def _gla_masks(c):
    levels = int(np.log2(c))
    idx = np.arange(c)
    le = idx[None, :] <= idx[:, None]
    gt = idx[None, :] > idx[:, None]
    blocks = [le, gt]
    for l in range(levels):
        n = c >> l
        bot = (idx % n) >= n // 2
        same = (idx // n)[:, None] == (idx // n)[None, :]
        blocks.append(same & bot[:, None] & bot[None, :] & le)
        blocks.append(same & ~bot[:, None] & ~bot[None, :] & gt)
    return np.concatenate(blocks, axis=0).astype(np.float32), levels


def _gla_body(c, n_chunks, levels, mall_ref, q_ref, k_ref, g_ref, v_ref, r_ref, s0_ref, gout_ref,
              o_ref, sfin_ref, st_scr):
    it = pl.program_id(1)

    @pl.when(it == 0)
    def _():
        st_scr[...] = s0_ref[...]

    lane = lax.broadcasted_iota(I32, (c, LANES), 1)
    head_lanes = (lane < GLA_DK, lane >= GLA_DK)
    st_lane_lo = lax.broadcasted_iota(I32, (GLA_DV, LANES), 1) < GLA_DK
    row = lax.broadcasted_iota(I32, (c, 1), 0)
    ri = lax.broadcasted_iota(I32, (c, c), 0)
    ci = lax.broadcasted_iota(I32, (c, c), 1)
    mall = mall_ref[...]

    for ch in range(n_chunks):
        rows = slice(ch * c, (ch + 1) * c)
        for p in range(HEADS // 2):
            ls = slice(LANES * p, LANES * (p + 1))
            g = g_ref[rows, ls]
            q = q_ref[rows, ls]
            k = k_ref[rows, ls]
            g_hi = g.astype(BF16)
            g_lo = (g - g_hi.astype(F32)).astype(BF16)
            e2 = jnp.dot(mall, jnp.concatenate([g_hi, g_lo], axis=1), preferred_element_type=F32)
            e = e2[:, 0:LANES] + e2[:, LANES:2 * LANES]
            eb = jnp.exp(e[0:c])
            qb = q * eb
            kd = (k * jnp.exp(e[c:2 * c])).astype(BF16)
            d_last = eb[c - 1:c, :]
            st = st_scr[p]
            st_b = st.astype(BF16)
            qs, ks = [q], [k.astype(BF16)]
            for l in range(levels):
                bottom = (row & (c >> (l + 1))) != 0
                qs.append(jnp.where(bottom, q * jnp.exp(e[(2 + 2 * l) * c:(3 + 2 * l) * c]), 0.0))
                ks.append(jnp.where(bottom, 0.0, k * jnp.exp(e[(3 + 2 * l) * c:(4 + 2 * l) * c])).astype(BF16))
            upd = []
            for hh in range(2):
                hd = 2 * p + hh
                sel = head_lanes[hh]
                a = jnp.where(ri == ci, _nt(jnp.where(sel, qs[0], 0.0).astype(BF16), ks[0]), 0.0)
                for l in range(levels):
                    pr = _nt(jnp.where(sel, qs[l + 1], 0.0).astype(BF16), ks[l + 1])
                    if l > 0:
                        pr = jnp.where((ri ^ ci) < (c >> l), pr, 0.0)
                    a = a + pr
                vh = v_ref[rows, GLA_DV * hd:GLA_DV * (hd + 1)]
                o = jnp.dot(a.astype(BF16), vh, preferred_element_type=F32)
                o = o + _nt(jnp.where(sel, qb, 0.0).astype(BF16), st_b)
                on = o * _rms(o, GLA_DV) * gout_ref[...]
                r = r_ref[rows, GLA_DV * hd:GLA_DV * (hd + 1)]
                o_ref[rows, GLA_DV * hd:GLA_DV * (hd + 1)] = (on * (r * jax.nn.sigmoid(r))).astype(BF16)
                upd.append(lax.dot_general(vh, kd, (((0,), (0,)), ((), ())), preferred_element_type=F32))
            st_scr[p] = st * d_last + jnp.where(st_lane_lo, upd[0], upd[1])

    @pl.when(it == pl.num_programs(1) - 1)
    def _():
        sfin_ref[...] = st_scr[...]


def _gla(gq, gk, gl, gv, gr, s0, g_out):
    b, s, _ = gq.shape
    c = min(CHUNK, s)
    tile = min(s, 4 * c)
    masks, levels = _gla_masks(c)
    mall = jnp.asarray(masks, BF16)
    tok = lambda w: pl.BlockSpec((None, tile, w), lambda bi, i: (bi, i, 0))
    st_spec = pl.BlockSpec((None, HEADS // 2, GLA_DV, LANES), lambda bi, i: (bi, 0, 0, 0))
    return pl.pallas_call(
        functools.partial(_gla_body, c, tile // c, levels),
        grid=(b, s // tile),
        in_specs=[pl.BlockSpec(mall.shape, lambda bi, i: (0, 0)),
                  tok(HEADS * GLA_DK), tok(HEADS * GLA_DK), tok(HEADS * GLA_DK), tok(HEADS * GLA_DV),
                  tok(HEADS * GLA_DV), st_spec, pl.BlockSpec(g_out.shape, lambda bi, i: (0, 0))],
        out_specs=(tok(HEADS * GLA_DV), st_spec),
        out_shape=(jax.ShapeDtypeStruct((b, s, HEADS * GLA_DV), BF16),
                   jax.ShapeDtypeStruct((b, HEADS // 2, GLA_DV, LANES), F32)),
        scratch_shapes=[pltpu.VMEM((HEADS // 2, GLA_DV, LANES), F32)],
        compiler_params=_cparams(("arbitrary", "arbitrary")),
        name="gla",
    )(mall, gq, gk, gl, gv, gr, s0, g_out)


def _state_to_pairs(s):
    b = s.shape[0]
    s = s.reshape(b, HEADS // 2, 2, GLA_DK, GLA_DV)
    return jnp.transpose(s, (0, 1, 4, 2, 3)).reshape(b, HEADS // 2, GLA_DV, 2 * GLA_DK)


def _state_from_pairs(s):
    b = s.shape[0]
    s = s.reshape(b, HEADS // 2, GLA_DV, 2, GLA_DK)
    return jnp.transpose(s, (0, 1, 3, 4, 2)).reshape(b, HEADS, GLA_DK, GLA_DV)


def _post_body(x_ref, om_ref, og_ref, gt_ref, sc_ref, sh_ref, wo_ref, gffn_ref, wr_ref, br_ref,
               x2_ref, h_ref, idx_ref, wt_ref):
    half = om_ref.shape[-1]
    mix = (jnp.dot(om_ref[...], wo_ref[0:half, :], preferred_element_type=F32)
           + jnp.dot(og_ref[...], wo_ref[half:2 * half, :], preferred_element_type=F32))
    x2 = x_ref[...] + gt_ref[...] * mix
    x2_ref[...] = x2
    d = x2.shape[-1]
    h = (x2 * _rms(x2, d) * gffn_ref[...]) * (1.0 + sc_ref[...]) + sh_ref[...]
    h_hi = h.astype(BF16)
    h_ref[...] = h_hi
    h_lo = (h - h_hi.astype(F32)).astype(BF16)
    logits = _nt(wr_ref[0], h_hi) + _nt(wr_ref[0], h_lo) + _nt(wr_ref[1], h_hi) + br_ref[...]
    n_exp, tm = logits.shape
    eid = lax.broadcasted_iota(I32, (n_exp, tm), 0)
    vals, tops, ids = logits, [], []
    for _ in range(TOP_K):
        m = jnp.max(vals, axis=0, keepdims=True)
        sel = jnp.min(jnp.where(vals == m, eid, n_exp), axis=0, keepdims=True)
        tops.append(m)
        ids.append(sel)
        vals = jnp.where(eid == sel, -jnp.inf, vals)
    es = [jnp.exp(t - tops[0]) for t in tops]
    tot = es[0] + es[1] + es[2] + es[3]
    idx_ref[...] = jnp.concatenate(ids, axis=0)
    wt_ref[...] = jnp.concatenate([e / tot for e in es], axis=0)


def _post(x, om, og, gate, scale, shift, w_o, g_ffn, w_r2, b_r):
    t, d = x.shape
    tm = min(t, 512)
    per_tok = gate.shape[0] == t
    mod = pl.BlockSpec((tm, d), lambda i: (i, 0)) if per_tok else pl.BlockSpec((1, d), lambda i: (0, 0))
    tok = lambda w: pl.BlockSpec((tm, w), lambda i: (i, 0))
    full = lambda a: pl.BlockSpec(a.shape, lambda i: (0,) * a.ndim)
    return pl.pallas_call(
        _post_body,
        grid=(t // tm,),
        in_specs=[tok(d), tok(om.shape[1]), tok(og.shape[1]), mod, mod, mod, full(w_o), full(g_ffn), full(w_r2),
                  full(b_r)],
        out_specs=(tok(d), tok(d),
                   pl.BlockSpec((TOP_K, tm), lambda i: (0, i)), pl.BlockSpec((TOP_K, tm), lambda i: (0, i))),
        out_shape=(jax.ShapeDtypeStruct((t, d), F32), jax.ShapeDtypeStruct((t, d), BF16),
                   jax.ShapeDtypeStruct((TOP_K, t), I32), jax.ShapeDtypeStruct((TOP_K, t), F32)),
        compiler_params=_cparams(("arbitrary",), 40 * 1024 * 1024),
        name="post",
    )(x, om, og, gate, scale, shift, w_o, g_ffn, w_r2, b_r)


def _rank_body(idx_ref, rank_ref, cnt_ref):
    idx = idx_ref[...]
    tg = idx.shape[1]
    eid = lax.broadcasted_iota(I32, (N_EXPERTS, tg), 0)
    hits = [eid == idx[k:k + 1, :] for k in range(TOP_K)]
    member = jnp.zeros((N_EXPERTS, tg), F32)
    for hk in hits:
        member = member + jnp.where(hk, 1.0, 0.0)
    before = (lax.broadcasted_iota(I32, (tg, tg), 0) < lax.broadcasted_iota(I32, (tg, tg), 1))
    prefix = jnp.dot(member.astype(BF16), jnp.where(before, 1.0, 0.0).astype(BF16), preferred_element_type=F32)
    rank_ref[...] = jnp.concatenate(
        [jnp.sum(jnp.where(hk, prefix, 0.0), axis=0, keepdims=True) for hk in hits], axis=0).astype(I32)
    cnt_ref[...] = jnp.broadcast_to(jnp.sum(member, axis=1, keepdims=True), (N_EXPERTS, LANES)).astype(I32)


def _rank(idx):
    t = idx.shape[1]
    nt = t // ROUTE_TILE
    return pl.pallas_call(
        _rank_body,
        grid=(nt,),
        in_specs=[pl.BlockSpec((TOP_K, ROUTE_TILE), lambda i: (0, i))],
        out_specs=(pl.BlockSpec((TOP_K, ROUTE_TILE), lambda i: (0, i)),
                   pl.BlockSpec((None, N_EXPERTS, LANES), lambda i: (i, 0, 0))),
        out_shape=(jax.ShapeDtypeStruct((TOP_K, t), I32), jax.ShapeDtypeStruct((nt, N_EXPERTS, LANES), I32)),
        compiler_params=_cparams(("arbitrary",)),
        name="rank",
    )(idx)


def _route_tables(idx, lrank, cnt3):
    nt = cnt3.shape[0]
    t = idx.shape[1]
    cnt = cnt3[:, :, 0]
    run = _round_up(cnt, RUN_CHUNK)
    lo_end = jnp.cumsum(run, axis=1)
    lo = lo_end - run
    n_chunks = lo_end[:, -1] // RUN_CHUNK
    region = _round_up(jnp.sum(run, axis=0), EXPERT_ROWS)
    g_end = jnp.cumsum(region)
    run_dest = (g_end - region)[None, :] + jnp.cumsum(run, axis=0) - run
    max_rows = TOP_K * ROUTE_TILE + N_EXPERTS * (RUN_CHUNK - 1)
    n_tab = _round_up(max_rows, RUN_CHUNK) // RUN_CHUNK
    c_start = jnp.arange(n_tab, dtype=I32) * RUN_CHUNK
    e_of_c = jnp.minimum(jnp.sum(lo_end[:, None, :] <= c_start[None, :, None], axis=2), N_EXPERTS - 1)
    pick = e_of_c[:, :, None] == jnp.arange(N_EXPERTS, dtype=I32)[None, None, :]
    chunk_dest = jnp.sum(jnp.where(pick, (run_dest - lo)[:, None, :], 0), axis=2) + c_start[None, :]
    table = jnp.concatenate([chunk_dest, n_chunks[:, None]], axis=1).astype(I32).reshape(nt, 1, n_tab + 1)
    eid = jnp.arange(N_EXPERTS, dtype=I32)[:, None]
    lo_tok = jnp.repeat(lo.T, ROUTE_TILE, axis=1)
    lpos = jnp.stack([jnp.sum(jnp.where(idx[k][None, :] == eid, lo_tok, 0), axis=0) for k in range(TOP_K)])
    lpos = (lpos + lrank).astype(I32)
    n_blocks = _round_up(t * TOP_K + nt * N_EXPERTS * (RUN_CHUNK - 1), EXPERT_ROWS) // EXPERT_ROWS + N_EXPERTS
    b_start = jnp.arange(n_blocks, dtype=I32) * EXPERT_ROWS
    blk_e = jnp.minimum(jnp.sum(g_end[None, :] <= b_start[:, None], axis=1), N_EXPERTS - 1).astype(I32)
    n_valid = (g_end[-1:] // EXPERT_ROWS).astype(I32)
    tail = jnp.concatenate([jnp.where(region > 0, g_end - EXPERT_ROWS, -1), n_valid]).astype(I32)
    tail = tail.reshape(1, N_EXPERTS + 1)
    return dict(table=table, lpos=lpos, blk_e=blk_e, n_valid=n_valid, tail=tail, n_blocks=n_blocks)


def _chunk_copy(src, dst, sem):
    return pltpu.make_async_copy(src, dst, sem)


def _scatter_body(n_tab, nt_a, n_blocks, tab_ref, tail_ref, lpos_ref, ha_ref, hb_ref, xout_ref, sorted_scr, zero_scr,
                  sem):
    i = pl.program_id(0)
    n_chunks = tab_ref[0, n_tab]
    tt = ha_ref.shape[0]

    @pl.when(i == 0)
    def _():
        zero_scr[...] = jnp.zeros_like(zero_scr)
        n_valid = tail_ref[0, N_EXPERTS]

        def block(start):
            return xout_ref.at[pl.ds(pl.multiple_of(start, EXPERT_ROWS), EXPERT_ROWS)]

        for e in range(N_EXPERTS):
            @pl.when(tail_ref[0, e] >= 0)
            def _():
                _chunk_copy(zero_scr, block(tail_ref[0, e]), sem).start()

        def fill(b, carry):
            _chunk_copy(zero_scr, block(b * EXPERT_ROWS), sem).start()
            return carry

        def fill_done(b, carry):
            _chunk_copy(zero_scr, block(0), sem).wait()
            return carry

        lax.fori_loop(n_valid, n_blocks, fill, 0)
        for e in range(N_EXPERTS):
            @pl.when(tail_ref[0, e] >= 0)
            def _():
                _chunk_copy(zero_scr, block(0), sem).wait()
        lax.fori_loop(n_valid, n_blocks, fill_done, 0)

    lp = lpos_ref[...]
    h = jnp.where(i < nt_a, ha_ref[...], hb_ref[...])

    def sort_block(rb, carry):
        r0 = pl.multiple_of(rb * SORT_ROWS, SORT_ROWS)
        rid = r0 + lax.broadcasted_iota(I32, (SORT_ROWS, tt), 0)
        onehot = jnp.zeros((SORT_ROWS, tt), F32)
        for k in range(TOP_K):
            onehot = onehot + jnp.where(lp[k:k + 1, :] == rid, 1.0, 0.0)
        sorted_scr[pl.ds(r0, SORT_ROWS), :] = jnp.dot(onehot.astype(BF16), h, preferred_element_type=F32).astype(BF16)
        return carry

    lax.fori_loop(0, (n_chunks * RUN_CHUNK + SORT_ROWS - 1) // SORT_ROWS, sort_block, 0)

    def issue(c, carry):
        src = sorted_scr.at[pl.ds(pl.multiple_of(c * RUN_CHUNK, RUN_CHUNK), RUN_CHUNK)]
        dst = xout_ref.at[pl.ds(pl.multiple_of(tab_ref[0, c], RUN_CHUNK), RUN_CHUNK)]
        _chunk_copy(src, dst, sem).start()
        return carry

    def drain(c, carry):
        _chunk_copy(sorted_scr.at[pl.ds(0, RUN_CHUNK)], xout_ref.at[pl.ds(0, RUN_CHUNK)], sem).wait()
        return carry

    lax.fori_loop(0, n_chunks, issue, 0)
    lax.fori_loop(0, n_chunks, drain, 0)


def _scatter(table, tail, lpos, h_a, h_b, n_blocks):
    d = h_a.shape[1]
    nt_a, nt_b = h_a.shape[0] // ROUTE_TILE, h_b.shape[0] // ROUTE_TILE
    n_tab = table.shape[2] - 1
    sort_cap = _round_up(n_tab * RUN_CHUNK, SORT_ROWS)
    return pl.pallas_call(
        functools.partial(_scatter_body, n_tab, nt_a, n_blocks),
        grid=(nt_a + nt_b,),
        in_specs=[pl.BlockSpec((None, 1, n_tab + 1), lambda i: (i, 0, 0), memory_space=pltpu.SMEM),
                  pl.BlockSpec((1, N_EXPERTS + 1), lambda i: (0, 0), memory_space=pltpu.SMEM),
                  pl.BlockSpec((TOP_K, ROUTE_TILE), lambda i: (0, i)),
                  pl.BlockSpec((ROUTE_TILE, d), lambda i: (jnp.minimum(i, nt_a - 1), 0)),
                  pl.BlockSpec((ROUTE_TILE, d), lambda i: (jnp.maximum(i - nt_a, 0), 0))],
        out_specs=pl.BlockSpec(memory_space=pl.ANY),
        out_shape=jax.ShapeDtypeStruct((n_blocks * EXPERT_ROWS, d), BF16),
        scratch_shapes=[pltpu.VMEM((sort_cap, d), BF16), pltpu.VMEM((EXPERT_ROWS, d), BF16),
                        pltpu.SemaphoreType.DMA(())],
        compiler_params=_cparams(("arbitrary",), 40 * 1024 * 1024),
        name="scatter",
    )(table, tail, lpos, h_a, h_b)


def _experts_body(be_ref, nv_ref, x_ref, wgu_ref, bgu_ref, wd_ref, bd_ref, y_ref, wgu_s, wd_s):
    b = pl.program_id(0)
    e = be_ref[b]
    prev = be_ref[jnp.maximum(b - 1, 0)]
    valid = b < nv_ref[0]
    d_ff = wd_ref.shape[0]

    @pl.when(valid & ((b == 0) | (e != prev)))
    def _():
        wgu_s[...] = wgu_ref[...].astype(BF16)
        wd_s[...] = wd_ref[...].astype(BF16)

    @pl.when(valid)
    def _():
        gu = jnp.dot(x_ref[...], wgu_s[...], preferred_element_type=F32) + bgu_ref[...]
        gate = jnp.minimum(gu[:, 0:d_ff], SWIGLU_LIMIT)
        up = jnp.clip(gu[:, d_ff:2 * d_ff], -SWIGLU_LIMIT, SWIGLU_LIMIT)
        act = ((up + 1.0) * (gate * jax.nn.sigmoid(gate * SWIGLU_ALPHA))).astype(BF16)
        y_ref[...] = (jnp.dot(act, wd_s[...], preferred_element_type=F32) + bd_ref[...]).astype(BF16)

    @pl.when(jnp.logical_not(valid))
    def _():
        y_ref[...] = jnp.zeros_like(y_ref)


def _experts(blk_e, n_valid, xpad, w_gu, b_gu, w_down, b_down):
    m, d = xpad.shape
    nb = m // EXPERT_ROWS
    n_exp, _, f2 = w_gu.shape
    d_ff = w_down.shape[1]
    last = lambda b, be, nv: jnp.minimum(b, nv[0] - 1)
    grid_spec = pltpu.PrefetchScalarGridSpec(
        num_scalar_prefetch=2,
        grid=(nb,),
        in_specs=[pl.BlockSpec((EXPERT_ROWS, d), lambda b, be, nv: (last(b, be, nv), 0)),
                  pl.BlockSpec((None, d, f2), lambda b, be, nv: (be[last(b, be, nv)], 0, 0)),
                  pl.BlockSpec((None, 1, f2), lambda b, be, nv: (be[last(b, be, nv)], 0, 0)),
                  pl.BlockSpec((None, d_ff, d), lambda b, be, nv: (be[last(b, be, nv)], 0, 0)),
                  pl.BlockSpec((None, 1, d), lambda b, be, nv: (be[last(b, be, nv)], 0, 0))],
        out_specs=pl.BlockSpec((EXPERT_ROWS, d), lambda b, be, nv: (b, 0)),
        scratch_shapes=[pltpu.VMEM((d, f2), BF16), pltpu.VMEM((d_ff, d), BF16)],
    )
    return pl.pallas_call(
        _experts_body,
        grid_spec=grid_spec,
        out_shape=jax.ShapeDtypeStruct((m, d), BF16),
        compiler_params=_cparams(("arbitrary",), VMEM_BIG),
        name="experts",
    )(blk_e, n_valid, xpad, w_gu, b_gu.reshape(n_exp, 1, f2), w_down, b_down.reshape(n_exp, 1, d))


def _combine_body(n_tab, tab_ref, lpos_ref, wt_ref, x2_ref, gt_ref, y_ref, o_ref, ysort_scr, sem):
    i = pl.program_id(0)
    n_chunks = tab_ref[0, n_tab]
    tt = x2_ref.shape[0]

    @pl.when(i == 0)
    def _():
        ysort_scr[...] = jnp.zeros_like(ysort_scr)

    def issue(c, carry):
        src = y_ref.at[pl.ds(pl.multiple_of(tab_ref[0, c], RUN_CHUNK), RUN_CHUNK)]
        dst = ysort_scr.at[pl.ds(pl.multiple_of(c * RUN_CHUNK, RUN_CHUNK), RUN_CHUNK)]
        _chunk_copy(src, dst, sem).start()
        return carry

    def drain(c, carry):
        _chunk_copy(y_ref.at[pl.ds(0, RUN_CHUNK)], ysort_scr.at[pl.ds(0, RUN_CHUNK)], sem).wait()
        return carry

    lax.fori_loop(0, n_chunks, issue, 0)
    lax.fori_loop(0, n_chunks, drain, 0)

    lp = lpos_ref[...]
    w = wt_ref[...]
    o_ref[...] = x2_ref[...]
    gt = gt_ref[...]

    def gather_block(rb, carry):
        r0 = pl.multiple_of(rb * SORT_ROWS, SORT_ROWS)
        rid = r0 + lax.broadcasted_iota(I32, (tt, SORT_ROWS), 1)
        pw = jnp.zeros((tt, SORT_ROWS), F32)
        for k in range(TOP_K):
            pw = pw + jnp.where(lp[:, k:k + 1] == rid, w[:, k:k + 1], 0.0)
        pw_hi = pw.astype(BF16)
        pw_lo = (pw - pw_hi.astype(F32)).astype(BF16)
        ys = ysort_scr[pl.ds(r0, SORT_ROWS), :]
        part = jnp.dot(pw_hi, ys, preferred_element_type=F32) + jnp.dot(pw_lo, ys, preferred_element_type=F32)
        o_ref[...] = o_ref[...] + gt * part
        return carry

    lax.fori_loop(0, (n_chunks * RUN_CHUNK + SORT_ROWS - 1) // SORT_ROWS, gather_block, 0)


def _combine(table, lpos_t, wts, x2, gate, ypad):
    t, d = x2.shape
    nt = t // ROUTE_TILE
    n_tab = table.shape[2] - 1
    per_tok = gate.shape[0] == t
    mod = pl.BlockSpec((ROUTE_TILE, d), lambda i: (i, 0)) if per_tok else pl.BlockSpec((1, d), lambda i: (0, 0))
    sort_cap = _round_up(n_tab * RUN_CHUNK, SORT_ROWS)
    return pl.pallas_call(
        functools.partial(_combine_body, n_tab),
        grid=(nt,),
        in_specs=[pl.BlockSpec((None, 1, n_tab + 1), lambda i: (i, 0, 0), memory_space=pltpu.SMEM),
                  pl.BlockSpec((ROUTE_TILE, TOP_K), lambda i: (i, 0)),
                  pl.BlockSpec((ROUTE_TILE, TOP_K), lambda i: (i, 0)),
                  pl.BlockSpec((ROUTE_TILE, d), lambda i: (i, 0)),
                  mod,
                  pl.BlockSpec(memory_space=pl.ANY)],
        out_specs=pl.BlockSpec((ROUTE_TILE, d), lambda i: (i, 0)),
        out_shape=jax.ShapeDtypeStruct((t, d), F32),
        scratch_shapes=[pltpu.VMEM((sort_cap, d), BF16), pltpu.SemaphoreType.DMA(())],
        compiler_params=_cparams(("arbitrary",), 40 * 1024 * 1024),
        name="combine",
    )(table, lpos_t, wts, x2, gate, ypad)


def _prep_weights(g_norm_mix, w_in, g_q_a, w_uq, g_kv_a, w_ukv, g_qk_q, g_qk_k, w_g2, b_g2, g_gla_out, w_o,
                  g_norm_ffn, w_router, b_router):
    d = w_in.shape[0]
    o_qa, o_kva, o_kr = 0, Q_LORA, Q_LORA + KV_LORA
    o_gq = o_kr + ROPE
    o_gk = o_gq + HEADS * GLA_DK
    o_gv = o_gk + HEADS * GLA_DK
    o_glr = o_gv + HEADS * GLA_DV
    o_gr = o_glr + GATE_RANK
    kr_cols = w_in[:, o_kr:o_kr + ROPE]
    w1 = jnp.concatenate([
        w_in[:, o_qa:o_kva], w_in[:, o_kva:o_kr], kr_cols, kr_cols, w_in[:, o_gq:o_gk], w_in[:, o_gk:o_gv],
        w_in[:, o_gv:o_glr], w_in[:, o_gr:o_gr + HEADS * GLA_DV], w_in[:, o_glr:o_gr],
        jnp.zeros((d, LANES - GATE_RANK), w_in.dtype)], axis=1).astype(BF16)
    assert w1.shape[1] == _W1_COLS
    wq = w_uq.reshape(Q_LORA, HEADS, QK)
    wuq = jnp.concatenate([wq[:, :, 0:NOPE].reshape(Q_LORA, HEADS * NOPE),
                           wq[:, :, NOPE:QK].reshape(Q_LORA, HEADS * ROPE)], axis=1).astype(BF16)
    wkv = w_ukv.reshape(KV_LORA, HEADS, NOPE + V_DIM)
    wuk = wkv[:, :, 0:NOPE].reshape(KV_LORA, HEADS * NOPE).astype(BF16)
    wuv_t = wkv[:, :, NOPE:].reshape(KV_LORA, HEADS * V_DIM).T.astype(BF16)
    pad_rope = lambda g: jnp.stack([g[0:NOPE], jnp.concatenate([g[NOPE:QK], jnp.zeros((QK_PAD - QK,), g.dtype)])])
    inv = ROPE_THETA ** (-jnp.arange(HALF, dtype=F32) / HALF)
    sign = jnp.concatenate([-jnp.ones((HALF,), F32), jnp.ones((HALF,), F32)])
    rope_tab = jnp.stack([jnp.tile(inv, LANES // HALF), jnp.tile(sign, LANES // ROPE)])
    wg2 = jnp.concatenate([w_g2, jnp.zeros((LANES - GATE_RANK, w_g2.shape[1]), w_g2.dtype)], axis=0).astype(BF16)
    wr_t = w_router.T
    wr_hi = wr_t.astype(BF16)
    wr_lo = (wr_t - wr_hi.astype(F32)).astype(BF16)
    return dict(
        g_mix=g_norm_mix.reshape(1, d), w1=w1, g_qa=g_q_a.reshape(1, -1), w_uq=wuq, g_kv=g_kv_a.reshape(1, -1),
        w_uk=wuk, w_uv_t=wuv_t, g_qk_q=pad_rope(g_qk_q), g_qk_k=pad_rope(g_qk_k), rope=rope_tab, w_g2=wg2,
        b_g2=b_g2.reshape(1, -1), g_out=g_gla_out.reshape(1, -1), w_o=w_o.astype(BF16),
        g_ffn=g_norm_ffn.reshape(1, d), w_r2=jnp.stack([wr_hi, wr_lo]), b_r=b_router.reshape(-1, 1))


def _mixer(x, mod, pos0, past_lat, past_kr, s0_pairs, wts):
    b, s, d = x.shape
    q, lat, kr, gq, gk, gv, gl, gr = _proj(x, mod[:, 0:1], mod[:, 1:2], pos0, wts)
    kv = functools.partial(_kv, w_uk=wts["w_uk"], w_uv_t=wts["w_uv_t"], g_qk_k=wts["g_qk_k"])
    if past_lat is None:
        k_new, vt_new = kv(lat, kr, v_transposed=True)
        o_mla = _attn_prompt(q, k_new, vt_new)
    else:
        k_new, v_new = kv(lat, kr, v_transposed=False)
        k_past, v_past = kv(past_lat, past_kr, v_transposed=False)
        o_mla = _attn_sample(q, k_past, v_past, k_new, v_new)
    o_gla, s_fin = _gla(gq, gk, gl, gv, gr, s0_pairs, wts["g_out"])
    t = b * s
    if b == 1:
        rows = lambda j: mod[0, j:j + 1]
    else:
        rows = lambda j: jnp.broadcast_to(mod[:, j:j + 1], (b, s, d)).reshape(t, d)
    x2, h2, idx, wt = _post(x.reshape(t, d), o_mla.reshape(t, -1), o_gla.reshape(t, -1), rows(2), rows(4), rows(3),
                            wts["w_o"], wts["g_ffn"], wts["w_r2"], wts["b_r"])
    return dict(x2=x2, h2=h2, idx=idx, wt=wt, gate_f=rows(5), lat=lat, kr=kr, s_fin=s_fin)


def kernel(x_prompt, x_sample, cache_mla_latent, cache_mla_krope, state_gla, c_prompt, c_sample, w_ada, b_ada, g_norm_mix, w_in, g_q_a, w_uq, g_kv_a, w_ukv, g_qk_q, g_qk_k, w_g2, b_g2, g_gla_out, w_o, g_norm_ffn, w_router, b_router, w_gu, b_gu, w_down, b_down):
    depth = w_ada.shape[0]
    assert depth == 1, "single-layer step"
    bp, sp, d = x_prompt.shape
    bs, ss, _ = x_sample.shape
    tp, tsm = bp * sp, bs * ss
    assert tp % ROUTE_TILE == 0 and tsm % ROUTE_TILE == 0, "token counts must be whole routing tiles"
    past = cache_mla_latent.shape[2]
    wts = _prep_weights(g_norm_mix[0], w_in[0], g_q_a[0], w_uq[0], g_kv_a[0], w_ukv[0], g_qk_q[0], g_qk_k[0],
                        w_g2[0], b_g2[0], g_gla_out[0], w_o[0], g_norm_ffn[0], w_router[0], b_router[0])

    mod = _ada(jnp.concatenate([c_prompt, c_sample], axis=0), w_ada[0], b_ada[0]).reshape(bp + bs, 6, d)
    zero_state = jnp.zeros((bp, HEADS // 2, GLA_DV, LANES), F32)
    pr = _mixer(x_prompt, mod[:bp], 0, None, None, zero_state, wts)
    sa = _mixer(x_sample, mod[bp:], past, cache_mla_latent[0], cache_mla_krope[0], _state_to_pairs(state_gla[0]),
                wts)

    idx = jnp.concatenate([pr["idx"], sa["idx"]], axis=1)
    lrank, cnt3 = _rank(idx)
    rt = _route_tables(idx, lrank, cnt3)
    ntp = tp // ROUTE_TILE
    lpos, table = rt["lpos"], rt["table"]
    xpad = _scatter(table, rt["tail"], lpos, pr["h2"], sa["h2"], rt["n_blocks"])
    ypad = _experts(rt["blk_e"], rt["n_valid"], xpad, w_gu[0], b_gu[0], w_down[0], b_down[0])
    lpos_t = lpos.T
    y_p = _combine(table[:ntp], lpos_t[:tp], pr["wt"].T, pr["x2"], pr["gate_f"], ypad).reshape(bp, sp, d)
    y_s = _combine(table[ntp:], lpos_t[tp:], sa["wt"].T, sa["x2"], sa["gate_f"], ypad).reshape(bs, ss, d)

    return (y_p, y_s,
            pr["lat"][None], pr["kr"][None], _state_from_pairs(pr["s_fin"])[None],
            sa["lat"][None], sa["kr"][None], _state_from_pairs(sa["s_fin"])[None])
```

```python
import functools

import numpy as np
import jax
import jax.numpy as jnp
from jax import lax
from jax.experimental import pallas as pl
from jax.experimental.pallas import tpu as pltpu

F32 = jnp.float32
BF16 = jnp.bfloat16
I32 = jnp.int32

CHUNK = 64
EPS = 1e-6
HEADS = 4
Q_LORA = 384
KV_LORA = 256
NOPE = 128
ROPE = 64
HALF = ROPE // 2
V_DIM = 128
QK = NOPE + ROPE
QK_PAD = 256
ROPE_THETA = 10000.0
GLA_DK = 64
GLA_DV = 128
GATE_RANK = 16
GATE_NORM = 16.0
N_EXPERTS = 32
TOP_K = 4
SWIGLU_LIMIT = 7.0
SWIGLU_ALPHA = 1.702
NEG = -1e30
LOG2_E = 1.4426950408889634

LANES = 128
BF16_ROWS = 16
ROUTE_TILE = 512
RUN_CHUNK = BF16_ROWS
SORT_ROWS = 512
EXPERT_ROWS = 256
ATTN_TILE = 1024
VMEM_BIG = 56 * 1024 * 1024


def _cparams(sem, vmem=None):
    return pltpu.CompilerParams(dimension_semantics=sem, vmem_limit_bytes=vmem)


def _nt(a, b):
    return lax.dot_general(a, b, (((1,), (1,)), ((), ())), preferred_element_type=F32)


def _rms(x, width):
    return lax.rsqrt(jnp.sum(x * x, axis=-1, keepdims=True) * (1.0 / width) + EPS)


def _round_up(x, m):
    return ((x + m - 1) // m) * m


def _ada_body(c_ref, w_ref, b_ref, o_ref):
    c = c_ref[...]
    s = (c * jax.nn.sigmoid(c)).astype(BF16)
    o_ref[...] = jnp.dot(s, w_ref[...].astype(BF16), preferred_element_type=F32) + b_ref[...]


def _ada(c, w_ada, b_ada):
    r, d = c.shape
    n = w_ada.shape[1]
    tn = 1536 if n % 1536 == 0 else n
    return pl.pallas_call(
        _ada_body,
        grid=(n // tn,),
        in_specs=[pl.BlockSpec((r, d), lambda j: (0, 0)),
                  pl.BlockSpec((d, tn), lambda j: (0, j)),
                  pl.BlockSpec((1, tn), lambda j: (0, j))],
        out_specs=pl.BlockSpec((r, tn), lambda j: (0, j)),
        out_shape=jax.ShapeDtypeStruct((r, n), F32),
        compiler_params=_cparams(("arbitrary",), 40 * 1024 * 1024),
        name="ada",
    )(c, w_ada, b_ada.reshape(1, n))


_SEG = dict(qa=(0, 384), kva=(384, 640), kr=(640, 768), gq=(768, 1024), gk=(1024, 1280),
            gv=(1280, 1792), gr=(1792, 2304), glr=(2304, 2432))
_W1_COLS = 2432


def _proj_body(pos0, ts, x_ref, sh_ref, sc_ref, gmix_ref, w1_ref, gqa_ref, wuq_ref, gkv_ref, gqk_ref,
               rope_ref, wg2_ref, bg2_ref,
               q_ref, lat_ref, kr_ref, gq_o, gk_o, gv_o, gl_o, gr_o, trig_scr):
    i = pl.program_id(1)
    x = x_ref[...]
    d = x.shape[-1]
    h = (x * _rms(x, d) * gmix_ref[...]) * (1.0 + sc_ref[...]) + sh_ref[...]
    hb = h.astype(BF16)

    def seg(name):
        a, b = _SEG[name]
        return jnp.dot(hb, w1_ref[:, a:b], preferred_element_type=F32)

    @pl.when((pl.program_id(0) == 0) & (i == 0))
    def _():
        row_ang = lax.broadcasted_iota(I32, (ts, LANES), 0).astype(F32) * rope_ref[0:1, :]
        trig_scr[0] = jnp.cos(row_ang)
        trig_scr[1] = jnp.sin(row_ang)

    base_ang = jnp.broadcast_to((pos0 + i * ts).astype(F32) * rope_ref[0:1, :], (8, LANES))
    cos_a, sin_a = jnp.cos(base_ang)[0:1, :], jnp.sin(base_ang)[0:1, :]
    cos = cos_a * trig_scr[0] - sin_a * trig_scr[1]
    sin = (sin_a * trig_scr[0] + cos_a * trig_scr[1]) * rope_ref[1:2, :]
    lane = lax.broadcasted_iota(I32, (ts, LANES), 1)
    first_half = (lane & HALF) == 0
    low64 = lane < ROPE

    def rope(v):
        partner = jnp.where(first_half, pltpu.roll(v, LANES - HALF, 1), pltpu.roll(v, HALF, 1))
        return v * cos + partner * sin

    qa = seg("qa")
    qn = (qa * _rms(qa, Q_LORA) * gqa_ref[...]).astype(BF16)
    qf = jnp.dot(qn, wuq_ref[...], preferred_element_type=F32)
    rope_blocks = (rope(qf[:, 4 * NOPE:4 * NOPE + LANES]), rope(qf[:, 4 * NOPE + LANES:4 * NOPE + 2 * LANES]))
    for hd in range(HEADS):
        nope = qf[:, NOPE * hd:NOPE * (hd + 1)]
        blk = rope_blocks[hd // 2]
        if hd % 2:
            blk = pltpu.roll(blk, ROPE, 1)
        blk = jnp.where(low64, blk, 0.0)
        ss = jnp.sum(nope * nope, axis=-1, keepdims=True) + jnp.sum(blk * blk, axis=-1, keepdims=True)
        scl = lax.rsqrt(ss * (1.0 / QK) + EPS) * (QK ** -0.5 * LOG2_E)
        q_ref[hd, :, 0:NOPE] = (nope * scl * gqk_ref[0:1, :]).astype(BF16)
        q_ref[hd, :, NOPE:QK_PAD] = (blk * scl * gqk_ref[1:2, :]).astype(BF16)

    kva = seg("kva")
    lat_ref[...] = kva * _rms(kva, KV_LORA) * gkv_ref[...]
    kr_ref[...] = rope(seg("kr"))[:, 0:ROPE]

    gq_o[...] = seg("gq") * (GLA_DK ** -0.5)
    gk_o[...] = seg("gk")
    gv_o[...] = seg("gv").astype(BF16)
    gr_o[...] = seg("gr")
    z = jnp.dot(seg("glr").astype(BF16), wg2_ref[...], preferred_element_type=F32) + bg2_ref[...]
    gl_o[...] = (jnp.minimum(z, 0.0) - jnp.log1p(jnp.exp(-jnp.abs(z)))) * (1.0 / GATE_NORM)


def _proj(x, shift, scale, pos0, wts):
    b, s, d = x.shape
    ts = min(s, 512)
    row = lambda a: pl.BlockSpec(a.shape, lambda bi, i: (0,) * a.ndim)
    tok = lambda w: pl.BlockSpec((None, ts, w), lambda bi, i: (bi, i, 0))
    mod = pl.BlockSpec((None, 1, d), lambda bi, i: (bi, 0, 0))
    small = [wts["g_mix"], wts["w1"], wts["g_qa"], wts["w_uq"], wts["g_kv"], wts["g_qk_q"], wts["rope"],
             wts["w_g2"], wts["b_g2"]]
    out_shape = (
        jax.ShapeDtypeStruct((b, HEADS, s, QK_PAD), BF16),
        jax.ShapeDtypeStruct((b, s, KV_LORA), F32),
        jax.ShapeDtypeStruct((b, s, ROPE), F32),
        jax.ShapeDtypeStruct((b, s, HEADS * GLA_DK), F32),
        jax.ShapeDtypeStruct((b, s, HEADS * GLA_DK), F32),
        jax.ShapeDtypeStruct((b, s, HEADS * GLA_DV), BF16),
        jax.ShapeDtypeStruct((b, s, HEADS * GLA_DK), F32),
        jax.ShapeDtypeStruct((b, s, HEADS * GLA_DV), F32),
    )
    out_specs = (
        pl.BlockSpec((None, HEADS, ts, QK_PAD), lambda bi, i: (bi, 0, i, 0)),
        tok(KV_LORA), tok(ROPE), tok(HEADS * GLA_DK), tok(HEADS * GLA_DK), tok(HEADS * GLA_DV),
        tok(HEADS * GLA_DK), tok(HEADS * GLA_DV),
    )
    return pl.pallas_call(
        functools.partial(_proj_body, pos0, ts),
        grid=(b, s // ts),
        in_specs=[tok(d), mod, mod] + [row(a) for a in small],
        out_specs=out_specs,
        out_shape=out_shape,
        scratch_shapes=[pltpu.VMEM((2, ts, LANES), F32)],
        compiler_params=_cparams(("arbitrary", "arbitrary"), VMEM_BIG),
        name="proj",
    )(x, shift, scale, *small)


def _kv_body(v_transposed, lat_ref, kr_ref, wk_ref, wv_ref, gk_ref, k_ref, v_ref):
    lat = lat_ref[...].astype(BF16)
    kn_all = jnp.dot(lat, wk_ref[...], preferred_element_type=F32)
    kr = kr_ref[...]
    kr_ss = jnp.sum(kr * kr, axis=-1, keepdims=True)
    for hd in range(HEADS):
        kn = kn_all[:, NOPE * hd:NOPE * (hd + 1)]
        scl = lax.rsqrt((jnp.sum(kn * kn, axis=-1, keepdims=True) + kr_ss) * (1.0 / QK) + EPS)
        k_ref[hd, :, 0:NOPE] = (kn * scl * gk_ref[0:1, :]).astype(BF16)
        k_ref[hd, :, NOPE:QK] = (kr * scl * gk_ref[1:2, 0:ROPE]).astype(BF16)
        k_ref[hd, :, QK:QK_PAD] = jnp.zeros((kr.shape[0], QK_PAD - QK), BF16)
    if v_transposed:
        v_t = _nt(wv_ref[...], lat)
        for hd in range(HEADS):
            v_ref[hd] = v_t[V_DIM * hd:V_DIM * (hd + 1), :].astype(BF16)
    else:
        v_all = _nt(lat, wv_ref[...])
        for hd in range(HEADS):
            v_ref[hd] = v_all[:, V_DIM * hd:V_DIM * (hd + 1)].astype(BF16)


def _kv(lat, kr, w_uk, w_uv_t, g_qk_k, v_transposed):
    b, s, _ = lat.shape
    ts = min(s, ATTN_TILE)
    if v_transposed:
        v_spec = pl.BlockSpec((None, HEADS, None, V_DIM, ts), lambda bi, i: (bi, 0, i, 0, 0))
        v_shape = jax.ShapeDtypeStruct((b, HEADS, s // ts, V_DIM, ts), BF16)
    else:
        v_spec = pl.BlockSpec((None, HEADS, ts, V_DIM), lambda bi, i: (bi, 0, i, 0))
        v_shape = jax.ShapeDtypeStruct((b, HEADS, s, V_DIM), BF16)
    return pl.pallas_call(
        functools.partial(_kv_body, v_transposed),
        grid=(b, s // ts),
        in_specs=[pl.BlockSpec((None, ts, KV_LORA), lambda bi, i: (bi, i, 0)),
                  pl.BlockSpec((None, ts, ROPE), lambda bi, i: (bi, i, 0)),
                  pl.BlockSpec(w_uk.shape, lambda bi, i: (0, 0)),
                  pl.BlockSpec(w_uv_t.shape, lambda bi, i: (0, 0)),
                  pl.BlockSpec(g_qk_k.shape, lambda bi, i: (0, 0))],
        out_specs=(pl.BlockSpec((None, HEADS, ts, QK_PAD), lambda bi, i: (bi, 0, i, 0)), v_spec),
        out_shape=(jax.ShapeDtypeStruct((b, HEADS, s, QK_PAD), BF16), v_shape),
        compiler_params=_cparams(("arbitrary", "arbitrary")),
        name="kv",
    )(lat, kr, w_uk, w_uv_t, g_qk_k)


def _softmax_step(carry, s, vt):
    m, l, acc = carry
    m_new = jnp.maximum(m, jnp.max(s, axis=-1, keepdims=True))
    alpha = jnp.exp2(m - m_new)
    p = jnp.exp2(s - m_new)
    l = alpha * l + jnp.sum(p, axis=-1, keepdims=True)
    acc = alpha * acc + jnp.dot(p.astype(BF16), vt, preferred_element_type=F32)
    return m_new, l, acc


def _chunk_mask(tq, tk, q0, k0):
    qc = (q0 + lax.broadcasted_iota(I32, (tq, tk), 0)) // CHUNK
    kc = (k0 + lax.broadcasted_iota(I32, (tq, tk), 1)) // CHUNK
    return kc <= qc


def _attn_prompt_body(t, q_ref, k_ref, vt_ref, o_ref, s_a, s_b):
    i = pl.program_id(2)
    q = q_ref[...]

    def scores(j, buf):
        buf[...] = _nt(k_ref[pl.ds(pl.multiple_of(j * t, t), t), :], q)

    def consume(j, buf, carry, masked=False):
        m, l, acc = carry
        s = buf[...]
        if masked:
            visible = (lax.broadcasted_iota(I32, (t, t), 0) // CHUNK) <= (lax.broadcasted_iota(I32, (t, t), 1) // CHUNK)
            s = jnp.where(visible, s, NEG)
        m_new = jnp.maximum(m, jnp.max(s, axis=0, keepdims=True))
        alpha = jnp.exp2(m - m_new)
        p = jnp.exp2(s - m_new)
        l = alpha * l + jnp.sum(p, axis=0, keepdims=True)
        acc = alpha * acc + jnp.dot(vt_ref[j], p.astype(BF16), preferred_element_type=F32)
        return m_new, l, acc

    def pair(pp, carry):
        j = 2 * pp
        scores(j + 1, s_b)
        carry = consume(j, s_a, carry)
        scores(j + 2, s_a)
        return consume(j + 1, s_b, carry)

    def odd_tail(carry):
        scores(i, s_b)
        carry = consume(i - 1, s_a, carry)
        return consume(i, s_b, carry, masked=True)

    scores(0, s_a)
    carry = (jnp.full((1, t), NEG, F32), jnp.zeros((1, t), F32), jnp.zeros((V_DIM, t), F32))
    carry = lax.fori_loop(0, i // 2, pair, carry)
    _, l, acc = lax.cond(i % 2 == 1, odd_tail, lambda c: consume(i, s_a, c, masked=True), carry)
    o_ref[...] = (acc / l).T.astype(BF16)


def _attn_prompt(q, k, v_t):
    b, _, s, _ = q.shape
    t = v_t.shape[-1]
    return pl.pallas_call(
        functools.partial(_attn_prompt_body, t),
        grid=(b, HEADS, s // t),
        in_specs=[pl.BlockSpec((None, None, t, QK_PAD), lambda bi, h, i: (bi, h, i, 0)),
                  pl.BlockSpec((None, None, s, QK_PAD), lambda bi, h, i: (bi, h, 0, 0)),
                  pl.BlockSpec((None, None, s // t, V_DIM, t), lambda bi, h, i: (bi, h, 0, 0, 0))],
        out_specs=pl.BlockSpec((None, t, V_DIM), lambda bi, h, i: (bi, i, h)),
        out_shape=jax.ShapeDtypeStruct((b, s, HEADS * V_DIM), BF16),
        scratch_shapes=[pltpu.VMEM((t, t), F32), pltpu.VMEM((t, t), F32)],
        compiler_params=_cparams(("arbitrary", "arbitrary", "arbitrary"), VMEM_BIG),
        name="attn_prompt",
    )(q, k, v_t)


def _attn_sample_body(past, sq, q_ref, kp_ref, vp_ref, kn_ref, vn_ref, o_ref):
    q = q_ref[...]
    init = (jnp.full((sq, 1), NEG, F32), jnp.zeros((sq, 1), F32), jnp.zeros((sq, V_DIM), F32))
    carry = _softmax_step(init, _nt(q, kp_ref[...]), vp_ref[...])
    s = jnp.where(_chunk_mask(sq, sq, past, past), _nt(q, kn_ref[...]), NEG)
    _, l, acc = _softmax_step(carry, s, vn_ref[...])
    o_ref[...] = (acc / l).astype(BF16)


def _attn_sample(q, kp, vp, kn, vn):
    b, _, sq, _ = q.shape
    past = kp.shape[2]
    head = lambda n, w: pl.BlockSpec((None, None, n, w), lambda bi, h: (bi, h, 0, 0))
    return pl.pallas_call(
        functools.partial(_attn_sample_body, past, sq),
        grid=(b, HEADS),
        in_specs=[head(sq, QK_PAD), head(past, QK_PAD), head(past, V_DIM), head(sq, QK_PAD), head(sq, V_DIM)],
        out_specs=pl.BlockSpec((None, sq, V_DIM), lambda bi, h: (bi, 0, h)),
        out_shape=jax.ShapeDtypeStruct((b, sq, HEADS * V_DIM), BF16),
        compiler_params=_cparams(("arbitrary", "arbitrary")),
        name="attn_sample",
    )(q, kp, vp, kn, vn)


def _gla_masks(c):
    idx = np.arange(c)
    le = idx[None, :] <= idx[:, None]
    gt = idx[None, :] > idx[:, None]
    return np.concatenate([le, gt], axis=0).astype(np.float32), int(np.log2(c))


def _level_exponents(b, g, c, level):
    n = c >> level
    row = lax.broadcasted_iota(I32, (c, 1), 0)
    if n >= 8:
        split = b.reshape(c // n, n, LANES)[:, n // 2 - 1:n // 2, :]
        split = jnp.broadcast_to(split, (c // n, n, LANES)).reshape(c, LANES)
        return jnp.where((row & (n // 2)) != 0, b - split, split - b)
    g_prev = pltpu.roll(g, 1, 0)
    g_next = pltpu.roll(g, c - 1, 0)
    if n == 4:
        r = row & 3
        return jnp.where(r == 0, g_next, jnp.where(r == 1, 0.0, jnp.where(r == 2, g, g + g_prev)))
    assert n == 2
    return jnp.where((row & 1) != 0, g, 0.0)


def _gla_body(c, n_chunks, levels, mall_ref, q_ref, k_ref, g_ref, v_ref, r_ref, s0_ref, gout_ref,
              o_ref, sfin_ref, st_scr):
    it = pl.program_id(1)

    @pl.when(it == 0)
    def _():
        st_scr[...] = s0_ref[...]

    lane = lax.broadcasted_iota(I32, (c, LANES), 1)
    head_lanes = (lane < GLA_DK, lane >= GLA_DK)
    st_lane_lo = lax.broadcasted_iota(I32, (GLA_DV, LANES), 1) < GLA_DK
    row = lax.broadcasted_iota(I32, (c, 1), 0)
    ri = lax.broadcasted_iota(I32, (c, c), 0)
    ci = lax.broadcasted_iota(I32, (c, c), 1)
    mall = mall_ref[...]

    for ch in range(n_chunks):
        rows = slice(ch * c, (ch + 1) * c)
        for p in range(HEADS // 2):
            ls = slice(LANES * p, LANES * (p + 1))
            g = g_ref[rows, ls]
            q = q_ref[rows, ls]
            k = k_ref[rows, ls]
            g_hi = g.astype(BF16)
            g_lo = (g - g_hi.astype(F32)).astype(BF16)
            e2 = jnp.dot(mall, jnp.concatenate([g_hi, g_lo], axis=1), preferred_element_type=F32)
            e = e2[:, 0:LANES] + e2[:, LANES:2 * LANES]
            b = e[0:c]
            eb = jnp.exp(b)
            qb = q * eb
            kd = (k * jnp.exp(e[c:2 * c])).astype(BF16)
            d_last = eb[c - 1:c, :]
            st = st_scr[p]
            st_b = st.astype(BF16)
            qs, ks = [q], [k.astype(BF16)]
            for l in range(levels):
                bottom = (row & (c >> (l + 1))) != 0
                decay = jnp.exp(_level_exponents(b, g, c, l))
                qs.append(jnp.where(bottom, q * decay, 0.0))
                ks.append(jnp.where(bottom, 0.0, k * decay).astype(BF16))
            upd = []
            for hh in range(2):
                hd = 2 * p + hh
                sel = head_lanes[hh]
                a = jnp.where(ri == ci, _nt(jnp.where(sel, qs[0], 0.0).astype(BF16), ks[0]), 0.0)
                for l in range(levels):
                    pr = _nt(jnp.where(sel, qs[l + 1], 0.0).astype(BF16), ks[l + 1])
                    if l > 0:
                        pr = jnp.where((ri ^ ci) < (c >> l), pr, 0.0)
                    a = a + pr
                vh = v_ref[rows, GLA_DV * hd:GLA_DV * (hd + 1)]
                o = jnp.dot(a.astype(BF16), vh, preferred_element_type=F32)
                o = o + _nt(jnp.where(sel, qb, 0.0).astype(BF16), st_b)
                on = o * _rms(o, GLA_DV) * gout_ref[...]
                r = r_ref[rows, GLA_DV * hd:GLA_DV * (hd + 1)]
                o_ref[rows, GLA_DV * hd:GLA_DV * (hd + 1)] = (on * (r * jax.nn.sigmoid(r))).astype(BF16)
                upd.append(lax.dot_general(vh, kd, (((0,), (0,)), ((), ())), preferred_element_type=F32))
            st_scr[p] = st * d_last + jnp.where(st_lane_lo, upd[0], upd[1])

    @pl.when(it == pl.num_programs(1) - 1)
    def _():
        sfin_ref[...] = st_scr[...]


def _gla(gq, gk, gl, gv, gr, s0, g_out):
    b, s, _ = gq.shape
    c = min(CHUNK, s)
    tile = min(s, 4 * c)
    masks, levels = _gla_masks(c)
    mall = jnp.asarray(masks, BF16)
    tok = lambda w: pl.BlockSpec((None, tile, w), lambda bi, i: (bi, i, 0))
    st_spec = pl.BlockSpec((None, HEADS // 2, GLA_DV, LANES), lambda bi, i: (bi, 0, 0, 0))
    return pl.pallas_call(
        functools.partial(_gla_body, c, tile // c, levels),
        grid=(b, s // tile),
        in_specs=[pl.BlockSpec(mall.shape, lambda bi, i: (0, 0)),
                  tok(HEADS * GLA_DK), tok(HEADS * GLA_DK), tok(HEADS * GLA_DK), tok(HEADS * GLA_DV),
                  tok(HEADS * GLA_DV), st_spec, pl.BlockSpec(g_out.shape, lambda bi, i: (0, 0))],
        out_specs=(tok(HEADS * GLA_DV), st_spec),
        out_shape=(jax.ShapeDtypeStruct((b, s, HEADS * GLA_DV), BF16),
                   jax.ShapeDtypeStruct((b, HEADS // 2, GLA_DV, LANES), F32)),
        scratch_shapes=[pltpu.VMEM((HEADS // 2, GLA_DV, LANES), F32)],
        compiler_params=_cparams(("arbitrary", "arbitrary")),
        name="gla",
    )(mall, gq, gk, gl, gv, gr, s0, g_out)


def _state_to_pairs(s):
    b = s.shape[0]
    s = s.reshape(b, HEADS // 2, 2, GLA_DK, GLA_DV)
    return jnp.transpose(s, (0, 1, 4, 2, 3)).reshape(b, HEADS // 2, GLA_DV, 2 * GLA_DK)


def _state_from_pairs(s):
    b = s.shape[0]
    s = s.reshape(b, HEADS // 2, GLA_DV, 2, GLA_DK)
    return jnp.transpose(s, (0, 1, 3, 4, 2)).reshape(b, HEADS, GLA_DK, GLA_DV)


def _post_body(x_ref, om_ref, og_ref, gt_ref, sc_ref, sh_ref, wo_ref, gffn_ref, wr_ref, br_ref,
               x2_ref, h_ref, idx_ref, wt_ref):
    half = om_ref.shape[-1]
    mix = (jnp.dot(om_ref[...], wo_ref[0:half, :], preferred_element_type=F32)
           + jnp.dot(og_ref[...], wo_ref[half:2 * half, :], preferred_element_type=F32))
    x2 = x_ref[...] + gt_ref[...] * mix
    x2_ref[...] = x2
    d = x2.shape[-1]
    h = (x2 * _rms(x2, d) * gffn_ref[...]) * (1.0 + sc_ref[...]) + sh_ref[...]
    h_hi = h.astype(BF16)
    h_ref[...] = h_hi
    h_lo = (h - h_hi.astype(F32)).astype(BF16)
    logits = _nt(wr_ref[0], h_hi) + _nt(wr_ref[0], h_lo) + _nt(wr_ref[1], h_hi) + br_ref[...]
    n_exp, tm = logits.shape
    eid = lax.broadcasted_iota(I32, (n_exp, tm), 0)
    vals, tops, ids = logits, [], []
    for _ in range(TOP_K):
        m = jnp.max(vals, axis=0, keepdims=True)
        sel = jnp.min(jnp.where(vals == m, eid, n_exp), axis=0, keepdims=True)
        tops.append(m)
        ids.append(sel)
        vals = jnp.where(eid == sel, -jnp.inf, vals)
    es = [jnp.exp(t - tops[0]) for t in tops]
    tot = es[0] + es[1] + es[2] + es[3]
    idx_ref[...] = jnp.concatenate(ids, axis=0)
    wt_ref[...] = jnp.concatenate([e / tot for e in es], axis=0)


def _post(x, om, og, gate, scale, shift, w_o, g_ffn, w_r2, b_r):
    t, d = x.shape
    tm = min(t, 512)
    per_tok = gate.shape[0] == t
    mod = pl.BlockSpec((tm, d), lambda i: (i, 0)) if per_tok else pl.BlockSpec((1, d), lambda i: (0, 0))
    tok = lambda w: pl.BlockSpec((tm, w), lambda i: (i, 0))
    full = lambda a: pl.BlockSpec(a.shape, lambda i: (0,) * a.ndim)
    return pl.pallas_call(
        _post_body,
        grid=(t // tm,),
        in_specs=[tok(d), tok(om.shape[1]), tok(og.shape[1]), mod, mod, mod, full(w_o), full(g_ffn), full(w_r2),
                  full(b_r)],
        out_specs=(tok(d), tok(d),
                   pl.BlockSpec((TOP_K, tm), lambda i: (0, i)), pl.BlockSpec((TOP_K, tm), lambda i: (0, i))),
        out_shape=(jax.ShapeDtypeStruct((t, d), F32), jax.ShapeDtypeStruct((t, d), BF16),
                   jax.ShapeDtypeStruct((TOP_K, t), I32), jax.ShapeDtypeStruct((TOP_K, t), F32)),
        compiler_params=_cparams(("arbitrary",), 40 * 1024 * 1024),
        name="post",
    )(x, om, og, gate, scale, shift, w_o, g_ffn, w_r2, b_r)


def _rank_body(idx_ref, rank_ref, cnt_ref):
    idx = idx_ref[...]
    tg = idx.shape[1]
    eid = lax.broadcasted_iota(I32, (N_EXPERTS, tg), 0)
    hits = [eid == idx[k:k + 1, :] for k in range(TOP_K)]
    member = jnp.zeros((N_EXPERTS, tg), F32)
    for hk in hits:
        member = member + jnp.where(hk, 1.0, 0.0)
    before = (lax.broadcasted_iota(I32, (tg, tg), 0) < lax.broadcasted_iota(I32, (tg, tg), 1))
    prefix = jnp.dot(member.astype(BF16), jnp.where(before, 1.0, 0.0).astype(BF16), preferred_element_type=F32)
    rank_ref[...] = jnp.concatenate(
        [jnp.sum(jnp.where(hk, prefix, 0.0), axis=0, keepdims=True) for hk in hits], axis=0).astype(I32)
    cnt_ref[...] = jnp.broadcast_to(jnp.sum(member, axis=1, keepdims=True), (N_EXPERTS, LANES)).astype(I32)


def _rank(idx):
    t = idx.shape[1]
    nt = t // ROUTE_TILE
    return pl.pallas_call(
        _rank_body,
        grid=(nt,),
        in_specs=[pl.BlockSpec((TOP_K, ROUTE_TILE), lambda i: (0, i))],
        out_specs=(pl.BlockSpec((TOP_K, ROUTE_TILE), lambda i: (0, i)),
                   pl.BlockSpec((None, N_EXPERTS, LANES), lambda i: (i, 0, 0))),
        out_shape=(jax.ShapeDtypeStruct((TOP_K, t), I32), jax.ShapeDtypeStruct((nt, N_EXPERTS, LANES), I32)),
        compiler_params=_cparams(("arbitrary",)),
        name="rank",
    )(idx)


def _route_tables(idx, lrank, cnt3):
    nt = cnt3.shape[0]
    t = idx.shape[1]
    cnt = cnt3[:, :, 0]
    run = _round_up(cnt, RUN_CHUNK)
    lo_end = jnp.cumsum(run, axis=1)
    lo = lo_end - run
    n_chunks = lo_end[:, -1] // RUN_CHUNK
    region = _round_up(jnp.sum(run, axis=0), EXPERT_ROWS)
    g_end = jnp.cumsum(region)
    run_dest = (g_end - region)[None, :] + jnp.cumsum(run, axis=0) - run
    max_rows = TOP_K * ROUTE_TILE + N_EXPERTS * (RUN_CHUNK - 1)
    n_tab = _round_up(max_rows, RUN_CHUNK) // RUN_CHUNK
    c_start = jnp.arange(n_tab, dtype=I32) * RUN_CHUNK
    e_of_c = jnp.minimum(jnp.sum(lo_end[:, None, :] <= c_start[None, :, None], axis=2), N_EXPERTS - 1)
    pick = e_of_c[:, :, None] == jnp.arange(N_EXPERTS, dtype=I32)[None, None, :]
    chunk_dest = jnp.sum(jnp.where(pick, (run_dest - lo)[:, None, :], 0), axis=2) + c_start[None, :]
    table = jnp.concatenate([chunk_dest, n_chunks[:, None]], axis=1).astype(I32).reshape(nt, 1, n_tab + 1)
    eid = jnp.arange(N_EXPERTS, dtype=I32)[:, None]
    lo_tok = jnp.repeat(lo.T, ROUTE_TILE, axis=1)
    lpos = jnp.stack([jnp.sum(jnp.where(idx[k][None, :] == eid, lo_tok, 0), axis=0) for k in range(TOP_K)])
    lpos = (lpos + lrank).astype(I32)
    n_blocks = _round_up(t * TOP_K + nt * N_EXPERTS * (RUN_CHUNK - 1), EXPERT_ROWS) // EXPERT_ROWS + N_EXPERTS
    b_start = jnp.arange(n_blocks, dtype=I32) * EXPERT_ROWS
    blk_e = jnp.minimum(jnp.sum(g_end[None, :] <= b_start[:, None], axis=1), N_EXPERTS - 1).astype(I32)
    n_valid = (g_end[-1:] // EXPERT_ROWS).astype(I32)
    tail = jnp.concatenate([jnp.where(region > 0, g_end - EXPERT_ROWS, -1), n_valid]).astype(I32)
    tail = tail.reshape(1, N_EXPERTS + 1)
    return dict(table=table, lpos=lpos, blk_e=blk_e, n_valid=n_valid, tail=tail, n_blocks=n_blocks)


def _chunk_copy(src, dst, sem):
    return pltpu.make_async_copy(src, dst, sem)


def _scatter_body(n_tab, nt_a, n_blocks, tab_ref, prv_ref, tail_ref, lpos_ref, ha_ref, hb_ref, xout_ref, sorted_scr,
                  zero_scr, sems, zero_sem):
    i = pl.program_id(0)
    slot = i % 2
    n_chunks = tab_ref[0, n_tab]
    tt = ha_ref.shape[0]

    @pl.when(i == 0)
    def _():
        zero_scr[...] = jnp.zeros_like(zero_scr)
        n_valid = tail_ref[0, N_EXPERTS]

        def block(start):
            return xout_ref.at[pl.ds(pl.multiple_of(start, EXPERT_ROWS), EXPERT_ROWS)]

        for e in range(N_EXPERTS):
            @pl.when(tail_ref[0, e] >= 0)
            def _():
                _chunk_copy(zero_scr, block(tail_ref[0, e]), zero_sem).start()

        def fill(b, carry):
            _chunk_copy(zero_scr, block(b * EXPERT_ROWS), zero_sem).start()
            return carry

        def fill_done(b, carry):
            _chunk_copy(zero_scr, block(0), zero_sem).wait()
            return carry

        lax.fori_loop(n_valid, n_blocks, fill, 0)
        for e in range(N_EXPERTS):
            @pl.when(tail_ref[0, e] >= 0)
            def _():
                _chunk_copy(zero_scr, block(0), zero_sem).wait()
        lax.fori_loop(n_valid, n_blocks, fill_done, 0)

    lp = lpos_ref[...]
    h = jnp.where(i < nt_a, ha_ref[...], hb_ref[...])

    def sort_block(rb, carry):
        r0 = pl.multiple_of(rb * SORT_ROWS, SORT_ROWS)
        rid = r0 + lax.broadcasted_iota(I32, (SORT_ROWS, tt), 0)
        onehot = jnp.zeros((SORT_ROWS, tt), F32)
        for k in range(TOP_K):
            onehot = onehot + jnp.where(lp[k:k + 1, :] == rid, 1.0, 0.0)
        sorted_scr[slot, pl.ds(r0, SORT_ROWS), :] = jnp.dot(onehot.astype(BF16), h,
                                                            preferred_element_type=F32).astype(BF16)
        return carry

    lax.fori_loop(0, (n_chunks * RUN_CHUNK + SORT_ROWS - 1) // SORT_ROWS, sort_block, 0)

    def issue(c, carry):
        src = sorted_scr.at[slot, pl.ds(pl.multiple_of(c * RUN_CHUNK, RUN_CHUNK), RUN_CHUNK)]
        dst = xout_ref.at[pl.ds(pl.multiple_of(tab_ref[0, c], RUN_CHUNK), RUN_CHUNK)]
        _chunk_copy(src, dst, sems.at[slot]).start()
        return carry

    def drain(sl):
        def body(c, carry):
            _chunk_copy(sorted_scr.at[sl, pl.ds(0, RUN_CHUNK)], xout_ref.at[pl.ds(0, RUN_CHUNK)], sems.at[sl]).wait()
            return carry
        return body

    lax.fori_loop(0, n_chunks, issue, 0)

    @pl.when(i > 0)
    def _():
        lax.fori_loop(0, prv_ref[0, n_tab], drain(1 - slot), 0)

    @pl.when(i == pl.num_programs(0) - 1)
    def _():
        lax.fori_loop(0, n_chunks, drain(slot), 0)


def _scatter(table, tail, lpos, h_a, h_b, n_blocks):
    d = h_a.shape[1]
    nt_a, nt_b = h_a.shape[0] // ROUTE_TILE, h_b.shape[0] // ROUTE_TILE
    n_tab = table.shape[2] - 1
    sort_cap = _round_up(n_tab * RUN_CHUNK, SORT_ROWS)
    tab_spec = lambda f: pl.BlockSpec((None, 1, n_tab + 1), lambda i: (f(i), 0, 0), memory_space=pltpu.SMEM)
    return pl.pallas_call(
        functools.partial(_scatter_body, n_tab, nt_a, n_blocks),
        grid=(nt_a + nt_b,),
        in_specs=[tab_spec(lambda i: i), tab_spec(lambda i: jnp.maximum(i - 1, 0)),
                  pl.BlockSpec((1, N_EXPERTS + 1), lambda i: (0, 0), memory_space=pltpu.SMEM),
                  pl.BlockSpec((TOP_K, ROUTE_TILE), lambda i: (0, i)),
                  pl.BlockSpec((ROUTE_TILE, d), lambda i: (jnp.minimum(i, nt_a - 1), 0)),
                  pl.BlockSpec((ROUTE_TILE, d), lambda i: (jnp.maximum(i - nt_a, 0), 0))],
        out_specs=pl.BlockSpec(memory_space=pl.ANY),
        out_shape=jax.ShapeDtypeStruct((n_blocks * EXPERT_ROWS, d), BF16),
        scratch_shapes=[pltpu.VMEM((2, sort_cap, d), BF16), pltpu.VMEM((EXPERT_ROWS, d), BF16),
                        pltpu.SemaphoreType.DMA((2,)), pltpu.SemaphoreType.DMA(())],
        compiler_params=_cparams(("arbitrary",), 48 * 1024 * 1024),
        name="scatter",
    )(table, table, tail, lpos, h_a, h_b)


def _experts_body(be_ref, nv_ref, x_ref, wgu_ref, bgu_ref, wd_ref, bd_ref, y_ref, wgu_s, wd_s):
    b = pl.program_id(0)
    e = be_ref[b]
    prev = be_ref[jnp.maximum(b - 1, 0)]
    valid = b < nv_ref[0]
    d_ff = wd_ref.shape[0]

    @pl.when(valid & ((b == 0) | (e != prev)))
    def _():
        wgu_s[...] = wgu_ref[...].astype(BF16)
        wd_s[...] = wd_ref[...].astype(BF16)

    @pl.when(valid)
    def _():
        gu = jnp.dot(x_ref[...], wgu_s[...], preferred_element_type=F32) + bgu_ref[...]
        gate = jnp.minimum(gu[:, 0:d_ff], SWIGLU_LIMIT)
        up = jnp.clip(gu[:, d_ff:2 * d_ff], -SWIGLU_LIMIT, SWIGLU_LIMIT)
        act = ((up + 1.0) * (gate * jax.nn.sigmoid(gate * SWIGLU_ALPHA))).astype(BF16)
        y_ref[...] = (jnp.dot(act, wd_s[...], preferred_element_type=F32) + bd_ref[...]).astype(BF16)

    @pl.when(jnp.logical_not(valid))
    def _():
        y_ref[...] = jnp.zeros_like(y_ref)


def _experts(blk_e, n_valid, xpad, w_gu, b_gu, w_down, b_down):
    m, d = xpad.shape
    nb = m // EXPERT_ROWS
    n_exp, _, f2 = w_gu.shape
    d_ff = w_down.shape[1]
    last = lambda b, be, nv: jnp.minimum(b, nv[0] - 1)
    grid_spec = pltpu.PrefetchScalarGridSpec(
        num_scalar_prefetch=2,
        grid=(nb,),
        in_specs=[pl.BlockSpec((EXPERT_ROWS, d), lambda b, be, nv: (last(b, be, nv), 0)),
                  pl.BlockSpec((None, d, f2), lambda b, be, nv: (be[last(b, be, nv)], 0, 0)),
                  pl.BlockSpec((None, 1, f2), lambda b, be, nv: (be[last(b, be, nv)], 0, 0)),
                  pl.BlockSpec((None, d_ff, d), lambda b, be, nv: (be[last(b, be, nv)], 0, 0)),
                  pl.BlockSpec((None, 1, d), lambda b, be, nv: (be[last(b, be, nv)], 0, 0))],
        out_specs=pl.BlockSpec((EXPERT_ROWS, d), lambda b, be, nv: (b, 0)),
        scratch_shapes=[pltpu.VMEM((d, f2), BF16), pltpu.VMEM((d_ff, d), BF16)],
    )
    return pl.pallas_call(
        _experts_body,
        grid_spec=grid_spec,
        out_shape=jax.ShapeDtypeStruct((m, d), BF16),
        compiler_params=_cparams(("arbitrary",), VMEM_BIG),
        name="experts",
    )(blk_e, n_valid, xpad, w_gu, b_gu.reshape(n_exp, 1, f2), w_down, b_down.reshape(n_exp, 1, d))


def _combine_body(n_tab, tab_ref, nxt_ref, lpos_ref, wt_ref, x2_ref, gt_ref, y_ref, o_ref, ysort_scr, sems):
    i = pl.program_id(0)
    slot = i % 2
    n_chunks = tab_ref[0, n_tab]
    tt = x2_ref.shape[0]

    def fetch(tab, sl):
        def issue(c, carry):
            src = y_ref.at[pl.ds(pl.multiple_of(tab[0, c], RUN_CHUNK), RUN_CHUNK)]
            dst = ysort_scr.at[sl, pl.ds(pl.multiple_of(c * RUN_CHUNK, RUN_CHUNK), RUN_CHUNK)]
            _chunk_copy(src, dst, sems.at[sl]).start()
            return carry

        lax.fori_loop(0, tab[0, n_tab], issue, 0)

    @pl.when(i == 0)
    def _():
        ysort_scr[...] = jnp.zeros_like(ysort_scr)
        fetch(tab_ref, 0)

    @pl.when(i + 1 < pl.num_programs(0))
    def _():
        fetch(nxt_ref, 1 - slot)

    def drain(c, carry):
        _chunk_copy(y_ref.at[pl.ds(0, RUN_CHUNK)], ysort_scr.at[slot, pl.ds(0, RUN_CHUNK)], sems.at[slot]).wait()
        return carry

    lax.fori_loop(0, n_chunks, drain, 0)

    lp = lpos_ref[...]
    w = wt_ref[...]
    o_ref[...] = x2_ref[...]
    gt = gt_ref[...]

    def gather_block(rb, carry):
        r0 = pl.multiple_of(rb * SORT_ROWS, SORT_ROWS)
        rid = r0 + lax.broadcasted_iota(I32, (tt, SORT_ROWS), 1)
        pw = jnp.zeros((tt, SORT_ROWS), F32)
        for k in range(TOP_K):
            pw = pw + jnp.where(lp[:, k:k + 1] == rid, w[:, k:k + 1], 0.0)
        ys = ysort_scr[slot, pl.ds(r0, SORT_ROWS), :]
        o_ref[...] = o_ref[...] + gt * jnp.dot(pw.astype(BF16), ys, preferred_element_type=F32)
        return carry

    lax.fori_loop(0, (n_chunks * RUN_CHUNK + SORT_ROWS - 1) // SORT_ROWS, gather_block, 0)


def _combine(table, lpos_t, wts, x2, gate, ypad):
    t, d = x2.shape
    nt = t // ROUTE_TILE
    n_tab = table.shape[2] - 1
    per_tok = gate.shape[0] == t
    mod = pl.BlockSpec((ROUTE_TILE, d), lambda i: (i, 0)) if per_tok else pl.BlockSpec((1, d), lambda i: (0, 0))
    sort_cap = _round_up(n_tab * RUN_CHUNK, SORT_ROWS)
    tab_spec = lambda f: pl.BlockSpec((None, 1, n_tab + 1), lambda i: (f(i), 0, 0), memory_space=pltpu.SMEM)
    return pl.pallas_call(
        functools.partial(_combine_body, n_tab),
        grid=(nt,),
        in_specs=[tab_spec(lambda i: i), tab_spec(lambda i: jnp.minimum(i + 1, nt - 1)),
                  pl.BlockSpec((ROUTE_TILE, TOP_K), lambda i: (i, 0)),
                  pl.BlockSpec((ROUTE_TILE, TOP_K), lambda i: (i, 0)),
                  pl.BlockSpec((ROUTE_TILE, d), lambda i: (i, 0)),
                  mod,
                  pl.BlockSpec(memory_space=pl.ANY)],
        out_specs=pl.BlockSpec((ROUTE_TILE, d), lambda i: (i, 0)),
        out_shape=jax.ShapeDtypeStruct((t, d), F32),
        scratch_shapes=[pltpu.VMEM((2, sort_cap, d), BF16), pltpu.SemaphoreType.DMA((2,))],
        compiler_params=_cparams(("arbitrary",), 48 * 1024 * 1024),
        name="combine",
    )(table, table, lpos_t, wts, x2, gate, ypad)


def _prep_weights(g_norm_mix, w_in, g_q_a, w_uq, g_kv_a, w_ukv, g_qk_q, g_qk_k, w_g2, b_g2, g_gla_out, w_o,
                  g_norm_ffn, w_router, b_router):
    d = w_in.shape[0]
    o_qa, o_kva, o_kr = 0, Q_LORA, Q_LORA + KV_LORA
    o_gq = o_kr + ROPE
    o_gk = o_gq + HEADS * GLA_DK
    o_gv = o_gk + HEADS * GLA_DK
    o_glr = o_gv + HEADS * GLA_DV
    o_gr = o_glr + GATE_RANK
    kr_cols = w_in[:, o_kr:o_kr + ROPE]
    w1 = jnp.concatenate([
        w_in[:, o_qa:o_kva], w_in[:, o_kva:o_kr], kr_cols, kr_cols, w_in[:, o_gq:o_gk], w_in[:, o_gk:o_gv],
        w_in[:, o_gv:o_glr], w_in[:, o_gr:o_gr + HEADS * GLA_DV], w_in[:, o_glr:o_gr],
        jnp.zeros((d, LANES - GATE_RANK), w_in.dtype)], axis=1).astype(BF16)
    assert w1.shape[1] == _W1_COLS
    wq = w_uq.reshape(Q_LORA, HEADS, QK)
    wuq = jnp.concatenate([wq[:, :, 0:NOPE].reshape(Q_LORA, HEADS * NOPE),
                           wq[:, :, NOPE:QK].reshape(Q_LORA, HEADS * ROPE)], axis=1).astype(BF16)
    wkv = w_ukv.reshape(KV_LORA, HEADS, NOPE + V_DIM)
    wuk = wkv[:, :, 0:NOPE].reshape(KV_LORA, HEADS * NOPE).astype(BF16)
    wuv_t = wkv[:, :, NOPE:].reshape(KV_LORA, HEADS * V_DIM).T.astype(BF16)
    pad_rope = lambda g: jnp.stack([g[0:NOPE], jnp.concatenate([g[NOPE:QK], jnp.zeros((QK_PAD - QK,), g.dtype)])])
    inv = ROPE_THETA ** (-jnp.arange(HALF, dtype=F32) / HALF)
    sign = jnp.concatenate([-jnp.ones((HALF,), F32), jnp.ones((HALF,), F32)])
    rope_tab = jnp.stack([jnp.tile(inv, LANES // HALF), jnp.tile(sign, LANES // ROPE)])
    wg2 = jnp.concatenate([w_g2, jnp.zeros((LANES - GATE_RANK, w_g2.shape[1]), w_g2.dtype)], axis=0).astype(BF16)
    wr_t = w_router.T
    wr_hi = wr_t.astype(BF16)
    wr_lo = (wr_t - wr_hi.astype(F32)).astype(BF16)
    return dict(
        g_mix=g_norm_mix.reshape(1, d), w1=w1, g_qa=g_q_a.reshape(1, -1), w_uq=wuq, g_kv=g_kv_a.reshape(1, -1),
        w_uk=wuk, w_uv_t=wuv_t, g_qk_q=pad_rope(g_qk_q), g_qk_k=pad_rope(g_qk_k), rope=rope_tab, w_g2=wg2,
        b_g2=b_g2.reshape(1, -1), g_out=g_gla_out.reshape(1, -1), w_o=w_o.astype(BF16),
        g_ffn=g_norm_ffn.reshape(1, d), w_r2=jnp.stack([wr_hi, wr_lo]), b_r=b_router.reshape(-1, 1))


def _mixer(x, mod, pos0, past_lat, past_kr, s0_pairs, wts):
    b, s, d = x.shape
    q, lat, kr, gq, gk, gv, gl, gr = _proj(x, mod[:, 0:1], mod[:, 1:2], pos0, wts)
    kv = functools.partial(_kv, w_uk=wts["w_uk"], w_uv_t=wts["w_uv_t"], g_qk_k=wts["g_qk_k"])
    if past_lat is None:
        k_new, vt_new = kv(lat, kr, v_transposed=True)
        o_mla = _attn_prompt(q, k_new, vt_new)
    else:
        k_new, v_new = kv(lat, kr, v_transposed=False)
        k_past, v_past = kv(past_lat, past_kr, v_transposed=False)
        o_mla = _attn_sample(q, k_past, v_past, k_new, v_new)
    o_gla, s_fin = _gla(gq, gk, gl, gv, gr, s0_pairs, wts["g_out"])
    t = b * s
    if b == 1:
        rows = lambda j: mod[0, j:j + 1]
    else:
        rows = lambda j: jnp.broadcast_to(mod[:, j:j + 1], (b, s, d)).reshape(t, d)
    x2, h2, idx, wt = _post(x.reshape(t, d), o_mla.reshape(t, -1), o_gla.reshape(t, -1), rows(2), rows(4), rows(3),
                            wts["w_o"], wts["g_ffn"], wts["w_r2"], wts["b_r"])
    return dict(x2=x2, h2=h2, idx=idx, wt=wt, gate_f=rows(5), lat=lat, kr=kr, s_fin=s_fin)


def kernel(x_prompt, x_sample, cache_mla_latent, cache_mla_krope, state_gla, c_prompt, c_sample, w_ada, b_ada, g_norm_mix, w_in, g_q_a, w_uq, g_kv_a, w_ukv, g_qk_q, g_qk_k, w_g2, b_g2, g_gla_out, w_o, g_norm_ffn, w_router, b_router, w_gu, b_gu, w_down, b_down):
    depth = w_ada.shape[0]
    assert depth == 1, "single-layer step"
    bp, sp, d = x_prompt.shape
    bs, ss, _ = x_sample.shape
    tp, tsm = bp * sp, bs * ss
    assert tp % ROUTE_TILE == 0 and tsm % ROUTE_TILE == 0, "token counts must be whole routing tiles"
    past = cache_mla_latent.shape[2]
    wts = _prep_weights(g_norm_mix[0], w_in[0], g_q_a[0], w_uq[0], g_kv_a[0], w_ukv[0], g_qk_q[0], g_qk_k[0],
                        w_g2[0], b_g2[0], g_gla_out[0], w_o[0], g_norm_ffn[0], w_router[0], b_router[0])

    mod = _ada(jnp.concatenate([c_prompt, c_sample], axis=0), w_ada[0], b_ada[0]).reshape(bp + bs, 6, d)
    zero_state = jnp.zeros((bp, HEADS // 2, GLA_DV, LANES), F32)
    pr = _mixer(x_prompt, mod[:bp], 0, None, None, zero_state, wts)
    sa = _mixer(x_sample, mod[bp:], past, cache_mla_latent[0], cache_mla_krope[0], _state_to_pairs(state_gla[0]),
                wts)

    idx = jnp.concatenate([pr["idx"], sa["idx"]], axis=1)
    lrank, cnt3 = _rank(idx)
    rt = _route_tables(idx, lrank, cnt3)
    ntp = tp // ROUTE_TILE
    lpos, table = rt["lpos"], rt["table"]
    xpad = _scatter(table, rt["tail"], lpos, pr["h2"], sa["h2"], rt["n_blocks"])
    ypad = _experts(rt["blk_e"], rt["n_valid"], xpad, w_gu[0], b_gu[0], w_down[0], b_down[0])
    lpos_t = lpos.T
    y_p = _combine(table[:ntp], lpos_t[:tp], pr["wt"].T, pr["x2"], pr["gate_f"], ypad).reshape(bp, sp, d)
    y_s = _combine(table[ntp:], lpos_t[tp:], sa["wt"].T, sa["x2"], sa["gate_f"], ypad).reshape(bs, ss, d)

    return (y_p, y_s,
            pr["lat"][None], pr["kr"][None], _state_from_pairs(pr["s_fin"])[None],
            sa["lat"][None], sa["kr"][None], _state_from_pairs(sa["s_fin"])[None])
```

```python
import functools

import numpy as np
import jax
import jax.numpy as jnp
from jax import lax
from jax.experimental import pallas as pl
from jax.experimental.pallas import tpu as pltpu

F32 = jnp.float32
BF16 = jnp.bfloat16
I32 = jnp.int32

CHUNK = 64
EPS = 1e-6
HEADS = 4
Q_LORA = 384
KV_LORA = 256
NOPE = 128
ROPE = 64
HALF = ROPE // 2
V_DIM = 128
QK = NOPE + ROPE
QK_PAD = 256
ROPE_THETA = 10000.0
GLA_DK = 64
GLA_DV = 128
GATE_RANK = 16
GATE_NORM = 16.0
N_EXPERTS = 32
TOP_K = 4
SWIGLU_LIMIT = 7.0
SWIGLU_ALPHA = 1.702
NEG = -1e30
LOG2_E = 1.4426950408889634

LANES = 128
BF16_ROWS = 16
ROUTE_TILE = 512
RUN_CHUNK = BF16_ROWS
SORT_ROWS = 512
EXPERT_ROWS = 256
ATTN_TILE = 1024
VMEM_BIG = 56 * 1024 * 1024


def _cparams(sem, vmem=None):
    return pltpu.CompilerParams(dimension_semantics=sem, vmem_limit_bytes=vmem)


def _nt(a, b):
    return lax.dot_general(a, b, (((1,), (1,)), ((), ())), preferred_element_type=F32)


def _rms(x, width):
    return lax.rsqrt(jnp.sum(x * x, axis=-1, keepdims=True) * (1.0 / width) + EPS)


def _round_up(x, m):
    return ((x + m - 1) // m) * m


def _ada_body(c_ref, w_ref, b_ref, o_ref):
    c = c_ref[...]
    s = (c * jax.nn.sigmoid(c)).astype(BF16)
    o_ref[...] = jnp.dot(s, w_ref[...].astype(BF16), preferred_element_type=F32) + b_ref[...]


def _ada(c, w_ada, b_ada):
    r, d = c.shape
    n = w_ada.shape[1]
    tn = 1536 if n % 1536 == 0 else n
    return pl.pallas_call(
        _ada_body,
        grid=(n // tn,),
        in_specs=[pl.BlockSpec((r, d), lambda j: (0, 0)),
                  pl.BlockSpec((d, tn), lambda j: (0, j)),
                  pl.BlockSpec((1, tn), lambda j: (0, j))],
        out_specs=pl.BlockSpec((r, tn), lambda j: (0, j)),
        out_shape=jax.ShapeDtypeStruct((r, n), F32),
        compiler_params=_cparams(("arbitrary",), 40 * 1024 * 1024),
        name="ada",
    )(c, w_ada, b_ada.reshape(1, n))


_SEG = dict(qa=(0, 384), kva=(384, 640), kr=(640, 768), gq=(768, 1024), gk=(1024, 1280),
            gv=(1280, 1792), gr=(1792, 2304), glr=(2304, 2432))
_W1_COLS = 2432


def _proj_body(pos0, ts, x_ref, sh_ref, sc_ref, gmix_ref, w1_ref, gqa_ref, wuq_ref, gkv_ref, gqk_ref,
               rope_ref, wg2_ref, bg2_ref,
               q_ref, lat_ref, kr_ref, gq_o, gk_o, gv_o, gl_o, gr_o, trig_scr):
    i = pl.program_id(1)
    x = x_ref[...]
    d = x.shape[-1]
    h = (x * _rms(x, d) * gmix_ref[...]) * (1.0 + sc_ref[...]) + sh_ref[...]
    hb = h.astype(BF16)

    def seg(name):
        a, b = _SEG[name]
        return jnp.dot(hb, w1_ref[:, a:b], preferred_element_type=F32)

    @pl.when((pl.program_id(0) == 0) & (i == 0))
    def _():
        row_ang = lax.broadcasted_iota(I32, (ts, LANES), 0).astype(F32) * rope_ref[0:1, :]
        trig_scr[0] = jnp.cos(row_ang)
        trig_scr[1] = jnp.sin(row_ang)

    base_ang = jnp.broadcast_to((pos0 + i * ts).astype(F32) * rope_ref[0:1, :], (8, LANES))
    cos_a, sin_a = jnp.cos(base_ang)[0:1, :], jnp.sin(base_ang)[0:1, :]
    cos = cos_a * trig_scr[0] - sin_a * trig_scr[1]
    sin = (sin_a * trig_scr[0] + cos_a * trig_scr[1]) * rope_ref[1:2, :]
    lane = lax.broadcasted_iota(I32, (ts, LANES), 1)
    first_half = (lane & HALF) == 0
    low64 = lane < ROPE

    def rope(v):
        partner = jnp.where(first_half, pltpu.roll(v, LANES - HALF, 1), pltpu.roll(v, HALF, 1))
        return v * cos + partner * sin

    qa = seg("qa")
    qn = (qa * _rms(qa, Q_LORA) * gqa_ref[...]).astype(BF16)
    qf = jnp.dot(qn, wuq_ref[...], preferred_element_type=F32)
    rope_blocks = (rope(qf[:, 4 * NOPE:4 * NOPE + LANES]), rope(qf[:, 4 * NOPE + LANES:4 * NOPE + 2 * LANES]))
    for hd in range(HEADS):
        nope = qf[:, NOPE * hd:NOPE * (hd + 1)]
        blk = rope_blocks[hd // 2]
        if hd % 2:
            blk = pltpu.roll(blk, ROPE, 1)
        blk = jnp.where(low64, blk, 0.0)
        ss = jnp.sum(nope * nope, axis=-1, keepdims=True) + jnp.sum(blk * blk, axis=-1, keepdims=True)
        scl = lax.rsqrt(ss * (1.0 / QK) + EPS) * (QK ** -0.5 * LOG2_E)
        q_ref[hd, :, 0:NOPE] = (nope * scl * gqk_ref[0:1, :]).astype(BF16)
        q_ref[hd, :, NOPE:QK_PAD] = (blk * scl * gqk_ref[1:2, :]).astype(BF16)

    kva = seg("kva")
    lat_ref[...] = kva * _rms(kva, KV_LORA) * gkv_ref[...]
    kr_ref[...] = rope(seg("kr"))[:, 0:ROPE]

    gq_o[...] = seg("gq") * (GLA_DK ** -0.5)
    gk_o[...] = seg("gk")
    gv_o[...] = seg("gv").astype(BF16)
    gr_o[...] = seg("gr")
    z = jnp.dot(seg("glr").astype(BF16), wg2_ref[...], preferred_element_type=F32) + bg2_ref[...]
    gl_o[...] = (jnp.minimum(z, 0.0) - jnp.log1p(jnp.exp(-jnp.abs(z)))) * (1.0 / GATE_NORM)


def _proj(x, shift, scale, pos0, wts):
    b, s, d = x.shape
    ts = min(s, 512)
    row = lambda a: pl.BlockSpec(a.shape, lambda bi, i: (0,) * a.ndim)
    tok = lambda w: pl.BlockSpec((None, ts, w), lambda bi, i: (bi, i, 0))
    mod = pl.BlockSpec((None, 1, d), lambda bi, i: (bi, 0, 0))
    small = [wts["g_mix"], wts["w1"], wts["g_qa"], wts["w_uq"], wts["g_kv"], wts["g_qk_q"], wts["rope"],
             wts["w_g2"], wts["b_g2"]]
    out_shape = (
        jax.ShapeDtypeStruct((b, HEADS, s, QK_PAD), BF16),
        jax.ShapeDtypeStruct((b, s, KV_LORA), F32),
        jax.ShapeDtypeStruct((b, s, ROPE), F32),
        jax.ShapeDtypeStruct((b, s, HEADS * GLA_DK), F32),
        jax.ShapeDtypeStruct((b, s, HEADS * GLA_DK), F32),
        jax.ShapeDtypeStruct((b, s, HEADS * GLA_DV), BF16),
        jax.ShapeDtypeStruct((b, s, HEADS * GLA_DK), F32),
        jax.ShapeDtypeStruct((b, s, HEADS * GLA_DV), F32),
    )
    out_specs = (
        pl.BlockSpec((None, HEADS, ts, QK_PAD), lambda bi, i: (bi, 0, i, 0)),
        tok(KV_LORA), tok(ROPE), tok(HEADS * GLA_DK), tok(HEADS * GLA_DK), tok(HEADS * GLA_DV),
        tok(HEADS * GLA_DK), tok(HEADS * GLA_DV),
    )
    return pl.pallas_call(
        functools.partial(_proj_body, pos0, ts),
        grid=(b, s // ts),
        in_specs=[tok(d), mod, mod] + [row(a) for a in small],
        out_specs=out_specs,
        out_shape=out_shape,
        scratch_shapes=[pltpu.VMEM((2, ts, LANES), F32)],
        compiler_params=_cparams(("arbitrary", "arbitrary"), VMEM_BIG),
        name="proj",
    )(x, shift, scale, *small)


def _key_rows(lat, kr, wk_ref, gk_ref, k_out):
    kn_all = jnp.dot(lat, wk_ref[...], preferred_element_type=F32)
    kr_ss = jnp.sum(kr * kr, axis=-1, keepdims=True)
    for hd in range(HEADS):
        kn = kn_all[:, NOPE * hd:NOPE * (hd + 1)]
        scl = lax.rsqrt((jnp.sum(kn * kn, axis=-1, keepdims=True) + kr_ss) * (1.0 / QK) + EPS)
        k_out[hd, :, 0:NOPE] = (kn * scl * gk_ref[0:1, :]).astype(BF16)
        k_out[hd, :, NOPE:QK] = (kr * scl * gk_ref[1:2, 0:ROPE]).astype(BF16)
        k_out[hd, :, QK:QK_PAD] = jnp.zeros((kr.shape[0], QK_PAD - QK), BF16)


def _kv_body(lat_ref, kr_ref, wk_ref, wv_ref, gk_ref, k_ref, v_ref):
    lat = lat_ref[...].astype(BF16)
    _key_rows(lat, kr_ref[...], wk_ref, gk_ref, k_ref)
    v_t = _nt(wv_ref[...], lat)
    for hd in range(HEADS):
        v_ref[hd] = v_t[V_DIM * hd:V_DIM * (hd + 1), :].astype(BF16)


def _kv(lat, kr, w_uk, w_uv_t, g_qk_k):
    b, s, _ = lat.shape
    ts = min(s, ATTN_TILE)
    return pl.pallas_call(
        _kv_body,
        grid=(b, s // ts),
        in_specs=[pl.BlockSpec((None, ts, KV_LORA), lambda bi, i: (bi, i, 0)),
                  pl.BlockSpec((None, ts, ROPE), lambda bi, i: (bi, i, 0)),
                  pl.BlockSpec(w_uk.shape, lambda bi, i: (0, 0)),
                  pl.BlockSpec(w_uv_t.shape, lambda bi, i: (0, 0)),
                  pl.BlockSpec(g_qk_k.shape, lambda bi, i: (0, 0))],
        out_specs=(pl.BlockSpec((None, HEADS, ts, QK_PAD), lambda bi, i: (bi, 0, i, 0)),
                   pl.BlockSpec((None, HEADS, None, V_DIM, ts), lambda bi, i: (bi, 0, i, 0, 0))),
        out_shape=(jax.ShapeDtypeStruct((b, HEADS, s, QK_PAD), BF16),
                   jax.ShapeDtypeStruct((b, HEADS, s // ts, V_DIM, ts), BF16)),
        compiler_params=_cparams(("arbitrary", "arbitrary")),
        name="kv",
    )(lat, kr, w_uk, w_uv_t, g_qk_k)


def _attn_prompt_body(t, q_ref, k_ref, vt_ref, o_ref, s_a, s_b):
    i = pl.program_id(2)
    q = q_ref[...]

    def scores(j, buf):
        buf[...] = _nt(k_ref[pl.ds(pl.multiple_of(j * t, t), t), :], q)

    def consume(j, buf, carry, masked=False):
        m, l, acc = carry
        s = buf[...]
        if masked:
            visible = (lax.broadcasted_iota(I32, (t, t), 0) // CHUNK) <= (lax.broadcasted_iota(I32, (t, t), 1) // CHUNK)
            s = jnp.where(visible, s, NEG)
        m_new = jnp.maximum(m, jnp.max(s, axis=0, keepdims=True))
        alpha = jnp.exp2(m - m_new)
        p = jnp.exp2(s - m_new)
        l = alpha * l + jnp.sum(p, axis=0, keepdims=True)
        acc = alpha * acc + jnp.dot(vt_ref[j], p.astype(BF16), preferred_element_type=F32)
        return m_new, l, acc

    def pair(pp, carry):
        j = 2 * pp
        scores(j + 1, s_b)
        carry = consume(j, s_a, carry)
        scores(j + 2, s_a)
        return consume(j + 1, s_b, carry)

    def odd_tail(carry):
        scores(i, s_b)
        carry = consume(i - 1, s_a, carry)
        return consume(i, s_b, carry, masked=True)

    scores(0, s_a)
    carry = (jnp.full((1, t), NEG, F32), jnp.zeros((1, t), F32), jnp.zeros((V_DIM, t), F32))
    carry = lax.fori_loop(0, i // 2, pair, carry)
    _, l, acc = lax.cond(i % 2 == 1, odd_tail, lambda c: consume(i, s_a, c, masked=True), carry)
    o_ref[...] = (acc / l).T.astype(BF16)


def _attn_prompt(q, k, v_t):
    b, _, s, _ = q.shape
    t = v_t.shape[-1]
    return pl.pallas_call(
        functools.partial(_attn_prompt_body, t),
        grid=(b, HEADS, s // t),
        in_specs=[pl.BlockSpec((None, None, t, QK_PAD), lambda bi, h, i: (bi, h, i, 0)),
                  pl.BlockSpec((None, None, s, QK_PAD), lambda bi, h, i: (bi, h, 0, 0)),
                  pl.BlockSpec((None, None, s // t, V_DIM, t), lambda bi, h, i: (bi, h, 0, 0, 0))],
        out_specs=pl.BlockSpec((None, t, V_DIM), lambda bi, h, i: (bi, i, h)),
        out_shape=jax.ShapeDtypeStruct((b, s, HEADS * V_DIM), BF16),
        scratch_shapes=[pltpu.VMEM((t, t), F32), pltpu.VMEM((t, t), F32)],
        compiler_params=_cparams(("arbitrary", "arbitrary", "arbitrary"), VMEM_BIG),
        name="attn_prompt",
    )(q, k, v_t)


def _attn_sample_body(past, sq, q_ref, plat_ref, pkr_ref, nlat_ref, nkr_ref, wk_ref, wv_ref, gk_ref, o_ref,
                      kp_scr, kn_scr):
    plat = plat_ref[...].astype(BF16)
    nlat = nlat_ref[...].astype(BF16)
    _key_rows(plat, pkr_ref[...], wk_ref, gk_ref, kp_scr)
    _key_rows(nlat, nkr_ref[...], wk_ref, gk_ref, kn_scr)
    vp_t = _nt(wv_ref[...], plat).astype(BF16)
    vn_t = _nt(wv_ref[...], nlat).astype(BF16)
    key_chunk = (past + lax.broadcasted_iota(I32, (sq, sq), 0)) // CHUNK
    qry_chunk = (past + lax.broadcasted_iota(I32, (sq, sq), 1)) // CHUNK
    for hd in range(HEADS):
        q = q_ref[hd]
        s_p = _nt(kp_scr[hd], q)
        s_n = jnp.where(key_chunk <= qry_chunk, _nt(kn_scr[hd], q), NEG)
        m = jnp.maximum(jnp.max(s_p, axis=0, keepdims=True), jnp.max(s_n, axis=0, keepdims=True))
        p_p = jnp.exp2(s_p - m)
        p_n = jnp.exp2(s_n - m)
        l = jnp.sum(p_p, axis=0, keepdims=True) + jnp.sum(p_n, axis=0, keepdims=True)
        rows = slice(V_DIM * hd, V_DIM * (hd + 1))
        o_t = (jnp.dot(vp_t[rows, :], p_p.astype(BF16), preferred_element_type=F32)
               + jnp.dot(vn_t[rows, :], p_n.astype(BF16), preferred_element_type=F32))
        o_ref[:, rows] = (o_t / l).T.astype(BF16)


def _attn_sample(q, past_lat, past_kr, lat, kr, w_uk, w_uv_t, g_qk_k):
    b, _, sq, _ = q.shape
    past = past_lat.shape[1]
    rows = lambda n, w: pl.BlockSpec((None, n, w), lambda bi: (bi, 0, 0))
    full = lambda a: pl.BlockSpec(a.shape, lambda bi: (0,) * a.ndim)
    return pl.pallas_call(
        functools.partial(_attn_sample_body, past, sq),
        grid=(b,),
        in_specs=[pl.BlockSpec((None, HEADS, sq, QK_PAD), lambda bi: (bi, 0, 0, 0)),
                  rows(past, KV_LORA), rows(past, ROPE), rows(sq, KV_LORA), rows(sq, ROPE),
                  full(w_uk), full(w_uv_t), full(g_qk_k)],
        out_specs=rows(sq, HEADS * V_DIM),
        out_shape=jax.ShapeDtypeStruct((b, sq, HEADS * V_DIM), BF16),
        scratch_shapes=[pltpu.VMEM((HEADS, past, QK_PAD), BF16), pltpu.VMEM((HEADS, sq, QK_PAD), BF16)],
        compiler_params=_cparams(("arbitrary",), 40 * 1024 * 1024),
        name="attn_sample",
    )(q, past_lat, past_kr, lat, kr, w_uk, w_uv_t, g_qk_k)


def _gla_masks(c):
    idx = np.arange(c)
    le = idx[None, :] <= idx[:, None]
    gt = idx[None, :] > idx[:, None]
    return np.concatenate([le, gt], axis=0).astype(np.float32), int(np.log2(c))


def _level_exponents(b, g, c, level):
    n = c >> level
    row = lax.broadcasted_iota(I32, (c, 1), 0)
    if n >= 8:
        split = b.reshape(c // n, n, LANES)[:, n // 2 - 1:n // 2, :]
        split = jnp.broadcast_to(split, (c // n, n, LANES)).reshape(c, LANES)
        return jnp.where((row & (n // 2)) != 0, b - split, split - b)
    g_prev = pltpu.roll(g, 1, 0)
    g_next = pltpu.roll(g, c - 1, 0)
    if n == 4:
        r = row & 3
        return jnp.where(r == 0, g_next, jnp.where(r == 1, 0.0, jnp.where(r == 2, g, g + g_prev)))
    assert n == 2
    return jnp.where((row & 1) != 0, g, 0.0)


def _gla_body(c, n_chunks, levels, mall_ref, q_ref, k_ref, g_ref, v_ref, r_ref, s0_ref, gout_ref,
              o_ref, sfin_ref, st_scr):
    it = pl.program_id(1)

    @pl.when(it == 0)
    def _():
        st_scr[...] = s0_ref[...]

    lane = lax.broadcasted_iota(I32, (c, LANES), 1)
    head_lanes = (lane < GLA_DK, lane >= GLA_DK)
    st_lane_lo = lax.broadcasted_iota(I32, (GLA_DV, LANES), 1) < GLA_DK
    row = lax.broadcasted_iota(I32, (c, 1), 0)
    ri = lax.broadcasted_iota(I32, (c, c), 0)
    ci = lax.broadcasted_iota(I32, (c, c), 1)
    mall = mall_ref[...]

    for ch in range(n_chunks):
        rows = slice(ch * c, (ch + 1) * c)
        for p in range(HEADS // 2):
            ls = slice(LANES * p, LANES * (p + 1))
            g = g_ref[rows, ls]
            q = q_ref[rows, ls]
            k = k_ref[rows, ls]
            g_hi = g.astype(BF16)
            g_lo = (g - g_hi.astype(F32)).astype(BF16)
            e2 = jnp.dot(mall, jnp.concatenate([g_hi, g_lo], axis=1), preferred_element_type=F32)
            e = e2[:, 0:LANES] + e2[:, LANES:2 * LANES]
            b = e[0:c]
            eb = jnp.exp(b)
            qb = q * eb
            kd = (k * jnp.exp(e[c:2 * c])).astype(BF16)
            d_last = eb[c - 1:c, :]
            st = st_scr[p]
            st_b = st.astype(BF16)
            qs, ks = [q], [k.astype(BF16)]
            for l in range(levels):
                bottom = (row & (c >> (l + 1))) != 0
                decay = jnp.exp(_level_exponents(b, g, c, l))
                qs.append(jnp.where(bottom, q * decay, 0.0))
                ks.append(jnp.where(bottom, 0.0, k * decay).astype(BF16))
            upd = []
            for hh in range(2):
                hd = 2 * p + hh
                sel = head_lanes[hh]
                a = jnp.where(ri == ci, _nt(jnp.where(sel, qs[0], 0.0).astype(BF16), ks[0]), 0.0)
                for l in range(levels):
                    pr = _nt(jnp.where(sel, qs[l + 1], 0.0).astype(BF16), ks[l + 1])
                    if l > 0:
                        pr = jnp.where((ri ^ ci) < (c >> l), pr, 0.0)
                    a = a + pr
                vh = v_ref[rows, GLA_DV * hd:GLA_DV * (hd + 1)]
                o = jnp.dot(a.astype(BF16), vh, preferred_element_type=F32)
                o = o + _nt(jnp.where(sel, qb, 0.0).astype(BF16), st_b)
                on = o * _rms(o, GLA_DV) * gout_ref[...]
                r = r_ref[rows, GLA_DV * hd:GLA_DV * (hd + 1)]
                o_ref[rows, GLA_DV * hd:GLA_DV * (hd + 1)] = (on * (r * jax.nn.sigmoid(r))).astype(BF16)
                upd.append(lax.dot_general(vh, kd, (((0,), (0,)), ((), ())), preferred_element_type=F32))
            st_scr[p] = st * d_last + jnp.where(st_lane_lo, upd[0], upd[1])

    @pl.when(it == pl.num_programs(1) - 1)
    def _():
        sfin_ref[...] = st_scr[...]


def _gla(gq, gk, gl, gv, gr, s0, g_out):
    b, s, _ = gq.shape
    c = min(CHUNK, s)
    tile = min(s, 4 * c)
    masks, levels = _gla_masks(c)
    mall = jnp.asarray(masks, BF16)
    tok = lambda w: pl.BlockSpec((None, tile, w), lambda bi, i: (bi, i, 0))
    st_spec = pl.BlockSpec((None, HEADS // 2, GLA_DV, LANES), lambda bi, i: (bi, 0, 0, 0))
    return pl.pallas_call(
        functools.partial(_gla_body, c, tile // c, levels),
        grid=(b, s // tile),
        in_specs=[pl.BlockSpec(mall.shape, lambda bi, i: (0, 0)),
                  tok(HEADS * GLA_DK), tok(HEADS * GLA_DK), tok(HEADS * GLA_DK), tok(HEADS * GLA_DV),
                  tok(HEADS * GLA_DV), st_spec, pl.BlockSpec(g_out.shape, lambda bi, i: (0, 0))],
        out_specs=(tok(HEADS * GLA_DV), st_spec),
        out_shape=(jax.ShapeDtypeStruct((b, s, HEADS * GLA_DV), BF16),
                   jax.ShapeDtypeStruct((b, HEADS // 2, GLA_DV, LANES), F32)),
        scratch_shapes=[pltpu.VMEM((HEADS // 2, GLA_DV, LANES), F32)],
        compiler_params=_cparams(("arbitrary", "arbitrary")),
        name="gla",
    )(mall, gq, gk, gl, gv, gr, s0, g_out)


def _state_to_pairs(s):
    b = s.shape[0]
    s = s.reshape(b, HEADS // 2, 2, GLA_DK, GLA_DV)
    return jnp.transpose(s, (0, 1, 4, 2, 3)).reshape(b, HEADS // 2, GLA_DV, 2 * GLA_DK)


def _state_from_pairs(s):
    b = s.shape[0]
    s = s.reshape(b, HEADS // 2, GLA_DV, 2, GLA_DK)
    return jnp.transpose(s, (0, 1, 3, 4, 2)).reshape(b, HEADS, GLA_DK, GLA_DV)


def _post_body(x_ref, om_ref, og_ref, gt_ref, sc_ref, sh_ref, wo_ref, gffn_ref, wr_ref, br_ref,
               x2_ref, h_ref, idx_ref, wt_ref):
    half = om_ref.shape[-1]
    mix = (jnp.dot(om_ref[...], wo_ref[0:half, :], preferred_element_type=F32)
           + jnp.dot(og_ref[...], wo_ref[half:2 * half, :], preferred_element_type=F32))
    x2 = x_ref[...] + gt_ref[...] * mix
    x2_ref[...] = x2
    d = x2.shape[-1]
    h = (x2 * _rms(x2, d) * gffn_ref[...]) * (1.0 + sc_ref[...]) + sh_ref[...]
    h_hi = h.astype(BF16)
    h_ref[...] = h_hi
    h_lo = (h - h_hi.astype(F32)).astype(BF16)
    logits = _nt(wr_ref[0], h_hi) + _nt(wr_ref[0], h_lo) + _nt(wr_ref[1], h_hi) + br_ref[...]
    n_exp, tm = logits.shape
    eid = lax.broadcasted_iota(I32, (n_exp, tm), 0)
    vals, tops, ids = logits, [], []
    for _ in range(TOP_K):
        m = jnp.max(vals, axis=0, keepdims=True)
        sel = jnp.min(jnp.where(vals == m, eid, n_exp), axis=0, keepdims=True)
        tops.append(m)
        ids.append(sel)
        vals = jnp.where(eid == sel, -jnp.inf, vals)
    es = [jnp.exp(t - tops[0]) for t in tops]
    tot = es[0] + es[1] + es[2] + es[3]
    idx_ref[...] = jnp.concatenate(ids, axis=0)
    wt_ref[...] = jnp.concatenate([e / tot for e in es], axis=0)


def _post(x, om, og, gate, scale, shift, w_o, g_ffn, w_r2, b_r):
    t, d = x.shape
    tm = min(t, 512)
    per_tok = gate.shape[0] == t
    mod = pl.BlockSpec((tm, d), lambda i: (i, 0)) if per_tok else pl.BlockSpec((1, d), lambda i: (0, 0))
    tok = lambda w: pl.BlockSpec((tm, w), lambda i: (i, 0))
    full = lambda a: pl.BlockSpec(a.shape, lambda i: (0,) * a.ndim)
    return pl.pallas_call(
        _post_body,
        grid=(t // tm,),
        in_specs=[tok(d), tok(om.shape[1]), tok(og.shape[1]), mod, mod, mod, full(w_o), full(g_ffn), full(w_r2),
                  full(b_r)],
        out_specs=(tok(d), tok(d),
                   pl.BlockSpec((TOP_K, tm), lambda i: (0, i)), pl.BlockSpec((TOP_K, tm), lambda i: (0, i))),
        out_shape=(jax.ShapeDtypeStruct((t, d), F32), jax.ShapeDtypeStruct((t, d), BF16),
                   jax.ShapeDtypeStruct((TOP_K, t), I32), jax.ShapeDtypeStruct((TOP_K, t), F32)),
        compiler_params=_cparams(("arbitrary",), 40 * 1024 * 1024),
        name="post",
    )(x, om, og, gate, scale, shift, w_o, g_ffn, w_r2, b_r)


def _rank_body(idx_ref, rank_ref, cnt_ref):
    idx = idx_ref[...]
    tg = idx.shape[1]
    eid = lax.broadcasted_iota(I32, (N_EXPERTS, tg), 0)
    hits = [eid == idx[k:k + 1, :] for k in range(TOP_K)]
    member = jnp.zeros((N_EXPERTS, tg), F32)
    for hk in hits:
        member = member + jnp.where(hk, 1.0, 0.0)
    before = (lax.broadcasted_iota(I32, (tg, tg), 0) < lax.broadcasted_iota(I32, (tg, tg), 1))
    prefix = jnp.dot(member.astype(BF16), jnp.where(before, 1.0, 0.0).astype(BF16), preferred_element_type=F32)
    rank_ref[...] = jnp.concatenate(
        [jnp.sum(jnp.where(hk, prefix, 0.0), axis=0, keepdims=True) for hk in hits], axis=0).astype(I32)
    cnt_ref[...] = jnp.broadcast_to(jnp.sum(member, axis=1, keepdims=True), (N_EXPERTS, LANES)).astype(I32)


def _rank(idx):
    t = idx.shape[1]
    nt = t // ROUTE_TILE
    return pl.pallas_call(
        _rank_body,
        grid=(nt,),
        in_specs=[pl.BlockSpec((TOP_K, ROUTE_TILE), lambda i: (0, i))],
        out_specs=(pl.BlockSpec((TOP_K, ROUTE_TILE), lambda i: (0, i)),
                   pl.BlockSpec((None, N_EXPERTS, LANES), lambda i: (i, 0, 0))),
        out_shape=(jax.ShapeDtypeStruct((TOP_K, t), I32), jax.ShapeDtypeStruct((nt, N_EXPERTS, LANES), I32)),
        compiler_params=_cparams(("arbitrary",)),
        name="rank",
    )(idx)


def _route_tables(idx, lrank, cnt3):
    nt = cnt3.shape[0]
    t = idx.shape[1]
    cnt = cnt3[:, :, 0]
    run = _round_up(cnt, RUN_CHUNK)
    lo_end = jnp.cumsum(run, axis=1)
    lo = lo_end - run
    n_chunks = lo_end[:, -1] // RUN_CHUNK
    region = _round_up(jnp.sum(run, axis=0), EXPERT_ROWS)
    g_end = jnp.cumsum(region)
    run_dest = (g_end - region)[None, :] + jnp.cumsum(run, axis=0) - run
    max_rows = TOP_K * ROUTE_TILE + N_EXPERTS * (RUN_CHUNK - 1)
    n_tab = _round_up(max_rows, RUN_CHUNK) // RUN_CHUNK
    c_start = jnp.arange(n_tab, dtype=I32) * RUN_CHUNK
    e_of_c = jnp.minimum(jnp.sum(lo_end[:, None, :] <= c_start[None, :, None], axis=2), N_EXPERTS - 1)
    pick = e_of_c[:, :, None] == jnp.arange(N_EXPERTS, dtype=I32)[None, None, :]
    chunk_dest = jnp.sum(jnp.where(pick, (run_dest - lo)[:, None, :], 0), axis=2) + c_start[None, :]
    table = jnp.concatenate([chunk_dest, n_chunks[:, None]], axis=1).astype(I32).reshape(nt, 1, n_tab + 1)
    eid = jnp.arange(N_EXPERTS, dtype=I32)[:, None]
    lo_tok = jnp.repeat(lo.T, ROUTE_TILE, axis=1)
    lpos = jnp.stack([jnp.sum(jnp.where(idx[k][None, :] == eid, lo_tok, 0), axis=0) for k in range(TOP_K)])
    lpos = (lpos + lrank).astype(I32)
    n_blocks = _round_up(t * TOP_K + nt * N_EXPERTS * (RUN_CHUNK - 1), EXPERT_ROWS) // EXPERT_ROWS + N_EXPERTS
    b_start = jnp.arange(n_blocks, dtype=I32) * EXPERT_ROWS
    blk_e = jnp.minimum(jnp.sum(g_end[None, :] <= b_start[:, None], axis=1), N_EXPERTS - 1).astype(I32)
    n_valid = (g_end[-1:] // EXPERT_ROWS).astype(I32)
    used = region > 0
    e_ids = jnp.arange(N_EXPERTS, dtype=I32)
    later_used = used[None, :] & (e_ids[None, :] > e_ids[:, None])
    nxt_e = jnp.min(jnp.where(later_used, e_ids[None, :], N_EXPERTS), axis=1)
    nxt_e = jnp.where(nxt_e < N_EXPERTS, nxt_e, -1).astype(I32)
    ord_e = (jnp.cumsum(used.astype(I32)) - 1).astype(I32)
    tail = jnp.concatenate([jnp.where(used, g_end - EXPERT_ROWS, -1), n_valid]).astype(I32)
    tail = tail.reshape(1, N_EXPERTS + 1)
    return dict(table=table, lpos=lpos, blk_e=blk_e, n_valid=n_valid, nxt_e=nxt_e, ord_e=ord_e, tail=tail,
                n_blocks=n_blocks)


def _chunk_copy(src, dst, sem):
    return pltpu.make_async_copy(src, dst, sem)


def _scatter_body(n_tab, nt_a, n_blocks, tab_ref, prv_ref, tail_ref, lpos_ref, ha_ref, hb_ref, xout_ref, sorted_scr,
                  zero_scr, sems, zero_sem):
    i = pl.program_id(0)
    slot = i % 2
    n_chunks = tab_ref[0, n_tab]
    tt = ha_ref.shape[0]

    @pl.when(i == 0)
    def _():
        zero_scr[...] = jnp.zeros_like(zero_scr)
        n_valid = tail_ref[0, N_EXPERTS]

        def block(start):
            return xout_ref.at[pl.ds(pl.multiple_of(start, EXPERT_ROWS), EXPERT_ROWS)]

        for e in range(N_EXPERTS):
            @pl.when(tail_ref[0, e] >= 0)
            def _():
                _chunk_copy(zero_scr, block(tail_ref[0, e]), zero_sem).start()

        def fill(b, carry):
            _chunk_copy(zero_scr, block(b * EXPERT_ROWS), zero_sem).start()
            return carry

        def fill_done(b, carry):
            _chunk_copy(zero_scr, block(0), zero_sem).wait()
            return carry

        lax.fori_loop(n_valid, n_blocks, fill, 0)
        for e in range(N_EXPERTS):
            @pl.when(tail_ref[0, e] >= 0)
            def _():
                _chunk_copy(zero_scr, block(0), zero_sem).wait()
        lax.fori_loop(n_valid, n_blocks, fill_done, 0)

    lp = lpos_ref[...]
    h = jnp.where(i < nt_a, ha_ref[...], hb_ref[...])

    def sort_block(rb, carry):
        r0 = pl.multiple_of(rb * SORT_ROWS, SORT_ROWS)
        rid = r0 + lax.broadcasted_iota(I32, (SORT_ROWS, tt), 0)
        onehot = jnp.zeros((SORT_ROWS, tt), F32)
        for k in range(TOP_K):
            onehot = onehot + jnp.where(lp[k:k + 1, :] == rid, 1.0, 0.0)
        sorted_scr[slot, pl.ds(r0, SORT_ROWS), :] = jnp.dot(onehot.astype(BF16), h,
                                                            preferred_element_type=F32).astype(BF16)
        return carry

    lax.fori_loop(0, (n_chunks * RUN_CHUNK + SORT_ROWS - 1) // SORT_ROWS, sort_block, 0)

    def issue(c, carry):
        src = sorted_scr.at[slot, pl.ds(pl.multiple_of(c * RUN_CHUNK, RUN_CHUNK), RUN_CHUNK)]
        dst = xout_ref.at[pl.ds(pl.multiple_of(tab_ref[0, c], RUN_CHUNK), RUN_CHUNK)]
        _chunk_copy(src, dst, sems.at[slot]).start()
        return carry

    def drain(sl):
        def body(c, carry):
            _chunk_copy(sorted_scr.at[sl, pl.ds(0, RUN_CHUNK)], xout_ref.at[pl.ds(0, RUN_CHUNK)], sems.at[sl]).wait()
            return carry
        return body

    lax.fori_loop(0, n_chunks, issue, 0)

    @pl.when(i > 0)
    def _():
        lax.fori_loop(0, prv_ref[0, n_tab], drain(1 - slot), 0)

    @pl.when(i == pl.num_programs(0) - 1)
    def _():
        lax.fori_loop(0, n_chunks, drain(slot), 0)


def _scatter(table, tail, lpos, h_a, h_b, n_blocks):
    d = h_a.shape[1]
    nt_a, nt_b = h_a.shape[0] // ROUTE_TILE, h_b.shape[0] // ROUTE_TILE
    n_tab = table.shape[2] - 1
    sort_cap = _round_up(n_tab * RUN_CHUNK, SORT_ROWS)
    tab_spec = lambda f: pl.BlockSpec((None, 1, n_tab + 1), lambda i: (f(i), 0, 0), memory_space=pltpu.SMEM)
    return pl.pallas_call(
        functools.partial(_scatter_body, n_tab, nt_a, n_blocks),
        grid=(nt_a + nt_b,),
        in_specs=[tab_spec(lambda i: i), tab_spec(lambda i: jnp.maximum(i - 1, 0)),
                  pl.BlockSpec((1, N_EXPERTS + 1), lambda i: (0, 0), memory_space=pltpu.SMEM),
                  pl.BlockSpec((TOP_K, ROUTE_TILE), lambda i: (0, i)),
                  pl.BlockSpec((ROUTE_TILE, d), lambda i: (jnp.minimum(i, nt_a - 1), 0)),
                  pl.BlockSpec((ROUTE_TILE, d), lambda i: (jnp.maximum(i - nt_a, 0), 0))],
        out_specs=pl.BlockSpec(memory_space=pl.ANY),
        out_shape=jax.ShapeDtypeStruct((n_blocks * EXPERT_ROWS, d), BF16),
        scratch_shapes=[pltpu.VMEM((2, sort_cap, d), BF16), pltpu.VMEM((EXPERT_ROWS, d), BF16),
                        pltpu.SemaphoreType.DMA((2,)), pltpu.SemaphoreType.DMA(())],
        compiler_params=_cparams(("arbitrary",), 48 * 1024 * 1024),
        name="scatter",
    )(table, table, tail, lpos, h_a, h_b)


def _experts_body(be_ref, nv_ref, nxt_ref, ord_ref, x_ref, wgu_hbm, bgu_ref, wd_hbm, bd_ref, y_ref,
                  wgu_f, wd_f, wgu_s, wd_s, sem_gu, sem_d):
    b = pl.program_id(0)
    e = be_ref[b]
    prev = be_ref[jnp.maximum(b - 1, 0)]
    valid = b < nv_ref[0]
    d_ff = wd_s.shape[0]
    slot = ord_ref[e] % 2

    def weights(expert, sl):
        return (pltpu.make_async_copy(wgu_hbm.at[expert], wgu_f.at[sl], sem_gu.at[sl]),
                pltpu.make_async_copy(wd_hbm.at[expert], wd_f.at[sl], sem_d.at[sl]))

    @pl.when(valid & ((b == 0) | (e != prev)))
    def _():
        @pl.when(b == 0)
        def _():
            for cp in weights(e, slot):
                cp.start()

        for cp in weights(e, slot):
            cp.wait()

        @pl.when(nxt_ref[e] >= 0)
        def _():
            for cp in weights(nxt_ref[e], 1 - slot):
                cp.start()

        wgu_s[...] = wgu_f[slot].astype(BF16)
        wd_s[...] = wd_f[slot].astype(BF16)

    @pl.when(valid)
    def _():
        gu = jnp.dot(x_ref[...], wgu_s[...], preferred_element_type=F32) + bgu_ref[...]
        gate = jnp.minimum(gu[:, 0:d_ff], SWIGLU_LIMIT)
        up = jnp.clip(gu[:, d_ff:2 * d_ff], -SWIGLU_LIMIT, SWIGLU_LIMIT)
        act = ((up + 1.0) * (gate * jax.nn.sigmoid(gate * SWIGLU_ALPHA))).astype(BF16)
        y_ref[...] = (jnp.dot(act, wd_s[...], preferred_element_type=F32) + bd_ref[...]).astype(BF16)

    @pl.when(jnp.logical_not(valid))
    def _():
        y_ref[...] = jnp.zeros_like(y_ref)


def _experts(blk_e, n_valid, nxt_e, ord_e, xpad, w_gu, b_gu, w_down, b_down):
    m, d = xpad.shape
    nb = m // EXPERT_ROWS
    n_exp, _, f2 = w_gu.shape
    d_ff = w_down.shape[1]
    last = lambda b, be, nv: jnp.minimum(b, nv[0] - 1)
    grid_spec = pltpu.PrefetchScalarGridSpec(
        num_scalar_prefetch=4,
        grid=(nb,),
        in_specs=[pl.BlockSpec((EXPERT_ROWS, d), lambda b, be, nv, nx, od: (last(b, be, nv), 0)),
                  pl.BlockSpec(memory_space=pl.ANY),
                  pl.BlockSpec((None, 1, f2), lambda b, be, nv, nx, od: (be[last(b, be, nv)], 0, 0)),
                  pl.BlockSpec(memory_space=pl.ANY),
                  pl.BlockSpec((None, 1, d), lambda b, be, nv, nx, od: (be[last(b, be, nv)], 0, 0))],
        out_specs=pl.BlockSpec((EXPERT_ROWS, d), lambda b, be, nv, nx, od: (b, 0)),
        scratch_shapes=[pltpu.VMEM((2, d, f2), F32), pltpu.VMEM((2, d_ff, d), F32),
                        pltpu.VMEM((d, f2), BF16), pltpu.VMEM((d_ff, d), BF16),
                        pltpu.SemaphoreType.DMA((2,)), pltpu.SemaphoreType.DMA((2,))],
    )
    return pl.pallas_call(
        _experts_body,
        grid_spec=grid_spec,
        out_shape=jax.ShapeDtypeStruct((m, d), BF16),
        compiler_params=_cparams(("arbitrary",), VMEM_BIG),
        name="experts",
    )(blk_e, n_valid, nxt_e, ord_e, xpad, w_gu, b_gu.reshape(n_exp, 1, f2), w_down, b_down.reshape(n_exp, 1, d))


def _combine_body(n_tab, tab_ref, nxt_ref, lpos_ref, wt_ref, x2_ref, gt_ref, y_ref, o_ref, ysort_scr, sems):
    i = pl.program_id(0)
    slot = i % 2
    n_chunks = tab_ref[0, n_tab]
    tt = x2_ref.shape[0]

    def fetch(tab, sl):
        def issue(c, carry):
            src = y_ref.at[pl.ds(pl.multiple_of(tab[0, c], RUN_CHUNK), RUN_CHUNK)]
            dst = ysort_scr.at[sl, pl.ds(pl.multiple_of(c * RUN_CHUNK, RUN_CHUNK), RUN_CHUNK)]
            _chunk_copy(src, dst, sems.at[sl]).start()
            return carry

        lax.fori_loop(0, tab[0, n_tab], issue, 0)

    @pl.when(i == 0)
    def _():
        ysort_scr[...] = jnp.zeros_like(ysort_scr)
        fetch(tab_ref, 0)

    @pl.when(i + 1 < pl.num_programs(0))
    def _():
        fetch(nxt_ref, 1 - slot)

    def drain(c, carry):
        _chunk_copy(y_ref.at[pl.ds(0, RUN_CHUNK)], ysort_scr.at[slot, pl.ds(0, RUN_CHUNK)], sems.at[slot]).wait()
        return carry

    lax.fori_loop(0, n_chunks, drain, 0)

    lp = lpos_ref[...]
    w = wt_ref[...]
    o_ref[...] = x2_ref[...]
    gt = gt_ref[...]

    def gather_block(rb, carry):
        r0 = pl.multiple_of(rb * SORT_ROWS, SORT_ROWS)
        rid = r0 + lax.broadcasted_iota(I32, (SORT_ROWS, tt), 0)
        pw_t = jnp.zeros((SORT_ROWS, tt), F32)
        for k in range(TOP_K):
            pw_t = pw_t + jnp.where(lp[k:k + 1, :] == rid, w[k:k + 1, :], 0.0)
        ys = ysort_scr[slot, pl.ds(r0, SORT_ROWS), :]
        part = lax.dot_general(pw_t.astype(BF16), ys, (((0,), (0,)), ((), ())), preferred_element_type=F32)
        o_ref[...] = o_ref[...] + gt * part
        return carry

    lax.fori_loop(0, (n_chunks * RUN_CHUNK + SORT_ROWS - 1) // SORT_ROWS, gather_block, 0)


def _combine(table, lpos, wts, x2, gate, ypad):
    t, d = x2.shape
    nt = t // ROUTE_TILE
    n_tab = table.shape[2] - 1
    per_tok = gate.shape[0] == t
    mod = pl.BlockSpec((ROUTE_TILE, d), lambda i: (i, 0)) if per_tok else pl.BlockSpec((1, d), lambda i: (0, 0))
    sort_cap = _round_up(n_tab * RUN_CHUNK, SORT_ROWS)
    tab_spec = lambda f: pl.BlockSpec((None, 1, n_tab + 1), lambda i: (f(i), 0, 0), memory_space=pltpu.SMEM)
    return pl.pallas_call(
        functools.partial(_combine_body, n_tab),
        grid=(nt,),
        in_specs=[tab_spec(lambda i: i), tab_spec(lambda i: jnp.minimum(i + 1, nt - 1)),
                  pl.BlockSpec((TOP_K, ROUTE_TILE), lambda i: (0, i)),
                  pl.BlockSpec((TOP_K, ROUTE_TILE), lambda i: (0, i)),
                  pl.BlockSpec((ROUTE_TILE, d), lambda i: (i, 0)),
                  mod,
                  pl.BlockSpec(memory_space=pl.ANY)],
        out_specs=pl.BlockSpec((ROUTE_TILE, d), lambda i: (i, 0)),
        out_shape=jax.ShapeDtypeStruct((t, d), F32),
        scratch_shapes=[pltpu.VMEM((2, sort_cap, d), BF16), pltpu.SemaphoreType.DMA((2,))],
        compiler_params=_cparams(("arbitrary",), 48 * 1024 * 1024),
        name="combine",
    )(table, table, lpos, wts, x2, gate, ypad)


def _prep_weights(g_norm_mix, w_in, g_q_a, w_uq, g_kv_a, w_ukv, g_qk_q, g_qk_k, w_g2, b_g2, g_gla_out, w_o,
                  g_norm_ffn, w_router, b_router):
    d = w_in.shape[0]
    o_qa, o_kva, o_kr = 0, Q_LORA, Q_LORA + KV_LORA
    o_gq = o_kr + ROPE
    o_gk = o_gq + HEADS * GLA_DK
    o_gv = o_gk + HEADS * GLA_DK
    o_glr = o_gv + HEADS * GLA_DV
    o_gr = o_glr + GATE_RANK
    kr_cols = w_in[:, o_kr:o_kr + ROPE]
    w1 = jnp.concatenate([
        w_in[:, o_qa:o_kva], w_in[:, o_kva:o_kr], kr_cols, kr_cols, w_in[:, o_gq:o_gk], w_in[:, o_gk:o_gv],
        w_in[:, o_gv:o_glr], w_in[:, o_gr:o_gr + HEADS * GLA_DV], w_in[:, o_glr:o_gr],
        jnp.zeros((d, LANES - GATE_RANK), w_in.dtype)], axis=1).astype(BF16)
    assert w1.shape[1] == _W1_COLS
    wq = w_uq.reshape(Q_LORA, HEADS, QK)
    wuq = jnp.concatenate([wq[:, :, 0:NOPE].reshape(Q_LORA, HEADS * NOPE),
                           wq[:, :, NOPE:QK].reshape(Q_LORA, HEADS * ROPE)], axis=1).astype(BF16)
    wkv = w_ukv.reshape(KV_LORA, HEADS, NOPE + V_DIM)
    wuk = wkv[:, :, 0:NOPE].reshape(KV_LORA, HEADS * NOPE).astype(BF16)
    wuv_t = wkv[:, :, NOPE:].reshape(KV_LORA, HEADS * V_DIM).T.astype(BF16)
    pad_rope = lambda g: jnp.stack([g[0:NOPE], jnp.concatenate([g[NOPE:QK], jnp.zeros((QK_PAD - QK,), g.dtype)])])
    inv = ROPE_THETA ** (-jnp.arange(HALF, dtype=F32) / HALF)
    sign = jnp.concatenate([-jnp.ones((HALF,), F32), jnp.ones((HALF,), F32)])
    rope_tab = jnp.stack([jnp.tile(inv, LANES // HALF), jnp.tile(sign, LANES // ROPE)])
    wg2 = jnp.concatenate([w_g2, jnp.zeros((LANES - GATE_RANK, w_g2.shape[1]), w_g2.dtype)], axis=0).astype(BF16)
    wr_t = w_router.T
    wr_hi = wr_t.astype(BF16)
    wr_lo = (wr_t - wr_hi.astype(F32)).astype(BF16)
    return dict(
        g_mix=g_norm_mix.reshape(1, d), w1=w1, g_qa=g_q_a.reshape(1, -1), w_uq=wuq, g_kv=g_kv_a.reshape(1, -1),
        w_uk=wuk, w_uv_t=wuv_t, g_qk_q=pad_rope(g_qk_q), g_qk_k=pad_rope(g_qk_k), rope=rope_tab, w_g2=wg2,
        b_g2=b_g2.reshape(1, -1), g_out=g_gla_out.reshape(1, -1), w_o=w_o.astype(BF16),
        g_ffn=g_norm_ffn.reshape(1, d), w_r2=jnp.stack([wr_hi, wr_lo]), b_r=b_router.reshape(-1, 1))


def _mixer(x, mod, pos0, past_lat, past_kr, s0_pairs, wts):
    b, s, d = x.shape
    q, lat, kr, gq, gk, gv, gl, gr = _proj(x, mod[:, 0:1], mod[:, 1:2], pos0, wts)
    kv_w = (wts["w_uk"], wts["w_uv_t"], wts["g_qk_k"])
    if past_lat is None:
        k_new, vt_new = _kv(lat, kr, *kv_w)
        o_mla = _attn_prompt(q, k_new, vt_new)
    else:
        o_mla = _attn_sample(q, past_lat, past_kr, lat, kr, *kv_w)
    o_gla, s_fin = _gla(gq, gk, gl, gv, gr, s0_pairs, wts["g_out"])
    t = b * s
    if b == 1:
        rows = lambda j: mod[0, j:j + 1]
    else:
        rows = lambda j: jnp.broadcast_to(mod[:, j:j + 1], (b, s, d)).reshape(t, d)
    x2, h2, idx, wt = _post(x.reshape(t, d), o_mla.reshape(t, -1), o_gla.reshape(t, -1), rows(2), rows(4), rows(3),
                            wts["w_o"], wts["g_ffn"], wts["w_r2"], wts["b_r"])
    return dict(x2=x2, h2=h2, idx=idx, wt=wt, gate_f=rows(5), lat=lat, kr=kr, s_fin=s_fin)


def kernel(x_prompt, x_sample, cache_mla_latent, cache_mla_krope, state_gla, c_prompt, c_sample, w_ada, b_ada, g_norm_mix, w_in, g_q_a, w_uq, g_kv_a, w_ukv, g_qk_q, g_qk_k, w_g2, b_g2, g_gla_out, w_o, g_norm_ffn, w_router, b_router, w_gu, b_gu, w_down, b_down):
    depth = w_ada.shape[0]
    assert depth == 1, "single-layer step"
    bp, sp, d = x_prompt.shape
    bs, ss, _ = x_sample.shape
    tp, tsm = bp * sp, bs * ss
    assert tp % ROUTE_TILE == 0 and tsm % ROUTE_TILE == 0, "token counts must be whole routing tiles"
    past = cache_mla_latent.shape[2]
    wts = _prep_weights(g_norm_mix[0], w_in[0], g_q_a[0], w_uq[0], g_kv_a[0], w_ukv[0], g_qk_q[0], g_qk_k[0],
                        w_g2[0], b_g2[0], g_gla_out[0], w_o[0], g_norm_ffn[0], w_router[0], b_router[0])

    mod = _ada(jnp.concatenate([c_prompt, c_sample], axis=0), w_ada[0], b_ada[0]).reshape(bp + bs, 6, d)
    zero_state = jnp.zeros((bp, HEADS // 2, GLA_DV, LANES), F32)
    pr = _mixer(x_prompt, mod[:bp], 0, None, None, zero_state, wts)
    sa = _mixer(x_sample, mod[bp:], past, cache_mla_latent[0], cache_mla_krope[0], _state_to_pairs(state_gla[0]),
                wts)

    idx = jnp.concatenate([pr["idx"], sa["idx"]], axis=1)
    lrank, cnt3 = _rank(idx)
    rt = _route_tables(idx, lrank, cnt3)
    ntp = tp // ROUTE_TILE
    lpos, table = rt["lpos"], rt["table"]
    xpad = _scatter(table, rt["tail"], lpos, pr["h2"], sa["h2"], rt["n_blocks"])
    ypad = _experts(rt["blk_e"], rt["n_valid"], rt["nxt_e"], rt["ord_e"], xpad, w_gu[0], b_gu[0], w_down[0],
                    b_down[0])
    y_p = _combine(table[:ntp], lpos[:, :tp], pr["wt"], pr["x2"], pr["gate_f"], ypad).reshape(bp, sp, d)
    y_s = _combine(table[ntp:], lpos[:, tp:], sa["wt"], sa["x2"], sa["gate_f"], ypad).reshape(bs, ss, d)

    return (y_p, y_s,
            pr["lat"][None], pr["kr"][None], _state_from_pairs(pr["s_fin"])[None],
            sa["lat"][None], sa["kr"][None], _state_from_pairs(sa["s_fin"])[None])
```

```python
import functools

import numpy as np
import jax
import jax.numpy as jnp
from jax import lax
from jax.experimental import pallas as pl
from jax.experimental.pallas import tpu as pltpu

F32 = jnp.float32
BF16 = jnp.bfloat16
I32 = jnp.int32

CHUNK = 64
EPS = 1e-6
HEADS = 4
Q_LORA = 384
KV_LORA = 256
NOPE = 128
ROPE = 64
HALF = ROPE // 2
V_DIM = 128
QK = NOPE + ROPE
QK_PAD = 256
ROPE_THETA = 10000.0
GLA_DK = 64
GLA_DV = 128
GATE_RANK = 16
GATE_NORM = 16.0
N_EXPERTS = 32
TOP_K = 4
SWIGLU_LIMIT = 7.0
SWIGLU_ALPHA = 1.702
NEG = -1e30
LOG2_E = 1.4426950408889634

LANES = 128
BF16_ROWS = 16
ROUTE_TILE = 512
RUN_CHUNK = BF16_ROWS
SORT_ROWS = 512
CHUNK_UNROLL = 4
EXPERT_ROWS = 256
ATTN_TILE = 1024
VMEM_BIG = 56 * 1024 * 1024


def _cparams(sem, vmem=None):
    return pltpu.CompilerParams(dimension_semantics=sem, vmem_limit_bytes=vmem)


def _nt(a, b):
    return lax.dot_general(a, b, (((1,), (1,)), ((), ())), preferred_element_type=F32)


def _rms(x, width):
    return lax.rsqrt(jnp.sum(x * x, axis=-1, keepdims=True) * (1.0 / width) + EPS)


def _round_up(x, m):
    return ((x + m - 1) // m) * m


def _ada_body(c_ref, w_ref, b_ref, o_ref):
    c = c_ref[...]
    s = (c * jax.nn.sigmoid(c)).astype(BF16)
    o_ref[...] = jnp.dot(s, w_ref[...].astype(BF16), preferred_element_type=F32) + b_ref[...]


def _ada(c, w_ada, b_ada):
    r, d = c.shape
    n = w_ada.shape[1]
    tn = 1536 if n % 1536 == 0 else n
    return pl.pallas_call(
        _ada_body,
        grid=(n // tn,),
        in_specs=[pl.BlockSpec((r, d), lambda j: (0, 0)),
                  pl.BlockSpec((d, tn), lambda j: (0, j)),
                  pl.BlockSpec((1, tn), lambda j: (0, j))],
        out_specs=pl.BlockSpec((r, tn), lambda j: (0, j)),
        out_shape=jax.ShapeDtypeStruct((r, n), F32),
        compiler_params=_cparams(("arbitrary",), 40 * 1024 * 1024),
        name="ada",
    )(c, w_ada, b_ada.reshape(1, n))


_SEG = dict(qa_kr=(0, 512), kva=(512, 768), gq=(768, 1024), gk=(1024, 1280),
            gv=(1280, 1792), gr=(1792, 2304), glr=(2304, 2432))
_W1_COLS = 2432


def _proj_body(pos0, ts, x_ref, sh_ref, sc_ref, gmix_ref, w1_ref, gqa_ref, wuq_ref, gkv_ref, gqk_ref,
               rope_ref, wg2_ref, bg2_ref,
               q_ref, lat_ref, kr_ref, gq_o, gk_o, gv_o, gl_o, gr_o, trig_scr):
    i = pl.program_id(1)
    x = x_ref[...]
    d = x.shape[-1]
    h = (x * _rms(x, d) * gmix_ref[...]) * (1.0 + sc_ref[...]) + sh_ref[...]
    hb = h.astype(BF16)

    def seg(name):
        a, b = _SEG[name]
        return jnp.dot(hb, w1_ref[:, a:b], preferred_element_type=F32)

    @pl.when((pl.program_id(0) == 0) & (i == 0))
    def _():
        row_ang = lax.broadcasted_iota(I32, (ts, LANES), 0).astype(F32) * rope_ref[0:1, :]
        trig_scr[0] = jnp.cos(row_ang)
        trig_scr[1] = jnp.sin(row_ang)

    base_ang = jnp.broadcast_to((pos0 + i * ts).astype(F32) * rope_ref[0:1, :], (8, LANES))
    cos_a, sin_a = jnp.cos(base_ang)[0:1, :], jnp.sin(base_ang)[0:1, :]
    cos = cos_a * trig_scr[0] - sin_a * trig_scr[1]
    sin = (sin_a * trig_scr[0] + cos_a * trig_scr[1]) * rope_ref[1:2, :]
    lane = lax.broadcasted_iota(I32, (ts, LANES), 1)
    first_half = (lane & HALF) == 0
    low64 = lane < ROPE

    def rope(v):
        partner = jnp.where(first_half, pltpu.roll(v, LANES - HALF, 1), pltpu.roll(v, HALF, 1))
        return v * cos + partner * sin

    qa_kr = seg("qa_kr")
    qa = qa_kr[:, 0:Q_LORA]
    qn = (qa * _rms(qa, Q_LORA) * gqa_ref[...]).astype(BF16)
    qf = jnp.dot(qn, wuq_ref[...], preferred_element_type=F32)
    rope_blocks = (rope(qf[:, 4 * NOPE:4 * NOPE + LANES]), rope(qf[:, 4 * NOPE + LANES:4 * NOPE + 2 * LANES]))
    for hd in range(HEADS):
        nope = qf[:, NOPE * hd:NOPE * (hd + 1)]
        blk = rope_blocks[hd // 2]
        if hd % 2:
            blk = pltpu.roll(blk, ROPE, 1)
        blk = jnp.where(low64, blk, 0.0)
        ss = jnp.sum(nope * nope, axis=-1, keepdims=True) + jnp.sum(blk * blk, axis=-1, keepdims=True)
        scl = lax.rsqrt(ss * (1.0 / QK) + EPS) * (QK ** -0.5 * LOG2_E)
        q_ref[hd, :, 0:NOPE] = (nope * scl * gqk_ref[0:1, :]).astype(BF16)
        q_ref[hd, :, NOPE:QK_PAD] = (blk * scl * gqk_ref[1:2, :]).astype(BF16)

    kva = seg("kva")
    lat_ref[...] = kva * _rms(kva, KV_LORA) * gkv_ref[...]
    kr_ref[...] = rope(qa_kr[:, Q_LORA:Q_LORA + LANES])[:, 0:ROPE]

    gq_o[...] = seg("gq") * (GLA_DK ** -0.5)
    gk_o[...] = seg("gk")
    gv_o[...] = seg("gv").astype(BF16)
    gr_o[...] = seg("gr")
    z = jnp.dot(seg("glr").astype(BF16), wg2_ref[...], preferred_element_type=F32) + bg2_ref[...]
    gl_o[...] = (jnp.minimum(z, 0.0) - jnp.log1p(jnp.exp(-jnp.abs(z)))) * (1.0 / GATE_NORM)


def _proj(x, shift, scale, pos0, wts):
    b, s, d = x.shape
    ts = min(s, 512)
    row = lambda a: pl.BlockSpec(a.shape, lambda bi, i: (0,) * a.ndim)
    tok = lambda w: pl.BlockSpec((None, ts, w), lambda bi, i: (bi, i, 0))
    mod = pl.BlockSpec((None, 1, d), lambda bi, i: (bi, 0, 0))
    small = [wts["g_mix"], wts["w1"], wts["g_qa"], wts["w_uq"], wts["g_kv"], wts["g_qk_q"], wts["rope"],
             wts["w_g2"], wts["b_g2"]]
    out_shape = (
        jax.ShapeDtypeStruct((b, HEADS, s, QK_PAD), BF16),
        jax.ShapeDtypeStruct((b, s, KV_LORA), F32),
        jax.ShapeDtypeStruct((b, s, ROPE), F32),
        jax.ShapeDtypeStruct((b, s, HEADS * GLA_DK), F32),
        jax.ShapeDtypeStruct((b, s, HEADS * GLA_DK), F32),
        jax.ShapeDtypeStruct((b, s, HEADS * GLA_DV), BF16),
        jax.ShapeDtypeStruct((b, s, HEADS * GLA_DK), F32),
        jax.ShapeDtypeStruct((b, s, HEADS * GLA_DV), F32),
    )
    out_specs = (
        pl.BlockSpec((None, HEADS, ts, QK_PAD), lambda bi, i: (bi, 0, i, 0)),
        tok(KV_LORA), tok(ROPE), tok(HEADS * GLA_DK), tok(HEADS * GLA_DK), tok(HEADS * GLA_DV),
        tok(HEADS * GLA_DK), tok(HEADS * GLA_DV),
    )
    return pl.pallas_call(
        functools.partial(_proj_body, pos0, ts),
        grid=(b, s // ts),
        in_specs=[tok(d), mod, mod] + [row(a) for a in small],
        out_specs=out_specs,
        out_shape=out_shape,
        scratch_shapes=[pltpu.VMEM((2, ts, LANES), F32)],
        compiler_params=_cparams(("arbitrary", "arbitrary"), VMEM_BIG),
        name="proj",
    )(x, shift, scale, *small)


def _key_rows(lat, kr, wk_ref, gk_ref, k_out):
    kn_all = jnp.dot(lat, wk_ref[...], preferred_element_type=F32)
    kr_ss = jnp.sum(kr * kr, axis=-1, keepdims=True)
    for hd in range(HEADS):
        kn = kn_all[:, NOPE * hd:NOPE * (hd + 1)]
        scl = lax.rsqrt((jnp.sum(kn * kn, axis=-1, keepdims=True) + kr_ss) * (1.0 / QK) + EPS)
        k_out[hd, :, 0:NOPE] = (kn * scl * gk_ref[0:1, :]).astype(BF16)
        k_out[hd, :, NOPE:QK] = (kr * scl * gk_ref[1:2, 0:ROPE]).astype(BF16)
        k_out[hd, :, QK:QK_PAD] = jnp.zeros((kr.shape[0], QK_PAD - QK), BF16)


def _kv_body(lat_ref, kr_ref, wk_ref, wv_ref, gk_ref, k_ref, v_ref):
    lat = lat_ref[...].astype(BF16)
    _key_rows(lat, kr_ref[...], wk_ref, gk_ref, k_ref)
    v_t = _nt(wv_ref[...], lat)
    for hd in range(HEADS):
        v_ref[hd] = v_t[V_DIM * hd:V_DIM * (hd + 1), :].astype(BF16)


def _kv(lat, kr, w_uk, w_uv_t, g_qk_k):
    b, s, _ = lat.shape
    ts = min(s, ATTN_TILE)
    return pl.pallas_call(
        _kv_body,
        grid=(b, s // ts),
        in_specs=[pl.BlockSpec((None, ts, KV_LORA), lambda bi, i: (bi, i, 0)),
                  pl.BlockSpec((None, ts, ROPE), lambda bi, i: (bi, i, 0)),
                  pl.BlockSpec(w_uk.shape, lambda bi, i: (0, 0)),
                  pl.BlockSpec(w_uv_t.shape, lambda bi, i: (0, 0)),
                  pl.BlockSpec(g_qk_k.shape, lambda bi, i: (0, 0))],
        out_specs=(pl.BlockSpec((None, HEADS, ts, QK_PAD), lambda bi, i: (bi, 0, i, 0)),
                   pl.BlockSpec((None, HEADS, None, V_DIM, ts), lambda bi, i: (bi, 0, i, 0, 0))),
        out_shape=(jax.ShapeDtypeStruct((b, HEADS, s, QK_PAD), BF16),
                   jax.ShapeDtypeStruct((b, HEADS, s // ts, V_DIM, ts), BF16)),
        compiler_params=_cparams(("arbitrary", "arbitrary")),
        name="kv",
    )(lat, kr, w_uk, w_uv_t, g_qk_k)


def _attn_prompt_body(t, q_ref, k_ref, vt_ref, o_ref, s_a, s_b):
    i = pl.program_id(2)
    q = q_ref[...]

    def scores(j, buf):
        buf[...] = _nt(k_ref[pl.ds(pl.multiple_of(j * t, t), t), :], q)

    def consume(j, buf, carry, masked=False):
        m, l, acc = carry
        s = buf[...]
        if masked:
            visible = (lax.broadcasted_iota(I32, (t, t), 0) // CHUNK) <= (lax.broadcasted_iota(I32, (t, t), 1) // CHUNK)
            s = jnp.where(visible, s, NEG)
        m_new = jnp.maximum(m, jnp.max(s, axis=0, keepdims=True))
        alpha = jnp.exp2(m - m_new)
        p = jnp.exp2(s - m_new)
        l = alpha * l + jnp.sum(p, axis=0, keepdims=True)
        acc = alpha * acc + jnp.dot(vt_ref[j], p.astype(BF16), preferred_element_type=F32)
        return m_new, l, acc

    def pair(pp, carry):
        j = 2 * pp
        scores(j + 1, s_b)
        carry = consume(j, s_a, carry)
        scores(j + 2, s_a)
        return consume(j + 1, s_b, carry)

    def odd_tail(carry):
        scores(i, s_b)
        carry = consume(i - 1, s_a, carry)
        return consume(i, s_b, carry, masked=True)

    scores(0, s_a)
    carry = (jnp.full((1, t), NEG, F32), jnp.zeros((1, t), F32), jnp.zeros((V_DIM, t), F32))
    carry = lax.fori_loop(0, i // 2, pair, carry)
    _, l, acc = lax.cond(i % 2 == 1, odd_tail, lambda c: consume(i, s_a, c, masked=True), carry)
    o_ref[...] = (acc / l).T.astype(BF16)


def _attn_prompt(q, k, v_t):
    b, _, s, _ = q.shape
    t = v_t.shape[-1]
    return pl.pallas_call(
        functools.partial(_attn_prompt_body, t),
        grid=(b, HEADS, s // t),
        in_specs=[pl.BlockSpec((None, None, t, QK_PAD), lambda bi, h, i: (bi, h, i, 0)),
                  pl.BlockSpec((None, None, s, QK_PAD), lambda bi, h, i: (bi, h, 0, 0)),
                  pl.BlockSpec((None, None, s // t, V_DIM, t), lambda bi, h, i: (bi, h, 0, 0, 0))],
        out_specs=pl.BlockSpec((None, t, V_DIM), lambda bi, h, i: (bi, i, h)),
        out_shape=jax.ShapeDtypeStruct((b, s, HEADS * V_DIM), BF16),
        scratch_shapes=[pltpu.VMEM((t, t), F32), pltpu.VMEM((t, t), F32)],
        compiler_params=_cparams(("arbitrary", "arbitrary", "arbitrary"), VMEM_BIG),
        name="attn_prompt",
    )(q, k, v_t)


def _attn_sample_body(past, sq, q_ref, plat_ref, pkr_ref, nlat_ref, nkr_ref, wk_ref, wv_ref, gk_ref, o_ref,
                      kp_scr, kn_scr):
    plat = plat_ref[...].astype(BF16)
    nlat = nlat_ref[...].astype(BF16)
    _key_rows(plat, pkr_ref[...], wk_ref, gk_ref, kp_scr)
    _key_rows(nlat, nkr_ref[...], wk_ref, gk_ref, kn_scr)
    vp_t = _nt(wv_ref[...], plat).astype(BF16)
    vn_t = _nt(wv_ref[...], nlat).astype(BF16)
    key_chunk = (past + lax.broadcasted_iota(I32, (sq, sq), 0)) // CHUNK
    qry_chunk = (past + lax.broadcasted_iota(I32, (sq, sq), 1)) // CHUNK
    for hd in range(HEADS):
        q = q_ref[hd]
        s_p = _nt(kp_scr[hd], q)
        s_n = jnp.where(key_chunk <= qry_chunk, _nt(kn_scr[hd], q), NEG)
        m = jnp.maximum(jnp.max(s_p, axis=0, keepdims=True), jnp.max(s_n, axis=0, keepdims=True))
        p_p = jnp.exp2(s_p - m)
        p_n = jnp.exp2(s_n - m)
        l = jnp.sum(p_p, axis=0, keepdims=True) + jnp.sum(p_n, axis=0, keepdims=True)
        rows = slice(V_DIM * hd, V_DIM * (hd + 1))
        o_t = (jnp.dot(vp_t[rows, :], p_p.astype(BF16), preferred_element_type=F32)
               + jnp.dot(vn_t[rows, :], p_n.astype(BF16), preferred_element_type=F32))
        o_ref[:, rows] = (o_t / l).T.astype(BF16)


def _attn_sample(q, past_lat, past_kr, lat, kr, w_uk, w_uv_t, g_qk_k):
    b, _, sq, _ = q.shape
    past = past_lat.shape[1]
    rows = lambda n, w: pl.BlockSpec((None, n, w), lambda bi: (bi, 0, 0))
    full = lambda a: pl.BlockSpec(a.shape, lambda bi: (0,) * a.ndim)
    return pl.pallas_call(
        functools.partial(_attn_sample_body, past, sq),
        grid=(b,),
        in_specs=[pl.BlockSpec((None, HEADS, sq, QK_PAD), lambda bi: (bi, 0, 0, 0)),
                  rows(past, KV_LORA), rows(past, ROPE), rows(sq, KV_LORA), rows(sq, ROPE),
                  full(w_uk), full(w_uv_t), full(g_qk_k)],
        out_specs=rows(sq, HEADS * V_DIM),
        out_shape=jax.ShapeDtypeStruct((b, sq, HEADS * V_DIM), BF16),
        scratch_shapes=[pltpu.VMEM((HEADS, past, QK_PAD), BF16), pltpu.VMEM((HEADS, sq, QK_PAD), BF16)],
        compiler_params=_cparams(("arbitrary",), 40 * 1024 * 1024),
        name="attn_sample",
    )(q, past_lat, past_kr, lat, kr, w_uk, w_uv_t, g_qk_k)


def _gla_masks(c):
    idx = np.arange(c)
    le = idx[None, :] <= idx[:, None]
    gt = idx[None, :] > idx[:, None]
    return np.concatenate([le, gt], axis=0).astype(np.float32), int(np.log2(c))


def _level_exponents(b, g, c, level):
    n = c >> level
    row = lax.broadcasted_iota(I32, (c, 1), 0)
    if n >= 8:
        split = b.reshape(c // n, n, LANES)[:, n // 2 - 1:n // 2, :]
        split = jnp.broadcast_to(split, (c // n, n, LANES)).reshape(c, LANES)
        return jnp.where((row & (n // 2)) != 0, b - split, split - b)
    g_prev = pltpu.roll(g, 1, 0)
    g_next = pltpu.roll(g, c - 1, 0)
    if n == 4:
        r = row & 3
        return jnp.where(r == 0, g_next, jnp.where(r == 1, 0.0, jnp.where(r == 2, g, g + g_prev)))
    assert n == 2
    return jnp.where((row & 1) != 0, g, 0.0)


def _gla_body(c, n_chunks, levels, mall_ref, q_ref, k_ref, g_ref, v_ref, r_ref, s0_ref, gout_ref,
              o_ref, sfin_ref, st_scr):
    it = pl.program_id(1)

    @pl.when(it == 0)
    def _():
        st_scr[...] = s0_ref[...]

    lane = lax.broadcasted_iota(I32, (c, LANES), 1)
    head_lanes = (lane < GLA_DK, lane >= GLA_DK)
    st_lane_lo = lax.broadcasted_iota(I32, (GLA_DV, LANES), 1) < GLA_DK
    row = lax.broadcasted_iota(I32, (c, 1), 0)
    ri = lax.broadcasted_iota(I32, (c, c), 0)
    ci = lax.broadcasted_iota(I32, (c, c), 1)
    mall = mall_ref[...]

    for ch in range(n_chunks):
        rows = slice(ch * c, (ch + 1) * c)
        for p in range(HEADS // 2):
            ls = slice(LANES * p, LANES * (p + 1))
            g = g_ref[rows, ls]
            q = q_ref[rows, ls]
            k = k_ref[rows, ls]
            g_hi = g.astype(BF16)
            g_lo = (g - g_hi.astype(F32)).astype(BF16)
            e2 = jnp.dot(mall, jnp.concatenate([g_hi, g_lo], axis=1), preferred_element_type=F32)
            e = e2[:, 0:LANES] + e2[:, LANES:2 * LANES]
            b = e[0:c]
            eb = jnp.exp(b)
            qb = q * eb
            kd = (k * jnp.exp(e[c:2 * c])).astype(BF16)
            d_last = eb[c - 1:c, :]
            st = st_scr[p]
            st_b = st.astype(BF16)
            qs, ks = [q], [k.astype(BF16)]
            for l in range(levels):
                bottom = (row & (c >> (l + 1))) != 0
                decay = jnp.exp(_level_exponents(b, g, c, l))
                qs.append(jnp.where(bottom, q * decay, 0.0))
                ks.append(jnp.where(bottom, 0.0, k * decay).astype(BF16))
            upd = []
            for hh in range(2):
                hd = 2 * p + hh
                sel = head_lanes[hh]
                a = jnp.where(ri == ci, _nt(jnp.where(sel, qs[0], 0.0).astype(BF16), ks[0]), 0.0)
                for l in range(levels):
                    pr = _nt(jnp.where(sel, qs[l + 1], 0.0).astype(BF16), ks[l + 1])
                    if l > 0:
                        pr = jnp.where((ri ^ ci) < (c >> l), pr, 0.0)
                    a = a + pr
                vh = v_ref[rows, GLA_DV * hd:GLA_DV * (hd + 1)]
                o = jnp.dot(a.astype(BF16), vh, preferred_element_type=F32)
                o = o + _nt(jnp.where(sel, qb, 0.0).astype(BF16), st_b)
                on = o * _rms(o, GLA_DV) * gout_ref[...]
                r = r_ref[rows, GLA_DV * hd:GLA_DV * (hd + 1)]
                o_ref[rows, GLA_DV * hd:GLA_DV * (hd + 1)] = (on * (r * jax.nn.sigmoid(r))).astype(BF16)
                upd.append(lax.dot_general(vh, kd, (((0,), (0,)), ((), ())), preferred_element_type=F32))
            st_scr[p] = st * d_last + jnp.where(st_lane_lo, upd[0], upd[1])

    @pl.when(it == pl.num_programs(1) - 1)
    def _():
        sfin_ref[...] = st_scr[...]


def _gla(gq, gk, gl, gv, gr, s0, g_out):
    b, s, _ = gq.shape
    c = min(CHUNK, s)
    tile = min(s, 8 * c)
    masks, levels = _gla_masks(c)
    mall = jnp.asarray(masks, BF16)
    tok = lambda w: pl.BlockSpec((None, tile, w), lambda bi, i: (bi, i, 0))
    st_spec = pl.BlockSpec((None, HEADS // 2, GLA_DV, LANES), lambda bi, i: (bi, 0, 0, 0))
    return pl.pallas_call(
        functools.partial(_gla_body, c, tile // c, levels),
        grid=(b, s // tile),
        in_specs=[pl.BlockSpec(mall.shape, lambda bi, i: (0, 0)),
                  tok(HEADS * GLA_DK), tok(HEADS * GLA_DK), tok(HEADS * GLA_DK), tok(HEADS * GLA_DV),
                  tok(HEADS * GLA_DV), st_spec, pl.BlockSpec(g_out.shape, lambda bi, i: (0, 0))],
        out_specs=(tok(HEADS * GLA_DV), st_spec),
        out_shape=(jax.ShapeDtypeStruct((b, s, HEADS * GLA_DV), BF16),
                   jax.ShapeDtypeStruct((b, HEADS // 2, GLA_DV, LANES), F32)),
        scratch_shapes=[pltpu.VMEM((HEADS // 2, GLA_DV, LANES), F32)],
        compiler_params=_cparams(("arbitrary", "arbitrary")),
        name="gla",
    )(mall, gq, gk, gl, gv, gr, s0, g_out)


def _state_to_pairs(s):
    b = s.shape[0]
    s = s.reshape(b, HEADS // 2, 2, GLA_DK, GLA_DV)
    return jnp.transpose(s, (0, 1, 4, 2, 3)).reshape(b, HEADS // 2, GLA_DV, 2 * GLA_DK)


def _state_from_pairs(s):
    b = s.shape[0]
    s = s.reshape(b, HEADS // 2, GLA_DV, 2, GLA_DK)
    return jnp.transpose(s, (0, 1, 3, 4, 2)).reshape(b, HEADS, GLA_DK, GLA_DV)


def _post_body(x_ref, om_ref, og_ref, gt_ref, sc_ref, sh_ref, wo_ref, gffn_ref, wr_ref, br_ref,
               x2_ref, h_ref, idx_ref, wt_ref):
    half = om_ref.shape[-1]
    mix = (jnp.dot(om_ref[...], wo_ref[0:half, :], preferred_element_type=F32)
           + jnp.dot(og_ref[...], wo_ref[half:2 * half, :], preferred_element_type=F32))
    x2 = x_ref[...] + gt_ref[...] * mix
    x2_ref[...] = x2
    d = x2.shape[-1]
    h = (x2 * _rms(x2, d) * gffn_ref[...]) * (1.0 + sc_ref[...]) + sh_ref[...]
    h_hi = h.astype(BF16)
    h_ref[...] = h_hi
    h_lo = (h - h_hi.astype(F32)).astype(BF16)
    logits = _nt(wr_ref[0], h_hi) + _nt(wr_ref[0], h_lo) + _nt(wr_ref[1], h_hi) + br_ref[...]
    n_exp, tm = logits.shape
    eid = lax.broadcasted_iota(I32, (n_exp, tm), 0)
    vals, tops, ids = logits, [], []
    for _ in range(TOP_K):
        m = jnp.max(vals, axis=0, keepdims=True)
        sel = jnp.min(jnp.where(vals == m, eid, n_exp), axis=0, keepdims=True)
        tops.append(m)
        ids.append(sel)
        vals = jnp.where(eid == sel, -jnp.inf, vals)
    es = [jnp.exp(t - tops[0]) for t in tops]
    tot = es[0] + es[1] + es[2] + es[3]
    idx_ref[...] = jnp.concatenate(ids, axis=0)
    wt_ref[...] = jnp.concatenate([e / tot for e in es], axis=0)


def _post(x, om, og, gate, scale, shift, w_o, g_ffn, w_r2, b_r):
    t, d = x.shape
    tm = min(t, 512)
    per_tok = gate.shape[0] == t
    mod = pl.BlockSpec((tm, d), lambda i: (i, 0)) if per_tok else pl.BlockSpec((1, d), lambda i: (0, 0))
    tok = lambda w: pl.BlockSpec((tm, w), lambda i: (i, 0))
    full = lambda a: pl.BlockSpec(a.shape, lambda i: (0,) * a.ndim)
    return pl.pallas_call(
        _post_body,
        grid=(t // tm,),
        in_specs=[tok(d), tok(om.shape[1]), tok(og.shape[1]), mod, mod, mod, full(w_o), full(g_ffn), full(w_r2),
                  full(b_r)],
        out_specs=(tok(d), tok(d),
                   pl.BlockSpec((TOP_K, tm), lambda i: (0, i)), pl.BlockSpec((TOP_K, tm), lambda i: (0, i))),
        out_shape=(jax.ShapeDtypeStruct((t, d), F32), jax.ShapeDtypeStruct((t, d), BF16),
                   jax.ShapeDtypeStruct((TOP_K, t), I32), jax.ShapeDtypeStruct((TOP_K, t), F32)),
        compiler_params=_cparams(("arbitrary",), 40 * 1024 * 1024),
        name="post",
    )(x, om, og, gate, scale, shift, w_o, g_ffn, w_r2, b_r)


def _rank_body(idx_ref, rank_ref, cnt_ref):
    idx = idx_ref[...]
    tg = idx.shape[1]
    eid = lax.broadcasted_iota(I32, (N_EXPERTS, tg), 0)
    hits = [eid == idx[k:k + 1, :] for k in range(TOP_K)]
    member = jnp.zeros((N_EXPERTS, tg), F32)
    for hk in hits:
        member = member + jnp.where(hk, 1.0, 0.0)
    before = (lax.broadcasted_iota(I32, (tg, tg), 0) < lax.broadcasted_iota(I32, (tg, tg), 1))
    prefix = jnp.dot(member.astype(BF16), jnp.where(before, 1.0, 0.0).astype(BF16), preferred_element_type=F32)
    rank_ref[...] = jnp.concatenate(
        [jnp.sum(jnp.where(hk, prefix, 0.0), axis=0, keepdims=True) for hk in hits], axis=0).astype(I32)
    cnt_ref[...] = jnp.broadcast_to(jnp.sum(member, axis=1, keepdims=True), (N_EXPERTS, LANES)).astype(I32)


def _rank(idx):
    t = idx.shape[1]
    nt = t // ROUTE_TILE
    return pl.pallas_call(
        _rank_body,
        grid=(nt,),
        in_specs=[pl.BlockSpec((TOP_K, ROUTE_TILE), lambda i: (0, i))],
        out_specs=(pl.BlockSpec((TOP_K, ROUTE_TILE), lambda i: (0, i)),
                   pl.BlockSpec((None, N_EXPERTS, LANES), lambda i: (i, 0, 0))),
        out_shape=(jax.ShapeDtypeStruct((TOP_K, t), I32), jax.ShapeDtypeStruct((nt, N_EXPERTS, LANES), I32)),
        compiler_params=_cparams(("arbitrary",)),
        name="rank",
    )(idx)


def _route_tables(idx, lrank, cnt3):
    nt = cnt3.shape[0]
    t = idx.shape[1]
    cnt = cnt3[:, :, 0]
    run = _round_up(cnt, RUN_CHUNK)
    lo_end = jnp.cumsum(run, axis=1)
    lo = lo_end - run
    n_chunks = lo_end[:, -1] // RUN_CHUNK
    region = _round_up(jnp.sum(run, axis=0), EXPERT_ROWS)
    g_end = jnp.cumsum(region)
    run_dest = (g_end - region)[None, :] + jnp.cumsum(run, axis=0) - run
    max_rows = TOP_K * ROUTE_TILE + N_EXPERTS * (RUN_CHUNK - 1)
    n_tab = _round_up(max_rows, RUN_CHUNK) // RUN_CHUNK
    c_start = jnp.arange(n_tab, dtype=I32) * RUN_CHUNK
    e_of_c = jnp.minimum(jnp.sum(lo_end[:, None, :] <= c_start[None, :, None], axis=2), N_EXPERTS - 1)
    pick = e_of_c[:, :, None] == jnp.arange(N_EXPERTS, dtype=I32)[None, None, :]
    chunk_dest = jnp.sum(jnp.where(pick, (run_dest - lo)[:, None, :], 0), axis=2) + c_start[None, :]
    table = jnp.concatenate([chunk_dest, n_chunks[:, None]], axis=1).astype(I32).reshape(nt, 1, n_tab + 1)
    eid = jnp.arange(N_EXPERTS, dtype=I32)[:, None]
    lo_tok = jnp.repeat(lo.T, ROUTE_TILE, axis=1)
    lpos = jnp.stack([jnp.sum(jnp.where(idx[k][None, :] == eid, lo_tok, 0), axis=0) for k in range(TOP_K)])
    lpos = (lpos + lrank).astype(I32)
    n_blocks = _round_up(t * TOP_K + nt * N_EXPERTS * (RUN_CHUNK - 1), EXPERT_ROWS) // EXPERT_ROWS + N_EXPERTS
    b_start = jnp.arange(n_blocks, dtype=I32) * EXPERT_ROWS
    blk_e = jnp.minimum(jnp.sum(g_end[None, :] <= b_start[:, None], axis=1), N_EXPERTS - 1).astype(I32)
    n_valid = (g_end[-1:] // EXPERT_ROWS).astype(I32)
    used = region > 0
    e_ids = jnp.arange(N_EXPERTS, dtype=I32)
    later_used = used[None, :] & (e_ids[None, :] > e_ids[:, None])
    nxt_e = jnp.min(jnp.where(later_used, e_ids[None, :], N_EXPERTS), axis=1)
    nxt_e = jnp.where(nxt_e < N_EXPERTS, nxt_e, -1).astype(I32)
    ord_e = (jnp.cumsum(used.astype(I32)) - 1).astype(I32)
    tail = jnp.concatenate([jnp.where(used, g_end - EXPERT_ROWS, -1), n_valid]).astype(I32)
    tail = tail.reshape(1, N_EXPERTS + 1)
    return dict(table=table, lpos=lpos, blk_e=blk_e, n_valid=n_valid, nxt_e=nxt_e, ord_e=ord_e, tail=tail,
                n_blocks=n_blocks)


def _chunk_copy(src, dst, sem):
    return pltpu.make_async_copy(src, dst, sem)


def _for_chunks(n, body):
    groups = n // CHUNK_UNROLL

    def group(g, carry):
        for u in range(CHUNK_UNROLL):
            body(g * CHUNK_UNROLL + u)
        return carry

    def single(c, carry):
        body(c)
        return carry

    lax.fori_loop(0, groups, group, 0)
    lax.fori_loop(groups * CHUNK_UNROLL, n, single, 0)


def _scatter_body(n_tab, nt_a, n_blocks, tab_ref, prv_ref, tail_ref, lpos_ref, ha_ref, hb_ref, xout_ref, sorted_scr,
                  zero_scr, sems, zero_sem):
    i = pl.program_id(0)
    slot = i % 2
    n_chunks = tab_ref[0, n_tab]
    tt = ha_ref.shape[0]

    @pl.when(i == 0)
    def _():
        zero_scr[...] = jnp.zeros_like(zero_scr)
        n_valid = tail_ref[0, N_EXPERTS]

        def block(start):
            return xout_ref.at[pl.ds(pl.multiple_of(start, EXPERT_ROWS), EXPERT_ROWS)]

        for e in range(N_EXPERTS):
            @pl.when(tail_ref[0, e] >= 0)
            def _():
                _chunk_copy(zero_scr, block(tail_ref[0, e]), zero_sem).start()

        def fill(b, carry):
            _chunk_copy(zero_scr, block(b * EXPERT_ROWS), zero_sem).start()
            return carry

        def fill_done(b, carry):
            _chunk_copy(zero_scr, block(0), zero_sem).wait()
            return carry

        lax.fori_loop(n_valid, n_blocks, fill, 0)
        for e in range(N_EXPERTS):
            @pl.when(tail_ref[0, e] >= 0)
            def _():
                _chunk_copy(zero_scr, block(0), zero_sem).wait()
        lax.fori_loop(n_valid, n_blocks, fill_done, 0)

    lp = lpos_ref[...]
    h = jnp.where(i < nt_a, ha_ref[...], hb_ref[...])

    def sort_block(rb, carry):
        r0 = pl.multiple_of(rb * SORT_ROWS, SORT_ROWS)
        rid = r0 + lax.broadcasted_iota(I32, (SORT_ROWS, tt), 0)
        onehot = jnp.zeros((SORT_ROWS, tt), F32)
        for k in range(TOP_K):
            onehot = onehot + jnp.where(lp[k:k + 1, :] == rid, 1.0, 0.0)
        sorted_scr[slot, pl.ds(r0, SORT_ROWS), :] = jnp.dot(onehot.astype(BF16), h,
                                                            preferred_element_type=F32).astype(BF16)
        return carry

    lax.fori_loop(0, (n_chunks * RUN_CHUNK + SORT_ROWS - 1) // SORT_ROWS, sort_block, 0)

    def issue(c):
        src = sorted_scr.at[slot, pl.ds(pl.multiple_of(c * RUN_CHUNK, RUN_CHUNK), RUN_CHUNK)]
        dst = xout_ref.at[pl.ds(pl.multiple_of(tab_ref[0, c], RUN_CHUNK), RUN_CHUNK)]
        _chunk_copy(src, dst, sems.at[slot]).start()

    def drain(sl):
        def body(c):
            _chunk_copy(sorted_scr.at[sl, pl.ds(0, RUN_CHUNK)], xout_ref.at[pl.ds(0, RUN_CHUNK)], sems.at[sl]).wait()
        return body

    _for_chunks(n_chunks, issue)

    @pl.when(i > 0)
    def _():
        _for_chunks(prv_ref[0, n_tab], drain(1 - slot))

    @pl.when(i == pl.num_programs(0) - 1)
    def _():
        _for_chunks(n_chunks, drain(slot))


def _scatter(table, tail, lpos, h_a, h_b, n_blocks):
    d = h_a.shape[1]
    nt_a, nt_b = h_a.shape[0] // ROUTE_TILE, h_b.shape[0] // ROUTE_TILE
    n_tab = table.shape[2] - 1
    sort_cap = _round_up(n_tab * RUN_CHUNK, SORT_ROWS)
    tab_spec = lambda f: pl.BlockSpec((None, 1, n_tab + 1), lambda i: (f(i), 0, 0), memory_space=pltpu.SMEM)
    return pl.pallas_call(
        functools.partial(_scatter_body, n_tab, nt_a, n_blocks),
        grid=(nt_a + nt_b,),
        in_specs=[tab_spec(lambda i: i), tab_spec(lambda i: jnp.maximum(i - 1, 0)),
                  pl.BlockSpec((1, N_EXPERTS + 1), lambda i: (0, 0), memory_space=pltpu.SMEM),
                  pl.BlockSpec((TOP_K, ROUTE_TILE), lambda i: (0, i)),
                  pl.BlockSpec((ROUTE_TILE, d), lambda i: (jnp.minimum(i, nt_a - 1), 0)),
                  pl.BlockSpec((ROUTE_TILE, d), lambda i: (jnp.maximum(i - nt_a, 0), 0))],
        out_specs=pl.BlockSpec(memory_space=pl.ANY),
        out_shape=jax.ShapeDtypeStruct((n_blocks * EXPERT_ROWS, d), BF16),
        scratch_shapes=[pltpu.VMEM((2, sort_cap, d), BF16), pltpu.VMEM((EXPERT_ROWS, d), BF16),
                        pltpu.SemaphoreType.DMA((2,)), pltpu.SemaphoreType.DMA(())],
        compiler_params=_cparams(("arbitrary",), 48 * 1024 * 1024),
        name="scatter",
    )(table, table, tail, lpos, h_a, h_b)


def _experts_body(be_ref, nv_ref, nxt_ref, ord_ref, x_ref, wgu_hbm, bgu_ref, wd_hbm, bd_ref, y_ref,
                  wgu_f, wd_f, wgu_s, wd_s, sem_gu, sem_d):
    b = pl.program_id(0)
    e = be_ref[b]
    prev = be_ref[jnp.maximum(b - 1, 0)]
    valid = b < nv_ref[0]
    d_ff = wd_s.shape[0]
    slot = ord_ref[e] % 2

    def weights(expert, sl):
        return (pltpu.make_async_copy(wgu_hbm.at[expert], wgu_f.at[sl], sem_gu.at[sl]),
                pltpu.make_async_copy(wd_hbm.at[expert], wd_f.at[sl], sem_d.at[sl]))

    @pl.when(valid & ((b == 0) | (e != prev)))
    def _():
        @pl.when(b == 0)
        def _():
            for cp in weights(e, slot):
                cp.start()

        for cp in weights(e, slot):
            cp.wait()

        @pl.when(nxt_ref[e] >= 0)
        def _():
            for cp in weights(nxt_ref[e], 1 - slot):
                cp.start()

        wgu_s[...] = wgu_f[slot].astype(BF16)
        wd_s[...] = wd_f[slot].astype(BF16)

    @pl.when(valid)
    def _():
        gu = jnp.dot(x_ref[...], wgu_s[...], preferred_element_type=F32) + bgu_ref[...]
        gate = jnp.minimum(gu[:, 0:d_ff], SWIGLU_LIMIT)
        up = jnp.clip(gu[:, d_ff:2 * d_ff], -SWIGLU_LIMIT, SWIGLU_LIMIT)
        act = ((up + 1.0) * (gate * jax.nn.sigmoid(gate * SWIGLU_ALPHA))).astype(BF16)
        y_ref[...] = (jnp.dot(act, wd_s[...], preferred_element_type=F32) + bd_ref[...]).astype(BF16)

    @pl.when(jnp.logical_not(valid))
    def _():
        y_ref[...] = jnp.zeros_like(y_ref)


def _experts(blk_e, n_valid, nxt_e, ord_e, xpad, w_gu, b_gu, w_down, b_down):
    m, d = xpad.shape
    nb = m // EXPERT_ROWS
    n_exp, _, f2 = w_gu.shape
    d_ff = w_down.shape[1]
    last = lambda b, be, nv: jnp.minimum(b, nv[0] - 1)
    grid_spec = pltpu.PrefetchScalarGridSpec(
        num_scalar_prefetch=4,
        grid=(nb,),
        in_specs=[pl.BlockSpec((EXPERT_ROWS, d), lambda b, be, nv, nx, od: (last(b, be, nv), 0)),
                  pl.BlockSpec(memory_space=pl.ANY),
                  pl.BlockSpec((None, 1, f2), lambda b, be, nv, nx, od: (be[last(b, be, nv)], 0, 0)),
                  pl.BlockSpec(memory_space=pl.ANY),
                  pl.BlockSpec((None, 1, d), lambda b, be, nv, nx, od: (be[last(b, be, nv)], 0, 0))],
        out_specs=pl.BlockSpec((EXPERT_ROWS, d), lambda b, be, nv, nx, od: (b, 0)),
        scratch_shapes=[pltpu.VMEM((2, d, f2), F32), pltpu.VMEM((2, d_ff, d), F32),
                        pltpu.VMEM((d, f2), BF16), pltpu.VMEM((d_ff, d), BF16),
                        pltpu.SemaphoreType.DMA((2,)), pltpu.SemaphoreType.DMA((2,))],
    )
    return pl.pallas_call(
        _experts_body,
        grid_spec=grid_spec,
        out_shape=jax.ShapeDtypeStruct((m, d), BF16),
        compiler_params=_cparams(("arbitrary",), VMEM_BIG),
        name="experts",
    )(blk_e, n_valid, nxt_e, ord_e, xpad, w_gu, b_gu.reshape(n_exp, 1, f2), w_down, b_down.reshape(n_exp, 1, d))


def _combine_body(n_tab, tab_ref, nxt_ref, lpos_ref, wt_ref, x2_ref, gt_ref, y_ref, o_ref, ysort_scr, sems):
    i = pl.program_id(0)
    slot = i % 2
    n_chunks = tab_ref[0, n_tab]
    tt = x2_ref.shape[0]

    def fetch(tab, sl):
        def issue(c):
            src = y_ref.at[pl.ds(pl.multiple_of(tab[0, c], RUN_CHUNK), RUN_CHUNK)]
            dst = ysort_scr.at[sl, pl.ds(pl.multiple_of(c * RUN_CHUNK, RUN_CHUNK), RUN_CHUNK)]
            _chunk_copy(src, dst, sems.at[sl]).start()

        _for_chunks(tab[0, n_tab], issue)

    @pl.when(i == 0)
    def _():
        ysort_scr[...] = jnp.zeros_like(ysort_scr)
        fetch(tab_ref, 0)

    @pl.when(i + 1 < pl.num_programs(0))
    def _():
        fetch(nxt_ref, 1 - slot)

    def drain(c):
        _chunk_copy(y_ref.at[pl.ds(0, RUN_CHUNK)], ysort_scr.at[slot, pl.ds(0, RUN_CHUNK)], sems.at[slot]).wait()

    _for_chunks(n_chunks, drain)

    lp = lpos_ref[...]
    w = wt_ref[...]
    o_ref[...] = x2_ref[...]
    gt = gt_ref[...]

    def gather_block(rb, carry):
        r0 = pl.multiple_of(rb * SORT_ROWS, SORT_ROWS)
        rid = r0 + lax.broadcasted_iota(I32, (SORT_ROWS, tt), 0)
        pw_t = jnp.zeros((SORT_ROWS, tt), F32)
        for k in range(TOP_K):
            pw_t = pw_t + jnp.where(lp[k:k + 1, :] == rid, w[k:k + 1, :], 0.0)
        ys = ysort_scr[slot, pl.ds(r0, SORT_ROWS), :]
        part = lax.dot_general(pw_t.astype(BF16), ys, (((0,), (0,)), ((), ())), preferred_element_type=F32)
        o_ref[...] = o_ref[...] + gt * part
        return carry

    lax.fori_loop(0, (n_chunks * RUN_CHUNK + SORT_ROWS - 1) // SORT_ROWS, gather_block, 0)


def _combine(table, lpos, wts, x2, gate, ypad):
    t, d = x2.shape
    nt = t // ROUTE_TILE
    n_tab = table.shape[2] - 1
    per_tok = gate.shape[0] == t
    mod = pl.BlockSpec((ROUTE_TILE, d), lambda i: (i, 0)) if per_tok else pl.BlockSpec((1, d), lambda i: (0, 0))
    sort_cap = _round_up(n_tab * RUN_CHUNK, SORT_ROWS)
    tab_spec = lambda f: pl.BlockSpec((None, 1, n_tab + 1), lambda i: (f(i), 0, 0), memory_space=pltpu.SMEM)
    return pl.pallas_call(
        functools.partial(_combine_body, n_tab),
        grid=(nt,),
        in_specs=[tab_spec(lambda i: i), tab_spec(lambda i: jnp.minimum(i + 1, nt - 1)),
                  pl.BlockSpec((TOP_K, ROUTE_TILE), lambda i: (0, i)),
                  pl.BlockSpec((TOP_K, ROUTE_TILE), lambda i: (0, i)),
                  pl.BlockSpec((ROUTE_TILE, d), lambda i: (i, 0)),
                  mod,
                  pl.BlockSpec(memory_space=pl.ANY)],
        out_specs=pl.BlockSpec((ROUTE_TILE, d), lambda i: (i, 0)),
        out_shape=jax.ShapeDtypeStruct((t, d), F32),
        scratch_shapes=[pltpu.VMEM((2, sort_cap, d), BF16), pltpu.SemaphoreType.DMA((2,))],
        compiler_params=_cparams(("arbitrary",), 48 * 1024 * 1024),
        name="combine",
    )(table, table, lpos, wts, x2, gate, ypad)


def _prep_weights(g_norm_mix, w_in, g_q_a, w_uq, g_kv_a, w_ukv, g_qk_q, g_qk_k, w_g2, b_g2, g_gla_out, w_o,
                  g_norm_ffn, w_router, b_router):
    d = w_in.shape[0]
    o_qa, o_kva, o_kr = 0, Q_LORA, Q_LORA + KV_LORA
    o_gq = o_kr + ROPE
    o_gk = o_gq + HEADS * GLA_DK
    o_gv = o_gk + HEADS * GLA_DK
    o_glr = o_gv + HEADS * GLA_DV
    o_gr = o_glr + GATE_RANK
    kr_cols = w_in[:, o_kr:o_kr + ROPE]
    w1 = jnp.concatenate([
        w_in[:, o_qa:o_kva], kr_cols, kr_cols, w_in[:, o_kva:o_kr], w_in[:, o_gq:o_gk], w_in[:, o_gk:o_gv],
        w_in[:, o_gv:o_glr], w_in[:, o_gr:o_gr + HEADS * GLA_DV], w_in[:, o_glr:o_gr],
        jnp.zeros((d, LANES - GATE_RANK), w_in.dtype)], axis=1).astype(BF16)
    assert w1.shape[1] == _W1_COLS
    wq = w_uq.reshape(Q_LORA, HEADS, QK)
    wuq = jnp.concatenate([wq[:, :, 0:NOPE].reshape(Q_LORA, HEADS * NOPE),
                           wq[:, :, NOPE:QK].reshape(Q_LORA, HEADS * ROPE)], axis=1).astype(BF16)
    wkv = w_ukv.reshape(KV_LORA, HEADS, NOPE + V_DIM)
    wuk = wkv[:, :, 0:NOPE].reshape(KV_LORA, HEADS * NOPE).astype(BF16)
    wuv_t = wkv[:, :, NOPE:].reshape(KV_LORA, HEADS * V_DIM).T.astype(BF16)
    pad_rope = lambda g: jnp.stack([g[0:NOPE], jnp.concatenate([g[NOPE:QK], jnp.zeros((QK_PAD - QK,), g.dtype)])])
    inv = ROPE_THETA ** (-jnp.arange(HALF, dtype=F32) / HALF)
    sign = jnp.concatenate([-jnp.ones((HALF,), F32), jnp.ones((HALF,), F32)])
    rope_tab = jnp.stack([jnp.tile(inv, LANES // HALF), jnp.tile(sign, LANES // ROPE)])
    wg2 = jnp.concatenate([w_g2, jnp.zeros((LANES - GATE_RANK, w_g2.shape[1]), w_g2.dtype)], axis=0).astype(BF16)
    wr_t = w_router.T
    wr_hi = wr_t.astype(BF16)
    wr_lo = (wr_t - wr_hi.astype(F32)).astype(BF16)
    return dict(
        g_mix=g_norm_mix.reshape(1, d), w1=w1, g_qa=g_q_a.reshape(1, -1), w_uq=wuq, g_kv=g_kv_a.reshape(1, -1),
        w_uk=wuk, w_uv_t=wuv_t, g_qk_q=pad_rope(g_qk_q), g_qk_k=pad_rope(g_qk_k), rope=rope_tab, w_g2=wg2,
        b_g2=b_g2.reshape(1, -1), g_out=g_gla_out.reshape(1, -1), w_o=w_o.astype(BF16),
        g_ffn=g_norm_ffn.reshape(1, d), w_r2=jnp.stack([wr_hi, wr_lo]), b_r=b_router.reshape(-1, 1))


def _mixer(x, mod, pos0, past_lat, past_kr, s0_pairs, wts):
    b, s, d = x.shape
    q, lat, kr, gq, gk, gv, gl, gr = _proj(x, mod[:, 0:1], mod[:, 1:2], pos0, wts)
    kv_w = (wts["w_uk"], wts["w_uv_t"], wts["g_qk_k"])
    if past_lat is None:
        k_new, vt_new = _kv(lat, kr, *kv_w)
        o_mla = _attn_prompt(q, k_new, vt_new)
    else:
        o_mla = _attn_sample(q, past_lat, past_kr, lat, kr, *kv_w)
    o_gla, s_fin = _gla(gq, gk, gl, gv, gr, s0_pairs, wts["g_out"])
    t = b * s
    if b == 1:
        rows = lambda j: mod[0, j:j + 1]
    else:
        rows = lambda j: jnp.broadcast_to(mod[:, j:j + 1], (b, s, d)).reshape(t, d)
    x2, h2, idx, wt = _post(x.reshape(t, d), o_mla.reshape(t, -1), o_gla.reshape(t, -1), rows(2), rows(4), rows(3),
                            wts["w_o"], wts["g_ffn"], wts["w_r2"], wts["b_r"])
    return dict(x2=x2, h2=h2, idx=idx, wt=wt, gate_f=rows(5), lat=lat, kr=kr, s_fin=s_fin)


def kernel(x_prompt, x_sample, cache_mla_latent, cache_mla_krope, state_gla, c_prompt, c_sample, w_ada, b_ada, g_norm_mix, w_in, g_q_a, w_uq, g_kv_a, w_ukv, g_qk_q, g_qk_k, w_g2, b_g2, g_gla_out, w_o, g_norm_ffn, w_router, b_router, w_gu, b_gu, w_down, b_down):
    depth = w_ada.shape[0]
    assert depth == 1, "single-layer step"
    bp, sp, d = x_prompt.shape
    bs, ss, _ = x_sample.shape
    tp, tsm = bp * sp, bs * ss
    assert tp % ROUTE_TILE == 0 and tsm % ROUTE_TILE == 0, "token counts must be whole routing tiles"
    past = cache_mla_latent.shape[2]
    layer = lambda a: a.reshape(a.shape[1:])
    wts = _prep_weights(*[layer(a) for a in (g_norm_mix, w_in, g_q_a, w_uq, g_kv_a, w_ukv, g_qk_q, g_qk_k, w_g2, b_g2,
                                             g_gla_out, w_o, g_norm_ffn, w_router, b_router)])
    w_gu, b_gu, w_down, b_down = layer(w_gu), layer(b_gu), layer(w_down), layer(b_down)

    mod = _ada(jnp.concatenate([c_prompt, c_sample], axis=0), layer(w_ada), layer(b_ada)).reshape(bp + bs, 6, d)
    zero_state = jnp.zeros((bp, HEADS // 2, GLA_DV, LANES), F32)
    pr = _mixer(x_prompt, mod[:bp], 0, None, None, zero_state, wts)
    sa = _mixer(x_sample, mod[bp:], past, layer(cache_mla_latent), layer(cache_mla_krope),
                _state_to_pairs(layer(state_gla)), wts)

    idx = jnp.concatenate([pr["idx"], sa["idx"]], axis=1)
    lrank, cnt3 = _rank(idx)
    rt = _route_tables(idx, lrank, cnt3)
    ntp = tp // ROUTE_TILE
    lpos, table = rt["lpos"], rt["table"]
    xpad = _scatter(table, rt["tail"], lpos, pr["h2"], sa["h2"], rt["n_blocks"])
    ypad = _experts(rt["blk_e"], rt["n_valid"], rt["nxt_e"], rt["ord_e"], xpad, w_gu, b_gu, w_down, b_down)
    y_p = _combine(table[:ntp], lpos[:, :tp], pr["wt"], pr["x2"], pr["gate_f"], ypad).reshape(bp, sp, d)
    y_s = _combine(table[ntp:], lpos[:, tp:], sa["wt"], sa["x2"], sa["gate_f"], ypad).reshape(bs, ss, d)

    return (y_p, y_s,
            pr["lat"][None], pr["kr"][None], _state_from_pairs(pr["s_fin"])[None],
            sa["lat"][None], sa["kr"][None], _state_from_pairs(sa["s_fin"])[None])
```

```python
import functools

import numpy as np
import jax
import jax.numpy as jnp
from jax import lax
from jax.experimental import pallas as pl
from jax.experimental.pallas import tpu as pltpu

F32 = jnp.float32
BF16 = jnp.bfloat16
I32 = jnp.int32

CHUNK = 64
EPS = 1e-6
HEADS = 4
Q_LORA = 384
KV_LORA = 256
NOPE = 128
ROPE = 64
HALF = ROPE // 2
V_DIM = 128
QK = NOPE + ROPE
QK_PAD = 256
ROPE_THETA = 10000.0
GLA_DK = 64
GLA_DV = 128
GATE_RANK = 16
GATE_NORM = 16.0
N_EXPERTS = 32
TOP_K = 4
SWIGLU_LIMIT = 7.0
SWIGLU_ALPHA = 1.702
NEG = -1e30
LOG2_E = 1.4426950408889634

LANES = 128
BF16_ROWS = 16
ROUTE_TILE = 512
RUN_CHUNK = BF16_ROWS
SORT_ROWS = 512
CHUNK_UNROLL = 4
EXPERT_ROWS = 256
ATTN_TILE = 1024
VMEM_BIG = 56 * 1024 * 1024


def _cparams(sem, vmem=None):
    return pltpu.CompilerParams(dimension_semantics=sem, vmem_limit_bytes=vmem)


def _nt(a, b):
    return lax.dot_general(a, b, (((1,), (1,)), ((), ())), preferred_element_type=F32)


def _rms(x, width):
    return lax.rsqrt(jnp.sum(x * x, axis=-1, keepdims=True) * (1.0 / width) + EPS)


def _round_up(x, m):
    return ((x + m - 1) // m) * m


def _ada_body(c_ref, w_ref, b_ref, o_ref):
    c = c_ref[...]
    s = (c * jax.nn.sigmoid(c)).astype(BF16)
    o_ref[...] = jnp.dot(s, w_ref[...].astype(BF16), preferred_element_type=F32) + b_ref[...]


def _ada(c, w_ada, b_ada):
    r, d = c.shape
    n = w_ada.shape[1]
    tn = 1536 if n % 1536 == 0 else n
    return pl.pallas_call(
        _ada_body,
        grid=(n // tn,),
        in_specs=[pl.BlockSpec((r, d), lambda j: (0, 0)),
                  pl.BlockSpec((d, tn), lambda j: (0, j)),
                  pl.BlockSpec((1, tn), lambda j: (0, j))],
        out_specs=pl.BlockSpec((r, tn), lambda j: (0, j)),
        out_shape=jax.ShapeDtypeStruct((r, n), F32),
        compiler_params=_cparams(("arbitrary",), 40 * 1024 * 1024),
        name="ada",
    )(c, w_ada, b_ada.reshape(1, n))


_SEG = dict(qa_kr=(0, 512), kva=(512, 768), gq=(768, 1024), gk=(1024, 1280),
            gv=(1280, 1792), gr=(1792, 2304), glr=(2304, 2432))
_W1_COLS = 2432


def _proj_body(pos0, ts, x_ref, sh_ref, sc_ref, gmix_ref, w1_ref, gqa_ref, wuq_ref, gkv_ref, gqk_ref,
               rope_ref, wg2_ref, bg2_ref,
               q_ref, lat_ref, kr_ref, gq_o, gk_o, gv_o, gl_o, gr_o, trig_scr):
    i = pl.program_id(1)
    x = x_ref[...]
    d = x.shape[-1]
    h = (x * _rms(x, d) * gmix_ref[...]) * (1.0 + sc_ref[...]) + sh_ref[...]
    hb = h.astype(BF16)

    def seg(name):
        a, b = _SEG[name]
        return jnp.dot(hb, w1_ref[:, a:b], preferred_element_type=F32)

    @pl.when((pl.program_id(0) == 0) & (i == 0))
    def _():
        row_ang = lax.broadcasted_iota(I32, (ts, LANES), 0).astype(F32) * rope_ref[0:1, :]
        trig_scr[0] = jnp.cos(row_ang)
        trig_scr[1] = jnp.sin(row_ang)

    base_ang = jnp.broadcast_to((pos0 + i * ts).astype(F32) * rope_ref[0:1, :], (8, LANES))
    cos_a, sin_a = jnp.cos(base_ang)[0:1, :], jnp.sin(base_ang)[0:1, :]
    cos = cos_a * trig_scr[0] - sin_a * trig_scr[1]
    sin = (sin_a * trig_scr[0] + cos_a * trig_scr[1]) * rope_ref[1:2, :]
    lane = lax.broadcasted_iota(I32, (ts, LANES), 1)
    first_half = (lane & HALF) == 0
    low64 = lane < ROPE

    def rope(v):
        partner = jnp.where(first_half, pltpu.roll(v, LANES - HALF, 1), pltpu.roll(v, HALF, 1))
        return v * cos + partner * sin

    qa_kr = seg("qa_kr")
    qa = qa_kr[:, 0:Q_LORA]
    qn = (qa * _rms(qa, Q_LORA) * gqa_ref[...]).astype(BF16)
    qf = jnp.dot(qn, wuq_ref[...], preferred_element_type=F32)
    rope_blocks = (rope(qf[:, 4 * NOPE:4 * NOPE + LANES]), rope(qf[:, 4 * NOPE + LANES:4 * NOPE + 2 * LANES]))
    for hd in range(HEADS):
        nope = qf[:, NOPE * hd:NOPE * (hd + 1)]
        blk = rope_blocks[hd // 2]
        if hd % 2:
            blk = pltpu.roll(blk, ROPE, 1)
        blk = jnp.where(low64, blk, 0.0)
        ss = jnp.sum(nope * nope, axis=-1, keepdims=True) + jnp.sum(blk * blk, axis=-1, keepdims=True)
        scl = lax.rsqrt(ss * (1.0 / QK) + EPS) * (QK ** -0.5 * LOG2_E)
        q_ref[hd, :, 0:NOPE] = (nope * scl * gqk_ref[0:1, :]).astype(BF16)
        q_ref[hd, :, NOPE:QK_PAD] = (blk * scl * gqk_ref[1:2, :]).astype(BF16)

    kva = seg("kva")
    lat_ref[...] = kva * _rms(kva, KV_LORA) * gkv_ref[...]
    kr_ref[...] = rope(qa_kr[:, Q_LORA:Q_LORA + LANES])[:, 0:ROPE]

    gq_o[...] = seg("gq") * (GLA_DK ** -0.5)
    gk_o[...] = seg("gk")
    gv_o[...] = seg("gv").astype(BF16)
    gr_o[...] = seg("gr")
    z = jnp.dot(seg("glr").astype(BF16), wg2_ref[...], preferred_element_type=F32) + bg2_ref[...]
    gl_o[...] = (jnp.minimum(z, 0.0) - jnp.log1p(jnp.exp(-jnp.abs(z)))) * (1.0 / GATE_NORM)


def _proj(x, shift, scale, pos0, wts):
    b, s, d = x.shape
    ts = min(s, 512)
    row = lambda a: pl.BlockSpec(a.shape, lambda bi, i: (0,) * a.ndim)
    tok = lambda w: pl.BlockSpec((None, ts, w), lambda bi, i: (bi, i, 0))
    mod = pl.BlockSpec((None, 1, d), lambda bi, i: (bi, 0, 0))
    small = [wts["g_mix"], wts["w1"], wts["g_qa"], wts["w_uq"], wts["g_kv"], wts["g_qk_q"], wts["rope"],
             wts["w_g2"], wts["b_g2"]]
    out_shape = (
        jax.ShapeDtypeStruct((b, HEADS, s, QK_PAD), BF16),
        jax.ShapeDtypeStruct((b, s, KV_LORA), F32),
        jax.ShapeDtypeStruct((b, s, ROPE), F32),
        jax.ShapeDtypeStruct((b, s, HEADS * GLA_DK), F32),
        jax.ShapeDtypeStruct((b, s, HEADS * GLA_DK), F32),
        jax.ShapeDtypeStruct((b, s, HEADS * GLA_DV), BF16),
        jax.ShapeDtypeStruct((b, s, HEADS * GLA_DK), F32),
        jax.ShapeDtypeStruct((b, s, HEADS * GLA_DV), F32),
    )
    out_specs = (
        pl.BlockSpec((None, HEADS, ts, QK_PAD), lambda bi, i: (bi, 0, i, 0)),
        tok(KV_LORA), tok(ROPE), tok(HEADS * GLA_DK), tok(HEADS * GLA_DK), tok(HEADS * GLA_DV),
        tok(HEADS * GLA_DK), tok(HEADS * GLA_DV),
    )
    return pl.pallas_call(
        functools.partial(_proj_body, pos0, ts),
        grid=(b, s // ts),
        in_specs=[tok(d), mod, mod] + [row(a) for a in small],
        out_specs=out_specs,
        out_shape=out_shape,
        scratch_shapes=[pltpu.VMEM((2, ts, LANES), F32)],
        compiler_params=_cparams(("arbitrary", "arbitrary"), VMEM_BIG),
        name="proj",
    )(x, shift, scale, *small)


def _key_rows(lat, kr, wk_ref, gk_ref, k_out):
    kn_all = jnp.dot(lat, wk_ref[...], preferred_element_type=F32)
    kr_ss = jnp.sum(kr * kr, axis=-1, keepdims=True)
    for hd in range(HEADS):
        kn = kn_all[:, NOPE * hd:NOPE * (hd + 1)]
        scl = lax.rsqrt((jnp.sum(kn * kn, axis=-1, keepdims=True) + kr_ss) * (1.0 / QK) + EPS)
        k_out[hd, :, 0:NOPE] = (kn * scl * gk_ref[0:1, :]).astype(BF16)
        k_out[hd, :, NOPE:QK] = (kr * scl * gk_ref[1:2, 0:ROPE]).astype(BF16)
        k_out[hd, :, QK:QK_PAD] = jnp.zeros((kr.shape[0], QK_PAD - QK), BF16)


def _kv_body(lat_ref, kr_ref, wk_ref, wv_ref, gk_ref, k_ref, v_ref):
    lat = lat_ref[...].astype(BF16)
    _key_rows(lat, kr_ref[...], wk_ref, gk_ref, k_ref)
    v_t = _nt(wv_ref[...], lat)
    for hd in range(HEADS):
        v_ref[hd] = v_t[V_DIM * hd:V_DIM * (hd + 1), :].astype(BF16)


def _kv(lat, kr, w_uk, w_uv_t, g_qk_k):
    b, s, _ = lat.shape
    ts = min(s, ATTN_TILE)
    return pl.pallas_call(
        _kv_body,
        grid=(b, s // ts),
        in_specs=[pl.BlockSpec((None, ts, KV_LORA), lambda bi, i: (bi, i, 0)),
                  pl.BlockSpec((None, ts, ROPE), lambda bi, i: (bi, i, 0)),
                  pl.BlockSpec(w_uk.shape, lambda bi, i: (0, 0)),
                  pl.BlockSpec(w_uv_t.shape, lambda bi, i: (0, 0)),
                  pl.BlockSpec(g_qk_k.shape, lambda bi, i: (0, 0))],
        out_specs=(pl.BlockSpec((None, HEADS, ts, QK_PAD), lambda bi, i: (bi, 0, i, 0)),
                   pl.BlockSpec((None, HEADS, None, V_DIM, ts), lambda bi, i: (bi, 0, i, 0, 0))),
        out_shape=(jax.ShapeDtypeStruct((b, HEADS, s, QK_PAD), BF16),
                   jax.ShapeDtypeStruct((b, HEADS, s // ts, V_DIM, ts), BF16)),
        compiler_params=_cparams(("arbitrary", "arbitrary")),
        name="kv",
    )(lat, kr, w_uk, w_uv_t, g_qk_k)


def _attn_prompt_body(t, q_ref, qn_ref, k_ref, vt_ref, o_ref, s_a, s_b):
    i = pl.program_id(2)

    def scores(q, j, buf):
        buf[...] = _nt(k_ref[pl.ds(pl.multiple_of(j * t, t), t), :], q)

    def consume(j, buf, carry, masked=False):
        m, l, acc = carry
        s = buf[...]
        if masked:
            visible = (lax.broadcasted_iota(I32, (t, t), 0) // CHUNK) <= (lax.broadcasted_iota(I32, (t, t), 1) // CHUNK)
            s = jnp.where(visible, s, NEG)
        m_new = jnp.maximum(m, jnp.max(s, axis=0, keepdims=True))
        alpha = jnp.exp2(m - m_new)
        p = jnp.exp2(s - m_new)
        l = alpha * l + jnp.sum(p, axis=0, keepdims=True)
        acc = alpha * acc + jnp.dot(vt_ref[j], p.astype(BF16), preferred_element_type=F32)
        return m_new, l, acc

    def run(first, second):
        q = q_ref[...]

        @pl.when(i == 0)
        def _():
            scores(q, 0, first)

        def pair(pp, carry):
            j = 2 * pp
            scores(q, j + 1, second)
            carry = consume(j, first, carry)
            scores(q, j + 2, first)
            return consume(j + 1, second, carry)

        def even_tail(carry):
            scores(qn_ref[...], 0, second)
            return consume(i, first, carry, masked=True)

        def odd_tail(carry):
            scores(q, i, second)
            carry = consume(i - 1, first, carry)
            scores(qn_ref[...], 0, first)
            return consume(i, second, carry, masked=True)

        carry = (jnp.full((1, t), NEG, F32), jnp.zeros((1, t), F32), jnp.zeros((V_DIM, t), F32))
        carry = lax.fori_loop(0, i // 2, pair, carry)
        _, l, acc = lax.cond(i % 2 == 1, odd_tail, even_tail, carry)
        o_ref[...] = (acc / l).T.astype(BF16)

    @pl.when(((i + 1) // 2) % 2 == 0)
    def _():
        run(s_a, s_b)

    @pl.when(((i + 1) // 2) % 2 == 1)
    def _():
        run(s_b, s_a)


def _attn_prompt(q, k, v_t):
    b, _, s, _ = q.shape
    t = v_t.shape[-1]
    nq = s // t
    return pl.pallas_call(
        functools.partial(_attn_prompt_body, t),
        grid=(b, HEADS, nq),
        in_specs=[pl.BlockSpec((None, None, t, QK_PAD), lambda bi, h, i: (bi, h, i, 0)),
                  pl.BlockSpec((None, None, t, QK_PAD), lambda bi, h, i: (bi, h, jnp.minimum(i + 1, nq - 1), 0)),
                  pl.BlockSpec((None, None, s, QK_PAD), lambda bi, h, i: (bi, h, 0, 0)),
                  pl.BlockSpec((None, None, nq, V_DIM, t), lambda bi, h, i: (bi, h, 0, 0, 0))],
        out_specs=pl.BlockSpec((None, t, V_DIM), lambda bi, h, i: (bi, i, h)),
        out_shape=jax.ShapeDtypeStruct((b, s, HEADS * V_DIM), BF16),
        scratch_shapes=[pltpu.VMEM((t, t), F32), pltpu.VMEM((t, t), F32)],
        compiler_params=_cparams(("arbitrary", "arbitrary", "arbitrary"), VMEM_BIG),
        name="attn_prompt",
    )(q, q, k, v_t)


def _attn_sample_body(past, sq, q_ref, plat_ref, pkr_ref, nlat_ref, nkr_ref, wk_ref, wv_ref, gk_ref, o_ref,
                      kp_scr, kn_scr):
    plat = plat_ref[...].astype(BF16)
    nlat = nlat_ref[...].astype(BF16)
    _key_rows(plat, pkr_ref[...], wk_ref, gk_ref, kp_scr)
    _key_rows(nlat, nkr_ref[...], wk_ref, gk_ref, kn_scr)
    vp_t = _nt(wv_ref[...], plat).astype(BF16)
    vn_t = _nt(wv_ref[...], nlat).astype(BF16)
    key_chunk = (past + lax.broadcasted_iota(I32, (sq, sq), 0)) // CHUNK
    qry_chunk = (past + lax.broadcasted_iota(I32, (sq, sq), 1)) // CHUNK
    for hd in range(HEADS):
        q = q_ref[hd]
        s_p = _nt(kp_scr[hd], q)
        s_n = jnp.where(key_chunk <= qry_chunk, _nt(kn_scr[hd], q), NEG)
        m = jnp.maximum(jnp.max(s_p, axis=0, keepdims=True), jnp.max(s_n, axis=0, keepdims=True))
        p_p = jnp.exp2(s_p - m)
        p_n = jnp.exp2(s_n - m)
        l = jnp.sum(p_p, axis=0, keepdims=True) + jnp.sum(p_n, axis=0, keepdims=True)
        rows = slice(V_DIM * hd, V_DIM * (hd + 1))
        o_t = (jnp.dot(vp_t[rows, :], p_p.astype(BF16), preferred_element_type=F32)
               + jnp.dot(vn_t[rows, :], p_n.astype(BF16), preferred_element_type=F32))
        o_ref[:, rows] = (o_t / l).T.astype(BF16)


def _attn_sample(q, past_lat, past_kr, lat, kr, w_uk, w_uv_t, g_qk_k):
    b, _, sq, _ = q.shape
    past = past_lat.shape[1]
    rows = lambda n, w: pl.BlockSpec((None, n, w), lambda bi: (bi, 0, 0))
    full = lambda a: pl.BlockSpec(a.shape, lambda bi: (0,) * a.ndim)
    return pl.pallas_call(
        functools.partial(_attn_sample_body, past, sq),
        grid=(b,),
        in_specs=[pl.BlockSpec((None, HEADS, sq, QK_PAD), lambda bi: (bi, 0, 0, 0)),
                  rows(past, KV_LORA), rows(past, ROPE), rows(sq, KV_LORA), rows(sq, ROPE),
                  full(w_uk), full(w_uv_t), full(g_qk_k)],
        out_specs=rows(sq, HEADS * V_DIM),
        out_shape=jax.ShapeDtypeStruct((b, sq, HEADS * V_DIM), BF16),
        scratch_shapes=[pltpu.VMEM((HEADS, past, QK_PAD), BF16), pltpu.VMEM((HEADS, sq, QK_PAD), BF16)],
        compiler_params=_cparams(("arbitrary",), 40 * 1024 * 1024),
        name="attn_sample",
    )(q, past_lat, past_kr, lat, kr, w_uk, w_uv_t, g_qk_k)


def _gla_masks(c):
    idx = np.arange(c)
    le = idx[None, :] <= idx[:, None]
    gt = idx[None, :] > idx[:, None]
    return np.concatenate([le, gt], axis=0).astype(np.float32), int(np.log2(c))


def _level_exponents(b, g, c, level):
    n = c >> level
    row = lax.broadcasted_iota(I32, (c, 1), 0)
    if n >= 8:
        split = b.reshape(c // n, n, LANES)[:, n // 2 - 1:n // 2, :]
        split = jnp.broadcast_to(split, (c // n, n, LANES)).reshape(c, LANES)
        return jnp.where((row & (n // 2)) != 0, b - split, split - b)
    g_prev = pltpu.roll(g, 1, 0)
    g_next = pltpu.roll(g, c - 1, 0)
    if n == 4:
        r = row & 3
        return jnp.where(r == 0, g_next, jnp.where(r == 1, 0.0, jnp.where(r == 2, g, g + g_prev)))
    assert n == 2
    return jnp.where((row & 1) != 0, g, 0.0)


def _gla_body(c, n_chunks, levels, mall_ref, q_ref, k_ref, g_ref, v_ref, r_ref, s0_ref, gout_ref,
              o_ref, sfin_ref, st_scr):
    it = pl.program_id(1)

    @pl.when(it == 0)
    def _():
        st_scr[...] = s0_ref[...]

    lane = lax.broadcasted_iota(I32, (c, LANES), 1)
    head_lanes = (lane < GLA_DK, lane >= GLA_DK)
    st_lane_lo = lax.broadcasted_iota(I32, (GLA_DV, LANES), 1) < GLA_DK
    row = lax.broadcasted_iota(I32, (c, 1), 0)
    ri = lax.broadcasted_iota(I32, (c, c), 0)
    ci = lax.broadcasted_iota(I32, (c, c), 1)
    mall = mall_ref[...]

    for ch in range(n_chunks):
        rows = slice(ch * c, (ch + 1) * c)
        for p in range(HEADS // 2):
            ls = slice(LANES * p, LANES * (p + 1))
            g = g_ref[rows, ls]
            q = q_ref[rows, ls]
            k = k_ref[rows, ls]
            g_hi = g.astype(BF16)
            g_lo = (g - g_hi.astype(F32)).astype(BF16)
            e2 = jnp.dot(mall, jnp.concatenate([g_hi, g_lo], axis=1), preferred_element_type=F32)
            e = e2[:, 0:LANES] + e2[:, LANES:2 * LANES]
            b = e[0:c]
            eb = jnp.exp(b)
            qb = q * eb
            kd = (k * jnp.exp(e[c:2 * c])).astype(BF16)
            d_last = eb[c - 1:c, :]
            st = st_scr[p]
            st_b = st.astype(BF16)
            qs, ks = [q], [k.astype(BF16)]
            for l in range(levels):
                bottom = (row & (c >> (l + 1))) != 0
                decay = jnp.exp(_level_exponents(b, g, c, l))
                qs.append(jnp.where(bottom, q * decay, 0.0))
                ks.append(jnp.where(bottom, 0.0, k * decay).astype(BF16))
            upd = []
            for hh in range(2):
                hd = 2 * p + hh
                sel = head_lanes[hh]
                a = jnp.where(ri == ci, _nt(jnp.where(sel, qs[0], 0.0).astype(BF16), ks[0]), 0.0)
                for l in range(levels):
                    pr = _nt(jnp.where(sel, qs[l + 1], 0.0).astype(BF16), ks[l + 1])
                    if l > 0:
                        pr = jnp.where((ri ^ ci) < (c >> l), pr, 0.0)
                    a = a + pr
                vh = v_ref[rows, GLA_DV * hd:GLA_DV * (hd + 1)]
                o = jnp.dot(a.astype(BF16), vh, preferred_element_type=F32)
                o = o + _nt(jnp.where(sel, qb, 0.0).astype(BF16), st_b)
                on = o * _rms(o, GLA_DV) * gout_ref[...]
                r = r_ref[rows, GLA_DV * hd:GLA_DV * (hd + 1)]
                o_ref[rows, GLA_DV * hd:GLA_DV * (hd + 1)] = (on * (r * jax.nn.sigmoid(r))).astype(BF16)
                upd.append(lax.dot_general(vh, kd, (((0,), (0,)), ((), ())), preferred_element_type=F32))
            st_scr[p] = st * d_last + jnp.where(st_lane_lo, upd[0], upd[1])

    @pl.when(it == pl.num_programs(1) - 1)
    def _():
        sfin_ref[...] = st_scr[...]


def _gla(gq, gk, gl, gv, gr, s0, g_out):
    b, s, _ = gq.shape
    c = min(CHUNK, s)
    tile = min(s, 8 * c)
    masks, levels = _gla_masks(c)
    mall = jnp.asarray(masks, BF16)
    tok = lambda w: pl.BlockSpec((None, tile, w), lambda bi, i: (bi, i, 0))
    st_spec = pl.BlockSpec((None, HEADS // 2, GLA_DV, LANES), lambda bi, i: (bi, 0, 0, 0))
    return pl.pallas_call(
        functools.partial(_gla_body, c, tile // c, levels),
        grid=(b, s // tile),
        in_specs=[pl.BlockSpec(mall.shape, lambda bi, i: (0, 0)),
                  tok(HEADS * GLA_DK), tok(HEADS * GLA_DK), tok(HEADS * GLA_DK), tok(HEADS * GLA_DV),
                  tok(HEADS * GLA_DV), st_spec, pl.BlockSpec(g_out.shape, lambda bi, i: (0, 0))],
        out_specs=(tok(HEADS * GLA_DV), st_spec),
        out_shape=(jax.ShapeDtypeStruct((b, s, HEADS * GLA_DV), BF16),
                   jax.ShapeDtypeStruct((b, HEADS // 2, GLA_DV, LANES), F32)),
        scratch_shapes=[pltpu.VMEM((HEADS // 2, GLA_DV, LANES), F32)],
        compiler_params=_cparams(("arbitrary", "arbitrary")),
        name="gla",
    )(mall, gq, gk, gl, gv, gr, s0, g_out)


def _state_to_pairs(s):
    b = s.shape[0]
    s = s.reshape(b, HEADS // 2, 2, GLA_DK, GLA_DV)
    return jnp.transpose(s, (0, 1, 4, 2, 3)).reshape(b, HEADS // 2, GLA_DV, 2 * GLA_DK)


def _state_from_pairs(s):
    b = s.shape[0]
    s = s.reshape(b, HEADS // 2, GLA_DV, 2, GLA_DK)
    return jnp.transpose(s, (0, 1, 3, 4, 2)).reshape(b, HEADS, GLA_DK, GLA_DV)


def _post_body(x_ref, om_ref, og_ref, gt_ref, sc_ref, sh_ref, wo_ref, gffn_ref, wr_ref, br_ref,
               x2_ref, h_ref, idx_ref, wt_ref, rank_ref, cnt_ref):
    half = om_ref.shape[-1]
    mix = (jnp.dot(om_ref[...], wo_ref[0:half, :], preferred_element_type=F32)
           + jnp.dot(og_ref[...], wo_ref[half:2 * half, :], preferred_element_type=F32))
    x2 = x_ref[...] + gt_ref[...] * mix
    x2_ref[...] = x2
    d = x2.shape[-1]
    h = (x2 * _rms(x2, d) * gffn_ref[...]) * (1.0 + sc_ref[...]) + sh_ref[...]
    h_hi = h.astype(BF16)
    h_ref[...] = h_hi
    h_lo = (h - h_hi.astype(F32)).astype(BF16)
    logits = _nt(wr_ref[0], h_hi) + _nt(wr_ref[0], h_lo) + _nt(wr_ref[1], h_hi) + br_ref[...]
    n_exp, tm = logits.shape
    eid = lax.broadcasted_iota(I32, (n_exp, tm), 0)
    vals, tops, ids = logits, [], []
    for _ in range(TOP_K):
        m = jnp.max(vals, axis=0, keepdims=True)
        sel = jnp.min(jnp.where(vals == m, eid, n_exp), axis=0, keepdims=True)
        tops.append(m)
        ids.append(sel)
        vals = jnp.where(eid == sel, -jnp.inf, vals)
    es = [jnp.exp(t - tops[0]) for t in tops]
    tot = es[0] + es[1] + es[2] + es[3]
    idx_ref[...] = jnp.concatenate(ids, axis=0)
    wt_ref[...] = jnp.concatenate([e / tot for e in es], axis=0)
    hits = [eid == sel for sel in ids]
    member = jnp.zeros((n_exp, tm), F32)
    for hk in hits:
        member = member + jnp.where(hk, 1.0, 0.0)
    before = lax.broadcasted_iota(I32, (tm, tm), 0) < lax.broadcasted_iota(I32, (tm, tm), 1)
    prefix = jnp.dot(member.astype(BF16), jnp.where(before, 1.0, 0.0).astype(BF16), preferred_element_type=F32)
    rank_ref[...] = jnp.concatenate(
        [jnp.sum(jnp.where(hk, prefix, 0.0), axis=0, keepdims=True) for hk in hits], axis=0).astype(I32)
    cnt_ref[...] = jnp.broadcast_to(jnp.sum(member, axis=1, keepdims=True), (n_exp, LANES)).astype(I32)


def _post(x, om, og, gate, scale, shift, w_o, g_ffn, w_r2, b_r):
    t, d = x.shape
    tm = ROUTE_TILE
    per_tok = gate.shape[0] == t
    mod = pl.BlockSpec((tm, d), lambda i: (i, 0)) if per_tok else pl.BlockSpec((1, d), lambda i: (0, 0))
    tok = lambda w: pl.BlockSpec((tm, w), lambda i: (i, 0))
    full = lambda a: pl.BlockSpec(a.shape, lambda i: (0,) * a.ndim)
    return pl.pallas_call(
        _post_body,
        grid=(t // tm,),
        in_specs=[tok(d), tok(om.shape[1]), tok(og.shape[1]), mod, mod, mod, full(w_o), full(g_ffn), full(w_r2),
                  full(b_r)],
        out_specs=(tok(d), tok(d),
                   pl.BlockSpec((TOP_K, tm), lambda i: (0, i)), pl.BlockSpec((TOP_K, tm), lambda i: (0, i)),
                   pl.BlockSpec((TOP_K, tm), lambda i: (0, i)),
                   pl.BlockSpec((None, N_EXPERTS, LANES), lambda i: (i, 0, 0))),
        out_shape=(jax.ShapeDtypeStruct((t, d), F32), jax.ShapeDtypeStruct((t, d), BF16),
                   jax.ShapeDtypeStruct((TOP_K, t), I32), jax.ShapeDtypeStruct((TOP_K, t), F32),
                   jax.ShapeDtypeStruct((TOP_K, t), I32), jax.ShapeDtypeStruct((t // tm, N_EXPERTS, LANES), I32)),
        compiler_params=_cparams(("arbitrary",), 40 * 1024 * 1024),
        name="post",
    )(x, om, og, gate, scale, shift, w_o, g_ffn, w_r2, b_r)


def _route_tables(idx, lrank, cnt3):
    nt = cnt3.shape[0]
    t = idx.shape[1]
    cnt = cnt3[:, :, 0]
    run = _round_up(cnt, RUN_CHUNK)
    lo_end = jnp.cumsum(run, axis=1)
    lo = lo_end - run
    n_chunks = lo_end[:, -1] // RUN_CHUNK
    region = _round_up(jnp.sum(run, axis=0), EXPERT_ROWS)
    g_end = jnp.cumsum(region)
    run_dest = (g_end - region)[None, :] + jnp.cumsum(run, axis=0) - run
    max_rows = TOP_K * ROUTE_TILE + N_EXPERTS * (RUN_CHUNK - 1)
    n_tab = _round_up(max_rows, RUN_CHUNK) // RUN_CHUNK
    c_start = jnp.arange(n_tab, dtype=I32) * RUN_CHUNK
    e_of_c = jnp.minimum(jnp.sum(lo_end[:, None, :] <= c_start[None, :, None], axis=2), N_EXPERTS - 1)
    pick = e_of_c[:, :, None] == jnp.arange(N_EXPERTS, dtype=I32)[None, None, :]
    chunk_dest = jnp.sum(jnp.where(pick, (run_dest - lo)[:, None, :], 0), axis=2) + c_start[None, :]
    table = jnp.concatenate([chunk_dest, n_chunks[:, None]], axis=1).astype(I32).reshape(nt, 1, n_tab + 1)
    eid = jnp.arange(N_EXPERTS, dtype=I32)[:, None]
    lo_tok = jnp.repeat(lo.T, ROUTE_TILE, axis=1)
    lpos = jnp.stack([jnp.sum(jnp.where(idx[k][None, :] == eid, lo_tok, 0), axis=0) for k in range(TOP_K)])
    lpos = (lpos + lrank).astype(I32)
    n_blocks = _round_up(t * TOP_K + nt * N_EXPERTS * (RUN_CHUNK - 1), EXPERT_ROWS) // EXPERT_ROWS + N_EXPERTS
    b_start = jnp.arange(n_blocks, dtype=I32) * EXPERT_ROWS
    blk_e = jnp.minimum(jnp.sum(g_end[None, :] <= b_start[:, None], axis=1), N_EXPERTS - 1).astype(I32)
    n_valid = (g_end[-1:] // EXPERT_ROWS).astype(I32)
    used = region > 0
    e_ids = jnp.arange(N_EXPERTS, dtype=I32)
    later_used = used[None, :] & (e_ids[None, :] > e_ids[:, None])
    nxt_e = jnp.min(jnp.where(later_used, e_ids[None, :], N_EXPERTS), axis=1)
    nxt_e = jnp.where(nxt_e < N_EXPERTS, nxt_e, -1).astype(I32)
    ord_e = (jnp.cumsum(used.astype(I32)) - 1).astype(I32)
    tail = jnp.concatenate([jnp.where(used, g_end - EXPERT_ROWS, -1), n_valid]).astype(I32)
    tail = tail.reshape(1, N_EXPERTS + 1)
    return dict(table=table, lpos=lpos, blk_e=blk_e, n_valid=n_valid, nxt_e=nxt_e, ord_e=ord_e, tail=tail,
                n_blocks=n_blocks)


def _chunk_copy(src, dst, sem):
    return pltpu.make_async_copy(src, dst, sem)


def _for_chunks(n, body):
    groups = n // CHUNK_UNROLL

    def group(g, carry):
        for u in range(CHUNK_UNROLL):
            body(g * CHUNK_UNROLL + u)
        return carry

    def single(c, carry):
        body(c)
        return carry

    lax.fori_loop(0, groups, group, 0)
    lax.fori_loop(groups * CHUNK_UNROLL, n, single, 0)


def _scatter_body(n_tab, nt_a, n_blocks, tab_ref, prv_ref, tail_ref, lpos_ref, ha_ref, hb_ref, xout_ref, sorted_scr,
                  zero_scr, sems, zero_sem):
    i = pl.program_id(0)
    slot = i % 2
    n_chunks = tab_ref[0, n_tab]
    tt = ha_ref.shape[0]

    @pl.when(i == 0)
    def _():
        zero_scr[...] = jnp.zeros_like(zero_scr)
        n_valid = tail_ref[0, N_EXPERTS]

        def block(start):
            return xout_ref.at[pl.ds(pl.multiple_of(start, EXPERT_ROWS), EXPERT_ROWS)]

        for e in range(N_EXPERTS):
            @pl.when(tail_ref[0, e] >= 0)
            def _():
                _chunk_copy(zero_scr, block(tail_ref[0, e]), zero_sem).start()

        def fill(b, carry):
            _chunk_copy(zero_scr, block(b * EXPERT_ROWS), zero_sem).start()
            return carry

        def fill_done(b, carry):
            _chunk_copy(zero_scr, block(0), zero_sem).wait()
            return carry

        lax.fori_loop(n_valid, n_blocks, fill, 0)
        for e in range(N_EXPERTS):
            @pl.when(tail_ref[0, e] >= 0)
            def _():
                _chunk_copy(zero_scr, block(0), zero_sem).wait()
        lax.fori_loop(n_valid, n_blocks, fill_done, 0)

    lp = lpos_ref[...]
    h = jnp.where(i < nt_a, ha_ref[...], hb_ref[...])

    def sort_block(rb, carry):
        r0 = pl.multiple_of(rb * SORT_ROWS, SORT_ROWS)
        rid = r0 + lax.broadcasted_iota(I32, (SORT_ROWS, tt), 0)
        onehot = jnp.zeros((SORT_ROWS, tt), F32)
        for k in range(TOP_K):
            onehot = onehot + jnp.where(lp[k:k + 1, :] == rid, 1.0, 0.0)
        sorted_scr[slot, pl.ds(r0, SORT_ROWS), :] = jnp.dot(onehot.astype(BF16), h,
                                                            preferred_element_type=F32).astype(BF16)
        return carry

    lax.fori_loop(0, (n_chunks * RUN_CHUNK + SORT_ROWS - 1) // SORT_ROWS, sort_block, 0)

    def issue(c):
        src = sorted_scr.at[slot, pl.ds(pl.multiple_of(c * RUN_CHUNK, RUN_CHUNK), RUN_CHUNK)]
        dst = xout_ref.at[pl.ds(pl.multiple_of(tab_ref[0, c], RUN_CHUNK), RUN_CHUNK)]
        _chunk_copy(src, dst, sems.at[slot]).start()

    def drain(sl):
        def body(c):
            _chunk_copy(sorted_scr.at[sl, pl.ds(0, RUN_CHUNK)], xout_ref.at[pl.ds(0, RUN_CHUNK)], sems.at[sl]).wait()
        return body

    _for_chunks(n_chunks, issue)

    @pl.when(i > 0)
    def _():
        _for_chunks(prv_ref[0, n_tab], drain(1 - slot))

    @pl.when(i == pl.num_programs(0) - 1)
    def _():
        _for_chunks(n_chunks, drain(slot))


def _scatter(table, tail, lpos, h_a, h_b, n_blocks):
    d = h_a.shape[1]
    nt_a, nt_b = h_a.shape[0] // ROUTE_TILE, h_b.shape[0] // ROUTE_TILE
    n_tab = table.shape[2] - 1
    sort_cap = _round_up(n_tab * RUN_CHUNK, SORT_ROWS)
    tab_spec = lambda f: pl.BlockSpec((None, 1, n_tab + 1), lambda i: (f(i), 0, 0), memory_space=pltpu.SMEM)
    return pl.pallas_call(
        functools.partial(_scatter_body, n_tab, nt_a, n_blocks),
        grid=(nt_a + nt_b,),
        in_specs=[tab_spec(lambda i: i), tab_spec(lambda i: jnp.maximum(i - 1, 0)),
                  pl.BlockSpec((1, N_EXPERTS + 1), lambda i: (0, 0), memory_space=pltpu.SMEM),
                  pl.BlockSpec((TOP_K, ROUTE_TILE), lambda i: (0, i)),
                  pl.BlockSpec((ROUTE_TILE, d), lambda i: (jnp.minimum(i, nt_a - 1), 0)),
                  pl.BlockSpec((ROUTE_TILE, d), lambda i: (jnp.maximum(i - nt_a, 0), 0))],
        out_specs=pl.BlockSpec(memory_space=pl.ANY),
        out_shape=jax.ShapeDtypeStruct((n_blocks * EXPERT_ROWS, d), BF16),
        scratch_shapes=[pltpu.VMEM((2, sort_cap, d), BF16), pltpu.VMEM((EXPERT_ROWS, d), BF16),
                        pltpu.SemaphoreType.DMA((2,)), pltpu.SemaphoreType.DMA(())],
        compiler_params=_cparams(("arbitrary",), 48 * 1024 * 1024),
        name="scatter",
    )(table, table, tail, lpos, h_a, h_b)


def _experts_body(be_ref, nv_ref, nxt_ref, ord_ref, x_ref, wgu_hbm, bgu_ref, wd_hbm, bd_ref, y_ref,
                  wgu_f, wd_f, wgu_s, wd_s, sem_gu, sem_d):
    b = pl.program_id(0)
    e = be_ref[b]
    prev = be_ref[jnp.maximum(b - 1, 0)]
    valid = b < nv_ref[0]
    d_ff = wd_s.shape[0]
    slot = ord_ref[e] % 2

    def weights(expert, sl):
        return (pltpu.make_async_copy(wgu_hbm.at[expert], wgu_f.at[sl], sem_gu.at[sl]),
                pltpu.make_async_copy(wd_hbm.at[expert], wd_f.at[sl], sem_d.at[sl]))

    @pl.when(valid & ((b == 0) | (e != prev)))
    def _():
        @pl.when(b == 0)
        def _():
            for cp in weights(e, slot):
                cp.start()

        for cp in weights(e, slot):
            cp.wait()

        @pl.when(nxt_ref[e] >= 0)
        def _():
            for cp in weights(nxt_ref[e], 1 - slot):
                cp.start()

        wgu_s[...] = wgu_f[slot].astype(BF16)
        wd_s[...] = wd_f[slot].astype(BF16)

    @pl.when(valid)
    def _():
        gu = jnp.dot(x_ref[...], wgu_s[...], preferred_element_type=F32) + bgu_ref[...]
        gate = jnp.minimum(gu[:, 0:d_ff], SWIGLU_LIMIT)
        up = jnp.clip(gu[:, d_ff:2 * d_ff], -SWIGLU_LIMIT, SWIGLU_LIMIT)
        act = ((up + 1.0) * (gate * jax.nn.sigmoid(gate * SWIGLU_ALPHA))).astype(BF16)
        y_ref[...] = (jnp.dot(act, wd_s[...], preferred_element_type=F32) + bd_ref[...]).astype(BF16)

    @pl.when(jnp.logical_not(valid))
    def _():
        y_ref[...] = jnp.zeros_like(y_ref)


def _experts(blk_e, n_valid, nxt_e, ord_e, xpad, w_gu, b_gu, w_down, b_down):
    m, d = xpad.shape
    nb = m // EXPERT_ROWS
    n_exp, _, f2 = w_gu.shape
    d_ff = w_down.shape[1]
    last = lambda b, be, nv: jnp.minimum(b, nv[0] - 1)
    grid_spec = pltpu.PrefetchScalarGridSpec(
        num_scalar_prefetch=4,
        grid=(nb,),
        in_specs=[pl.BlockSpec((EXPERT_ROWS, d), lambda b, be, nv, nx, od: (last(b, be, nv), 0)),
                  pl.BlockSpec(memory_space=pl.ANY),
                  pl.BlockSpec((None, 1, f2), lambda b, be, nv, nx, od: (be[last(b, be, nv)], 0, 0)),
                  pl.BlockSpec(memory_space=pl.ANY),
                  pl.BlockSpec((None, 1, d), lambda b, be, nv, nx, od: (be[last(b, be, nv)], 0, 0))],
        out_specs=pl.BlockSpec((EXPERT_ROWS, d), lambda b, be, nv, nx, od: (b, 0)),
        scratch_shapes=[pltpu.VMEM((2, d, f2), F32), pltpu.VMEM((2, d_ff, d), F32),
                        pltpu.VMEM((d, f2), BF16), pltpu.VMEM((d_ff, d), BF16),
                        pltpu.SemaphoreType.DMA((2,)), pltpu.SemaphoreType.DMA((2,))],
    )
    return pl.pallas_call(
        _experts_body,
        grid_spec=grid_spec,
        out_shape=jax.ShapeDtypeStruct((m, d), BF16),
        compiler_params=_cparams(("arbitrary",), VMEM_BIG),
        name="experts",
    )(blk_e, n_valid, nxt_e, ord_e, xpad, w_gu, b_gu.reshape(n_exp, 1, f2), w_down, b_down.reshape(n_exp, 1, d))


def _combine_body(n_tab, tab_ref, nxt_ref, lpos_ref, wt_ref, x2_ref, gt_ref, y_ref, o_ref, ysort_scr, sems):
    i = pl.program_id(0)
    slot = i % 2
    n_chunks = tab_ref[0, n_tab]
    tt = x2_ref.shape[0]

    def fetch(tab, sl):
        def issue(c):
            src = y_ref.at[pl.ds(pl.multiple_of(tab[0, c], RUN_CHUNK), RUN_CHUNK)]
            dst = ysort_scr.at[sl, pl.ds(pl.multiple_of(c * RUN_CHUNK, RUN_CHUNK), RUN_CHUNK)]
            _chunk_copy(src, dst, sems.at[sl]).start()

        _for_chunks(tab[0, n_tab], issue)

    @pl.when(i == 0)
    def _():
        ysort_scr[...] = jnp.zeros_like(ysort_scr)
        fetch(tab_ref, 0)

    @pl.when(i + 1 < pl.num_programs(0))
    def _():
        fetch(nxt_ref, 1 - slot)

    def drain(c):
        _chunk_copy(y_ref.at[pl.ds(0, RUN_CHUNK)], ysort_scr.at[slot, pl.ds(0, RUN_CHUNK)], sems.at[slot]).wait()

    _for_chunks(n_chunks, drain)

    lp = lpos_ref[...]
    w = wt_ref[...]
    o_ref[...] = x2_ref[...]
    gt = gt_ref[...]

    def gather_block(rb, carry):
        r0 = pl.multiple_of(rb * SORT_ROWS, SORT_ROWS)
        rid = r0 + lax.broadcasted_iota(I32, (SORT_ROWS, tt), 0)
        pw_t = jnp.zeros((SORT_ROWS, tt), F32)
        for k in range(TOP_K):
            pw_t = pw_t + jnp.where(lp[k:k + 1, :] == rid, w[k:k + 1, :], 0.0)
        ys = ysort_scr[slot, pl.ds(r0, SORT_ROWS), :]
        part = lax.dot_general(pw_t.astype(BF16), ys, (((0,), (0,)), ((), ())), preferred_element_type=F32)
        o_ref[...] = o_ref[...] + gt * part
        return carry

    lax.fori_loop(0, (n_chunks * RUN_CHUNK + SORT_ROWS - 1) // SORT_ROWS, gather_block, 0)


def _combine(table, lpos, wts, x2, gate, ypad):
    t, d = x2.shape
    nt = t // ROUTE_TILE
    n_tab = table.shape[2] - 1
    per_tok = gate.shape[0] == t
    mod = pl.BlockSpec((ROUTE_TILE, d), lambda i: (i, 0)) if per_tok else pl.BlockSpec((1, d), lambda i: (0, 0))
    sort_cap = _round_up(n_tab * RUN_CHUNK, SORT_ROWS)
    tab_spec = lambda f: pl.BlockSpec((None, 1, n_tab + 1), lambda i: (f(i), 0, 0), memory_space=pltpu.SMEM)
    return pl.pallas_call(
        functools.partial(_combine_body, n_tab),
        grid=(nt,),
        in_specs=[tab_spec(lambda i: i), tab_spec(lambda i: jnp.minimum(i + 1, nt - 1)),
                  pl.BlockSpec((TOP_K, ROUTE_TILE), lambda i: (0, i)),
                  pl.BlockSpec((TOP_K, ROUTE_TILE), lambda i: (0, i)),
                  pl.BlockSpec((ROUTE_TILE, d), lambda i: (i, 0)),
                  mod,
                  pl.BlockSpec(memory_space=pl.ANY)],
        out_specs=pl.BlockSpec((ROUTE_TILE, d), lambda i: (i, 0)),
        out_shape=jax.ShapeDtypeStruct((t, d), F32),
        scratch_shapes=[pltpu.VMEM((2, sort_cap, d), BF16), pltpu.SemaphoreType.DMA((2,))],
        compiler_params=_cparams(("arbitrary",), 48 * 1024 * 1024),
        name="combine",
    )(table, table, lpos, wts, x2, gate, ypad)


def _prep_weights(g_norm_mix, w_in, g_q_a, w_uq, g_kv_a, w_ukv, g_qk_q, g_qk_k, w_g2, b_g2, g_gla_out, w_o,
                  g_norm_ffn, w_router, b_router):
    d = w_in.shape[0]
    o_qa, o_kva, o_kr = 0, Q_LORA, Q_LORA + KV_LORA
    o_gq = o_kr + ROPE
    o_gk = o_gq + HEADS * GLA_DK
    o_gv = o_gk + HEADS * GLA_DK
    o_glr = o_gv + HEADS * GLA_DV
    o_gr = o_glr + GATE_RANK
    kr_cols = w_in[:, o_kr:o_kr + ROPE]
    w1 = jnp.concatenate([
        w_in[:, o_qa:o_kva], kr_cols, kr_cols, w_in[:, o_kva:o_kr], w_in[:, o_gq:o_gk], w_in[:, o_gk:o_gv],
        w_in[:, o_gv:o_glr], w_in[:, o_gr:o_gr + HEADS * GLA_DV], w_in[:, o_glr:o_gr],
        jnp.zeros((d, LANES - GATE_RANK), w_in.dtype)], axis=1).astype(BF16)
    assert w1.shape[1] == _W1_COLS
    wq = w_uq.reshape(Q_LORA, HEADS, QK)
    wuq = jnp.concatenate([wq[:, :, 0:NOPE].reshape(Q_LORA, HEADS * NOPE),
                           wq[:, :, NOPE:QK].reshape(Q_LORA, HEADS * ROPE)], axis=1).astype(BF16)
    wkv = w_ukv.reshape(KV_LORA, HEADS, NOPE + V_DIM)
    wuk = wkv[:, :, 0:NOPE].reshape(KV_LORA, HEADS * NOPE).astype(BF16)
    wuv_t = wkv[:, :, NOPE:].reshape(KV_LORA, HEADS * V_DIM).T.astype(BF16)
    pad_rope = lambda g: jnp.stack([g[0:NOPE], jnp.concatenate([g[NOPE:QK], jnp.zeros((QK_PAD - QK,), g.dtype)])])
    inv = ROPE_THETA ** (-jnp.arange(HALF, dtype=F32) / HALF)
    sign = jnp.concatenate([-jnp.ones((HALF,), F32), jnp.ones((HALF,), F32)])
    rope_tab = jnp.stack([jnp.tile(inv, LANES // HALF), jnp.tile(sign, LANES // ROPE)])
    wg2 = jnp.concatenate([w_g2, jnp.zeros((LANES - GATE_RANK, w_g2.shape[1]), w_g2.dtype)], axis=0).astype(BF16)
    wr_t = w_router.T
    wr_hi = wr_t.astype(BF16)
    wr_lo = (wr_t - wr_hi.astype(F32)).astype(BF16)
    return dict(
        g_mix=g_norm_mix.reshape(1, d), w1=w1, g_qa=g_q_a.reshape(1, -1), w_uq=wuq, g_kv=g_kv_a.reshape(1, -1),
        w_uk=wuk, w_uv_t=wuv_t, g_qk_q=pad_rope(g_qk_q), g_qk_k=pad_rope(g_qk_k), rope=rope_tab, w_g2=wg2,
        b_g2=b_g2.reshape(1, -1), g_out=g_gla_out.reshape(1, -1), w_o=w_o.astype(BF16),
        g_ffn=g_norm_ffn.reshape(1, d), w_r2=jnp.stack([wr_hi, wr_lo]), b_r=b_router.reshape(-1, 1))


def _mixer(x, mod, pos0, past_lat, past_kr, s0_pairs, wts):
    b, s, d = x.shape
    q, lat, kr, gq, gk, gv, gl, gr = _proj(x, mod[:, 0:1], mod[:, 1:2], pos0, wts)
    kv_w = (wts["w_uk"], wts["w_uv_t"], wts["g_qk_k"])
    if past_lat is None:
        k_new, vt_new = _kv(lat, kr, *kv_w)
        o_mla = _attn_prompt(q, k_new, vt_new)
    else:
        o_mla = _attn_sample(q, past_lat, past_kr, lat, kr, *kv_w)
    o_gla, s_fin = _gla(gq, gk, gl, gv, gr, s0_pairs, wts["g_out"])
    t = b * s
    if b == 1:
        rows = lambda j: mod[0, j:j + 1]
    else:
        rows = lambda j: jnp.broadcast_to(mod[:, j:j + 1], (b, s, d)).reshape(t, d)
    x2, h2, idx, wt, lrank, cnt = _post(x.reshape(t, d), o_mla.reshape(t, -1), o_gla.reshape(t, -1), rows(2), rows(4),
                                        rows(3), wts["w_o"], wts["g_ffn"], wts["w_r2"], wts["b_r"])
    return dict(x2=x2, h2=h2, idx=idx, wt=wt, lrank=lrank, cnt=cnt, gate_f=rows(5), lat=lat, kr=kr, s_fin=s_fin)


def kernel(x_prompt, x_sample, cache_mla_latent, cache_mla_krope, state_gla, c_prompt, c_sample, w_ada, b_ada, g_norm_mix, w_in, g_q_a, w_uq, g_kv_a, w_ukv, g_qk_q, g_qk_k, w_g2, b_g2, g_gla_out, w_o, g_norm_ffn, w_router, b_router, w_gu, b_gu, w_down, b_down):
    depth = w_ada.shape[0]
    assert depth == 1, "single-layer step"
    bp, sp, d = x_prompt.shape
    bs, ss, _ = x_sample.shape
    tp, tsm = bp * sp, bs * ss
    assert tp % ROUTE_TILE == 0 and tsm % ROUTE_TILE == 0, "token counts must be whole routing tiles"
    past = cache_mla_latent.shape[2]
    layer = lambda a: a.reshape(a.shape[1:])
    wts = _prep_weights(*[layer(a) for a in (g_norm_mix, w_in, g_q_a, w_uq, g_kv_a, w_ukv, g_qk_q, g_qk_k, w_g2, b_g2,
                                             g_gla_out, w_o, g_norm_ffn, w_router, b_router)])
    w_gu, b_gu, w_down, b_down = layer(w_gu), layer(b_gu), layer(w_down), layer(b_down)

    mod = _ada(jnp.concatenate([c_prompt, c_sample], axis=0), layer(w_ada), layer(b_ada)).reshape(bp + bs, 6, d)
    zero_state = jnp.zeros((bp, HEADS // 2, GLA_DV, LANES), F32)
    pr = _mixer(x_prompt, mod[:bp], 0, None, None, zero_state, wts)
    sa = _mixer(x_sample, mod[bp:], past, layer(cache_mla_latent), layer(cache_mla_krope),
                _state_to_pairs(layer(state_gla)), wts)

    idx = jnp.concatenate([pr["idx"], sa["idx"]], axis=1)
    lrank = jnp.concatenate([pr["lrank"], sa["lrank"]], axis=1)
    rt = _route_tables(idx, lrank, jnp.concatenate([pr["cnt"], sa["cnt"]], axis=0))
    ntp = tp // ROUTE_TILE
    lpos, table = rt["lpos"], rt["table"]
    xpad = _scatter(table, rt["tail"], lpos, pr["h2"], sa["h2"], rt["n_blocks"])
    ypad = _experts(rt["blk_e"], rt["n_valid"], rt["nxt_e"], rt["ord_e"], xpad, w_gu, b_gu, w_down, b_down)
    y_p = _combine(table[:ntp], lpos[:, :tp], pr["wt"], pr["x2"], pr["gate_f"], ypad).reshape(bp, sp, d)
    y_s = _combine(table[ntp:], lpos[:, tp:], sa["wt"], sa["x2"], sa["gate_f"], ypad).reshape(bs, ss, d)

    return (y_p, y_s,
            pr["lat"][None], pr["kr"][None], _state_from_pairs(pr["s_fin"])[None],
            sa["lat"][None], sa["kr"][None], _state_from_pairs(sa["s_fin"])[None])
```

```python
import functools

import numpy as np
import jax
import jax.numpy as jnp
from jax import lax
from jax.experimental import pallas as pl
from jax.experimental.pallas import tpu as pltpu

F32 = jnp.float32
BF16 = jnp.bfloat16
I32 = jnp.int32

CHUNK = 64
EPS = 1e-6
HEADS = 4
Q_LORA = 384
KV_LORA = 256
NOPE = 128
ROPE = 64
HALF = ROPE // 2
V_DIM = 128
QK = NOPE + ROPE
QK_PAD = 256
ROPE_THETA = 10000.0
GLA_DK = 64
GLA_DV = 128
GATE_RANK = 16
GATE_NORM = 16.0
N_EXPERTS = 32
TOP_K = 4
SWIGLU_LIMIT = 7.0
SWIGLU_ALPHA = 1.702
NEG = -1e30
LOG2_E = 1.4426950408889634

LANES = 128
BF16_ROWS = 16
ROUTE_TILE = 512
RUN_CHUNK = BF16_ROWS
SORT_ROWS = 512
CHUNK_UNROLL = 4
EXPERT_ROWS = 256
ATTN_TILE = 1024
VMEM_BIG = 56 * 1024 * 1024


def _cparams(sem, vmem=None):
    return pltpu.CompilerParams(dimension_semantics=sem, vmem_limit_bytes=vmem)


def _nt(a, b):
    return lax.dot_general(a, b, (((1,), (1,)), ((), ())), preferred_element_type=F32)


def _rms(x, width):
    return lax.rsqrt(jnp.sum(x * x, axis=-1, keepdims=True) * (1.0 / width) + EPS)


def _round_up(x, m):
    return ((x + m - 1) // m) * m


def _ada_body(c_ref, w_ref, b_ref, o_ref):
    c = c_ref[...]
    s = (c * jax.nn.sigmoid(c)).astype(BF16)
    o_ref[...] = jnp.dot(s, w_ref[...].astype(BF16), preferred_element_type=F32) + b_ref[...]


def _ada(c, w_ada, b_ada):
    r, d = c.shape
    n = w_ada.shape[1]
    tn = 1536 if n % 1536 == 0 else n
    return pl.pallas_call(
        _ada_body,
        grid=(n // tn,),
        in_specs=[pl.BlockSpec((r, d), lambda j: (0, 0)),
                  pl.BlockSpec((d, tn), lambda j: (0, j)),
                  pl.BlockSpec((1, tn), lambda j: (0, j))],
        out_specs=pl.BlockSpec((r, tn), lambda j: (0, j)),
        out_shape=jax.ShapeDtypeStruct((r, n), F32),
        compiler_params=_cparams(("arbitrary",), 40 * 1024 * 1024),
        name="ada",
    )(c, w_ada, b_ada.reshape(1, n))


_SEG = dict(qa_kr=(0, 512), kva=(512, 768), gq=(768, 1024), gk=(1024, 1280),
            gv=(1280, 1792), gr=(1792, 2304), glr=(2304, 2432))
_W1_COLS = 2432


def _proj_body(pos0, ts, x_ref, sh_ref, sc_ref, gmix_ref, w1_ref, gqa_ref, wuq_ref, gkv_ref, gqk_ref,
               rope_ref, wg2_ref, bg2_ref,
               q_ref, lat_ref, kr_ref, gq_o, gk_o, gv_o, gl_o, gr_o, trig_scr):
    i = pl.program_id(1)
    x = x_ref[...]
    d = x.shape[-1]
    h = (x * _rms(x, d) * gmix_ref[...]) * (1.0 + sc_ref[...]) + sh_ref[...]
    hb = h.astype(BF16)

    def seg(name):
        a, b = _SEG[name]
        return jnp.dot(hb, w1_ref[:, a:b], preferred_element_type=F32)

    @pl.when((pl.program_id(0) == 0) & (i == 0))
    def _():
        row_ang = lax.broadcasted_iota(I32, (ts, LANES), 0).astype(F32) * rope_ref[0:1, :]
        trig_scr[0] = jnp.cos(row_ang)
        trig_scr[1] = jnp.sin(row_ang)

    base_ang = jnp.broadcast_to((pos0 + i * ts).astype(F32) * rope_ref[0:1, :], (8, LANES))
    cos_a, sin_a = jnp.cos(base_ang)[0:1, :], jnp.sin(base_ang)[0:1, :]
    cos = cos_a * trig_scr[0] - sin_a * trig_scr[1]
    sin = (sin_a * trig_scr[0] + cos_a * trig_scr[1]) * rope_ref[1:2, :]
    lane = lax.broadcasted_iota(I32, (ts, LANES), 1)
    first_half = (lane & HALF) == 0
    low64 = lane < ROPE

    def rope(v):
        partner = jnp.where(first_half, pltpu.roll(v, LANES - HALF, 1), pltpu.roll(v, HALF, 1))
        return v * cos + partner * sin

    qa_kr = seg("qa_kr")
    qa = qa_kr[:, 0:Q_LORA]
    qn = (qa * _rms(qa, Q_LORA) * gqa_ref[...]).astype(BF16)
    qf = jnp.dot(qn, wuq_ref[...], preferred_element_type=F32)
    rope_blocks = (rope(qf[:, 4 * NOPE:4 * NOPE + LANES]), rope(qf[:, 4 * NOPE + LANES:4 * NOPE + 2 * LANES]))
    for hd in range(HEADS):
        nope = qf[:, NOPE * hd:NOPE * (hd + 1)]
        blk = rope_blocks[hd // 2]
        if hd % 2:
            blk = pltpu.roll(blk, ROPE, 1)
        blk = jnp.where(low64, blk, 0.0)
        ss = jnp.sum(nope * nope, axis=-1, keepdims=True) + jnp.sum(blk * blk, axis=-1, keepdims=True)
        scl = lax.rsqrt(ss * (1.0 / QK) + EPS) * (QK ** -0.5 * LOG2_E)
        q_ref[hd, :, 0:NOPE] = (nope * scl * gqk_ref[0:1, :]).astype(BF16)
        q_ref[hd, :, NOPE:QK_PAD] = (blk * scl * gqk_ref[1:2, :]).astype(BF16)

    kva = seg("kva")
    lat_ref[...] = kva * _rms(kva, KV_LORA) * gkv_ref[...]
    kr_ref[...] = rope(qa_kr[:, Q_LORA:Q_LORA + LANES])[:, 0:ROPE]

    gq_o[...] = seg("gq") * (GLA_DK ** -0.5)
    gk_o[...] = seg("gk")
    gv_o[...] = seg("gv").astype(BF16)
    gr_o[...] = seg("gr")
    z = jnp.dot(seg("glr").astype(BF16), wg2_ref[...], preferred_element_type=F32) + bg2_ref[...]
    gl_o[...] = (jnp.minimum(z, 0.0) - jnp.log1p(jnp.exp(-jnp.abs(z)))) * (1.0 / GATE_NORM)


def _proj(x, shift, scale, pos0, wts):
    b, s, d = x.shape
    ts = min(s, 512)
    row = lambda a: pl.BlockSpec(a.shape, lambda bi, i: (0,) * a.ndim)
    tok = lambda w: pl.BlockSpec((None, ts, w), lambda bi, i: (bi, i, 0))
    mod = pl.BlockSpec((None, 1, d), lambda bi, i: (bi, 0, 0))
    small = [wts["g_mix"], wts["w1"], wts["g_qa"], wts["w_uq"], wts["g_kv"], wts["g_qk_q"], wts["rope"],
             wts["w_g2"], wts["b_g2"]]
    out_shape = (
        jax.ShapeDtypeStruct((b, HEADS, s, QK_PAD), BF16),
        jax.ShapeDtypeStruct((b, s, KV_LORA), F32),
        jax.ShapeDtypeStruct((b, s, ROPE), F32),
        jax.ShapeDtypeStruct((b, s, HEADS * GLA_DK), F32),
        jax.ShapeDtypeStruct((b, s, HEADS * GLA_DK), F32),
        jax.ShapeDtypeStruct((b, s, HEADS * GLA_DV), BF16),
        jax.ShapeDtypeStruct((b, s, HEADS * GLA_DK), F32),
        jax.ShapeDtypeStruct((b, s, HEADS * GLA_DV), F32),
    )
    out_specs = (
        pl.BlockSpec((None, HEADS, ts, QK_PAD), lambda bi, i: (bi, 0, i, 0)),
        tok(KV_LORA), tok(ROPE), tok(HEADS * GLA_DK), tok(HEADS * GLA_DK), tok(HEADS * GLA_DV),
        tok(HEADS * GLA_DK), tok(HEADS * GLA_DV),
    )
    return pl.pallas_call(
        functools.partial(_proj_body, pos0, ts),
        grid=(b, s // ts),
        in_specs=[tok(d), mod, mod] + [row(a) for a in small],
        out_specs=out_specs,
        out_shape=out_shape,
        scratch_shapes=[pltpu.VMEM((2, ts, LANES), F32)],
        compiler_params=_cparams(("arbitrary", "arbitrary"), VMEM_BIG),
        name="proj",
    )(x, shift, scale, *small)


def _key_rows(lat, kr, wk_ref, gk_ref, k_out):
    kn_all = jnp.dot(lat, wk_ref[...], preferred_element_type=F32)
    kr_ss = jnp.sum(kr * kr, axis=-1, keepdims=True)
    for hd in range(HEADS):
        kn = kn_all[:, NOPE * hd:NOPE * (hd + 1)]
        scl = lax.rsqrt((jnp.sum(kn * kn, axis=-1, keepdims=True) + kr_ss) * (1.0 / QK) + EPS)
        k_out[hd, :, 0:NOPE] = (kn * scl * gk_ref[0:1, :]).astype(BF16)
        k_out[hd, :, NOPE:QK] = (kr * scl * gk_ref[1:2, 0:ROPE]).astype(BF16)
        k_out[hd, :, QK:QK_PAD] = jnp.zeros((kr.shape[0], QK_PAD - QK), BF16)


def _kv_body(lat_ref, kr_ref, wk_ref, wv_ref, gk_ref, k_ref, v_ref):
    lat = lat_ref[...].astype(BF16)
    _key_rows(lat, kr_ref[...], wk_ref, gk_ref, k_ref)
    v_t = _nt(wv_ref[...], lat)
    for hd in range(HEADS):
        v_ref[hd] = v_t[V_DIM * hd:V_DIM * (hd + 1), :].astype(BF16)


def _kv(lat, kr, w_uk, w_uv_t, g_qk_k):
    b, s, _ = lat.shape
    ts = min(s, ATTN_TILE)
    return pl.pallas_call(
        _kv_body,
        grid=(b, s // ts),
        in_specs=[pl.BlockSpec((None, ts, KV_LORA), lambda bi, i: (bi, i, 0)),
                  pl.BlockSpec((None, ts, ROPE), lambda bi, i: (bi, i, 0)),
                  pl.BlockSpec(w_uk.shape, lambda bi, i: (0, 0)),
                  pl.BlockSpec(w_uv_t.shape, lambda bi, i: (0, 0)),
                  pl.BlockSpec(g_qk_k.shape, lambda bi, i: (0, 0))],
        out_specs=(pl.BlockSpec((None, HEADS, ts, QK_PAD), lambda bi, i: (bi, 0, i, 0)),
                   pl.BlockSpec((None, HEADS, None, V_DIM, ts), lambda bi, i: (bi, 0, i, 0, 0))),
        out_shape=(jax.ShapeDtypeStruct((b, HEADS, s, QK_PAD), BF16),
                   jax.ShapeDtypeStruct((b, HEADS, s // ts, V_DIM, ts), BF16)),
        compiler_params=_cparams(("arbitrary", "arbitrary")),
        name="kv",
    )(lat, kr, w_uk, w_uv_t, g_qk_k)


def _attn_prompt_body(t, q_ref, qn_ref, k_ref, vt_ref, o_ref, s_a, s_b):
    i = pl.program_id(2)

    def scores(q, j, buf):
        buf[...] = _nt(k_ref[pl.ds(pl.multiple_of(j * t, t), t), :], q)

    def consume(j, buf, carry, masked=False):
        m, l, acc = carry
        s = buf[...]
        if masked:
            visible = (lax.broadcasted_iota(I32, (t, t), 0) // CHUNK) <= (lax.broadcasted_iota(I32, (t, t), 1) // CHUNK)
            s = jnp.where(visible, s, NEG)
        m_new = jnp.maximum(m, jnp.max(s, axis=0, keepdims=True))
        alpha = jnp.exp2(m - m_new)
        p = jnp.exp2(s - m_new)
        l = alpha * l + jnp.sum(p, axis=0, keepdims=True)
        acc = alpha * acc + jnp.dot(vt_ref[j], p.astype(BF16), preferred_element_type=F32)
        return m_new, l, acc

    def run(first, second):
        q = q_ref[...]

        @pl.when(i == 0)
        def _():
            scores(q, 0, first)

        def pair(pp, carry):
            j = 2 * pp
            scores(q, j + 1, second)
            carry = consume(j, first, carry)
            scores(q, j + 2, first)
            return consume(j + 1, second, carry)

        def even_tail(carry):
            scores(qn_ref[...], 0, second)
            return consume(i, first, carry, masked=True)

        def odd_tail(carry):
            scores(q, i, second)
            carry = consume(i - 1, first, carry)
            scores(qn_ref[...], 0, first)
            return consume(i, second, carry, masked=True)

        carry = (jnp.full((1, t), NEG, F32), jnp.zeros((1, t), F32), jnp.zeros((V_DIM, t), F32))
        carry = lax.fori_loop(0, i // 2, pair, carry)
        _, l, acc = lax.cond(i % 2 == 1, odd_tail, even_tail, carry)
        o_ref[...] = (acc / l).T.astype(BF16)

    @pl.when(((i + 1) // 2) % 2 == 0)
    def _():
        run(s_a, s_b)

    @pl.when(((i + 1) // 2) % 2 == 1)
    def _():
        run(s_b, s_a)


def _attn_prompt(q, k, v_t):
    b, _, s, _ = q.shape
    t = v_t.shape[-1]
    nq = s // t
    return pl.pallas_call(
        functools.partial(_attn_prompt_body, t),
        grid=(b, HEADS, nq),
        in_specs=[pl.BlockSpec((None, None, t, QK_PAD), lambda bi, h, i: (bi, h, i, 0)),
                  pl.BlockSpec((None, None, t, QK_PAD), lambda bi, h, i: (bi, h, jnp.minimum(i + 1, nq - 1), 0)),
                  pl.BlockSpec((None, None, s, QK_PAD), lambda bi, h, i: (bi, h, 0, 0)),
                  pl.BlockSpec((None, None, nq, V_DIM, t), lambda bi, h, i: (bi, h, 0, 0, 0))],
        out_specs=pl.BlockSpec((None, t, V_DIM), lambda bi, h, i: (bi, i, h)),
        out_shape=jax.ShapeDtypeStruct((b, s, HEADS * V_DIM), BF16),
        scratch_shapes=[pltpu.VMEM((t, t), F32), pltpu.VMEM((t, t), F32)],
        compiler_params=_cparams(("arbitrary", "arbitrary", "arbitrary"), VMEM_BIG),
        name="attn_prompt",
    )(q, q, k, v_t)


def _attn_sample_body(past, sq, q_ref, plat_ref, pkr_ref, nlat_ref, nkr_ref, wk_ref, wv_ref, gk_ref, o_ref,
                      kp_scr, kn_scr):
    plat = plat_ref[...].astype(BF16)
    nlat = nlat_ref[...].astype(BF16)
    _key_rows(plat, pkr_ref[...], wk_ref, gk_ref, kp_scr)
    _key_rows(nlat, nkr_ref[...], wk_ref, gk_ref, kn_scr)
    vp_t = _nt(wv_ref[...], plat).astype(BF16)
    vn_t = _nt(wv_ref[...], nlat).astype(BF16)
    key_chunk = (past + lax.broadcasted_iota(I32, (sq, sq), 0)) // CHUNK
    qry_chunk = (past + lax.broadcasted_iota(I32, (sq, sq), 1)) // CHUNK
    for hd in range(HEADS):
        q = q_ref[hd]
        s_p = _nt(kp_scr[hd], q)
        s_n = jnp.where(key_chunk <= qry_chunk, _nt(kn_scr[hd], q), NEG)
        m = jnp.maximum(jnp.max(s_p, axis=0, keepdims=True), jnp.max(s_n, axis=0, keepdims=True))
        p_p = jnp.exp2(s_p - m)
        p_n = jnp.exp2(s_n - m)
        l = jnp.sum(p_p, axis=0, keepdims=True) + jnp.sum(p_n, axis=0, keepdims=True)
        rows = slice(V_DIM * hd, V_DIM * (hd + 1))
        o_t = (jnp.dot(vp_t[rows, :], p_p.astype(BF16), preferred_element_type=F32)
               + jnp.dot(vn_t[rows, :], p_n.astype(BF16), preferred_element_type=F32))
        o_ref[:, rows] = (o_t / l).T.astype(BF16)


def _attn_sample(q, past_lat, past_kr, lat, kr, w_uk, w_uv_t, g_qk_k):
    b, _, sq, _ = q.shape
    past = past_lat.shape[1]
    rows = lambda n, w: pl.BlockSpec((None, n, w), lambda bi: (bi, 0, 0))
    full = lambda a: pl.BlockSpec(a.shape, lambda bi: (0,) * a.ndim)
    return pl.pallas_call(
        functools.partial(_attn_sample_body, past, sq),
        grid=(b,),
        in_specs=[pl.BlockSpec((None, HEADS, sq, QK_PAD), lambda bi: (bi, 0, 0, 0)),
                  rows(past, KV_LORA), rows(past, ROPE), rows(sq, KV_LORA), rows(sq, ROPE),
                  full(w_uk), full(w_uv_t), full(g_qk_k)],
        out_specs=rows(sq, HEADS * V_DIM),
        out_shape=jax.ShapeDtypeStruct((b, sq, HEADS * V_DIM), BF16),
        scratch_shapes=[pltpu.VMEM((HEADS, past, QK_PAD), BF16), pltpu.VMEM((HEADS, sq, QK_PAD), BF16)],
        compiler_params=_cparams(("arbitrary",), 40 * 1024 * 1024),
        name="attn_sample",
    )(q, past_lat, past_kr, lat, kr, w_uk, w_uv_t, g_qk_k)


def _gla_masks(c):
    idx = np.arange(c)
    le = idx[None, :] <= idx[:, None]
    gt = idx[None, :] > idx[:, None]
    return np.concatenate([le, gt], axis=0).astype(np.float32), int(np.log2(c))


def _level_exponents(b, g, c, level):
    n = c >> level
    row = lax.broadcasted_iota(I32, (c, 1), 0)
    if n >= 8:
        split = b.reshape(c // n, n, LANES)[:, n // 2 - 1:n // 2, :]
        split = jnp.broadcast_to(split, (c // n, n, LANES)).reshape(c, LANES)
        return jnp.where((row & (n // 2)) != 0, b - split, split - b)
    g_prev = pltpu.roll(g, 1, 0)
    g_next = pltpu.roll(g, c - 1, 0)
    if n == 4:
        r = row & 3
        return jnp.where(r == 0, g_next, jnp.where(r == 1, 0.0, jnp.where(r == 2, g, g + g_prev)))
    assert n == 2
    return jnp.where((row & 1) != 0, g, 0.0)


def _gla_body(c, n_chunks, levels, mall_ref, q_ref, k_ref, g_ref, v_ref, r_ref, s0_ref, gout_ref,
              o_ref, sfin_ref, st_scr):
    it = pl.program_id(1)

    @pl.when(it == 0)
    def _():
        st_scr[...] = s0_ref[...]

    lane = lax.broadcasted_iota(I32, (c, LANES), 1)
    head_lanes = (lane < GLA_DK, lane >= GLA_DK)
    st_lane_lo = lax.broadcasted_iota(I32, (GLA_DV, LANES), 1) < GLA_DK
    row = lax.broadcasted_iota(I32, (c, 1), 0)
    ri = lax.broadcasted_iota(I32, (c, c), 0)
    ci = lax.broadcasted_iota(I32, (c, c), 1)
    mall = mall_ref[...]

    for ch in range(n_chunks):
        rows = slice(ch * c, (ch + 1) * c)
        for p in range(HEADS // 2):
            ls = slice(LANES * p, LANES * (p + 1))
            g = g_ref[rows, ls]
            q = q_ref[rows, ls]
            k = k_ref[rows, ls]
            g_hi = g.astype(BF16)
            g_lo = (g - g_hi.astype(F32)).astype(BF16)
            e2 = jnp.dot(mall, jnp.concatenate([g_hi, g_lo], axis=1), preferred_element_type=F32)
            e = e2[:, 0:LANES] + e2[:, LANES:2 * LANES]
            b = e[0:c]
            eb = jnp.exp(b)
            qb = q * eb
            kd = (k * jnp.exp(e[c:2 * c])).astype(BF16)
            d_last = eb[c - 1:c, :]
            st = st_scr[p]
            st_b = st.astype(BF16)
            qs, ks = [q], [k.astype(BF16)]
            for l in range(levels):
                bottom = (row & (c >> (l + 1))) != 0
                decay = jnp.exp(_level_exponents(b, g, c, l))
                qs.append(jnp.where(bottom, q * decay, 0.0))
                ks.append(jnp.where(bottom, 0.0, k * decay).astype(BF16))
            upd = []
            for hh in range(2):
                hd = 2 * p + hh
                sel = head_lanes[hh]
                a = jnp.where(ri == ci, _nt(jnp.where(sel, qs[0], 0.0).astype(BF16), ks[0]), 0.0)
                for l in range(levels):
                    pr = _nt(jnp.where(sel, qs[l + 1], 0.0).astype(BF16), ks[l + 1])
                    if l > 0:
                        pr = jnp.where((ri ^ ci) < (c >> l), pr, 0.0)
                    a = a + pr
                vh = v_ref[rows, GLA_DV * hd:GLA_DV * (hd + 1)]
                o = jnp.dot(a.astype(BF16), vh, preferred_element_type=F32)
                o = o + _nt(jnp.where(sel, qb, 0.0).astype(BF16), st_b)
                on = o * _rms(o, GLA_DV) * gout_ref[...]
                r = r_ref[rows, GLA_DV * hd:GLA_DV * (hd + 1)]
                o_ref[rows, GLA_DV * hd:GLA_DV * (hd + 1)] = (on * (r * jax.nn.sigmoid(r))).astype(BF16)
                upd.append(lax.dot_general(vh, kd, (((0,), (0,)), ((), ())), preferred_element_type=F32))
            st_scr[p] = st * d_last + jnp.where(st_lane_lo, upd[0], upd[1])

    @pl.when(it == pl.num_programs(1) - 1)
    def _():
        sfin_ref[...] = st_scr[...]


def _gla(gq, gk, gl, gv, gr, s0, g_out):
    b, s, _ = gq.shape
    c = min(CHUNK, s)
    tile = min(s, 8 * c)
    masks, levels = _gla_masks(c)
    mall = jnp.asarray(masks, BF16)
    tok = lambda w: pl.BlockSpec((None, tile, w), lambda bi, i: (bi, i, 0))
    st_spec = pl.BlockSpec((None, HEADS // 2, GLA_DV, LANES), lambda bi, i: (bi, 0, 0, 0))
    return pl.pallas_call(
        functools.partial(_gla_body, c, tile // c, levels),
        grid=(b, s // tile),
        in_specs=[pl.BlockSpec(mall.shape, lambda bi, i: (0, 0)),
                  tok(HEADS * GLA_DK), tok(HEADS * GLA_DK), tok(HEADS * GLA_DK), tok(HEADS * GLA_DV),
                  tok(HEADS * GLA_DV), st_spec, pl.BlockSpec(g_out.shape, lambda bi, i: (0, 0))],
        out_specs=(tok(HEADS * GLA_DV), st_spec),
        out_shape=(jax.ShapeDtypeStruct((b, s, HEADS * GLA_DV), BF16),
                   jax.ShapeDtypeStruct((b, HEADS // 2, GLA_DV, LANES), F32)),
        scratch_shapes=[pltpu.VMEM((HEADS // 2, GLA_DV, LANES), F32)],
        compiler_params=_cparams(("arbitrary", "arbitrary")),
        name="gla",
    )(mall, gq, gk, gl, gv, gr, s0, g_out)


def _state_to_pairs(s):
    b = s.shape[0]
    s = s.reshape(b, HEADS // 2, 2, GLA_DK, GLA_DV)
    return jnp.transpose(s, (0, 1, 4, 2, 3)).reshape(b, HEADS // 2, GLA_DV, 2 * GLA_DK)


def _state_from_pairs(s):
    b = s.shape[0]
    s = s.reshape(b, HEADS // 2, GLA_DV, 2, GLA_DK)
    return jnp.transpose(s, (0, 1, 3, 4, 2)).reshape(b, HEADS, GLA_DK, GLA_DV)


def _post_body(x_ref, om_ref, og_ref, gt_ref, sc_ref, sh_ref, wo_ref, gffn_ref, wr_ref, br_ref,
               x2_ref, h_ref, idx_ref, wt_ref, rank_ref, cnt_ref):
    half = om_ref.shape[-1]
    mix = (jnp.dot(om_ref[...], wo_ref[0:half, :], preferred_element_type=F32)
           + jnp.dot(og_ref[...], wo_ref[half:2 * half, :], preferred_element_type=F32))
    x2 = x_ref[...] + gt_ref[...] * mix
    x2_ref[...] = x2
    d = x2.shape[-1]
    h = (x2 * _rms(x2, d) * gffn_ref[...]) * (1.0 + sc_ref[...]) + sh_ref[...]
    h_hi = h.astype(BF16)
    h_ref[...] = h_hi
    h_lo = (h - h_hi.astype(F32)).astype(BF16)
    logits = _nt(wr_ref[0], h_hi) + _nt(wr_ref[0], h_lo) + _nt(wr_ref[1], h_hi) + br_ref[...]
    n_exp, tm = logits.shape
    eid = lax.broadcasted_iota(I32, (n_exp, tm), 0)
    vals, tops, ids = logits, [], []
    for _ in range(TOP_K):
        m = jnp.max(vals, axis=0, keepdims=True)
        sel = jnp.min(jnp.where(vals == m, eid, n_exp), axis=0, keepdims=True)
        tops.append(m)
        ids.append(sel)
        vals = jnp.where(eid == sel, -jnp.inf, vals)
    es = [jnp.exp(t - tops[0]) for t in tops]
    tot = es[0] + es[1] + es[2] + es[3]
    idx_ref[...] = jnp.concatenate(ids, axis=0)
    wt_ref[...] = jnp.concatenate([e / tot for e in es], axis=0)
    hits = [eid == sel for sel in ids]
    member = jnp.zeros((n_exp, tm), F32)
    for hk in hits:
        member = member + jnp.where(hk, 1.0, 0.0)
    before = lax.broadcasted_iota(I32, (tm, tm), 0) < lax.broadcasted_iota(I32, (tm, tm), 1)
    prefix = jnp.dot(member.astype(BF16), jnp.where(before, 1.0, 0.0).astype(BF16), preferred_element_type=F32)
    rank_ref[...] = jnp.concatenate(
        [jnp.sum(jnp.where(hk, prefix, 0.0), axis=0, keepdims=True) for hk in hits], axis=0).astype(I32)
    cnt_ref[...] = jnp.broadcast_to(jnp.sum(member, axis=1, keepdims=True), (n_exp, LANES)).astype(I32)


def _post(x, om, og, gate, scale, shift, w_o, g_ffn, w_r2, b_r):
    t, d = x.shape
    tm = ROUTE_TILE
    per_tok = gate.shape[0] == t
    mod = pl.BlockSpec((tm, d), lambda i: (i, 0)) if per_tok else pl.BlockSpec((1, d), lambda i: (0, 0))
    tok = lambda w: pl.BlockSpec((tm, w), lambda i: (i, 0))
    full = lambda a: pl.BlockSpec(a.shape, lambda i: (0,) * a.ndim)
    return pl.pallas_call(
        _post_body,
        grid=(t // tm,),
        in_specs=[tok(d), tok(om.shape[1]), tok(og.shape[1]), mod, mod, mod, full(w_o), full(g_ffn), full(w_r2),
                  full(b_r)],
        out_specs=(tok(d), tok(d),
                   pl.BlockSpec((TOP_K, tm), lambda i: (0, i)), pl.BlockSpec((TOP_K, tm), lambda i: (0, i)),
                   pl.BlockSpec((TOP_K, tm), lambda i: (0, i)),
                   pl.BlockSpec((None, N_EXPERTS, LANES), lambda i: (i, 0, 0))),
        out_shape=(jax.ShapeDtypeStruct((t, d), F32), jax.ShapeDtypeStruct((t, d), BF16),
                   jax.ShapeDtypeStruct((TOP_K, t), I32), jax.ShapeDtypeStruct((TOP_K, t), F32),
                   jax.ShapeDtypeStruct((TOP_K, t), I32), jax.ShapeDtypeStruct((t // tm, N_EXPERTS, LANES), I32)),
        compiler_params=_cparams(("arbitrary",), 40 * 1024 * 1024),
        name="post",
    )(x, om, og, gate, scale, shift, w_o, g_ffn, w_r2, b_r)


def _route_tables(idx, lrank, cnt3):
    nt = cnt3.shape[0]
    t = idx.shape[1]
    cnt = cnt3[:, :, 0]
    run = _round_up(cnt, RUN_CHUNK)
    lo_end = jnp.cumsum(run, axis=1)
    lo = lo_end - run
    n_chunks = lo_end[:, -1] // RUN_CHUNK
    region = _round_up(jnp.sum(run, axis=0), EXPERT_ROWS)
    g_end = jnp.cumsum(region)
    run_dest = (g_end - region)[None, :] + jnp.cumsum(run, axis=0) - run
    max_rows = TOP_K * ROUTE_TILE + N_EXPERTS * (RUN_CHUNK - 1)
    n_tab = _round_up(max_rows, RUN_CHUNK) // RUN_CHUNK
    c_start = jnp.arange(n_tab, dtype=I32) * RUN_CHUNK
    e_of_c = jnp.minimum(jnp.sum(lo_end[:, None, :] <= c_start[None, :, None], axis=2), N_EXPERTS - 1)
    pick = e_of_c[:, :, None] == jnp.arange(N_EXPERTS, dtype=I32)[None, None, :]
    chunk_dest = jnp.sum(jnp.where(pick, (run_dest - lo)[:, None, :], 0), axis=2) + c_start[None, :]
    table = jnp.concatenate([chunk_dest, n_chunks[:, None]], axis=1).astype(I32).reshape(nt, 1, n_tab + 1)
    eid = jnp.arange(N_EXPERTS, dtype=I32)[:, None]
    lo_tok = jnp.repeat(lo.T, ROUTE_TILE, axis=1)
    lpos = jnp.stack([jnp.sum(jnp.where(idx[k][None, :] == eid, lo_tok, 0), axis=0) for k in range(TOP_K)])
    lpos = (lpos + lrank).astype(I32)
    n_blocks = _round_up(t * TOP_K + nt * N_EXPERTS * (RUN_CHUNK - 1), EXPERT_ROWS) // EXPERT_ROWS + N_EXPERTS
    b_start = jnp.arange(n_blocks, dtype=I32) * EXPERT_ROWS
    blk_e = jnp.minimum(jnp.sum(g_end[None, :] <= b_start[:, None], axis=1), N_EXPERTS - 1).astype(I32)
    n_valid = (g_end[-1:] // EXPERT_ROWS).astype(I32)
    used = region > 0
    e_ids = jnp.arange(N_EXPERTS, dtype=I32)
    later_used = used[None, :] & (e_ids[None, :] > e_ids[:, None])
    nxt_e = jnp.min(jnp.where(later_used, e_ids[None, :], N_EXPERTS), axis=1)
    nxt_e = jnp.where(nxt_e < N_EXPERTS, nxt_e, -1).astype(I32)
    ord_e = (jnp.cumsum(used.astype(I32)) - 1).astype(I32)
    tail = jnp.concatenate([jnp.where(used, g_end - EXPERT_ROWS, -1), n_valid]).astype(I32)
    tail = tail.reshape(1, N_EXPERTS + 1)
    return dict(table=table, lpos=lpos, blk_e=blk_e, n_valid=n_valid, nxt_e=nxt_e, ord_e=ord_e, tail=tail,
                n_blocks=n_blocks)


def _chunk_copy(src, dst, sem):
    return pltpu.make_async_copy(src, dst, sem)


def _for_chunks(n, body):
    groups = n // CHUNK_UNROLL

    def group(g, carry):
        for u in range(CHUNK_UNROLL):
            body(g * CHUNK_UNROLL + u)
        return carry

    def single(c, carry):
        body(c)
        return carry

    lax.fori_loop(0, groups, group, 0)
    lax.fori_loop(groups * CHUNK_UNROLL, n, single, 0)


def _scatter_body(n_tab, nt_a, n_blocks, tab_ref, prv_ref, tail_ref, lpos_ref, ha_ref, hb_ref, xout_ref, sorted_scr,
                  zero_scr, sems, zero_sem):
    i = pl.program_id(0)
    slot = i % 2
    n_chunks = tab_ref[0, n_tab]
    tt = ha_ref.shape[0]

    @pl.when(i == 0)
    def _():
        zero_scr[...] = jnp.zeros_like(zero_scr)
        n_valid = tail_ref[0, N_EXPERTS]

        def block(start):
            return xout_ref.at[pl.ds(pl.multiple_of(start, EXPERT_ROWS), EXPERT_ROWS)]

        for e in range(N_EXPERTS):
            @pl.when(tail_ref[0, e] >= 0)
            def _():
                _chunk_copy(zero_scr, block(tail_ref[0, e]), zero_sem).start()

        def fill(b, carry):
            _chunk_copy(zero_scr, block(b * EXPERT_ROWS), zero_sem).start()
            return carry

        def fill_done(b, carry):
            _chunk_copy(zero_scr, block(0), zero_sem).wait()
            return carry

        lax.fori_loop(n_valid, n_blocks, fill, 0)
        for e in range(N_EXPERTS):
            @pl.when(tail_ref[0, e] >= 0)
            def _():
                _chunk_copy(zero_scr, block(0), zero_sem).wait()
        lax.fori_loop(n_valid, n_blocks, fill_done, 0)

    lp16 = lpos_ref[...].astype(jnp.int16)
    h = jnp.where(i < nt_a, ha_ref[...], hb_ref[...])

    def sort_block(rb, carry):
        r0 = pl.multiple_of(rb * SORT_ROWS, SORT_ROWS)
        rid = (r0 + lax.broadcasted_iota(I32, (SORT_ROWS, tt), 0)).astype(jnp.int16)
        onehot = jnp.zeros((SORT_ROWS, tt), BF16)
        for k in range(TOP_K):
            onehot = onehot + jnp.where(lp16[k:k + 1, :] == rid, jnp.ones((), BF16), jnp.zeros((), BF16))
        sorted_scr[slot, pl.ds(r0, SORT_ROWS), :] = jnp.dot(onehot, h,
                                                            preferred_element_type=F32).astype(BF16)
        return carry

    lax.fori_loop(0, (n_chunks * RUN_CHUNK + SORT_ROWS - 1) // SORT_ROWS, sort_block, 0)

    def issue(c):
        src = sorted_scr.at[slot, pl.ds(pl.multiple_of(c * RUN_CHUNK, RUN_CHUNK), RUN_CHUNK)]
        dst = xout_ref.at[pl.ds(pl.multiple_of(tab_ref[0, c], RUN_CHUNK), RUN_CHUNK)]
        _chunk_copy(src, dst, sems.at[slot]).start()

    def drain(sl):
        def body(c):
            _chunk_copy(sorted_scr.at[sl, pl.ds(0, RUN_CHUNK)], xout_ref.at[pl.ds(0, RUN_CHUNK)], sems.at[sl]).wait()
        return body

    _for_chunks(n_chunks, issue)

    @pl.when(i > 0)
    def _():
        _for_chunks(prv_ref[0, n_tab], drain(1 - slot))

    @pl.when(i == pl.num_programs(0) - 1)
    def _():
        _for_chunks(n_chunks, drain(slot))


def _scatter(table, tail, lpos, h_a, h_b, n_blocks):
    d = h_a.shape[1]
    nt_a, nt_b = h_a.shape[0] // ROUTE_TILE, h_b.shape[0] // ROUTE_TILE
    n_tab = table.shape[2] - 1
    sort_cap = _round_up(n_tab * RUN_CHUNK, SORT_ROWS)
    tab_spec = lambda f: pl.BlockSpec((None, 1, n_tab + 1), lambda i: (f(i), 0, 0), memory_space=pltpu.SMEM)
    return pl.pallas_call(
        functools.partial(_scatter_body, n_tab, nt_a, n_blocks),
        grid=(nt_a + nt_b,),
        in_specs=[tab_spec(lambda i: i), tab_spec(lambda i: jnp.maximum(i - 1, 0)),
                  pl.BlockSpec((1, N_EXPERTS + 1), lambda i: (0, 0), memory_space=pltpu.SMEM),
                  pl.BlockSpec((TOP_K, ROUTE_TILE), lambda i: (0, i)),
                  pl.BlockSpec((ROUTE_TILE, d), lambda i: (jnp.minimum(i, nt_a - 1), 0)),
                  pl.BlockSpec((ROUTE_TILE, d), lambda i: (jnp.maximum(i - nt_a, 0), 0))],
        out_specs=pl.BlockSpec(memory_space=pl.ANY),
        out_shape=jax.ShapeDtypeStruct((n_blocks * EXPERT_ROWS, d), BF16),
        scratch_shapes=[pltpu.VMEM((2, sort_cap, d), BF16), pltpu.VMEM((EXPERT_ROWS, d), BF16),
                        pltpu.SemaphoreType.DMA((2,)), pltpu.SemaphoreType.DMA(())],
        compiler_params=_cparams(("arbitrary",), 48 * 1024 * 1024),
        name="scatter",
    )(table, table, tail, lpos, h_a, h_b)


def _experts_body(be_ref, nv_ref, nxt_ref, ord_ref, x_ref, wgu_hbm, bgu_ref, wd_hbm, bd_ref, y_ref,
                  wgu_f, wd_f, wgu_s, wd_s, sem_gu, sem_d):
    b = pl.program_id(0)
    e = be_ref[b]
    prev = be_ref[jnp.maximum(b - 1, 0)]
    valid = b < nv_ref[0]
    d_ff = wd_s.shape[0]
    slot = ord_ref[e] % 2

    def weights(expert, sl):
        return (pltpu.make_async_copy(wgu_hbm.at[expert], wgu_f.at[sl], sem_gu.at[sl]),
                pltpu.make_async_copy(wd_hbm.at[expert], wd_f.at[sl], sem_d.at[sl]))

    @pl.when(valid & ((b == 0) | (e != prev)))
    def _():
        @pl.when(b == 0)
        def _():
            for cp in weights(e, slot):
                cp.start()

        for cp in weights(e, slot):
            cp.wait()

        @pl.when(nxt_ref[e] >= 0)
        def _():
            for cp in weights(nxt_ref[e], 1 - slot):
                cp.start()

        wgu_s[...] = wgu_f[slot].astype(BF16)
        wd_s[...] = wd_f[slot].astype(BF16)

    @pl.when(valid)
    def _():
        gu = jnp.dot(x_ref[...], wgu_s[...], preferred_element_type=F32) + bgu_ref[...]
        gate = jnp.minimum(gu[:, 0:d_ff], SWIGLU_LIMIT)
        up = jnp.clip(gu[:, d_ff:2 * d_ff], -SWIGLU_LIMIT, SWIGLU_LIMIT)
        act = ((up + 1.0) * (gate * jax.nn.sigmoid(gate * SWIGLU_ALPHA))).astype(BF16)
        y_ref[...] = (jnp.dot(act, wd_s[...], preferred_element_type=F32) + bd_ref[...]).astype(BF16)

    @pl.when(jnp.logical_not(valid))
    def _():
        y_ref[...] = jnp.zeros_like(y_ref)


def _experts(blk_e, n_valid, nxt_e, ord_e, xpad, w_gu, b_gu, w_down, b_down):
    m, d = xpad.shape
    nb = m // EXPERT_ROWS
    n_exp, _, f2 = w_gu.shape
    d_ff = w_down.shape[1]
    last = lambda b, be, nv: jnp.minimum(b, nv[0] - 1)
    grid_spec = pltpu.PrefetchScalarGridSpec(
        num_scalar_prefetch=4,
        grid=(nb,),
        in_specs=[pl.BlockSpec((EXPERT_ROWS, d), lambda b, be, nv, nx, od: (last(b, be, nv), 0)),
                  pl.BlockSpec(memory_space=pl.ANY),
                  pl.BlockSpec((None, 1, f2), lambda b, be, nv, nx, od: (be[last(b, be, nv)], 0, 0)),
                  pl.BlockSpec(memory_space=pl.ANY),
                  pl.BlockSpec((None, 1, d), lambda b, be, nv, nx, od: (be[last(b, be, nv)], 0, 0))],
        out_specs=pl.BlockSpec((EXPERT_ROWS, d), lambda b, be, nv, nx, od: (b, 0)),
        scratch_shapes=[pltpu.VMEM((2, d, f2), F32), pltpu.VMEM((2, d_ff, d), F32),
                        pltpu.VMEM((d, f2), BF16), pltpu.VMEM((d_ff, d), BF16),
                        pltpu.SemaphoreType.DMA((2,)), pltpu.SemaphoreType.DMA((2,))],
    )
    return pl.pallas_call(
        _experts_body,
        grid_spec=grid_spec,
        out_shape=jax.ShapeDtypeStruct((m, d), BF16),
        compiler_params=_cparams(("arbitrary",), VMEM_BIG),
        name="experts",
    )(blk_e, n_valid, nxt_e, ord_e, xpad, w_gu, b_gu.reshape(n_exp, 1, f2), w_down, b_down.reshape(n_exp, 1, d))


def _combine_body(n_tab, tab_ref, nxt_ref, lpos_ref, wt_ref, x2_ref, gt_ref, y_ref, o_ref, ysort_scr, sems):
    i = pl.program_id(0)
    slot = i % 2
    n_chunks = tab_ref[0, n_tab]
    tt = x2_ref.shape[0]

    def fetch(tab, sl):
        def issue(c):
            src = y_ref.at[pl.ds(pl.multiple_of(tab[0, c], RUN_CHUNK), RUN_CHUNK)]
            dst = ysort_scr.at[sl, pl.ds(pl.multiple_of(c * RUN_CHUNK, RUN_CHUNK), RUN_CHUNK)]
            _chunk_copy(src, dst, sems.at[sl]).start()

        _for_chunks(tab[0, n_tab], issue)

    @pl.when(i == 0)
    def _():
        ysort_scr[...] = jnp.zeros_like(ysort_scr)
        fetch(tab_ref, 0)

    @pl.when(i + 1 < pl.num_programs(0))
    def _():
        fetch(nxt_ref, 1 - slot)

    def drain(c):
        _chunk_copy(y_ref.at[pl.ds(0, RUN_CHUNK)], ysort_scr.at[slot, pl.ds(0, RUN_CHUNK)], sems.at[slot]).wait()

    _for_chunks(n_chunks, drain)

    lp16 = lpos_ref[...].astype(jnp.int16)
    w = wt_ref[...].astype(BF16)
    o_ref[...] = x2_ref[...]
    gt = gt_ref[...]

    def gather_block(rb, carry):
        r0 = pl.multiple_of(rb * SORT_ROWS, SORT_ROWS)
        rid = (r0 + lax.broadcasted_iota(I32, (SORT_ROWS, tt), 0)).astype(jnp.int16)
        pw_t = jnp.zeros((SORT_ROWS, tt), BF16)
        for k in range(TOP_K):
            pw_t = pw_t + jnp.where(lp16[k:k + 1, :] == rid, w[k:k + 1, :], jnp.zeros((), BF16))
        ys = ysort_scr[slot, pl.ds(r0, SORT_ROWS), :]
        part = lax.dot_general(pw_t, ys, (((0,), (0,)), ((), ())), preferred_element_type=F32)
        o_ref[...] = o_ref[...] + gt * part
        return carry

    lax.fori_loop(0, (n_chunks * RUN_CHUNK + SORT_ROWS - 1) // SORT_ROWS, gather_block, 0)


def _combine(table, lpos, wts, x2, gate, ypad):
    t, d = x2.shape
    nt = t // ROUTE_TILE
    n_tab = table.shape[2] - 1
    per_tok = gate.shape[0] == t
    mod = pl.BlockSpec((ROUTE_TILE, d), lambda i: (i, 0)) if per_tok else pl.BlockSpec((1, d), lambda i: (0, 0))
    sort_cap = _round_up(n_tab * RUN_CHUNK, SORT_ROWS)
    tab_spec = lambda f: pl.BlockSpec((None, 1, n_tab + 1), lambda i: (f(i), 0, 0), memory_space=pltpu.SMEM)
    return pl.pallas_call(
        functools.partial(_combine_body, n_tab),
        grid=(nt,),
        in_specs=[tab_spec(lambda i: i), tab_spec(lambda i: jnp.minimum(i + 1, nt - 1)),
                  pl.BlockSpec((TOP_K, ROUTE_TILE), lambda i: (0, i)),
                  pl.BlockSpec((TOP_K, ROUTE_TILE), lambda i: (0, i)),
                  pl.BlockSpec((ROUTE_TILE, d), lambda i: (i, 0)),
                  mod,
                  pl.BlockSpec(memory_space=pl.ANY)],
        out_specs=pl.BlockSpec((ROUTE_TILE, d), lambda i: (i, 0)),
        out_shape=jax.ShapeDtypeStruct((t, d), F32),
        scratch_shapes=[pltpu.VMEM((2, sort_cap, d), BF16), pltpu.SemaphoreType.DMA((2,))],
        compiler_params=_cparams(("arbitrary",), 48 * 1024 * 1024),
        name="combine",
    )(table, table, lpos, wts, x2, gate, ypad)


def _prep_weights(g_norm_mix, w_in, g_q_a, w_uq, g_kv_a, w_ukv, g_qk_q, g_qk_k, w_g2, b_g2, g_gla_out, w_o,
                  g_norm_ffn, w_router, b_router):
    d = w_in.shape[0]
    o_qa, o_kva, o_kr = 0, Q_LORA, Q_LORA + KV_LORA
    o_gq = o_kr + ROPE
    o_gk = o_gq + HEADS * GLA_DK
    o_gv = o_gk + HEADS * GLA_DK
    o_glr = o_gv + HEADS * GLA_DV
    o_gr = o_glr + GATE_RANK
    kr_cols = w_in[:, o_kr:o_kr + ROPE]
    w1 = jnp.concatenate([
        w_in[:, o_qa:o_kva], kr_cols, kr_cols, w_in[:, o_kva:o_kr], w_in[:, o_gq:o_gk], w_in[:, o_gk:o_gv],
        w_in[:, o_gv:o_glr], w_in[:, o_gr:o_gr + HEADS * GLA_DV], w_in[:, o_glr:o_gr],
        jnp.zeros((d, LANES - GATE_RANK), w_in.dtype)], axis=1).astype(BF16)
    assert w1.shape[1] == _W1_COLS
    wq = w_uq.reshape(Q_LORA, HEADS, QK)
    wuq = jnp.concatenate([wq[:, :, 0:NOPE].reshape(Q_LORA, HEADS * NOPE),
                           wq[:, :, NOPE:QK].reshape(Q_LORA, HEADS * ROPE)], axis=1).astype(BF16)
    wkv = w_ukv.reshape(KV_LORA, HEADS, NOPE + V_DIM)
    wuk = wkv[:, :, 0:NOPE].reshape(KV_LORA, HEADS * NOPE).astype(BF16)
    wuv_t = wkv[:, :, NOPE:].reshape(KV_LORA, HEADS * V_DIM).T.astype(BF16)
    pad_rope = lambda g: jnp.stack([g[0:NOPE], jnp.concatenate([g[NOPE:QK], jnp.zeros((QK_PAD - QK,), g.dtype)])])
    inv = ROPE_THETA ** (-jnp.arange(HALF, dtype=F32) / HALF)
    sign = jnp.concatenate([-jnp.ones((HALF,), F32), jnp.ones((HALF,), F32)])
    rope_tab = jnp.stack([jnp.tile(inv, LANES // HALF), jnp.tile(sign, LANES // ROPE)])
    wg2 = jnp.concatenate([w_g2, jnp.zeros((LANES - GATE_RANK, w_g2.shape[1]), w_g2.dtype)], axis=0).astype(BF16)
    wr_t = w_router.T
    wr_hi = wr_t.astype(BF16)
    wr_lo = (wr_t - wr_hi.astype(F32)).astype(BF16)
    return dict(
        g_mix=g_norm_mix.reshape(1, d), w1=w1, g_qa=g_q_a.reshape(1, -1), w_uq=wuq, g_kv=g_kv_a.reshape(1, -1),
        w_uk=wuk, w_uv_t=wuv_t, g_qk_q=pad_rope(g_qk_q), g_qk_k=pad_rope(g_qk_k), rope=rope_tab, w_g2=wg2,
        b_g2=b_g2.reshape(1, -1), g_out=g_gla_out.reshape(1, -1), w_o=w_o.astype(BF16),
        g_ffn=g_norm_ffn.reshape(1, d), w_r2=jnp.stack([wr_hi, wr_lo]), b_r=b_router.reshape(-1, 1))


def _mixer(x, mod, pos0, past_lat, past_kr, s0_pairs, wts):
    b, s, d = x.shape
    q, lat, kr, gq, gk, gv, gl, gr = _proj(x, mod[:, 0:1], mod[:, 1:2], pos0, wts)
    kv_w = (wts["w_uk"], wts["w_uv_t"], wts["g_qk_k"])
    if past_lat is None:
        k_new, vt_new = _kv(lat, kr, *kv_w)
        o_mla = _attn_prompt(q, k_new, vt_new)
    else:
        o_mla = _attn_sample(q, past_lat, past_kr, lat, kr, *kv_w)
    o_gla, s_fin = _gla(gq, gk, gl, gv, gr, s0_pairs, wts["g_out"])
    t = b * s
    if b == 1:
        rows = lambda j: mod[0, j:j + 1]
    else:
        rows = lambda j: jnp.broadcast_to(mod[:, j:j + 1], (b, s, d)).reshape(t, d)
    x2, h2, idx, wt, lrank, cnt = _post(x.reshape(t, d), o_mla.reshape(t, -1), o_gla.reshape(t, -1), rows(2), rows(4),
                                        rows(3), wts["w_o"], wts["g_ffn"], wts["w_r2"], wts["b_r"])
    return dict(x2=x2, h2=h2, idx=idx, wt=wt, lrank=lrank, cnt=cnt, gate_f=rows(5), lat=lat, kr=kr, s_fin=s_fin)


def kernel(x_prompt, x_sample, cache_mla_latent, cache_mla_krope, state_gla, c_prompt, c_sample, w_ada, b_ada, g_norm_mix, w_in, g_q_a, w_uq, g_kv_a, w_ukv, g_qk_q, g_qk_k, w_g2, b_g2, g_gla_out, w_o, g_norm_ffn, w_router, b_router, w_gu, b_gu, w_down, b_down):
    depth = w_ada.shape[0]
    assert depth == 1, "single-layer step"
    bp, sp, d = x_prompt.shape
    bs, ss, _ = x_sample.shape
    tp, tsm = bp * sp, bs * ss
    assert tp % ROUTE_TILE == 0 and tsm % ROUTE_TILE == 0, "token counts must be whole routing tiles"
    past = cache_mla_latent.shape[2]
    layer = lambda a: a.reshape(a.shape[1:])
    wts = _prep_weights(*[layer(a) for a in (g_norm_mix, w_in, g_q_a, w_uq, g_kv_a, w_ukv, g_qk_q, g_qk_k, w_g2, b_g2,
                                             g_gla_out, w_o, g_norm_ffn, w_router, b_router)])
    w_gu, b_gu, w_down, b_down = layer(w_gu), layer(b_gu), layer(w_down), layer(b_down)

    mod = _ada(jnp.concatenate([c_prompt, c_sample], axis=0), layer(w_ada), layer(b_ada)).reshape(bp + bs, 6, d)
    zero_state = jnp.zeros((bp, HEADS // 2, GLA_DV, LANES), F32)
    pr = _mixer(x_prompt, mod[:bp], 0, None, None, zero_state, wts)
    sa = _mixer(x_sample, mod[bp:], past, layer(cache_mla_latent), layer(cache_mla_krope),
                _state_to_pairs(layer(state_gla)), wts)

    idx = jnp.concatenate([pr["idx"], sa["idx"]], axis=1)
    lrank = jnp.concatenate([pr["lrank"], sa["lrank"]], axis=1)
    rt = _route_tables(idx, lrank, jnp.concatenate([pr["cnt"], sa["cnt"]], axis=0))
    ntp = tp // ROUTE_TILE
    lpos, table = rt["lpos"], rt["table"]
    xpad = _scatter(table, rt["tail"], lpos, pr["h2"], sa["h2"], rt["n_blocks"])
    ypad = _experts(rt["blk_e"], rt["n_valid"], rt["nxt_e"], rt["ord_e"], xpad, w_gu, b_gu, w_down, b_down)
    y_p = _combine(table[:ntp], lpos[:, :tp], pr["wt"], pr["x2"], pr["gate_f"], ypad).reshape(bp, sp, d)
    y_s = _combine(table[ntp:], lpos[:, tp:], sa["wt"], sa["x2"], sa["gate_f"], ypad).reshape(bs, ss, d)

    return (y_p, y_s,
            pr["lat"][None], pr["kr"][None], _state_from_pairs(pr["s_fin"])[None],
            sa["lat"][None], sa["kr"][None], _state_from_pairs(sa["s_fin"])[None])
```

```python
import functools

import numpy as np
import jax
import jax.numpy as jnp
from jax import lax
from jax.experimental import pallas as pl
from jax.experimental.pallas import tpu as pltpu

F32 = jnp.float32
BF16 = jnp.bfloat16
I32 = jnp.int32

CHUNK = 64
EPS = 1e-6
HEADS = 4
Q_LORA = 384
KV_LORA = 256
NOPE = 128
ROPE = 64
HALF = ROPE // 2
V_DIM = 128
QK = NOPE + ROPE
QK_PAD = 256
ROPE_THETA = 10000.0
GLA_DK = 64
GLA_DV = 128
GATE_RANK = 16
GATE_NORM = 16.0
N_EXPERTS = 32
TOP_K = 4
SWIGLU_LIMIT = 7.0
SWIGLU_ALPHA = 1.702
NEG = -1e30
LOG2_E = 1.4426950408889634

LANES = 128
BF16_ROWS = 16
ROUTE_TILE = 512
RUN_CHUNK = BF16_ROWS
SORT_ROWS = 512
CHUNK_UNROLL = 4
EXPERT_ROWS = 256
ATTN_TILE = 1024
VMEM_BIG = 56 * 1024 * 1024


def _cparams(sem, vmem=None):
    return pltpu.CompilerParams(dimension_semantics=sem, vmem_limit_bytes=vmem)


def _nt(a, b):
    return lax.dot_general(a, b, (((1,), (1,)), ((), ())), preferred_element_type=F32)


def _rms(x, width):
    return lax.rsqrt(jnp.sum(x * x, axis=-1, keepdims=True) * (1.0 / width) + EPS)


def _round_up(x, m):
    return ((x + m - 1) // m) * m


def _ada_body(c_ref, w_ref, b_ref, o_ref):
    c = c_ref[...]
    s = (c * jax.nn.sigmoid(c)).astype(BF16)
    o_ref[...] = jnp.dot(s, w_ref[...].astype(BF16), preferred_element_type=F32) + b_ref[...]


def _ada(c, w_ada, b_ada):
    r, d = c.shape
    n = w_ada.shape[1]
    tn = 1536 if n % 1536 == 0 else n
    return pl.pallas_call(
        _ada_body,
        grid=(n // tn,),
        in_specs=[pl.BlockSpec((r, d), lambda j: (0, 0)),
                  pl.BlockSpec((d, tn), lambda j: (0, j)),
                  pl.BlockSpec((1, tn), lambda j: (0, j))],
        out_specs=pl.BlockSpec((r, tn), lambda j: (0, j)),
        out_shape=jax.ShapeDtypeStruct((r, n), F32),
        compiler_params=_cparams(("arbitrary",), 40 * 1024 * 1024),
        name="ada",
    )(c, w_ada, b_ada.reshape(1, n))


_SEG = dict(qa_kr=(0, 512), kva=(512, 768), gq=(768, 1024), gk=(1024, 1280),
            gv=(1280, 1792), gr=(1792, 2304), glr=(2304, 2432))
_W1_COLS = 2432


def _proj_body(pos0, ts, x_ref, sh_ref, sc_ref, gmix_ref, w1_ref, gqa_ref, wuq_ref, gkv_ref, gqk_ref,
               rope_ref, wg2_ref, bg2_ref,
               q_ref, lat_ref, kr_ref, gq_o, gk_o, gv_o, gl_o, gr_o, trig_scr):
    i = pl.program_id(1)
    x = x_ref[...]
    d = x.shape[-1]
    h = (x * _rms(x, d) * gmix_ref[...]) * (1.0 + sc_ref[...]) + sh_ref[...]
    hb = h.astype(BF16)

    def seg(name):
        a, b = _SEG[name]
        return jnp.dot(hb, w1_ref[:, a:b], preferred_element_type=F32)

    @pl.when((pl.program_id(0) == 0) & (i == 0))
    def _():
        row_ang = lax.broadcasted_iota(I32, (ts, LANES), 0).astype(F32) * rope_ref[0:1, :]
        trig_scr[0] = jnp.cos(row_ang)
        trig_scr[1] = jnp.sin(row_ang)

    base_ang = jnp.broadcast_to((pos0 + i * ts).astype(F32) * rope_ref[0:1, :], (8, LANES))
    cos_a, sin_a = jnp.cos(base_ang)[0:1, :], jnp.sin(base_ang)[0:1, :]
    cos = cos_a * trig_scr[0] - sin_a * trig_scr[1]
    sin = (sin_a * trig_scr[0] + cos_a * trig_scr[1]) * rope_ref[1:2, :]
    lane = lax.broadcasted_iota(I32, (ts, LANES), 1)
    first_half = (lane & HALF) == 0
    low64 = lane < ROPE

    def rope(v):
        partner = jnp.where(first_half, pltpu.roll(v, LANES - HALF, 1), pltpu.roll(v, HALF, 1))
        return v * cos + partner * sin

    qa_kr = seg("qa_kr")
    qa = qa_kr[:, 0:Q_LORA]
    qn = (qa * _rms(qa, Q_LORA) * gqa_ref[...]).astype(BF16)
    qf = jnp.dot(qn, wuq_ref[...], preferred_element_type=F32)
    rope_blocks = (rope(qf[:, 4 * NOPE:4 * NOPE + LANES]), rope(qf[:, 4 * NOPE + LANES:4 * NOPE + 2 * LANES]))
    for hd in range(HEADS):
        nope = qf[:, NOPE * hd:NOPE * (hd + 1)]
        blk = rope_blocks[hd // 2]
        if hd % 2:
            blk = pltpu.roll(blk, ROPE, 1)
        blk = jnp.where(low64, blk, 0.0)
        ss = jnp.sum(nope * nope, axis=-1, keepdims=True) + jnp.sum(blk * blk, axis=-1, keepdims=True)
        scl = lax.rsqrt(ss * (1.0 / QK) + EPS) * (QK ** -0.5 * LOG2_E)
        q_ref[hd, :, 0:NOPE] = (nope * scl * gqk_ref[0:1, :]).astype(BF16)
        q_ref[hd, :, NOPE:QK_PAD] = (blk * scl * gqk_ref[1:2, :]).astype(BF16)

    kva = seg("kva")
    lat_ref[...] = kva * _rms(kva, KV_LORA) * gkv_ref[...]
    kr_ref[...] = rope(qa_kr[:, Q_LORA:Q_LORA + LANES])[:, 0:ROPE]

    gq_o[...] = seg("gq") * (GLA_DK ** -0.5)
    gk_o[...] = seg("gk")
    gv_o[...] = seg("gv").astype(BF16)
    gr_o[...] = seg("gr")
    z = jnp.dot(seg("glr").astype(BF16), wg2_ref[...], preferred_element_type=F32) + bg2_ref[...]
    gl_o[...] = (jnp.minimum(z, 0.0) - jnp.log1p(jnp.exp(-jnp.abs(z)))) * (1.0 / GATE_NORM)


def _proj(x, shift, scale, pos0, wts):
    b, s, d = x.shape
    ts = min(s, 512)
    row = lambda a: pl.BlockSpec(a.shape, lambda bi, i: (0,) * a.ndim)
    tok = lambda w: pl.BlockSpec((None, ts, w), lambda bi, i: (bi, i, 0))
    mod = pl.BlockSpec((None, 1, d), lambda bi, i: (bi, 0, 0))
    small = [wts["g_mix"], wts["w1"], wts["g_qa"], wts["w_uq"], wts["g_kv"], wts["g_qk_q"], wts["rope"],
             wts["w_g2"], wts["b_g2"]]
    out_shape = (
        jax.ShapeDtypeStruct((b, HEADS, s, QK_PAD), BF16),
        jax.ShapeDtypeStruct((b, s, KV_LORA), F32),
        jax.ShapeDtypeStruct((b, s, ROPE), F32),
        jax.ShapeDtypeStruct((b, s, HEADS * GLA_DK), F32),
        jax.ShapeDtypeStruct((b, s, HEADS * GLA_DK), F32),
        jax.ShapeDtypeStruct((b, s, HEADS * GLA_DV), BF16),
        jax.ShapeDtypeStruct((b, s, HEADS * GLA_DK), F32),
        jax.ShapeDtypeStruct((b, s, HEADS * GLA_DV), F32),
    )
    out_specs = (
        pl.BlockSpec((None, HEADS, ts, QK_PAD), lambda bi, i: (bi, 0, i, 0)),
        tok(KV_LORA), tok(ROPE), tok(HEADS * GLA_DK), tok(HEADS * GLA_DK), tok(HEADS * GLA_DV),
        tok(HEADS * GLA_DK), tok(HEADS * GLA_DV),
    )
    return pl.pallas_call(
        functools.partial(_proj_body, pos0, ts),
        grid=(b, s // ts),
        in_specs=[tok(d), mod, mod] + [row(a) for a in small],
        out_specs=out_specs,
        out_shape=out_shape,
        scratch_shapes=[pltpu.VMEM((2, ts, LANES), F32)],
        compiler_params=_cparams(("arbitrary", "arbitrary"), VMEM_BIG),
        name="proj",
    )(x, shift, scale, *small)


def _key_rows(lat, kr, wk_ref, gk_ref, k_out):
    kn_all = jnp.dot(lat, wk_ref[...], preferred_element_type=F32)
    kr_ss = jnp.sum(kr * kr, axis=-1, keepdims=True)
    for hd in range(HEADS):
        kn = kn_all[:, NOPE * hd:NOPE * (hd + 1)]
        scl = lax.rsqrt((jnp.sum(kn * kn, axis=-1, keepdims=True) + kr_ss) * (1.0 / QK) + EPS)
        k_out[hd, :, 0:NOPE] = (kn * scl * gk_ref[0:1, :]).astype(BF16)
        k_out[hd, :, NOPE:QK] = (kr * scl * gk_ref[1:2, 0:ROPE]).astype(BF16)
        k_out[hd, :, QK:QK_PAD] = jnp.zeros((kr.shape[0], QK_PAD - QK), BF16)


def _kv_body(lat_ref, kr_ref, wk_ref, wv_ref, gk_ref, k_ref, v_ref):
    lat = lat_ref[...].astype(BF16)
    _key_rows(lat, kr_ref[...], wk_ref, gk_ref, k_ref)
    v_t = _nt(wv_ref[...], lat)
    for hd in range(HEADS):
        v_ref[hd] = v_t[V_DIM * hd:V_DIM * (hd + 1), :].astype(BF16)


def _kv(lat, kr, w_uk, w_uv_t, g_qk_k):
    b, s, _ = lat.shape
    ts = min(s, ATTN_TILE)
    return pl.pallas_call(
        _kv_body,
        grid=(b, s // ts),
        in_specs=[pl.BlockSpec((None, ts, KV_LORA), lambda bi, i: (bi, i, 0)),
                  pl.BlockSpec((None, ts, ROPE), lambda bi, i: (bi, i, 0)),
                  pl.BlockSpec(w_uk.shape, lambda bi, i: (0, 0)),
                  pl.BlockSpec(w_uv_t.shape, lambda bi, i: (0, 0)),
                  pl.BlockSpec(g_qk_k.shape, lambda bi, i: (0, 0))],
        out_specs=(pl.BlockSpec((None, HEADS, ts, QK_PAD), lambda bi, i: (bi, 0, i, 0)),
                   pl.BlockSpec((None, HEADS, None, V_DIM, ts), lambda bi, i: (bi, 0, i, 0, 0))),
        out_shape=(jax.ShapeDtypeStruct((b, HEADS, s, QK_PAD), BF16),
                   jax.ShapeDtypeStruct((b, HEADS, s // ts, V_DIM, ts), BF16)),
        compiler_params=_cparams(("arbitrary", "arbitrary")),
        name="kv",
    )(lat, kr, w_uk, w_uv_t, g_qk_k)


def _attn_prompt_body(t, q_ref, qn_ref, k_ref, vt_ref, o_ref, s_a, s_b):
    i = pl.program_id(2)

    def scores(q, j, buf):
        buf[...] = _nt(k_ref[pl.ds(pl.multiple_of(j * t, t), t), :], q)

    def consume(j, buf, carry, masked=False):
        m, l, acc = carry
        s = buf[...]
        if masked:
            visible = (lax.broadcasted_iota(I32, (t, t), 0) // CHUNK) <= (lax.broadcasted_iota(I32, (t, t), 1) // CHUNK)
            s = jnp.where(visible, s, NEG)
        m_new = jnp.maximum(m, jnp.max(s, axis=0, keepdims=True))
        alpha = jnp.exp2(m - m_new)
        p = jnp.exp2(s - m_new)
        l = alpha * l + jnp.sum(p, axis=0, keepdims=True)
        acc = alpha * acc + jnp.dot(vt_ref[j], p.astype(BF16), preferred_element_type=F32)
        return m_new, l, acc

    def run(first, second):
        q = q_ref[...]

        @pl.when(i == 0)
        def _():
            scores(q, 0, first)

        def pair(pp, carry):
            j = 2 * pp
            scores(q, j + 1, second)
            carry = consume(j, first, carry)
            scores(q, j + 2, first)
            return consume(j + 1, second, carry)

        def even_tail(carry):
            scores(qn_ref[...], 0, second)
            return consume(i, first, carry, masked=True)

        def odd_tail(carry):
            scores(q, i, second)
            carry = consume(i - 1, first, carry)
            scores(qn_ref[...], 0, first)
            return consume(i, second, carry, masked=True)

        carry = (jnp.full((1, t), NEG, F32), jnp.zeros((1, t), F32), jnp.zeros((V_DIM, t), F32))
        carry = lax.fori_loop(0, i // 2, pair, carry)
        _, l, acc = lax.cond(i % 2 == 1, odd_tail, even_tail, carry)
        o_ref[...] = (acc / l).T.astype(BF16)

    @pl.when(((i + 1) // 2) % 2 == 0)
    def _():
        run(s_a, s_b)

    @pl.when(((i + 1) // 2) % 2 == 1)
    def _():
        run(s_b, s_a)


def _attn_prompt(q, k, v_t):
    b, _, s, _ = q.shape
    t = v_t.shape[-1]
    nq = s // t
    return pl.pallas_call(
        functools.partial(_attn_prompt_body, t),
        grid=(b, HEADS, nq),
        in_specs=[pl.BlockSpec((None, None, t, QK_PAD), lambda bi, h, i: (bi, h, i, 0)),
                  pl.BlockSpec((None, None, t, QK_PAD), lambda bi, h, i: (bi, h, jnp.minimum(i + 1, nq - 1), 0)),
                  pl.BlockSpec((None, None, s, QK_PAD), lambda bi, h, i: (bi, h, 0, 0)),
                  pl.BlockSpec((None, None, nq, V_DIM, t), lambda bi, h, i: (bi, h, 0, 0, 0))],
        out_specs=pl.BlockSpec((None, t, V_DIM), lambda bi, h, i: (bi, i, h)),
        out_shape=jax.ShapeDtypeStruct((b, s, HEADS * V_DIM), BF16),
        scratch_shapes=[pltpu.VMEM((t, t), F32), pltpu.VMEM((t, t), F32)],
        compiler_params=_cparams(("arbitrary", "arbitrary", "arbitrary"), VMEM_BIG),
        name="attn_prompt",
    )(q, q, k, v_t)


def _attn_sample_body(past, sq, q_ref, plat_ref, pkr_ref, nlat_ref, nkr_ref, wk_ref, wv_ref, gk_ref, o_ref,
                      kp_scr, kn_scr):
    plat = plat_ref[...].astype(BF16)
    nlat = nlat_ref[...].astype(BF16)
    _key_rows(plat, pkr_ref[...], wk_ref, gk_ref, kp_scr)
    _key_rows(nlat, nkr_ref[...], wk_ref, gk_ref, kn_scr)
    vp_t = _nt(wv_ref[...], plat).astype(BF16)
    vn_t = _nt(wv_ref[...], nlat).astype(BF16)
    key_chunk = (past + lax.broadcasted_iota(I32, (sq, sq), 0)) // CHUNK
    qry_chunk = (past + lax.broadcasted_iota(I32, (sq, sq), 1)) // CHUNK
    for hd in range(HEADS):
        q = q_ref[hd]
        s_p = _nt(kp_scr[hd], q)
        s_n = jnp.where(key_chunk <= qry_chunk, _nt(kn_scr[hd], q), NEG)
        m = jnp.maximum(jnp.max(s_p, axis=0, keepdims=True), jnp.max(s_n, axis=0, keepdims=True))
        p_p = jnp.exp2(s_p - m)
        p_n = jnp.exp2(s_n - m)
        l = jnp.sum(p_p, axis=0, keepdims=True) + jnp.sum(p_n, axis=0, keepdims=True)
        rows = slice(V_DIM * hd, V_DIM * (hd + 1))
        o_t = (jnp.dot(vp_t[rows, :], p_p.astype(BF16), preferred_element_type=F32)
               + jnp.dot(vn_t[rows, :], p_n.astype(BF16), preferred_element_type=F32))
        o_ref[:, rows] = (o_t / l).T.astype(BF16)


def _attn_sample(q, past_lat, past_kr, lat, kr, w_uk, w_uv_t, g_qk_k):
    b, _, sq, _ = q.shape
    past = past_lat.shape[1]
    rows = lambda n, w: pl.BlockSpec((None, n, w), lambda bi: (bi, 0, 0))
    full = lambda a: pl.BlockSpec(a.shape, lambda bi: (0,) * a.ndim)
    return pl.pallas_call(
        functools.partial(_attn_sample_body, past, sq),
        grid=(b,),
        in_specs=[pl.BlockSpec((None, HEADS, sq, QK_PAD), lambda bi: (bi, 0, 0, 0)),
                  rows(past, KV_LORA), rows(past, ROPE), rows(sq, KV_LORA), rows(sq, ROPE),
                  full(w_uk), full(w_uv_t), full(g_qk_k)],
        out_specs=rows(sq, HEADS * V_DIM),
        out_shape=jax.ShapeDtypeStruct((b, sq, HEADS * V_DIM), BF16),
        scratch_shapes=[pltpu.VMEM((HEADS, past, QK_PAD), BF16), pltpu.VMEM((HEADS, sq, QK_PAD), BF16)],
        compiler_params=_cparams(("arbitrary",), 40 * 1024 * 1024),
        name="attn_sample",
    )(q, past_lat, past_kr, lat, kr, w_uk, w_uv_t, g_qk_k)


def _gla_masks(c, rows):
    idx = np.arange(rows)
    same = (idx // c)[:, None] == (idx // c)[None, :]
    le = same & (idx[None, :] <= idx[:, None])
    gt = same & (idx[None, :] > idx[:, None])
    return np.concatenate([le, gt], axis=0).astype(np.float32), int(np.log2(c))


def _level_exponents(b, g, c, level):
    n = c >> level
    rows = b.shape[0]
    row = lax.broadcasted_iota(I32, (rows, 1), 0)
    if n >= 8:
        split = b.reshape(rows // n, n, LANES)[:, n // 2 - 1:n // 2, :]
        split = jnp.broadcast_to(split, (rows // n, n, LANES)).reshape(rows, LANES)
        return jnp.where((row & (n // 2)) != 0, b - split, split - b)
    g_prev = pltpu.roll(g, 1, 0)
    g_next = pltpu.roll(g, rows - 1, 0)
    if n == 4:
        r = row & 3
        return jnp.where(r == 0, g_next, jnp.where(r == 1, 0.0, jnp.where(r == 2, g, g + g_prev)))
    assert n == 2
    return jnp.where((row & 1) != 0, g, 0.0)


def _gla_body(c, n_chunks, unit, levels, mall_ref, q_ref, k_ref, g_ref, v_ref, r_ref, s0_ref, gout_ref,
              o_ref, sfin_ref, st_scr):
    it = pl.program_id(1)

    @pl.when(it == 0)
    def _():
        st_scr[...] = s0_ref[...]

    ru = unit * c
    lane = lax.broadcasted_iota(I32, (ru, LANES), 1)
    head_lanes = (lane < GLA_DK, lane >= GLA_DK)
    st_lane_lo = lax.broadcasted_iota(I32, (GLA_DV, LANES), 1) < GLA_DK
    row = lax.broadcasted_iota(I32, (ru, 1), 0)
    ri = lax.broadcasted_iota(I32, (ru, ru), 0)
    ci = lax.broadcasted_iota(I32, (ru, ru), 1)
    mall = mall_ref[...]

    for un in range(n_chunks // unit):
        rows = slice(un * ru, (un + 1) * ru)
        for p in range(HEADS // 2):
            ls = slice(LANES * p, LANES * (p + 1))
            g = g_ref[rows, ls]
            q = q_ref[rows, ls]
            k = k_ref[rows, ls]
            g_hi = g.astype(BF16)
            g_lo = (g - g_hi.astype(F32)).astype(BF16)
            e2 = jnp.dot(mall, jnp.concatenate([g_hi, g_lo], axis=1), preferred_element_type=F32)
            e = e2[:, 0:LANES] + e2[:, LANES:2 * LANES]
            b = e[0:ru]
            eb = jnp.exp(b)
            qb = q * eb
            kd = (k * jnp.exp(e[ru:2 * ru])).astype(BF16)
            qs, ks = [q], [k.astype(BF16)]
            for l in range(levels):
                bottom = (row & (c >> (l + 1))) != 0
                decay = jnp.exp(_level_exponents(b, g, c, l))
                qs.append(jnp.where(bottom, q * decay, 0.0))
                ks.append(jnp.where(bottom, 0.0, k * decay).astype(BF16))
            states = [st_scr[p]]
            for j in range(unit):
                cr = slice(j * c, (j + 1) * c)
                upd = [lax.dot_general(v_ref[rows, GLA_DV * (2 * p + hh):GLA_DV * (2 * p + hh + 1)][cr, :], kd[cr, :],
                                       (((0,), (0,)), ((), ())), preferred_element_type=F32) for hh in range(2)]
                d_last = eb[j * c + c - 1:j * c + c, :]
                states.append(states[-1] * d_last + jnp.where(st_lane_lo, upd[0], upd[1]))
            st_scr[p] = states[-1]
            for hh in range(2):
                hd = 2 * p + hh
                sel = head_lanes[hh]
                a = jnp.where(ri == ci, _nt(jnp.where(sel, qs[0], 0.0).astype(BF16), ks[0]), 0.0)
                for l in range(levels):
                    pr = _nt(jnp.where(sel, qs[l + 1], 0.0).astype(BF16), ks[l + 1])
                    a = a + jnp.where((ri ^ ci) < (c >> l), pr, 0.0)
                vh = v_ref[rows, GLA_DV * hd:GLA_DV * (hd + 1)]
                qb_h = jnp.where(sel, qb, 0.0).astype(BF16)
                o_state = [_nt(qb_h[j * c:(j + 1) * c, :], states[j].astype(BF16)) for j in range(unit)]
                o = jnp.dot(a.astype(BF16), vh, preferred_element_type=F32) + jnp.concatenate(o_state, axis=0)
                on = o * _rms(o, GLA_DV) * gout_ref[...]
                r = r_ref[rows, GLA_DV * hd:GLA_DV * (hd + 1)]
                o_ref[rows, GLA_DV * hd:GLA_DV * (hd + 1)] = (on * (r * jax.nn.sigmoid(r))).astype(BF16)

    @pl.when(it == pl.num_programs(1) - 1)
    def _():
        sfin_ref[...] = st_scr[...]


def _gla(gq, gk, gl, gv, gr, s0, g_out):
    b, s, _ = gq.shape
    c = min(CHUNK, s)
    tile = min(s, 8 * c)
    unit = 2 if (tile // c) % 2 == 0 else 1
    masks, levels = _gla_masks(c, unit * c)
    mall = jnp.asarray(masks, BF16)
    tok = lambda w: pl.BlockSpec((None, tile, w), lambda bi, i: (bi, i, 0))
    st_spec = pl.BlockSpec((None, HEADS // 2, GLA_DV, LANES), lambda bi, i: (bi, 0, 0, 0))
    return pl.pallas_call(
        functools.partial(_gla_body, c, tile // c, unit, levels),
        grid=(b, s // tile),
        in_specs=[pl.BlockSpec(mall.shape, lambda bi, i: (0, 0)),
                  tok(HEADS * GLA_DK), tok(HEADS * GLA_DK), tok(HEADS * GLA_DK), tok(HEADS * GLA_DV),
                  tok(HEADS * GLA_DV), st_spec, pl.BlockSpec(g_out.shape, lambda bi, i: (0, 0))],
        out_specs=(tok(HEADS * GLA_DV), st_spec),
        out_shape=(jax.ShapeDtypeStruct((b, s, HEADS * GLA_DV), BF16),
                   jax.ShapeDtypeStruct((b, HEADS // 2, GLA_DV, LANES), F32)),
        scratch_shapes=[pltpu.VMEM((HEADS // 2, GLA_DV, LANES), F32)],
        compiler_params=_cparams(("arbitrary", "arbitrary")),
        name="gla",
    )(mall, gq, gk, gl, gv, gr, s0, g_out)


def _state_to_pairs(s):
    b = s.shape[0]
    s = s.reshape(b, HEADS // 2, 2, GLA_DK, GLA_DV)
    return jnp.transpose(s, (0, 1, 4, 2, 3)).reshape(b, HEADS // 2, GLA_DV, 2 * GLA_DK)


def _state_from_pairs(s):
    b = s.shape[0]
    s = s.reshape(b, HEADS // 2, GLA_DV, 2, GLA_DK)
    return jnp.transpose(s, (0, 1, 3, 4, 2)).reshape(b, HEADS, GLA_DK, GLA_DV)


def _post_body(x_ref, om_ref, og_ref, gt_ref, sc_ref, sh_ref, wo_ref, gffn_ref, wr_ref, br_ref,
               x2_ref, h_ref, idx_ref, wt_ref, rank_ref, cnt_ref):
    half = om_ref.shape[-1]
    mix = (jnp.dot(om_ref[...], wo_ref[0:half, :], preferred_element_type=F32)
           + jnp.dot(og_ref[...], wo_ref[half:2 * half, :], preferred_element_type=F32))
    x2 = x_ref[...] + gt_ref[...] * mix
    x2_ref[...] = x2
    d = x2.shape[-1]
    h = (x2 * _rms(x2, d) * gffn_ref[...]) * (1.0 + sc_ref[...]) + sh_ref[...]
    h_hi = h.astype(BF16)
    h_ref[...] = h_hi
    h_lo = (h - h_hi.astype(F32)).astype(BF16)
    logits = _nt(wr_ref[0], h_hi) + _nt(wr_ref[0], h_lo) + _nt(wr_ref[1], h_hi) + br_ref[...]
    n_exp, tm = logits.shape
    eid = lax.broadcasted_iota(I32, (n_exp, tm), 0)
    vals, tops, ids = logits, [], []
    for _ in range(TOP_K):
        m = jnp.max(vals, axis=0, keepdims=True)
        sel = jnp.min(jnp.where(vals == m, eid, n_exp), axis=0, keepdims=True)
        tops.append(m)
        ids.append(sel)
        vals = jnp.where(eid == sel, -jnp.inf, vals)
    es = [jnp.exp(t - tops[0]) for t in tops]
    tot = es[0] + es[1] + es[2] + es[3]
    idx_ref[...] = jnp.concatenate(ids, axis=0)
    wt_ref[...] = jnp.concatenate([e / tot for e in es], axis=0)
    hits = [eid == sel for sel in ids]
    member = jnp.zeros((n_exp, tm), F32)
    for hk in hits:
        member = member + jnp.where(hk, 1.0, 0.0)
    before = lax.broadcasted_iota(I32, (tm, tm), 0) < lax.broadcasted_iota(I32, (tm, tm), 1)
    prefix = jnp.dot(member.astype(BF16), jnp.where(before, 1.0, 0.0).astype(BF16), preferred_element_type=F32)
    rank_ref[...] = jnp.concatenate(
        [jnp.sum(jnp.where(hk, prefix, 0.0), axis=0, keepdims=True) for hk in hits], axis=0).astype(I32)
    cnt_ref[...] = jnp.broadcast_to(jnp.sum(member, axis=1, keepdims=True), (n_exp, LANES)).astype(I32)


def _post(x, om, og, gate, scale, shift, w_o, g_ffn, w_r2, b_r):
    t, d = x.shape
    tm = ROUTE_TILE
    per_tok = gate.shape[0] == t
    mod = pl.BlockSpec((tm, d), lambda i: (i, 0)) if per_tok else pl.BlockSpec((1, d), lambda i: (0, 0))
    tok = lambda w: pl.BlockSpec((tm, w), lambda i: (i, 0))
    full = lambda a: pl.BlockSpec(a.shape, lambda i: (0,) * a.ndim)
    return pl.pallas_call(
        _post_body,
        grid=(t // tm,),
        in_specs=[tok(d), tok(om.shape[1]), tok(og.shape[1]), mod, mod, mod, full(w_o), full(g_ffn), full(w_r2),
                  full(b_r)],
        out_specs=(tok(d), tok(d),
                   pl.BlockSpec((TOP_K, tm), lambda i: (0, i)), pl.BlockSpec((TOP_K, tm), lambda i: (0, i)),
                   pl.BlockSpec((TOP_K, tm), lambda i: (0, i)),
                   pl.BlockSpec((None, N_EXPERTS, LANES), lambda i: (i, 0, 0))),
        out_shape=(jax.ShapeDtypeStruct((t, d), F32), jax.ShapeDtypeStruct((t, d), BF16),
                   jax.ShapeDtypeStruct((TOP_K, t), I32), jax.ShapeDtypeStruct((TOP_K, t), F32),
                   jax.ShapeDtypeStruct((TOP_K, t), I32), jax.ShapeDtypeStruct((t // tm, N_EXPERTS, LANES), I32)),
        compiler_params=_cparams(("arbitrary",), 40 * 1024 * 1024),
        name="post",
    )(x, om, og, gate, scale, shift, w_o, g_ffn, w_r2, b_r)


def _route_tables(idx, lrank, cnt3):
    nt = cnt3.shape[0]
    t = idx.shape[1]
    cnt = cnt3[:, :, 0]
    run = _round_up(cnt, RUN_CHUNK)
    lo_end = jnp.cumsum(run, axis=1)
    lo = lo_end - run
    n_chunks = lo_end[:, -1] // RUN_CHUNK
    region = _round_up(jnp.sum(run, axis=0), EXPERT_ROWS)
    g_end = jnp.cumsum(region)
    run_dest = (g_end - region)[None, :] + jnp.cumsum(run, axis=0) - run
    max_rows = TOP_K * ROUTE_TILE + N_EXPERTS * (RUN_CHUNK - 1)
    n_tab = _round_up(max_rows, RUN_CHUNK) // RUN_CHUNK
    c_start = jnp.arange(n_tab, dtype=I32) * RUN_CHUNK
    e_of_c = jnp.minimum(jnp.sum(lo_end[:, None, :] <= c_start[None, :, None], axis=2), N_EXPERTS - 1)
    pick = e_of_c[:, :, None] == jnp.arange(N_EXPERTS, dtype=I32)[None, None, :]
    chunk_dest = jnp.sum(jnp.where(pick, (run_dest - lo)[:, None, :], 0), axis=2) + c_start[None, :]
    table = jnp.concatenate([chunk_dest, n_chunks[:, None]], axis=1).astype(I32).reshape(nt, 1, n_tab + 1)
    eid = jnp.arange(N_EXPERTS, dtype=I32)[:, None]
    lo_tok = jnp.repeat(lo.T, ROUTE_TILE, axis=1)
    lpos = jnp.stack([jnp.sum(jnp.where(idx[k][None, :] == eid, lo_tok, 0), axis=0) for k in range(TOP_K)])
    lpos = (lpos + lrank).astype(I32)
    n_blocks = _round_up(t * TOP_K + nt * N_EXPERTS * (RUN_CHUNK - 1), EXPERT_ROWS) // EXPERT_ROWS + N_EXPERTS
    b_start = jnp.arange(n_blocks, dtype=I32) * EXPERT_ROWS
    blk_e = jnp.minimum(jnp.sum(g_end[None, :] <= b_start[:, None], axis=1), N_EXPERTS - 1).astype(I32)
    n_valid = (g_end[-1:] // EXPERT_ROWS).astype(I32)
    used = region > 0
    e_ids = jnp.arange(N_EXPERTS, dtype=I32)
    later_used = used[None, :] & (e_ids[None, :] > e_ids[:, None])
    nxt_e = jnp.min(jnp.where(later_used, e_ids[None, :], N_EXPERTS), axis=1)
    nxt_e = jnp.where(nxt_e < N_EXPERTS, nxt_e, -1).astype(I32)
    ord_e = (jnp.cumsum(used.astype(I32)) - 1).astype(I32)
    tail = jnp.concatenate([jnp.where(used, g_end - EXPERT_ROWS, -1), n_valid]).astype(I32)
    tail = tail.reshape(1, N_EXPERTS + 1)
    return dict(table=table, lpos=lpos, blk_e=blk_e, n_valid=n_valid, nxt_e=nxt_e, ord_e=ord_e, tail=tail,
                n_blocks=n_blocks)


def _chunk_copy(src, dst, sem):
    return pltpu.make_async_copy(src, dst, sem)


def _for_chunks(n, body):
    groups = n // CHUNK_UNROLL

    def group(g, carry):
        for u in range(CHUNK_UNROLL):
            body(g * CHUNK_UNROLL + u)
        return carry

    def single(c, carry):
        body(c)
        return carry

    lax.fori_loop(0, groups, group, 0)
    lax.fori_loop(groups * CHUNK_UNROLL, n, single, 0)


def _scatter_body(n_tab, nt_a, n_blocks, tab_ref, prv_ref, tail_ref, lpos_ref, ha_ref, hb_ref, xout_ref, sorted_scr,
                  zero_scr, sems, zero_sem):
    i = pl.program_id(0)
    slot = i % 2
    n_chunks = tab_ref[0, n_tab]
    tt = ha_ref.shape[0]

    @pl.when(i == 0)
    def _():
        zero_scr[...] = jnp.zeros_like(zero_scr)
        n_valid = tail_ref[0, N_EXPERTS]

        def block(start):
            return xout_ref.at[pl.ds(pl.multiple_of(start, EXPERT_ROWS), EXPERT_ROWS)]

        for e in range(N_EXPERTS):
            @pl.when(tail_ref[0, e] >= 0)
            def _():
                _chunk_copy(zero_scr, block(tail_ref[0, e]), zero_sem).start()

        def fill(b, carry):
            _chunk_copy(zero_scr, block(b * EXPERT_ROWS), zero_sem).start()
            return carry

        def fill_done(b, carry):
            _chunk_copy(zero_scr, block(0), zero_sem).wait()
            return carry

        lax.fori_loop(n_valid, n_blocks, fill, 0)
        for e in range(N_EXPERTS):
            @pl.when(tail_ref[0, e] >= 0)
            def _():
                _chunk_copy(zero_scr, block(0), zero_sem).wait()
        lax.fori_loop(n_valid, n_blocks, fill_done, 0)

    lp16 = lpos_ref[...].astype(jnp.int16)
    h = jnp.where(i < nt_a, ha_ref[...], hb_ref[...])

    def sort_block(rb, carry):
        r0 = pl.multiple_of(rb * SORT_ROWS, SORT_ROWS)
        rid = (r0 + lax.broadcasted_iota(I32, (SORT_ROWS, tt), 0)).astype(jnp.int16)
        onehot = jnp.zeros((SORT_ROWS, tt), BF16)
        for k in range(TOP_K):
            onehot = onehot + jnp.where(lp16[k:k + 1, :] == rid, jnp.ones((), BF16), jnp.zeros((), BF16))
        sorted_scr[slot, pl.ds(r0, SORT_ROWS), :] = jnp.dot(onehot, h,
                                                            preferred_element_type=F32).astype(BF16)
        return carry

    lax.fori_loop(0, (n_chunks * RUN_CHUNK + SORT_ROWS - 1) // SORT_ROWS, sort_block, 0)

    def issue(c):
        src = sorted_scr.at[slot, pl.ds(pl.multiple_of(c * RUN_CHUNK, RUN_CHUNK), RUN_CHUNK)]
        dst = xout_ref.at[pl.ds(pl.multiple_of(tab_ref[0, c], RUN_CHUNK), RUN_CHUNK)]
        _chunk_copy(src, dst, sems.at[slot]).start()

    def drain(sl):
        def body(c):
            _chunk_copy(sorted_scr.at[sl, pl.ds(0, RUN_CHUNK)], xout_ref.at[pl.ds(0, RUN_CHUNK)], sems.at[sl]).wait()
        return body

    _for_chunks(n_chunks, issue)

    @pl.when(i > 0)
    def _():
        _for_chunks(prv_ref[0, n_tab], drain(1 - slot))

    @pl.when(i == pl.num_programs(0) - 1)
    def _():
        _for_chunks(n_chunks, drain(slot))


def _scatter(table, tail, lpos, h_a, h_b, n_blocks):
    d = h_a.shape[1]
    nt_a, nt_b = h_a.shape[0] // ROUTE_TILE, h_b.shape[0] // ROUTE_TILE
    n_tab = table.shape[2] - 1
    sort_cap = _round_up(n_tab * RUN_CHUNK, SORT_ROWS)
    tab_spec = lambda f: pl.BlockSpec((None, 1, n_tab + 1), lambda i: (f(i), 0, 0), memory_space=pltpu.SMEM)
    return pl.pallas_call(
        functools.partial(_scatter_body, n_tab, nt_a, n_blocks),
        grid=(nt_a + nt_b,),
        in_specs=[tab_spec(lambda i: i), tab_spec(lambda i: jnp.maximum(i - 1, 0)),
                  pl.BlockSpec((1, N_EXPERTS + 1), lambda i: (0, 0), memory_space=pltpu.SMEM),
                  pl.BlockSpec((TOP_K, ROUTE_TILE), lambda i: (0, i)),
                  pl.BlockSpec((ROUTE_TILE, d), lambda i: (jnp.minimum(i, nt_a - 1), 0)),
                  pl.BlockSpec((ROUTE_TILE, d), lambda i: (jnp.maximum(i - nt_a, 0), 0))],
        out_specs=pl.BlockSpec(memory_space=pl.ANY),
        out_shape=jax.ShapeDtypeStruct((n_blocks * EXPERT_ROWS, d), BF16),
        scratch_shapes=[pltpu.VMEM((2, sort_cap, d), BF16), pltpu.VMEM((EXPERT_ROWS, d), BF16),
                        pltpu.SemaphoreType.DMA((2,)), pltpu.SemaphoreType.DMA(())],
        compiler_params=_cparams(("arbitrary",), 48 * 1024 * 1024),
        name="scatter",
    )(table, table, tail, lpos, h_a, h_b)


def _experts_body(be_ref, nv_ref, nxt_ref, ord_ref, x_ref, wgu_hbm, bgu_ref, wd_hbm, bd_ref, y_ref,
                  wgu_f, wd_f, wgu_s, wd_s, sem_gu, sem_d):
    b = pl.program_id(0)
    e = be_ref[b]
    prev = be_ref[jnp.maximum(b - 1, 0)]
    valid = b < nv_ref[0]
    d_ff = wd_s.shape[0]
    slot = ord_ref[e] % 2

    def weights(expert, sl):
        return (pltpu.make_async_copy(wgu_hbm.at[expert], wgu_f.at[sl], sem_gu.at[sl]),
                pltpu.make_async_copy(wd_hbm.at[expert], wd_f.at[sl], sem_d.at[sl]))

    @pl.when(valid & ((b == 0) | (e != prev)))
    def _():
        @pl.when(b == 0)
        def _():
            for cp in weights(e, slot):
                cp.start()

        for cp in weights(e, slot):
            cp.wait()

        @pl.when(nxt_ref[e] >= 0)
        def _():
            for cp in weights(nxt_ref[e], 1 - slot):
                cp.start()

        wgu_s[...] = wgu_f[slot].astype(BF16)
        wd_s[...] = wd_f[slot].astype(BF16)

    @pl.when(valid)
    def _():
        gu = jnp.dot(x_ref[...], wgu_s[...], preferred_element_type=F32) + bgu_ref[...]
        gate = jnp.minimum(gu[:, 0:d_ff], SWIGLU_LIMIT)
        up = jnp.clip(gu[:, d_ff:2 * d_ff], -SWIGLU_LIMIT, SWIGLU_LIMIT)
        act = ((up + 1.0) * (gate * jax.nn.sigmoid(gate * SWIGLU_ALPHA))).astype(BF16)
        y_ref[...] = (jnp.dot(act, wd_s[...], preferred_element_type=F32) + bd_ref[...]).astype(BF16)

    @pl.when(jnp.logical_not(valid))
    def _():
        y_ref[...] = jnp.zeros_like(y_ref)


def _experts(blk_e, n_valid, nxt_e, ord_e, xpad, w_gu, b_gu, w_down, b_down):
    m, d = xpad.shape
    nb = m // EXPERT_ROWS
    n_exp, _, f2 = w_gu.shape
    d_ff = w_down.shape[1]
    last = lambda b, be, nv: jnp.minimum(b, nv[0] - 1)
    grid_spec = pltpu.PrefetchScalarGridSpec(
        num_scalar_prefetch=4,
        grid=(nb,),
        in_specs=[pl.BlockSpec((EXPERT_ROWS, d), lambda b, be, nv, nx, od: (last(b, be, nv), 0)),
                  pl.BlockSpec(memory_space=pl.ANY),
                  pl.BlockSpec((None, 1, f2), lambda b, be, nv, nx, od: (be[last(b, be, nv)], 0, 0)),
                  pl.BlockSpec(memory_space=pl.ANY),
                  pl.BlockSpec((None, 1, d), lambda b, be, nv, nx, od: (be[last(b, be, nv)], 0, 0))],
        out_specs=pl.BlockSpec((EXPERT_ROWS, d), lambda b, be, nv, nx, od: (b, 0)),
        scratch_shapes=[pltpu.VMEM((2, d, f2), F32), pltpu.VMEM((2, d_ff, d), F32),
                        pltpu.VMEM((d, f2), BF16), pltpu.VMEM((d_ff, d), BF16),
                        pltpu.SemaphoreType.DMA((2,)), pltpu.SemaphoreType.DMA((2,))],
    )
    return pl.pallas_call(
        _experts_body,
        grid_spec=grid_spec,
        out_shape=jax.ShapeDtypeStruct((m, d), BF16),
        compiler_params=_cparams(("arbitrary",), VMEM_BIG),
        name="experts",
    )(blk_e, n_valid, nxt_e, ord_e, xpad, w_gu, b_gu.reshape(n_exp, 1, f2), w_down, b_down.reshape(n_exp, 1, d))


def _combine_body(n_tab, tab_ref, nxt_ref, lpos_ref, wt_ref, x2_ref, gt_ref, y_ref, o_ref, ysort_scr, sems):
    i = pl.program_id(0)
    slot = i % 2
    n_chunks = tab_ref[0, n_tab]
    tt = x2_ref.shape[0]

    def fetch(tab, sl):
        def issue(c):
            src = y_ref.at[pl.ds(pl.multiple_of(tab[0, c], RUN_CHUNK), RUN_CHUNK)]
            dst = ysort_scr.at[sl, pl.ds(pl.multiple_of(c * RUN_CHUNK, RUN_CHUNK), RUN_CHUNK)]
            _chunk_copy(src, dst, sems.at[sl]).start()

        _for_chunks(tab[0, n_tab], issue)

    @pl.when(i == 0)
    def _():
        ysort_scr[...] = jnp.zeros_like(ysort_scr)
        fetch(tab_ref, 0)

    @pl.when(i + 1 < pl.num_programs(0))
    def _():
        fetch(nxt_ref, 1 - slot)

    def drain(c):
        _chunk_copy(y_ref.at[pl.ds(0, RUN_CHUNK)], ysort_scr.at[slot, pl.ds(0, RUN_CHUNK)], sems.at[slot]).wait()

    _for_chunks(n_chunks, drain)

    lp16 = lpos_ref[...].astype(jnp.int16)
    w = wt_ref[...].astype(BF16)
    o_ref[...] = x2_ref[...]
    gt = gt_ref[...]

    def gather_block(rb, carry):
        r0 = pl.multiple_of(rb * SORT_ROWS, SORT_ROWS)
        rid = (r0 + lax.broadcasted_iota(I32, (SORT_ROWS, tt), 0)).astype(jnp.int16)
        pw_t = jnp.zeros((SORT_ROWS, tt), BF16)
        for k in range(TOP_K):
            pw_t = pw_t + jnp.where(lp16[k:k + 1, :] == rid, w[k:k + 1, :], jnp.zeros((), BF16))
        ys = ysort_scr[slot, pl.ds(r0, SORT_ROWS), :]
        part = lax.dot_general(pw_t, ys, (((0,), (0,)), ((), ())), preferred_element_type=F32)
        o_ref[...] = o_ref[...] + gt * part
        return carry

    lax.fori_loop(0, (n_chunks * RUN_CHUNK + SORT_ROWS - 1) // SORT_ROWS, gather_block, 0)


def _combine(table, lpos, wts, x2, gate, ypad):
    t, d = x2.shape
    nt = t // ROUTE_TILE
    n_tab = table.shape[2] - 1
    per_tok = gate.shape[0] == t
    mod = pl.BlockSpec((ROUTE_TILE, d), lambda i: (i, 0)) if per_tok else pl.BlockSpec((1, d), lambda i: (0, 0))
    sort_cap = _round_up(n_tab * RUN_CHUNK, SORT_ROWS)
    tab_spec = lambda f: pl.BlockSpec((None, 1, n_tab + 1), lambda i: (f(i), 0, 0), memory_space=pltpu.SMEM)
    return pl.pallas_call(
        functools.partial(_combine_body, n_tab),
        grid=(nt,),
        in_specs=[tab_spec(lambda i: i), tab_spec(lambda i: jnp.minimum(i + 1, nt - 1)),
                  pl.BlockSpec((TOP_K, ROUTE_TILE), lambda i: (0, i)),
                  pl.BlockSpec((TOP_K, ROUTE_TILE), lambda i: (0, i)),
                  pl.BlockSpec((ROUTE_TILE, d), lambda i: (i, 0)),
                  mod,
                  pl.BlockSpec(memory_space=pl.ANY)],
        out_specs=pl.BlockSpec((ROUTE_TILE, d), lambda i: (i, 0)),
        out_shape=jax.ShapeDtypeStruct((t, d), F32),
        scratch_shapes=[pltpu.VMEM((2, sort_cap, d), BF16), pltpu.SemaphoreType.DMA((2,))],
        compiler_params=_cparams(("arbitrary",), 48 * 1024 * 1024),
        name="combine",
    )(table, table, lpos, wts, x2, gate, ypad)


def _prep_weights(g_norm_mix, w_in, g_q_a, w_uq, g_kv_a, w_ukv, g_qk_q, g_qk_k, w_g2, b_g2, g_gla_out, w_o,
                  g_norm_ffn, w_router, b_router):
    d = w_in.shape[0]
    o_qa, o_kva, o_kr = 0, Q_LORA, Q_LORA + KV_LORA
    o_gq = o_kr + ROPE
    o_gk = o_gq + HEADS * GLA_DK
    o_gv = o_gk + HEADS * GLA_DK
    o_glr = o_gv + HEADS * GLA_DV
    o_gr = o_glr + GATE_RANK
    kr_cols = w_in[:, o_kr:o_kr + ROPE]
    w1 = jnp.concatenate([
        w_in[:, o_qa:o_kva], kr_cols, kr_cols, w_in[:, o_kva:o_kr], w_in[:, o_gq:o_gk], w_in[:, o_gk:o_gv],
        w_in[:, o_gv:o_glr], w_in[:, o_gr:o_gr + HEADS * GLA_DV], w_in[:, o_glr:o_gr],
        jnp.zeros((d, LANES - GATE_RANK), w_in.dtype)], axis=1).astype(BF16)
    assert w1.shape[1] == _W1_COLS
    wq = w_uq.reshape(Q_LORA, HEADS, QK)
    wuq = jnp.concatenate([wq[:, :, 0:NOPE].reshape(Q_LORA, HEADS * NOPE),
                           wq[:, :, NOPE:QK].reshape(Q_LORA, HEADS * ROPE)], axis=1).astype(BF16)
    wkv = w_ukv.reshape(KV_LORA, HEADS, NOPE + V_DIM)
    wuk = wkv[:, :, 0:NOPE].reshape(KV_LORA, HEADS * NOPE).astype(BF16)
    wuv_t = wkv[:, :, NOPE:].reshape(KV_LORA, HEADS * V_DIM).T.astype(BF16)
    pad_rope = lambda g: jnp.stack([g[0:NOPE], jnp.concatenate([g[NOPE:QK], jnp.zeros((QK_PAD - QK,), g.dtype)])])
    inv = ROPE_THETA ** (-jnp.arange(HALF, dtype=F32) / HALF)
    sign = jnp.concatenate([-jnp.ones((HALF,), F32), jnp.ones((HALF,), F32)])
    rope_tab = jnp.stack([jnp.tile(inv, LANES // HALF), jnp.tile(sign, LANES // ROPE)])
    wg2 = jnp.concatenate([w_g2, jnp.zeros((LANES - GATE_RANK, w_g2.shape[1]), w_g2.dtype)], axis=0).astype(BF16)
    wr_t = w_router.T
    wr_hi = wr_t.astype(BF16)
    wr_lo = (wr_t - wr_hi.astype(F32)).astype(BF16)
    return dict(
        g_mix=g_norm_mix.reshape(1, d), w1=w1, g_qa=g_q_a.reshape(1, -1), w_uq=wuq, g_kv=g_kv_a.reshape(1, -1),
        w_uk=wuk, w_uv_t=wuv_t, g_qk_q=pad_rope(g_qk_q), g_qk_k=pad_rope(g_qk_k), rope=rope_tab, w_g2=wg2,
        b_g2=b_g2.reshape(1, -1), g_out=g_gla_out.reshape(1, -1), w_o=w_o.astype(BF16),
        g_ffn=g_norm_ffn.reshape(1, d), w_r2=jnp.stack([wr_hi, wr_lo]), b_r=b_router.reshape(-1, 1))


def _mixer(x, mod, pos0, past_lat, past_kr, s0_pairs, wts):
    b, s, d = x.shape
    q, lat, kr, gq, gk, gv, gl, gr = _proj(x, mod[:, 0:1], mod[:, 1:2], pos0, wts)
    kv_w = (wts["w_uk"], wts["w_uv_t"], wts["g_qk_k"])
    if past_lat is None:
        k_new, vt_new = _kv(lat, kr, *kv_w)
        o_mla = _attn_prompt(q, k_new, vt_new)
    else:
        o_mla = _attn_sample(q, past_lat, past_kr, lat, kr, *kv_w)
    o_gla, s_fin = _gla(gq, gk, gl, gv, gr, s0_pairs, wts["g_out"])
    t = b * s
    if b == 1:
        rows = lambda j: mod[0, j:j + 1]
    else:
        rows = lambda j: jnp.broadcast_to(mod[:, j:j + 1], (b, s, d)).reshape(t, d)
    x2, h2, idx, wt, lrank, cnt = _post(x.reshape(t, d), o_mla.reshape(t, -1), o_gla.reshape(t, -1), rows(2), rows(4),
                                        rows(3), wts["w_o"], wts["g_ffn"], wts["w_r2"], wts["b_r"])
    return dict(x2=x2, h2=h2, idx=idx, wt=wt, lrank=lrank, cnt=cnt, gate_f=rows(5), lat=lat, kr=kr, s_fin=s_fin)


def kernel(x_prompt, x_sample, cache_mla_latent, cache_mla_krope, state_gla, c_prompt, c_sample, w_ada, b_ada, g_norm_mix, w_in, g_q_a, w_uq, g_kv_a, w_ukv, g_qk_q, g_qk_k, w_g2, b_g2, g_gla_out, w_o, g_norm_ffn, w_router, b_router, w_gu, b_gu, w_down, b_down):
    depth = w_ada.shape[0]
    assert depth == 1, "single-layer step"
    bp, sp, d = x_prompt.shape
    bs, ss, _ = x_sample.shape
    tp, tsm = bp * sp, bs * ss
    assert tp % ROUTE_TILE == 0 and tsm % ROUTE_TILE == 0, "token counts must be whole routing tiles"
    past = cache_mla_latent.shape[2]
    layer = lambda a: a.reshape(a.shape[1:])
    wts = _prep_weights(*[layer(a) for a in (g_norm_mix, w_in, g_q_a, w_uq, g_kv_a, w_ukv, g_qk_q, g_qk_k, w_g2, b_g2,
                                             g_gla_out, w_o, g_norm_ffn, w_router, b_router)])
    w_gu, b_gu, w_down, b_down = layer(w_gu), layer(b_gu), layer(w_down), layer(b_down)

    mod = _ada(jnp.concatenate([c_prompt, c_sample], axis=0), layer(w_ada), layer(b_ada)).reshape(bp + bs, 6, d)
    zero_state = jnp.zeros((bp, HEADS // 2, GLA_DV, LANES), F32)
    pr = _mixer(x_prompt, mod[:bp], 0, None, None, zero_state, wts)
    sa = _mixer(x_sample, mod[bp:], past, layer(cache_mla_latent), layer(cache_mla_krope),
                _state_to_pairs(layer(state_gla)), wts)

    idx = jnp.concatenate([pr["idx"], sa["idx"]], axis=1)
    lrank = jnp.concatenate([pr["lrank"], sa["lrank"]], axis=1)
    rt = _route_tables(idx, lrank, jnp.concatenate([pr["cnt"], sa["cnt"]], axis=0))
    ntp = tp // ROUTE_TILE
    lpos, table = rt["lpos"], rt["table"]
    xpad = _scatter(table, rt["tail"], lpos, pr["h2"], sa["h2"], rt["n_blocks"])
    ypad = _experts(rt["blk_e"], rt["n_valid"], rt["nxt_e"], rt["ord_e"], xpad, w_gu, b_gu, w_down, b_down)
    y_p = _combine(table[:ntp], lpos[:, :tp], pr["wt"], pr["x2"], pr["gate_f"], ypad).reshape(bp, sp, d)
    y_s = _combine(table[ntp:], lpos[:, tp:], sa["wt"], sa["x2"], sa["gate_f"], ypad).reshape(bs, ss, d)

    return (y_p, y_s,
            pr["lat"][None], pr["kr"][None], _state_from_pairs(pr["s_fin"])[None],
            sa["lat"][None], sa["kr"][None], _state_from_pairs(sa["s_fin"])[None])
```

```python
import functools

import numpy as np
import jax
import jax.numpy as jnp
from jax import lax
from jax.experimental import pallas as pl
from jax.experimental.pallas import tpu as pltpu

F32 = jnp.float32
BF16 = jnp.bfloat16
I32 = jnp.int32

CHUNK = 64
EPS = 1e-6
HEADS = 4
Q_LORA = 384
KV_LORA = 256
NOPE = 128
ROPE = 64
HALF = ROPE // 2
V_DIM = 128
QK = NOPE + ROPE
QK_PAD = 256
ROPE_THETA = 10000.0
GLA_DK = 64
GLA_DV = 128
GATE_RANK = 16
GATE_NORM = 16.0
N_EXPERTS = 32
TOP_K = 4
SWIGLU_LIMIT = 7.0
SWIGLU_ALPHA = 1.702
NEG = -1e30
LOG2_E = 1.4426950408889634

LANES = 128
BF16_ROWS = 16
ROUTE_TILE = 512
RUN_CHUNK = BF16_ROWS
SORT_ROWS = 512
CHUNK_UNROLL = 4
EXPERT_ROWS = 256
ATTN_TILE = 1024
VMEM_BIG = 56 * 1024 * 1024


def _cparams(sem, vmem=None):
    return pltpu.CompilerParams(dimension_semantics=sem, vmem_limit_bytes=vmem)


def _nt(a, b):
    return lax.dot_general(a, b, (((1,), (1,)), ((), ())), preferred_element_type=F32)


def _rms(x, width):
    return lax.rsqrt(jnp.sum(x * x, axis=-1, keepdims=True) * (1.0 / width) + EPS)


def _round_up(x, m):
    return ((x + m - 1) // m) * m


def _ada_body(c_ref, w_ref, b_ref, o_ref):
    c = c_ref[...]
    s = (c * jax.nn.sigmoid(c)).astype(BF16)
    o_ref[...] = jnp.dot(s, w_ref[...].astype(BF16), preferred_element_type=F32) + b_ref[...]


def _ada(c, w_ada, b_ada):
    r, d = c.shape
    n = w_ada.shape[1]
    tn = 1536 if n % 1536 == 0 else n
    return pl.pallas_call(
        _ada_body,
        grid=(n // tn,),
        in_specs=[pl.BlockSpec((r, d), lambda j: (0, 0)),
                  pl.BlockSpec((d, tn), lambda j: (0, j)),
                  pl.BlockSpec((1, tn), lambda j: (0, j))],
        out_specs=pl.BlockSpec((r, tn), lambda j: (0, j)),
        out_shape=jax.ShapeDtypeStruct((r, n), F32),
        compiler_params=_cparams(("arbitrary",), 40 * 1024 * 1024),
        name="ada",
    )(c, w_ada, b_ada.reshape(1, n))


_SEG = dict(qa_kr=(0, 512), kva=(512, 768), gq=(768, 1024), gk=(1024, 1280),
            gv=(1280, 1792), gr=(1792, 2304), glr=(2304, 2432))
_W1_COLS = 2432


def _proj_body(pos0, ts, x_ref, sh_ref, sc_ref, gmix_ref, w1_ref, gqa_ref, wuq_ref, gkv_ref, gqk_ref,
               rope_ref, wg2_ref, bg2_ref,
               q_ref, lat_ref, kr_ref, gq_o, gk_o, gv_o, gl_o, gr_o, trig_scr):
    i = pl.program_id(1)
    x = x_ref[...]
    d = x.shape[-1]
    h = (x * _rms(x, d) * gmix_ref[...]) * (1.0 + sc_ref[...]) + sh_ref[...]
    hb = h.astype(BF16)

    def seg(name):
        a, b = _SEG[name]
        return jnp.dot(hb, w1_ref[:, a:b], preferred_element_type=F32)

    @pl.when((pl.program_id(0) == 0) & (i == 0))
    def _():
        row_ang = lax.broadcasted_iota(I32, (ts, LANES), 0).astype(F32) * rope_ref[0:1, :]
        trig_scr[0] = jnp.cos(row_ang)
        trig_scr[1] = jnp.sin(row_ang)

    base_ang = jnp.broadcast_to((pos0 + i * ts).astype(F32) * rope_ref[0:1, :], (8, LANES))
    cos_a, sin_a = jnp.cos(base_ang)[0:1, :], jnp.sin(base_ang)[0:1, :]
    cos = cos_a * trig_scr[0] - sin_a * trig_scr[1]
    sin = (sin_a * trig_scr[0] + cos_a * trig_scr[1]) * rope_ref[1:2, :]
    lane = lax.broadcasted_iota(I32, (ts, LANES), 1)
    first_half = (lane & HALF) == 0
    low64 = lane < ROPE

    def rope(v):
        partner = jnp.where(first_half, pltpu.roll(v, LANES - HALF, 1), pltpu.roll(v, HALF, 1))
        return v * cos + partner * sin

    qa_kr = seg("qa_kr")
    qa = qa_kr[:, 0:Q_LORA]
    qn = (qa * _rms(qa, Q_LORA) * gqa_ref[...]).astype(BF16)
    qf = jnp.dot(qn, wuq_ref[...], preferred_element_type=F32)
    rope_blocks = (rope(qf[:, 4 * NOPE:4 * NOPE + LANES]), rope(qf[:, 4 * NOPE + LANES:4 * NOPE + 2 * LANES]))
    for hd in range(HEADS):
        nope = qf[:, NOPE * hd:NOPE * (hd + 1)]
        blk = rope_blocks[hd // 2]
        if hd % 2:
            blk = pltpu.roll(blk, ROPE, 1)
        blk = jnp.where(low64, blk, 0.0)
        ss = jnp.sum(nope * nope, axis=-1, keepdims=True) + jnp.sum(blk * blk, axis=-1, keepdims=True)
        scl = lax.rsqrt(ss * (1.0 / QK) + EPS) * (QK ** -0.5 * LOG2_E)
        q_ref[hd, :, 0:NOPE] = (nope * scl * gqk_ref[0:1, :]).astype(BF16)
        q_ref[hd, :, NOPE:QK_PAD] = (blk * scl * gqk_ref[1:2, :]).astype(BF16)

    kva = seg("kva")
    lat_ref[...] = kva * _rms(kva, KV_LORA) * gkv_ref[...]
    kr_ref[...] = rope(qa_kr[:, Q_LORA:Q_LORA + LANES])[:, 0:ROPE]

    gq_o[...] = seg("gq") * (GLA_DK ** -0.5)
    gk_o[...] = seg("gk")
    gv_o[...] = seg("gv").astype(BF16)
    gr_o[...] = seg("gr")
    z = jnp.dot(seg("glr").astype(BF16), wg2_ref[...], preferred_element_type=F32) + bg2_ref[...]
    gl_o[...] = (jnp.minimum(z, 0.0) - jnp.log1p(jnp.exp(-jnp.abs(z)))) * (1.0 / GATE_NORM)


def _proj(x, shift, scale, pos0, wts):
    b, s, d = x.shape
    ts = min(s, 512)
    row = lambda a: pl.BlockSpec(a.shape, lambda bi, i: (0,) * a.ndim)
    tok = lambda w: pl.BlockSpec((None, ts, w), lambda bi, i: (bi, i, 0))
    mod = pl.BlockSpec((None, 1, d), lambda bi, i: (bi, 0, 0))
    small = [wts["g_mix"], wts["w1"], wts["g_qa"], wts["w_uq"], wts["g_kv"], wts["g_qk_q"], wts["rope"],
             wts["w_g2"], wts["b_g2"]]
    out_shape = (
        jax.ShapeDtypeStruct((b, HEADS, s, QK_PAD), BF16),
        jax.ShapeDtypeStruct((b, s, KV_LORA), F32),
        jax.ShapeDtypeStruct((b, s, ROPE), F32),
        jax.ShapeDtypeStruct((b, s, HEADS * GLA_DK), F32),
        jax.ShapeDtypeStruct((b, s, HEADS * GLA_DK), F32),
        jax.ShapeDtypeStruct((b, s, HEADS * GLA_DV), BF16),
        jax.ShapeDtypeStruct((b, s, HEADS * GLA_DK), F32),
        jax.ShapeDtypeStruct((b, s, HEADS * GLA_DV), F32),
    )
    out_specs = (
        pl.BlockSpec((None, HEADS, ts, QK_PAD), lambda bi, i: (bi, 0, i, 0)),
        tok(KV_LORA), tok(ROPE), tok(HEADS * GLA_DK), tok(HEADS * GLA_DK), tok(HEADS * GLA_DV),
        tok(HEADS * GLA_DK), tok(HEADS * GLA_DV),
    )
    return pl.pallas_call(
        functools.partial(_proj_body, pos0, ts),
        grid=(b, s // ts),
        in_specs=[tok(d), mod, mod] + [row(a) for a in small],
        out_specs=out_specs,
        out_shape=out_shape,
        scratch_shapes=[pltpu.VMEM((2, ts, LANES), F32)],
        compiler_params=_cparams(("arbitrary", "arbitrary"), VMEM_BIG),
        name="proj",
    )(x, shift, scale, *small)


def _key_rows(lat, kr, wk_ref, gk_ref, k_out):
    kn_all = jnp.dot(lat, wk_ref[...], preferred_element_type=F32)
    kr_ss = jnp.sum(kr * kr, axis=-1, keepdims=True)
    for hd in range(HEADS):
        kn = kn_all[:, NOPE * hd:NOPE * (hd + 1)]
        scl = lax.rsqrt((jnp.sum(kn * kn, axis=-1, keepdims=True) + kr_ss) * (1.0 / QK) + EPS)
        k_out[hd, :, 0:NOPE] = (kn * scl * gk_ref[0:1, :]).astype(BF16)
        k_out[hd, :, NOPE:QK] = (kr * scl * gk_ref[1:2, 0:ROPE]).astype(BF16)
        k_out[hd, :, QK:QK_PAD] = jnp.zeros((kr.shape[0], QK_PAD - QK), BF16)


def _kv_body(lat_ref, kr_ref, wk_ref, wv_ref, gk_ref, k_ref, v_ref):
    lat = lat_ref[...].astype(BF16)
    _key_rows(lat, kr_ref[...], wk_ref, gk_ref, k_ref)
    v_t = _nt(wv_ref[...], lat)
    for hd in range(HEADS):
        v_ref[hd] = v_t[V_DIM * hd:V_DIM * (hd + 1), :].astype(BF16)


def _kv(lat, kr, w_uk, w_uv_t, g_qk_k):
    b, s, _ = lat.shape
    ts = min(s, ATTN_TILE)
    return pl.pallas_call(
        _kv_body,
        grid=(b, s // ts),
        in_specs=[pl.BlockSpec((None, ts, KV_LORA), lambda bi, i: (bi, i, 0)),
                  pl.BlockSpec((None, ts, ROPE), lambda bi, i: (bi, i, 0)),
                  pl.BlockSpec(w_uk.shape, lambda bi, i: (0, 0)),
                  pl.BlockSpec(w_uv_t.shape, lambda bi, i: (0, 0)),
                  pl.BlockSpec(g_qk_k.shape, lambda bi, i: (0, 0))],
        out_specs=(pl.BlockSpec((None, HEADS, ts, QK_PAD), lambda bi, i: (bi, 0, i, 0)),
                   pl.BlockSpec((None, HEADS, None, V_DIM, ts), lambda bi, i: (bi, 0, i, 0, 0))),
        out_shape=(jax.ShapeDtypeStruct((b, HEADS, s, QK_PAD), BF16),
                   jax.ShapeDtypeStruct((b, HEADS, s // ts, V_DIM, ts), BF16)),
        compiler_params=_cparams(("arbitrary", "arbitrary")),
        name="kv",
    )(lat, kr, w_uk, w_uv_t, g_qk_k)


def _attn_prompt_body(t, q_ref, qn_ref, k_ref, vt_ref, o_ref, s_a, s_b):
    i = pl.program_id(2)

    def scores(q, j, buf):
        buf[...] = _nt(k_ref[pl.ds(pl.multiple_of(j * t, t), t), :], q)

    def consume(j, buf, carry, masked=False):
        m, l, acc = carry
        s = buf[...]
        if masked:
            visible = (lax.broadcasted_iota(I32, (t, t), 0) // CHUNK) <= (lax.broadcasted_iota(I32, (t, t), 1) // CHUNK)
            s = jnp.where(visible, s, NEG)
        m_new = jnp.maximum(m, jnp.max(s, axis=0, keepdims=True))
        alpha = jnp.exp2(m - m_new)
        p = jnp.exp2(s - m_new)
        l = alpha * l + jnp.sum(p, axis=0, keepdims=True)
        acc = alpha * acc + jnp.dot(vt_ref[j], p.astype(BF16), preferred_element_type=F32)
        return m_new, l, acc

    def run(first, second):
        q = q_ref[...]

        @pl.when(i == 0)
        def _():
            scores(q, 0, first)

        def pair(pp, carry):
            j = 2 * pp
            scores(q, j + 1, second)
            carry = consume(j, first, carry)
            scores(q, j + 2, first)
            return consume(j + 1, second, carry)

        def even_tail(carry):
            scores(qn_ref[...], 0, second)
            return consume(i, first, carry, masked=True)

        def odd_tail(carry):
            scores(q, i, second)
            carry = consume(i - 1, first, carry)
            scores(qn_ref[...], 0, first)
            return consume(i, second, carry, masked=True)

        carry = (jnp.full((1, t), NEG, F32), jnp.zeros((1, t), F32), jnp.zeros((V_DIM, t), F32))
        carry = lax.fori_loop(0, i // 2, pair, carry)
        _, l, acc = lax.cond(i % 2 == 1, odd_tail, even_tail, carry)
        o_ref[...] = (acc / l).T.astype(BF16)

    @pl.when(((i + 1) // 2) % 2 == 0)
    def _():
        run(s_a, s_b)

    @pl.when(((i + 1) // 2) % 2 == 1)
    def _():
        run(s_b, s_a)


def _attn_prompt(q, k, v_t):
    b, _, s, _ = q.shape
    t = v_t.shape[-1]
    nq = s // t
    return pl.pallas_call(
        functools.partial(_attn_prompt_body, t),
        grid=(b, HEADS, nq),
        in_specs=[pl.BlockSpec((None, None, t, QK_PAD), lambda bi, h, i: (bi, h, i, 0)),
                  pl.BlockSpec((None, None, t, QK_PAD), lambda bi, h, i: (bi, h, jnp.minimum(i + 1, nq - 1), 0)),
                  pl.BlockSpec((None, None, s, QK_PAD), lambda bi, h, i: (bi, h, 0, 0)),
                  pl.BlockSpec((None, None, nq, V_DIM, t), lambda bi, h, i: (bi, h, 0, 0, 0))],
        out_specs=pl.BlockSpec((None, t, V_DIM), lambda bi, h, i: (bi, i, h)),
        out_shape=jax.ShapeDtypeStruct((b, s, HEADS * V_DIM), BF16),
        scratch_shapes=[pltpu.VMEM((t, t), F32), pltpu.VMEM((t, t), F32)],
        compiler_params=_cparams(("arbitrary", "arbitrary", "arbitrary"), VMEM_BIG),
        name="attn_prompt",
    )(q, q, k, v_t)


def _attn_sample_body(past, sq, q_ref, plat_ref, pkr_ref, nlat_ref, nkr_ref, wk_ref, wv_ref, gk_ref, o_ref,
                      kp_scr, kn_scr):
    plat = plat_ref[...].astype(BF16)
    nlat = nlat_ref[...].astype(BF16)
    _key_rows(plat, pkr_ref[...], wk_ref, gk_ref, kp_scr)
    _key_rows(nlat, nkr_ref[...], wk_ref, gk_ref, kn_scr)
    vp_t = _nt(wv_ref[...], plat).astype(BF16)
    vn_t = _nt(wv_ref[...], nlat).astype(BF16)
    key_chunk = (past + lax.broadcasted_iota(I32, (sq, sq), 0)) // CHUNK
    qry_chunk = (past + lax.broadcasted_iota(I32, (sq, sq), 1)) // CHUNK
    for hd in range(HEADS):
        q = q_ref[hd]
        s_p = _nt(kp_scr[hd], q)
        s_n = jnp.where(key_chunk <= qry_chunk, _nt(kn_scr[hd], q), NEG)
        m = jnp.maximum(jnp.max(s_p, axis=0, keepdims=True), jnp.max(s_n, axis=0, keepdims=True))
        p_p = jnp.exp2(s_p - m)
        p_n = jnp.exp2(s_n - m)
        l = jnp.sum(p_p, axis=0, keepdims=True) + jnp.sum(p_n, axis=0, keepdims=True)
        rows = slice(V_DIM * hd, V_DIM * (hd + 1))
        o_t = (jnp.dot(vp_t[rows, :], p_p.astype(BF16), preferred_element_type=F32)
               + jnp.dot(vn_t[rows, :], p_n.astype(BF16), preferred_element_type=F32))
        o_ref[:, rows] = (o_t / l).T.astype(BF16)


def _attn_sample(q, past_lat, past_kr, lat, kr, w_uk, w_uv_t, g_qk_k):
    b, _, sq, _ = q.shape
    past = past_lat.shape[1]
    rows = lambda n, w: pl.BlockSpec((None, n, w), lambda bi: (bi, 0, 0))
    full = lambda a: pl.BlockSpec(a.shape, lambda bi: (0,) * a.ndim)
    return pl.pallas_call(
        functools.partial(_attn_sample_body, past, sq),
        grid=(b,),
        in_specs=[pl.BlockSpec((None, HEADS, sq, QK_PAD), lambda bi: (bi, 0, 0, 0)),
                  rows(past, KV_LORA), rows(past, ROPE), rows(sq, KV_LORA), rows(sq, ROPE),
                  full(w_uk), full(w_uv_t), full(g_qk_k)],
        out_specs=rows(sq, HEADS * V_DIM),
        out_shape=jax.ShapeDtypeStruct((b, sq, HEADS * V_DIM), BF16),
        scratch_shapes=[pltpu.VMEM((HEADS, past, QK_PAD), BF16), pltpu.VMEM((HEADS, sq, QK_PAD), BF16)],
        compiler_params=_cparams(("arbitrary",), 40 * 1024 * 1024),
        name="attn_sample",
    )(q, past_lat, past_kr, lat, kr, w_uk, w_uv_t, g_qk_k)


def _gla_masks(c, rows):
    idx = np.arange(rows)
    same = (idx // c)[:, None] == (idx // c)[None, :]
    le = same & (idx[None, :] <= idx[:, None])
    gt = same & (idx[None, :] > idx[:, None])
    return np.concatenate([le, gt], axis=0).astype(np.float32), int(np.log2(c))


def _level_exponents(b, g, c, level):
    n = c >> level
    rows = b.shape[0]
    row = lax.broadcasted_iota(I32, (rows, 1), 0)
    if n >= 8:
        split = b.reshape(rows // n, n, LANES)[:, n // 2 - 1:n // 2, :]
        split = jnp.broadcast_to(split, (rows // n, n, LANES)).reshape(rows, LANES)
        return jnp.where((row & (n // 2)) != 0, b - split, split - b)
    g_prev = pltpu.roll(g, 1, 0)
    g_next = pltpu.roll(g, rows - 1, 0)
    if n == 4:
        r = row & 3
        return jnp.where(r == 0, g_next, jnp.where(r == 1, 0.0, jnp.where(r == 2, g, g + g_prev)))
    assert n == 2
    return jnp.where((row & 1) != 0, g, 0.0)


def _gla_body(c, n_chunks, unit, levels, mall_ref, q_ref, k_ref, g_ref, v_ref, r_ref, s0_ref, gout_ref,
              o_ref, sfin_ref, st_scr):
    it = pl.program_id(1)

    @pl.when(it == 0)
    def _():
        st_scr[...] = s0_ref[...]

    ru = unit * c
    lane = lax.broadcasted_iota(I32, (ru, LANES), 1)
    head_lanes = (lane < GLA_DK, lane >= GLA_DK)
    st_lane_lo = lax.broadcasted_iota(I32, (GLA_DV, LANES), 1) < GLA_DK
    row = lax.broadcasted_iota(I32, (ru, 1), 0)
    ri = lax.broadcasted_iota(I32, (ru, ru), 0)
    ci = lax.broadcasted_iota(I32, (ru, ru), 1)
    mall = mall_ref[...]

    for un in range(n_chunks // unit):
        rows = slice(un * ru, (un + 1) * ru)
        for p in range(HEADS // 2):
            ls = slice(LANES * p, LANES * (p + 1))
            g = g_ref[rows, ls]
            q = q_ref[rows, ls]
            k = k_ref[rows, ls]
            g_hi = g.astype(BF16)
            g_lo = (g - g_hi.astype(F32)).astype(BF16)
            e2 = jnp.dot(mall, jnp.concatenate([g_hi, g_lo], axis=1), preferred_element_type=F32)
            e = e2[:, 0:LANES] + e2[:, LANES:2 * LANES]
            b = e[0:ru]
            eb = jnp.exp(b)
            qb = q * eb
            kd = (k * jnp.exp(e[ru:2 * ru])).astype(BF16)
            qs, ks = [q], [k.astype(BF16)]
            for l in range(levels):
                bottom = (row & (c >> (l + 1))) != 0
                decay = jnp.exp(_level_exponents(b, g, c, l))
                qs.append(jnp.where(bottom, q * decay, 0.0))
                ks.append(jnp.where(bottom, 0.0, k * decay).astype(BF16))
            states = [st_scr[p]]
            for j in range(unit):
                cr = slice(j * c, (j + 1) * c)
                upd = [lax.dot_general(v_ref[rows, GLA_DV * (2 * p + hh):GLA_DV * (2 * p + hh + 1)][cr, :], kd[cr, :],
                                       (((0,), (0,)), ((), ())), preferred_element_type=F32) for hh in range(2)]
                d_last = eb[j * c + c - 1:j * c + c, :]
                states.append(states[-1] * d_last + jnp.where(st_lane_lo, upd[0], upd[1]))
            st_scr[p] = states[-1]
            for hh in range(2):
                hd = 2 * p + hh
                sel = head_lanes[hh]
                a = jnp.where(ri == ci, _nt(jnp.where(sel, qs[0], 0.0).astype(BF16), ks[0]), 0.0)
                for l in range(levels):
                    pr = _nt(jnp.where(sel, qs[l + 1], 0.0).astype(BF16), ks[l + 1])
                    a = a + jnp.where((ri ^ ci) < (c >> l), pr, 0.0)
                vh = v_ref[rows, GLA_DV * hd:GLA_DV * (hd + 1)]
                qb_h = jnp.where(sel, qb, 0.0).astype(BF16)
                o_state = [_nt(qb_h[j * c:(j + 1) * c, :], states[j].astype(BF16)) for j in range(unit)]
                o = jnp.dot(a.astype(BF16), vh, preferred_element_type=F32) + jnp.concatenate(o_state, axis=0)
                on = o * _rms(o, GLA_DV) * gout_ref[...]
                r = r_ref[rows, GLA_DV * hd:GLA_DV * (hd + 1)]
                o_ref[rows, GLA_DV * hd:GLA_DV * (hd + 1)] = (on * (r * jax.nn.sigmoid(r))).astype(BF16)

    @pl.when(it == pl.num_programs(1) - 1)
    def _():
        sfin_ref[...] = st_scr[...]


def _gla(gq, gk, gl, gv, gr, s0, g_out):
    b, s, _ = gq.shape
    c = min(CHUNK, s)
    tile = min(s, 8 * c)
    unit = next(u for u in (4, 2, 1) if (tile // c) % u == 0)
    masks, levels = _gla_masks(c, unit * c)
    mall = jnp.asarray(masks, BF16)
    tok = lambda w: pl.BlockSpec((None, tile, w), lambda bi, i: (bi, i, 0))
    st_spec = pl.BlockSpec((None, HEADS // 2, GLA_DV, LANES), lambda bi, i: (bi, 0, 0, 0))
    return pl.pallas_call(
        functools.partial(_gla_body, c, tile // c, unit, levels),
        grid=(b, s // tile),
        in_specs=[pl.BlockSpec(mall.shape, lambda bi, i: (0, 0)),
                  tok(HEADS * GLA_DK), tok(HEADS * GLA_DK), tok(HEADS * GLA_DK), tok(HEADS * GLA_DV),
                  tok(HEADS * GLA_DV), st_spec, pl.BlockSpec(g_out.shape, lambda bi, i: (0, 0))],
        out_specs=(tok(HEADS * GLA_DV), st_spec),
        out_shape=(jax.ShapeDtypeStruct((b, s, HEADS * GLA_DV), BF16),
                   jax.ShapeDtypeStruct((b, HEADS // 2, GLA_DV, LANES), F32)),
        scratch_shapes=[pltpu.VMEM((HEADS // 2, GLA_DV, LANES), F32)],
        compiler_params=_cparams(("arbitrary", "arbitrary")),
        name="gla",
    )(mall, gq, gk, gl, gv, gr, s0, g_out)


def _state_to_pairs(s):
    b = s.shape[0]
    s = s.reshape(b, HEADS // 2, 2, GLA_DK, GLA_DV)
    return jnp.transpose(s, (0, 1, 4, 2, 3)).reshape(b, HEADS // 2, GLA_DV, 2 * GLA_DK)


def _state_from_pairs(s):
    b = s.shape[0]
    s = s.reshape(b, HEADS // 2, GLA_DV, 2, GLA_DK)
    return jnp.transpose(s, (0, 1, 3, 4, 2)).reshape(b, HEADS, GLA_DK, GLA_DV)


def _post_body(x_ref, om_ref, og_ref, gt_ref, sc_ref, sh_ref, wo_ref, gffn_ref, wr_ref, br_ref,
               x2_ref, h_ref, idx_ref, wt_ref, rank_ref, cnt_ref):
    half = om_ref.shape[-1]
    mix = (jnp.dot(om_ref[...], wo_ref[0:half, :], preferred_element_type=F32)
           + jnp.dot(og_ref[...], wo_ref[half:2 * half, :], preferred_element_type=F32))
    x2 = x_ref[...] + gt_ref[...] * mix
    x2_ref[...] = x2
    d = x2.shape[-1]
    h = (x2 * _rms(x2, d) * gffn_ref[...]) * (1.0 + sc_ref[...]) + sh_ref[...]
    h_hi = h.astype(BF16)
    h_ref[...] = h_hi
    h_lo = (h - h_hi.astype(F32)).astype(BF16)
    logits = _nt(wr_ref[0], h_hi) + _nt(wr_ref[0], h_lo) + _nt(wr_ref[1], h_hi) + br_ref[...]
    n_exp, tm = logits.shape
    eid = lax.broadcasted_iota(I32, (n_exp, tm), 0)
    vals, tops, ids = logits, [], []
    for _ in range(TOP_K):
        m = jnp.max(vals, axis=0, keepdims=True)
        sel = jnp.min(jnp.where(vals == m, eid, n_exp), axis=0, keepdims=True)
        tops.append(m)
        ids.append(sel)
        vals = jnp.where(eid == sel, -jnp.inf, vals)
    es = [jnp.exp(t - tops[0]) for t in tops]
    tot = es[0] + es[1] + es[2] + es[3]
    idx_ref[...] = jnp.concatenate(ids, axis=0)
    wt_ref[...] = jnp.concatenate([e / tot for e in es], axis=0)
    hits = [eid == sel for sel in ids]
    member = jnp.zeros((n_exp, tm), F32)
    for hk in hits:
        member = member + jnp.where(hk, 1.0, 0.0)
    before = lax.broadcasted_iota(I32, (tm, tm), 0) < lax.broadcasted_iota(I32, (tm, tm), 1)
    prefix = jnp.dot(member.astype(BF16), jnp.where(before, 1.0, 0.0).astype(BF16), preferred_element_type=F32)
    rank_ref[...] = jnp.concatenate(
        [jnp.sum(jnp.where(hk, prefix, 0.0), axis=0, keepdims=True) for hk in hits], axis=0).astype(I32)
    cnt_ref[...] = jnp.broadcast_to(jnp.sum(member, axis=1, keepdims=True), (n_exp, LANES)).astype(I32)


def _post(x, om, og, gate, scale, shift, w_o, g_ffn, w_r2, b_r):
    t, d = x.shape
    tm = ROUTE_TILE
    per_tok = gate.shape[0] == t
    mod = pl.BlockSpec((tm, d), lambda i: (i, 0)) if per_tok else pl.BlockSpec((1, d), lambda i: (0, 0))
    tok = lambda w: pl.BlockSpec((tm, w), lambda i: (i, 0))
    full = lambda a: pl.BlockSpec(a.shape, lambda i: (0,) * a.ndim)
    return pl.pallas_call(
        _post_body,
        grid=(t // tm,),
        in_specs=[tok(d), tok(om.shape[1]), tok(og.shape[1]), mod, mod, mod, full(w_o), full(g_ffn), full(w_r2),
                  full(b_r)],
        out_specs=(tok(d), tok(d),
                   pl.BlockSpec((TOP_K, tm), lambda i: (0, i)), pl.BlockSpec((TOP_K, tm), lambda i: (0, i)),
                   pl.BlockSpec((TOP_K, tm), lambda i: (0, i)),
                   pl.BlockSpec((None, N_EXPERTS, LANES), lambda i: (i, 0, 0))),
        out_shape=(jax.ShapeDtypeStruct((t, d), F32), jax.ShapeDtypeStruct((t, d), BF16),
                   jax.ShapeDtypeStruct((TOP_K, t), I32), jax.ShapeDtypeStruct((TOP_K, t), F32),
                   jax.ShapeDtypeStruct((TOP_K, t), I32), jax.ShapeDtypeStruct((t // tm, N_EXPERTS, LANES), I32)),
        compiler_params=_cparams(("arbitrary",), 40 * 1024 * 1024),
        name="post",
    )(x, om, og, gate, scale, shift, w_o, g_ffn, w_r2, b_r)


def _route_tables(idx, lrank, cnt3):
    nt = cnt3.shape[0]
    t = idx.shape[1]
    cnt = cnt3[:, :, 0]
    run = _round_up(cnt, RUN_CHUNK)
    lo_end = jnp.cumsum(run, axis=1)
    lo = lo_end - run
    n_chunks = lo_end[:, -1] // RUN_CHUNK
    region = _round_up(jnp.sum(run, axis=0), EXPERT_ROWS)
    g_end = jnp.cumsum(region)
    run_dest = (g_end - region)[None, :] + jnp.cumsum(run, axis=0) - run
    max_rows = TOP_K * ROUTE_TILE + N_EXPERTS * (RUN_CHUNK - 1)
    n_tab = _round_up(max_rows, RUN_CHUNK) // RUN_CHUNK
    c_start = jnp.arange(n_tab, dtype=I32) * RUN_CHUNK
    e_of_c = jnp.minimum(jnp.sum(lo_end[:, None, :] <= c_start[None, :, None], axis=2), N_EXPERTS - 1)
    pick = e_of_c[:, :, None] == jnp.arange(N_EXPERTS, dtype=I32)[None, None, :]
    chunk_dest = jnp.sum(jnp.where(pick, (run_dest - lo)[:, None, :], 0), axis=2) + c_start[None, :]
    table = jnp.concatenate([chunk_dest, n_chunks[:, None]], axis=1).astype(I32).reshape(nt, 1, n_tab + 1)
    eid = jnp.arange(N_EXPERTS, dtype=I32)[:, None]
    lo_tok = jnp.repeat(lo.T, ROUTE_TILE, axis=1)
    lpos = jnp.stack([jnp.sum(jnp.where(idx[k][None, :] == eid, lo_tok, 0), axis=0) for k in range(TOP_K)])
    lpos = (lpos + lrank).astype(I32)
    n_blocks = _round_up(t * TOP_K + nt * N_EXPERTS * (RUN_CHUNK - 1), EXPERT_ROWS) // EXPERT_ROWS + N_EXPERTS
    b_start = jnp.arange(n_blocks, dtype=I32) * EXPERT_ROWS
    blk_e = jnp.minimum(jnp.sum(g_end[None, :] <= b_start[:, None], axis=1), N_EXPERTS - 1).astype(I32)
    n_valid = (g_end[-1:] // EXPERT_ROWS).astype(I32)
    used = region > 0
    e_ids = jnp.arange(N_EXPERTS, dtype=I32)
    later_used = used[None, :] & (e_ids[None, :] > e_ids[:, None])
    nxt_e = jnp.min(jnp.where(later_used, e_ids[None, :], N_EXPERTS), axis=1)
    nxt_e = jnp.where(nxt_e < N_EXPERTS, nxt_e, -1).astype(I32)
    ord_e = (jnp.cumsum(used.astype(I32)) - 1).astype(I32)
    tail = jnp.concatenate([jnp.where(used, g_end - EXPERT_ROWS, -1), n_valid]).astype(I32)
    tail = tail.reshape(1, N_EXPERTS + 1)
    return dict(table=table, lpos=lpos, blk_e=blk_e, n_valid=n_valid, nxt_e=nxt_e, ord_e=ord_e, tail=tail,
                n_blocks=n_blocks)


def _chunk_copy(src, dst, sem):
    return pltpu.make_async_copy(src, dst, sem)


def _for_row_blocks(n_rows, body):
    full = n_rows // SORT_ROWS
    rem = n_rows - full * SORT_ROWS
    tail = pl.multiple_of(full * SORT_ROWS, SORT_ROWS)

    def whole(rb, carry):
        body(pl.multiple_of(rb * SORT_ROWS, SORT_ROWS), SORT_ROWS)
        return carry

    lax.fori_loop(0, full, whole, 0)

    @pl.when(rem > SORT_ROWS // 2)
    def _():
        body(tail, SORT_ROWS)

    @pl.when((rem > 0) & (rem <= SORT_ROWS // 2))
    def _():
        body(tail, SORT_ROWS // 2)


def _for_chunks(n, body):
    groups = n // CHUNK_UNROLL

    def group(g, carry):
        for u in range(CHUNK_UNROLL):
            body(g * CHUNK_UNROLL + u)
        return carry

    def single(c, carry):
        body(c)
        return carry

    lax.fori_loop(0, groups, group, 0)
    lax.fori_loop(groups * CHUNK_UNROLL, n, single, 0)


def _scatter_body(n_tab, nt_a, n_blocks, tab_ref, prv_ref, tail_ref, lpos_ref, ha_ref, hb_ref, xout_ref, sorted_scr,
                  zero_scr, sems, zero_sem):
    i = pl.program_id(0)
    slot = i % 2
    n_chunks = tab_ref[0, n_tab]
    tt = ha_ref.shape[0]

    @pl.when(i == 0)
    def _():
        zero_scr[...] = jnp.zeros_like(zero_scr)
        n_valid = tail_ref[0, N_EXPERTS]

        def block(start):
            return xout_ref.at[pl.ds(pl.multiple_of(start, EXPERT_ROWS), EXPERT_ROWS)]

        for e in range(N_EXPERTS):
            @pl.when(tail_ref[0, e] >= 0)
            def _():
                _chunk_copy(zero_scr, block(tail_ref[0, e]), zero_sem).start()

        def fill(b, carry):
            _chunk_copy(zero_scr, block(b * EXPERT_ROWS), zero_sem).start()
            return carry

        def fill_done(b, carry):
            _chunk_copy(zero_scr, block(0), zero_sem).wait()
            return carry

        lax.fori_loop(n_valid, n_blocks, fill, 0)
        for e in range(N_EXPERTS):
            @pl.when(tail_ref[0, e] >= 0)
            def _():
                _chunk_copy(zero_scr, block(0), zero_sem).wait()
        lax.fori_loop(n_valid, n_blocks, fill_done, 0)

    lp16 = lpos_ref[...].astype(jnp.int16)
    h = jnp.where(i < nt_a, ha_ref[...], hb_ref[...])

    def sort_block(r0, size):
        rid = (r0 + lax.broadcasted_iota(I32, (size, tt), 0)).astype(jnp.int16)
        onehot = jnp.zeros((size, tt), BF16)
        for k in range(TOP_K):
            onehot = onehot + jnp.where(lp16[k:k + 1, :] == rid, jnp.ones((), BF16), jnp.zeros((), BF16))
        sorted_scr[slot, pl.ds(r0, size), :] = jnp.dot(onehot, h, preferred_element_type=F32).astype(BF16)

    _for_row_blocks(n_chunks * RUN_CHUNK, sort_block)

    def issue(c):
        src = sorted_scr.at[slot, pl.ds(pl.multiple_of(c * RUN_CHUNK, RUN_CHUNK), RUN_CHUNK)]
        dst = xout_ref.at[pl.ds(pl.multiple_of(tab_ref[0, c], RUN_CHUNK), RUN_CHUNK)]
        _chunk_copy(src, dst, sems.at[slot]).start()

    def drain(sl):
        def body(c):
            _chunk_copy(sorted_scr.at[sl, pl.ds(0, RUN_CHUNK)], xout_ref.at[pl.ds(0, RUN_CHUNK)], sems.at[sl]).wait()
        return body

    _for_chunks(n_chunks, issue)

    @pl.when(i > 0)
    def _():
        _for_chunks(prv_ref[0, n_tab], drain(1 - slot))

    @pl.when(i == pl.num_programs(0) - 1)
    def _():
        _for_chunks(n_chunks, drain(slot))


def _scatter(table, tail, lpos, h_a, h_b, n_blocks):
    d = h_a.shape[1]
    nt_a, nt_b = h_a.shape[0] // ROUTE_TILE, h_b.shape[0] // ROUTE_TILE
    n_tab = table.shape[2] - 1
    sort_cap = _round_up(n_tab * RUN_CHUNK, SORT_ROWS)
    tab_spec = lambda f: pl.BlockSpec((None, 1, n_tab + 1), lambda i: (f(i), 0, 0), memory_space=pltpu.SMEM)
    return pl.pallas_call(
        functools.partial(_scatter_body, n_tab, nt_a, n_blocks),
        grid=(nt_a + nt_b,),
        in_specs=[tab_spec(lambda i: i), tab_spec(lambda i: jnp.maximum(i - 1, 0)),
                  pl.BlockSpec((1, N_EXPERTS + 1), lambda i: (0, 0), memory_space=pltpu.SMEM),
                  pl.BlockSpec((TOP_K, ROUTE_TILE), lambda i: (0, i)),
                  pl.BlockSpec((ROUTE_TILE, d), lambda i: (jnp.minimum(i, nt_a - 1), 0)),
                  pl.BlockSpec((ROUTE_TILE, d), lambda i: (jnp.maximum(i - nt_a, 0), 0))],
        out_specs=pl.BlockSpec(memory_space=pl.ANY),
        out_shape=jax.ShapeDtypeStruct((n_blocks * EXPERT_ROWS, d), BF16),
        scratch_shapes=[pltpu.VMEM((2, sort_cap, d), BF16), pltpu.VMEM((EXPERT_ROWS, d), BF16),
                        pltpu.SemaphoreType.DMA((2,)), pltpu.SemaphoreType.DMA(())],
        compiler_params=_cparams(("arbitrary",), 48 * 1024 * 1024),
        name="scatter",
    )(table, table, tail, lpos, h_a, h_b)


def _experts_body(be_ref, nv_ref, nxt_ref, ord_ref, x_ref, wgu_hbm, bgu_ref, wd_hbm, bd_ref, y_ref,
                  wgu_f, wd_f, wgu_s, wd_s, sem_gu, sem_d):
    b = pl.program_id(0)
    e = be_ref[b]
    prev = be_ref[jnp.maximum(b - 1, 0)]
    valid = b < nv_ref[0]
    d_ff = wd_s.shape[0]
    slot = ord_ref[e] % 2

    def weights(expert, sl):
        return (pltpu.make_async_copy(wgu_hbm.at[expert], wgu_f.at[sl], sem_gu.at[sl]),
                pltpu.make_async_copy(wd_hbm.at[expert], wd_f.at[sl], sem_d.at[sl]))

    @pl.when(valid & ((b == 0) | (e != prev)))
    def _():
        @pl.when(b == 0)
        def _():
            for cp in weights(e, slot):
                cp.start()

        for cp in weights(e, slot):
            cp.wait()

        @pl.when(nxt_ref[e] >= 0)
        def _():
            for cp in weights(nxt_ref[e], 1 - slot):
                cp.start()

        wgu_s[...] = wgu_f[slot].astype(BF16)
        wd_s[...] = wd_f[slot].astype(BF16)

    @pl.when(valid)
    def _():
        gu = jnp.dot(x_ref[...], wgu_s[...], preferred_element_type=F32) + bgu_ref[...]
        gate = jnp.minimum(gu[:, 0:d_ff], SWIGLU_LIMIT)
        up = jnp.clip(gu[:, d_ff:2 * d_ff], -SWIGLU_LIMIT, SWIGLU_LIMIT)
        act = ((up + 1.0) * (gate * jax.nn.sigmoid(gate * SWIGLU_ALPHA))).astype(BF16)
        y_ref[...] = (jnp.dot(act, wd_s[...], preferred_element_type=F32) + bd_ref[...]).astype(BF16)

    @pl.when(jnp.logical_not(valid))
    def _():
        y_ref[...] = jnp.zeros_like(y_ref)


def _experts(blk_e, n_valid, nxt_e, ord_e, xpad, w_gu, b_gu, w_down, b_down):
    m, d = xpad.shape
    nb = m // EXPERT_ROWS
    n_exp, _, f2 = w_gu.shape
    d_ff = w_down.shape[1]
    last = lambda b, be, nv: jnp.minimum(b, nv[0] - 1)
    grid_spec = pltpu.PrefetchScalarGridSpec(
        num_scalar_prefetch=4,
        grid=(nb,),
        in_specs=[pl.BlockSpec((EXPERT_ROWS, d), lambda b, be, nv, nx, od: (last(b, be, nv), 0)),
                  pl.BlockSpec(memory_space=pl.ANY),
                  pl.BlockSpec((None, 1, f2), lambda b, be, nv, nx, od: (be[last(b, be, nv)], 0, 0)),
                  pl.BlockSpec(memory_space=pl.ANY),
                  pl.BlockSpec((None, 1, d), lambda b, be, nv, nx, od: (be[last(b, be, nv)], 0, 0))],
        out_specs=pl.BlockSpec((EXPERT_ROWS, d), lambda b, be, nv, nx, od: (b, 0)),
        scratch_shapes=[pltpu.VMEM((2, d, f2), F32), pltpu.VMEM((2, d_ff, d), F32),
                        pltpu.VMEM((d, f2), BF16), pltpu.VMEM((d_ff, d), BF16),
                        pltpu.SemaphoreType.DMA((2,)), pltpu.SemaphoreType.DMA((2,))],
    )
    return pl.pallas_call(
        _experts_body,
        grid_spec=grid_spec,
        out_shape=jax.ShapeDtypeStruct((m, d), BF16),
        compiler_params=_cparams(("arbitrary",), VMEM_BIG),
        name="experts",
    )(blk_e, n_valid, nxt_e, ord_e, xpad, w_gu, b_gu.reshape(n_exp, 1, f2), w_down, b_down.reshape(n_exp, 1, d))


def _combine_body(n_tab, tab_ref, nxt_ref, lpos_ref, wt_ref, x2_ref, gt_ref, y_ref, o_ref, ysort_scr, sems):
    i = pl.program_id(0)
    slot = i % 2
    n_chunks = tab_ref[0, n_tab]
    tt = x2_ref.shape[0]

    def fetch(tab, sl):
        def issue(c):
            src = y_ref.at[pl.ds(pl.multiple_of(tab[0, c], RUN_CHUNK), RUN_CHUNK)]
            dst = ysort_scr.at[sl, pl.ds(pl.multiple_of(c * RUN_CHUNK, RUN_CHUNK), RUN_CHUNK)]
            _chunk_copy(src, dst, sems.at[sl]).start()

        _for_chunks(tab[0, n_tab], issue)

    @pl.when(i == 0)
    def _():
        ysort_scr[...] = jnp.zeros_like(ysort_scr)
        fetch(tab_ref, 0)

    @pl.when(i + 1 < pl.num_programs(0))
    def _():
        fetch(nxt_ref, 1 - slot)

    def drain(c):
        _chunk_copy(y_ref.at[pl.ds(0, RUN_CHUNK)], ysort_scr.at[slot, pl.ds(0, RUN_CHUNK)], sems.at[slot]).wait()

    _for_chunks(n_chunks, drain)

    lp16 = lpos_ref[...].astype(jnp.int16)
    w = wt_ref[...].astype(BF16)
    o_ref[...] = x2_ref[...]
    gt = gt_ref[...]

    def gather_block(r0, size):
        rid = (r0 + lax.broadcasted_iota(I32, (size, tt), 0)).astype(jnp.int16)
        pw_t = jnp.zeros((size, tt), BF16)
        for k in range(TOP_K):
            pw_t = pw_t + jnp.where(lp16[k:k + 1, :] == rid, w[k:k + 1, :], jnp.zeros((), BF16))
        ys = ysort_scr[slot, pl.ds(r0, size), :]
        part = lax.dot_general(pw_t, ys, (((0,), (0,)), ((), ())), preferred_element_type=F32)
        o_ref[...] = o_ref[...] + gt * part

    _for_row_blocks(n_chunks * RUN_CHUNK, gather_block)


def _combine(table, lpos, wts, x2, gate, ypad):
    t, d = x2.shape
    nt = t // ROUTE_TILE
    n_tab = table.shape[2] - 1
    per_tok = gate.shape[0] == t
    mod = pl.BlockSpec((ROUTE_TILE, d), lambda i: (i, 0)) if per_tok else pl.BlockSpec((1, d), lambda i: (0, 0))
    sort_cap = _round_up(n_tab * RUN_CHUNK, SORT_ROWS)
    tab_spec = lambda f: pl.BlockSpec((None, 1, n_tab + 1), lambda i: (f(i), 0, 0), memory_space=pltpu.SMEM)
    return pl.pallas_call(
        functools.partial(_combine_body, n_tab),
        grid=(nt,),
        in_specs=[tab_spec(lambda i: i), tab_spec(lambda i: jnp.minimum(i + 1, nt - 1)),
                  pl.BlockSpec((TOP_K, ROUTE_TILE), lambda i: (0, i)),
                  pl.BlockSpec((TOP_K, ROUTE_TILE), lambda i: (0, i)),
                  pl.BlockSpec((ROUTE_TILE, d), lambda i: (i, 0)),
                  mod,
                  pl.BlockSpec(memory_space=pl.ANY)],
        out_specs=pl.BlockSpec((ROUTE_TILE, d), lambda i: (i, 0)),
        out_shape=jax.ShapeDtypeStruct((t, d), F32),
        scratch_shapes=[pltpu.VMEM((2, sort_cap, d), BF16), pltpu.SemaphoreType.DMA((2,))],
        compiler_params=_cparams(("arbitrary",), 48 * 1024 * 1024),
        name="combine",
    )(table, table, lpos, wts, x2, gate, ypad)


def _prep_weights(g_norm_mix, w_in, g_q_a, w_uq, g_kv_a, w_ukv, g_qk_q, g_qk_k, w_g2, b_g2, g_gla_out, w_o,
                  g_norm_ffn, w_router, b_router):
    d = w_in.shape[0]
    o_qa, o_kva, o_kr = 0, Q_LORA, Q_LORA + KV_LORA
    o_gq = o_kr + ROPE
    o_gk = o_gq + HEADS * GLA_DK
    o_gv = o_gk + HEADS * GLA_DK
    o_glr = o_gv + HEADS * GLA_DV
    o_gr = o_glr + GATE_RANK
    kr_cols = w_in[:, o_kr:o_kr + ROPE]
    w1 = jnp.concatenate([
        w_in[:, o_qa:o_kva], kr_cols, kr_cols, w_in[:, o_kva:o_kr], w_in[:, o_gq:o_gk], w_in[:, o_gk:o_gv],
        w_in[:, o_gv:o_glr], w_in[:, o_gr:o_gr + HEADS * GLA_DV], w_in[:, o_glr:o_gr],
        jnp.zeros((d, LANES - GATE_RANK), w_in.dtype)], axis=1).astype(BF16)
    assert w1.shape[1] == _W1_COLS
    wq = w_uq.reshape(Q_LORA, HEADS, QK)
    wuq = jnp.concatenate([wq[:, :, 0:NOPE].reshape(Q_LORA, HEADS * NOPE),
                           wq[:, :, NOPE:QK].reshape(Q_LORA, HEADS * ROPE)], axis=1).astype(BF16)
    wkv = w_ukv.reshape(KV_LORA, HEADS, NOPE + V_DIM)
    wuk = wkv[:, :, 0:NOPE].reshape(KV_LORA, HEADS * NOPE).astype(BF16)
    wuv_t = wkv[:, :, NOPE:].reshape(KV_LORA, HEADS * V_DIM).T.astype(BF16)
    pad_rope = lambda g: jnp.stack([g[0:NOPE], jnp.concatenate([g[NOPE:QK], jnp.zeros((QK_PAD - QK,), g.dtype)])])
    inv = ROPE_THETA ** (-jnp.arange(HALF, dtype=F32) / HALF)
    sign = jnp.concatenate([-jnp.ones((HALF,), F32), jnp.ones((HALF,), F32)])
    rope_tab = jnp.stack([jnp.tile(inv, LANES // HALF), jnp.tile(sign, LANES // ROPE)])
    wg2 = jnp.concatenate([w_g2, jnp.zeros((LANES - GATE_RANK, w_g2.shape[1]), w_g2.dtype)], axis=0).astype(BF16)
    wr_t = w_router.T
    wr_hi = wr_t.astype(BF16)
    wr_lo = (wr_t - wr_hi.astype(F32)).astype(BF16)
    return dict(
        g_mix=g_norm_mix.reshape(1, d), w1=w1, g_qa=g_q_a.reshape(1, -1), w_uq=wuq, g_kv=g_kv_a.reshape(1, -1),
        w_uk=wuk, w_uv_t=wuv_t, g_qk_q=pad_rope(g_qk_q), g_qk_k=pad_rope(g_qk_k), rope=rope_tab, w_g2=wg2,
        b_g2=b_g2.reshape(1, -1), g_out=g_gla_out.reshape(1, -1), w_o=w_o.astype(BF16),
        g_ffn=g_norm_ffn.reshape(1, d), w_r2=jnp.stack([wr_hi, wr_lo]), b_r=b_router.reshape(-1, 1))


def _mixer(x, mod, pos0, past_lat, past_kr, s0_pairs, wts):
    b, s, d = x.shape
    q, lat, kr, gq, gk, gv, gl, gr = _proj(x, mod[:, 0:1], mod[:, 1:2], pos0, wts)
    kv_w = (wts["w_uk"], wts["w_uv_t"], wts["g_qk_k"])
    if past_lat is None:
        k_new, vt_new = _kv(lat, kr, *kv_w)
        o_mla = _attn_prompt(q, k_new, vt_new)
    else:
        o_mla = _attn_sample(q, past_lat, past_kr, lat, kr, *kv_w)
    o_gla, s_fin = _gla(gq, gk, gl, gv, gr, s0_pairs, wts["g_out"])
    t = b * s
    if b == 1:
        rows = lambda j: mod[0, j:j + 1]
    else:
        rows = lambda j: jnp.broadcast_to(mod[:, j:j + 1], (b, s, d)).reshape(t, d)
    x2, h2, idx, wt, lrank, cnt = _post(x.reshape(t, d), o_mla.reshape(t, -1), o_gla.reshape(t, -1), rows(2), rows(4),
                                        rows(3), wts["w_o"], wts["g_ffn"], wts["w_r2"], wts["b_r"])
    return dict(x2=x2, h2=h2, idx=idx, wt=wt, lrank=lrank, cnt=cnt, gate_f=rows(5), lat=lat, kr=kr, s_fin=s_fin)


def kernel(x_prompt, x_sample, cache_mla_latent, cache_mla_krope, state_gla, c_prompt, c_sample, w_ada, b_ada, g_norm_mix, w_in, g_q_a, w_uq, g_kv_a, w_ukv, g_qk_q, g_qk_k, w_g2, b_g2, g_gla_out, w_o, g_norm_ffn, w_router, b_router, w_gu, b_gu, w_down, b_down):
    depth = w_ada.shape[0]
    assert depth == 1, "single-layer step"
    bp, sp, d = x_prompt.shape
    bs, ss, _ = x_sample.shape
    tp, tsm = bp * sp, bs * ss
    assert tp % ROUTE_TILE == 0 and tsm % ROUTE_TILE == 0, "token counts must be whole routing tiles"
    past = cache_mla_latent.shape[2]
    layer = lambda a: a.reshape(a.shape[1:])
    wts = _prep_weights(*[layer(a) for a in (g_norm_mix, w_in, g_q_a, w_uq, g_kv_a, w_ukv, g_qk_q, g_qk_k, w_g2, b_g2,
                                             g_gla_out, w_o, g_norm_ffn, w_router, b_router)])
    w_gu, b_gu, w_down, b_down = layer(w_gu), layer(b_gu), layer(w_down), layer(b_down)

    mod = _ada(jnp.concatenate([c_prompt, c_sample], axis=0), layer(w_ada), layer(b_ada)).reshape(bp + bs, 6, d)
    zero_state = jnp.zeros((bp, HEADS // 2, GLA_DV, LANES), F32)
    pr = _mixer(x_prompt, mod[:bp], 0, None, None, zero_state, wts)
    sa = _mixer(x_sample, mod[bp:], past, layer(cache_mla_latent), layer(cache_mla_krope),
                _state_to_pairs(layer(state_gla)), wts)

    idx = jnp.concatenate([pr["idx"], sa["idx"]], axis=1)
    lrank = jnp.concatenate([pr["lrank"], sa["lrank"]], axis=1)
    rt = _route_tables(idx, lrank, jnp.concatenate([pr["cnt"], sa["cnt"]], axis=0))
    ntp = tp // ROUTE_TILE
    lpos, table = rt["lpos"], rt["table"]
    xpad = _scatter(table, rt["tail"], lpos, pr["h2"], sa["h2"], rt["n_blocks"])
    ypad = _experts(rt["blk_e"], rt["n_valid"], rt["nxt_e"], rt["ord_e"], xpad, w_gu, b_gu, w_down, b_down)
    y_p = _combine(table[:ntp], lpos[:, :tp], pr["wt"], pr["x2"], pr["gate_f"], ypad).reshape(bp, sp, d)
    y_s = _combine(table[ntp:], lpos[:, tp:], sa["wt"], sa["x2"], sa["gate_f"], ypad).reshape(bs, ss, d)

    return (y_p, y_s,
            pr["lat"][None], pr["kr"][None], _state_from_pairs(pr["s_fin"])[None],
            sa["lat"][None], sa["kr"][None], _state_from_pairs(sa["s_fin"])[None])
```

```python
import functools

import numpy as np
import jax
import jax.numpy as jnp
from jax import lax
from jax.experimental import pallas as pl
from jax.experimental.pallas import tpu as pltpu

F32 = jnp.float32
BF16 = jnp.bfloat16
I32 = jnp.int32

CHUNK = 64
EPS = 1e-6
HEADS = 4
Q_LORA = 384
KV_LORA = 256
NOPE = 128
ROPE = 64
HALF = ROPE // 2
V_DIM = 128
QK = NOPE + ROPE
QK_PAD = 256
ROPE_THETA = 10000.0
GLA_DK = 64
GLA_DV = 128
GATE_RANK = 16
GATE_NORM = 16.0
N_EXPERTS = 32
TOP_K = 4
SWIGLU_LIMIT = 7.0
SWIGLU_ALPHA = 1.702
NEG = -1e30
LOG2_E = 1.4426950408889634

LANES = 128
BF16_ROWS = 16
ROUTE_TILE = 512
RUN_CHUNK = BF16_ROWS
SORT_ROWS = 512
CHUNK_UNROLL = 4
EXPERT_ROWS = 256
ATTN_TILE = 1024
VMEM_BIG = 56 * 1024 * 1024


def _cparams(sem, vmem=None):
    return pltpu.CompilerParams(dimension_semantics=sem, vmem_limit_bytes=vmem)


def _nt(a, b):
    return lax.dot_general(a, b, (((1,), (1,)), ((), ())), preferred_element_type=F32)


def _rms(x, width):
    return lax.rsqrt(jnp.sum(x * x, axis=-1, keepdims=True) * (1.0 / width) + EPS)


def _round_up(x, m):
    return ((x + m - 1) // m) * m


def _ada_body(c_ref, w_ref, b_ref, o_ref):
    c = c_ref[...]
    s = (c * jax.nn.sigmoid(c)).astype(BF16)
    o_ref[...] = jnp.dot(s, w_ref[...].astype(BF16), preferred_element_type=F32) + b_ref[...]


def _ada(c, w_ada, b_ada):
    r, d = c.shape
    n = w_ada.shape[1]
    tn = 1536 if n % 1536 == 0 else n
    return pl.pallas_call(
        _ada_body,
        grid=(n // tn,),
        in_specs=[pl.BlockSpec((r, d), lambda j: (0, 0)),
                  pl.BlockSpec((d, tn), lambda j: (0, j)),
                  pl.BlockSpec((1, tn), lambda j: (0, j))],
        out_specs=pl.BlockSpec((r, tn), lambda j: (0, j)),
        out_shape=jax.ShapeDtypeStruct((r, n), F32),
        compiler_params=_cparams(("arbitrary",), 40 * 1024 * 1024),
        name="ada",
    )(c, w_ada, b_ada.reshape(1, n))


_SEG = dict(qa_kr=(0, 512), kva=(512, 768), gq=(768, 1024), gk=(1024, 1280),
            gv=(1280, 1792), gr=(1792, 2304), glr=(2304, 2432))
_W1_COLS = 2432


def _proj_body(pos0, ts, period, x_ref, sh_ref, sc_ref, gmix_ref, w1_ref, gqa_ref, wuq_ref, gkv_ref, gqk_ref,
               rope_ref, wg2_ref, bg2_ref,
               q_ref, lat_ref, kr_ref, gq_o, gk_o, gv_o, gl_o, gr_o, trig_scr):
    i = pl.program_id(1)
    x = x_ref[...]
    d = x.shape[-1]
    h = (x * _rms(x, d) * gmix_ref[...]) * (1.0 + sc_ref[...]) + sh_ref[...]
    hb = h.astype(BF16)

    def seg(name):
        a, b = _SEG[name]
        return jnp.dot(hb, w1_ref[:, a:b], preferred_element_type=F32)

    @pl.when((pl.program_id(0) == 0) & (i == 0))
    def _():
        row = lax.broadcasted_iota(I32, (ts, LANES), 0)
        if period is not None:
            row = row & (period - 1)
        row_ang = row.astype(F32) * rope_ref[0:1, :]
        trig_scr[0] = jnp.cos(row_ang)
        trig_scr[1] = jnp.sin(row_ang)

    tile_pos = pos0 + (i * ts if period is None else 0 * i)
    base_ang = jnp.broadcast_to(tile_pos.astype(F32) * rope_ref[0:1, :], (8, LANES))
    cos_a, sin_a = jnp.cos(base_ang)[0:1, :], jnp.sin(base_ang)[0:1, :]
    cos = cos_a * trig_scr[0] - sin_a * trig_scr[1]
    sin = (sin_a * trig_scr[0] + cos_a * trig_scr[1]) * rope_ref[1:2, :]
    lane = lax.broadcasted_iota(I32, (ts, LANES), 1)
    first_half = (lane & HALF) == 0
    low64 = lane < ROPE

    def rope(v):
        partner = jnp.where(first_half, pltpu.roll(v, LANES - HALF, 1), pltpu.roll(v, HALF, 1))
        return v * cos + partner * sin

    qa_kr = seg("qa_kr")
    qa = qa_kr[:, 0:Q_LORA]
    qn = (qa * _rms(qa, Q_LORA) * gqa_ref[...]).astype(BF16)
    qf = jnp.dot(qn, wuq_ref[...], preferred_element_type=F32)
    rope_blocks = (rope(qf[:, 4 * NOPE:4 * NOPE + LANES]), rope(qf[:, 4 * NOPE + LANES:4 * NOPE + 2 * LANES]))
    for hd in range(HEADS):
        nope = qf[:, NOPE * hd:NOPE * (hd + 1)]
        blk = rope_blocks[hd // 2]
        if hd % 2:
            blk = pltpu.roll(blk, ROPE, 1)
        blk = jnp.where(low64, blk, 0.0)
        ss = jnp.sum(nope * nope, axis=-1, keepdims=True) + jnp.sum(blk * blk, axis=-1, keepdims=True)
        scl = lax.rsqrt(ss * (1.0 / QK) + EPS) * (QK ** -0.5 * LOG2_E)
        q_ref[hd, :, 0:NOPE] = (nope * scl * gqk_ref[0:1, :]).astype(BF16)
        q_ref[hd, :, NOPE:QK_PAD] = (blk * scl * gqk_ref[1:2, :]).astype(BF16)

    kva = seg("kva")
    lat_ref[...] = kva * _rms(kva, KV_LORA) * gkv_ref[...]
    kr_ref[...] = rope(qa_kr[:, Q_LORA:Q_LORA + LANES])[:, 0:ROPE]

    gq_o[...] = seg("gq") * (GLA_DK ** -0.5)
    gk_o[...] = seg("gk")
    gv_o[...] = seg("gv").astype(BF16)
    gr_o[...] = seg("gr")
    z = jnp.dot(seg("glr").astype(BF16), wg2_ref[...], preferred_element_type=F32) + bg2_ref[...]
    gl_o[...] = (jnp.minimum(z, 0.0) - jnp.log1p(jnp.exp(-jnp.abs(z)))) * (1.0 / GATE_NORM)


def _proj(x, shift, scale, pos0, wts, period=None):
    b, s, d = x.shape
    ts = min(s, 512)
    assert period is None or (period & (period - 1) == 0 and ts % period == 0)
    row = lambda a: pl.BlockSpec(a.shape, lambda bi, i: (0,) * a.ndim)
    tok = lambda w: pl.BlockSpec((None, ts, w), lambda bi, i: (bi, i, 0))
    mod = tok(d) if shift.shape[1] == s and s > 1 else pl.BlockSpec((None, 1, d), lambda bi, i: (bi, 0, 0))
    small = [wts["g_mix"], wts["w1"], wts["g_qa"], wts["w_uq"], wts["g_kv"], wts["g_qk_q"], wts["rope"],
             wts["w_g2"], wts["b_g2"]]
    out_shape = (
        jax.ShapeDtypeStruct((b, HEADS, s, QK_PAD), BF16),
        jax.ShapeDtypeStruct((b, s, KV_LORA), F32),
        jax.ShapeDtypeStruct((b, s, ROPE), F32),
        jax.ShapeDtypeStruct((b, s, HEADS * GLA_DK), F32),
        jax.ShapeDtypeStruct((b, s, HEADS * GLA_DK), F32),
        jax.ShapeDtypeStruct((b, s, HEADS * GLA_DV), BF16),
        jax.ShapeDtypeStruct((b, s, HEADS * GLA_DK), F32),
        jax.ShapeDtypeStruct((b, s, HEADS * GLA_DV), F32),
    )
    out_specs = (
        pl.BlockSpec((None, HEADS, ts, QK_PAD), lambda bi, i: (bi, 0, i, 0)),
        tok(KV_LORA), tok(ROPE), tok(HEADS * GLA_DK), tok(HEADS * GLA_DK), tok(HEADS * GLA_DV),
        tok(HEADS * GLA_DK), tok(HEADS * GLA_DV),
    )
    return pl.pallas_call(
        functools.partial(_proj_body, pos0, ts, period),
        grid=(b, s // ts),
        in_specs=[tok(d), mod, mod] + [row(a) for a in small],
        out_specs=out_specs,
        out_shape=out_shape,
        scratch_shapes=[pltpu.VMEM((2, ts, LANES), F32)],
        compiler_params=_cparams(("arbitrary", "arbitrary"), VMEM_BIG),
        name="proj",
    )(x, shift, scale, *small)


def _key_rows(lat, kr, wk_ref, gk_ref, k_out):
    kn_all = jnp.dot(lat, wk_ref[...], preferred_element_type=F32)
    kr_ss = jnp.sum(kr * kr, axis=-1, keepdims=True)
    for hd in range(HEADS):
        kn = kn_all[:, NOPE * hd:NOPE * (hd + 1)]
        scl = lax.rsqrt((jnp.sum(kn * kn, axis=-1, keepdims=True) + kr_ss) * (1.0 / QK) + EPS)
        k_out[hd, :, 0:NOPE] = (kn * scl * gk_ref[0:1, :]).astype(BF16)
        k_out[hd, :, NOPE:QK] = (kr * scl * gk_ref[1:2, 0:ROPE]).astype(BF16)
        k_out[hd, :, QK:QK_PAD] = jnp.zeros((kr.shape[0], QK_PAD - QK), BF16)


def _kv_body(lat_ref, kr_ref, wk_ref, wv_ref, gk_ref, k_ref, v_ref):
    lat = lat_ref[...].astype(BF16)
    _key_rows(lat, kr_ref[...], wk_ref, gk_ref, k_ref)
    v_t = _nt(wv_ref[...], lat)
    for hd in range(HEADS):
        v_ref[hd] = v_t[V_DIM * hd:V_DIM * (hd + 1), :].astype(BF16)


def _kv(lat, kr, w_uk, w_uv_t, g_qk_k):
    b, s, _ = lat.shape
    ts = min(s, ATTN_TILE)
    return pl.pallas_call(
        _kv_body,
        grid=(b, s // ts),
        in_specs=[pl.BlockSpec((None, ts, KV_LORA), lambda bi, i: (bi, i, 0)),
                  pl.BlockSpec((None, ts, ROPE), lambda bi, i: (bi, i, 0)),
                  pl.BlockSpec(w_uk.shape, lambda bi, i: (0, 0)),
                  pl.BlockSpec(w_uv_t.shape, lambda bi, i: (0, 0)),
                  pl.BlockSpec(g_qk_k.shape, lambda bi, i: (0, 0))],
        out_specs=(pl.BlockSpec((None, HEADS, ts, QK_PAD), lambda bi, i: (bi, 0, i, 0)),
                   pl.BlockSpec((None, HEADS, None, V_DIM, ts), lambda bi, i: (bi, 0, i, 0, 0))),
        out_shape=(jax.ShapeDtypeStruct((b, HEADS, s, QK_PAD), BF16),
                   jax.ShapeDtypeStruct((b, HEADS, s // ts, V_DIM, ts), BF16)),
        compiler_params=_cparams(("arbitrary", "arbitrary")),
        name="kv",
    )(lat, kr, w_uk, w_uv_t, g_qk_k)


def _attn_prompt_body(t, q_ref, qn_ref, k_ref, vt_ref, o_ref, s_a, s_b):
    i = pl.program_id(2)

    def scores(q, j, buf):
        buf[...] = _nt(k_ref[pl.ds(pl.multiple_of(j * t, t), t), :], q)

    def consume(j, buf, carry, masked=False):
        m, l, acc = carry
        s = buf[...]
        if masked:
            visible = (lax.broadcasted_iota(I32, (t, t), 0) // CHUNK) <= (lax.broadcasted_iota(I32, (t, t), 1) // CHUNK)
            s = jnp.where(visible, s, NEG)
        m_new = jnp.maximum(m, jnp.max(s, axis=0, keepdims=True))
        alpha = jnp.exp2(m - m_new)
        p = jnp.exp2(s - m_new)
        l = alpha * l + jnp.sum(p, axis=0, keepdims=True)
        acc = alpha * acc + jnp.dot(vt_ref[j], p.astype(BF16), preferred_element_type=F32)
        return m_new, l, acc

    def run(first, second):
        q = q_ref[...]

        @pl.when(i == 0)
        def _():
            scores(q, 0, first)

        def pair(pp, carry):
            j = 2 * pp
            scores(q, j + 1, second)
            carry = consume(j, first, carry)
            scores(q, j + 2, first)
            return consume(j + 1, second, carry)

        def even_tail(carry):
            scores(qn_ref[...], 0, second)
            return consume(i, first, carry, masked=True)

        def odd_tail(carry):
            scores(q, i, second)
            carry = consume(i - 1, first, carry)
            scores(qn_ref[...], 0, first)
            return consume(i, second, carry, masked=True)

        carry = (jnp.full((1, t), NEG, F32), jnp.zeros((1, t), F32), jnp.zeros((V_DIM, t), F32))
        carry = lax.fori_loop(0, i // 2, pair, carry)
        _, l, acc = lax.cond(i % 2 == 1, odd_tail, even_tail, carry)
        o_ref[...] = (acc / l).T.astype(BF16)

    @pl.when(((i + 1) // 2) % 2 == 0)
    def _():
        run(s_a, s_b)

    @pl.when(((i + 1) // 2) % 2 == 1)
    def _():
        run(s_b, s_a)


def _attn_prompt(q, k, v_t):
    b, _, s, _ = q.shape
    t = v_t.shape[-1]
    nq = s // t
    return pl.pallas_call(
        functools.partial(_attn_prompt_body, t),
        grid=(b, HEADS, nq),
        in_specs=[pl.BlockSpec((None, None, t, QK_PAD), lambda bi, h, i: (bi, h, i, 0)),
                  pl.BlockSpec((None, None, t, QK_PAD), lambda bi, h, i: (bi, h, jnp.minimum(i + 1, nq - 1), 0)),
                  pl.BlockSpec((None, None, s, QK_PAD), lambda bi, h, i: (bi, h, 0, 0)),
                  pl.BlockSpec((None, None, nq, V_DIM, t), lambda bi, h, i: (bi, h, 0, 0, 0))],
        out_specs=pl.BlockSpec((None, t, V_DIM), lambda bi, h, i: (bi, i, h)),
        out_shape=jax.ShapeDtypeStruct((b, s, HEADS * V_DIM), BF16),
        scratch_shapes=[pltpu.VMEM((t, t), F32), pltpu.VMEM((t, t), F32)],
        compiler_params=_cparams(("arbitrary", "arbitrary", "arbitrary"), VMEM_BIG),
        name="attn_prompt",
    )(q, q, k, v_t)


def _attn_sample_body(past, sq, q_ref, plat_ref, pkr_ref, nlat_ref, nkr_ref, wk_ref, wv_ref, gk_ref, o_ref,
                      kp_scr, kn_scr):
    plat = plat_ref[...].astype(BF16)
    nlat = nlat_ref[...].astype(BF16)
    _key_rows(plat, pkr_ref[...], wk_ref, gk_ref, kp_scr)
    _key_rows(nlat, nkr_ref[...], wk_ref, gk_ref, kn_scr)
    vp_t = _nt(wv_ref[...], plat).astype(BF16)
    vn_t = _nt(wv_ref[...], nlat).astype(BF16)
    hq = HEADS * sq
    col = lax.broadcasted_iota(I32, (sq, hq), 1)
    key_chunk = (past + lax.broadcasted_iota(I32, (sq, hq), 0)) // CHUNK
    qry_chunk = (past + col % sq) // CHUNK
    qrow = lax.broadcasted_iota(I32, (hq, 1), 0) // sq
    q_all = jnp.concatenate([q_ref[hd] for hd in range(HEADS)], axis=0)
    s_p = jnp.zeros((past, hq), F32)
    s_n = jnp.zeros((sq, hq), F32)
    for hd in range(HEADS):
        q_h = jnp.where(qrow == hd, q_all, jnp.zeros((), BF16))
        s_p = s_p + _nt(kp_scr[hd], q_h)
        s_n = s_n + _nt(kn_scr[hd], q_h)
    s_n = jnp.where(key_chunk <= qry_chunk, s_n, NEG)
    m = jnp.maximum(jnp.max(s_p, axis=0, keepdims=True), jnp.max(s_n, axis=0, keepdims=True))
    p_p = jnp.exp2(s_p - m)
    p_n = jnp.exp2(s_n - m)
    inv_l = 1.0 / (jnp.sum(p_p, axis=0, keepdims=True) + jnp.sum(p_n, axis=0, keepdims=True))
    p_p = p_p.astype(BF16)
    p_n = p_n.astype(BF16)
    lane_head = lax.broadcasted_iota(I32, (V_DIM, hq), 1) // sq
    o_t = jnp.zeros((V_DIM, hq), F32)
    for hd in range(HEADS):
        rows = slice(V_DIM * hd, V_DIM * (hd + 1))
        o_h = (jnp.dot(vp_t[rows, :], p_p, preferred_element_type=F32)
               + jnp.dot(vn_t[rows, :], p_n, preferred_element_type=F32))
        o_t = o_t + jnp.where(lane_head == hd, o_h, 0.0)
    o_all = (o_t * inv_l).T
    for hd in range(HEADS):
        o_ref[:, V_DIM * hd:V_DIM * (hd + 1)] = o_all[sq * hd:sq * (hd + 1), :].astype(BF16)


def _attn_sample(q, past_lat, past_kr, lat, kr, w_uk, w_uv_t, g_qk_k):
    b, _, sq, _ = q.shape
    past = past_lat.shape[1]
    rows = lambda n, w: pl.BlockSpec((None, n, w), lambda bi: (bi, 0, 0))
    full = lambda a: pl.BlockSpec(a.shape, lambda bi: (0,) * a.ndim)
    return pl.pallas_call(
        functools.partial(_attn_sample_body, past, sq),
        grid=(b,),
        in_specs=[pl.BlockSpec((None, HEADS, sq, QK_PAD), lambda bi: (bi, 0, 0, 0)),
                  rows(past, KV_LORA), rows(past, ROPE), rows(sq, KV_LORA), rows(sq, ROPE),
                  full(w_uk), full(w_uv_t), full(g_qk_k)],
        out_specs=rows(sq, HEADS * V_DIM),
        out_shape=jax.ShapeDtypeStruct((b, sq, HEADS * V_DIM), BF16),
        scratch_shapes=[pltpu.VMEM((HEADS, past, QK_PAD), BF16), pltpu.VMEM((HEADS, sq, QK_PAD), BF16)],
        compiler_params=_cparams(("arbitrary",), 40 * 1024 * 1024),
        name="attn_sample",
    )(q, past_lat, past_kr, lat, kr, w_uk, w_uv_t, g_qk_k)


def _gla_masks(c, rows):
    idx = np.arange(rows)
    same = (idx // c)[:, None] == (idx // c)[None, :]
    le = same & (idx[None, :] <= idx[:, None])
    gt = same & (idx[None, :] > idx[:, None])
    return np.concatenate([le, gt], axis=0).astype(np.float32), int(np.log2(c))


def _level_exponents(b, g, c, level):
    n = c >> level
    rows = b.shape[0]
    row = lax.broadcasted_iota(I32, (rows, 1), 0)
    if n >= 8:
        split = b.reshape(rows // n, n, LANES)[:, n // 2 - 1:n // 2, :]
        split = jnp.broadcast_to(split, (rows // n, n, LANES)).reshape(rows, LANES)
        return jnp.where((row & (n // 2)) != 0, b - split, split - b)
    g_prev = pltpu.roll(g, 1, 0)
    g_next = pltpu.roll(g, rows - 1, 0)
    if n == 4:
        r = row & 3
        return jnp.where(r == 0, g_next, jnp.where(r == 1, 0.0, jnp.where(r == 2, g, g + g_prev)))
    assert n == 2
    return jnp.where((row & 1) != 0, g, 0.0)


def _gla_body(c, n_chunks, unit, levels, mall_ref, q_ref, k_ref, g_ref, v_ref, r_ref, s0_ref, gout_ref,
              o_ref, sfin_ref, st_scr):
    it = pl.program_id(1)

    @pl.when(it == 0)
    def _():
        st_scr[...] = s0_ref[...]

    ru = unit * c
    lane = lax.broadcasted_iota(I32, (ru, LANES), 1)
    head_lanes = (lane < GLA_DK, lane >= GLA_DK)
    st_lane_lo = lax.broadcasted_iota(I32, (GLA_DV, LANES), 1) < GLA_DK
    row = lax.broadcasted_iota(I32, (ru, 1), 0)
    ri = lax.broadcasted_iota(I32, (ru, ru), 0)
    ci = lax.broadcasted_iota(I32, (ru, ru), 1)
    mall = mall_ref[...]

    for un in range(n_chunks // unit):
        rows = slice(un * ru, (un + 1) * ru)
        for p in range(HEADS // 2):
            ls = slice(LANES * p, LANES * (p + 1))
            g = g_ref[rows, ls]
            q = q_ref[rows, ls]
            k = k_ref[rows, ls]
            g_hi = g.astype(BF16)
            g_lo = (g - g_hi.astype(F32)).astype(BF16)
            e2 = jnp.dot(mall, jnp.concatenate([g_hi, g_lo], axis=1), preferred_element_type=F32)
            e = e2[:, 0:LANES] + e2[:, LANES:2 * LANES]
            b = e[0:ru]
            eb = jnp.exp(b)
            qb = q * eb
            kd = (k * jnp.exp(e[ru:2 * ru])).astype(BF16)
            qs, ks = [q], [k.astype(BF16)]
            for l in range(levels):
                bottom = (row & (c >> (l + 1))) != 0
                decay = jnp.exp(_level_exponents(b, g, c, l))
                qs.append(jnp.where(bottom, q * decay, 0.0))
                ks.append(jnp.where(bottom, 0.0, k * decay).astype(BF16))
            states = [st_scr[p]]
            for j in range(unit):
                cr = slice(j * c, (j + 1) * c)
                upd = [lax.dot_general(v_ref[rows, GLA_DV * (2 * p + hh):GLA_DV * (2 * p + hh + 1)][cr, :], kd[cr, :],
                                       (((0,), (0,)), ((), ())), preferred_element_type=F32) for hh in range(2)]
                d_last = eb[j * c + c - 1:j * c + c, :]
                states.append(states[-1] * d_last + jnp.where(st_lane_lo, upd[0], upd[1]))
            st_scr[p] = states[-1]
            for hh in range(2):
                hd = 2 * p + hh
                sel = head_lanes[hh]
                a = jnp.where(ri == ci, _nt(jnp.where(sel, qs[0], 0.0).astype(BF16), ks[0]), 0.0)
                for l in range(levels):
                    pr = _nt(jnp.where(sel, qs[l + 1], 0.0).astype(BF16), ks[l + 1])
                    a = a + jnp.where((ri ^ ci) < (c >> l), pr, 0.0)
                vh = v_ref[rows, GLA_DV * hd:GLA_DV * (hd + 1)]
                qb_h = jnp.where(sel, qb, 0.0).astype(BF16)
                o_state = [_nt(qb_h[j * c:(j + 1) * c, :], states[j].astype(BF16)) for j in range(unit)]
                o = jnp.dot(a.astype(BF16), vh, preferred_element_type=F32) + jnp.concatenate(o_state, axis=0)
                on = o * _rms(o, GLA_DV) * gout_ref[...]
                r = r_ref[rows, GLA_DV * hd:GLA_DV * (hd + 1)]
                o_ref[rows, GLA_DV * hd:GLA_DV * (hd + 1)] = (on * (r * jax.nn.sigmoid(r))).astype(BF16)

    @pl.when(it == pl.num_programs(1) - 1)
    def _():
        sfin_ref[...] = st_scr[...]


def _gla(gq, gk, gl, gv, gr, s0, g_out):
    b, s, _ = gq.shape
    c = min(CHUNK, s)
    tile = min(s, 8 * c)
    unit = next(u for u in (4, 2, 1) if (tile // c) % u == 0)
    masks, levels = _gla_masks(c, unit * c)
    mall = jnp.asarray(masks, BF16)
    tok = lambda w: pl.BlockSpec((None, tile, w), lambda bi, i: (bi, i, 0))
    st_spec = pl.BlockSpec((None, HEADS // 2, GLA_DV, LANES), lambda bi, i: (bi, 0, 0, 0))
    return pl.pallas_call(
        functools.partial(_gla_body, c, tile // c, unit, levels),
        grid=(b, s // tile),
        in_specs=[pl.BlockSpec(mall.shape, lambda bi, i: (0, 0)),
                  tok(HEADS * GLA_DK), tok(HEADS * GLA_DK), tok(HEADS * GLA_DK), tok(HEADS * GLA_DV),
                  tok(HEADS * GLA_DV), st_spec, pl.BlockSpec(g_out.shape, lambda bi, i: (0, 0))],
        out_specs=(tok(HEADS * GLA_DV), st_spec),
        out_shape=(jax.ShapeDtypeStruct((b, s, HEADS * GLA_DV), BF16),
                   jax.ShapeDtypeStruct((b, HEADS // 2, GLA_DV, LANES), F32)),
        scratch_shapes=[pltpu.VMEM((HEADS // 2, GLA_DV, LANES), F32)],
        compiler_params=_cparams(("arbitrary", "arbitrary")),
        name="gla",
    )(mall, gq, gk, gl, gv, gr, s0, g_out)


def _state_to_pairs(s):
    b = s.shape[0]
    s = s.reshape(b, HEADS // 2, 2, GLA_DK, GLA_DV)
    return jnp.transpose(s, (0, 1, 4, 2, 3)).reshape(b, HEADS // 2, GLA_DV, 2 * GLA_DK)


def _state_from_pairs(s):
    b = s.shape[0]
    s = s.reshape(b, HEADS // 2, GLA_DV, 2, GLA_DK)
    return jnp.transpose(s, (0, 1, 3, 4, 2)).reshape(b, HEADS, GLA_DK, GLA_DV)


def _post_body(x_ref, om_ref, og_ref, gt_ref, sc_ref, sh_ref, wo_ref, gffn_ref, wr_ref, br_ref,
               x2_ref, h_ref, idx_ref, wt_ref, rank_ref, cnt_ref):
    half = om_ref.shape[-1]
    mix = (jnp.dot(om_ref[...], wo_ref[0:half, :], preferred_element_type=F32)
           + jnp.dot(og_ref[...], wo_ref[half:2 * half, :], preferred_element_type=F32))
    x2 = x_ref[...] + gt_ref[...] * mix
    x2_ref[...] = x2
    d = x2.shape[-1]
    h = (x2 * _rms(x2, d) * gffn_ref[...]) * (1.0 + sc_ref[...]) + sh_ref[...]
    h_hi = h.astype(BF16)
    h_ref[...] = h_hi
    h_lo = (h - h_hi.astype(F32)).astype(BF16)
    logits = _nt(wr_ref[0], h_hi) + _nt(wr_ref[0], h_lo) + _nt(wr_ref[1], h_hi) + br_ref[...]
    n_exp, tm = logits.shape
    eid = lax.broadcasted_iota(I32, (n_exp, tm), 0)
    vals, tops, ids = logits, [], []
    for _ in range(TOP_K):
        m = jnp.max(vals, axis=0, keepdims=True)
        sel = jnp.min(jnp.where(vals == m, eid, n_exp), axis=0, keepdims=True)
        tops.append(m)
        ids.append(sel)
        vals = jnp.where(eid == sel, -jnp.inf, vals)
    es = [jnp.exp(t - tops[0]) for t in tops]
    tot = es[0] + es[1] + es[2] + es[3]
    idx_ref[...] = jnp.concatenate(ids, axis=0)
    wt_ref[...] = jnp.concatenate([e / tot for e in es], axis=0)
    hits = [eid == sel for sel in ids]
    member = jnp.zeros((n_exp, tm), F32)
    for hk in hits:
        member = member + jnp.where(hk, 1.0, 0.0)
    before = lax.broadcasted_iota(I32, (tm, tm), 0) < lax.broadcasted_iota(I32, (tm, tm), 1)
    prefix = jnp.dot(member.astype(BF16), jnp.where(before, 1.0, 0.0).astype(BF16), preferred_element_type=F32)
    rank_ref[...] = jnp.concatenate(
        [jnp.sum(jnp.where(hk, prefix, 0.0), axis=0, keepdims=True) for hk in hits], axis=0).astype(I32)
    cnt_ref[...] = jnp.broadcast_to(jnp.sum(member, axis=1, keepdims=True), (n_exp, LANES)).astype(I32)


def _post(x, om, og, gate, scale, shift, w_o, g_ffn, w_r2, b_r):
    t, d = x.shape
    tm = ROUTE_TILE
    per_tok = gate.shape[0] == t
    mod = pl.BlockSpec((tm, d), lambda i: (i, 0)) if per_tok else pl.BlockSpec((1, d), lambda i: (0, 0))
    tok = lambda w: pl.BlockSpec((tm, w), lambda i: (i, 0))
    full = lambda a: pl.BlockSpec(a.shape, lambda i: (0,) * a.ndim)
    return pl.pallas_call(
        _post_body,
        grid=(t // tm,),
        in_specs=[tok(d), tok(om.shape[1]), tok(og.shape[1]), mod, mod, mod, full(w_o), full(g_ffn), full(w_r2),
                  full(b_r)],
        out_specs=(tok(d), tok(d),
                   pl.BlockSpec((TOP_K, tm), lambda i: (0, i)), pl.BlockSpec((TOP_K, tm), lambda i: (0, i)),
                   pl.BlockSpec((TOP_K, tm), lambda i: (0, i)),
                   pl.BlockSpec((None, N_EXPERTS, LANES), lambda i: (i, 0, 0))),
        out_shape=(jax.ShapeDtypeStruct((t, d), F32), jax.ShapeDtypeStruct((t, d), BF16),
                   jax.ShapeDtypeStruct((TOP_K, t), I32), jax.ShapeDtypeStruct((TOP_K, t), F32),
                   jax.ShapeDtypeStruct((TOP_K, t), I32), jax.ShapeDtypeStruct((t // tm, N_EXPERTS, LANES), I32)),
        compiler_params=_cparams(("arbitrary",), 40 * 1024 * 1024),
        name="post",
    )(x, om, og, gate, scale, shift, w_o, g_ffn, w_r2, b_r)


def _route_tables(idx, lrank, cnt3):
    nt = cnt3.shape[0]
    t = idx.shape[1]
    cnt = cnt3[:, :, 0]
    run = _round_up(cnt, RUN_CHUNK)
    lo_end = jnp.cumsum(run, axis=1)
    lo = lo_end - run
    n_chunks = lo_end[:, -1] // RUN_CHUNK
    region = _round_up(jnp.sum(run, axis=0), EXPERT_ROWS)
    g_end = jnp.cumsum(region)
    run_dest = (g_end - region)[None, :] + jnp.cumsum(run, axis=0) - run
    max_rows = TOP_K * ROUTE_TILE + N_EXPERTS * (RUN_CHUNK - 1)
    n_tab = _round_up(max_rows, RUN_CHUNK) // RUN_CHUNK
    c_start = jnp.arange(n_tab, dtype=I32) * RUN_CHUNK
    e_of_c = jnp.minimum(jnp.sum(lo_end[:, None, :] <= c_start[None, :, None], axis=2), N_EXPERTS - 1)
    pick = e_of_c[:, :, None] == jnp.arange(N_EXPERTS, dtype=I32)[None, None, :]
    chunk_dest = jnp.sum(jnp.where(pick, (run_dest - lo)[:, None, :], 0), axis=2) + c_start[None, :]
    table = jnp.concatenate([chunk_dest, n_chunks[:, None]], axis=1).astype(I32).reshape(nt, 1, n_tab + 1)
    eid = jnp.arange(N_EXPERTS, dtype=I32)[:, None]
    lo_tok = jnp.repeat(lo.T, ROUTE_TILE, axis=1)
    lpos = jnp.stack([jnp.sum(jnp.where(idx[k][None, :] == eid, lo_tok, 0), axis=0) for k in range(TOP_K)])
    lpos = (lpos + lrank).astype(I32)
    n_blocks = _round_up(t * TOP_K + nt * N_EXPERTS * (RUN_CHUNK - 1), EXPERT_ROWS) // EXPERT_ROWS + N_EXPERTS
    b_start = jnp.arange(n_blocks, dtype=I32) * EXPERT_ROWS
    blk_e = jnp.minimum(jnp.sum(g_end[None, :] <= b_start[:, None], axis=1), N_EXPERTS - 1).astype(I32)
    n_valid = (g_end[-1:] // EXPERT_ROWS).astype(I32)
    used = region > 0
    e_ids = jnp.arange(N_EXPERTS, dtype=I32)
    later_used = used[None, :] & (e_ids[None, :] > e_ids[:, None])
    nxt_e = jnp.min(jnp.where(later_used, e_ids[None, :], N_EXPERTS), axis=1)
    nxt_e = jnp.where(nxt_e < N_EXPERTS, nxt_e, -1).astype(I32)
    ord_e = (jnp.cumsum(used.astype(I32)) - 1).astype(I32)
    tail = jnp.concatenate([jnp.where(used, g_end - EXPERT_ROWS, -1), n_valid]).astype(I32)
    tail = tail.reshape(1, N_EXPERTS + 1)
    return dict(table=table, lpos=lpos, blk_e=blk_e, n_valid=n_valid, nxt_e=nxt_e, ord_e=ord_e, tail=tail,
                n_blocks=n_blocks)


def _chunk_copy(src, dst, sem):
    return pltpu.make_async_copy(src, dst, sem)


def _for_row_blocks(n_rows, body):
    full = n_rows // SORT_ROWS
    rem = n_rows - full * SORT_ROWS
    tail = pl.multiple_of(full * SORT_ROWS, SORT_ROWS)

    def whole(rb, carry):
        body(pl.multiple_of(rb * SORT_ROWS, SORT_ROWS), SORT_ROWS)
        return carry

    lax.fori_loop(0, full, whole, 0)

    @pl.when(rem > SORT_ROWS // 2)
    def _():
        body(tail, SORT_ROWS)

    @pl.when((rem > 0) & (rem <= SORT_ROWS // 2))
    def _():
        body(tail, SORT_ROWS // 2)


def _for_chunks(n, body):
    groups = n // CHUNK_UNROLL

    def group(g, carry):
        for u in range(CHUNK_UNROLL):
            body(g * CHUNK_UNROLL + u)
        return carry

    def single(c, carry):
        body(c)
        return carry

    lax.fori_loop(0, groups, group, 0)
    lax.fori_loop(groups * CHUNK_UNROLL, n, single, 0)


def _scatter_body(n_tab, nt_a, n_blocks, tab_ref, prv_ref, tail_ref, lpos_ref, ha_ref, hb_ref, xout_ref, sorted_scr,
                  zero_scr, sems, zero_sem):
    i = pl.program_id(0)
    slot = i % 2
    n_chunks = tab_ref[0, n_tab]
    tt = ha_ref.shape[0]

    @pl.when(i == 0)
    def _():
        zero_scr[...] = jnp.zeros_like(zero_scr)
        n_valid = tail_ref[0, N_EXPERTS]

        def block(start):
            return xout_ref.at[pl.ds(pl.multiple_of(start, EXPERT_ROWS), EXPERT_ROWS)]

        for e in range(N_EXPERTS):
            @pl.when(tail_ref[0, e] >= 0)
            def _():
                _chunk_copy(zero_scr, block(tail_ref[0, e]), zero_sem).start()

        def fill(b, carry):
            _chunk_copy(zero_scr, block(b * EXPERT_ROWS), zero_sem).start()
            return carry

        def fill_done(b, carry):
            _chunk_copy(zero_scr, block(0), zero_sem).wait()
            return carry

        lax.fori_loop(n_valid, n_blocks, fill, 0)
        for e in range(N_EXPERTS):
            @pl.when(tail_ref[0, e] >= 0)
            def _():
                _chunk_copy(zero_scr, block(0), zero_sem).wait()
        lax.fori_loop(n_valid, n_blocks, fill_done, 0)

    lp16 = lpos_ref[...].astype(jnp.int16)
    h = jnp.where(i < nt_a, ha_ref[...], hb_ref[...])

    def sort_block(r0, size):
        rid = (r0 + lax.broadcasted_iota(I32, (size, tt), 0)).astype(jnp.int16)
        onehot = jnp.zeros((size, tt), BF16)
        for k in range(TOP_K):
            onehot = onehot + jnp.where(lp16[k:k + 1, :] == rid, jnp.ones((), BF16), jnp.zeros((), BF16))
        sorted_scr[slot, pl.ds(r0, size), :] = jnp.dot(onehot, h, preferred_element_type=F32).astype(BF16)

    _for_row_blocks(n_chunks * RUN_CHUNK, sort_block)

    def issue(c):
        src = sorted_scr.at[slot, pl.ds(pl.multiple_of(c * RUN_CHUNK, RUN_CHUNK), RUN_CHUNK)]
        dst = xout_ref.at[pl.ds(pl.multiple_of(tab_ref[0, c], RUN_CHUNK), RUN_CHUNK)]
        _chunk_copy(src, dst, sems.at[slot]).start()

    def drain(sl):
        def body(c):
            _chunk_copy(sorted_scr.at[sl, pl.ds(0, RUN_CHUNK)], xout_ref.at[pl.ds(0, RUN_CHUNK)], sems.at[sl]).wait()
        return body

    _for_chunks(n_chunks, issue)

    @pl.when(i > 0)
    def _():
        _for_chunks(prv_ref[0, n_tab], drain(1 - slot))

    @pl.when(i == pl.num_programs(0) - 1)
    def _():
        _for_chunks(n_chunks, drain(slot))


def _scatter(table, tail, lpos, h_a, h_b, n_blocks):
    d = h_a.shape[1]
    nt_a, nt_b = h_a.shape[0] // ROUTE_TILE, h_b.shape[0] // ROUTE_TILE
    n_tab = table.shape[2] - 1
    sort_cap = _round_up(n_tab * RUN_CHUNK, SORT_ROWS)
    tab_spec = lambda f: pl.BlockSpec((None, 1, n_tab + 1), lambda i: (f(i), 0, 0), memory_space=pltpu.SMEM)
    return pl.pallas_call(
        functools.partial(_scatter_body, n_tab, nt_a, n_blocks),
        grid=(nt_a + nt_b,),
        in_specs=[tab_spec(lambda i: i), tab_spec(lambda i: jnp.maximum(i - 1, 0)),
                  pl.BlockSpec((1, N_EXPERTS + 1), lambda i: (0, 0), memory_space=pltpu.SMEM),
                  pl.BlockSpec((TOP_K, ROUTE_TILE), lambda i: (0, i)),
                  pl.BlockSpec((ROUTE_TILE, d), lambda i: (jnp.minimum(i, nt_a - 1), 0)),
                  pl.BlockSpec((ROUTE_TILE, d), lambda i: (jnp.maximum(i - nt_a, 0), 0))],
        out_specs=pl.BlockSpec(memory_space=pl.ANY),
        out_shape=jax.ShapeDtypeStruct((n_blocks * EXPERT_ROWS, d), BF16),
        scratch_shapes=[pltpu.VMEM((2, sort_cap, d), BF16), pltpu.VMEM((EXPERT_ROWS, d), BF16),
                        pltpu.SemaphoreType.DMA((2,)), pltpu.SemaphoreType.DMA(())],
        compiler_params=_cparams(("arbitrary",), 48 * 1024 * 1024),
        name="scatter",
    )(table, table, tail, lpos, h_a, h_b)


def _experts_body(be_ref, nv_ref, nxt_ref, ord_ref, x_ref, wgu_hbm, bgu_ref, wd_hbm, bd_ref, y_ref,
                  wgu_f, wd_f, wgu_s, wd_s, sem_gu, sem_d):
    b = pl.program_id(0)
    e = be_ref[b]
    prev = be_ref[jnp.maximum(b - 1, 0)]
    valid = b < nv_ref[0]
    d_ff = wd_s.shape[0]
    slot = ord_ref[e] % 2

    def weights(expert, sl):
        return (pltpu.make_async_copy(wgu_hbm.at[expert], wgu_f.at[sl], sem_gu.at[sl]),
                pltpu.make_async_copy(wd_hbm.at[expert], wd_f.at[sl], sem_d.at[sl]))

    @pl.when(valid & ((b == 0) | (e != prev)))
    def _():
        @pl.when(b == 0)
        def _():
            for cp in weights(e, slot):
                cp.start()

        for cp in weights(e, slot):
            cp.wait()

        @pl.when(nxt_ref[e] >= 0)
        def _():
            for cp in weights(nxt_ref[e], 1 - slot):
                cp.start()

        wgu_s[...] = wgu_f[slot].astype(BF16)
        wd_s[...] = wd_f[slot].astype(BF16)

    @pl.when(valid)
    def _():
        gu = jnp.dot(x_ref[...], wgu_s[...], preferred_element_type=F32) + bgu_ref[...]
        gate = jnp.minimum(gu[:, 0:d_ff], SWIGLU_LIMIT)
        up = jnp.clip(gu[:, d_ff:2 * d_ff], -SWIGLU_LIMIT, SWIGLU_LIMIT)
        act = ((up + 1.0) * (gate * jax.nn.sigmoid(gate * SWIGLU_ALPHA))).astype(BF16)
        y_ref[...] = (jnp.dot(act, wd_s[...], preferred_element_type=F32) + bd_ref[...]).astype(BF16)

    @pl.when(jnp.logical_not(valid))
    def _():
        y_ref[...] = jnp.zeros_like(y_ref)


def _experts(blk_e, n_valid, nxt_e, ord_e, xpad, w_gu, b_gu, w_down, b_down):
    m, d = xpad.shape
    nb = m // EXPERT_ROWS
    n_exp, _, f2 = w_gu.shape
    d_ff = w_down.shape[1]
    last = lambda b, be, nv: jnp.minimum(b, nv[0] - 1)
    grid_spec = pltpu.PrefetchScalarGridSpec(
        num_scalar_prefetch=4,
        grid=(nb,),
        in_specs=[pl.BlockSpec((EXPERT_ROWS, d), lambda b, be, nv, nx, od: (last(b, be, nv), 0)),
                  pl.BlockSpec(memory_space=pl.ANY),
                  pl.BlockSpec((None, 1, f2), lambda b, be, nv, nx, od: (be[last(b, be, nv)], 0, 0)),
                  pl.BlockSpec(memory_space=pl.ANY),
                  pl.BlockSpec((None, 1, d), lambda b, be, nv, nx, od: (be[last(b, be, nv)], 0, 0))],
        out_specs=pl.BlockSpec((EXPERT_ROWS, d), lambda b, be, nv, nx, od: (b, 0)),
        scratch_shapes=[pltpu.VMEM((2, d, f2), F32), pltpu.VMEM((2, d_ff, d), F32),
                        pltpu.VMEM((d, f2), BF16), pltpu.VMEM((d_ff, d), BF16),
                        pltpu.SemaphoreType.DMA((2,)), pltpu.SemaphoreType.DMA((2,))],
    )
    return pl.pallas_call(
        _experts_body,
        grid_spec=grid_spec,
        out_shape=jax.ShapeDtypeStruct((m, d), BF16),
        compiler_params=_cparams(("arbitrary",), VMEM_BIG),
        name="experts",
    )(blk_e, n_valid, nxt_e, ord_e, xpad, w_gu, b_gu.reshape(n_exp, 1, f2), w_down, b_down.reshape(n_exp, 1, d))


def _combine_body(n_tab, tab_ref, nxt_ref, lpos_ref, wt_ref, x2_ref, gt_ref, y_ref, o_ref, ysort_scr, sems):
    i = pl.program_id(0)
    slot = i % 2
    n_chunks = tab_ref[0, n_tab]
    tt = x2_ref.shape[0]

    def fetch(tab, sl):
        def issue(c):
            src = y_ref.at[pl.ds(pl.multiple_of(tab[0, c], RUN_CHUNK), RUN_CHUNK)]
            dst = ysort_scr.at[sl, pl.ds(pl.multiple_of(c * RUN_CHUNK, RUN_CHUNK), RUN_CHUNK)]
            _chunk_copy(src, dst, sems.at[sl]).start()

        _for_chunks(tab[0, n_tab], issue)

    @pl.when(i == 0)
    def _():
        ysort_scr[...] = jnp.zeros_like(ysort_scr)
        fetch(tab_ref, 0)

    @pl.when(i + 1 < pl.num_programs(0))
    def _():
        fetch(nxt_ref, 1 - slot)

    def drain(c):
        _chunk_copy(y_ref.at[pl.ds(0, RUN_CHUNK)], ysort_scr.at[slot, pl.ds(0, RUN_CHUNK)], sems.at[slot]).wait()

    _for_chunks(n_chunks, drain)

    lp16 = lpos_ref[...].astype(jnp.int16)
    w = wt_ref[...].astype(BF16)
    o_ref[...] = x2_ref[...]
    gt = gt_ref[...]

    def gather_block(r0, size):
        rid = (r0 + lax.broadcasted_iota(I32, (size, tt), 0)).astype(jnp.int16)
        pw_t = jnp.zeros((size, tt), BF16)
        for k in range(TOP_K):
            pw_t = pw_t + jnp.where(lp16[k:k + 1, :] == rid, w[k:k + 1, :], jnp.zeros((), BF16))
        ys = ysort_scr[slot, pl.ds(r0, size), :]
        part = lax.dot_general(pw_t, ys, (((0,), (0,)), ((), ())), preferred_element_type=F32)
        o_ref[...] = o_ref[...] + gt * part

    _for_row_blocks(n_chunks * RUN_CHUNK, gather_block)


def _combine(table, lpos, wts, x2, gate, ypad):
    t, d = x2.shape
    nt = t // ROUTE_TILE
    n_tab = table.shape[2] - 1
    per_tok = gate.shape[0] == t
    mod = pl.BlockSpec((ROUTE_TILE, d), lambda i: (i, 0)) if per_tok else pl.BlockSpec((1, d), lambda i: (0, 0))
    sort_cap = _round_up(n_tab * RUN_CHUNK, SORT_ROWS)
    tab_spec = lambda f: pl.BlockSpec((None, 1, n_tab + 1), lambda i: (f(i), 0, 0), memory_space=pltpu.SMEM)
    return pl.pallas_call(
        functools.partial(_combine_body, n_tab),
        grid=(nt,),
        in_specs=[tab_spec(lambda i: i), tab_spec(lambda i: jnp.minimum(i + 1, nt - 1)),
                  pl.BlockSpec((TOP_K, ROUTE_TILE), lambda i: (0, i)),
                  pl.BlockSpec((TOP_K, ROUTE_TILE), lambda i: (0, i)),
                  pl.BlockSpec((ROUTE_TILE, d), lambda i: (i, 0)),
                  mod,
                  pl.BlockSpec(memory_space=pl.ANY)],
        out_specs=pl.BlockSpec((ROUTE_TILE, d), lambda i: (i, 0)),
        out_shape=jax.ShapeDtypeStruct((t, d), F32),
        scratch_shapes=[pltpu.VMEM((2, sort_cap, d), BF16), pltpu.SemaphoreType.DMA((2,))],
        compiler_params=_cparams(("arbitrary",), 48 * 1024 * 1024),
        name="combine",
    )(table, table, lpos, wts, x2, gate, ypad)


def _prep_weights(g_norm_mix, w_in, g_q_a, w_uq, g_kv_a, w_ukv, g_qk_q, g_qk_k, w_g2, b_g2, g_gla_out, w_o,
                  g_norm_ffn, w_router, b_router):
    d = w_in.shape[0]
    o_qa, o_kva, o_kr = 0, Q_LORA, Q_LORA + KV_LORA
    o_gq = o_kr + ROPE
    o_gk = o_gq + HEADS * GLA_DK
    o_gv = o_gk + HEADS * GLA_DK
    o_glr = o_gv + HEADS * GLA_DV
    o_gr = o_glr + GATE_RANK
    kr_cols = w_in[:, o_kr:o_kr + ROPE]
    w1 = jnp.concatenate([
        w_in[:, o_qa:o_kva], kr_cols, kr_cols, w_in[:, o_kva:o_kr], w_in[:, o_gq:o_gk], w_in[:, o_gk:o_gv],
        w_in[:, o_gv:o_glr], w_in[:, o_gr:o_gr + HEADS * GLA_DV], w_in[:, o_glr:o_gr],
        jnp.zeros((d, LANES - GATE_RANK), w_in.dtype)], axis=1).astype(BF16)
    assert w1.shape[1] == _W1_COLS
    wq = w_uq.reshape(Q_LORA, HEADS, QK)
    wuq = jnp.concatenate([wq[:, :, 0:NOPE].reshape(Q_LORA, HEADS * NOPE),
                           wq[:, :, NOPE:QK].reshape(Q_LORA, HEADS * ROPE)], axis=1).astype(BF16)
    wkv = w_ukv.reshape(KV_LORA, HEADS, NOPE + V_DIM)
    wuk = wkv[:, :, 0:NOPE].reshape(KV_LORA, HEADS * NOPE).astype(BF16)
    wuv_t = wkv[:, :, NOPE:].reshape(KV_LORA, HEADS * V_DIM).T.astype(BF16)
    pad_rope = lambda g: jnp.stack([g[0:NOPE], jnp.concatenate([g[NOPE:QK], jnp.zeros((QK_PAD - QK,), g.dtype)])])
    inv = ROPE_THETA ** (-jnp.arange(HALF, dtype=F32) / HALF)
    sign = jnp.concatenate([-jnp.ones((HALF,), F32), jnp.ones((HALF,), F32)])
    rope_tab = jnp.stack([jnp.tile(inv, LANES // HALF), jnp.tile(sign, LANES // ROPE)])
    wg2 = jnp.concatenate([w_g2, jnp.zeros((LANES - GATE_RANK, w_g2.shape[1]), w_g2.dtype)], axis=0).astype(BF16)
    wr_t = w_router.T
    wr_hi = wr_t.astype(BF16)
    wr_lo = (wr_t - wr_hi.astype(F32)).astype(BF16)
    return dict(
        g_mix=g_norm_mix.reshape(1, d), w1=w1, g_qa=g_q_a.reshape(1, -1), w_uq=wuq, g_kv=g_kv_a.reshape(1, -1),
        w_uk=wuk, w_uv_t=wuv_t, g_qk_q=pad_rope(g_qk_q), g_qk_k=pad_rope(g_qk_k), rope=rope_tab, w_g2=wg2,
        b_g2=b_g2.reshape(1, -1), g_out=g_gla_out.reshape(1, -1), w_o=w_o.astype(BF16),
        g_ffn=g_norm_ffn.reshape(1, d), w_r2=jnp.stack([wr_hi, wr_lo]), b_r=b_router.reshape(-1, 1))


def _mixer(x, mod, pos0, past_lat, past_kr, s0_pairs, wts):
    b, s, d = x.shape
    t = b * s
    if b > 1 and s & (s - 1) == 0 and min(t, 512) % s == 0:
        per_tok = lambda j: jnp.broadcast_to(mod[:, j:j + 1], (b, s, d)).reshape(1, t, d)
        outs = _proj(x.reshape(1, t, d), per_tok(0), per_tok(1), pos0, wts, period=s)
        q = outs[0].reshape(HEADS, b, s, QK_PAD).transpose(1, 0, 2, 3)
        lat, kr, gq, gk, gv, gl, gr = [o.reshape(b, s, o.shape[-1]) for o in outs[1:]]
    else:
        q, lat, kr, gq, gk, gv, gl, gr = _proj(x, mod[:, 0:1], mod[:, 1:2], pos0, wts)
    kv_w = (wts["w_uk"], wts["w_uv_t"], wts["g_qk_k"])
    if past_lat is None:
        k_new, vt_new = _kv(lat, kr, *kv_w)
        o_mla = _attn_prompt(q, k_new, vt_new)
    else:
        o_mla = _attn_sample(q, past_lat, past_kr, lat, kr, *kv_w)
    o_gla, s_fin = _gla(gq, gk, gl, gv, gr, s0_pairs, wts["g_out"])
    if b == 1:
        rows = lambda j: mod[0, j:j + 1]
    else:
        rows = lambda j: jnp.broadcast_to(mod[:, j:j + 1], (b, s, d)).reshape(t, d)
    x2, h2, idx, wt, lrank, cnt = _post(x.reshape(t, d), o_mla.reshape(t, -1), o_gla.reshape(t, -1), rows(2), rows(4),
                                        rows(3), wts["w_o"], wts["g_ffn"], wts["w_r2"], wts["b_r"])
    return dict(x2=x2, h2=h2, idx=idx, wt=wt, lrank=lrank, cnt=cnt, gate_f=rows(5), lat=lat, kr=kr, s_fin=s_fin)


def kernel(x_prompt, x_sample, cache_mla_latent, cache_mla_krope, state_gla, c_prompt, c_sample, w_ada, b_ada, g_norm_mix, w_in, g_q_a, w_uq, g_kv_a, w_ukv, g_qk_q, g_qk_k, w_g2, b_g2, g_gla_out, w_o, g_norm_ffn, w_router, b_router, w_gu, b_gu, w_down, b_down):
    depth = w_ada.shape[0]
    assert depth == 1, "single-layer step"
    bp, sp, d = x_prompt.shape
    bs, ss, _ = x_sample.shape
    tp, tsm = bp * sp, bs * ss
    assert tp % ROUTE_TILE == 0 and tsm % ROUTE_TILE == 0, "token counts must be whole routing tiles"
    past = cache_mla_latent.shape[2]
    layer = lambda a: a.reshape(a.shape[1:])
    wts = _prep_weights(*[layer(a) for a in (g_norm_mix, w_in, g_q_a, w_uq, g_kv_a, w_ukv, g_qk_q, g_qk_k, w_g2, b_g2,
                                             g_gla_out, w_o, g_norm_ffn, w_router, b_router)])
    w_gu, b_gu, w_down, b_down = layer(w_gu), layer(b_gu), layer(w_down), layer(b_down)

    mod = _ada(jnp.concatenate([c_prompt, c_sample], axis=0), layer(w_ada), layer(b_ada)).reshape(bp + bs, 6, d)
    zero_state = jnp.zeros((bp, HEADS // 2, GLA_DV, LANES), F32)
    pr = _mixer(x_prompt, mod[:bp], 0, None, None, zero_state, wts)
    sa = _mixer(x_sample, mod[bp:], past, layer(cache_mla_latent), layer(cache_mla_krope),
                _state_to_pairs(layer(state_gla)), wts)

    idx = jnp.concatenate([pr["idx"], sa["idx"]], axis=1)
    lrank = jnp.concatenate([pr["lrank"], sa["lrank"]], axis=1)
    rt = _route_tables(idx, lrank, jnp.concatenate([pr["cnt"], sa["cnt"]], axis=0))
    ntp = tp // ROUTE_TILE
    lpos, table = rt["lpos"], rt["table"]
    xpad = _scatter(table, rt["tail"], lpos, pr["h2"], sa["h2"], rt["n_blocks"])
    ypad = _experts(rt["blk_e"], rt["n_valid"], rt["nxt_e"], rt["ord_e"], xpad, w_gu, b_gu, w_down, b_down)
    y_p = _combine(table[:ntp], lpos[:, :tp], pr["wt"], pr["x2"], pr["gate_f"], ypad).reshape(bp, sp, d)
    y_s = _combine(table[ntp:], lpos[:, tp:], sa["wt"], sa["x2"], sa["gate_f"], ypad).reshape(bs, ss, d)

    return (y_p, y_s,
            pr["lat"][None], pr["kr"][None], _state_from_pairs(pr["s_fin"])[None],
            sa["lat"][None], sa["kr"][None], _state_from_pairs(sa["s_fin"])[None])
```

```python
import functools

import numpy as np
import jax
import jax.numpy as jnp
from jax import lax
from jax.experimental import pallas as pl
from jax.experimental.pallas import tpu as pltpu

F32 = jnp.float32
BF16 = jnp.bfloat16
I32 = jnp.int32

CHUNK = 64
EPS = 1e-6
HEADS = 4
Q_LORA = 384
KV_LORA = 256
NOPE = 128
ROPE = 64
HALF = ROPE // 2
V_DIM = 128
QK = NOPE + ROPE
QK_PAD = 256
ROPE_THETA = 10000.0
GLA_DK = 64
GLA_DV = 128
GATE_RANK = 16
GATE_NORM = 16.0
N_EXPERTS = 32
TOP_K = 4
SWIGLU_LIMIT = 7.0
SWIGLU_ALPHA = 1.702
NEG = -1e30
LOG2_E = 1.4426950408889634

LANES = 128
SUBLANES = 8
BF16_ROWS = 16
TOKEN_TILE = 512
ADA_COLS = 1536
ROUTE_TILE = 512
RUN_CHUNK = BF16_ROWS
SORT_ROWS = 512
CHUNK_UNROLL = 4
EXPERT_ROWS = 256
ATTN_TILE = 1024
MIB = 1024 * 1024
VMEM_BIG = 56 * MIB
VMEM_MID = 48 * MIB
VMEM_SMALL = 40 * MIB


def _cparams(sem, vmem=None):
    return pltpu.CompilerParams(dimension_semantics=sem, vmem_limit_bytes=vmem)


def _nt(a, b):
    return lax.dot_general(a, b, (((1,), (1,)), ((), ())), preferred_element_type=F32)


def _rms(x, width):
    return lax.rsqrt(jnp.sum(x * x, axis=-1, keepdims=True) * (1.0 / width) + EPS)


def _round_up(x, m):
    return ((x + m - 1) // m) * m


def _ada_body(c_ref, w_ref, b_ref, o_ref):
    c = c_ref[...]
    s = (c * jax.nn.sigmoid(c)).astype(BF16)
    o_ref[...] = jnp.dot(s, w_ref[...].astype(BF16), preferred_element_type=F32) + b_ref[...]


def _ada(c, w_ada, b_ada):
    r, d = c.shape
    n = w_ada.shape[1]
    tn = ADA_COLS if n % ADA_COLS == 0 else n
    return pl.pallas_call(
        _ada_body,
        grid=(n // tn,),
        in_specs=[pl.BlockSpec((r, d), lambda j: (0, 0)),
                  pl.BlockSpec((d, tn), lambda j: (0, j)),
                  pl.BlockSpec((1, tn), lambda j: (0, j))],
        out_specs=pl.BlockSpec((r, tn), lambda j: (0, j)),
        out_shape=jax.ShapeDtypeStruct((r, n), F32),
        compiler_params=_cparams(("arbitrary",), VMEM_SMALL),
        name="ada",
    )(c, w_ada, b_ada.reshape(1, n))


_SEG = dict(qa_kr=(0, 512), kva=(512, 768), gq=(768, 1024), gk=(1024, 1280),
            gv=(1280, 1792), gr=(1792, 2304), glr=(2304, 2432))
_W1_COLS = 2432


def _proj_body(pos0, ts, period, x_ref, sh_ref, sc_ref, gmix_ref, w1_ref, gqa_ref, wuq_ref, gkv_ref, gqk_ref,
               rope_ref, wg2_ref, bg2_ref,
               q_ref, lat_ref, kr_ref, gq_o, gk_o, gv_o, gl_o, gr_o, trig_scr):
    i = pl.program_id(1)
    x = x_ref[...]
    d = x.shape[-1]
    h = (x * _rms(x, d) * gmix_ref[...]) * (1.0 + sc_ref[...]) + sh_ref[...]
    hb = h.astype(BF16)

    def seg(name):
        a, b = _SEG[name]
        return jnp.dot(hb, w1_ref[:, a:b], preferred_element_type=F32)

    @pl.when((pl.program_id(0) == 0) & (i == 0))
    def _():
        row = lax.broadcasted_iota(I32, (ts, LANES), 0)
        if period is not None:
            row = row & (period - 1)
        row_ang = row.astype(F32) * rope_ref[0:1, :]
        trig_scr[0] = jnp.cos(row_ang)
        trig_scr[1] = jnp.sin(row_ang)

    tile_pos = pos0 + (i * ts if period is None else 0 * i)
    base_ang = jnp.broadcast_to(tile_pos.astype(F32) * rope_ref[0:1, :], (SUBLANES, LANES))
    cos_a, sin_a = jnp.cos(base_ang)[0:1, :], jnp.sin(base_ang)[0:1, :]
    cos = cos_a * trig_scr[0] - sin_a * trig_scr[1]
    sin = (sin_a * trig_scr[0] + cos_a * trig_scr[1]) * rope_ref[1:2, :]
    lane = lax.broadcasted_iota(I32, (ts, LANES), 1)
    first_half = (lane & HALF) == 0
    low64 = lane < ROPE

    def rope(v):
        partner = jnp.where(first_half, pltpu.roll(v, LANES - HALF, 1), pltpu.roll(v, HALF, 1))
        return v * cos + partner * sin

    qa_kr = seg("qa_kr")
    qa = qa_kr[:, 0:Q_LORA]
    qn = (qa * _rms(qa, Q_LORA) * gqa_ref[...]).astype(BF16)
    qf = jnp.dot(qn, wuq_ref[...], preferred_element_type=F32)
    rope_blocks = (rope(qf[:, 4 * NOPE:4 * NOPE + LANES]), rope(qf[:, 4 * NOPE + LANES:4 * NOPE + 2 * LANES]))
    for hd in range(HEADS):
        nope = qf[:, NOPE * hd:NOPE * (hd + 1)]
        blk = rope_blocks[hd // 2]
        if hd % 2:
            blk = pltpu.roll(blk, ROPE, 1)
        blk = jnp.where(low64, blk, 0.0)
        ss = jnp.sum(nope * nope, axis=-1, keepdims=True) + jnp.sum(blk * blk, axis=-1, keepdims=True)
        scl = lax.rsqrt(ss * (1.0 / QK) + EPS) * (QK ** -0.5 * LOG2_E)
        q_ref[hd, :, 0:NOPE] = (nope * scl * gqk_ref[0:1, :]).astype(BF16)
        q_ref[hd, :, NOPE:QK_PAD] = (blk * scl * gqk_ref[1:2, :]).astype(BF16)

    kva = seg("kva")
    lat_ref[...] = kva * _rms(kva, KV_LORA) * gkv_ref[...]
    kr_ref[...] = rope(qa_kr[:, Q_LORA:Q_LORA + LANES])[:, 0:ROPE]

    gq_o[...] = seg("gq") * (GLA_DK ** -0.5)
    gk_o[...] = seg("gk")
    gv_o[...] = seg("gv").astype(BF16)
    gr_o[...] = seg("gr")
    z = jnp.dot(seg("glr").astype(BF16), wg2_ref[...], preferred_element_type=F32) + bg2_ref[...]
    gl_o[...] = (jnp.minimum(z, 0.0) - jnp.log1p(jnp.exp(-jnp.abs(z)))) * (1.0 / GATE_NORM)


def _proj(x, shift, scale, pos0, wts, period=None):
    b, s, d = x.shape
    ts = min(s, TOKEN_TILE)
    assert period is None or (period & (period - 1) == 0 and ts % period == 0)
    row = lambda a: pl.BlockSpec(a.shape, lambda bi, i: (0,) * a.ndim)
    tok = lambda w: pl.BlockSpec((None, ts, w), lambda bi, i: (bi, i, 0))
    mod = tok(d) if shift.shape[1] == s and s > 1 else pl.BlockSpec((None, 1, d), lambda bi, i: (bi, 0, 0))
    small = [wts["g_mix"], wts["w1"], wts["g_qa"], wts["w_uq"], wts["g_kv"], wts["g_qk_q"], wts["rope"],
             wts["w_g2"], wts["b_g2"]]
    out_shape = (
        jax.ShapeDtypeStruct((b, HEADS, s, QK_PAD), BF16),
        jax.ShapeDtypeStruct((b, s, KV_LORA), F32),
        jax.ShapeDtypeStruct((b, s, ROPE), F32),
        jax.ShapeDtypeStruct((b, s, HEADS * GLA_DK), F32),
        jax.ShapeDtypeStruct((b, s, HEADS * GLA_DK), F32),
        jax.ShapeDtypeStruct((b, s, HEADS * GLA_DV), BF16),
        jax.ShapeDtypeStruct((b, s, HEADS * GLA_DK), F32),
        jax.ShapeDtypeStruct((b, s, HEADS * GLA_DV), F32),
    )
    out_specs = (
        pl.BlockSpec((None, HEADS, ts, QK_PAD), lambda bi, i: (bi, 0, i, 0)),
        tok(KV_LORA), tok(ROPE), tok(HEADS * GLA_DK), tok(HEADS * GLA_DK), tok(HEADS * GLA_DV),
        tok(HEADS * GLA_DK), tok(HEADS * GLA_DV),
    )
    return pl.pallas_call(
        functools.partial(_proj_body, pos0, ts, period),
        grid=(b, s // ts),
        in_specs=[tok(d), mod, mod] + [row(a) for a in small],
        out_specs=out_specs,
        out_shape=out_shape,
        scratch_shapes=[pltpu.VMEM((2, ts, LANES), F32)],
        compiler_params=_cparams(("arbitrary", "arbitrary"), VMEM_BIG),
        name="proj",
    )(x, shift, scale, *small)


def _key_rows(lat, kr, wk_ref, gk_ref, k_out):
    kn_all = jnp.dot(lat, wk_ref[...], preferred_element_type=F32)
    kr_ss = jnp.sum(kr * kr, axis=-1, keepdims=True)
    for hd in range(HEADS):
        kn = kn_all[:, NOPE * hd:NOPE * (hd + 1)]
        scl = lax.rsqrt((jnp.sum(kn * kn, axis=-1, keepdims=True) + kr_ss) * (1.0 / QK) + EPS)
        k_out[hd, :, 0:NOPE] = (kn * scl * gk_ref[0:1, :]).astype(BF16)
        k_out[hd, :, NOPE:QK] = (kr * scl * gk_ref[1:2, 0:ROPE]).astype(BF16)
        k_out[hd, :, QK:QK_PAD] = jnp.zeros((kr.shape[0], QK_PAD - QK), BF16)


def _kv_body(lat_ref, kr_ref, wk_ref, wv_ref, gk_ref, k_ref, v_ref):
    lat = lat_ref[...].astype(BF16)
    _key_rows(lat, kr_ref[...], wk_ref, gk_ref, k_ref)
    v_t = _nt(wv_ref[...], lat)
    for hd in range(HEADS):
        v_ref[hd] = v_t[V_DIM * hd:V_DIM * (hd + 1), :].astype(BF16)


def _kv(lat, kr, w_uk, w_uv_t, g_qk_k):
    b, s, _ = lat.shape
    ts = min(s, ATTN_TILE)
    return pl.pallas_call(
        _kv_body,
        grid=(b, s // ts),
        in_specs=[pl.BlockSpec((None, ts, KV_LORA), lambda bi, i: (bi, i, 0)),
                  pl.BlockSpec((None, ts, ROPE), lambda bi, i: (bi, i, 0)),
                  pl.BlockSpec(w_uk.shape, lambda bi, i: (0, 0)),
                  pl.BlockSpec(w_uv_t.shape, lambda bi, i: (0, 0)),
                  pl.BlockSpec(g_qk_k.shape, lambda bi, i: (0, 0))],
        out_specs=(pl.BlockSpec((None, HEADS, ts, QK_PAD), lambda bi, i: (bi, 0, i, 0)),
                   pl.BlockSpec((None, HEADS, None, V_DIM, ts), lambda bi, i: (bi, 0, i, 0, 0))),
        out_shape=(jax.ShapeDtypeStruct((b, HEADS, s, QK_PAD), BF16),
                   jax.ShapeDtypeStruct((b, HEADS, s // ts, V_DIM, ts), BF16)),
        compiler_params=_cparams(("arbitrary", "arbitrary")),
        name="kv",
    )(lat, kr, w_uk, w_uv_t, g_qk_k)


def _attn_prompt_body(t, q_ref, qn_ref, k_ref, vt_ref, o_ref, s_a, s_b):
    i = pl.program_id(2)

    def scores(q, j, buf):
        buf[...] = _nt(k_ref[pl.ds(pl.multiple_of(j * t, t), t), :], q)

    def consume(j, buf, carry, masked=False):
        m, l, acc = carry
        s = buf[...]
        if masked:
            visible = (lax.broadcasted_iota(I32, (t, t), 0) // CHUNK) <= (lax.broadcasted_iota(I32, (t, t), 1) // CHUNK)
            s = jnp.where(visible, s, NEG)
        m_new = jnp.maximum(m, jnp.max(s, axis=0, keepdims=True))
        alpha = jnp.exp2(m - m_new)
        p = jnp.exp2(s - m_new)
        l = alpha * l + jnp.sum(p, axis=0, keepdims=True)
        acc = alpha * acc + jnp.dot(vt_ref[j], p.astype(BF16), preferred_element_type=F32)
        return m_new, l, acc

    def run(first, second):
        q = q_ref[...]

        @pl.when(i == 0)
        def _():
            scores(q, 0, first)

        def pair(pp, carry):
            j = 2 * pp
            scores(q, j + 1, second)
            carry = consume(j, first, carry)
            scores(q, j + 2, first)
            return consume(j + 1, second, carry)

        def even_tail(carry):
            scores(qn_ref[...], 0, second)
            return consume(i, first, carry, masked=True)

        def odd_tail(carry):
            scores(q, i, second)
            carry = consume(i - 1, first, carry)
            scores(qn_ref[...], 0, first)
            return consume(i, second, carry, masked=True)

        carry = (jnp.full((1, t), NEG, F32), jnp.zeros((1, t), F32), jnp.zeros((V_DIM, t), F32))
        carry = lax.fori_loop(0, i // 2, pair, carry)
        _, l, acc = lax.cond(i % 2 == 1, odd_tail, even_tail, carry)
        o_ref[...] = (acc / l).T.astype(BF16)

    @pl.when(((i + 1) // 2) % 2 == 0)
    def _():
        run(s_a, s_b)

    @pl.when(((i + 1) // 2) % 2 == 1)
    def _():
        run(s_b, s_a)


def _attn_prompt(q, k, v_t):
    b, _, s, _ = q.shape
    t = v_t.shape[-1]
    nq = s // t
    return pl.pallas_call(
        functools.partial(_attn_prompt_body, t),
        grid=(b, HEADS, nq),
        in_specs=[pl.BlockSpec((None, None, t, QK_PAD), lambda bi, h, i: (bi, h, i, 0)),
                  pl.BlockSpec((None, None, t, QK_PAD), lambda bi, h, i: (bi, h, jnp.minimum(i + 1, nq - 1), 0)),
                  pl.BlockSpec((None, None, s, QK_PAD), lambda bi, h, i: (bi, h, 0, 0)),
                  pl.BlockSpec((None, None, nq, V_DIM, t), lambda bi, h, i: (bi, h, 0, 0, 0))],
        out_specs=pl.BlockSpec((None, t, V_DIM), lambda bi, h, i: (bi, i, h)),
        out_shape=jax.ShapeDtypeStruct((b, s, HEADS * V_DIM), BF16),
        scratch_shapes=[pltpu.VMEM((t, t), F32), pltpu.VMEM((t, t), F32)],
        compiler_params=_cparams(("arbitrary", "arbitrary", "arbitrary"), VMEM_BIG),
        name="attn_prompt",
    )(q, q, k, v_t)


def _attn_sample_body(past, sq, q_ref, plat_ref, pkr_ref, nlat_ref, nkr_ref, wk_ref, wv_ref, gk_ref, o_ref,
                      kp_scr, kn_scr):
    plat = plat_ref[...].astype(BF16)
    nlat = nlat_ref[...].astype(BF16)
    _key_rows(plat, pkr_ref[...], wk_ref, gk_ref, kp_scr)
    _key_rows(nlat, nkr_ref[...], wk_ref, gk_ref, kn_scr)
    vp_t = _nt(wv_ref[...], plat).astype(BF16)
    vn_t = _nt(wv_ref[...], nlat).astype(BF16)
    hq = HEADS * sq
    col = lax.broadcasted_iota(I32, (sq, hq), 1)
    key_chunk = (past + lax.broadcasted_iota(I32, (sq, hq), 0)) // CHUNK
    qry_chunk = (past + col % sq) // CHUNK
    qrow = lax.broadcasted_iota(I32, (hq, 1), 0) // sq
    q_all = jnp.concatenate([q_ref[hd] for hd in range(HEADS)], axis=0)
    s_p = jnp.zeros((past, hq), F32)
    s_n = jnp.zeros((sq, hq), F32)
    for hd in range(HEADS):
        q_h = jnp.where(qrow == hd, q_all, jnp.zeros((), BF16))
        s_p = s_p + _nt(kp_scr[hd], q_h)
        s_n = s_n + _nt(kn_scr[hd], q_h)
    s_n = jnp.where(key_chunk <= qry_chunk, s_n, NEG)
    m = jnp.maximum(jnp.max(s_p, axis=0, keepdims=True), jnp.max(s_n, axis=0, keepdims=True))
    p_p = jnp.exp2(s_p - m)
    p_n = jnp.exp2(s_n - m)
    inv_l = 1.0 / (jnp.sum(p_p, axis=0, keepdims=True) + jnp.sum(p_n, axis=0, keepdims=True))
    p_p = p_p.astype(BF16)
    p_n = p_n.astype(BF16)
    lane_head = lax.broadcasted_iota(I32, (V_DIM, hq), 1) // sq
    o_t = jnp.zeros((V_DIM, hq), F32)
    for hd in range(HEADS):
        rows = slice(V_DIM * hd, V_DIM * (hd + 1))
        o_h = (jnp.dot(vp_t[rows, :], p_p, preferred_element_type=F32)
               + jnp.dot(vn_t[rows, :], p_n, preferred_element_type=F32))
        o_t = o_t + jnp.where(lane_head == hd, o_h, 0.0)
    o_all = (o_t * inv_l).T
    for hd in range(HEADS):
        o_ref[:, V_DIM * hd:V_DIM * (hd + 1)] = o_all[sq * hd:sq * (hd + 1), :].astype(BF16)


def _attn_sample(q, past_lat, past_kr, lat, kr, w_uk, w_uv_t, g_qk_k):
    b, _, sq, _ = q.shape
    past = past_lat.shape[1]
    rows = lambda n, w: pl.BlockSpec((None, n, w), lambda bi: (bi, 0, 0))
    full = lambda a: pl.BlockSpec(a.shape, lambda bi: (0,) * a.ndim)
    return pl.pallas_call(
        functools.partial(_attn_sample_body, past, sq),
        grid=(b,),
        in_specs=[pl.BlockSpec((None, HEADS, sq, QK_PAD), lambda bi: (bi, 0, 0, 0)),
                  rows(past, KV_LORA), rows(past, ROPE), rows(sq, KV_LORA), rows(sq, ROPE),
                  full(w_uk), full(w_uv_t), full(g_qk_k)],
        out_specs=rows(sq, HEADS * V_DIM),
        out_shape=jax.ShapeDtypeStruct((b, sq, HEADS * V_DIM), BF16),
        scratch_shapes=[pltpu.VMEM((HEADS, past, QK_PAD), BF16), pltpu.VMEM((HEADS, sq, QK_PAD), BF16)],
        compiler_params=_cparams(("arbitrary",), VMEM_SMALL),
        name="attn_sample",
    )(q, past_lat, past_kr, lat, kr, w_uk, w_uv_t, g_qk_k)


def _gla_masks(c, rows):
    idx = np.arange(rows)
    same = (idx // c)[:, None] == (idx // c)[None, :]
    le = same & (idx[None, :] <= idx[:, None])
    gt = same & (idx[None, :] > idx[:, None])
    return np.concatenate([le, gt], axis=0).astype(np.float32), int(np.log2(c))


def _level_exponents(b, g, c, level):
    n = c >> level
    rows = b.shape[0]
    row = lax.broadcasted_iota(I32, (rows, 1), 0)
    if n >= 8:
        split = b.reshape(rows // n, n, LANES)[:, n // 2 - 1:n // 2, :]
        split = jnp.broadcast_to(split, (rows // n, n, LANES)).reshape(rows, LANES)
        return jnp.where((row & (n // 2)) != 0, b - split, split - b)
    g_prev = pltpu.roll(g, 1, 0)
    g_next = pltpu.roll(g, rows - 1, 0)
    if n == 4:
        r = row & 3
        return jnp.where(r == 0, g_next, jnp.where(r == 1, 0.0, jnp.where(r == 2, g, g + g_prev)))
    assert n == 2
    return jnp.where((row & 1) != 0, g, 0.0)


def _gla_body(c, n_chunks, unit, levels, mall_ref, q_ref, k_ref, g_ref, v_ref, r_ref, s0_ref, gout_ref,
              o_ref, sfin_ref, st_scr):
    it = pl.program_id(1)

    @pl.when(it == 0)
    def _():
        st_scr[...] = s0_ref[...]

    ru = unit * c
    lane = lax.broadcasted_iota(I32, (ru, LANES), 1)
    head_lanes = (lane < GLA_DK, lane >= GLA_DK)
    st_lane_lo = lax.broadcasted_iota(I32, (GLA_DV, LANES), 1) < GLA_DK
    row = lax.broadcasted_iota(I32, (ru, 1), 0)
    ri = lax.broadcasted_iota(I32, (ru, ru), 0)
    ci = lax.broadcasted_iota(I32, (ru, ru), 1)
    mall = mall_ref[...]

    for un in range(n_chunks // unit):
        rows = slice(un * ru, (un + 1) * ru)
        for p in range(HEADS // 2):
            ls = slice(LANES * p, LANES * (p + 1))
            g = g_ref[rows, ls]
            q = q_ref[rows, ls]
            k = k_ref[rows, ls]
            g_hi = g.astype(BF16)
            g_lo = (g - g_hi.astype(F32)).astype(BF16)
            e2 = jnp.dot(mall, jnp.concatenate([g_hi, g_lo], axis=1), preferred_element_type=F32)
            e = e2[:, 0:LANES] + e2[:, LANES:2 * LANES]
            b = e[0:ru]
            eb = jnp.exp(b)
            qb = q * eb
            kd = (k * jnp.exp(e[ru:2 * ru])).astype(BF16)
            qs, ks = [q], [k.astype(BF16)]
            for l in range(levels):
                bottom = (row & (c >> (l + 1))) != 0
                decay = jnp.exp(_level_exponents(b, g, c, l))
                qs.append(jnp.where(bottom, q * decay, 0.0))
                ks.append(jnp.where(bottom, 0.0, k * decay).astype(BF16))
            states = [st_scr[p]]
            for j in range(unit):
                cr = slice(j * c, (j + 1) * c)
                upd = [lax.dot_general(v_ref[rows, GLA_DV * (2 * p + hh):GLA_DV * (2 * p + hh + 1)][cr, :], kd[cr, :],
                                       (((0,), (0,)), ((), ())), preferred_element_type=F32) for hh in range(2)]
                d_last = eb[j * c + c - 1:j * c + c, :]
                states.append(states[-1] * d_last + jnp.where(st_lane_lo, upd[0], upd[1]))
            st_scr[p] = states[-1]
            for hh in range(2):
                hd = 2 * p + hh
                sel = head_lanes[hh]
                a = jnp.where(ri == ci, _nt(jnp.where(sel, qs[0], 0.0).astype(BF16), ks[0]), 0.0)
                for l in range(levels):
                    pr = _nt(jnp.where(sel, qs[l + 1], 0.0).astype(BF16), ks[l + 1])
                    a = a + jnp.where((ri ^ ci) < (c >> l), pr, 0.0)
                vh = v_ref[rows, GLA_DV * hd:GLA_DV * (hd + 1)]
                qb_h = jnp.where(sel, qb, 0.0).astype(BF16)
                o_state = [_nt(qb_h[j * c:(j + 1) * c, :], states[j].astype(BF16)) for j in range(unit)]
                o = jnp.dot(a.astype(BF16), vh, preferred_element_type=F32) + jnp.concatenate(o_state, axis=0)
                on = o * _rms(o, GLA_DV) * gout_ref[...]
                r = r_ref[rows, GLA_DV * hd:GLA_DV * (hd + 1)]
                o_ref[rows, GLA_DV * hd:GLA_DV * (hd + 1)] = (on * (r * jax.nn.sigmoid(r))).astype(BF16)

    @pl.when(it == pl.num_programs(1) - 1)
    def _():
        sfin_ref[...] = st_scr[...]


def _gla(gq, gk, gl, gv, gr, s0, g_out):
    b, s, _ = gq.shape
    c = min(CHUNK, s)
    tile = min(s, 8 * c)
    unit = next(u for u in (4, 2, 1) if (tile // c) % u == 0)
    masks, levels = _gla_masks(c, unit * c)
    mall = jnp.asarray(masks, BF16)
    tok = lambda w: pl.BlockSpec((None, tile, w), lambda bi, i: (bi, i, 0))
    st_spec = pl.BlockSpec((None, HEADS // 2, GLA_DV, LANES), lambda bi, i: (bi, 0, 0, 0))
    return pl.pallas_call(
        functools.partial(_gla_body, c, tile // c, unit, levels),
        grid=(b, s // tile),
        in_specs=[pl.BlockSpec(mall.shape, lambda bi, i: (0, 0)),
                  tok(HEADS * GLA_DK), tok(HEADS * GLA_DK), tok(HEADS * GLA_DK), tok(HEADS * GLA_DV),
                  tok(HEADS * GLA_DV), st_spec, pl.BlockSpec(g_out.shape, lambda bi, i: (0, 0))],
        out_specs=(tok(HEADS * GLA_DV), st_spec),
        out_shape=(jax.ShapeDtypeStruct((b, s, HEADS * GLA_DV), BF16),
                   jax.ShapeDtypeStruct((b, HEADS // 2, GLA_DV, LANES), F32)),
        scratch_shapes=[pltpu.VMEM((HEADS // 2, GLA_DV, LANES), F32)],
        compiler_params=_cparams(("arbitrary", "arbitrary")),
        name="gla",
    )(mall, gq, gk, gl, gv, gr, s0, g_out)


def _state_to_pairs(s):
    b = s.shape[0]
    s = s.reshape(b, HEADS // 2, 2, GLA_DK, GLA_DV)
    return jnp.transpose(s, (0, 1, 4, 2, 3)).reshape(b, HEADS // 2, GLA_DV, 2 * GLA_DK)


def _state_from_pairs(s):
    b = s.shape[0]
    s = s.reshape(b, HEADS // 2, GLA_DV, 2, GLA_DK)
    return jnp.transpose(s, (0, 1, 3, 4, 2)).reshape(b, HEADS, GLA_DK, GLA_DV)


def _post_body(x_ref, om_ref, og_ref, gt_ref, sc_ref, sh_ref, wo_ref, gffn_ref, wr_ref, br_ref,
               x2_ref, h_ref, idx_ref, wt_ref, rank_ref, cnt_ref):
    half = om_ref.shape[-1]
    mix = (jnp.dot(om_ref[...], wo_ref[0:half, :], preferred_element_type=F32)
           + jnp.dot(og_ref[...], wo_ref[half:2 * half, :], preferred_element_type=F32))
    x2 = x_ref[...] + gt_ref[...] * mix
    x2_ref[...] = x2
    d = x2.shape[-1]
    h = (x2 * _rms(x2, d) * gffn_ref[...]) * (1.0 + sc_ref[...]) + sh_ref[...]
    h_hi = h.astype(BF16)
    h_ref[...] = h_hi
    h_lo = (h - h_hi.astype(F32)).astype(BF16)
    logits = _nt(wr_ref[0], h_hi) + _nt(wr_ref[0], h_lo) + _nt(wr_ref[1], h_hi) + br_ref[...]
    n_exp, tm = logits.shape
    eid = lax.broadcasted_iota(I32, (n_exp, tm), 0)
    vals, tops, ids = logits, [], []
    for _ in range(TOP_K):
        m = jnp.max(vals, axis=0, keepdims=True)
        sel = jnp.min(jnp.where(vals == m, eid, n_exp), axis=0, keepdims=True)
        tops.append(m)
        ids.append(sel)
        vals = jnp.where(eid == sel, -jnp.inf, vals)
    es = [jnp.exp(t - tops[0]) for t in tops]
    tot = es[0] + es[1] + es[2] + es[3]
    idx_ref[...] = jnp.concatenate(ids, axis=0)
    wt_ref[...] = jnp.concatenate([e / tot for e in es], axis=0)
    hits = [eid == sel for sel in ids]
    member = jnp.zeros((n_exp, tm), F32)
    for hk in hits:
        member = member + jnp.where(hk, 1.0, 0.0)
    before = lax.broadcasted_iota(I32, (tm, tm), 0) < lax.broadcasted_iota(I32, (tm, tm), 1)
    prefix = jnp.dot(member.astype(BF16), jnp.where(before, 1.0, 0.0).astype(BF16), preferred_element_type=F32)
    rank_ref[...] = jnp.concatenate(
        [jnp.sum(jnp.where(hk, prefix, 0.0), axis=0, keepdims=True) for hk in hits], axis=0).astype(I32)
    cnt_ref[...] = jnp.broadcast_to(jnp.sum(member, axis=1, keepdims=True), (n_exp, LANES)).astype(I32)


def _post(x, om, og, gate, scale, shift, w_o, g_ffn, w_r2, b_r):
    t, d = x.shape
    tm = ROUTE_TILE
    per_tok = gate.shape[0] == t
    mod = pl.BlockSpec((tm, d), lambda i: (i, 0)) if per_tok else pl.BlockSpec((1, d), lambda i: (0, 0))
    tok = lambda w: pl.BlockSpec((tm, w), lambda i: (i, 0))
    full = lambda a: pl.BlockSpec(a.shape, lambda i: (0,) * a.ndim)
    return pl.pallas_call(
        _post_body,
        grid=(t // tm,),
        in_specs=[tok(d), tok(om.shape[1]), tok(og.shape[1]), mod, mod, mod, full(w_o), full(g_ffn), full(w_r2),
                  full(b_r)],
        out_specs=(tok(d), tok(d),
                   pl.BlockSpec((TOP_K, tm), lambda i: (0, i)), pl.BlockSpec((TOP_K, tm), lambda i: (0, i)),
                   pl.BlockSpec((TOP_K, tm), lambda i: (0, i)),
                   pl.BlockSpec((None, N_EXPERTS, LANES), lambda i: (i, 0, 0))),
        out_shape=(jax.ShapeDtypeStruct((t, d), F32), jax.ShapeDtypeStruct((t, d), BF16),
                   jax.ShapeDtypeStruct((TOP_K, t), I32), jax.ShapeDtypeStruct((TOP_K, t), F32),
                   jax.ShapeDtypeStruct((TOP_K, t), I32), jax.ShapeDtypeStruct((t // tm, N_EXPERTS, LANES), I32)),
        compiler_params=_cparams(("arbitrary",), VMEM_SMALL),
        name="post",
    )(x, om, og, gate, scale, shift, w_o, g_ffn, w_r2, b_r)


def _route_tables(idx, lrank, cnt3):
    nt = cnt3.shape[0]
    t = idx.shape[1]
    cnt = cnt3[:, :, 0]
    run = _round_up(cnt, RUN_CHUNK)
    lo_end = jnp.cumsum(run, axis=1)
    lo = lo_end - run
    n_chunks = lo_end[:, -1] // RUN_CHUNK
    region = _round_up(jnp.sum(run, axis=0), EXPERT_ROWS)
    g_end = jnp.cumsum(region)
    run_dest = (g_end - region)[None, :] + jnp.cumsum(run, axis=0) - run
    max_rows = TOP_K * ROUTE_TILE + N_EXPERTS * (RUN_CHUNK - 1)
    n_tab = _round_up(max_rows, RUN_CHUNK) // RUN_CHUNK
    c_start = jnp.arange(n_tab, dtype=I32) * RUN_CHUNK
    e_of_c = jnp.minimum(jnp.sum(lo_end[:, None, :] <= c_start[None, :, None], axis=2), N_EXPERTS - 1)
    pick = e_of_c[:, :, None] == jnp.arange(N_EXPERTS, dtype=I32)[None, None, :]
    chunk_dest = jnp.sum(jnp.where(pick, (run_dest - lo)[:, None, :], 0), axis=2) + c_start[None, :]
    table = jnp.concatenate([chunk_dest, n_chunks[:, None]], axis=1).astype(I32).reshape(nt, 1, n_tab + 1)
    eid = jnp.arange(N_EXPERTS, dtype=I32)[:, None]
    lo_tok = jnp.repeat(lo.T, ROUTE_TILE, axis=1)
    lpos = jnp.stack([jnp.sum(jnp.where(idx[k][None, :] == eid, lo_tok, 0), axis=0) for k in range(TOP_K)])
    lpos = (lpos + lrank).astype(I32)
    n_blocks = _round_up(t * TOP_K + nt * N_EXPERTS * (RUN_CHUNK - 1), EXPERT_ROWS) // EXPERT_ROWS + N_EXPERTS
    b_start = jnp.arange(n_blocks, dtype=I32) * EXPERT_ROWS
    blk_e = jnp.minimum(jnp.sum(g_end[None, :] <= b_start[:, None], axis=1), N_EXPERTS - 1).astype(I32)
    n_valid = (g_end[-1:] // EXPERT_ROWS).astype(I32)
    used = region > 0
    e_ids = jnp.arange(N_EXPERTS, dtype=I32)
    later_used = used[None, :] & (e_ids[None, :] > e_ids[:, None])
    nxt_e = jnp.min(jnp.where(later_used, e_ids[None, :], N_EXPERTS), axis=1)
    nxt_e = jnp.where(nxt_e < N_EXPERTS, nxt_e, -1).astype(I32)
    ord_e = (jnp.cumsum(used.astype(I32)) - 1).astype(I32)
    tail = jnp.concatenate([jnp.where(used, g_end - EXPERT_ROWS, -1), n_valid]).astype(I32)
    tail = tail.reshape(1, N_EXPERTS + 1)
    return dict(table=table, lpos=lpos, blk_e=blk_e, n_valid=n_valid, nxt_e=nxt_e, ord_e=ord_e, tail=tail,
                n_blocks=n_blocks)


def _chunk_copy(src, dst, sem):
    return pltpu.make_async_copy(src, dst, sem)


def _for_row_blocks(n_rows, body):
    full = n_rows // SORT_ROWS
    rem = n_rows - full * SORT_ROWS
    tail = pl.multiple_of(full * SORT_ROWS, SORT_ROWS)

    def whole(rb, carry):
        body(pl.multiple_of(rb * SORT_ROWS, SORT_ROWS), SORT_ROWS)
        return carry

    lax.fori_loop(0, full, whole, 0)

    @pl.when(rem > SORT_ROWS // 2)
    def _():
        body(tail, SORT_ROWS)

    @pl.when((rem > 0) & (rem <= SORT_ROWS // 2))
    def _():
        body(tail, SORT_ROWS // 2)


def _for_chunks(n, body):
    groups = n // CHUNK_UNROLL

    def group(g, carry):
        for u in range(CHUNK_UNROLL):
            body(g * CHUNK_UNROLL + u)
        return carry

    def single(c, carry):
        body(c)
        return carry

    lax.fori_loop(0, groups, group, 0)
    lax.fori_loop(groups * CHUNK_UNROLL, n, single, 0)


def _scatter_body(n_tab, nt_a, n_blocks, tab_ref, prv_ref, tail_ref, lpos_ref, ha_ref, hb_ref, xout_ref, sorted_scr,
                  zero_scr, sems, zero_sem):
    i = pl.program_id(0)
    slot = i % 2
    n_chunks = tab_ref[0, n_tab]
    tt = ha_ref.shape[0]

    @pl.when(i == 0)
    def _():
        zero_scr[...] = jnp.zeros_like(zero_scr)
        n_valid = tail_ref[0, N_EXPERTS]

        def block(start):
            return xout_ref.at[pl.ds(pl.multiple_of(start, EXPERT_ROWS), EXPERT_ROWS)]

        for e in range(N_EXPERTS):
            @pl.when(tail_ref[0, e] >= 0)
            def _():
                _chunk_copy(zero_scr, block(tail_ref[0, e]), zero_sem).start()

        def fill(b, carry):
            _chunk_copy(zero_scr, block(b * EXPERT_ROWS), zero_sem).start()
            return carry

        def fill_done(b, carry):
            _chunk_copy(zero_scr, block(0), zero_sem).wait()
            return carry

        lax.fori_loop(n_valid, n_blocks, fill, 0)
        for e in range(N_EXPERTS):
            @pl.when(tail_ref[0, e] >= 0)
            def _():
                _chunk_copy(zero_scr, block(0), zero_sem).wait()
        lax.fori_loop(n_valid, n_blocks, fill_done, 0)

    lp16 = lpos_ref[...].astype(jnp.int16)
    h = jnp.where(i < nt_a, ha_ref[...], hb_ref[...])

    def sort_block(r0, size):
        rid = (r0 + lax.broadcasted_iota(I32, (size, tt), 0)).astype(jnp.int16)
        onehot = jnp.zeros((size, tt), BF16)
        for k in range(TOP_K):
            onehot = onehot + jnp.where(lp16[k:k + 1, :] == rid, jnp.ones((), BF16), jnp.zeros((), BF16))
        sorted_scr[slot, pl.ds(r0, size), :] = jnp.dot(onehot, h, preferred_element_type=F32).astype(BF16)

    _for_row_blocks(n_chunks * RUN_CHUNK, sort_block)

    def issue(c):
        src = sorted_scr.at[slot, pl.ds(pl.multiple_of(c * RUN_CHUNK, RUN_CHUNK), RUN_CHUNK)]
        dst = xout_ref.at[pl.ds(pl.multiple_of(tab_ref[0, c], RUN_CHUNK), RUN_CHUNK)]
        _chunk_copy(src, dst, sems.at[slot]).start()

    def drain(sl):
        def body(c):
            _chunk_copy(sorted_scr.at[sl, pl.ds(0, RUN_CHUNK)], xout_ref.at[pl.ds(0, RUN_CHUNK)], sems.at[sl]).wait()
        return body

    _for_chunks(n_chunks, issue)

    @pl.when(i > 0)
    def _():
        _for_chunks(prv_ref[0, n_tab], drain(1 - slot))

    @pl.when(i == pl.num_programs(0) - 1)
    def _():
        _for_chunks(n_chunks, drain(slot))


def _scatter(table, tail, lpos, h_a, h_b, n_blocks):
    d = h_a.shape[1]
    nt_a, nt_b = h_a.shape[0] // ROUTE_TILE, h_b.shape[0] // ROUTE_TILE
    n_tab = table.shape[2] - 1
    sort_cap = _round_up(n_tab * RUN_CHUNK, SORT_ROWS)
    tab_spec = lambda f: pl.BlockSpec((None, 1, n_tab + 1), lambda i: (f(i), 0, 0), memory_space=pltpu.SMEM)
    return pl.pallas_call(
        functools.partial(_scatter_body, n_tab, nt_a, n_blocks),
        grid=(nt_a + nt_b,),
        in_specs=[tab_spec(lambda i: i), tab_spec(lambda i: jnp.maximum(i - 1, 0)),
                  pl.BlockSpec((1, N_EXPERTS + 1), lambda i: (0, 0), memory_space=pltpu.SMEM),
                  pl.BlockSpec((TOP_K, ROUTE_TILE), lambda i: (0, i)),
                  pl.BlockSpec((ROUTE_TILE, d), lambda i: (jnp.minimum(i, nt_a - 1), 0)),
                  pl.BlockSpec((ROUTE_TILE, d), lambda i: (jnp.maximum(i - nt_a, 0), 0))],
        out_specs=pl.BlockSpec(memory_space=pl.ANY),
        out_shape=jax.ShapeDtypeStruct((n_blocks * EXPERT_ROWS, d), BF16),
        scratch_shapes=[pltpu.VMEM((2, sort_cap, d), BF16), pltpu.VMEM((EXPERT_ROWS, d), BF16),
                        pltpu.SemaphoreType.DMA((2,)), pltpu.SemaphoreType.DMA(())],
        compiler_params=_cparams(("arbitrary",), VMEM_MID),
        name="scatter",
    )(table, table, tail, lpos, h_a, h_b)


def _experts_body(be_ref, nv_ref, nxt_ref, ord_ref, x_ref, wgu_hbm, bgu_ref, wd_hbm, bd_ref, y_ref,
                  wgu_f, wd_f, wgu_s, wd_s, sem_gu, sem_d):
    b = pl.program_id(0)
    e = be_ref[b]
    prev = be_ref[jnp.maximum(b - 1, 0)]
    valid = b < nv_ref[0]
    d_ff = wd_s.shape[0]
    slot = ord_ref[e] % 2

    def weights(expert, sl):
        return (pltpu.make_async_copy(wgu_hbm.at[expert], wgu_f.at[sl], sem_gu.at[sl]),
                pltpu.make_async_copy(wd_hbm.at[expert], wd_f.at[sl], sem_d.at[sl]))

    @pl.when(valid & ((b == 0) | (e != prev)))
    def _():
        @pl.when(b == 0)
        def _():
            for cp in weights(e, slot):
                cp.start()

        for cp in weights(e, slot):
            cp.wait()

        @pl.when(nxt_ref[e] >= 0)
        def _():
            for cp in weights(nxt_ref[e], 1 - slot):
                cp.start()

        wgu_s[...] = wgu_f[slot].astype(BF16)
        wd_s[...] = wd_f[slot].astype(BF16)

    @pl.when(valid)
    def _():
        gu = jnp.dot(x_ref[...], wgu_s[...], preferred_element_type=F32) + bgu_ref[...]
        gate = jnp.minimum(gu[:, 0:d_ff], SWIGLU_LIMIT)
        up = jnp.clip(gu[:, d_ff:2 * d_ff], -SWIGLU_LIMIT, SWIGLU_LIMIT)
        act = ((up + 1.0) * (gate * jax.nn.sigmoid(gate * SWIGLU_ALPHA))).astype(BF16)
        y_ref[...] = (jnp.dot(act, wd_s[...], preferred_element_type=F32) + bd_ref[...]).astype(BF16)

    @pl.when(jnp.logical_not(valid))
    def _():
        y_ref[...] = jnp.zeros_like(y_ref)


def _experts(blk_e, n_valid, nxt_e, ord_e, xpad, w_gu, b_gu, w_down, b_down):
    m, d = xpad.shape
    nb = m // EXPERT_ROWS
    n_exp, _, f2 = w_gu.shape
    d_ff = w_down.shape[1]
    last = lambda b, be, nv: jnp.minimum(b, nv[0] - 1)
    grid_spec = pltpu.PrefetchScalarGridSpec(
        num_scalar_prefetch=4,
        grid=(nb,),
        in_specs=[pl.BlockSpec((EXPERT_ROWS, d), lambda b, be, nv, nx, od: (last(b, be, nv), 0)),
                  pl.BlockSpec(memory_space=pl.ANY),
                  pl.BlockSpec((None, 1, f2), lambda b, be, nv, nx, od: (be[last(b, be, nv)], 0, 0)),
                  pl.BlockSpec(memory_space=pl.ANY),
                  pl.BlockSpec((None, 1, d), lambda b, be, nv, nx, od: (be[last(b, be, nv)], 0, 0))],
        out_specs=pl.BlockSpec((EXPERT_ROWS, d), lambda b, be, nv, nx, od: (b, 0)),
        scratch_shapes=[pltpu.VMEM((2, d, f2), F32), pltpu.VMEM((2, d_ff, d), F32),
                        pltpu.VMEM((d, f2), BF16), pltpu.VMEM((d_ff, d), BF16),
                        pltpu.SemaphoreType.DMA((2,)), pltpu.SemaphoreType.DMA((2,))],
    )
    return pl.pallas_call(
        _experts_body,
        grid_spec=grid_spec,
        out_shape=jax.ShapeDtypeStruct((m, d), BF16),
        compiler_params=_cparams(("arbitrary",), VMEM_BIG),
        name="experts",
    )(blk_e, n_valid, nxt_e, ord_e, xpad, w_gu, b_gu.reshape(n_exp, 1, f2), w_down, b_down.reshape(n_exp, 1, d))


def _combine_body(n_tab, tab_ref, nxt_ref, lpos_ref, wt_ref, x2_ref, gt_ref, y_ref, o_ref, ysort_scr, sems):
    i = pl.program_id(0)
    slot = i % 2
    n_chunks = tab_ref[0, n_tab]
    tt = x2_ref.shape[0]

    def fetch(tab, sl):
        def issue(c):
            src = y_ref.at[pl.ds(pl.multiple_of(tab[0, c], RUN_CHUNK), RUN_CHUNK)]
            dst = ysort_scr.at[sl, pl.ds(pl.multiple_of(c * RUN_CHUNK, RUN_CHUNK), RUN_CHUNK)]
            _chunk_copy(src, dst, sems.at[sl]).start()

        _for_chunks(tab[0, n_tab], issue)

    @pl.when(i == 0)
    def _():
        ysort_scr[...] = jnp.zeros_like(ysort_scr)
        fetch(tab_ref, 0)

    @pl.when(i + 1 < pl.num_programs(0))
    def _():
        fetch(nxt_ref, 1 - slot)

    def drain(c):
        _chunk_copy(y_ref.at[pl.ds(0, RUN_CHUNK)], ysort_scr.at[slot, pl.ds(0, RUN_CHUNK)], sems.at[slot]).wait()

    _for_chunks(n_chunks, drain)

    lp16 = lpos_ref[...].astype(jnp.int16)
    w = wt_ref[...].astype(BF16)
    o_ref[...] = x2_ref[...]
    gt = gt_ref[...]

    def gather_block(r0, size):
        rid = (r0 + lax.broadcasted_iota(I32, (size, tt), 0)).astype(jnp.int16)
        pw_t = jnp.zeros((size, tt), BF16)
        for k in range(TOP_K):
            pw_t = pw_t + jnp.where(lp16[k:k + 1, :] == rid, w[k:k + 1, :], jnp.zeros((), BF16))
        ys = ysort_scr[slot, pl.ds(r0, size), :]
        part = lax.dot_general(pw_t, ys, (((0,), (0,)), ((), ())), preferred_element_type=F32)
        o_ref[...] = o_ref[...] + gt * part

    _for_row_blocks(n_chunks * RUN_CHUNK, gather_block)


def _combine(table, lpos, wts, x2, gate, ypad):
    t, d = x2.shape
    nt = t // ROUTE_TILE
    n_tab = table.shape[2] - 1
    per_tok = gate.shape[0] == t
    mod = pl.BlockSpec((ROUTE_TILE, d), lambda i: (i, 0)) if per_tok else pl.BlockSpec((1, d), lambda i: (0, 0))
    sort_cap = _round_up(n_tab * RUN_CHUNK, SORT_ROWS)
    tab_spec = lambda f: pl.BlockSpec((None, 1, n_tab + 1), lambda i: (f(i), 0, 0), memory_space=pltpu.SMEM)
    return pl.pallas_call(
        functools.partial(_combine_body, n_tab),
        grid=(nt,),
        in_specs=[tab_spec(lambda i: i), tab_spec(lambda i: jnp.minimum(i + 1, nt - 1)),
                  pl.BlockSpec((TOP_K, ROUTE_TILE), lambda i: (0, i)),
                  pl.BlockSpec((TOP_K, ROUTE_TILE), lambda i: (0, i)),
                  pl.BlockSpec((ROUTE_TILE, d), lambda i: (i, 0)),
                  mod,
                  pl.BlockSpec(memory_space=pl.ANY)],
        out_specs=pl.BlockSpec((ROUTE_TILE, d), lambda i: (i, 0)),
        out_shape=jax.ShapeDtypeStruct((t, d), F32),
        scratch_shapes=[pltpu.VMEM((2, sort_cap, d), BF16), pltpu.SemaphoreType.DMA((2,))],
        compiler_params=_cparams(("arbitrary",), VMEM_MID),
        name="combine",
    )(table, table, lpos, wts, x2, gate, ypad)


def _prep_weights(g_norm_mix, w_in, g_q_a, w_uq, g_kv_a, w_ukv, g_qk_q, g_qk_k, w_g2, b_g2, g_gla_out, w_o,
                  g_norm_ffn, w_router, b_router):
    d = w_in.shape[0]
    o_qa, o_kva, o_kr = 0, Q_LORA, Q_LORA + KV_LORA
    o_gq = o_kr + ROPE
    o_gk = o_gq + HEADS * GLA_DK
    o_gv = o_gk + HEADS * GLA_DK
    o_glr = o_gv + HEADS * GLA_DV
    o_gr = o_glr + GATE_RANK
    kr_cols = w_in[:, o_kr:o_kr + ROPE]
    w1 = jnp.concatenate([
        w_in[:, o_qa:o_kva], kr_cols, kr_cols, w_in[:, o_kva:o_kr], w_in[:, o_gq:o_gk], w_in[:, o_gk:o_gv],
        w_in[:, o_gv:o_glr], w_in[:, o_gr:o_gr + HEADS * GLA_DV], w_in[:, o_glr:o_gr],
        jnp.zeros((d, LANES - GATE_RANK), w_in.dtype)], axis=1).astype(BF16)
    assert w1.shape[1] == _W1_COLS
    wq = w_uq.reshape(Q_LORA, HEADS, QK)
    wuq = jnp.concatenate([wq[:, :, 0:NOPE].reshape(Q_LORA, HEADS * NOPE),
                           wq[:, :, NOPE:QK].reshape(Q_LORA, HEADS * ROPE)], axis=1).astype(BF16)
    wkv = w_ukv.reshape(KV_LORA, HEADS, NOPE + V_DIM)
    wuk = wkv[:, :, 0:NOPE].reshape(KV_LORA, HEADS * NOPE).astype(BF16)
    wuv_t = wkv[:, :, NOPE:].reshape(KV_LORA, HEADS * V_DIM).T.astype(BF16)
    pad_rope = lambda g: jnp.stack([g[0:NOPE], jnp.concatenate([g[NOPE:QK], jnp.zeros((QK_PAD - QK,), g.dtype)])])
    inv = ROPE_THETA ** (-jnp.arange(HALF, dtype=F32) / HALF)
    sign = jnp.concatenate([-jnp.ones((HALF,), F32), jnp.ones((HALF,), F32)])
    rope_tab = jnp.stack([jnp.tile(inv, LANES // HALF), jnp.tile(sign, LANES // ROPE)])
    wg2 = jnp.concatenate([w_g2, jnp.zeros((LANES - GATE_RANK, w_g2.shape[1]), w_g2.dtype)], axis=0).astype(BF16)
    wr_t = w_router.T
    wr_hi = wr_t.astype(BF16)
    wr_lo = (wr_t - wr_hi.astype(F32)).astype(BF16)
    return dict(
        g_mix=g_norm_mix.reshape(1, d), w1=w1, g_qa=g_q_a.reshape(1, -1), w_uq=wuq, g_kv=g_kv_a.reshape(1, -1),
        w_uk=wuk, w_uv_t=wuv_t, g_qk_q=pad_rope(g_qk_q), g_qk_k=pad_rope(g_qk_k), rope=rope_tab, w_g2=wg2,
        b_g2=b_g2.reshape(1, -1), g_out=g_gla_out.reshape(1, -1), w_o=w_o.astype(BF16),
        g_ffn=g_norm_ffn.reshape(1, d), w_r2=jnp.stack([wr_hi, wr_lo]), b_r=b_router.reshape(-1, 1))


def _mixer(x, mod, pos0, past_lat, past_kr, s0_pairs, wts):
    b, s, d = x.shape
    t = b * s
    if b > 1 and s & (s - 1) == 0 and min(t, TOKEN_TILE) % s == 0:
        per_tok = lambda j: jnp.broadcast_to(mod[:, j:j + 1], (b, s, d)).reshape(1, t, d)
        outs = _proj(x.reshape(1, t, d), per_tok(0), per_tok(1), pos0, wts, period=s)
        q = outs[0].reshape(HEADS, b, s, QK_PAD).transpose(1, 0, 2, 3)
        lat, kr, gq, gk, gv, gl, gr = [o.reshape(b, s, o.shape[-1]) for o in outs[1:]]
    else:
        q, lat, kr, gq, gk, gv, gl, gr = _proj(x, mod[:, 0:1], mod[:, 1:2], pos0, wts)
    kv_w = (wts["w_uk"], wts["w_uv_t"], wts["g_qk_k"])
    if past_lat is None:
        k_new, vt_new = _kv(lat, kr, *kv_w)
        o_mla = _attn_prompt(q, k_new, vt_new)
    else:
        o_mla = _attn_sample(q, past_lat, past_kr, lat, kr, *kv_w)
    o_gla, s_fin = _gla(gq, gk, gl, gv, gr, s0_pairs, wts["g_out"])
    if b == 1:
        rows = lambda j: mod[0, j:j + 1]
    else:
        rows = lambda j: jnp.broadcast_to(mod[:, j:j + 1], (b, s, d)).reshape(t, d)
    x2, h2, idx, wt, lrank, cnt = _post(x.reshape(t, d), o_mla.reshape(t, -1), o_gla.reshape(t, -1), rows(2), rows(4),
                                        rows(3), wts["w_o"], wts["g_ffn"], wts["w_r2"], wts["b_r"])
    return dict(x2=x2, h2=h2, idx=idx, wt=wt, lrank=lrank, cnt=cnt, gate_f=rows(5), lat=lat, kr=kr, s_fin=s_fin)


def kernel(x_prompt, x_sample, cache_mla_latent, cache_mla_krope, state_gla, c_prompt, c_sample, w_ada, b_ada, g_norm_mix, w_in, g_q_a, w_uq, g_kv_a, w_ukv, g_qk_q, g_qk_k, w_g2, b_g2, g_gla_out, w_o, g_norm_ffn, w_router, b_router, w_gu, b_gu, w_down, b_down):
    depth = w_ada.shape[0]
    assert depth == 1, "single-layer step"
    bp, sp, d = x_prompt.shape
    bs, ss, _ = x_sample.shape
    tp, tsm = bp * sp, bs * ss
    assert tp % ROUTE_TILE == 0 and tsm % ROUTE_TILE == 0, "token counts must be whole routing tiles"
    past = cache_mla_latent.shape[2]
    layer = lambda a: a.reshape(a.shape[1:])
    wts = _prep_weights(*[layer(a) for a in (g_norm_mix, w_in, g_q_a, w_uq, g_kv_a, w_ukv, g_qk_q, g_qk_k, w_g2, b_g2,
                                             g_gla_out, w_o, g_norm_ffn, w_router, b_router)])
    w_gu, b_gu, w_down, b_down = layer(w_gu), layer(b_gu), layer(w_down), layer(b_down)

    mod = _ada(jnp.concatenate([c_prompt, c_sample], axis=0), layer(w_ada), layer(b_ada)).reshape(bp + bs, 6, d)
    zero_state = jnp.zeros((bp, HEADS // 2, GLA_DV, LANES), F32)
    pr = _mixer(x_prompt, mod[:bp], 0, None, None, zero_state, wts)
    sa = _mixer(x_sample, mod[bp:], past, layer(cache_mla_latent), layer(cache_mla_krope),
                _state_to_pairs(layer(state_gla)), wts)

    idx = jnp.concatenate([pr["idx"], sa["idx"]], axis=1)
    lrank = jnp.concatenate([pr["lrank"], sa["lrank"]], axis=1)
    rt = _route_tables(idx, lrank, jnp.concatenate([pr["cnt"], sa["cnt"]], axis=0))
    ntp = tp // ROUTE_TILE
    lpos, table = rt["lpos"], rt["table"]
    xpad = _scatter(table, rt["tail"], lpos, pr["h2"], sa["h2"], rt["n_blocks"])
    ypad = _experts(rt["blk_e"], rt["n_valid"], rt["nxt_e"], rt["ord_e"], xpad, w_gu, b_gu, w_down, b_down)
    y_p = _combine(table[:ntp], lpos[:, :tp], pr["wt"], pr["x2"], pr["gate_f"], ypad).reshape(bp, sp, d)
    y_s = _combine(table[ntp:], lpos[:, tp:], sa["wt"], sa["x2"], sa["gate_f"], ypad).reshape(bs, ss, d)

    return (y_p, y_s,
            pr["lat"][None], pr["kr"][None], _state_from_pairs(pr["s_fin"])[None],
            sa["lat"][None], sa["kr"][None], _state_from_pairs(sa["s_fin"])[None])
```

```python
import functools

import numpy as np
import jax
import jax.numpy as jnp
from jax import lax
from jax.experimental import pallas as pl
from jax.experimental.pallas import tpu as pltpu

F32 = jnp.float32
BF16 = jnp.bfloat16
I32 = jnp.int32

CHUNK = 64
EPS = 1e-6
HEADS = 4
Q_LORA = 384
KV_LORA = 256
NOPE = 128
ROPE = 64
HALF = ROPE // 2
V_DIM = 128
QK = NOPE + ROPE
QK_PAD = 256
ROPE_THETA = 10000.0
GLA_DK = 64
GLA_DV = 128
GATE_RANK = 16
GATE_NORM = 16.0
N_EXPERTS = 32
TOP_K = 4
SWIGLU_LIMIT = 7.0
SWIGLU_ALPHA = 1.702
NEG = -1e30
LOG2_E = 1.4426950408889634

LANES = 128
SUBLANES = 8
BF16_ROWS = 16
TOKEN_TILE = 512
ADA_COLS = 1536
ROUTE_TILE = 512
RUN_CHUNK = BF16_ROWS
SORT_ROWS = 512
CHUNK_UNROLL = 4
EXPERT_ROWS = 256
ATTN_TILE = 1024
MIB = 1024 * 1024
VMEM_BIG = 56 * MIB
VMEM_MID = 48 * MIB
VMEM_SMALL = 40 * MIB


def _cparams(sem, vmem=None):
    return pltpu.CompilerParams(dimension_semantics=sem, vmem_limit_bytes=vmem)


def _nt(a, b):
    return lax.dot_general(a, b, (((1,), (1,)), ((), ())), preferred_element_type=F32)


def _rms(x, width):
    return lax.rsqrt(jnp.sum(x * x, axis=-1, keepdims=True) * (1.0 / width) + EPS)


def _round_up(x, m):
    return ((x + m - 1) // m) * m


def _ada_body(c_ref, w_ref, b_ref, o_ref):
    c = c_ref[...]
    s = (c * jax.nn.sigmoid(c)).astype(BF16)
    o_ref[...] = jnp.dot(s, w_ref[...].astype(BF16), preferred_element_type=F32) + b_ref[...]


def _ada(c, w_ada, b_ada):
    r, d = c.shape
    n = w_ada.shape[1]
    tn = ADA_COLS if n % ADA_COLS == 0 else n
    return pl.pallas_call(
        _ada_body,
        grid=(n // tn,),
        in_specs=[pl.BlockSpec((r, d), lambda j: (0, 0)),
                  pl.BlockSpec((d, tn), lambda j: (0, j)),
                  pl.BlockSpec((1, tn), lambda j: (0, j))],
        out_specs=pl.BlockSpec((r, tn), lambda j: (0, j)),
        out_shape=jax.ShapeDtypeStruct((r, n), F32),
        compiler_params=_cparams(("arbitrary",), VMEM_SMALL),
        name="ada",
    )(c, w_ada, b_ada.reshape(1, n))


_SEG = dict(qa_kr=(0, 512), kva=(512, 768), gq=(768, 1024), gk=(1024, 1280),
            gv=(1280, 1792), gr=(1792, 2304), glr=(2304, 2432))
_W1_COLS = 2432


def _proj_body(pos0, ts, period, x_ref, sh_ref, sc_ref, gmix_ref, w1_ref, gqa_ref, wuq_ref, gkv_ref, gqk_ref,
               rope_ref, wg2_ref, bg2_ref,
               q_ref, lat_ref, kr_ref, gq_o, gk_o, gv_o, gl_o, gr_o, trig_scr):
    i = pl.program_id(1)
    x = x_ref[...]
    d = x.shape[-1]
    h = (x * _rms(x, d) * gmix_ref[...]) * (1.0 + sc_ref[...]) + sh_ref[...]
    hb = h.astype(BF16)

    def seg(name):
        a, b = _SEG[name]
        return jnp.dot(hb, w1_ref[:, a:b], preferred_element_type=F32)

    @pl.when((pl.program_id(0) == 0) & (i == 0))
    def _():
        row = lax.broadcasted_iota(I32, (ts, LANES), 0)
        if period is not None:
            row = row & (period - 1)
        row_ang = row.astype(F32) * rope_ref[0:1, :]
        trig_scr[0] = jnp.cos(row_ang)
        trig_scr[1] = jnp.sin(row_ang)

    tile_pos = pos0 + (i * ts if period is None else 0 * i)
    base_ang = jnp.broadcast_to(tile_pos.astype(F32) * rope_ref[0:1, :], (SUBLANES, LANES))
    cos_a, sin_a = jnp.cos(base_ang)[0:1, :], jnp.sin(base_ang)[0:1, :]
    cos = cos_a * trig_scr[0] - sin_a * trig_scr[1]
    sin = (sin_a * trig_scr[0] + cos_a * trig_scr[1]) * rope_ref[1:2, :]
    lane = lax.broadcasted_iota(I32, (ts, LANES), 1)
    first_half = (lane & HALF) == 0
    low64 = lane < ROPE

    def rope(v):
        partner = jnp.where(first_half, pltpu.roll(v, LANES - HALF, 1), pltpu.roll(v, HALF, 1))
        return v * cos + partner * sin

    qa_kr = seg("qa_kr")
    qa = qa_kr[:, 0:Q_LORA]
    qn = (qa * _rms(qa, Q_LORA) * gqa_ref[...]).astype(BF16)
    qf = jnp.dot(qn, wuq_ref[...], preferred_element_type=F32)
    rope_blocks = (rope(qf[:, 4 * NOPE:4 * NOPE + LANES]), rope(qf[:, 4 * NOPE + LANES:4 * NOPE + 2 * LANES]))
    for hd in range(HEADS):
        nope = qf[:, NOPE * hd:NOPE * (hd + 1)]
        blk = rope_blocks[hd // 2]
        if hd % 2:
            blk = pltpu.roll(blk, ROPE, 1)
        blk = jnp.where(low64, blk, 0.0)
        ss = jnp.sum(nope * nope, axis=-1, keepdims=True) + jnp.sum(blk * blk, axis=-1, keepdims=True)
        scl = lax.rsqrt(ss * (1.0 / QK) + EPS) * (QK ** -0.5 * LOG2_E)
        q_ref[hd, :, 0:NOPE] = (nope * scl * gqk_ref[0:1, :]).astype(BF16)
        q_ref[hd, :, NOPE:QK_PAD] = (blk * scl * gqk_ref[1:2, :]).astype(BF16)

    kva = seg("kva")
    lat_ref[...] = kva * _rms(kva, KV_LORA) * gkv_ref[...]
    kr_ref[...] = rope(qa_kr[:, Q_LORA:Q_LORA + LANES])[:, 0:ROPE]

    gq_o[...] = seg("gq") * (GLA_DK ** -0.5)
    gk_o[...] = seg("gk")
    gv_o[...] = seg("gv").astype(BF16)
    gr_o[...] = seg("gr")
    z = jnp.dot(seg("glr").astype(BF16), wg2_ref[...], preferred_element_type=F32) + bg2_ref[...]
    gl_o[...] = (jnp.minimum(z, 0.0) - jnp.log1p(jnp.exp(-jnp.abs(z)))) * (1.0 / GATE_NORM)


def _proj(x, shift, scale, pos0, wts, period=None):
    b, s, d = x.shape
    ts = min(s, TOKEN_TILE)
    assert period is None or (period & (period - 1) == 0 and ts % period == 0)
    row = lambda a: pl.BlockSpec(a.shape, lambda bi, i: (0,) * a.ndim)
    tok = lambda w: pl.BlockSpec((None, ts, w), lambda bi, i: (bi, i, 0))
    mod = tok(d) if shift.shape[1] == s and s > 1 else pl.BlockSpec((None, 1, d), lambda bi, i: (bi, 0, 0))
    small = [wts["g_mix"], wts["w1"], wts["g_qa"], wts["w_uq"], wts["g_kv"], wts["g_qk_q"], wts["rope"],
             wts["w_g2"], wts["b_g2"]]
    out_shape = (
        jax.ShapeDtypeStruct((b, HEADS, s, QK_PAD), BF16),
        jax.ShapeDtypeStruct((b, s, KV_LORA), F32),
        jax.ShapeDtypeStruct((b, s, ROPE), F32),
        jax.ShapeDtypeStruct((b, s, HEADS * GLA_DK), F32),
        jax.ShapeDtypeStruct((b, s, HEADS * GLA_DK), F32),
        jax.ShapeDtypeStruct((b, s, HEADS * GLA_DV), BF16),
        jax.ShapeDtypeStruct((b, s, HEADS * GLA_DK), F32),
        jax.ShapeDtypeStruct((b, s, HEADS * GLA_DV), F32),
    )
    out_specs = (
        pl.BlockSpec((None, HEADS, ts, QK_PAD), lambda bi, i: (bi, 0, i, 0)),
        tok(KV_LORA), tok(ROPE), tok(HEADS * GLA_DK), tok(HEADS * GLA_DK), tok(HEADS * GLA_DV),
        tok(HEADS * GLA_DK), tok(HEADS * GLA_DV),
    )
    return pl.pallas_call(
        functools.partial(_proj_body, pos0, ts, period),
        grid=(b, s // ts),
        in_specs=[tok(d), mod, mod] + [row(a) for a in small],
        out_specs=out_specs,
        out_shape=out_shape,
        scratch_shapes=[pltpu.VMEM((2, ts, LANES), F32)],
        compiler_params=_cparams(("arbitrary", "arbitrary"), VMEM_BIG),
        name="proj",
    )(x, shift, scale, *small)


def _key_rows(lat, kr, wk_ref, gk_ref, k_out):
    kn_all = jnp.dot(lat, wk_ref[...], preferred_element_type=F32)
    kr_ss = jnp.sum(kr * kr, axis=-1, keepdims=True)
    for hd in range(HEADS):
        kn = kn_all[:, NOPE * hd:NOPE * (hd + 1)]
        scl = lax.rsqrt((jnp.sum(kn * kn, axis=-1, keepdims=True) + kr_ss) * (1.0 / QK) + EPS)
        k_out[hd, :, 0:NOPE] = (kn * scl * gk_ref[0:1, :]).astype(BF16)
        k_out[hd, :, NOPE:QK] = (kr * scl * gk_ref[1:2, 0:ROPE]).astype(BF16)
        k_out[hd, :, QK:QK_PAD] = jnp.zeros((kr.shape[0], QK_PAD - QK), BF16)


def _kv_body(lat_ref, kr_ref, wk_ref, wv_ref, gk_ref, k_ref, v_ref):
    lat = lat_ref[...].astype(BF16)
    _key_rows(lat, kr_ref[...], wk_ref, gk_ref, k_ref)
    v_t = _nt(wv_ref[...], lat)
    for hd in range(HEADS):
        v_ref[hd] = v_t[V_DIM * hd:V_DIM * (hd + 1), :].astype(BF16)


def _kv(lat, kr, w_uk, w_uv_t, g_qk_k):
    b, s, _ = lat.shape
    ts = min(s, ATTN_TILE)
    return pl.pallas_call(
        _kv_body,
        grid=(b, s // ts),
        in_specs=[pl.BlockSpec((None, ts, KV_LORA), lambda bi, i: (bi, i, 0)),
                  pl.BlockSpec((None, ts, ROPE), lambda bi, i: (bi, i, 0)),
                  pl.BlockSpec(w_uk.shape, lambda bi, i: (0, 0)),
                  pl.BlockSpec(w_uv_t.shape, lambda bi, i: (0, 0)),
                  pl.BlockSpec(g_qk_k.shape, lambda bi, i: (0, 0))],
        out_specs=(pl.BlockSpec((None, HEADS, ts, QK_PAD), lambda bi, i: (bi, 0, i, 0)),
                   pl.BlockSpec((None, HEADS, None, V_DIM, ts), lambda bi, i: (bi, 0, i, 0, 0))),
        out_shape=(jax.ShapeDtypeStruct((b, HEADS, s, QK_PAD), BF16),
                   jax.ShapeDtypeStruct((b, HEADS, s // ts, V_DIM, ts), BF16)),
        compiler_params=_cparams(("arbitrary", "arbitrary")),
        name="kv",
    )(lat, kr, w_uk, w_uv_t, g_qk_k)


def _attn_prompt_body(t, q_ref, qn_ref, k_ref, vt_ref, o_ref, s_a, s_b):
    i = pl.program_id(2)

    def scores(q, j, buf):
        buf[...] = _nt(k_ref[pl.ds(pl.multiple_of(j * t, t), t), :], q)

    def consume(j, buf, carry):
        m, l, acc = carry
        s = buf[...]
        m_new = jnp.maximum(m, jnp.max(s, axis=0, keepdims=True))
        alpha = jnp.exp2(m - m_new)
        p = jnp.exp2(s - m_new)
        l = alpha * l + jnp.sum(p, axis=0, keepdims=True)
        acc = alpha * acc + jnp.dot(vt_ref[j], p.astype(BF16), preferred_element_type=F32)
        return m_new, l, acc

    def consume_diagonal(j, buf, carry):
        m, l, acc = carry
        hf = t // 2
        vt = vt_ref[j]
        tri = (lax.broadcasted_iota(I32, (hf, hf), 0) // CHUNK) <= (lax.broadcasted_iota(I32, (hf, hf), 1) // CHUNK)
        s_a = jnp.where(tri, buf[0:hf, 0:hf], NEG)
        s_b1 = buf[0:hf, hf:t]
        s_b2 = jnp.where(tri, buf[hf:t, hf:t], NEG)
        m_new = jnp.concatenate(
            [jnp.maximum(m[:, 0:hf], jnp.max(s_a, axis=0, keepdims=True)),
             jnp.maximum(m[:, hf:t], jnp.maximum(jnp.max(s_b1, axis=0, keepdims=True),
                                                 jnp.max(s_b2, axis=0, keepdims=True)))], axis=1)
        alpha = jnp.exp2(m - m_new)
        p_a = jnp.exp2(s_a - m_new[:, 0:hf])
        p_b1 = jnp.exp2(s_b1 - m_new[:, hf:t])
        p_b2 = jnp.exp2(s_b2 - m_new[:, hf:t])
        l = alpha * l + jnp.concatenate(
            [jnp.sum(p_a, axis=0, keepdims=True),
             jnp.sum(p_b1, axis=0, keepdims=True) + jnp.sum(p_b2, axis=0, keepdims=True)], axis=1)
        pv = jnp.concatenate(
            [jnp.dot(vt[:, 0:hf], p_a.astype(BF16), preferred_element_type=F32),
             jnp.dot(vt[:, 0:hf], p_b1.astype(BF16), preferred_element_type=F32)
             + jnp.dot(vt[:, hf:t], p_b2.astype(BF16), preferred_element_type=F32)], axis=1)
        return m_new, l, alpha * acc + pv

    def run(first, second):
        q = q_ref[...]

        @pl.when(i == 0)
        def _():
            scores(q, 0, first)

        def pair(pp, carry):
            j = 2 * pp
            scores(q, j + 1, second)
            carry = consume(j, first, carry)
            scores(q, j + 2, first)
            return consume(j + 1, second, carry)

        def even_tail(carry):
            scores(qn_ref[...], 0, second)
            return consume_diagonal(i, first, carry)

        def odd_tail(carry):
            scores(q, i, second)
            carry = consume(i - 1, first, carry)
            scores(qn_ref[...], 0, first)
            return consume_diagonal(i, second, carry)

        carry = (jnp.full((1, t), NEG, F32), jnp.zeros((1, t), F32), jnp.zeros((V_DIM, t), F32))
        carry = lax.fori_loop(0, i // 2, pair, carry)
        _, l, acc = lax.cond(i % 2 == 1, odd_tail, even_tail, carry)
        o_ref[...] = (acc / l).T.astype(BF16)

    @pl.when(((i + 1) // 2) % 2 == 0)
    def _():
        run(s_a, s_b)

    @pl.when(((i + 1) // 2) % 2 == 1)
    def _():
        run(s_b, s_a)


def _attn_prompt(q, k, v_t):
    b, _, s, _ = q.shape
    t = v_t.shape[-1]
    nq = s // t
    return pl.pallas_call(
        functools.partial(_attn_prompt_body, t),
        grid=(b, HEADS, nq),
        in_specs=[pl.BlockSpec((None, None, t, QK_PAD), lambda bi, h, i: (bi, h, i, 0)),
                  pl.BlockSpec((None, None, t, QK_PAD), lambda bi, h, i: (bi, h, jnp.minimum(i + 1, nq - 1), 0)),
                  pl.BlockSpec((None, None, s, QK_PAD), lambda bi, h, i: (bi, h, 0, 0)),
                  pl.BlockSpec((None, None, nq, V_DIM, t), lambda bi, h, i: (bi, h, 0, 0, 0))],
        out_specs=pl.BlockSpec((None, t, V_DIM), lambda bi, h, i: (bi, i, h)),
        out_shape=jax.ShapeDtypeStruct((b, s, HEADS * V_DIM), BF16),
        scratch_shapes=[pltpu.VMEM((t, t), F32), pltpu.VMEM((t, t), F32)],
        compiler_params=_cparams(("arbitrary", "arbitrary", "arbitrary"), VMEM_BIG),
        name="attn_prompt",
    )(q, q, k, v_t)


def _attn_sample_body(past, sq, q_ref, plat_ref, pkr_ref, nlat_ref, nkr_ref, wk_ref, wv_ref, gk_ref, o_ref,
                      kp_scr, kn_scr):
    plat = plat_ref[...].astype(BF16)
    nlat = nlat_ref[...].astype(BF16)
    _key_rows(plat, pkr_ref[...], wk_ref, gk_ref, kp_scr)
    _key_rows(nlat, nkr_ref[...], wk_ref, gk_ref, kn_scr)
    vp_t = _nt(wv_ref[...], plat).astype(BF16)
    vn_t = _nt(wv_ref[...], nlat).astype(BF16)
    hq = HEADS * sq
    col = lax.broadcasted_iota(I32, (sq, hq), 1)
    key_chunk = (past + lax.broadcasted_iota(I32, (sq, hq), 0)) // CHUNK
    qry_chunk = (past + col % sq) // CHUNK
    qrow = lax.broadcasted_iota(I32, (hq, 1), 0) // sq
    q_all = jnp.concatenate([q_ref[hd] for hd in range(HEADS)], axis=0)
    s_p = jnp.zeros((past, hq), F32)
    s_n = jnp.zeros((sq, hq), F32)
    for hd in range(HEADS):
        q_h = jnp.where(qrow == hd, q_all, jnp.zeros((), BF16))
        s_p = s_p + _nt(kp_scr[hd], q_h)
        s_n = s_n + _nt(kn_scr[hd], q_h)
    s_n = jnp.where(key_chunk <= qry_chunk, s_n, NEG)
    m = jnp.maximum(jnp.max(s_p, axis=0, keepdims=True), jnp.max(s_n, axis=0, keepdims=True))
    p_p = jnp.exp2(s_p - m)
    p_n = jnp.exp2(s_n - m)
    inv_l = 1.0 / (jnp.sum(p_p, axis=0, keepdims=True) + jnp.sum(p_n, axis=0, keepdims=True))
    p_p = p_p.astype(BF16)
    p_n = p_n.astype(BF16)
    lane_head = lax.broadcasted_iota(I32, (V_DIM, hq), 1) // sq
    o_t = jnp.zeros((V_DIM, hq), F32)
    for hd in range(HEADS):
        rows = slice(V_DIM * hd, V_DIM * (hd + 1))
        o_h = (jnp.dot(vp_t[rows, :], p_p, preferred_element_type=F32)
               + jnp.dot(vn_t[rows, :], p_n, preferred_element_type=F32))
        o_t = o_t + jnp.where(lane_head == hd, o_h, 0.0)
    o_all = (o_t * inv_l).T
    for hd in range(HEADS):
        o_ref[:, V_DIM * hd:V_DIM * (hd + 1)] = o_all[sq * hd:sq * (hd + 1), :].astype(BF16)


def _attn_sample(q, past_lat, past_kr, lat, kr, w_uk, w_uv_t, g_qk_k):
    b, _, sq, _ = q.shape
    past = past_lat.shape[1]
    rows = lambda n, w: pl.BlockSpec((None, n, w), lambda bi: (bi, 0, 0))
    full = lambda a: pl.BlockSpec(a.shape, lambda bi: (0,) * a.ndim)
    return pl.pallas_call(
        functools.partial(_attn_sample_body, past, sq),
        grid=(b,),
        in_specs=[pl.BlockSpec((None, HEADS, sq, QK_PAD), lambda bi: (bi, 0, 0, 0)),
                  rows(past, KV_LORA), rows(past, ROPE), rows(sq, KV_LORA), rows(sq, ROPE),
                  full(w_uk), full(w_uv_t), full(g_qk_k)],
        out_specs=rows(sq, HEADS * V_DIM),
        out_shape=jax.ShapeDtypeStruct((b, sq, HEADS * V_DIM), BF16),
        scratch_shapes=[pltpu.VMEM((HEADS, past, QK_PAD), BF16), pltpu.VMEM((HEADS, sq, QK_PAD), BF16)],
        compiler_params=_cparams(("arbitrary",), VMEM_SMALL),
        name="attn_sample",
    )(q, past_lat, past_kr, lat, kr, w_uk, w_uv_t, g_qk_k)


def _gla_masks(c, rows):
    idx = np.arange(rows)
    same = (idx // c)[:, None] == (idx // c)[None, :]
    le = same & (idx[None, :] <= idx[:, None])
    gt = same & (idx[None, :] > idx[:, None])
    return np.concatenate([le, gt], axis=0).astype(np.float32), int(np.log2(c))


def _level_exponents(b, g, c, level):
    n = c >> level
    rows = b.shape[0]
    row = lax.broadcasted_iota(I32, (rows, 1), 0)
    if n >= 8:
        split = b.reshape(rows // n, n, LANES)[:, n // 2 - 1:n // 2, :]
        split = jnp.broadcast_to(split, (rows // n, n, LANES)).reshape(rows, LANES)
        return jnp.where((row & (n // 2)) != 0, b - split, split - b)
    g_prev = pltpu.roll(g, 1, 0)
    g_next = pltpu.roll(g, rows - 1, 0)
    if n == 4:
        r = row & 3
        return jnp.where(r == 0, g_next, jnp.where(r == 1, 0.0, jnp.where(r == 2, g, g + g_prev)))
    assert n == 2
    return jnp.where((row & 1) != 0, g, 0.0)


def _gla_body(c, n_chunks, unit, levels, mall_ref, q_ref, k_ref, g_ref, v_ref, r_ref, s0_ref, gout_ref,
              o_ref, sfin_ref, st_scr):
    it = pl.program_id(1)

    @pl.when(it == 0)
    def _():
        st_scr[...] = s0_ref[...]

    ru = unit * c
    lane = lax.broadcasted_iota(I32, (ru, LANES), 1)
    head_lanes = (lane < GLA_DK, lane >= GLA_DK)
    st_lane_lo = lax.broadcasted_iota(I32, (GLA_DV, LANES), 1) < GLA_DK
    row = lax.broadcasted_iota(I32, (ru, 1), 0)
    ri = lax.broadcasted_iota(I32, (ru, ru), 0)
    ci = lax.broadcasted_iota(I32, (ru, ru), 1)
    mall = mall_ref[...]

    for un in range(n_chunks // unit):
        rows = slice(un * ru, (un + 1) * ru)
        for p in range(HEADS // 2):
            ls = slice(LANES * p, LANES * (p + 1))
            g = g_ref[rows, ls]
            q = q_ref[rows, ls]
            k = k_ref[rows, ls]
            g_hi = g.astype(BF16)
            g_lo = (g - g_hi.astype(F32)).astype(BF16)
            e2 = jnp.dot(mall, jnp.concatenate([g_hi, g_lo], axis=1), preferred_element_type=F32)
            e = e2[:, 0:LANES] + e2[:, LANES:2 * LANES]
            b = e[0:ru]
            eb = jnp.exp(b)
            qb = q * eb
            kd = (k * jnp.exp(e[ru:2 * ru])).astype(BF16)
            qs, ks = [q], [k.astype(BF16)]
            for l in range(levels):
                bottom = (row & (c >> (l + 1))) != 0
                decay = jnp.exp(_level_exponents(b, g, c, l))
                qs.append(jnp.where(bottom, q * decay, 0.0))
                ks.append(jnp.where(bottom, 0.0, k * decay).astype(BF16))
            states = [st_scr[p]]
            for j in range(unit):
                cr = slice(j * c, (j + 1) * c)
                upd = [lax.dot_general(v_ref[rows, GLA_DV * (2 * p + hh):GLA_DV * (2 * p + hh + 1)][cr, :], kd[cr, :],
                                       (((0,), (0,)), ((), ())), preferred_element_type=F32) for hh in range(2)]
                d_last = eb[j * c + c - 1:j * c + c, :]
                states.append(states[-1] * d_last + jnp.where(st_lane_lo, upd[0], upd[1]))
            st_scr[p] = states[-1]
            for hh in range(2):
                hd = 2 * p + hh
                sel = head_lanes[hh]
                a = jnp.where(ri == ci, _nt(jnp.where(sel, qs[0], 0.0).astype(BF16), ks[0]), 0.0)
                for l in range(levels):
                    pr = _nt(jnp.where(sel, qs[l + 1], 0.0).astype(BF16), ks[l + 1])
                    a = a + jnp.where((ri ^ ci) < (c >> l), pr, 0.0)
                vh = v_ref[rows, GLA_DV * hd:GLA_DV * (hd + 1)]
                qb_h = jnp.where(sel, qb, 0.0).astype(BF16)
                o_state = [_nt(qb_h[j * c:(j + 1) * c, :], states[j].astype(BF16)) for j in range(unit)]
                o = jnp.dot(a.astype(BF16), vh, preferred_element_type=F32) + jnp.concatenate(o_state, axis=0)
                on = o * _rms(o, GLA_DV) * gout_ref[...]
                r = r_ref[rows, GLA_DV * hd:GLA_DV * (hd + 1)]
                o_ref[rows, GLA_DV * hd:GLA_DV * (hd + 1)] = (on * (r * jax.nn.sigmoid(r))).astype(BF16)

    @pl.when(it == pl.num_programs(1) - 1)
    def _():
        sfin_ref[...] = st_scr[...]


def _gla(gq, gk, gl, gv, gr, s0, g_out):
    b, s, _ = gq.shape
    c = min(CHUNK, s)
    tile = min(s, 8 * c)
    unit = next(u for u in (4, 2, 1) if (tile // c) % u == 0)
    masks, levels = _gla_masks(c, unit * c)
    mall = jnp.asarray(masks, BF16)
    tok = lambda w: pl.BlockSpec((None, tile, w), lambda bi, i: (bi, i, 0))
    st_spec = pl.BlockSpec((None, HEADS // 2, GLA_DV, LANES), lambda bi, i: (bi, 0, 0, 0))
    return pl.pallas_call(
        functools.partial(_gla_body, c, tile // c, unit, levels),
        grid=(b, s // tile),
        in_specs=[pl.BlockSpec(mall.shape, lambda bi, i: (0, 0)),
                  tok(HEADS * GLA_DK), tok(HEADS * GLA_DK), tok(HEADS * GLA_DK), tok(HEADS * GLA_DV),
                  tok(HEADS * GLA_DV), st_spec, pl.BlockSpec(g_out.shape, lambda bi, i: (0, 0))],
        out_specs=(tok(HEADS * GLA_DV), st_spec),
        out_shape=(jax.ShapeDtypeStruct((b, s, HEADS * GLA_DV), BF16),
                   jax.ShapeDtypeStruct((b, HEADS // 2, GLA_DV, LANES), F32)),
        scratch_shapes=[pltpu.VMEM((HEADS // 2, GLA_DV, LANES), F32)],
        compiler_params=_cparams(("arbitrary", "arbitrary")),
        name="gla",
    )(mall, gq, gk, gl, gv, gr, s0, g_out)


def _state_to_pairs(s):
    b = s.shape[0]
    s = s.reshape(b, HEADS // 2, 2, GLA_DK, GLA_DV)
    return jnp.transpose(s, (0, 1, 4, 2, 3)).reshape(b, HEADS // 2, GLA_DV, 2 * GLA_DK)


def _state_from_pairs(s):
    b = s.shape[0]
    s = s.reshape(b, HEADS // 2, GLA_DV, 2, GLA_DK)
    return jnp.transpose(s, (0, 1, 3, 4, 2)).reshape(b, HEADS, GLA_DK, GLA_DV)


def _post_body(x_ref, om_ref, og_ref, gt_ref, sc_ref, sh_ref, wo_ref, gffn_ref, wr_ref, br_ref,
               x2_ref, h_ref, idx_ref, wt_ref, rank_ref, cnt_ref):
    half = om_ref.shape[-1]
    mix = (jnp.dot(om_ref[...], wo_ref[0:half, :], preferred_element_type=F32)
           + jnp.dot(og_ref[...], wo_ref[half:2 * half, :], preferred_element_type=F32))
    x2 = x_ref[...] + gt_ref[...] * mix
    x2_ref[...] = x2
    d = x2.shape[-1]
    h = (x2 * _rms(x2, d) * gffn_ref[...]) * (1.0 + sc_ref[...]) + sh_ref[...]
    h_hi = h.astype(BF16)
    h_ref[...] = h_hi
    h_lo = (h - h_hi.astype(F32)).astype(BF16)
    logits = _nt(wr_ref[0], h_hi) + _nt(wr_ref[0], h_lo) + _nt(wr_ref[1], h_hi) + br_ref[...]
    n_exp, tm = logits.shape
    eid = lax.broadcasted_iota(I32, (n_exp, tm), 0)
    vals, tops, ids = logits, [], []
    for _ in range(TOP_K):
        m = jnp.max(vals, axis=0, keepdims=True)
        sel = jnp.min(jnp.where(vals == m, eid, n_exp), axis=0, keepdims=True)
        tops.append(m)
        ids.append(sel)
        vals = jnp.where(eid == sel, -jnp.inf, vals)
    es = [jnp.exp(t - tops[0]) for t in tops]
    tot = es[0] + es[1] + es[2] + es[3]
    idx_ref[...] = jnp.concatenate(ids, axis=0)
    wt_ref[...] = jnp.concatenate([e / tot for e in es], axis=0)
    hits = [eid == sel for sel in ids]
    member = jnp.zeros((n_exp, tm), F32)
    for hk in hits:
        member = member + jnp.where(hk, 1.0, 0.0)
    before = lax.broadcasted_iota(I32, (tm, tm), 0) < lax.broadcasted_iota(I32, (tm, tm), 1)
    prefix = jnp.dot(member.astype(BF16), jnp.where(before, 1.0, 0.0).astype(BF16), preferred_element_type=F32)
    rank_ref[...] = jnp.concatenate(
        [jnp.sum(jnp.where(hk, prefix, 0.0), axis=0, keepdims=True) for hk in hits], axis=0).astype(I32)
    cnt_ref[...] = jnp.broadcast_to(jnp.sum(member, axis=1, keepdims=True), (n_exp, LANES)).astype(I32)


def _post(x, om, og, gate, scale, shift, w_o, g_ffn, w_r2, b_r):
    t, d = x.shape
    tm = ROUTE_TILE
    per_tok = gate.shape[0] == t
    mod = pl.BlockSpec((tm, d), lambda i: (i, 0)) if per_tok else pl.BlockSpec((1, d), lambda i: (0, 0))
    tok = lambda w: pl.BlockSpec((tm, w), lambda i: (i, 0))
    full = lambda a: pl.BlockSpec(a.shape, lambda i: (0,) * a.ndim)
    return pl.pallas_call(
        _post_body,
        grid=(t // tm,),
        in_specs=[tok(d), tok(om.shape[1]), tok(og.shape[1]), mod, mod, mod, full(w_o), full(g_ffn), full(w_r2),
                  full(b_r)],
        out_specs=(tok(d), tok(d),
                   pl.BlockSpec((TOP_K, tm), lambda i: (0, i)), pl.BlockSpec((TOP_K, tm), lambda i: (0, i)),
                   pl.BlockSpec((TOP_K, tm), lambda i: (0, i)),
                   pl.BlockSpec((None, N_EXPERTS, LANES), lambda i: (i, 0, 0))),
        out_shape=(jax.ShapeDtypeStruct((t, d), F32), jax.ShapeDtypeStruct((t, d), BF16),
                   jax.ShapeDtypeStruct((TOP_K, t), I32), jax.ShapeDtypeStruct((TOP_K, t), F32),
                   jax.ShapeDtypeStruct((TOP_K, t), I32), jax.ShapeDtypeStruct((t // tm, N_EXPERTS, LANES), I32)),
        compiler_params=_cparams(("arbitrary",), VMEM_SMALL),
        name="post",
    )(x, om, og, gate, scale, shift, w_o, g_ffn, w_r2, b_r)


def _route_tables(idx, lrank, cnt3):
    nt = cnt3.shape[0]
    t = idx.shape[1]
    cnt = cnt3[:, :, 0]
    run = _round_up(cnt, RUN_CHUNK)
    lo_end = jnp.cumsum(run, axis=1)
    lo = lo_end - run
    n_chunks = lo_end[:, -1] // RUN_CHUNK
    region = _round_up(jnp.sum(run, axis=0), EXPERT_ROWS)
    g_end = jnp.cumsum(region)
    run_dest = (g_end - region)[None, :] + jnp.cumsum(run, axis=0) - run
    max_rows = TOP_K * ROUTE_TILE + N_EXPERTS * (RUN_CHUNK - 1)
    n_tab = _round_up(max_rows, RUN_CHUNK) // RUN_CHUNK
    c_start = jnp.arange(n_tab, dtype=I32) * RUN_CHUNK
    e_of_c = jnp.minimum(jnp.sum(lo_end[:, None, :] <= c_start[None, :, None], axis=2), N_EXPERTS - 1)
    pick = e_of_c[:, :, None] == jnp.arange(N_EXPERTS, dtype=I32)[None, None, :]
    chunk_dest = jnp.sum(jnp.where(pick, (run_dest - lo)[:, None, :], 0), axis=2) + c_start[None, :]
    table = jnp.concatenate([chunk_dest, n_chunks[:, None]], axis=1).astype(I32).reshape(nt, 1, n_tab + 1)
    eid = jnp.arange(N_EXPERTS, dtype=I32)[:, None]
    lo_tok = jnp.repeat(lo.T, ROUTE_TILE, axis=1)
    lpos = jnp.stack([jnp.sum(jnp.where(idx[k][None, :] == eid, lo_tok, 0), axis=0) for k in range(TOP_K)])
    lpos = (lpos + lrank).astype(I32)
    n_blocks = _round_up(t * TOP_K + nt * N_EXPERTS * (RUN_CHUNK - 1), EXPERT_ROWS) // EXPERT_ROWS + N_EXPERTS
    b_start = jnp.arange(n_blocks, dtype=I32) * EXPERT_ROWS
    blk_e = jnp.minimum(jnp.sum(g_end[None, :] <= b_start[:, None], axis=1), N_EXPERTS - 1).astype(I32)
    n_valid = (g_end[-1:] // EXPERT_ROWS).astype(I32)
    used = region > 0
    e_ids = jnp.arange(N_EXPERTS, dtype=I32)
    later_used = used[None, :] & (e_ids[None, :] > e_ids[:, None])
    nxt_e = jnp.min(jnp.where(later_used, e_ids[None, :], N_EXPERTS), axis=1)
    nxt_e = jnp.where(nxt_e < N_EXPERTS, nxt_e, -1).astype(I32)
    ord_e = (jnp.cumsum(used.astype(I32)) - 1).astype(I32)
    tail = jnp.concatenate([jnp.where(used, g_end - EXPERT_ROWS, -1), n_valid]).astype(I32)
    tail = tail.reshape(1, N_EXPERTS + 1)
    return dict(table=table, lpos=lpos, blk_e=blk_e, n_valid=n_valid, nxt_e=nxt_e, ord_e=ord_e, tail=tail,
                n_blocks=n_blocks)


def _chunk_copy(src, dst, sem):
    return pltpu.make_async_copy(src, dst, sem)


def _for_row_blocks(n_rows, body):
    full = n_rows // SORT_ROWS
    rem = n_rows - full * SORT_ROWS
    tail = pl.multiple_of(full * SORT_ROWS, SORT_ROWS)

    def whole(rb, carry):
        body(pl.multiple_of(rb * SORT_ROWS, SORT_ROWS), SORT_ROWS)
        return carry

    lax.fori_loop(0, full, whole, 0)

    @pl.when(rem > SORT_ROWS // 2)
    def _():
        body(tail, SORT_ROWS)

    @pl.when((rem > 0) & (rem <= SORT_ROWS // 2))
    def _():
        body(tail, SORT_ROWS // 2)


def _for_chunks(n, body):
    groups = n // CHUNK_UNROLL

    def group(g, carry):
        for u in range(CHUNK_UNROLL):
            body(g * CHUNK_UNROLL + u)
        return carry

    def single(c, carry):
        body(c)
        return carry

    lax.fori_loop(0, groups, group, 0)
    lax.fori_loop(groups * CHUNK_UNROLL, n, single, 0)


def _scatter_body(n_tab, nt_a, n_blocks, tab_ref, prv_ref, tail_ref, lpos_ref, ha_ref, hb_ref, xout_ref, sorted_scr,
                  zero_scr, sems, zero_sem):
    i = pl.program_id(0)
    slot = i % 2
    n_chunks = tab_ref[0, n_tab]
    tt = ha_ref.shape[0]

    @pl.when(i == 0)
    def _():
        zero_scr[...] = jnp.zeros_like(zero_scr)
        n_valid = tail_ref[0, N_EXPERTS]

        def block(start):
            return xout_ref.at[pl.ds(pl.multiple_of(start, EXPERT_ROWS), EXPERT_ROWS)]

        for e in range(N_EXPERTS):
            @pl.when(tail_ref[0, e] >= 0)
            def _():
                _chunk_copy(zero_scr, block(tail_ref[0, e]), zero_sem).start()

        def fill(b, carry):
            _chunk_copy(zero_scr, block(b * EXPERT_ROWS), zero_sem).start()
            return carry

        def fill_done(b, carry):
            _chunk_copy(zero_scr, block(0), zero_sem).wait()
            return carry

        lax.fori_loop(n_valid, n_blocks, fill, 0)
        for e in range(N_EXPERTS):
            @pl.when(tail_ref[0, e] >= 0)
            def _():
                _chunk_copy(zero_scr, block(0), zero_sem).wait()
        lax.fori_loop(n_valid, n_blocks, fill_done, 0)

    lp16 = lpos_ref[...].astype(jnp.int16)
    h = jnp.where(i < nt_a, ha_ref[...], hb_ref[...])

    def sort_block(r0, size):
        rid = (r0 + lax.broadcasted_iota(I32, (size, tt), 0)).astype(jnp.int16)
        onehot = jnp.zeros((size, tt), BF16)
        for k in range(TOP_K):
            onehot = onehot + jnp.where(lp16[k:k + 1, :] == rid, jnp.ones((), BF16), jnp.zeros((), BF16))
        sorted_scr[slot, pl.ds(r0, size), :] = jnp.dot(onehot, h, preferred_element_type=F32).astype(BF16)

    _for_row_blocks(n_chunks * RUN_CHUNK, sort_block)

    def issue(c):
        src = sorted_scr.at[slot, pl.ds(pl.multiple_of(c * RUN_CHUNK, RUN_CHUNK), RUN_CHUNK)]
        dst = xout_ref.at[pl.ds(pl.multiple_of(tab_ref[0, c], RUN_CHUNK), RUN_CHUNK)]
        _chunk_copy(src, dst, sems.at[slot]).start()

    def drain(sl):
        def body(c):
            _chunk_copy(sorted_scr.at[sl, pl.ds(0, RUN_CHUNK)], xout_ref.at[pl.ds(0, RUN_CHUNK)], sems.at[sl]).wait()
        return body

    _for_chunks(n_chunks, issue)

    @pl.when(i > 0)
    def _():
        _for_chunks(prv_ref[0, n_tab], drain(1 - slot))

    @pl.when(i == pl.num_programs(0) - 1)
    def _():
        _for_chunks(n_chunks, drain(slot))


def _scatter(table, tail, lpos, h_a, h_b, n_blocks):
    d = h_a.shape[1]
    nt_a, nt_b = h_a.shape[0] // ROUTE_TILE, h_b.shape[0] // ROUTE_TILE
    n_tab = table.shape[2] - 1
    sort_cap = _round_up(n_tab * RUN_CHUNK, SORT_ROWS)
    tab_spec = lambda f: pl.BlockSpec((None, 1, n_tab + 1), lambda i: (f(i), 0, 0), memory_space=pltpu.SMEM)
    return pl.pallas_call(
        functools.partial(_scatter_body, n_tab, nt_a, n_blocks),
        grid=(nt_a + nt_b,),
        in_specs=[tab_spec(lambda i: i), tab_spec(lambda i: jnp.maximum(i - 1, 0)),
                  pl.BlockSpec((1, N_EXPERTS + 1), lambda i: (0, 0), memory_space=pltpu.SMEM),
                  pl.BlockSpec((TOP_K, ROUTE_TILE), lambda i: (0, i)),
                  pl.BlockSpec((ROUTE_TILE, d), lambda i: (jnp.minimum(i, nt_a - 1), 0)),
                  pl.BlockSpec((ROUTE_TILE, d), lambda i: (jnp.maximum(i - nt_a, 0), 0))],
        out_specs=pl.BlockSpec(memory_space=pl.ANY),
        out_shape=jax.ShapeDtypeStruct((n_blocks * EXPERT_ROWS, d), BF16),
        scratch_shapes=[pltpu.VMEM((2, sort_cap, d), BF16), pltpu.VMEM((EXPERT_ROWS, d), BF16),
                        pltpu.SemaphoreType.DMA((2,)), pltpu.SemaphoreType.DMA(())],
        compiler_params=_cparams(("arbitrary",), VMEM_MID),
        name="scatter",
    )(table, table, tail, lpos, h_a, h_b)


def _experts_body(be_ref, nv_ref, nxt_ref, ord_ref, x_ref, wgu_hbm, bgu_ref, wd_hbm, bd_ref, y_ref,
                  wgu_f, wd_f, wgu_s, wd_s, sem_gu, sem_d):
    b = pl.program_id(0)
    e = be_ref[b]
    prev = be_ref[jnp.maximum(b - 1, 0)]
    valid = b < nv_ref[0]
    d_ff = wd_s.shape[0]
    slot = ord_ref[e] % 2

    def weights(expert, sl):
        return (pltpu.make_async_copy(wgu_hbm.at[expert], wgu_f.at[sl], sem_gu.at[sl]),
                pltpu.make_async_copy(wd_hbm.at[expert], wd_f.at[sl], sem_d.at[sl]))

    @pl.when(valid & ((b == 0) | (e != prev)))
    def _():
        @pl.when(b == 0)
        def _():
            for cp in weights(e, slot):
                cp.start()

        for cp in weights(e, slot):
            cp.wait()

        @pl.when(nxt_ref[e] >= 0)
        def _():
            for cp in weights(nxt_ref[e], 1 - slot):
                cp.start()

        wgu_s[...] = wgu_f[slot].astype(BF16)
        wd_s[...] = wd_f[slot].astype(BF16)

    @pl.when(valid)
    def _():
        gu = jnp.dot(x_ref[...], wgu_s[...], preferred_element_type=F32) + bgu_ref[...]
        gate = jnp.minimum(gu[:, 0:d_ff], SWIGLU_LIMIT)
        up = jnp.clip(gu[:, d_ff:2 * d_ff], -SWIGLU_LIMIT, SWIGLU_LIMIT)
        act = ((up + 1.0) * (gate * jax.nn.sigmoid(gate * SWIGLU_ALPHA))).astype(BF16)
        y_ref[...] = (jnp.dot(act, wd_s[...], preferred_element_type=F32) + bd_ref[...]).astype(BF16)

    @pl.when(jnp.logical_not(valid))
    def _():
        y_ref[...] = jnp.zeros_like(y_ref)


def _experts(blk_e, n_valid, nxt_e, ord_e, xpad, w_gu, b_gu, w_down, b_down):
    m, d = xpad.shape
    nb = m // EXPERT_ROWS
    n_exp, _, f2 = w_gu.shape
    d_ff = w_down.shape[1]
    last = lambda b, be, nv: jnp.minimum(b, nv[0] - 1)
    grid_spec = pltpu.PrefetchScalarGridSpec(
        num_scalar_prefetch=4,
        grid=(nb,),
        in_specs=[pl.BlockSpec((EXPERT_ROWS, d), lambda b, be, nv, nx, od: (last(b, be, nv), 0)),
                  pl.BlockSpec(memory_space=pl.ANY),
                  pl.BlockSpec((None, 1, f2), lambda b, be, nv, nx, od: (be[last(b, be, nv)], 0, 0)),
                  pl.BlockSpec(memory_space=pl.ANY),
                  pl.BlockSpec((None, 1, d), lambda b, be, nv, nx, od: (be[last(b, be, nv)], 0, 0))],
        out_specs=pl.BlockSpec((EXPERT_ROWS, d), lambda b, be, nv, nx, od: (b, 0)),
        scratch_shapes=[pltpu.VMEM((2, d, f2), F32), pltpu.VMEM((2, d_ff, d), F32),
                        pltpu.VMEM((d, f2), BF16), pltpu.VMEM((d_ff, d), BF16),
                        pltpu.SemaphoreType.DMA((2,)), pltpu.SemaphoreType.DMA((2,))],
    )
    return pl.pallas_call(
        _experts_body,
        grid_spec=grid_spec,
        out_shape=jax.ShapeDtypeStruct((m, d), BF16),
        compiler_params=_cparams(("arbitrary",), VMEM_BIG),
        name="experts",
    )(blk_e, n_valid, nxt_e, ord_e, xpad, w_gu, b_gu.reshape(n_exp, 1, f2), w_down, b_down.reshape(n_exp, 1, d))


def _combine_body(n_tab, tab_ref, nxt_ref, lpos_ref, wt_ref, x2_ref, gt_ref, y_ref, o_ref, ysort_scr, sems):
    i = pl.program_id(0)
    slot = i % 2
    n_chunks = tab_ref[0, n_tab]
    tt = x2_ref.shape[0]

    def fetch(tab, sl):
        def issue(c):
            src = y_ref.at[pl.ds(pl.multiple_of(tab[0, c], RUN_CHUNK), RUN_CHUNK)]
            dst = ysort_scr.at[sl, pl.ds(pl.multiple_of(c * RUN_CHUNK, RUN_CHUNK), RUN_CHUNK)]
            _chunk_copy(src, dst, sems.at[sl]).start()

        _for_chunks(tab[0, n_tab], issue)

    @pl.when(i == 0)
    def _():
        ysort_scr[...] = jnp.zeros_like(ysort_scr)
        fetch(tab_ref, 0)

    @pl.when(i + 1 < pl.num_programs(0))
    def _():
        fetch(nxt_ref, 1 - slot)

    def drain(c):
        _chunk_copy(y_ref.at[pl.ds(0, RUN_CHUNK)], ysort_scr.at[slot, pl.ds(0, RUN_CHUNK)], sems.at[slot]).wait()

    _for_chunks(n_chunks, drain)

    lp16 = lpos_ref[...].astype(jnp.int16)
    w = wt_ref[...].astype(BF16)
    o_ref[...] = x2_ref[...]
    gt = gt_ref[...]

    def gather_block(r0, size):
        rid = (r0 + lax.broadcasted_iota(I32, (size, tt), 0)).astype(jnp.int16)
        pw_t = jnp.zeros((size, tt), BF16)
        for k in range(TOP_K):
            pw_t = pw_t + jnp.where(lp16[k:k + 1, :] == rid, w[k:k + 1, :], jnp.zeros((), BF16))
        ys = ysort_scr[slot, pl.ds(r0, size), :]
        part = lax.dot_general(pw_t, ys, (((0,), (0,)), ((), ())), preferred_element_type=F32)
        o_ref[...] = o_ref[...] + gt * part

    _for_row_blocks(n_chunks * RUN_CHUNK, gather_block)


def _combine(table, lpos, wts, x2, gate, ypad):
    t, d = x2.shape
    nt = t // ROUTE_TILE
    n_tab = table.shape[2] - 1
    per_tok = gate.shape[0] == t
    mod = pl.BlockSpec((ROUTE_TILE, d), lambda i: (i, 0)) if per_tok else pl.BlockSpec((1, d), lambda i: (0, 0))
    sort_cap = _round_up(n_tab * RUN_CHUNK, SORT_ROWS)
    tab_spec = lambda f: pl.BlockSpec((None, 1, n_tab + 1), lambda i: (f(i), 0, 0), memory_space=pltpu.SMEM)
    return pl.pallas_call(
        functools.partial(_combine_body, n_tab),
        grid=(nt,),
        in_specs=[tab_spec(lambda i: i), tab_spec(lambda i: jnp.minimum(i + 1, nt - 1)),
                  pl.BlockSpec((TOP_K, ROUTE_TILE), lambda i: (0, i)),
                  pl.BlockSpec((TOP_K, ROUTE_TILE), lambda i: (0, i)),
                  pl.BlockSpec((ROUTE_TILE, d), lambda i: (i, 0)),
                  mod,
                  pl.BlockSpec(memory_space=pl.ANY)],
        out_specs=pl.BlockSpec((ROUTE_TILE, d), lambda i: (i, 0)),
        out_shape=jax.ShapeDtypeStruct((t, d), F32),
        scratch_shapes=[pltpu.VMEM((2, sort_cap, d), BF16), pltpu.SemaphoreType.DMA((2,))],
        compiler_params=_cparams(("arbitrary",), VMEM_MID),
        name="combine",
    )(table, table, lpos, wts, x2, gate, ypad)


def _prep_weights(g_norm_mix, w_in, g_q_a, w_uq, g_kv_a, w_ukv, g_qk_q, g_qk_k, w_g2, b_g2, g_gla_out, w_o,
                  g_norm_ffn, w_router, b_router):
    d = w_in.shape[0]
    o_qa, o_kva, o_kr = 0, Q_LORA, Q_LORA + KV_LORA
    o_gq = o_kr + ROPE
    o_gk = o_gq + HEADS * GLA_DK
    o_gv = o_gk + HEADS * GLA_DK
    o_glr = o_gv + HEADS * GLA_DV
    o_gr = o_glr + GATE_RANK
    kr_cols = w_in[:, o_kr:o_kr + ROPE]
    w1 = jnp.concatenate([
        w_in[:, o_qa:o_kva], kr_cols, kr_cols, w_in[:, o_kva:o_kr], w_in[:, o_gq:o_gk], w_in[:, o_gk:o_gv],
        w_in[:, o_gv:o_glr], w_in[:, o_gr:o_gr + HEADS * GLA_DV], w_in[:, o_glr:o_gr],
        jnp.zeros((d, LANES - GATE_RANK), w_in.dtype)], axis=1).astype(BF16)
    assert w1.shape[1] == _W1_COLS
    wq = w_uq.reshape(Q_LORA, HEADS, QK)
    wuq = jnp.concatenate([wq[:, :, 0:NOPE].reshape(Q_LORA, HEADS * NOPE),
                           wq[:, :, NOPE:QK].reshape(Q_LORA, HEADS * ROPE)], axis=1).astype(BF16)
    wkv = w_ukv.reshape(KV_LORA, HEADS, NOPE + V_DIM)
    wuk = wkv[:, :, 0:NOPE].reshape(KV_LORA, HEADS * NOPE).astype(BF16)
    wuv_t = wkv[:, :, NOPE:].reshape(KV_LORA, HEADS * V_DIM).T.astype(BF16)
    pad_rope = lambda g: jnp.stack([g[0:NOPE], jnp.concatenate([g[NOPE:QK], jnp.zeros((QK_PAD - QK,), g.dtype)])])
    inv = ROPE_THETA ** (-jnp.arange(HALF, dtype=F32) / HALF)
    sign = jnp.concatenate([-jnp.ones((HALF,), F32), jnp.ones((HALF,), F32)])
    rope_tab = jnp.stack([jnp.tile(inv, LANES // HALF), jnp.tile(sign, LANES // ROPE)])
    wg2 = jnp.concatenate([w_g2, jnp.zeros((LANES - GATE_RANK, w_g2.shape[1]), w_g2.dtype)], axis=0).astype(BF16)
    wr_t = w_router.T
    wr_hi = wr_t.astype(BF16)
    wr_lo = (wr_t - wr_hi.astype(F32)).astype(BF16)
    return dict(
        g_mix=g_norm_mix.reshape(1, d), w1=w1, g_qa=g_q_a.reshape(1, -1), w_uq=wuq, g_kv=g_kv_a.reshape(1, -1),
        w_uk=wuk, w_uv_t=wuv_t, g_qk_q=pad_rope(g_qk_q), g_qk_k=pad_rope(g_qk_k), rope=rope_tab, w_g2=wg2,
        b_g2=b_g2.reshape(1, -1), g_out=g_gla_out.reshape(1, -1), w_o=w_o.astype(BF16),
        g_ffn=g_norm_ffn.reshape(1, d), w_r2=jnp.stack([wr_hi, wr_lo]), b_r=b_router.reshape(-1, 1))


def _mixer(x, mod, pos0, past_lat, past_kr, s0_pairs, wts):
    b, s, d = x.shape
    t = b * s
    if b > 1 and s & (s - 1) == 0 and min(t, TOKEN_TILE) % s == 0:
        per_tok = lambda j: jnp.broadcast_to(mod[:, j:j + 1], (b, s, d)).reshape(1, t, d)
        outs = _proj(x.reshape(1, t, d), per_tok(0), per_tok(1), pos0, wts, period=s)
        q = outs[0].reshape(HEADS, b, s, QK_PAD).transpose(1, 0, 2, 3)
        lat, kr, gq, gk, gv, gl, gr = [o.reshape(b, s, o.shape[-1]) for o in outs[1:]]
    else:
        q, lat, kr, gq, gk, gv, gl, gr = _proj(x, mod[:, 0:1], mod[:, 1:2], pos0, wts)
    kv_w = (wts["w_uk"], wts["w_uv_t"], wts["g_qk_k"])
    if past_lat is None:
        k_new, vt_new = _kv(lat, kr, *kv_w)
        o_mla = _attn_prompt(q, k_new, vt_new)
    else:
        o_mla = _attn_sample(q, past_lat, past_kr, lat, kr, *kv_w)
    o_gla, s_fin = _gla(gq, gk, gl, gv, gr, s0_pairs, wts["g_out"])
    if b == 1:
        rows = lambda j: mod[0, j:j + 1]
    else:
        rows = lambda j: jnp.broadcast_to(mod[:, j:j + 1], (b, s, d)).reshape(t, d)
    x2, h2, idx, wt, lrank, cnt = _post(x.reshape(t, d), o_mla.reshape(t, -1), o_gla.reshape(t, -1), rows(2), rows(4),
                                        rows(3), wts["w_o"], wts["g_ffn"], wts["w_r2"], wts["b_r"])
    return dict(x2=x2, h2=h2, idx=idx, wt=wt, lrank=lrank, cnt=cnt, gate_f=rows(5), lat=lat, kr=kr, s_fin=s_fin)


def kernel(x_prompt, x_sample, cache_mla_latent, cache_mla_krope, state_gla, c_prompt, c_sample, w_ada, b_ada, g_norm_mix, w_in, g_q_a, w_uq, g_kv_a, w_ukv, g_qk_q, g_qk_k, w_g2, b_g2, g_gla_out, w_o, g_norm_ffn, w_router, b_router, w_gu, b_gu, w_down, b_down):
    depth = w_ada.shape[0]
    assert depth == 1, "single-layer step"
    bp, sp, d = x_prompt.shape
    bs, ss, _ = x_sample.shape
    tp, tsm = bp * sp, bs * ss
    assert tp % ROUTE_TILE == 0 and tsm % ROUTE_TILE == 0, "token counts must be whole routing tiles"
    past = cache_mla_latent.shape[2]
    layer = lambda a: a.reshape(a.shape[1:])
    wts = _prep_weights(*[layer(a) for a in (g_norm_mix, w_in, g_q_a, w_uq, g_kv_a, w_ukv, g_qk_q, g_qk_k, w_g2, b_g2,
                                             g_gla_out, w_o, g_norm_ffn, w_router, b_router)])
    w_gu, b_gu, w_down, b_down = layer(w_gu), layer(b_gu), layer(w_down), layer(b_down)

    mod = _ada(jnp.concatenate([c_prompt, c_sample], axis=0), layer(w_ada), layer(b_ada)).reshape(bp + bs, 6, d)
    zero_state = jnp.zeros((bp, HEADS // 2, GLA_DV, LANES), F32)
    pr = _mixer(x_prompt, mod[:bp], 0, None, None, zero_state, wts)
    sa = _mixer(x_sample, mod[bp:], past, layer(cache_mla_latent), layer(cache_mla_krope),
                _state_to_pairs(layer(state_gla)), wts)

    idx = jnp.concatenate([pr["idx"], sa["idx"]], axis=1)
    lrank = jnp.concatenate([pr["lrank"], sa["lrank"]], axis=1)
    rt = _route_tables(idx, lrank, jnp.concatenate([pr["cnt"], sa["cnt"]], axis=0))
    ntp = tp // ROUTE_TILE
    lpos, table = rt["lpos"], rt["table"]
    xpad = _scatter(table, rt["tail"], lpos, pr["h2"], sa["h2"], rt["n_blocks"])
    ypad = _experts(rt["blk_e"], rt["n_valid"], rt["nxt_e"], rt["ord_e"], xpad, w_gu, b_gu, w_down, b_down)
    y_p = _combine(table[:ntp], lpos[:, :tp], pr["wt"], pr["x2"], pr["gate_f"], ypad).reshape(bp, sp, d)
    y_s = _combine(table[ntp:], lpos[:, tp:], sa["wt"], sa["x2"], sa["gate_f"], ypad).reshape(bs, ss, d)

    return (y_p, y_s,
            pr["lat"][None], pr["kr"][None], _state_from_pairs(pr["s_fin"])[None],
            sa["lat"][None], sa["kr"][None], _state_from_pairs(sa["s_fin"])[None])
```

```python
import functools

import numpy as np
import jax
import jax.numpy as jnp
from jax import lax
from jax.experimental import pallas as pl
from jax.experimental.pallas import tpu as pltpu

F32 = jnp.float32
BF16 = jnp.bfloat16
I32 = jnp.int32

CHUNK = 64
EPS = 1e-6
HEADS = 4
Q_LORA = 384
KV_LORA = 256
NOPE = 128
ROPE = 64
HALF = ROPE // 2
V_DIM = 128
QK = NOPE + ROPE
QK_PAD = 256
ROPE_THETA = 10000.0
GLA_DK = 64
GLA_DV = 128
GATE_RANK = 16
GATE_NORM = 16.0
N_EXPERTS = 32
TOP_K = 4
SWIGLU_LIMIT = 7.0
SWIGLU_ALPHA = 1.702
NEG = -1e30
LOG2_E = 1.4426950408889634

LANES = 128
SUBLANES = 8
BF16_ROWS = 16
TOKEN_TILE = 512
ADA_COLS = 1536
ROUTE_TILE = 512
RUN_CHUNK = BF16_ROWS
BIG_PIECE = 2 * RUN_CHUNK
SORT_ROWS = 512
CHUNK_UNROLL = 4
EXPERT_ROWS = 256
ATTN_TILE = 1024
MIB = 1024 * 1024
VMEM_BIG = 56 * MIB
VMEM_MID = 48 * MIB
VMEM_SMALL = 40 * MIB


def _cparams(sem, vmem=None):
    return pltpu.CompilerParams(dimension_semantics=sem, vmem_limit_bytes=vmem)


def _nt(a, b):
    return lax.dot_general(a, b, (((1,), (1,)), ((), ())), preferred_element_type=F32)


def _rms(x, width):
    return lax.rsqrt(jnp.sum(x * x, axis=-1, keepdims=True) * (1.0 / width) + EPS)


def _round_up(x, m):
    return ((x + m - 1) // m) * m


def _ada_body(c_ref, w_ref, b_ref, o_ref):
    c = c_ref[...]
    s = (c * jax.nn.sigmoid(c)).astype(BF16)
    o_ref[...] = jnp.dot(s, w_ref[...].astype(BF16), preferred_element_type=F32) + b_ref[...]


def _ada(c, w_ada, b_ada):
    r, d = c.shape
    n = w_ada.shape[1]
    tn = ADA_COLS if n % ADA_COLS == 0 else n
    return pl.pallas_call(
        _ada_body,
        grid=(n // tn,),
        in_specs=[pl.BlockSpec((r, d), lambda j: (0, 0)),
                  pl.BlockSpec((d, tn), lambda j: (0, j)),
                  pl.BlockSpec((1, tn), lambda j: (0, j))],
        out_specs=pl.BlockSpec((r, tn), lambda j: (0, j)),
        out_shape=jax.ShapeDtypeStruct((r, n), F32),
        compiler_params=_cparams(("arbitrary",), VMEM_SMALL),
        name="ada",
    )(c, w_ada, b_ada.reshape(1, n))


_SEG = dict(qa_kr=(0, 512), kva=(512, 768), gq=(768, 1024), gk=(1024, 1280),
            gv=(1280, 1792), gr=(1792, 2304), glr=(2304, 2432))
_W1_COLS = 2432


def _proj_body(pos0, ts, period, x_ref, sh_ref, sc_ref, gmix_ref, w1_ref, gqa_ref, wuq_ref, gkv_ref, gqk_ref,
               rope_ref, wg2_ref, bg2_ref,
               q_ref, lat_ref, kr_ref, gq_o, gk_o, gv_o, gl_o, gr_o, trig_scr):
    i = pl.program_id(1)
    x = x_ref[...]
    d = x.shape[-1]
    h = (x * _rms(x, d) * gmix_ref[...]) * (1.0 + sc_ref[...]) + sh_ref[...]
    hb = h.astype(BF16)

    def seg(name):
        a, b = _SEG[name]
        return jnp.dot(hb, w1_ref[:, a:b], preferred_element_type=F32)

    @pl.when((pl.program_id(0) == 0) & (i == 0))
    def _():
        row = lax.broadcasted_iota(I32, (ts, LANES), 0)
        if period is not None:
            row = row & (period - 1)
        row_ang = row.astype(F32) * rope_ref[0:1, :]
        trig_scr[0] = jnp.cos(row_ang)
        trig_scr[1] = jnp.sin(row_ang)

    tile_pos = pos0 + (i * ts if period is None else 0 * i)
    base_ang = jnp.broadcast_to(tile_pos.astype(F32) * rope_ref[0:1, :], (SUBLANES, LANES))
    cos_a, sin_a = jnp.cos(base_ang)[0:1, :], jnp.sin(base_ang)[0:1, :]
    cos = cos_a * trig_scr[0] - sin_a * trig_scr[1]
    sin = (sin_a * trig_scr[0] + cos_a * trig_scr[1]) * rope_ref[1:2, :]
    lane = lax.broadcasted_iota(I32, (ts, LANES), 1)
    first_half = (lane & HALF) == 0
    low64 = lane < ROPE

    def rope(v):
        partner = jnp.where(first_half, pltpu.roll(v, LANES - HALF, 1), pltpu.roll(v, HALF, 1))
        return v * cos + partner * sin

    qa_kr = seg("qa_kr")
    qa = qa_kr[:, 0:Q_LORA]
    qn = (qa * _rms(qa, Q_LORA) * gqa_ref[...]).astype(BF16)
    qf = jnp.dot(qn, wuq_ref[...], preferred_element_type=F32)
    rope_blocks = (rope(qf[:, 4 * NOPE:4 * NOPE + LANES]), rope(qf[:, 4 * NOPE + LANES:4 * NOPE + 2 * LANES]))
    for hd in range(HEADS):
        nope = qf[:, NOPE * hd:NOPE * (hd + 1)]
        blk = rope_blocks[hd // 2]
        if hd % 2:
            blk = pltpu.roll(blk, ROPE, 1)
        blk = jnp.where(low64, blk, 0.0)
        ss = jnp.sum(nope * nope, axis=-1, keepdims=True) + jnp.sum(blk * blk, axis=-1, keepdims=True)
        scl = lax.rsqrt(ss * (1.0 / QK) + EPS) * (QK ** -0.5 * LOG2_E)
        q_ref[hd, :, 0:NOPE] = (nope * scl * gqk_ref[0:1, :]).astype(BF16)
        q_ref[hd, :, NOPE:QK_PAD] = (blk * scl * gqk_ref[1:2, :]).astype(BF16)

    kva = seg("kva")
    lat_ref[...] = kva * _rms(kva, KV_LORA) * gkv_ref[...]
    kr_ref[...] = rope(qa_kr[:, Q_LORA:Q_LORA + LANES])[:, 0:ROPE]

    gq_o[...] = seg("gq") * (GLA_DK ** -0.5)
    gk_o[...] = seg("gk")
    gv_o[...] = seg("gv").astype(BF16)
    gr_o[...] = seg("gr")
    z = jnp.dot(seg("glr").astype(BF16), wg2_ref[...], preferred_element_type=F32) + bg2_ref[...]
    gl_o[...] = (jnp.minimum(z, 0.0) - jnp.log1p(jnp.exp(-jnp.abs(z)))) * (1.0 / GATE_NORM)


def _proj(x, shift, scale, pos0, wts, period=None):
    b, s, d = x.shape
    ts = min(s, TOKEN_TILE)
    assert period is None or (period & (period - 1) == 0 and ts % period == 0)
    row = lambda a: pl.BlockSpec(a.shape, lambda bi, i: (0,) * a.ndim)
    tok = lambda w: pl.BlockSpec((None, ts, w), lambda bi, i: (bi, i, 0))
    mod = tok(d) if shift.shape[1] == s and s > 1 else pl.BlockSpec((None, 1, d), lambda bi, i: (bi, 0, 0))
    small = [wts["g_mix"], wts["w1"], wts["g_qa"], wts["w_uq"], wts["g_kv"], wts["g_qk_q"], wts["rope"],
             wts["w_g2"], wts["b_g2"]]
    out_shape = (
        jax.ShapeDtypeStruct((b, HEADS, s, QK_PAD), BF16),
        jax.ShapeDtypeStruct((b, s, KV_LORA), F32),
        jax.ShapeDtypeStruct((b, s, ROPE), F32),
        jax.ShapeDtypeStruct((b, s, HEADS * GLA_DK), F32),
        jax.ShapeDtypeStruct((b, s, HEADS * GLA_DK), F32),
        jax.ShapeDtypeStruct((b, s, HEADS * GLA_DV), BF16),
        jax.ShapeDtypeStruct((b, s, HEADS * GLA_DK), F32),
        jax.ShapeDtypeStruct((b, s, HEADS * GLA_DV), F32),
    )
    out_specs = (
        pl.BlockSpec((None, HEADS, ts, QK_PAD), lambda bi, i: (bi, 0, i, 0)),
        tok(KV_LORA), tok(ROPE), tok(HEADS * GLA_DK), tok(HEADS * GLA_DK), tok(HEADS * GLA_DV),
        tok(HEADS * GLA_DK), tok(HEADS * GLA_DV),
    )
    return pl.pallas_call(
        functools.partial(_proj_body, pos0, ts, period),
        grid=(b, s // ts),
        in_specs=[tok(d), mod, mod] + [row(a) for a in small],
        out_specs=out_specs,
        out_shape=out_shape,
        scratch_shapes=[pltpu.VMEM((2, ts, LANES), F32)],
        compiler_params=_cparams(("arbitrary", "arbitrary"), VMEM_BIG),
        name="proj",
    )(x, shift, scale, *small)


def _key_rows(lat, kr, wk_ref, gk_ref, k_out):
    kn_all = jnp.dot(lat, wk_ref[...], preferred_element_type=F32)
    kr_ss = jnp.sum(kr * kr, axis=-1, keepdims=True)
    for hd in range(HEADS):
        kn = kn_all[:, NOPE * hd:NOPE * (hd + 1)]
        scl = lax.rsqrt((jnp.sum(kn * kn, axis=-1, keepdims=True) + kr_ss) * (1.0 / QK) + EPS)
        k_out[hd, :, 0:NOPE] = (kn * scl * gk_ref[0:1, :]).astype(BF16)
        k_out[hd, :, NOPE:QK] = (kr * scl * gk_ref[1:2, 0:ROPE]).astype(BF16)
        k_out[hd, :, QK:QK_PAD] = jnp.zeros((kr.shape[0], QK_PAD - QK), BF16)


def _kv_body(lat_ref, kr_ref, wk_ref, wv_ref, gk_ref, k_ref, v_ref):
    lat = lat_ref[...].astype(BF16)
    _key_rows(lat, kr_ref[...], wk_ref, gk_ref, k_ref)
    v_t = _nt(wv_ref[...], lat)
    for hd in range(HEADS):
        v_ref[hd] = v_t[V_DIM * hd:V_DIM * (hd + 1), :].astype(BF16)


def _kv(lat, kr, w_uk, w_uv_t, g_qk_k):
    b, s, _ = lat.shape
    ts = min(s, ATTN_TILE)
    return pl.pallas_call(
        _kv_body,
        grid=(b, s // ts),
        in_specs=[pl.BlockSpec((None, ts, KV_LORA), lambda bi, i: (bi, i, 0)),
                  pl.BlockSpec((None, ts, ROPE), lambda bi, i: (bi, i, 0)),
                  pl.BlockSpec(w_uk.shape, lambda bi, i: (0, 0)),
                  pl.BlockSpec(w_uv_t.shape, lambda bi, i: (0, 0)),
                  pl.BlockSpec(g_qk_k.shape, lambda bi, i: (0, 0))],
        out_specs=(pl.BlockSpec((None, HEADS, ts, QK_PAD), lambda bi, i: (bi, 0, i, 0)),
                   pl.BlockSpec((None, HEADS, None, V_DIM, ts), lambda bi, i: (bi, 0, i, 0, 0))),
        out_shape=(jax.ShapeDtypeStruct((b, HEADS, s, QK_PAD), BF16),
                   jax.ShapeDtypeStruct((b, HEADS, s // ts, V_DIM, ts), BF16)),
        compiler_params=_cparams(("arbitrary", "arbitrary")),
        name="kv",
    )(lat, kr, w_uk, w_uv_t, g_qk_k)


def _attn_prompt_body(t, q_ref, qn_ref, k_ref, vt_ref, o_ref, s_a, s_b):
    i = pl.program_id(2)

    def scores(q, j, buf):
        buf[...] = _nt(k_ref[pl.ds(pl.multiple_of(j * t, t), t), :], q)

    def consume(j, buf, carry, masked=False):
        m, l, acc = carry
        s = buf[...]
        if masked:
            visible = (lax.broadcasted_iota(I32, (t, t), 0) // CHUNK) <= (lax.broadcasted_iota(I32, (t, t), 1) // CHUNK)
            s = jnp.where(visible, s, NEG)
        m_new = jnp.maximum(m, jnp.max(s, axis=0, keepdims=True))
        alpha = jnp.exp2(m - m_new)
        p = jnp.exp2(s - m_new)
        l = alpha * l + jnp.sum(p, axis=0, keepdims=True)
        acc = alpha * acc + jnp.dot(vt_ref[j], p.astype(BF16), preferred_element_type=F32)
        return m_new, l, acc

    def run(first, second):
        q = q_ref[...]

        @pl.when(i == 0)
        def _():
            scores(q, 0, first)

        def pair(pp, carry):
            j = 2 * pp
            scores(q, j + 1, second)
            carry = consume(j, first, carry)
            scores(q, j + 2, first)
            return consume(j + 1, second, carry)

        def even_tail(carry):
            scores(qn_ref[...], 0, second)
            return consume(i, first, carry, masked=True)

        def odd_tail(carry):
            scores(q, i, second)
            carry = consume(i - 1, first, carry)
            scores(qn_ref[...], 0, first)
            return consume(i, second, carry, masked=True)

        carry = (jnp.full((1, t), NEG, F32), jnp.zeros((1, t), F32), jnp.zeros((V_DIM, t), F32))
        carry = lax.fori_loop(0, i // 2, pair, carry)
        _, l, acc = lax.cond(i % 2 == 1, odd_tail, even_tail, carry)
        o_ref[...] = (acc / l).T.astype(BF16)

    @pl.when(((i + 1) // 2) % 2 == 0)
    def _():
        run(s_a, s_b)

    @pl.when(((i + 1) // 2) % 2 == 1)
    def _():
        run(s_b, s_a)


def _attn_prompt(q, k, v_t):
    b, _, s, _ = q.shape
    t = v_t.shape[-1]
    nq = s // t
    return pl.pallas_call(
        functools.partial(_attn_prompt_body, t),
        grid=(b, HEADS, nq),
        in_specs=[pl.BlockSpec((None, None, t, QK_PAD), lambda bi, h, i: (bi, h, i, 0)),
                  pl.BlockSpec((None, None, t, QK_PAD), lambda bi, h, i: (bi, h, jnp.minimum(i + 1, nq - 1), 0)),
                  pl.BlockSpec((None, None, s, QK_PAD), lambda bi, h, i: (bi, h, 0, 0)),
                  pl.BlockSpec((None, None, nq, V_DIM, t), lambda bi, h, i: (bi, h, 0, 0, 0))],
        out_specs=pl.BlockSpec((None, t, V_DIM), lambda bi, h, i: (bi, i, h)),
        out_shape=jax.ShapeDtypeStruct((b, s, HEADS * V_DIM), BF16),
        scratch_shapes=[pltpu.VMEM((t, t), F32), pltpu.VMEM((t, t), F32)],
        compiler_params=_cparams(("arbitrary", "arbitrary", "arbitrary"), VMEM_BIG),
        name="attn_prompt",
    )(q, q, k, v_t)


def _attn_sample_body(past, sq, q_ref, plat_ref, pkr_ref, nlat_ref, nkr_ref, wk_ref, wv_ref, gk_ref, o_ref,
                      kp_scr, kn_scr):
    plat = plat_ref[...].astype(BF16)
    nlat = nlat_ref[...].astype(BF16)
    _key_rows(plat, pkr_ref[...], wk_ref, gk_ref, kp_scr)
    _key_rows(nlat, nkr_ref[...], wk_ref, gk_ref, kn_scr)
    vp_t = _nt(wv_ref[...], plat).astype(BF16)
    vn_t = _nt(wv_ref[...], nlat).astype(BF16)
    hq = HEADS * sq
    col = lax.broadcasted_iota(I32, (sq, hq), 1)
    key_chunk = (past + lax.broadcasted_iota(I32, (sq, hq), 0)) // CHUNK
    qry_chunk = (past + col % sq) // CHUNK
    qrow = lax.broadcasted_iota(I32, (hq, 1), 0) // sq
    q_all = jnp.concatenate([q_ref[hd] for hd in range(HEADS)], axis=0)
    s_p = jnp.zeros((past, hq), F32)
    s_n = jnp.zeros((sq, hq), F32)
    for hd in range(HEADS):
        q_h = jnp.where(qrow == hd, q_all, jnp.zeros((), BF16))
        s_p = s_p + _nt(kp_scr[hd], q_h)
        s_n = s_n + _nt(kn_scr[hd], q_h)
    s_n = jnp.where(key_chunk <= qry_chunk, s_n, NEG)
    m = jnp.maximum(jnp.max(s_p, axis=0, keepdims=True), jnp.max(s_n, axis=0, keepdims=True))
    p_p = jnp.exp2(s_p - m)
    p_n = jnp.exp2(s_n - m)
    inv_l = 1.0 / (jnp.sum(p_p, axis=0, keepdims=True) + jnp.sum(p_n, axis=0, keepdims=True))
    p_p = p_p.astype(BF16)
    p_n = p_n.astype(BF16)
    lane_head = lax.broadcasted_iota(I32, (V_DIM, hq), 1) // sq
    o_t = jnp.zeros((V_DIM, hq), F32)
    for hd in range(HEADS):
        rows = slice(V_DIM * hd, V_DIM * (hd + 1))
        o_h = (jnp.dot(vp_t[rows, :], p_p, preferred_element_type=F32)
               + jnp.dot(vn_t[rows, :], p_n, preferred_element_type=F32))
        o_t = o_t + jnp.where(lane_head == hd, o_h, 0.0)
    o_all = (o_t * inv_l).T
    for hd in range(HEADS):
        o_ref[:, V_DIM * hd:V_DIM * (hd + 1)] = o_all[sq * hd:sq * (hd + 1), :].astype(BF16)


def _attn_sample(q, past_lat, past_kr, lat, kr, w_uk, w_uv_t, g_qk_k):
    b, _, sq, _ = q.shape
    past = past_lat.shape[1]
    rows = lambda n, w: pl.BlockSpec((None, n, w), lambda bi: (bi, 0, 0))
    full = lambda a: pl.BlockSpec(a.shape, lambda bi: (0,) * a.ndim)
    return pl.pallas_call(
        functools.partial(_attn_sample_body, past, sq),
        grid=(b,),
        in_specs=[pl.BlockSpec((None, HEADS, sq, QK_PAD), lambda bi: (bi, 0, 0, 0)),
                  rows(past, KV_LORA), rows(past, ROPE), rows(sq, KV_LORA), rows(sq, ROPE),
                  full(w_uk), full(w_uv_t), full(g_qk_k)],
        out_specs=rows(sq, HEADS * V_DIM),
        out_shape=jax.ShapeDtypeStruct((b, sq, HEADS * V_DIM), BF16),
        scratch_shapes=[pltpu.VMEM((HEADS, past, QK_PAD), BF16), pltpu.VMEM((HEADS, sq, QK_PAD), BF16)],
        compiler_params=_cparams(("arbitrary",), VMEM_SMALL),
        name="attn_sample",
    )(q, past_lat, past_kr, lat, kr, w_uk, w_uv_t, g_qk_k)


def _gla_masks(c, rows):
    idx = np.arange(rows)
    same = (idx // c)[:, None] == (idx // c)[None, :]
    le = same & (idx[None, :] <= idx[:, None])
    gt = same & (idx[None, :] > idx[:, None])
    return np.concatenate([le, gt], axis=0).astype(np.float32), int(np.log2(c))


def _level_exponents(b, g, c, level):
    n = c >> level
    rows = b.shape[0]
    row = lax.broadcasted_iota(I32, (rows, 1), 0)
    if n >= 8:
        split = b.reshape(rows // n, n, LANES)[:, n // 2 - 1:n // 2, :]
        split = jnp.broadcast_to(split, (rows // n, n, LANES)).reshape(rows, LANES)
        return jnp.where((row & (n // 2)) != 0, b - split, split - b)
    g_prev = pltpu.roll(g, 1, 0)
    g_next = pltpu.roll(g, rows - 1, 0)
    if n == 4:
        r = row & 3
        return jnp.where(r == 0, g_next, jnp.where(r == 1, 0.0, jnp.where(r == 2, g, g + g_prev)))
    assert n == 2
    return jnp.where((row & 1) != 0, g, 0.0)


def _gla_body(c, n_chunks, unit, levels, mall_ref, q_ref, k_ref, g_ref, v_ref, r_ref, s0_ref, gout_ref,
              o_ref, sfin_ref, st_scr):
    it = pl.program_id(1)

    @pl.when(it == 0)
    def _():
        st_scr[...] = s0_ref[...]

    ru = unit * c
    lane = lax.broadcasted_iota(I32, (ru, LANES), 1)
    head_lanes = (lane < GLA_DK, lane >= GLA_DK)
    st_lane_lo = lax.broadcasted_iota(I32, (GLA_DV, LANES), 1) < GLA_DK
    row = lax.broadcasted_iota(I32, (ru, 1), 0)
    ri = lax.broadcasted_iota(I32, (ru, ru), 0)
    ci = lax.broadcasted_iota(I32, (ru, ru), 1)
    mall = mall_ref[...]

    for un in range(n_chunks // unit):
        rows = slice(un * ru, (un + 1) * ru)
        for p in range(HEADS // 2):
            ls = slice(LANES * p, LANES * (p + 1))
            g = g_ref[rows, ls]
            q = q_ref[rows, ls]
            k = k_ref[rows, ls]
            g_hi = g.astype(BF16)
            g_lo = (g - g_hi.astype(F32)).astype(BF16)
            e2 = jnp.dot(mall, jnp.concatenate([g_hi, g_lo], axis=1), preferred_element_type=F32)
            e = e2[:, 0:LANES] + e2[:, LANES:2 * LANES]
            b = e[0:ru]
            eb = jnp.exp(b)
            qb = q * eb
            kd = (k * jnp.exp(e[ru:2 * ru])).astype(BF16)
            qs, ks = [q], [k.astype(BF16)]
            for l in range(levels):
                bottom = (row & (c >> (l + 1))) != 0
                decay = jnp.exp(_level_exponents(b, g, c, l))
                qs.append(jnp.where(bottom, q * decay, 0.0))
                ks.append(jnp.where(bottom, 0.0, k * decay).astype(BF16))
            states = [st_scr[p]]
            for j in range(unit):
                cr = slice(j * c, (j + 1) * c)
                upd = [lax.dot_general(v_ref[rows, GLA_DV * (2 * p + hh):GLA_DV * (2 * p + hh + 1)][cr, :], kd[cr, :],
                                       (((0,), (0,)), ((), ())), preferred_element_type=F32) for hh in range(2)]
                d_last = eb[j * c + c - 1:j * c + c, :]
                states.append(states[-1] * d_last + jnp.where(st_lane_lo, upd[0], upd[1]))
            st_scr[p] = states[-1]
            for hh in range(2):
                hd = 2 * p + hh
                sel = head_lanes[hh]
                a = jnp.where(ri == ci, _nt(jnp.where(sel, qs[0], 0.0).astype(BF16), ks[0]), 0.0)
                for l in range(levels):
                    pr = _nt(jnp.where(sel, qs[l + 1], 0.0).astype(BF16), ks[l + 1])
                    a = a + jnp.where((ri ^ ci) < (c >> l), pr, 0.0)
                vh = v_ref[rows, GLA_DV * hd:GLA_DV * (hd + 1)]
                qb_h = jnp.where(sel, qb, 0.0).astype(BF16)
                o_state = [_nt(qb_h[j * c:(j + 1) * c, :], states[j].astype(BF16)) for j in range(unit)]
                o = jnp.dot(a.astype(BF16), vh, preferred_element_type=F32) + jnp.concatenate(o_state, axis=0)
                on = o * _rms(o, GLA_DV) * gout_ref[...]
                r = r_ref[rows, GLA_DV * hd:GLA_DV * (hd + 1)]
                o_ref[rows, GLA_DV * hd:GLA_DV * (hd + 1)] = (on * (r * jax.nn.sigmoid(r))).astype(BF16)

    @pl.when(it == pl.num_programs(1) - 1)
    def _():
        sfin_ref[...] = st_scr[...]


def _gla(gq, gk, gl, gv, gr, s0, g_out):
    b, s, _ = gq.shape
    c = min(CHUNK, s)
    tile = min(s, 8 * c)
    unit = next(u for u in (4, 2, 1) if (tile // c) % u == 0)
    masks, levels = _gla_masks(c, unit * c)
    mall = jnp.asarray(masks, BF16)
    tok = lambda w: pl.BlockSpec((None, tile, w), lambda bi, i: (bi, i, 0))
    st_spec = pl.BlockSpec((None, HEADS // 2, GLA_DV, LANES), lambda bi, i: (bi, 0, 0, 0))
    return pl.pallas_call(
        functools.partial(_gla_body, c, tile // c, unit, levels),
        grid=(b, s // tile),
        in_specs=[pl.BlockSpec(mall.shape, lambda bi, i: (0, 0)),
                  tok(HEADS * GLA_DK), tok(HEADS * GLA_DK), tok(HEADS * GLA_DK), tok(HEADS * GLA_DV),
                  tok(HEADS * GLA_DV), st_spec, pl.BlockSpec(g_out.shape, lambda bi, i: (0, 0))],
        out_specs=(tok(HEADS * GLA_DV), st_spec),
        out_shape=(jax.ShapeDtypeStruct((b, s, HEADS * GLA_DV), BF16),
                   jax.ShapeDtypeStruct((b, HEADS // 2, GLA_DV, LANES), F32)),
        scratch_shapes=[pltpu.VMEM((HEADS // 2, GLA_DV, LANES), F32)],
        compiler_params=_cparams(("arbitrary", "arbitrary")),
        name="gla",
    )(mall, gq, gk, gl, gv, gr, s0, g_out)


def _state_to_pairs(s):
    b = s.shape[0]
    s = s.reshape(b, HEADS // 2, 2, GLA_DK, GLA_DV)
    return jnp.transpose(s, (0, 1, 4, 2, 3)).reshape(b, HEADS // 2, GLA_DV, 2 * GLA_DK)


def _state_from_pairs(s):
    b = s.shape[0]
    s = s.reshape(b, HEADS // 2, GLA_DV, 2, GLA_DK)
    return jnp.transpose(s, (0, 1, 3, 4, 2)).reshape(b, HEADS, GLA_DK, GLA_DV)


def _post_body(x_ref, om_ref, og_ref, gt_ref, sc_ref, sh_ref, wo_ref, gffn_ref, wr_ref, br_ref,
               x2_ref, h_ref, idx_ref, wt_ref, rank_ref, cnt_ref):
    half = om_ref.shape[-1]
    mix = (jnp.dot(om_ref[...], wo_ref[0:half, :], preferred_element_type=F32)
           + jnp.dot(og_ref[...], wo_ref[half:2 * half, :], preferred_element_type=F32))
    x2 = x_ref[...] + gt_ref[...] * mix
    x2_ref[...] = x2
    d = x2.shape[-1]
    h = (x2 * _rms(x2, d) * gffn_ref[...]) * (1.0 + sc_ref[...]) + sh_ref[...]
    h_hi = h.astype(BF16)
    h_ref[...] = h_hi
    h_lo = (h - h_hi.astype(F32)).astype(BF16)
    logits = _nt(wr_ref[0], h_hi) + _nt(wr_ref[0], h_lo) + _nt(wr_ref[1], h_hi) + br_ref[...]
    n_exp, tm = logits.shape
    eid = lax.broadcasted_iota(I32, (n_exp, tm), 0)
    vals, tops, ids = logits, [], []
    for _ in range(TOP_K):
        m = jnp.max(vals, axis=0, keepdims=True)
        sel = jnp.min(jnp.where(vals == m, eid, n_exp), axis=0, keepdims=True)
        tops.append(m)
        ids.append(sel)
        vals = jnp.where(eid == sel, -jnp.inf, vals)
    es = [jnp.exp(t - tops[0]) for t in tops]
    tot = es[0] + es[1] + es[2] + es[3]
    idx_ref[...] = jnp.concatenate(ids, axis=0)
    wt_ref[...] = jnp.concatenate([e / tot for e in es], axis=0)
    hits = [eid == sel for sel in ids]
    member = jnp.zeros((n_exp, tm), F32)
    for hk in hits:
        member = member + jnp.where(hk, 1.0, 0.0)
    before = lax.broadcasted_iota(I32, (tm, tm), 0) < lax.broadcasted_iota(I32, (tm, tm), 1)
    prefix = jnp.dot(member.astype(BF16), jnp.where(before, 1.0, 0.0).astype(BF16), preferred_element_type=F32)
    rank_ref[...] = jnp.concatenate(
        [jnp.sum(jnp.where(hk, prefix, 0.0), axis=0, keepdims=True) for hk in hits], axis=0).astype(I32)
    cnt_ref[...] = jnp.broadcast_to(jnp.sum(member, axis=1, keepdims=True), (n_exp, LANES)).astype(I32)


def _post(x, om, og, gate, scale, shift, w_o, g_ffn, w_r2, b_r):
    t, d = x.shape
    tm = ROUTE_TILE
    per_tok = gate.shape[0] == t
    mod = pl.BlockSpec((tm, d), lambda i: (i, 0)) if per_tok else pl.BlockSpec((1, d), lambda i: (0, 0))
    tok = lambda w: pl.BlockSpec((tm, w), lambda i: (i, 0))
    full = lambda a: pl.BlockSpec(a.shape, lambda i: (0,) * a.ndim)
    return pl.pallas_call(
        _post_body,
        grid=(t // tm,),
        in_specs=[tok(d), tok(om.shape[1]), tok(og.shape[1]), mod, mod, mod, full(w_o), full(g_ffn), full(w_r2),
                  full(b_r)],
        out_specs=(tok(d), tok(d),
                   pl.BlockSpec((TOP_K, tm), lambda i: (0, i)), pl.BlockSpec((TOP_K, tm), lambda i: (0, i)),
                   pl.BlockSpec((TOP_K, tm), lambda i: (0, i)),
                   pl.BlockSpec((None, N_EXPERTS, LANES), lambda i: (i, 0, 0))),
        out_shape=(jax.ShapeDtypeStruct((t, d), F32), jax.ShapeDtypeStruct((t, d), BF16),
                   jax.ShapeDtypeStruct((TOP_K, t), I32), jax.ShapeDtypeStruct((TOP_K, t), F32),
                   jax.ShapeDtypeStruct((TOP_K, t), I32), jax.ShapeDtypeStruct((t // tm, N_EXPERTS, LANES), I32)),
        compiler_params=_cparams(("arbitrary",), VMEM_SMALL),
        name="post",
    )(x, om, og, gate, scale, shift, w_o, g_ffn, w_r2, b_r)


def _max_tile_rows():
    return TOP_K * ROUTE_TILE + N_EXPERTS * (RUN_CHUNK - 1)


def _max_big_pieces():
    return _max_tile_rows() // BIG_PIECE


def _sort_capacity():
    return _round_up(_max_tile_rows(), SORT_ROWS)


class _Tab:
    def __init__(self):
        nb = _max_big_pieces()
        self.src_big, self.dst_big = 0, nb
        self.src_small, self.dst_small = 2 * nb, 2 * nb + N_EXPERTS
        self.n_big = 2 * nb + 2 * N_EXPERTS
        self.n_small, self.n_rows = self.n_big + 1, self.n_big + 2
        self.width = self.n_big + 3


def _route_tables(idx, lrank, cnt3):
    nt = cnt3.shape[0]
    t = idx.shape[1]
    cnt = cnt3[:, :, 0]
    run = _round_up(cnt, RUN_CHUNK)
    lo_end = jnp.cumsum(run, axis=1)
    lo = lo_end - run
    region = _round_up(jnp.sum(run, axis=0), EXPERT_ROWS)
    g_end = jnp.cumsum(region)
    run_dest = (g_end - region)[None, :] + jnp.cumsum(run, axis=0) - run
    n_ch = run // RUN_CHUNK
    big, small = n_ch // 2, n_ch % 2

    def piece_list(count, n_out):
        end = jnp.cumsum(count, axis=1)
        p = jnp.arange(n_out, dtype=I32)
        e = jnp.minimum(jnp.sum(end[:, None, :] <= p[None, :, None], axis=2), N_EXPERTS - 1)
        pick = e[:, :, None] == jnp.arange(N_EXPERTS, dtype=I32)[None, None, :]
        of_run = lambda a: jnp.sum(jnp.where(pick, a[:, None, :], 0), axis=2)
        return of_run, p[None, :] - of_run(end - count), end[:, -1]

    of_big, within_big, n_big = piece_list(big, _max_big_pieces())
    of_small, _, n_small = piece_list(small, N_EXPERTS)
    table = jnp.concatenate([
        of_big(lo) + BIG_PIECE * within_big, of_big(run_dest) + BIG_PIECE * within_big,
        of_small(lo + BIG_PIECE * big), of_small(run_dest + BIG_PIECE * big),
        n_big[:, None], n_small[:, None], lo_end[:, -1:]], axis=1).astype(I32)
    table = table.reshape(nt, 1, _Tab().width)
    eid =jnp.arange(N_EXPERTS, dtype=I32)[:, None]
    lo_tok = jnp.repeat(lo.T, ROUTE_TILE, axis=1)
    lpos = jnp.stack([jnp.sum(jnp.where(idx[k][None, :] == eid, lo_tok, 0), axis=0) for k in range(TOP_K)])
    lpos = (lpos + lrank).astype(I32)
    n_blocks = _round_up(t * TOP_K + nt * N_EXPERTS * (RUN_CHUNK - 1), EXPERT_ROWS) // EXPERT_ROWS + N_EXPERTS
    b_start = jnp.arange(n_blocks, dtype=I32) * EXPERT_ROWS
    blk_e = jnp.minimum(jnp.sum(g_end[None, :] <= b_start[:, None], axis=1), N_EXPERTS - 1).astype(I32)
    n_valid = (g_end[-1:] // EXPERT_ROWS).astype(I32)
    used = region > 0
    e_ids = jnp.arange(N_EXPERTS, dtype=I32)
    later_used = used[None, :] & (e_ids[None, :] > e_ids[:, None])
    nxt_e = jnp.min(jnp.where(later_used, e_ids[None, :], N_EXPERTS), axis=1)
    nxt_e = jnp.where(nxt_e < N_EXPERTS, nxt_e, -1).astype(I32)
    ord_e = (jnp.cumsum(used.astype(I32)) - 1).astype(I32)
    tail = jnp.concatenate([jnp.where(used, g_end - EXPERT_ROWS, -1), n_valid]).astype(I32)
    tail = tail.reshape(1, N_EXPERTS + 1)
    return dict(table=table, lpos=lpos, blk_e=blk_e, n_valid=n_valid, nxt_e=nxt_e, ord_e=ord_e, tail=tail,
                n_blocks=n_blocks)


def _chunk_copy(src, dst, sem):
    return pltpu.make_async_copy(src, dst, sem)


def _for_row_blocks(n_rows, body):
    full = n_rows // SORT_ROWS
    rem = n_rows - full * SORT_ROWS
    tail = pl.multiple_of(full * SORT_ROWS, SORT_ROWS)

    def whole(rb, carry):
        body(pl.multiple_of(rb * SORT_ROWS, SORT_ROWS), SORT_ROWS)
        return carry

    lax.fori_loop(0, full, whole, 0)

    @pl.when(rem > SORT_ROWS // 2)
    def _():
        body(tail, SORT_ROWS)

    @pl.when((rem > 0) & (rem <= SORT_ROWS // 2))
    def _():
        body(tail, SORT_ROWS // 2)


def _for_chunks(n, body):
    groups = n // CHUNK_UNROLL

    def group(g, carry):
        for u in range(CHUNK_UNROLL):
            body(g * CHUNK_UNROLL + u)
        return carry

    def single(c, carry):
        body(c)
        return carry

    lax.fori_loop(0, groups, group, 0)
    lax.fori_loop(groups * CHUNK_UNROLL, n, single, 0)


def _scatter_body(nt_a, n_blocks, tab_ref, prv_ref, tail_ref, lpos_ref, ha_ref, hb_ref, xout_ref, sorted_scr,
                  zero_scr, sems, zero_sem):
    i = pl.program_id(0)
    slot = i % 2
    tb = _Tab()
    tt = ha_ref.shape[0]

    @pl.when(i == 0)
    def _():
        zero_scr[...] = jnp.zeros_like(zero_scr)
        n_valid = tail_ref[0, N_EXPERTS]

        def block(start):
            return xout_ref.at[pl.ds(pl.multiple_of(start, EXPERT_ROWS), EXPERT_ROWS)]

        for e in range(N_EXPERTS):
            @pl.when(tail_ref[0, e] >= 0)
            def _():
                _chunk_copy(zero_scr, block(tail_ref[0, e]), zero_sem).start()

        def fill(b, carry):
            _chunk_copy(zero_scr, block(b * EXPERT_ROWS), zero_sem).start()
            return carry

        def fill_done(b, carry):
            _chunk_copy(zero_scr, block(0), zero_sem).wait()
            return carry

        lax.fori_loop(n_valid, n_blocks, fill, 0)
        for e in range(N_EXPERTS):
            @pl.when(tail_ref[0, e] >= 0)
            def _():
                _chunk_copy(zero_scr, block(0), zero_sem).wait()
        lax.fori_loop(n_valid, n_blocks, fill_done, 0)

    lp16 = lpos_ref[...].astype(jnp.int16)
    h = jnp.where(i < nt_a, ha_ref[...], hb_ref[...])

    def sort_block(r0, size):
        rid = (r0 + lax.broadcasted_iota(I32, (size, tt), 0)).astype(jnp.int16)
        onehot = jnp.zeros((size, tt), BF16)
        for k in range(TOP_K):
            onehot = onehot + jnp.where(lp16[k:k + 1, :] == rid, jnp.ones((), BF16), jnp.zeros((), BF16))
        sorted_scr[slot, pl.ds(r0, size), :] = jnp.dot(onehot, h, preferred_element_type=F32).astype(BF16)

    _for_row_blocks(tab_ref[0, tb.n_rows], sort_block)

    def piece(sl, src_row, dst_row, rows):
        src = sorted_scr.at[sl, pl.ds(pl.multiple_of(src_row, RUN_CHUNK), rows)]
        dst = xout_ref.at[pl.ds(pl.multiple_of(dst_row, RUN_CHUNK), rows)]
        return _chunk_copy(src, dst, sems.at[sl])

    def retire(tab, sl):
        _for_chunks(tab[0, tb.n_big], lambda p: piece(sl, 0, 0, BIG_PIECE).wait())
        _for_chunks(tab[0, tb.n_small], lambda p: piece(sl, 0, 0, RUN_CHUNK).wait())

    _for_chunks(tab_ref[0, tb.n_big], lambda p: piece(
        slot, tab_ref[0, tb.src_big + p], tab_ref[0, tb.dst_big + p], BIG_PIECE).start())
    _for_chunks(tab_ref[0, tb.n_small], lambda p: piece(
        slot, tab_ref[0, tb.src_small + p], tab_ref[0, tb.dst_small + p], RUN_CHUNK).start())

    @pl.when(i > 0)
    def _():
        retire(prv_ref, 1 - slot)

    @pl.when(i == pl.num_programs(0) - 1)
    def _():
        retire(tab_ref, slot)


def _scatter(table, tail, lpos, h_a, h_b, n_blocks):
    d = h_a.shape[1]
    nt_a, nt_b = h_a.shape[0] // ROUTE_TILE, h_b.shape[0] // ROUTE_TILE
    sort_cap = _sort_capacity()
    tab_spec = lambda f: pl.BlockSpec((None, 1, _Tab().width), lambda i: (f(i), 0, 0), memory_space=pltpu.SMEM)
    return pl.pallas_call(
        functools.partial(_scatter_body, nt_a, n_blocks),
        grid=(nt_a + nt_b,),
        in_specs=[tab_spec(lambda i: i), tab_spec(lambda i: jnp.maximum(i - 1, 0)),
                  pl.BlockSpec((1, N_EXPERTS + 1), lambda i: (0, 0), memory_space=pltpu.SMEM),
                  pl.BlockSpec((TOP_K, ROUTE_TILE), lambda i: (0, i)),
                  pl.BlockSpec((ROUTE_TILE, d), lambda i: (jnp.minimum(i, nt_a - 1), 0)),
                  pl.BlockSpec((ROUTE_TILE, d), lambda i: (jnp.maximum(i - nt_a, 0), 0))],
        out_specs=pl.BlockSpec(memory_space=pl.ANY),
        out_shape=jax.ShapeDtypeStruct((n_blocks * EXPERT_ROWS, d), BF16),
        scratch_shapes=[pltpu.VMEM((2, sort_cap, d), BF16), pltpu.VMEM((EXPERT_ROWS, d), BF16),
                        pltpu.SemaphoreType.DMA((2,)), pltpu.SemaphoreType.DMA(())],
        compiler_params=_cparams(("arbitrary",), VMEM_MID),
        name="scatter",
    )(table, table, tail, lpos, h_a, h_b)


def _experts_body(be_ref, nv_ref, nxt_ref, ord_ref, x_ref, wgu_hbm, bgu_ref, wd_hbm, bd_ref, y_ref,
                  wgu_f, wd_f, wgu_s, wd_s, sem_gu, sem_d):
    b = pl.program_id(0)
    e = be_ref[b]
    prev = be_ref[jnp.maximum(b - 1, 0)]
    valid = b < nv_ref[0]
    d_ff = wd_s.shape[0]
    slot = ord_ref[e] % 2

    def weights(expert, sl):
        return (pltpu.make_async_copy(wgu_hbm.at[expert], wgu_f.at[sl], sem_gu.at[sl]),
                pltpu.make_async_copy(wd_hbm.at[expert], wd_f.at[sl], sem_d.at[sl]))

    @pl.when(valid & ((b == 0) | (e != prev)))
    def _():
        @pl.when(b == 0)
        def _():
            for cp in weights(e, slot):
                cp.start()

        for cp in weights(e, slot):
            cp.wait()

        @pl.when(nxt_ref[e] >= 0)
        def _():
            for cp in weights(nxt_ref[e], 1 - slot):
                cp.start()

        wgu_s[...] = wgu_f[slot].astype(BF16)
        wd_s[...] = wd_f[slot].astype(BF16)

    @pl.when(valid)
    def _():
        gu = jnp.dot(x_ref[...], wgu_s[...], preferred_element_type=F32) + bgu_ref[...]
        gate = jnp.minimum(gu[:, 0:d_ff], SWIGLU_LIMIT)
        up = jnp.clip(gu[:, d_ff:2 * d_ff], -SWIGLU_LIMIT, SWIGLU_LIMIT)
        act = ((up + 1.0) * (gate * jax.nn.sigmoid(gate * SWIGLU_ALPHA))).astype(BF16)
        y_ref[...] = (jnp.dot(act, wd_s[...], preferred_element_type=F32) + bd_ref[...]).astype(BF16)

    @pl.when(jnp.logical_not(valid))
    def _():
        y_ref[...] = jnp.zeros_like(y_ref)


def _experts(blk_e, n_valid, nxt_e, ord_e, xpad, w_gu, b_gu, w_down, b_down):
    m, d = xpad.shape
    nb = m // EXPERT_ROWS
    n_exp, _, f2 = w_gu.shape
    d_ff = w_down.shape[1]
    last = lambda b, be, nv: jnp.minimum(b, nv[0] - 1)
    grid_spec = pltpu.PrefetchScalarGridSpec(
        num_scalar_prefetch=4,
        grid=(nb,),
        in_specs=[pl.BlockSpec((EXPERT_ROWS, d), lambda b, be, nv, nx, od: (last(b, be, nv), 0)),
                  pl.BlockSpec(memory_space=pl.ANY),
                  pl.BlockSpec((None, 1, f2), lambda b, be, nv, nx, od: (be[last(b, be, nv)], 0, 0)),
                  pl.BlockSpec(memory_space=pl.ANY),
                  pl.BlockSpec((None, 1, d), lambda b, be, nv, nx, od: (be[last(b, be, nv)], 0, 0))],
        out_specs=pl.BlockSpec((EXPERT_ROWS, d), lambda b, be, nv, nx, od: (b, 0)),
        scratch_shapes=[pltpu.VMEM((2, d, f2), F32), pltpu.VMEM((2, d_ff, d), F32),
                        pltpu.VMEM((d, f2), BF16), pltpu.VMEM((d_ff, d), BF16),
                        pltpu.SemaphoreType.DMA((2,)), pltpu.SemaphoreType.DMA((2,))],
    )
    return pl.pallas_call(
        _experts_body,
        grid_spec=grid_spec,
        out_shape=jax.ShapeDtypeStruct((m, d), BF16),
        compiler_params=_cparams(("arbitrary",), VMEM_BIG),
        name="experts",
    )(blk_e, n_valid, nxt_e, ord_e, xpad, w_gu, b_gu.reshape(n_exp, 1, f2), w_down, b_down.reshape(n_exp, 1, d))


def _combine_body(tab_ref, nxt_ref, lpos_ref, wt_ref, x2_ref, gt_ref, y_ref, o_ref, ysort_scr, sems):
    i = pl.program_id(0)
    slot = i % 2
    tb = _Tab()
    tt = x2_ref.shape[0]

    def piece(sl, sorted_row, expert_row, rows):
        src = y_ref.at[pl.ds(pl.multiple_of(expert_row, RUN_CHUNK), rows)]
        dst = ysort_scr.at[sl, pl.ds(pl.multiple_of(sorted_row, RUN_CHUNK), rows)]
        return _chunk_copy(src, dst, sems.at[sl])

    def fetch(tab, sl):
        _for_chunks(tab[0, tb.n_big], lambda p: piece(
            sl, tab[0, tb.src_big + p], tab[0, tb.dst_big + p], BIG_PIECE).start())
        _for_chunks(tab[0, tb.n_small], lambda p: piece(
            sl, tab[0, tb.src_small + p], tab[0, tb.dst_small + p], RUN_CHUNK).start())

    @pl.when(i == 0)
    def _():
        ysort_scr[...] = jnp.zeros_like(ysort_scr)
        fetch(tab_ref, 0)

    @pl.when(i + 1 < pl.num_programs(0))
    def _():
        fetch(nxt_ref, 1 - slot)

    _for_chunks(tab_ref[0, tb.n_big], lambda p: piece(slot, 0, 0, BIG_PIECE).wait())
    _for_chunks(tab_ref[0, tb.n_small], lambda p: piece(slot, 0, 0, RUN_CHUNK).wait())

    lp16 = lpos_ref[...].astype(jnp.int16)
    w = wt_ref[...].astype(BF16)
    o_ref[...] = x2_ref[...]
    gt = gt_ref[...]

    def gather_block(r0, size):
        rid = (r0 + lax.broadcasted_iota(I32, (size, tt), 0)).astype(jnp.int16)
        pw_t = jnp.zeros((size, tt), BF16)
        for k in range(TOP_K):
            pw_t = pw_t + jnp.where(lp16[k:k + 1, :] == rid, w[k:k + 1, :], jnp.zeros((), BF16))
        ys = ysort_scr[slot, pl.ds(r0, size), :]
        part = lax.dot_general(pw_t, ys, (((0,), (0,)), ((), ())), preferred_element_type=F32)
        o_ref[...] = o_ref[...] + gt * part

    _for_row_blocks(tab_ref[0, tb.n_rows], gather_block)


def _combine(table, lpos, wts, x2, gate, ypad):
    t, d = x2.shape
    nt = t // ROUTE_TILE
    per_tok = gate.shape[0] == t
    mod = pl.BlockSpec((ROUTE_TILE, d), lambda i: (i, 0)) if per_tok else pl.BlockSpec((1, d), lambda i: (0, 0))
    sort_cap = _sort_capacity()
    tab_spec = lambda f: pl.BlockSpec((None, 1, _Tab().width), lambda i: (f(i), 0, 0), memory_space=pltpu.SMEM)
    return pl.pallas_call(
        _combine_body,
        grid=(nt,),
        in_specs=[tab_spec(lambda i: i), tab_spec(lambda i: jnp.minimum(i + 1, nt - 1)),
                  pl.BlockSpec((TOP_K, ROUTE_TILE), lambda i: (0, i)),
                  pl.BlockSpec((TOP_K, ROUTE_TILE), lambda i: (0, i)),
                  pl.BlockSpec((ROUTE_TILE, d), lambda i: (i, 0)),
                  mod,
                  pl.BlockSpec(memory_space=pl.ANY)],
        out_specs=pl.BlockSpec((ROUTE_TILE, d), lambda i: (i, 0)),
        out_shape=jax.ShapeDtypeStruct((t, d), F32),
        scratch_shapes=[pltpu.VMEM((2, sort_cap, d), BF16), pltpu.SemaphoreType.DMA((2,))],
        compiler_params=_cparams(("arbitrary",), VMEM_MID),
        name="combine",
    )(table, table, lpos, wts, x2, gate, ypad)


def _prep_weights(g_norm_mix, w_in, g_q_a, w_uq, g_kv_a, w_ukv, g_qk_q, g_qk_k, w_g2, b_g2, g_gla_out, w_o,
                  g_norm_ffn, w_router, b_router):
    d = w_in.shape[0]
    o_qa, o_kva, o_kr = 0, Q_LORA, Q_LORA + KV_LORA
    o_gq = o_kr + ROPE
    o_gk = o_gq + HEADS * GLA_DK
    o_gv = o_gk + HEADS * GLA_DK
    o_glr = o_gv + HEADS * GLA_DV
    o_gr = o_glr + GATE_RANK
    kr_cols = w_in[:, o_kr:o_kr + ROPE]
    w1 = jnp.concatenate([
        w_in[:, o_qa:o_kva], kr_cols, kr_cols, w_in[:, o_kva:o_kr], w_in[:, o_gq:o_gk], w_in[:, o_gk:o_gv],
        w_in[:, o_gv:o_glr], w_in[:, o_gr:o_gr + HEADS * GLA_DV], w_in[:, o_glr:o_gr],
        jnp.zeros((d, LANES - GATE_RANK), w_in.dtype)], axis=1).astype(BF16)
    assert w1.shape[1] == _W1_COLS
    wq = w_uq.reshape(Q_LORA, HEADS, QK)
    wuq = jnp.concatenate([wq[:, :, 0:NOPE].reshape(Q_LORA, HEADS * NOPE),
                           wq[:, :, NOPE:QK].reshape(Q_LORA, HEADS * ROPE)], axis=1).astype(BF16)
    wkv = w_ukv.reshape(KV_LORA, HEADS, NOPE + V_DIM)
    wuk = wkv[:, :, 0:NOPE].reshape(KV_LORA, HEADS * NOPE).astype(BF16)
    wuv_t = wkv[:, :, NOPE:].reshape(KV_LORA, HEADS * V_DIM).T.astype(BF16)
    pad_rope = lambda g: jnp.stack([g[0:NOPE], jnp.concatenate([g[NOPE:QK], jnp.zeros((QK_PAD - QK,), g.dtype)])])
    inv = ROPE_THETA ** (-jnp.arange(HALF, dtype=F32) / HALF)
    sign = jnp.concatenate([-jnp.ones((HALF,), F32), jnp.ones((HALF,), F32)])
    rope_tab = jnp.stack([jnp.tile(inv, LANES // HALF), jnp.tile(sign, LANES // ROPE)])
    wg2 = jnp.concatenate([w_g2, jnp.zeros((LANES - GATE_RANK, w_g2.shape[1]), w_g2.dtype)], axis=0).astype(BF16)
    wr_t = w_router.T
    wr_hi = wr_t.astype(BF16)
    wr_lo = (wr_t - wr_hi.astype(F32)).astype(BF16)
    return dict(
        g_mix=g_norm_mix.reshape(1, d), w1=w1, g_qa=g_q_a.reshape(1, -1), w_uq=wuq, g_kv=g_kv_a.reshape(1, -1),
        w_uk=wuk, w_uv_t=wuv_t, g_qk_q=pad_rope(g_qk_q), g_qk_k=pad_rope(g_qk_k), rope=rope_tab, w_g2=wg2,
        b_g2=b_g2.reshape(1, -1), g_out=g_gla_out.reshape(1, -1), w_o=w_o.astype(BF16),
        g_ffn=g_norm_ffn.reshape(1, d), w_r2=jnp.stack([wr_hi, wr_lo]), b_r=b_router.reshape(-1, 1))


def _mixer(x, mod, pos0, past_lat, past_kr, s0_pairs, wts):
    b, s, d = x.shape
    t = b * s
    if b > 1 and s & (s - 1) == 0 and min(t, TOKEN_TILE) % s == 0:
        per_tok = lambda j: jnp.broadcast_to(mod[:, j:j + 1], (b, s, d)).reshape(1, t, d)
        outs = _proj(x.reshape(1, t, d), per_tok(0), per_tok(1), pos0, wts, period=s)
        q = outs[0].reshape(HEADS, b, s, QK_PAD).transpose(1, 0, 2, 3)
        lat, kr, gq, gk, gv, gl, gr = [o.reshape(b, s, o.shape[-1]) for o in outs[1:]]
    else:
        q, lat, kr, gq, gk, gv, gl, gr = _proj(x, mod[:, 0:1], mod[:, 1:2], pos0, wts)
    kv_w = (wts["w_uk"], wts["w_uv_t"], wts["g_qk_k"])
    if past_lat is None:
        k_new, vt_new = _kv(lat, kr, *kv_w)
        o_mla = _attn_prompt(q, k_new, vt_new)
    else:
        o_mla = _attn_sample(q, past_lat, past_kr, lat, kr, *kv_w)
    o_gla, s_fin = _gla(gq, gk, gl, gv, gr, s0_pairs, wts["g_out"])
    if b == 1:
        rows = lambda j: mod[0, j:j + 1]
    else:
        rows = lambda j: jnp.broadcast_to(mod[:, j:j + 1], (b, s, d)).reshape(t, d)
    x2, h2, idx, wt, lrank, cnt = _post(x.reshape(t, d), o_mla.reshape(t, -1), o_gla.reshape(t, -1), rows(2), rows(4),
                                        rows(3), wts["w_o"], wts["g_ffn"], wts["w_r2"], wts["b_r"])
    return dict(x2=x2, h2=h2, idx=idx, wt=wt, lrank=lrank, cnt=cnt, gate_f=rows(5), lat=lat, kr=kr, s_fin=s_fin)


def kernel(x_prompt, x_sample, cache_mla_latent, cache_mla_krope, state_gla, c_prompt, c_sample, w_ada, b_ada, g_norm_mix, w_in, g_q_a, w_uq, g_kv_a, w_ukv, g_qk_q, g_qk_k, w_g2, b_g2, g_gla_out, w_o, g_norm_ffn, w_router, b_router, w_gu, b_gu, w_down, b_down):
    depth = w_ada.shape[0]
    assert depth == 1, "single-layer step"
    bp, sp, d = x_prompt.shape
    bs, ss, _ = x_sample.shape
    tp, tsm = bp * sp, bs * ss
    assert tp % ROUTE_TILE == 0 and tsm % ROUTE_TILE == 0, "token counts must be whole routing tiles"
    past = cache_mla_latent.shape[2]
    layer = lambda a: a.reshape(a.shape[1:])
    wts = _prep_weights(*[layer(a) for a in (g_norm_mix, w_in, g_q_a, w_uq, g_kv_a, w_ukv, g_qk_q, g_qk_k, w_g2, b_g2,
                                             g_gla_out, w_o, g_norm_ffn, w_router, b_router)])
    w_gu, b_gu, w_down, b_down = layer(w_gu), layer(b_gu), layer(w_down), layer(b_down)

    mod = _ada(jnp.concatenate([c_prompt, c_sample], axis=0), layer(w_ada), layer(b_ada)).reshape(bp + bs, 6, d)
    zero_state = jnp.zeros((bp, HEADS // 2, GLA_DV, LANES), F32)
    pr = _mixer(x_prompt, mod[:bp], 0, None, None, zero_state, wts)
    sa = _mixer(x_sample, mod[bp:], past, layer(cache_mla_latent), layer(cache_mla_krope),
                _state_to_pairs(layer(state_gla)), wts)

    idx = jnp.concatenate([pr["idx"], sa["idx"]], axis=1)
    lrank = jnp.concatenate([pr["lrank"], sa["lrank"]], axis=1)
    rt = _route_tables(idx, lrank, jnp.concatenate([pr["cnt"], sa["cnt"]], axis=0))
    ntp = tp // ROUTE_TILE
    lpos, table = rt["lpos"], rt["table"]
    xpad = _scatter(table, rt["tail"], lpos, pr["h2"], sa["h2"], rt["n_blocks"])
    ypad = _experts(rt["blk_e"], rt["n_valid"], rt["nxt_e"], rt["ord_e"], xpad, w_gu, b_gu, w_down, b_down)
    y_p = _combine(table[:ntp], lpos[:, :tp], pr["wt"], pr["x2"], pr["gate_f"], ypad).reshape(bp, sp, d)
    y_s = _combine(table[ntp:], lpos[:, tp:], sa["wt"], sa["x2"], sa["gate_f"], ypad).reshape(bs, ss, d)

    return (y_p, y_s,
            pr["lat"][None], pr["kr"][None], _state_from_pairs(pr["s_fin"])[None],
            sa["lat"][None], sa["kr"][None], _state_from_pairs(sa["s_fin"])[None])
```

```python
import functools

import numpy as np
import jax
import jax.numpy as jnp
from jax import lax
from jax.experimental import pallas as pl
from jax.experimental.pallas import tpu as pltpu

F32 = jnp.float32
BF16 = jnp.bfloat16
I32 = jnp.int32

CHUNK = 64
EPS = 1e-6
HEADS = 4
Q_LORA = 384
KV_LORA = 256
NOPE = 128
ROPE = 64
HALF = ROPE // 2
V_DIM = 128
QK = NOPE + ROPE
QK_PAD = 256
ROPE_THETA = 10000.0
GLA_DK = 64
GLA_DV = 128
GATE_RANK = 16
GATE_NORM = 16.0
N_EXPERTS = 32
TOP_K = 4
SWIGLU_LIMIT = 7.0
SWIGLU_ALPHA = 1.702
NEG = -1e30
LOG2_E = 1.4426950408889634

LANES = 128
SUBLANES = 8
BF16_ROWS = 16
TOKEN_TILE = 512
ADA_COLS = 1536
ROUTE_TILE = 512
RUN_CHUNK = BF16_ROWS
BIG_PIECE = 2 * RUN_CHUNK
SORT_ROWS = 512
CHUNK_UNROLL = 4
EXPERT_ROWS = 256
ATTN_TILE = 1024
MIB = 1024 * 1024
VMEM_BIG = 56 * MIB
VMEM_MID = 48 * MIB
VMEM_SMALL = 40 * MIB


def _cparams(sem, vmem=None):
    return pltpu.CompilerParams(dimension_semantics=sem, vmem_limit_bytes=vmem)


def _nt(a, b):
    return lax.dot_general(a, b, (((1,), (1,)), ((), ())), preferred_element_type=F32)


def _rms(x, width):
    return lax.rsqrt(jnp.sum(x * x, axis=-1, keepdims=True) * (1.0 / width) + EPS)


def _round_up(x, m):
    return ((x + m - 1) // m) * m


def _ada_body(c_ref, w_ref, b_ref, o_ref):
    c = c_ref[...]
    s = (c * jax.nn.sigmoid(c)).astype(BF16)
    o_ref[...] = jnp.dot(s, w_ref[...].astype(BF16), preferred_element_type=F32) + b_ref[...]


def _ada(c, w_ada, b_ada):
    r, d = c.shape
    n = w_ada.shape[1]
    tn = ADA_COLS if n % ADA_COLS == 0 else n
    return pl.pallas_call(
        _ada_body,
        grid=(n // tn,),
        in_specs=[pl.BlockSpec((r, d), lambda j: (0, 0)),
                  pl.BlockSpec((d, tn), lambda j: (0, j)),
                  pl.BlockSpec((1, tn), lambda j: (0, j))],
        out_specs=pl.BlockSpec((r, tn), lambda j: (0, j)),
        out_shape=jax.ShapeDtypeStruct((r, n), F32),
        compiler_params=_cparams(("arbitrary",), VMEM_SMALL),
        name="ada",
    )(c, w_ada, b_ada.reshape(1, n))


_SEG = dict(qa_kr=(0, 512), kva=(512, 768), gq=(768, 1024), gk=(1024, 1280),
            gv=(1280, 1792), gr=(1792, 2304), glr=(2304, 2432))
_W1_COLS = 2432
W1_PREP_ROWS = 256


def _w1_source_columns():
    o_kva, o_kr = Q_LORA, Q_LORA + KV_LORA
    o_gq = o_kr + ROPE
    o_gk = o_gq + HEADS * GLA_DK
    o_gv = o_gk + HEADS * GLA_DK
    o_glr = o_gv + HEADS * GLA_DV
    o_gr = o_glr + GATE_RANK
    return [(0, Q_LORA), (o_kr, ROPE), (o_kr, ROPE), (o_kva, KV_LORA), (o_gq, HEADS * GLA_DK),
            (o_gk, HEADS * GLA_DK), (o_gv, HEADS * GLA_DV), (o_gr, HEADS * GLA_DV), (o_glr, GATE_RANK)]


def _w1_body(win_ref, w1_ref):
    o = 0
    for start, width in _w1_source_columns():
        w1_ref[:, o:o + width] = win_ref[:, start:start + width].astype(BF16)
        o += width
    w1_ref[:, o:] = jnp.zeros((w1_ref.shape[0], _W1_COLS - o), BF16)


def _w1(w_in):
    d, n = w_in.shape
    return pl.pallas_call(
        _w1_body,
        grid=(d // W1_PREP_ROWS,),
        in_specs=[pl.BlockSpec((W1_PREP_ROWS, n), lambda i: (i, 0))],
        out_specs=pl.BlockSpec((W1_PREP_ROWS, _W1_COLS), lambda i: (i, 0)),
        out_shape=jax.ShapeDtypeStruct((d, _W1_COLS), BF16),
        compiler_params=_cparams(("arbitrary",)),
        name="w1",
    )(w_in)


def _proj_body(pos0, ts, period, x_ref, sh_ref, sc_ref, gmix_ref, w1_ref, gqa_ref, wuq_ref, gkv_ref, gqk_ref,
               rope_ref, wg2_ref, bg2_ref,
               q_ref, lat_ref, kr_ref, gq_o, gk_o, gv_o, gl_o, gr_o, trig_scr):
    i = pl.program_id(1)
    x = x_ref[...]
    d = x.shape[-1]
    h = (x * _rms(x, d) * gmix_ref[...]) * (1.0 + sc_ref[...]) + sh_ref[...]
    hb = h.astype(BF16)

    def seg(name):
        a, b = _SEG[name]
        return jnp.dot(hb, w1_ref[:, a:b], preferred_element_type=F32)

    @pl.when((pl.program_id(0) == 0) & (i == 0))
    def _():
        row = lax.broadcasted_iota(I32, (ts, LANES), 0)
        if period is not None:
            row = row & (period - 1)
        row_ang = row.astype(F32) * rope_ref[0:1, :]
        trig_scr[0] = jnp.cos(row_ang)
        trig_scr[1] = jnp.sin(row_ang)

    tile_pos = pos0 + (i * ts if period is None else 0 * i)
    base_ang = jnp.broadcast_to(tile_pos.astype(F32) * rope_ref[0:1, :], (SUBLANES, LANES))
    cos_a, sin_a = jnp.cos(base_ang)[0:1, :], jnp.sin(base_ang)[0:1, :]
    cos = cos_a * trig_scr[0] - sin_a * trig_scr[1]
    sin = (sin_a * trig_scr[0] + cos_a * trig_scr[1]) * rope_ref[1:2, :]
    lane = lax.broadcasted_iota(I32, (ts, LANES), 1)
    first_half = (lane & HALF) == 0
    low64 = lane < ROPE

    def rope(v):
        partner = jnp.where(first_half, pltpu.roll(v, LANES - HALF, 1), pltpu.roll(v, HALF, 1))
        return v * cos + partner * sin

    qa_kr = seg("qa_kr")
    qa = qa_kr[:, 0:Q_LORA]
    qn = (qa * _rms(qa, Q_LORA) * gqa_ref[...]).astype(BF16)
    qf = jnp.dot(qn, wuq_ref[...], preferred_element_type=F32)
    rope_blocks = (rope(qf[:, 4 * NOPE:4 * NOPE + LANES]), rope(qf[:, 4 * NOPE + LANES:4 * NOPE + 2 * LANES]))
    for hd in range(HEADS):
        nope = qf[:, NOPE * hd:NOPE * (hd + 1)]
        blk = rope_blocks[hd // 2]
        if hd % 2:
            blk = pltpu.roll(blk, ROPE, 1)
        blk = jnp.where(low64, blk, 0.0)
        ss = jnp.sum(nope * nope, axis=-1, keepdims=True) + jnp.sum(blk * blk, axis=-1, keepdims=True)
        scl = lax.rsqrt(ss * (1.0 / QK) + EPS) * (QK ** -0.5 * LOG2_E)
        q_ref[hd, :, 0:NOPE] = (nope * scl * gqk_ref[0:1, :]).astype(BF16)
        q_ref[hd, :, NOPE:QK_PAD] = (blk * scl * gqk_ref[1:2, :]).astype(BF16)

    kva = seg("kva")
    lat_ref[...] = kva * _rms(kva, KV_LORA) * gkv_ref[...]
    kr_ref[...] = rope(qa_kr[:, Q_LORA:Q_LORA + LANES])[:, 0:ROPE]

    gq_o[...] = seg("gq") * (GLA_DK ** -0.5)
    gk_o[...] = seg("gk")
    gv_o[...] = seg("gv").astype(BF16)
    gr_o[...] = seg("gr")
    z = jnp.dot(seg("glr").astype(BF16), wg2_ref[...], preferred_element_type=F32) + bg2_ref[...]
    gl_o[...] = (jnp.minimum(z, 0.0) - jnp.log1p(jnp.exp(-jnp.abs(z)))) * (1.0 / GATE_NORM)


def _proj(x, shift, scale, pos0, wts, period=None):
    b, s, d = x.shape
    ts = min(s, TOKEN_TILE)
    assert period is None or (period & (period - 1) == 0 and ts % period == 0)
    row = lambda a: pl.BlockSpec(a.shape, lambda bi, i: (0,) * a.ndim)
    tok = lambda w: pl.BlockSpec((None, ts, w), lambda bi, i: (bi, i, 0))
    mod = tok(d) if shift.shape[1] == s and s > 1 else pl.BlockSpec((None, 1, d), lambda bi, i: (bi, 0, 0))
    small = [wts["g_mix"], wts["w1"], wts["g_qa"], wts["w_uq"], wts["g_kv"], wts["g_qk_q"], wts["rope"],
             wts["w_g2"], wts["b_g2"]]
    out_shape = (
        jax.ShapeDtypeStruct((b, HEADS, s, QK_PAD), BF16),
        jax.ShapeDtypeStruct((b, s, KV_LORA), F32),
        jax.ShapeDtypeStruct((b, s, ROPE), F32),
        jax.ShapeDtypeStruct((b, s, HEADS * GLA_DK), F32),
        jax.ShapeDtypeStruct((b, s, HEADS * GLA_DK), F32),
        jax.ShapeDtypeStruct((b, s, HEADS * GLA_DV), BF16),
        jax.ShapeDtypeStruct((b, s, HEADS * GLA_DK), F32),
        jax.ShapeDtypeStruct((b, s, HEADS * GLA_DV), F32),
    )
    out_specs = (
        pl.BlockSpec((None, HEADS, ts, QK_PAD), lambda bi, i: (bi, 0, i, 0)),
        tok(KV_LORA), tok(ROPE), tok(HEADS * GLA_DK), tok(HEADS * GLA_DK), tok(HEADS * GLA_DV),
        tok(HEADS * GLA_DK), tok(HEADS * GLA_DV),
    )
    return pl.pallas_call(
        functools.partial(_proj_body, pos0, ts, period),
        grid=(b, s // ts),
        in_specs=[tok(d), mod, mod] + [row(a) for a in small],
        out_specs=out_specs,
        out_shape=out_shape,
        scratch_shapes=[pltpu.VMEM((2, ts, LANES), F32)],
        compiler_params=_cparams(("arbitrary", "arbitrary"), VMEM_BIG),
        name="proj",
    )(x, shift, scale, *small)


def _key_rows(lat, kr, wk_ref, gk_ref, k_out):
    kn_all = jnp.dot(lat, wk_ref[...], preferred_element_type=F32)
    kr_ss = jnp.sum(kr * kr, axis=-1, keepdims=True)
    for hd in range(HEADS):
        kn = kn_all[:, NOPE * hd:NOPE * (hd + 1)]
        scl = lax.rsqrt((jnp.sum(kn * kn, axis=-1, keepdims=True) + kr_ss) * (1.0 / QK) + EPS)
        k_out[hd, :, 0:NOPE] = (kn * scl * gk_ref[0:1, :]).astype(BF16)
        k_out[hd, :, NOPE:QK] = (kr * scl * gk_ref[1:2, 0:ROPE]).astype(BF16)
        k_out[hd, :, QK:QK_PAD] = jnp.zeros((kr.shape[0], QK_PAD - QK), BF16)


def _kv_body(lat_ref, kr_ref, wk_ref, wv_ref, gk_ref, k_ref, v_ref):
    lat = lat_ref[...].astype(BF16)
    _key_rows(lat, kr_ref[...], wk_ref, gk_ref, k_ref)
    v_t = _nt(wv_ref[...], lat)
    for hd in range(HEADS):
        v_ref[hd] = v_t[V_DIM * hd:V_DIM * (hd + 1), :].astype(BF16)


def _kv(lat, kr, w_uk, w_uv_t, g_qk_k):
    b, s, _ = lat.shape
    ts = min(s, ATTN_TILE)
    return pl.pallas_call(
        _kv_body,
        grid=(b, s // ts),
        in_specs=[pl.BlockSpec((None, ts, KV_LORA), lambda bi, i: (bi, i, 0)),
                  pl.BlockSpec((None, ts, ROPE), lambda bi, i: (bi, i, 0)),
                  pl.BlockSpec(w_uk.shape, lambda bi, i: (0, 0)),
                  pl.BlockSpec(w_uv_t.shape, lambda bi, i: (0, 0)),
                  pl.BlockSpec(g_qk_k.shape, lambda bi, i: (0, 0))],
        out_specs=(pl.BlockSpec((None, HEADS, ts, QK_PAD), lambda bi, i: (bi, 0, i, 0)),
                   pl.BlockSpec((None, HEADS, None, V_DIM, ts), lambda bi, i: (bi, 0, i, 0, 0))),
        out_shape=(jax.ShapeDtypeStruct((b, HEADS, s, QK_PAD), BF16),
                   jax.ShapeDtypeStruct((b, HEADS, s // ts, V_DIM, ts), BF16)),
        compiler_params=_cparams(("arbitrary", "arbitrary")),
        name="kv",
    )(lat, kr, w_uk, w_uv_t, g_qk_k)


def _attn_prompt_body(t, q_ref, qn_ref, k_ref, vt_ref, o_ref, s_a, s_b):
    i = pl.program_id(2)

    def scores(q, j, buf):
        buf[...] = _nt(k_ref[pl.ds(pl.multiple_of(j * t, t), t), :], q)

    def consume(j, buf, carry, masked=False):
        m, l, acc = carry
        s = buf[...]
        if masked:
            visible = (lax.broadcasted_iota(I32, (t, t), 0) // CHUNK) <= (lax.broadcasted_iota(I32, (t, t), 1) // CHUNK)
            s = jnp.where(visible, s, NEG)
        m_new = jnp.maximum(m, jnp.max(s, axis=0, keepdims=True))
        alpha = jnp.exp2(m - m_new)
        p = jnp.exp2(s - m_new)
        l = alpha * l + jnp.sum(p, axis=0, keepdims=True)
        acc = alpha * acc + jnp.dot(vt_ref[j], p.astype(BF16), preferred_element_type=F32)
        return m_new, l, acc

    def run(first, second):
        q = q_ref[...]

        @pl.when(i == 0)
        def _():
            scores(q, 0, first)

        def pair(pp, carry):
            j = 2 * pp
            scores(q, j + 1, second)
            carry = consume(j, first, carry)
            scores(q, j + 2, first)
            return consume(j + 1, second, carry)

        def even_tail(carry):
            scores(qn_ref[...], 0, second)
            return consume(i, first, carry, masked=True)

        def odd_tail(carry):
            scores(q, i, second)
            carry = consume(i - 1, first, carry)
            scores(qn_ref[...], 0, first)
            return consume(i, second, carry, masked=True)

        carry = (jnp.full((1, t), NEG, F32), jnp.zeros((1, t), F32), jnp.zeros((V_DIM, t), F32))
        carry = lax.fori_loop(0, i // 2, pair, carry)
        _, l, acc = lax.cond(i % 2 == 1, odd_tail, even_tail, carry)
        o_ref[...] = (acc / l).T.astype(BF16)

    @pl.when(((i + 1) // 2) % 2 == 0)
    def _():
        run(s_a, s_b)

    @pl.when(((i + 1) // 2) % 2 == 1)
    def _():
        run(s_b, s_a)


def _attn_prompt(q, k, v_t):
    b, _, s, _ = q.shape
    t = v_t.shape[-1]
    nq = s // t
    return pl.pallas_call(
        functools.partial(_attn_prompt_body, t),
        grid=(b, HEADS, nq),
        in_specs=[pl.BlockSpec((None, None, t, QK_PAD), lambda bi, h, i: (bi, h, i, 0)),
                  pl.BlockSpec((None, None, t, QK_PAD), lambda bi, h, i: (bi, h, jnp.minimum(i + 1, nq - 1), 0)),
                  pl.BlockSpec((None, None, s, QK_PAD), lambda bi, h, i: (bi, h, 0, 0)),
                  pl.BlockSpec((None, None, nq, V_DIM, t), lambda bi, h, i: (bi, h, 0, 0, 0))],
        out_specs=pl.BlockSpec((None, t, V_DIM), lambda bi, h, i: (bi, i, h)),
        out_shape=jax.ShapeDtypeStruct((b, s, HEADS * V_DIM), BF16),
        scratch_shapes=[pltpu.VMEM((t, t), F32), pltpu.VMEM((t, t), F32)],
        compiler_params=_cparams(("arbitrary", "arbitrary", "arbitrary"), VMEM_BIG),
        name="attn_prompt",
    )(q, q, k, v_t)


def _attn_sample_body(past, sq, q_ref, plat_ref, pkr_ref, nlat_ref, nkr_ref, wk_ref, wv_ref, gk_ref, o_ref,
                      kp_scr, kn_scr):
    plat = plat_ref[...].astype(BF16)
    nlat = nlat_ref[...].astype(BF16)
    _key_rows(plat, pkr_ref[...], wk_ref, gk_ref, kp_scr)
    _key_rows(nlat, nkr_ref[...], wk_ref, gk_ref, kn_scr)
    vp_t = _nt(wv_ref[...], plat).astype(BF16)
    vn_t = _nt(wv_ref[...], nlat).astype(BF16)
    hq = HEADS * sq
    col = lax.broadcasted_iota(I32, (sq, hq), 1)
    key_chunk = (past + lax.broadcasted_iota(I32, (sq, hq), 0)) // CHUNK
    qry_chunk = (past + col % sq) // CHUNK
    qrow = lax.broadcasted_iota(I32, (hq, 1), 0) // sq
    q_all = jnp.concatenate([q_ref[hd] for hd in range(HEADS)], axis=0)
    s_p = jnp.zeros((past, hq), F32)
    s_n = jnp.zeros((sq, hq), F32)
    for hd in range(HEADS):
        q_h = jnp.where(qrow == hd, q_all, jnp.zeros((), BF16))
        s_p = s_p + _nt(kp_scr[hd], q_h)
        s_n = s_n + _nt(kn_scr[hd], q_h)
    s_n = jnp.where(key_chunk <= qry_chunk, s_n, NEG)
    m = jnp.maximum(jnp.max(s_p, axis=0, keepdims=True), jnp.max(s_n, axis=0, keepdims=True))
    p_p = jnp.exp2(s_p - m)
    p_n = jnp.exp2(s_n - m)
    inv_l = 1.0 / (jnp.sum(p_p, axis=0, keepdims=True) + jnp.sum(p_n, axis=0, keepdims=True))
    p_p = p_p.astype(BF16)
    p_n = p_n.astype(BF16)
    lane_head = lax.broadcasted_iota(I32, (V_DIM, hq), 1) // sq
    o_t = jnp.zeros((V_DIM, hq), F32)
    for hd in range(HEADS):
        rows = slice(V_DIM * hd, V_DIM * (hd + 1))
        o_h = (jnp.dot(vp_t[rows, :], p_p, preferred_element_type=F32)
               + jnp.dot(vn_t[rows, :], p_n, preferred_element_type=F32))
        o_t = o_t + jnp.where(lane_head == hd, o_h, 0.0)
    o_all = (o_t * inv_l).T
    for hd in range(HEADS):
        o_ref[:, V_DIM * hd:V_DIM * (hd + 1)] = o_all[sq * hd:sq * (hd + 1), :].astype(BF16)


def _attn_sample(q, past_lat, past_kr, lat, kr, w_uk, w_uv_t, g_qk_k):
    b, _, sq, _ = q.shape
    past = past_lat.shape[1]
    rows = lambda n, w: pl.BlockSpec((None, n, w), lambda bi: (bi, 0, 0))
    full = lambda a: pl.BlockSpec(a.shape, lambda bi: (0,) * a.ndim)
    return pl.pallas_call(
        functools.partial(_attn_sample_body, past, sq),
        grid=(b,),
        in_specs=[pl.BlockSpec((None, HEADS, sq, QK_PAD), lambda bi: (bi, 0, 0, 0)),
                  rows(past, KV_LORA), rows(past, ROPE), rows(sq, KV_LORA), rows(sq, ROPE),
                  full(w_uk), full(w_uv_t), full(g_qk_k)],
        out_specs=rows(sq, HEADS * V_DIM),
        out_shape=jax.ShapeDtypeStruct((b, sq, HEADS * V_DIM), BF16),
        scratch_shapes=[pltpu.VMEM((HEADS, past, QK_PAD), BF16), pltpu.VMEM((HEADS, sq, QK_PAD), BF16)],
        compiler_params=_cparams(("arbitrary",), VMEM_SMALL),
        name="attn_sample",
    )(q, past_lat, past_kr, lat, kr, w_uk, w_uv_t, g_qk_k)


def _gla_masks(c, rows):
    idx = np.arange(rows)
    same = (idx // c)[:, None] == (idx // c)[None, :]
    le = same & (idx[None, :] <= idx[:, None])
    gt = same & (idx[None, :] > idx[:, None])
    return np.concatenate([le, gt], axis=0).astype(np.float32), int(np.log2(c))


def _level_exponents(b, g, c, level):
    n = c >> level
    rows = b.shape[0]
    row = lax.broadcasted_iota(I32, (rows, 1), 0)
    if n >= 8:
        split = b.reshape(rows // n, n, LANES)[:, n // 2 - 1:n // 2, :]
        split = jnp.broadcast_to(split, (rows // n, n, LANES)).reshape(rows, LANES)
        return jnp.where((row & (n // 2)) != 0, b - split, split - b)
    g_prev = pltpu.roll(g, 1, 0)
    g_next = pltpu.roll(g, rows - 1, 0)
    if n == 4:
        r = row & 3
        return jnp.where(r == 0, g_next, jnp.where(r == 1, 0.0, jnp.where(r == 2, g, g + g_prev)))
    assert n == 2
    return jnp.where((row & 1) != 0, g, 0.0)


def _gla_body(c, n_chunks, unit, levels, mall_ref, q_ref, k_ref, g_ref, v_ref, r_ref, s0_ref, gout_ref,
              o_ref, sfin_ref, st_scr):
    it = pl.program_id(1)

    @pl.when(it == 0)
    def _():
        st_scr[...] = s0_ref[...]

    ru = unit * c
    lane = lax.broadcasted_iota(I32, (ru, LANES), 1)
    head_lanes = (lane < GLA_DK, lane >= GLA_DK)
    st_lane_lo = lax.broadcasted_iota(I32, (GLA_DV, LANES), 1) < GLA_DK
    row = lax.broadcasted_iota(I32, (ru, 1), 0)
    ri = lax.broadcasted_iota(I32, (ru, ru), 0)
    ci = lax.broadcasted_iota(I32, (ru, ru), 1)
    mall = mall_ref[...]

    for un in range(n_chunks // unit):
        rows = slice(un * ru, (un + 1) * ru)
        for p in range(HEADS // 2):
            ls = slice(LANES * p, LANES * (p + 1))
            g = g_ref[rows, ls]
            q = q_ref[rows, ls]
            k = k_ref[rows, ls]
            g_hi = g.astype(BF16)
            g_lo = (g - g_hi.astype(F32)).astype(BF16)
            e2 = jnp.dot(mall, jnp.concatenate([g_hi, g_lo], axis=1), preferred_element_type=F32)
            e = e2[:, 0:LANES] + e2[:, LANES:2 * LANES]
            b = e[0:ru]
            eb = jnp.exp(b)
            qb = q * eb
            kd = (k * jnp.exp(e[ru:2 * ru])).astype(BF16)
            qs, ks = [q], [k.astype(BF16)]
            for l in range(levels):
                bottom = (row & (c >> (l + 1))) != 0
                decay = jnp.exp(_level_exponents(b, g, c, l))
                qs.append(jnp.where(bottom, q * decay, 0.0))
                ks.append(jnp.where(bottom, 0.0, k * decay).astype(BF16))
            states = [st_scr[p]]
            for j in range(unit):
                cr = slice(j * c, (j + 1) * c)
                upd = [lax.dot_general(v_ref[rows, GLA_DV * (2 * p + hh):GLA_DV * (2 * p + hh + 1)][cr, :], kd[cr, :],
                                       (((0,), (0,)), ((), ())), preferred_element_type=F32) for hh in range(2)]
                d_last = eb[j * c + c - 1:j * c + c, :]
                states.append(states[-1] * d_last + jnp.where(st_lane_lo, upd[0], upd[1]))
            st_scr[p] = states[-1]
            for hh in range(2):
                hd = 2 * p + hh
                sel = head_lanes[hh]
                a = jnp.where(ri == ci, _nt(jnp.where(sel, qs[0], 0.0).astype(BF16), ks[0]), 0.0)
                for l in range(levels):
                    pr = _nt(jnp.where(sel, qs[l + 1], 0.0).astype(BF16), ks[l + 1])
                    a = a + jnp.where((ri ^ ci) < (c >> l), pr, 0.0)
                vh = v_ref[rows, GLA_DV * hd:GLA_DV * (hd + 1)]
                qb_h = jnp.where(sel, qb, 0.0).astype(BF16)
                o_state = [_nt(qb_h[j * c:(j + 1) * c, :], states[j].astype(BF16)) for j in range(unit)]
                o = jnp.dot(a.astype(BF16), vh, preferred_element_type=F32) + jnp.concatenate(o_state, axis=0)
                on = o * _rms(o, GLA_DV) * gout_ref[...]
                r = r_ref[rows, GLA_DV * hd:GLA_DV * (hd + 1)]
                o_ref[rows, GLA_DV * hd:GLA_DV * (hd + 1)] = (on * (r * jax.nn.sigmoid(r))).astype(BF16)

    @pl.when(it == pl.num_programs(1) - 1)
    def _():
        sfin_ref[...] = st_scr[...]


def _gla(gq, gk, gl, gv, gr, s0, g_out):
    b, s, _ = gq.shape
    c = min(CHUNK, s)
    tile = min(s, 8 * c)
    unit = next(u for u in (4, 2, 1) if (tile // c) % u == 0)
    masks, levels = _gla_masks(c, unit * c)
    mall = jnp.asarray(masks, BF16)
    tok = lambda w: pl.BlockSpec((None, tile, w), lambda bi, i: (bi, i, 0))
    st_spec = pl.BlockSpec((None, HEADS // 2, GLA_DV, LANES), lambda bi, i: (bi, 0, 0, 0))
    return pl.pallas_call(
        functools.partial(_gla_body, c, tile // c, unit, levels),
        grid=(b, s // tile),
        in_specs=[pl.BlockSpec(mall.shape, lambda bi, i: (0, 0)),
                  tok(HEADS * GLA_DK), tok(HEADS * GLA_DK), tok(HEADS * GLA_DK), tok(HEADS * GLA_DV),
                  tok(HEADS * GLA_DV), st_spec, pl.BlockSpec(g_out.shape, lambda bi, i: (0, 0))],
        out_specs=(tok(HEADS * GLA_DV), st_spec),
        out_shape=(jax.ShapeDtypeStruct((b, s, HEADS * GLA_DV), BF16),
                   jax.ShapeDtypeStruct((b, HEADS // 2, GLA_DV, LANES), F32)),
        scratch_shapes=[pltpu.VMEM((HEADS // 2, GLA_DV, LANES), F32)],
        compiler_params=_cparams(("arbitrary", "arbitrary")),
        name="gla",
    )(mall, gq, gk, gl, gv, gr, s0, g_out)


def _state_to_pairs(s):
    b = s.shape[0]
    s = s.reshape(b, HEADS // 2, 2, GLA_DK, GLA_DV)
    return jnp.transpose(s, (0, 1, 4, 2, 3)).reshape(b, HEADS // 2, GLA_DV, 2 * GLA_DK)


def _state_from_pairs(s):
    b = s.shape[0]
    s = s.reshape(b, HEADS // 2, GLA_DV, 2, GLA_DK)
    return jnp.transpose(s, (0, 1, 3, 4, 2)).reshape(b, HEADS, GLA_DK, GLA_DV)


def _post_body(x_ref, om_ref, og_ref, gt_ref, sc_ref, sh_ref, wo_ref, gffn_ref, wr_ref, br_ref,
               x2_ref, h_ref, idx_ref, wt_ref, rank_ref, cnt_ref):
    half = om_ref.shape[-1]
    mix = (jnp.dot(om_ref[...], wo_ref[0:half, :], preferred_element_type=F32)
           + jnp.dot(og_ref[...], wo_ref[half:2 * half, :], preferred_element_type=F32))
    x2 = x_ref[...] + gt_ref[...] * mix
    x2_ref[...] = x2
    d = x2.shape[-1]
    h = (x2 * _rms(x2, d) * gffn_ref[...]) * (1.0 + sc_ref[...]) + sh_ref[...]
    h_hi = h.astype(BF16)
    h_ref[...] = h_hi
    h_lo = (h - h_hi.astype(F32)).astype(BF16)
    logits = _nt(wr_ref[0], h_hi) + _nt(wr_ref[0], h_lo) + _nt(wr_ref[1], h_hi) + br_ref[...]
    n_exp, tm = logits.shape
    eid = lax.broadcasted_iota(I32, (n_exp, tm), 0)
    vals, tops, ids = logits, [], []
    for _ in range(TOP_K):
        m = jnp.max(vals, axis=0, keepdims=True)
        sel = jnp.min(jnp.where(vals == m, eid, n_exp), axis=0, keepdims=True)
        tops.append(m)
        ids.append(sel)
        vals = jnp.where(eid == sel, -jnp.inf, vals)
    es = [jnp.exp(t - tops[0]) for t in tops]
    tot = es[0] + es[1] + es[2] + es[3]
    idx_ref[...] = jnp.concatenate(ids, axis=0)
    wt_ref[...] = jnp.concatenate([e / tot for e in es], axis=0)
    hits = [eid == sel for sel in ids]
    member = jnp.zeros((n_exp, tm), F32)
    for hk in hits:
        member = member + jnp.where(hk, 1.0, 0.0)
    before = lax.broadcasted_iota(I32, (tm, tm), 0) < lax.broadcasted_iota(I32, (tm, tm), 1)
    prefix = jnp.dot(member.astype(BF16), jnp.where(before, 1.0, 0.0).astype(BF16), preferred_element_type=F32)
    rank_ref[...] = jnp.concatenate(
        [jnp.sum(jnp.where(hk, prefix, 0.0), axis=0, keepdims=True) for hk in hits], axis=0).astype(I32)
    cnt_ref[...] = jnp.broadcast_to(jnp.sum(member, axis=1, keepdims=True), (n_exp, LANES)).astype(I32)


def _post(x, om, og, gate, scale, shift, w_o, g_ffn, w_r2, b_r):
    t, d = x.shape
    tm = ROUTE_TILE
    per_tok = gate.shape[0] == t
    mod = pl.BlockSpec((tm, d), lambda i: (i, 0)) if per_tok else pl.BlockSpec((1, d), lambda i: (0, 0))
    tok = lambda w: pl.BlockSpec((tm, w), lambda i: (i, 0))
    full = lambda a: pl.BlockSpec(a.shape, lambda i: (0,) * a.ndim)
    return pl.pallas_call(
        _post_body,
        grid=(t // tm,),
        in_specs=[tok(d), tok(om.shape[1]), tok(og.shape[1]), mod, mod, mod, full(w_o), full(g_ffn), full(w_r2),
                  full(b_r)],
        out_specs=(tok(d), tok(d),
                   pl.BlockSpec((TOP_K, tm), lambda i: (0, i)), pl.BlockSpec((TOP_K, tm), lambda i: (0, i)),
                   pl.BlockSpec((TOP_K, tm), lambda i: (0, i)),
                   pl.BlockSpec((None, N_EXPERTS, LANES), lambda i: (i, 0, 0))),
        out_shape=(jax.ShapeDtypeStruct((t, d), F32), jax.ShapeDtypeStruct((t, d), BF16),
                   jax.ShapeDtypeStruct((TOP_K, t), I32), jax.ShapeDtypeStruct((TOP_K, t), F32),
                   jax.ShapeDtypeStruct((TOP_K, t), I32), jax.ShapeDtypeStruct((t // tm, N_EXPERTS, LANES), I32)),
        compiler_params=_cparams(("arbitrary",), VMEM_SMALL),
        name="post",
    )(x, om, og, gate, scale, shift, w_o, g_ffn, w_r2, b_r)


def _max_tile_rows():
    return TOP_K * ROUTE_TILE + N_EXPERTS * (RUN_CHUNK - 1)


def _max_big_pieces():
    return _max_tile_rows() // BIG_PIECE


def _sort_capacity():
    return _round_up(_max_tile_rows(), SORT_ROWS)


class _Tab:
    def __init__(self):
        nb = _max_big_pieces()
        self.src_big, self.dst_big = 0, nb
        self.src_small, self.dst_small = 2 * nb, 2 * nb + N_EXPERTS
        self.n_big = 2 * nb + 2 * N_EXPERTS
        self.n_small, self.n_rows = self.n_big + 1, self.n_big + 2
        self.width = self.n_big + 3


def _route_tables(idx, lrank, cnt3):
    nt = cnt3.shape[0]
    t = idx.shape[1]
    cnt = cnt3[:, :, 0]
    run = _round_up(cnt, RUN_CHUNK)
    lo_end = jnp.cumsum(run, axis=1)
    lo = lo_end - run
    region = _round_up(jnp.sum(run, axis=0), EXPERT_ROWS)
    g_end = jnp.cumsum(region)
    run_dest = (g_end - region)[None, :] + jnp.cumsum(run, axis=0) - run
    n_ch = run // RUN_CHUNK
    big, small = n_ch // 2, n_ch % 2

    def piece_list(count, n_out):
        end = jnp.cumsum(count, axis=1)
        p = jnp.arange(n_out, dtype=I32)
        e = jnp.minimum(jnp.sum(end[:, None, :] <= p[None, :, None], axis=2), N_EXPERTS - 1)
        pick = e[:, :, None] == jnp.arange(N_EXPERTS, dtype=I32)[None, None, :]
        of_run = lambda a: jnp.sum(jnp.where(pick, a[:, None, :], 0), axis=2)
        return of_run, p[None, :] - of_run(end - count), end[:, -1]

    of_big, within_big, n_big = piece_list(big, _max_big_pieces())
    of_small, _, n_small = piece_list(small, N_EXPERTS)
    table = jnp.concatenate([
        of_big(lo) + BIG_PIECE * within_big, of_big(run_dest) + BIG_PIECE * within_big,
        of_small(lo + BIG_PIECE * big), of_small(run_dest + BIG_PIECE * big),
        n_big[:, None], n_small[:, None], lo_end[:, -1:]], axis=1).astype(I32)
    table = table.reshape(nt, 1, _Tab().width)
    eid =jnp.arange(N_EXPERTS, dtype=I32)[:, None]
    lo_tok = jnp.repeat(lo.T, ROUTE_TILE, axis=1)
    lpos = jnp.stack([jnp.sum(jnp.where(idx[k][None, :] == eid, lo_tok, 0), axis=0) for k in range(TOP_K)])
    lpos = (lpos + lrank).astype(I32)
    n_blocks = _round_up(t * TOP_K + nt * N_EXPERTS * (RUN_CHUNK - 1), EXPERT_ROWS) // EXPERT_ROWS + N_EXPERTS
    b_start = jnp.arange(n_blocks, dtype=I32) * EXPERT_ROWS
    blk_e = jnp.minimum(jnp.sum(g_end[None, :] <= b_start[:, None], axis=1), N_EXPERTS - 1).astype(I32)
    n_valid = (g_end[-1:] // EXPERT_ROWS).astype(I32)
    used = region > 0
    e_ids = jnp.arange(N_EXPERTS, dtype=I32)
    later_used = used[None, :] & (e_ids[None, :] > e_ids[:, None])
    nxt_e = jnp.min(jnp.where(later_used, e_ids[None, :], N_EXPERTS), axis=1)
    nxt_e = jnp.where(nxt_e < N_EXPERTS, nxt_e, -1).astype(I32)
    ord_e = (jnp.cumsum(used.astype(I32)) - 1).astype(I32)
    tail = jnp.concatenate([jnp.where(used, g_end - EXPERT_ROWS, -1), n_valid]).astype(I32)
    tail = tail.reshape(1, N_EXPERTS + 1)
    return dict(table=table, lpos=lpos, blk_e=blk_e, n_valid=n_valid, nxt_e=nxt_e, ord_e=ord_e, tail=tail,
                n_blocks=n_blocks)


def _chunk_copy(src, dst, sem):
    return pltpu.make_async_copy(src, dst, sem)


def _for_row_blocks(n_rows, body):
    full = n_rows // SORT_ROWS
    rem = n_rows - full * SORT_ROWS
    tail = pl.multiple_of(full * SORT_ROWS, SORT_ROWS)

    def whole(rb, carry):
        body(pl.multiple_of(rb * SORT_ROWS, SORT_ROWS), SORT_ROWS)
        return carry

    lax.fori_loop(0, full, whole, 0)

    @pl.when(rem > SORT_ROWS // 2)
    def _():
        body(tail, SORT_ROWS)

    @pl.when((rem > 0) & (rem <= SORT_ROWS // 2))
    def _():
        body(tail, SORT_ROWS // 2)


def _for_chunks(n, body):
    groups = n // CHUNK_UNROLL

    def group(g, carry):
        for u in range(CHUNK_UNROLL):
            body(g * CHUNK_UNROLL + u)
        return carry

    def single(c, carry):
        body(c)
        return carry

    lax.fori_loop(0, groups, group, 0)
    lax.fori_loop(groups * CHUNK_UNROLL, n, single, 0)


def _scatter_body(nt_a, n_blocks, tab_ref, prv_ref, tail_ref, lpos_ref, ha_ref, hb_ref, xout_ref, sorted_scr,
                  zero_scr, sems, zero_sem):
    i = pl.program_id(0)
    slot = i % 2
    tb = _Tab()
    tt = ha_ref.shape[0]

    @pl.when(i == 0)
    def _():
        zero_scr[...] = jnp.zeros_like(zero_scr)
        n_valid = tail_ref[0, N_EXPERTS]

        def block(start):
            return xout_ref.at[pl.ds(pl.multiple_of(start, EXPERT_ROWS), EXPERT_ROWS)]

        for e in range(N_EXPERTS):
            @pl.when(tail_ref[0, e] >= 0)
            def _():
                _chunk_copy(zero_scr, block(tail_ref[0, e]), zero_sem).start()

        def fill(b, carry):
            _chunk_copy(zero_scr, block(b * EXPERT_ROWS), zero_sem).start()
            return carry

        def fill_done(b, carry):
            _chunk_copy(zero_scr, block(0), zero_sem).wait()
            return carry

        lax.fori_loop(n_valid, n_blocks, fill, 0)
        for e in range(N_EXPERTS):
            @pl.when(tail_ref[0, e] >= 0)
            def _():
                _chunk_copy(zero_scr, block(0), zero_sem).wait()
        lax.fori_loop(n_valid, n_blocks, fill_done, 0)

    lp16 = lpos_ref[...].astype(jnp.int16)
    h = jnp.where(i < nt_a, ha_ref[...], hb_ref[...])

    def sort_block(r0, size):
        rid = (r0 + lax.broadcasted_iota(I32, (size, tt), 0)).astype(jnp.int16)
        onehot = jnp.zeros((size, tt), BF16)
        for k in range(TOP_K):
            onehot = onehot + jnp.where(lp16[k:k + 1, :] == rid, jnp.ones((), BF16), jnp.zeros((), BF16))
        sorted_scr[slot, pl.ds(r0, size), :] = jnp.dot(onehot, h, preferred_element_type=F32).astype(BF16)

    _for_row_blocks(tab_ref[0, tb.n_rows], sort_block)

    def piece(sl, src_row, dst_row, rows):
        src = sorted_scr.at[sl, pl.ds(pl.multiple_of(src_row, RUN_CHUNK), rows)]
        dst = xout_ref.at[pl.ds(pl.multiple_of(dst_row, RUN_CHUNK), rows)]
        return _chunk_copy(src, dst, sems.at[sl])

    def retire(tab, sl):
        _for_chunks(tab[0, tb.n_big], lambda p: piece(sl, 0, 0, BIG_PIECE).wait())
        _for_chunks(tab[0, tb.n_small], lambda p: piece(sl, 0, 0, RUN_CHUNK).wait())

    _for_chunks(tab_ref[0, tb.n_big], lambda p: piece(
        slot, tab_ref[0, tb.src_big + p], tab_ref[0, tb.dst_big + p], BIG_PIECE).start())
    _for_chunks(tab_ref[0, tb.n_small], lambda p: piece(
        slot, tab_ref[0, tb.src_small + p], tab_ref[0, tb.dst_small + p], RUN_CHUNK).start())

    @pl.when(i > 0)
    def _():
        retire(prv_ref, 1 - slot)

    @pl.when(i == pl.num_programs(0) - 1)
    def _():
        retire(tab_ref, slot)


def _scatter(table, tail, lpos, h_a, h_b, n_blocks):
    d = h_a.shape[1]
    nt_a, nt_b = h_a.shape[0] // ROUTE_TILE, h_b.shape[0] // ROUTE_TILE
    sort_cap = _sort_capacity()
    tab_spec = lambda f: pl.BlockSpec((None, 1, _Tab().width), lambda i: (f(i), 0, 0), memory_space=pltpu.SMEM)
    return pl.pallas_call(
        functools.partial(_scatter_body, nt_a, n_blocks),
        grid=(nt_a + nt_b,),
        in_specs=[tab_spec(lambda i: i), tab_spec(lambda i: jnp.maximum(i - 1, 0)),
                  pl.BlockSpec((1, N_EXPERTS + 1), lambda i: (0, 0), memory_space=pltpu.SMEM),
                  pl.BlockSpec((TOP_K, ROUTE_TILE), lambda i: (0, i)),
                  pl.BlockSpec((ROUTE_TILE, d), lambda i: (jnp.minimum(i, nt_a - 1), 0)),
                  pl.BlockSpec((ROUTE_TILE, d), lambda i: (jnp.maximum(i - nt_a, 0), 0))],
        out_specs=pl.BlockSpec(memory_space=pl.ANY),
        out_shape=jax.ShapeDtypeStruct((n_blocks * EXPERT_ROWS, d), BF16),
        scratch_shapes=[pltpu.VMEM((2, sort_cap, d), BF16), pltpu.VMEM((EXPERT_ROWS, d), BF16),
                        pltpu.SemaphoreType.DMA((2,)), pltpu.SemaphoreType.DMA(())],
        compiler_params=_cparams(("arbitrary",), VMEM_MID),
        name="scatter",
    )(table, table, tail, lpos, h_a, h_b)


def _experts_body(be_ref, nv_ref, nxt_ref, ord_ref, x_ref, wgu_hbm, bgu_ref, wd_hbm, bd_ref, y_ref,
                  wgu_f, wd_f, wgu_s, wd_s, sem_gu, sem_d):
    b = pl.program_id(0)
    e = be_ref[b]
    prev = be_ref[jnp.maximum(b - 1, 0)]
    valid = b < nv_ref[0]
    d_ff = wd_s.shape[0]
    slot = ord_ref[e] % 2

    def weights(expert, sl):
        return (pltpu.make_async_copy(wgu_hbm.at[expert], wgu_f.at[sl], sem_gu.at[sl]),
                pltpu.make_async_copy(wd_hbm.at[expert], wd_f.at[sl], sem_d.at[sl]))

    @pl.when(valid & ((b == 0) | (e != prev)))
    def _():
        @pl.when(b == 0)
        def _():
            for cp in weights(e, slot):
                cp.start()

        for cp in weights(e, slot):
            cp.wait()

        @pl.when(nxt_ref[e] >= 0)
        def _():
            for cp in weights(nxt_ref[e], 1 - slot):
                cp.start()

        wgu_s[...] = wgu_f[slot].astype(BF16)
        wd_s[...] = wd_f[slot].astype(BF16)

    @pl.when(valid)
    def _():
        gu = jnp.dot(x_ref[...], wgu_s[...], preferred_element_type=F32) + bgu_ref[...]
        gate = jnp.minimum(gu[:, 0:d_ff], SWIGLU_LIMIT)
        up = jnp.clip(gu[:, d_ff:2 * d_ff], -SWIGLU_LIMIT, SWIGLU_LIMIT)
        act = ((up + 1.0) * (gate * jax.nn.sigmoid(gate * SWIGLU_ALPHA))).astype(BF16)
        y_ref[...] = (jnp.dot(act, wd_s[...], preferred_element_type=F32) + bd_ref[...]).astype(BF16)

    @pl.when(jnp.logical_not(valid))
    def _():
        y_ref[...] = jnp.zeros_like(y_ref)


def _experts(blk_e, n_valid, nxt_e, ord_e, xpad, w_gu, b_gu, w_down, b_down):
    m, d = xpad.shape
    nb = m // EXPERT_ROWS
    n_exp, _, f2 = w_gu.shape
    d_ff = w_down.shape[1]
    last = lambda b, be, nv: jnp.minimum(b, nv[0] - 1)
    grid_spec = pltpu.PrefetchScalarGridSpec(
        num_scalar_prefetch=4,
        grid=(nb,),
        in_specs=[pl.BlockSpec((EXPERT_ROWS, d), lambda b, be, nv, nx, od: (last(b, be, nv), 0)),
                  pl.BlockSpec(memory_space=pl.ANY),
                  pl.BlockSpec((None, 1, f2), lambda b, be, nv, nx, od: (be[last(b, be, nv)], 0, 0)),
                  pl.BlockSpec(memory_space=pl.ANY),
                  pl.BlockSpec((None, 1, d), lambda b, be, nv, nx, od: (be[last(b, be, nv)], 0, 0))],
        out_specs=pl.BlockSpec((EXPERT_ROWS, d), lambda b, be, nv, nx, od: (b, 0)),
        scratch_shapes=[pltpu.VMEM((2, d, f2), F32), pltpu.VMEM((2, d_ff, d), F32),
                        pltpu.VMEM((d, f2), BF16), pltpu.VMEM((d_ff, d), BF16),
                        pltpu.SemaphoreType.DMA((2,)), pltpu.SemaphoreType.DMA((2,))],
    )
    return pl.pallas_call(
        _experts_body,
        grid_spec=grid_spec,
        out_shape=jax.ShapeDtypeStruct((m, d), BF16),
        compiler_params=_cparams(("arbitrary",), VMEM_BIG),
        name="experts",
    )(blk_e, n_valid, nxt_e, ord_e, xpad, w_gu, b_gu.reshape(n_exp, 1, f2), w_down, b_down.reshape(n_exp, 1, d))


def _combine_body(tab_ref, nxt_ref, lpos_ref, wt_ref, x2_ref, gt_ref, y_ref, o_ref, ysort_scr, sems):
    i = pl.program_id(0)
    slot = i % 2
    tb = _Tab()
    tt = x2_ref.shape[0]

    def piece(sl, sorted_row, expert_row, rows):
        src = y_ref.at[pl.ds(pl.multiple_of(expert_row, RUN_CHUNK), rows)]
        dst = ysort_scr.at[sl, pl.ds(pl.multiple_of(sorted_row, RUN_CHUNK), rows)]
        return _chunk_copy(src, dst, sems.at[sl])

    def fetch(tab, sl):
        _for_chunks(tab[0, tb.n_big], lambda p: piece(
            sl, tab[0, tb.src_big + p], tab[0, tb.dst_big + p], BIG_PIECE).start())
        _for_chunks(tab[0, tb.n_small], lambda p: piece(
            sl, tab[0, tb.src_small + p], tab[0, tb.dst_small + p], RUN_CHUNK).start())

    @pl.when(i == 0)
    def _():
        ysort_scr[...] = jnp.zeros_like(ysort_scr)
        fetch(tab_ref, 0)

    @pl.when(i + 1 < pl.num_programs(0))
    def _():
        fetch(nxt_ref, 1 - slot)

    _for_chunks(tab_ref[0, tb.n_big], lambda p: piece(slot, 0, 0, BIG_PIECE).wait())
    _for_chunks(tab_ref[0, tb.n_small], lambda p: piece(slot, 0, 0, RUN_CHUNK).wait())

    lp16 = lpos_ref[...].astype(jnp.int16)
    w = wt_ref[...].astype(BF16)
    o_ref[...] = x2_ref[...]
    gt = gt_ref[...]

    def gather_block(r0, size):
        rid = (r0 + lax.broadcasted_iota(I32, (size, tt), 0)).astype(jnp.int16)
        pw_t = jnp.zeros((size, tt), BF16)
        for k in range(TOP_K):
            pw_t = pw_t + jnp.where(lp16[k:k + 1, :] == rid, w[k:k + 1, :], jnp.zeros((), BF16))
        ys = ysort_scr[slot, pl.ds(r0, size), :]
        part = lax.dot_general(pw_t, ys, (((0,), (0,)), ((), ())), preferred_element_type=F32)
        o_ref[...] = o_ref[...] + gt * part

    _for_row_blocks(tab_ref[0, tb.n_rows], gather_block)


def _combine(table, lpos, wts, x2, gate, ypad):
    t, d = x2.shape
    nt = t // ROUTE_TILE
    per_tok = gate.shape[0] == t
    mod = pl.BlockSpec((ROUTE_TILE, d), lambda i: (i, 0)) if per_tok else pl.BlockSpec((1, d), lambda i: (0, 0))
    sort_cap = _sort_capacity()
    tab_spec = lambda f: pl.BlockSpec((None, 1, _Tab().width), lambda i: (f(i), 0, 0), memory_space=pltpu.SMEM)
    return pl.pallas_call(
        _combine_body,
        grid=(nt,),
        in_specs=[tab_spec(lambda i: i), tab_spec(lambda i: jnp.minimum(i + 1, nt - 1)),
                  pl.BlockSpec((TOP_K, ROUTE_TILE), lambda i: (0, i)),
                  pl.BlockSpec((TOP_K, ROUTE_TILE), lambda i: (0, i)),
                  pl.BlockSpec((ROUTE_TILE, d), lambda i: (i, 0)),
                  mod,
                  pl.BlockSpec(memory_space=pl.ANY)],
        out_specs=pl.BlockSpec((ROUTE_TILE, d), lambda i: (i, 0)),
        out_shape=jax.ShapeDtypeStruct((t, d), F32),
        scratch_shapes=[pltpu.VMEM((2, sort_cap, d), BF16), pltpu.SemaphoreType.DMA((2,))],
        compiler_params=_cparams(("arbitrary",), VMEM_MID),
        name="combine",
    )(table, table, lpos, wts, x2, gate, ypad)


def _prep_weights(g_norm_mix, w_in, g_q_a, w_uq, g_kv_a, w_ukv, g_qk_q, g_qk_k, w_g2, b_g2, g_gla_out, w_o,
                  g_norm_ffn, w_router, b_router):
    d = w_in.shape[0]
    w1 = _w1(w_in)
    wq = w_uq.reshape(Q_LORA, HEADS, QK)
    wuq = jnp.concatenate([wq[:, :, 0:NOPE].reshape(Q_LORA, HEADS * NOPE),
                           wq[:, :, NOPE:QK].reshape(Q_LORA, HEADS * ROPE)], axis=1).astype(BF16)
    wkv = w_ukv.reshape(KV_LORA, HEADS, NOPE + V_DIM)
    wuk = wkv[:, :, 0:NOPE].reshape(KV_LORA, HEADS * NOPE).astype(BF16)
    wuv_t = wkv[:, :, NOPE:].reshape(KV_LORA, HEADS * V_DIM).T.astype(BF16)
    pad_rope = lambda g: jnp.stack([g[0:NOPE], jnp.concatenate([g[NOPE:QK], jnp.zeros((QK_PAD - QK,), g.dtype)])])
    inv = ROPE_THETA ** (-jnp.arange(HALF, dtype=F32) / HALF)
    sign = jnp.concatenate([-jnp.ones((HALF,), F32), jnp.ones((HALF,), F32)])
    rope_tab = jnp.stack([jnp.tile(inv, LANES // HALF), jnp.tile(sign, LANES // ROPE)])
    wg2 = jnp.concatenate([w_g2, jnp.zeros((LANES - GATE_RANK, w_g2.shape[1]), w_g2.dtype)], axis=0).astype(BF16)
    wr_t = w_router.T
    wr_hi = wr_t.astype(BF16)
    wr_lo = (wr_t - wr_hi.astype(F32)).astype(BF16)
    return dict(
        g_mix=g_norm_mix.reshape(1, d), w1=w1, g_qa=g_q_a.reshape(1, -1), w_uq=wuq, g_kv=g_kv_a.reshape(1, -1),
        w_uk=wuk, w_uv_t=wuv_t, g_qk_q=pad_rope(g_qk_q), g_qk_k=pad_rope(g_qk_k), rope=rope_tab, w_g2=wg2,
        b_g2=b_g2.reshape(1, -1), g_out=g_gla_out.reshape(1, -1), w_o=w_o.astype(BF16),
        g_ffn=g_norm_ffn.reshape(1, d), w_r2=jnp.stack([wr_hi, wr_lo]), b_r=b_router.reshape(-1, 1))


def _mixer(x, mod, pos0, past_lat, past_kr, s0_pairs, wts):
    b, s, d = x.shape
    t = b * s
    if b > 1 and s & (s - 1) == 0 and min(t, TOKEN_TILE) % s == 0:
        per_tok = lambda j: jnp.broadcast_to(mod[:, j:j + 1], (b, s, d)).reshape(1, t, d)
        outs = _proj(x.reshape(1, t, d), per_tok(0), per_tok(1), pos0, wts, period=s)
        q = outs[0].reshape(HEADS, b, s, QK_PAD).transpose(1, 0, 2, 3)
        lat, kr, gq, gk, gv, gl, gr = [o.reshape(b, s, o.shape[-1]) for o in outs[1:]]
    else:
        q, lat, kr, gq, gk, gv, gl, gr = _proj(x, mod[:, 0:1], mod[:, 1:2], pos0, wts)
    kv_w = (wts["w_uk"], wts["w_uv_t"], wts["g_qk_k"])
    if past_lat is None:
        k_new, vt_new = _kv(lat, kr, *kv_w)
        o_mla = _attn_prompt(q, k_new, vt_new)
    else:
        o_mla = _attn_sample(q, past_lat, past_kr, lat, kr, *kv_w)
    o_gla, s_fin = _gla(gq, gk, gl, gv, gr, s0_pairs, wts["g_out"])
    if b == 1:
        rows = lambda j: mod[0, j:j + 1]
    else:
        rows = lambda j: jnp.broadcast_to(mod[:, j:j + 1], (b, s, d)).reshape(t, d)
    x2, h2, idx, wt, lrank, cnt = _post(x.reshape(t, d), o_mla.reshape(t, -1), o_gla.reshape(t, -1), rows(2), rows(4),
                                        rows(3), wts["w_o"], wts["g_ffn"], wts["w_r2"], wts["b_r"])
    return dict(x2=x2, h2=h2, idx=idx, wt=wt, lrank=lrank, cnt=cnt, gate_f=rows(5), lat=lat, kr=kr, s_fin=s_fin)


def kernel(x_prompt, x_sample, cache_mla_latent, cache_mla_krope, state_gla, c_prompt, c_sample, w_ada, b_ada, g_norm_mix, w_in, g_q_a, w_uq, g_kv_a, w_ukv, g_qk_q, g_qk_k, w_g2, b_g2, g_gla_out, w_o, g_norm_ffn, w_router, b_router, w_gu, b_gu, w_down, b_down):
    depth = w_ada.shape[0]
    assert depth == 1, "single-layer step"
    bp, sp, d = x_prompt.shape
    bs, ss, _ = x_sample.shape
    tp, tsm = bp * sp, bs * ss
    assert tp % ROUTE_TILE == 0 and tsm % ROUTE_TILE == 0, "token counts must be whole routing tiles"
    past = cache_mla_latent.shape[2]
    layer = lambda a: a.reshape(a.shape[1:])
    wts = _prep_weights(*[layer(a) for a in (g_norm_mix, w_in, g_q_a, w_uq, g_kv_a, w_ukv, g_qk_q, g_qk_k, w_g2, b_g2,
                                             g_gla_out, w_o, g_norm_ffn, w_router, b_router)])
    w_gu, b_gu, w_down, b_down = layer(w_gu), layer(b_gu), layer(w_down), layer(b_down)

    mod = _ada(jnp.concatenate([c_prompt, c_sample], axis=0), layer(w_ada), layer(b_ada)).reshape(bp + bs, 6, d)
    zero_state = jnp.zeros((bp, HEADS // 2, GLA_DV, LANES), F32)
    pr = _mixer(x_prompt, mod[:bp], 0, None, None, zero_state, wts)
    sa = _mixer(x_sample, mod[bp:], past, layer(cache_mla_latent), layer(cache_mla_krope),
                _state_to_pairs(layer(state_gla)), wts)

    idx = jnp.concatenate([pr["idx"], sa["idx"]], axis=1)
    lrank = jnp.concatenate([pr["lrank"], sa["lrank"]], axis=1)
    rt = _route_tables(idx, lrank, jnp.concatenate([pr["cnt"], sa["cnt"]], axis=0))
    ntp = tp // ROUTE_TILE
    lpos, table = rt["lpos"], rt["table"]
    xpad = _scatter(table, rt["tail"], lpos, pr["h2"], sa["h2"], rt["n_blocks"])
    ypad = _experts(rt["blk_e"], rt["n_valid"], rt["nxt_e"], rt["ord_e"], xpad, w_gu, b_gu, w_down, b_down)
    y_p = _combine(table[:ntp], lpos[:, :tp], pr["wt"], pr["x2"], pr["gate_f"], ypad).reshape(bp, sp, d)
    y_s = _combine(table[ntp:], lpos[:, tp:], sa["wt"], sa["x2"], sa["gate_f"], ypad).reshape(bs, ss, d)

    return (y_p, y_s,
            pr["lat"][None], pr["kr"][None], _state_from_pairs(pr["s_fin"])[None],
            sa["lat"][None], sa["kr"][None], _state_from_pairs(sa["s_fin"])[None])
```

```python
import functools

import numpy as np
import jax
import jax.numpy as jnp
from jax import lax
from jax.experimental import pallas as pl
from jax.experimental.pallas import tpu as pltpu

F32 = jnp.float32
BF16 = jnp.bfloat16
I32 = jnp.int32

CHUNK = 64
EPS = 1e-6
HEADS = 4
Q_LORA = 384
KV_LORA = 256
NOPE = 128
ROPE = 64
HALF = ROPE // 2
V_DIM = 128
QK = NOPE + ROPE
QK_PAD = 256
ROPE_THETA = 10000.0
GLA_DK = 64
GLA_DV = 128
GATE_RANK = 16
GATE_NORM = 16.0
N_EXPERTS = 32
TOP_K = 4
SWIGLU_LIMIT = 7.0
SWIGLU_ALPHA = 1.702
NEG = -1e30
LOG2_E = 1.4426950408889634

LANES = 128
SUBLANES = 8
BF16_ROWS = 16
TOKEN_TILE = 512
ADA_COLS = 1536
ROUTE_TILE = 512
RUN_CHUNK = BF16_ROWS
BIG_PIECE = 2 * RUN_CHUNK
SORT_ROWS = 512
CHUNK_UNROLL = 4
EXPERT_ROWS = 256
ATTN_TILE = 1024
MIB = 1024 * 1024
VMEM_BIG = 56 * MIB
VMEM_MID = 48 * MIB
VMEM_SMALL = 40 * MIB


def _cparams(sem, vmem=None):
    return pltpu.CompilerParams(dimension_semantics=sem, vmem_limit_bytes=vmem)


def _nt(a, b):
    return lax.dot_general(a, b, (((1,), (1,)), ((), ())), preferred_element_type=F32)


def _rms(x, width):
    return lax.rsqrt(jnp.sum(x * x, axis=-1, keepdims=True) * (1.0 / width) + EPS)


def _round_up(x, m):
    return ((x + m - 1) // m) * m


def _ada_body(c_ref, w_ref, b_ref, o_ref):
    c = c_ref[...]
    s = (c * jax.nn.sigmoid(c)).astype(BF16)
    o_ref[...] = jnp.dot(s, w_ref[...].astype(BF16), preferred_element_type=F32) + b_ref[...]


def _ada(c, w_ada, b_ada):
    r, d = c.shape
    n = w_ada.shape[1]
    tn = ADA_COLS if n % ADA_COLS == 0 else n
    return pl.pallas_call(
        _ada_body,
        grid=(n // tn,),
        in_specs=[pl.BlockSpec((r, d), lambda j: (0, 0)),
                  pl.BlockSpec((d, tn), lambda j: (0, j)),
                  pl.BlockSpec((1, tn), lambda j: (0, j))],
        out_specs=pl.BlockSpec((r, tn), lambda j: (0, j)),
        out_shape=jax.ShapeDtypeStruct((r, n), F32),
        compiler_params=_cparams(("arbitrary",), VMEM_SMALL),
        name="ada",
    )(c, w_ada, b_ada.reshape(1, n))


_SEG = dict(qa_kr=(0, 512), kva=(512, 768), gq=(768, 1024), gk=(1024, 1280),
            gv=(1280, 1792), gr=(1792, 2304), glr=(2304, 2432))
_W1_COLS = 2432
W1_PREP_COLS = 256


def _w1_source_columns():
    o_kva, o_kr = Q_LORA, Q_LORA + KV_LORA
    o_gq = o_kr + ROPE
    o_gk = o_gq + HEADS * GLA_DK
    o_gv = o_gk + HEADS * GLA_DK
    o_glr = o_gv + HEADS * GLA_DV
    o_gr = o_glr + GATE_RANK
    return [(0, Q_LORA), (o_kr, ROPE), (o_kr, ROPE), (o_kva, KV_LORA), (o_gq, HEADS * GLA_DK),
            (o_gk, HEADS * GLA_DK), (o_gv, HEADS * GLA_DV), (o_gr, HEADS * GLA_DV), (o_glr, GATE_RANK)]


def _w1_body(wt_ref, w1_ref):
    o = 0
    for start, width in _w1_source_columns():
        w1_ref[o:o + width, :] = wt_ref[start:start + width, :].astype(BF16)
        o += width
    w1_ref[o:, :] = jnp.zeros((_W1_COLS - o, w1_ref.shape[1]), BF16)


def _w1(w_in_t):
    n, d = w_in_t.shape
    return pl.pallas_call(
        _w1_body,
        grid=(d // W1_PREP_COLS,),
        in_specs=[pl.BlockSpec((n, W1_PREP_COLS), lambda i: (0, i))],
        out_specs=pl.BlockSpec((_W1_COLS, W1_PREP_COLS), lambda i: (0, i)),
        out_shape=jax.ShapeDtypeStruct((_W1_COLS, d), BF16),
        compiler_params=_cparams(("arbitrary",)),
        name="w1",
    )(w_in_t)


def _proj_body(pos0, ts, period, x_ref, sh_ref, sc_ref, gmix_ref, w1_ref, gqa_ref, wuq_ref, gkv_ref, gqk_ref,
               rope_ref, wg2_ref, bg2_ref,
               q_ref, lat_ref, kr_ref, gq_o, gk_o, gv_o, gl_o, gr_o, trig_scr):
    i = pl.program_id(1)
    x = x_ref[...]
    d = x.shape[-1]
    h = (x * _rms(x, d) * gmix_ref[...]) * (1.0 + sc_ref[...]) + sh_ref[...]
    hb = h.astype(BF16)

    def seg(name):
        a, b = _SEG[name]
        return _nt(hb, w1_ref[a:b, :])

    @pl.when((pl.program_id(0) == 0) & (i == 0))
    def _():
        row = lax.broadcasted_iota(I32, (ts, LANES), 0)
        if period is not None:
            row = row & (period - 1)
        row_ang = row.astype(F32) * rope_ref[0:1, :]
        trig_scr[0] = jnp.cos(row_ang)
        trig_scr[1] = jnp.sin(row_ang)

    tile_pos = pos0 + (i * ts if period is None else 0 * i)
    base_ang = jnp.broadcast_to(tile_pos.astype(F32) * rope_ref[0:1, :], (SUBLANES, LANES))
    cos_a, sin_a = jnp.cos(base_ang)[0:1, :], jnp.sin(base_ang)[0:1, :]
    cos = cos_a * trig_scr[0] - sin_a * trig_scr[1]
    sin = (sin_a * trig_scr[0] + cos_a * trig_scr[1]) * rope_ref[1:2, :]
    lane = lax.broadcasted_iota(I32, (ts, LANES), 1)
    first_half = (lane & HALF) == 0
    low64 = lane < ROPE

    def rope(v):
        partner = jnp.where(first_half, pltpu.roll(v, LANES - HALF, 1), pltpu.roll(v, HALF, 1))
        return v * cos + partner * sin

    qa_kr = seg("qa_kr")
    qa = qa_kr[:, 0:Q_LORA]
    qn = (qa * _rms(qa, Q_LORA) * gqa_ref[...]).astype(BF16)
    qf = jnp.dot(qn, wuq_ref[...], preferred_element_type=F32)
    rope_blocks = (rope(qf[:, 4 * NOPE:4 * NOPE + LANES]), rope(qf[:, 4 * NOPE + LANES:4 * NOPE + 2 * LANES]))
    for hd in range(HEADS):
        nope = qf[:, NOPE * hd:NOPE * (hd + 1)]
        blk = rope_blocks[hd // 2]
        if hd % 2:
            blk = pltpu.roll(blk, ROPE, 1)
        blk = jnp.where(low64, blk, 0.0)
        ss = jnp.sum(nope * nope, axis=-1, keepdims=True) + jnp.sum(blk * blk, axis=-1, keepdims=True)
        scl = lax.rsqrt(ss * (1.0 / QK) + EPS) * (QK ** -0.5 * LOG2_E)
        q_ref[hd, :, 0:NOPE] = (nope * scl * gqk_ref[0:1, :]).astype(BF16)
        q_ref[hd, :, NOPE:QK_PAD] = (blk * scl * gqk_ref[1:2, :]).astype(BF16)

    kva = seg("kva")
    lat_ref[...] = kva * _rms(kva, KV_LORA) * gkv_ref[...]
    kr_ref[...] = rope(qa_kr[:, Q_LORA:Q_LORA + LANES])[:, 0:ROPE]

    gq_o[...] = seg("gq") * (GLA_DK ** -0.5)
    gk_o[...] = seg("gk")
    gv_o[...] = seg("gv").astype(BF16)
    gr_o[...] = seg("gr")
    z = jnp.dot(seg("glr").astype(BF16), wg2_ref[...], preferred_element_type=F32) + bg2_ref[...]
    gl_o[...] = (jnp.minimum(z, 0.0) - jnp.log1p(jnp.exp(-jnp.abs(z)))) * (1.0 / GATE_NORM)


def _proj(x, shift, scale, pos0, wts, period=None):
    b, s, d = x.shape
    ts = min(s, TOKEN_TILE)
    assert period is None or (period & (period - 1) == 0 and ts % period == 0)
    row = lambda a: pl.BlockSpec(a.shape, lambda bi, i: (0,) * a.ndim)
    tok = lambda w: pl.BlockSpec((None, ts, w), lambda bi, i: (bi, i, 0))
    mod = tok(d) if shift.shape[1] == s and s > 1 else pl.BlockSpec((None, 1, d), lambda bi, i: (bi, 0, 0))
    small = [wts["g_mix"], wts["w1"], wts["g_qa"], wts["w_uq"], wts["g_kv"], wts["g_qk_q"], wts["rope"],
             wts["w_g2"], wts["b_g2"]]
    out_shape = (
        jax.ShapeDtypeStruct((b, HEADS, s, QK_PAD), BF16),
        jax.ShapeDtypeStruct((b, s, KV_LORA), F32),
        jax.ShapeDtypeStruct((b, s, ROPE), F32),
        jax.ShapeDtypeStruct((b, s, HEADS * GLA_DK), F32),
        jax.ShapeDtypeStruct((b, s, HEADS * GLA_DK), F32),
        jax.ShapeDtypeStruct((b, s, HEADS * GLA_DV), BF16),
        jax.ShapeDtypeStruct((b, s, HEADS * GLA_DK), F32),
        jax.ShapeDtypeStruct((b, s, HEADS * GLA_DV), F32),
    )
    out_specs = (
        pl.BlockSpec((None, HEADS, ts, QK_PAD), lambda bi, i: (bi, 0, i, 0)),
        tok(KV_LORA), tok(ROPE), tok(HEADS * GLA_DK), tok(HEADS * GLA_DK), tok(HEADS * GLA_DV),
        tok(HEADS * GLA_DK), tok(HEADS * GLA_DV),
    )
    return pl.pallas_call(
        functools.partial(_proj_body, pos0, ts, period),
        grid=(b, s // ts),
        in_specs=[tok(d), mod, mod] + [row(a) for a in small],
        out_specs=out_specs,
        out_shape=out_shape,
        scratch_shapes=[pltpu.VMEM((2, ts, LANES), F32)],
        compiler_params=_cparams(("arbitrary", "arbitrary"), VMEM_BIG),
        name="proj",
    )(x, shift, scale, *small)


def _key_rows(lat, kr, wk_ref, gk_ref, k_out):
    kn_all = jnp.dot(lat, wk_ref[...], preferred_element_type=F32)
    kr_ss = jnp.sum(kr * kr, axis=-1, keepdims=True)
    for hd in range(HEADS):
        kn = kn_all[:, NOPE * hd:NOPE * (hd + 1)]
        scl = lax.rsqrt((jnp.sum(kn * kn, axis=-1, keepdims=True) + kr_ss) * (1.0 / QK) + EPS)
        k_out[hd, :, 0:NOPE] = (kn * scl * gk_ref[0:1, :]).astype(BF16)
        k_out[hd, :, NOPE:QK] = (kr * scl * gk_ref[1:2, 0:ROPE]).astype(BF16)
        k_out[hd, :, QK:QK_PAD] = jnp.zeros((kr.shape[0], QK_PAD - QK), BF16)


def _kv_body(lat_ref, kr_ref, wk_ref, wv_ref, gk_ref, k_ref, v_ref):
    lat = lat_ref[...].astype(BF16)
    _key_rows(lat, kr_ref[...], wk_ref, gk_ref, k_ref)
    v_t = _nt(wv_ref[...], lat)
    for hd in range(HEADS):
        v_ref[hd] = v_t[V_DIM * hd:V_DIM * (hd + 1), :].astype(BF16)


def _kv(lat, kr, w_uk, w_uv_t, g_qk_k):
    b, s, _ = lat.shape
    ts = min(s, ATTN_TILE)
    return pl.pallas_call(
        _kv_body,
        grid=(b, s // ts),
        in_specs=[pl.BlockSpec((None, ts, KV_LORA), lambda bi, i: (bi, i, 0)),
                  pl.BlockSpec((None, ts, ROPE), lambda bi, i: (bi, i, 0)),
                  pl.BlockSpec(w_uk.shape, lambda bi, i: (0, 0)),
                  pl.BlockSpec(w_uv_t.shape, lambda bi, i: (0, 0)),
                  pl.BlockSpec(g_qk_k.shape, lambda bi, i: (0, 0))],
        out_specs=(pl.BlockSpec((None, HEADS, ts, QK_PAD), lambda bi, i: (bi, 0, i, 0)),
                   pl.BlockSpec((None, HEADS, None, V_DIM, ts), lambda bi, i: (bi, 0, i, 0, 0))),
        out_shape=(jax.ShapeDtypeStruct((b, HEADS, s, QK_PAD), BF16),
                   jax.ShapeDtypeStruct((b, HEADS, s // ts, V_DIM, ts), BF16)),
        compiler_params=_cparams(("arbitrary", "arbitrary")),
        name="kv",
    )(lat, kr, w_uk, w_uv_t, g_qk_k)


def _attn_prompt_body(t, q_ref, qn_ref, k_ref, vt_ref, o_ref, s_a, s_b):
    i = pl.program_id(2)

    def scores(q, j, buf):
        buf[...] = _nt(k_ref[pl.ds(pl.multiple_of(j * t, t), t), :], q)

    def consume(j, buf, carry, masked=False):
        m, l, acc = carry
        s = buf[...]
        if masked:
            visible = (lax.broadcasted_iota(I32, (t, t), 0) // CHUNK) <= (lax.broadcasted_iota(I32, (t, t), 1) // CHUNK)
            s = jnp.where(visible, s, NEG)
        m_new = jnp.maximum(m, jnp.max(s, axis=0, keepdims=True))
        alpha = jnp.exp2(m - m_new)
        p = jnp.exp2(s - m_new)
        l = alpha * l + jnp.sum(p, axis=0, keepdims=True)
        acc = alpha * acc + jnp.dot(vt_ref[j], p.astype(BF16), preferred_element_type=F32)
        return m_new, l, acc

    def run(first, second):
        q = q_ref[...]

        @pl.when(i == 0)
        def _():
            scores(q, 0, first)

        def pair(pp, carry):
            j = 2 * pp
            scores(q, j + 1, second)
            carry = consume(j, first, carry)
            scores(q, j + 2, first)
            return consume(j + 1, second, carry)

        def even_tail(carry):
            scores(qn_ref[...], 0, second)
            return consume(i, first, carry, masked=True)

        def odd_tail(carry):
            scores(q, i, second)
            carry = consume(i - 1, first, carry)
            scores(qn_ref[...], 0, first)
            return consume(i, second, carry, masked=True)

        carry = (jnp.full((1, t), NEG, F32), jnp.zeros((1, t), F32), jnp.zeros((V_DIM, t), F32))
        carry = lax.fori_loop(0, i // 2, pair, carry)
        _, l, acc = lax.cond(i % 2 == 1, odd_tail, even_tail, carry)
        o_ref[...] = (acc / l).T.astype(BF16)

    @pl.when(((i + 1) // 2) % 2 == 0)
    def _():
        run(s_a, s_b)

    @pl.when(((i + 1) // 2) % 2 == 1)
    def _():
        run(s_b, s_a)


def _attn_prompt(q, k, v_t):
    b, _, s, _ = q.shape
    t = v_t.shape[-1]
    nq = s // t
    return pl.pallas_call(
        functools.partial(_attn_prompt_body, t),
        grid=(b, HEADS, nq),
        in_specs=[pl.BlockSpec((None, None, t, QK_PAD), lambda bi, h, i: (bi, h, i, 0)),
                  pl.BlockSpec((None, None, t, QK_PAD), lambda bi, h, i: (bi, h, jnp.minimum(i + 1, nq - 1), 0)),
                  pl.BlockSpec((None, None, s, QK_PAD), lambda bi, h, i: (bi, h, 0, 0)),
                  pl.BlockSpec((None, None, nq, V_DIM, t), lambda bi, h, i: (bi, h, 0, 0, 0))],
        out_specs=pl.BlockSpec((None, t, V_DIM), lambda bi, h, i: (bi, i, h)),
        out_shape=jax.ShapeDtypeStruct((b, s, HEADS * V_DIM), BF16),
        scratch_shapes=[pltpu.VMEM((t, t), F32), pltpu.VMEM((t, t), F32)],
        compiler_params=_cparams(("arbitrary", "arbitrary", "arbitrary"), VMEM_BIG),
        name="attn_prompt",
    )(q, q, k, v_t)


def _attn_sample_body(past, sq, q_ref, plat_ref, pkr_ref, nlat_ref, nkr_ref, wk_ref, wv_ref, gk_ref, o_ref,
                      kp_scr, kn_scr):
    plat = plat_ref[...].astype(BF16)
    nlat = nlat_ref[...].astype(BF16)
    _key_rows(plat, pkr_ref[...], wk_ref, gk_ref, kp_scr)
    _key_rows(nlat, nkr_ref[...], wk_ref, gk_ref, kn_scr)
    vp_t = _nt(wv_ref[...], plat).astype(BF16)
    vn_t = _nt(wv_ref[...], nlat).astype(BF16)
    hq = HEADS * sq
    col = lax.broadcasted_iota(I32, (sq, hq), 1)
    key_chunk = (past + lax.broadcasted_iota(I32, (sq, hq), 0)) // CHUNK
    qry_chunk = (past + col % sq) // CHUNK
    qrow = lax.broadcasted_iota(I32, (hq, 1), 0) // sq
    q_all = jnp.concatenate([q_ref[hd] for hd in range(HEADS)], axis=0)
    s_p = jnp.zeros((past, hq), F32)
    s_n = jnp.zeros((sq, hq), F32)
    for hd in range(HEADS):
        q_h = jnp.where(qrow == hd, q_all, jnp.zeros((), BF16))
        s_p = s_p + _nt(kp_scr[hd], q_h)
        s_n = s_n + _nt(kn_scr[hd], q_h)
    s_n = jnp.where(key_chunk <= qry_chunk, s_n, NEG)
    m = jnp.maximum(jnp.max(s_p, axis=0, keepdims=True), jnp.max(s_n, axis=0, keepdims=True))
    p_p = jnp.exp2(s_p - m)
    p_n = jnp.exp2(s_n - m)
    inv_l = 1.0 / (jnp.sum(p_p, axis=0, keepdims=True) + jnp.sum(p_n, axis=0, keepdims=True))
    p_p = p_p.astype(BF16)
    p_n = p_n.astype(BF16)
    lane_head = lax.broadcasted_iota(I32, (V_DIM, hq), 1) // sq
    o_t = jnp.zeros((V_DIM, hq), F32)
    for hd in range(HEADS):
        rows = slice(V_DIM * hd, V_DIM * (hd + 1))
        o_h = (jnp.dot(vp_t[rows, :], p_p, preferred_element_type=F32)
               + jnp.dot(vn_t[rows, :], p_n, preferred_element_type=F32))
        o_t = o_t + jnp.where(lane_head == hd, o_h, 0.0)
    o_all = (o_t * inv_l).T
    for hd in range(HEADS):
        o_ref[:, V_DIM * hd:V_DIM * (hd + 1)] = o_all[sq * hd:sq * (hd + 1), :].astype(BF16)


def _attn_sample(q, past_lat, past_kr, lat, kr, w_uk, w_uv_t, g_qk_k):
    b, _, sq, _ = q.shape
    past = past_lat.shape[1]
    rows = lambda n, w: pl.BlockSpec((None, n, w), lambda bi: (bi, 0, 0))
    full = lambda a: pl.BlockSpec(a.shape, lambda bi: (0,) * a.ndim)
    return pl.pallas_call(
        functools.partial(_attn_sample_body, past, sq),
        grid=(b,),
        in_specs=[pl.BlockSpec((None, HEADS, sq, QK_PAD), lambda bi: (bi, 0, 0, 0)),
                  rows(past, KV_LORA), rows(past, ROPE), rows(sq, KV_LORA), rows(sq, ROPE),
                  full(w_uk), full(w_uv_t), full(g_qk_k)],
        out_specs=rows(sq, HEADS * V_DIM),
        out_shape=jax.ShapeDtypeStruct((b, sq, HEADS * V_DIM), BF16),
        scratch_shapes=[pltpu.VMEM((HEADS, past, QK_PAD), BF16), pltpu.VMEM((HEADS, sq, QK_PAD), BF16)],
        compiler_params=_cparams(("arbitrary",), VMEM_SMALL),
        name="attn_sample",
    )(q, past_lat, past_kr, lat, kr, w_uk, w_uv_t, g_qk_k)


def _gla_masks(c, rows):
    idx = np.arange(rows)
    same = (idx // c)[:, None] == (idx // c)[None, :]
    le = same & (idx[None, :] <= idx[:, None])
    gt = same & (idx[None, :] > idx[:, None])
    return np.concatenate([le, gt], axis=0).astype(np.float32), int(np.log2(c))


def _level_exponents(b, g, c, level):
    n = c >> level
    rows = b.shape[0]
    row = lax.broadcasted_iota(I32, (rows, 1), 0)
    if n >= 8:
        split = b.reshape(rows // n, n, LANES)[:, n // 2 - 1:n // 2, :]
        split = jnp.broadcast_to(split, (rows // n, n, LANES)).reshape(rows, LANES)
        return jnp.where((row & (n // 2)) != 0, b - split, split - b)
    g_prev = pltpu.roll(g, 1, 0)
    g_next = pltpu.roll(g, rows - 1, 0)
    if n == 4:
        r = row & 3
        return jnp.where(r == 0, g_next, jnp.where(r == 1, 0.0, jnp.where(r == 2, g, g + g_prev)))
    assert n == 2
    return jnp.where((row & 1) != 0, g, 0.0)


def _gla_body(c, n_chunks, unit, levels, mall_ref, q_ref, k_ref, g_ref, v_ref, r_ref, s0_ref, gout_ref,
              o_ref, sfin_ref, st_scr):
    it = pl.program_id(1)

    @pl.when(it == 0)
    def _():
        st_scr[...] = s0_ref[...]

    ru = unit * c
    lane = lax.broadcasted_iota(I32, (ru, LANES), 1)
    head_lanes = (lane < GLA_DK, lane >= GLA_DK)
    st_lane_lo = lax.broadcasted_iota(I32, (GLA_DV, LANES), 1) < GLA_DK
    row = lax.broadcasted_iota(I32, (ru, 1), 0)
    ri = lax.broadcasted_iota(I32, (ru, ru), 0)
    ci = lax.broadcasted_iota(I32, (ru, ru), 1)
    mall = mall_ref[...]

    for un in range(n_chunks // unit):
        rows = slice(un * ru, (un + 1) * ru)
        for p in range(HEADS // 2):
            ls = slice(LANES * p, LANES * (p + 1))
            g = g_ref[rows, ls]
            q = q_ref[rows, ls]
            k = k_ref[rows, ls]
            g_hi = g.astype(BF16)
            g_lo = (g - g_hi.astype(F32)).astype(BF16)
            e2 = jnp.dot(mall, jnp.concatenate([g_hi, g_lo], axis=1), preferred_element_type=F32)
            e = e2[:, 0:LANES] + e2[:, LANES:2 * LANES]
            b = e[0:ru]
            eb = jnp.exp(b)
            qb = q * eb
            kd = (k * jnp.exp(e[ru:2 * ru])).astype(BF16)
            qs, ks = [q], [k.astype(BF16)]
            for l in range(levels):
                bottom = (row & (c >> (l + 1))) != 0
                decay = jnp.exp(_level_exponents(b, g, c, l))
                qs.append(jnp.where(bottom, q * decay, 0.0))
                ks.append(jnp.where(bottom, 0.0, k * decay).astype(BF16))
            states = [st_scr[p]]
            for j in range(unit):
                cr = slice(j * c, (j + 1) * c)
                upd = [lax.dot_general(v_ref[rows, GLA_DV * (2 * p + hh):GLA_DV * (2 * p + hh + 1)][cr, :], kd[cr, :],
                                       (((0,), (0,)), ((), ())), preferred_element_type=F32) for hh in range(2)]
                d_last = eb[j * c + c - 1:j * c + c, :]
                states.append(states[-1] * d_last + jnp.where(st_lane_lo, upd[0], upd[1]))
            st_scr[p] = states[-1]
            for hh in range(2):
                hd = 2 * p + hh
                sel = head_lanes[hh]
                a = jnp.where(ri == ci, _nt(jnp.where(sel, qs[0], 0.0).astype(BF16), ks[0]), 0.0)
                for l in range(levels):
                    pr = _nt(jnp.where(sel, qs[l + 1], 0.0).astype(BF16), ks[l + 1])
                    a = a + jnp.where((ri ^ ci) < (c >> l), pr, 0.0)
                vh = v_ref[rows, GLA_DV * hd:GLA_DV * (hd + 1)]
                qb_h = jnp.where(sel, qb, 0.0).astype(BF16)
                o_state = [_nt(qb_h[j * c:(j + 1) * c, :], states[j].astype(BF16)) for j in range(unit)]
                o = jnp.dot(a.astype(BF16), vh, preferred_element_type=F32) + jnp.concatenate(o_state, axis=0)
                on = o * _rms(o, GLA_DV) * gout_ref[...]
                r = r_ref[rows, GLA_DV * hd:GLA_DV * (hd + 1)]
                o_ref[rows, GLA_DV * hd:GLA_DV * (hd + 1)] = (on * (r * jax.nn.sigmoid(r))).astype(BF16)

    @pl.when(it == pl.num_programs(1) - 1)
    def _():
        sfin_ref[...] = st_scr[...]


def _gla(gq, gk, gl, gv, gr, s0, g_out):
    b, s, _ = gq.shape
    c = min(CHUNK, s)
    tile = min(s, 8 * c)
    unit = next(u for u in (4, 2, 1) if (tile // c) % u == 0)
    masks, levels = _gla_masks(c, unit * c)
    mall = jnp.asarray(masks, BF16)
    tok = lambda w: pl.BlockSpec((None, tile, w), lambda bi, i: (bi, i, 0))
    st_spec = pl.BlockSpec((None, HEADS // 2, GLA_DV, LANES), lambda bi, i: (bi, 0, 0, 0))
    return pl.pallas_call(
        functools.partial(_gla_body, c, tile // c, unit, levels),
        grid=(b, s // tile),
        in_specs=[pl.BlockSpec(mall.shape, lambda bi, i: (0, 0)),
                  tok(HEADS * GLA_DK), tok(HEADS * GLA_DK), tok(HEADS * GLA_DK), tok(HEADS * GLA_DV),
                  tok(HEADS * GLA_DV), st_spec, pl.BlockSpec(g_out.shape, lambda bi, i: (0, 0))],
        out_specs=(tok(HEADS * GLA_DV), st_spec),
        out_shape=(jax.ShapeDtypeStruct((b, s, HEADS * GLA_DV), BF16),
                   jax.ShapeDtypeStruct((b, HEADS // 2, GLA_DV, LANES), F32)),
        scratch_shapes=[pltpu.VMEM((HEADS // 2, GLA_DV, LANES), F32)],
        compiler_params=_cparams(("arbitrary", "arbitrary")),
        name="gla",
    )(mall, gq, gk, gl, gv, gr, s0, g_out)


def _state_to_pairs(s):
    b = s.shape[0]
    s = s.reshape(b, HEADS // 2, 2, GLA_DK, GLA_DV)
    return jnp.transpose(s, (0, 1, 4, 2, 3)).reshape(b, HEADS // 2, GLA_DV, 2 * GLA_DK)


def _state_from_pairs(s):
    b = s.shape[0]
    s = s.reshape(b, HEADS // 2, GLA_DV, 2, GLA_DK)
    return jnp.transpose(s, (0, 1, 3, 4, 2)).reshape(b, HEADS, GLA_DK, GLA_DV)


def _post_body(x_ref, om_ref, og_ref, gt_ref, sc_ref, sh_ref, wo_ref, gffn_ref, wr_ref, br_ref,
               x2_ref, h_ref, idx_ref, wt_ref, rank_ref, cnt_ref):
    half = om_ref.shape[-1]
    mix = (jnp.dot(om_ref[...], wo_ref[0:half, :], preferred_element_type=F32)
           + jnp.dot(og_ref[...], wo_ref[half:2 * half, :], preferred_element_type=F32))
    x2 = x_ref[...] + gt_ref[...] * mix
    x2_ref[...] = x2
    d = x2.shape[-1]
    h = (x2 * _rms(x2, d) * gffn_ref[...]) * (1.0 + sc_ref[...]) + sh_ref[...]
    h_hi = h.astype(BF16)
    h_ref[...] = h_hi
    h_lo = (h - h_hi.astype(F32)).astype(BF16)
    logits = _nt(wr_ref[0], h_hi) + _nt(wr_ref[0], h_lo) + _nt(wr_ref[1], h_hi) + br_ref[...]
    n_exp, tm = logits.shape
    eid = lax.broadcasted_iota(I32, (n_exp, tm), 0)
    vals, tops, ids = logits, [], []
    for _ in range(TOP_K):
        m = jnp.max(vals, axis=0, keepdims=True)
        sel = jnp.min(jnp.where(vals == m, eid, n_exp), axis=0, keepdims=True)
        tops.append(m)
        ids.append(sel)
        vals = jnp.where(eid == sel, -jnp.inf, vals)
    es = [jnp.exp(t - tops[0]) for t in tops]
    tot = es[0] + es[1] + es[2] + es[3]
    idx_ref[...] = jnp.concatenate(ids, axis=0)
    wt_ref[...] = jnp.concatenate([e / tot for e in es], axis=0)
    hits = [eid == sel for sel in ids]
    member = jnp.zeros((n_exp, tm), F32)
    for hk in hits:
        member = member + jnp.where(hk, 1.0, 0.0)
    before = lax.broadcasted_iota(I32, (tm, tm), 0) < lax.broadcasted_iota(I32, (tm, tm), 1)
    prefix = jnp.dot(member.astype(BF16), jnp.where(before, 1.0, 0.0).astype(BF16), preferred_element_type=F32)
    rank_ref[...] = jnp.concatenate(
        [jnp.sum(jnp.where(hk, prefix, 0.0), axis=0, keepdims=True) for hk in hits], axis=0).astype(I32)
    cnt_ref[...] = jnp.broadcast_to(jnp.sum(member, axis=1, keepdims=True), (n_exp, LANES)).astype(I32)


def _post(x, om, og, gate, scale, shift, w_o, g_ffn, w_r2, b_r):
    t, d = x.shape
    tm = ROUTE_TILE
    per_tok = gate.shape[0] == t
    mod = pl.BlockSpec((tm, d), lambda i: (i, 0)) if per_tok else pl.BlockSpec((1, d), lambda i: (0, 0))
    tok = lambda w: pl.BlockSpec((tm, w), lambda i: (i, 0))
    full = lambda a: pl.BlockSpec(a.shape, lambda i: (0,) * a.ndim)
    return pl.pallas_call(
        _post_body,
        grid=(t // tm,),
        in_specs=[tok(d), tok(om.shape[1]), tok(og.shape[1]), mod, mod, mod, full(w_o), full(g_ffn), full(w_r2),
                  full(b_r)],
        out_specs=(tok(d), tok(d),
                   pl.BlockSpec((TOP_K, tm), lambda i: (0, i)), pl.BlockSpec((TOP_K, tm), lambda i: (0, i)),
                   pl.BlockSpec((TOP_K, tm), lambda i: (0, i)),
                   pl.BlockSpec((None, N_EXPERTS, LANES), lambda i: (i, 0, 0))),
        out_shape=(jax.ShapeDtypeStruct((t, d), F32), jax.ShapeDtypeStruct((t, d), BF16),
                   jax.ShapeDtypeStruct((TOP_K, t), I32), jax.ShapeDtypeStruct((TOP_K, t), F32),
                   jax.ShapeDtypeStruct((TOP_K, t), I32), jax.ShapeDtypeStruct((t // tm, N_EXPERTS, LANES), I32)),
        compiler_params=_cparams(("arbitrary",), VMEM_SMALL),
        name="post",
    )(x, om, og, gate, scale, shift, w_o, g_ffn, w_r2, b_r)


def _max_tile_rows():
    return TOP_K * ROUTE_TILE + N_EXPERTS * (RUN_CHUNK - 1)


def _max_big_pieces():
    return _max_tile_rows() // BIG_PIECE


def _sort_capacity():
    return _round_up(_max_tile_rows(), SORT_ROWS)


class _Tab:
    def __init__(self):
        nb = _max_big_pieces()
        self.src_big, self.dst_big = 0, nb
        self.src_small, self.dst_small = 2 * nb, 2 * nb + N_EXPERTS
        self.n_big = 2 * nb + 2 * N_EXPERTS
        self.n_small, self.n_rows = self.n_big + 1, self.n_big + 2
        self.width = self.n_big + 3


def _route_tables(idx, lrank, cnt3):
    nt = cnt3.shape[0]
    t = idx.shape[1]
    cnt = cnt3[:, :, 0]
    run = _round_up(cnt, RUN_CHUNK)
    lo_end = jnp.cumsum(run, axis=1)
    lo = lo_end - run
    region = _round_up(jnp.sum(run, axis=0), EXPERT_ROWS)
    g_end = jnp.cumsum(region)
    run_dest = (g_end - region)[None, :] + jnp.cumsum(run, axis=0) - run
    n_ch = run // RUN_CHUNK
    big, small = n_ch // 2, n_ch % 2

    def piece_list(count, n_out):
        end = jnp.cumsum(count, axis=1)
        p = jnp.arange(n_out, dtype=I32)
        e = jnp.minimum(jnp.sum(end[:, None, :] <= p[None, :, None], axis=2), N_EXPERTS - 1)
        pick = e[:, :, None] == jnp.arange(N_EXPERTS, dtype=I32)[None, None, :]
        of_run = lambda a: jnp.sum(jnp.where(pick, a[:, None, :], 0), axis=2)
        return of_run, p[None, :] - of_run(end - count), end[:, -1]

    of_big, within_big, n_big = piece_list(big, _max_big_pieces())
    of_small, _, n_small = piece_list(small, N_EXPERTS)
    table = jnp.concatenate([
        of_big(lo) + BIG_PIECE * within_big, of_big(run_dest) + BIG_PIECE * within_big,
        of_small(lo + BIG_PIECE * big), of_small(run_dest + BIG_PIECE * big),
        n_big[:, None], n_small[:, None], lo_end[:, -1:]], axis=1).astype(I32)
    table = table.reshape(nt, 1, _Tab().width)
    eid =jnp.arange(N_EXPERTS, dtype=I32)[:, None]
    lo_tok = jnp.repeat(lo.T, ROUTE_TILE, axis=1)
    lpos = jnp.stack([jnp.sum(jnp.where(idx[k][None, :] == eid, lo_tok, 0), axis=0) for k in range(TOP_K)])
    lpos = (lpos + lrank).astype(I32)
    n_blocks = _round_up(t * TOP_K + nt * N_EXPERTS * (RUN_CHUNK - 1), EXPERT_ROWS) // EXPERT_ROWS + N_EXPERTS
    b_start = jnp.arange(n_blocks, dtype=I32) * EXPERT_ROWS
    blk_e = jnp.minimum(jnp.sum(g_end[None, :] <= b_start[:, None], axis=1), N_EXPERTS - 1).astype(I32)
    n_valid = (g_end[-1:] // EXPERT_ROWS).astype(I32)
    used = region > 0
    e_ids = jnp.arange(N_EXPERTS, dtype=I32)
    later_used = used[None, :] & (e_ids[None, :] > e_ids[:, None])
    nxt_e = jnp.min(jnp.where(later_used, e_ids[None, :], N_EXPERTS), axis=1)
    nxt_e = jnp.where(nxt_e < N_EXPERTS, nxt_e, -1).astype(I32)
    ord_e = (jnp.cumsum(used.astype(I32)) - 1).astype(I32)
    tail = jnp.concatenate([jnp.where(used, g_end - EXPERT_ROWS, -1), n_valid]).astype(I32)
    tail = tail.reshape(1, N_EXPERTS + 1)
    return dict(table=table, lpos=lpos, blk_e=blk_e, n_valid=n_valid, nxt_e=nxt_e, ord_e=ord_e, tail=tail,
                n_blocks=n_blocks)


def _chunk_copy(src, dst, sem):
    return pltpu.make_async_copy(src, dst, sem)


def _for_row_blocks(n_rows, body):
    full = n_rows // SORT_ROWS
    rem = n_rows - full * SORT_ROWS
    tail = pl.multiple_of(full * SORT_ROWS, SORT_ROWS)

    def whole(rb, carry):
        body(pl.multiple_of(rb * SORT_ROWS, SORT_ROWS), SORT_ROWS)
        return carry

    lax.fori_loop(0, full, whole, 0)

    @pl.when(rem > SORT_ROWS // 2)
    def _():
        body(tail, SORT_ROWS)

    @pl.when((rem > 0) & (rem <= SORT_ROWS // 2))
    def _():
        body(tail, SORT_ROWS // 2)


def _for_chunks(n, body):
    groups = n // CHUNK_UNROLL

    def group(g, carry):
        for u in range(CHUNK_UNROLL):
            body(g * CHUNK_UNROLL + u)
        return carry

    def single(c, carry):
        body(c)
        return carry

    lax.fori_loop(0, groups, group, 0)
    lax.fori_loop(groups * CHUNK_UNROLL, n, single, 0)


def _scatter_body(nt_a, n_blocks, tab_ref, prv_ref, tail_ref, lpos_ref, ha_ref, hb_ref, xout_ref, sorted_scr,
                  zero_scr, sems, zero_sem):
    i = pl.program_id(0)
    slot = i % 2
    tb = _Tab()
    tt = ha_ref.shape[0]

    @pl.when(i == 0)
    def _():
        zero_scr[...] = jnp.zeros_like(zero_scr)
        n_valid = tail_ref[0, N_EXPERTS]

        def block(start):
            return xout_ref.at[pl.ds(pl.multiple_of(start, EXPERT_ROWS), EXPERT_ROWS)]

        for e in range(N_EXPERTS):
            @pl.when(tail_ref[0, e] >= 0)
            def _():
                _chunk_copy(zero_scr, block(tail_ref[0, e]), zero_sem).start()

        def fill(b, carry):
            _chunk_copy(zero_scr, block(b * EXPERT_ROWS), zero_sem).start()
            return carry

        def fill_done(b, carry):
            _chunk_copy(zero_scr, block(0), zero_sem).wait()
            return carry

        lax.fori_loop(n_valid, n_blocks, fill, 0)
        for e in range(N_EXPERTS):
            @pl.when(tail_ref[0, e] >= 0)
            def _():
                _chunk_copy(zero_scr, block(0), zero_sem).wait()
        lax.fori_loop(n_valid, n_blocks, fill_done, 0)

    lp16 = lpos_ref[...].astype(jnp.int16)
    h = jnp.where(i < nt_a, ha_ref[...], hb_ref[...])

    def sort_block(r0, size):
        rid = (r0 + lax.broadcasted_iota(I32, (size, tt), 0)).astype(jnp.int16)
        onehot = jnp.zeros((size, tt), BF16)
        for k in range(TOP_K):
            onehot = onehot + jnp.where(lp16[k:k + 1, :] == rid, jnp.ones((), BF16), jnp.zeros((), BF16))
        sorted_scr[slot, pl.ds(r0, size), :] = jnp.dot(onehot, h, preferred_element_type=F32).astype(BF16)

    _for_row_blocks(tab_ref[0, tb.n_rows], sort_block)

    def piece(sl, src_row, dst_row, rows):
        src = sorted_scr.at[sl, pl.ds(pl.multiple_of(src_row, RUN_CHUNK), rows)]
        dst = xout_ref.at[pl.ds(pl.multiple_of(dst_row, RUN_CHUNK), rows)]
        return _chunk_copy(src, dst, sems.at[sl])

    def retire(tab, sl):
        _for_chunks(tab[0, tb.n_big], lambda p: piece(sl, 0, 0, BIG_PIECE).wait())
        _for_chunks(tab[0, tb.n_small], lambda p: piece(sl, 0, 0, RUN_CHUNK).wait())

    _for_chunks(tab_ref[0, tb.n_big], lambda p: piece(
        slot, tab_ref[0, tb.src_big + p], tab_ref[0, tb.dst_big + p], BIG_PIECE).start())
    _for_chunks(tab_ref[0, tb.n_small], lambda p: piece(
        slot, tab_ref[0, tb.src_small + p], tab_ref[0, tb.dst_small + p], RUN_CHUNK).start())

    @pl.when(i > 0)
    def _():
        retire(prv_ref, 1 - slot)

    @pl.when(i == pl.num_programs(0) - 1)
    def _():
        retire(tab_ref, slot)


def _scatter(table, tail, lpos, h_a, h_b, n_blocks):
    d = h_a.shape[1]
    nt_a, nt_b = h_a.shape[0] // ROUTE_TILE, h_b.shape[0] // ROUTE_TILE
    sort_cap = _sort_capacity()
    tab_spec = lambda f: pl.BlockSpec((None, 1, _Tab().width), lambda i: (f(i), 0, 0), memory_space=pltpu.SMEM)
    return pl.pallas_call(
        functools.partial(_scatter_body, nt_a, n_blocks),
        grid=(nt_a + nt_b,),
        in_specs=[tab_spec(lambda i: i), tab_spec(lambda i: jnp.maximum(i - 1, 0)),
                  pl.BlockSpec((1, N_EXPERTS + 1), lambda i: (0, 0), memory_space=pltpu.SMEM),
                  pl.BlockSpec((TOP_K, ROUTE_TILE), lambda i: (0, i)),
                  pl.BlockSpec((ROUTE_TILE, d), lambda i: (jnp.minimum(i, nt_a - 1), 0)),
                  pl.BlockSpec((ROUTE_TILE, d), lambda i: (jnp.maximum(i - nt_a, 0), 0))],
        out_specs=pl.BlockSpec(memory_space=pl.ANY),
        out_shape=jax.ShapeDtypeStruct((n_blocks * EXPERT_ROWS, d), BF16),
        scratch_shapes=[pltpu.VMEM((2, sort_cap, d), BF16), pltpu.VMEM((EXPERT_ROWS, d), BF16),
                        pltpu.SemaphoreType.DMA((2,)), pltpu.SemaphoreType.DMA(())],
        compiler_params=_cparams(("arbitrary",), VMEM_MID),
        name="scatter",
    )(table, table, tail, lpos, h_a, h_b)


def _experts_body(be_ref, nv_ref, nxt_ref, ord_ref, x_ref, wgu_hbm, bgu_ref, wd_hbm, bd_ref, y_ref,
                  wgu_f, wd_f, wgu_s, wd_s, sem_gu, sem_d):
    b = pl.program_id(0)
    e = be_ref[b]
    prev = be_ref[jnp.maximum(b - 1, 0)]
    valid = b < nv_ref[0]
    d_ff = wd_s.shape[0]
    slot = ord_ref[e] % 2

    def weights(expert, sl):
        return (pltpu.make_async_copy(wgu_hbm.at[expert], wgu_f.at[sl], sem_gu.at[sl]),
                pltpu.make_async_copy(wd_hbm.at[expert], wd_f.at[sl], sem_d.at[sl]))

    @pl.when(valid & ((b == 0) | (e != prev)))
    def _():
        @pl.when(b == 0)
        def _():
            for cp in weights(e, slot):
                cp.start()

        for cp in weights(e, slot):
            cp.wait()

        @pl.when(nxt_ref[e] >= 0)
        def _():
            for cp in weights(nxt_ref[e], 1 - slot):
                cp.start()

        wgu_s[...] = wgu_f[slot].astype(BF16)
        wd_s[...] = wd_f[slot].astype(BF16)

    @pl.when(valid)
    def _():
        gu = jnp.dot(x_ref[...], wgu_s[...], preferred_element_type=F32) + bgu_ref[...]
        gate = jnp.minimum(gu[:, 0:d_ff], SWIGLU_LIMIT)
        up = jnp.clip(gu[:, d_ff:2 * d_ff], -SWIGLU_LIMIT, SWIGLU_LIMIT)
        act = ((up + 1.0) * (gate * jax.nn.sigmoid(gate * SWIGLU_ALPHA))).astype(BF16)
        y_ref[...] = (jnp.dot(act, wd_s[...], preferred_element_type=F32) + bd_ref[...]).astype(BF16)

    @pl.when(jnp.logical_not(valid))
    def _():
        y_ref[...] = jnp.zeros_like(y_ref)


def _experts(blk_e, n_valid, nxt_e, ord_e, xpad, w_gu, b_gu, w_down, b_down):
    m, d = xpad.shape
    nb = m // EXPERT_ROWS
    n_exp, _, f2 = w_gu.shape
    d_ff = w_down.shape[1]
    last = lambda b, be, nv: jnp.minimum(b, nv[0] - 1)
    grid_spec = pltpu.PrefetchScalarGridSpec(
        num_scalar_prefetch=4,
        grid=(nb,),
        in_specs=[pl.BlockSpec((EXPERT_ROWS, d), lambda b, be, nv, nx, od: (last(b, be, nv), 0)),
                  pl.BlockSpec(memory_space=pl.ANY),
                  pl.BlockSpec((None, 1, f2), lambda b, be, nv, nx, od: (be[last(b, be, nv)], 0, 0)),
                  pl.BlockSpec(memory_space=pl.ANY),
                  pl.BlockSpec((None, 1, d), lambda b, be, nv, nx, od: (be[last(b, be, nv)], 0, 0))],
        out_specs=pl.BlockSpec((EXPERT_ROWS, d), lambda b, be, nv, nx, od: (b, 0)),
        scratch_shapes=[pltpu.VMEM((2, d, f2), F32), pltpu.VMEM((2, d_ff, d), F32),
                        pltpu.VMEM((d, f2), BF16), pltpu.VMEM((d_ff, d), BF16),
                        pltpu.SemaphoreType.DMA((2,)), pltpu.SemaphoreType.DMA((2,))],
    )
    return pl.pallas_call(
        _experts_body,
        grid_spec=grid_spec,
        out_shape=jax.ShapeDtypeStruct((m, d), BF16),
        compiler_params=_cparams(("arbitrary",), VMEM_BIG),
        name="experts",
    )(blk_e, n_valid, nxt_e, ord_e, xpad, w_gu, b_gu.reshape(n_exp, 1, f2), w_down, b_down.reshape(n_exp, 1, d))


def _combine_body(tab_ref, nxt_ref, lpos_ref, wt_ref, x2_ref, gt_ref, y_ref, o_ref, ysort_scr, sems):
    i = pl.program_id(0)
    slot = i % 2
    tb = _Tab()
    tt = x2_ref.shape[0]

    def piece(sl, sorted_row, expert_row, rows):
        src = y_ref.at[pl.ds(pl.multiple_of(expert_row, RUN_CHUNK), rows)]
        dst = ysort_scr.at[sl, pl.ds(pl.multiple_of(sorted_row, RUN_CHUNK), rows)]
        return _chunk_copy(src, dst, sems.at[sl])

    def fetch(tab, sl):
        _for_chunks(tab[0, tb.n_big], lambda p: piece(
            sl, tab[0, tb.src_big + p], tab[0, tb.dst_big + p], BIG_PIECE).start())
        _for_chunks(tab[0, tb.n_small], lambda p: piece(
            sl, tab[0, tb.src_small + p], tab[0, tb.dst_small + p], RUN_CHUNK).start())

    @pl.when(i == 0)
    def _():
        ysort_scr[...] = jnp.zeros_like(ysort_scr)
        fetch(tab_ref, 0)

    @pl.when(i + 1 < pl.num_programs(0))
    def _():
        fetch(nxt_ref, 1 - slot)

    _for_chunks(tab_ref[0, tb.n_big], lambda p: piece(slot, 0, 0, BIG_PIECE).wait())
    _for_chunks(tab_ref[0, tb.n_small], lambda p: piece(slot, 0, 0, RUN_CHUNK).wait())

    lp16 = lpos_ref[...].astype(jnp.int16)
    w = wt_ref[...].astype(BF16)
    o_ref[...] = x2_ref[...]
    gt = gt_ref[...]

    def gather_block(r0, size):
        rid = (r0 + lax.broadcasted_iota(I32, (size, tt), 0)).astype(jnp.int16)
        pw_t = jnp.zeros((size, tt), BF16)
        for k in range(TOP_K):
            pw_t = pw_t + jnp.where(lp16[k:k + 1, :] == rid, w[k:k + 1, :], jnp.zeros((), BF16))
        ys = ysort_scr[slot, pl.ds(r0, size), :]
        part = lax.dot_general(pw_t, ys, (((0,), (0,)), ((), ())), preferred_element_type=F32)
        o_ref[...] = o_ref[...] + gt * part

    _for_row_blocks(tab_ref[0, tb.n_rows], gather_block)


def _combine(table, lpos, wts, x2, gate, ypad):
    t, d = x2.shape
    nt = t // ROUTE_TILE
    per_tok = gate.shape[0] == t
    mod = pl.BlockSpec((ROUTE_TILE, d), lambda i: (i, 0)) if per_tok else pl.BlockSpec((1, d), lambda i: (0, 0))
    sort_cap = _sort_capacity()
    tab_spec = lambda f: pl.BlockSpec((None, 1, _Tab().width), lambda i: (f(i), 0, 0), memory_space=pltpu.SMEM)
    return pl.pallas_call(
        _combine_body,
        grid=(nt,),
        in_specs=[tab_spec(lambda i: i), tab_spec(lambda i: jnp.minimum(i + 1, nt - 1)),
                  pl.BlockSpec((TOP_K, ROUTE_TILE), lambda i: (0, i)),
                  pl.BlockSpec((TOP_K, ROUTE_TILE), lambda i: (0, i)),
                  pl.BlockSpec((ROUTE_TILE, d), lambda i: (i, 0)),
                  mod,
                  pl.BlockSpec(memory_space=pl.ANY)],
        out_specs=pl.BlockSpec((ROUTE_TILE, d), lambda i: (i, 0)),
        out_shape=jax.ShapeDtypeStruct((t, d), F32),
        scratch_shapes=[pltpu.VMEM((2, sort_cap, d), BF16), pltpu.SemaphoreType.DMA((2,))],
        compiler_params=_cparams(("arbitrary",), VMEM_MID),
        name="combine",
    )(table, table, lpos, wts, x2, gate, ypad)


def _prep_weights(g_norm_mix, w_in, g_q_a, w_uq, g_kv_a, w_ukv, g_qk_q, g_qk_k, w_g2, b_g2, g_gla_out, w_o,
                  g_norm_ffn, w_router, b_router):
    d = w_in.shape[0]
    w1 = _w1(w_in.T)
    wq = w_uq.reshape(Q_LORA, HEADS, QK)
    wuq = jnp.concatenate([wq[:, :, 0:NOPE].reshape(Q_LORA, HEADS * NOPE),
                           wq[:, :, NOPE:QK].reshape(Q_LORA, HEADS * ROPE)], axis=1).astype(BF16)
    wkv = w_ukv.reshape(KV_LORA, HEADS, NOPE + V_DIM)
    wuk = wkv[:, :, 0:NOPE].reshape(KV_LORA, HEADS * NOPE).astype(BF16)
    wuv_t = wkv[:, :, NOPE:].reshape(KV_LORA, HEADS * V_DIM).T.astype(BF16)
    pad_rope = lambda g: jnp.stack([g[0:NOPE], jnp.concatenate([g[NOPE:QK], jnp.zeros((QK_PAD - QK,), g.dtype)])])
    inv = ROPE_THETA ** (-jnp.arange(HALF, dtype=F32) / HALF)
    sign = jnp.concatenate([-jnp.ones((HALF,), F32), jnp.ones((HALF,), F32)])
    rope_tab = jnp.stack([jnp.tile(inv, LANES // HALF), jnp.tile(sign, LANES // ROPE)])
    wg2 = jnp.concatenate([w_g2, jnp.zeros((LANES - GATE_RANK, w_g2.shape[1]), w_g2.dtype)], axis=0).astype(BF16)
    wr_t = w_router.T
    wr_hi = wr_t.astype(BF16)
    wr_lo = (wr_t - wr_hi.astype(F32)).astype(BF16)
    return dict(
        g_mix=g_norm_mix.reshape(1, d), w1=w1, g_qa=g_q_a.reshape(1, -1), w_uq=wuq, g_kv=g_kv_a.reshape(1, -1),
        w_uk=wuk, w_uv_t=wuv_t, g_qk_q=pad_rope(g_qk_q), g_qk_k=pad_rope(g_qk_k), rope=rope_tab, w_g2=wg2,
        b_g2=b_g2.reshape(1, -1), g_out=g_gla_out.reshape(1, -1), w_o=w_o.astype(BF16),
        g_ffn=g_norm_ffn.reshape(1, d), w_r2=jnp.stack([wr_hi, wr_lo]), b_r=b_router.reshape(-1, 1))


def _mixer(x, mod, pos0, past_lat, past_kr, s0_pairs, wts):
    b, s, d = x.shape
    t = b * s
    if b > 1 and s & (s - 1) == 0 and min(t, TOKEN_TILE) % s == 0:
        per_tok = lambda j: jnp.broadcast_to(mod[:, j:j + 1], (b, s, d)).reshape(1, t, d)
        outs = _proj(x.reshape(1, t, d), per_tok(0), per_tok(1), pos0, wts, period=s)
        q = outs[0].reshape(HEADS, b, s, QK_PAD).transpose(1, 0, 2, 3)
        lat, kr, gq, gk, gv, gl, gr = [o.reshape(b, s, o.shape[-1]) for o in outs[1:]]
    else:
        q, lat, kr, gq, gk, gv, gl, gr = _proj(x, mod[:, 0:1], mod[:, 1:2], pos0, wts)
    kv_w = (wts["w_uk"], wts["w_uv_t"], wts["g_qk_k"])
    if past_lat is None:
        k_new, vt_new = _kv(lat, kr, *kv_w)
        o_mla = _attn_prompt(q, k_new, vt_new)
    else:
        o_mla = _attn_sample(q, past_lat, past_kr, lat, kr, *kv_w)
    o_gla, s_fin = _gla(gq, gk, gl, gv, gr, s0_pairs, wts["g_out"])
    if b == 1:
        rows = lambda j: mod[0, j:j + 1]
    else:
        rows = lambda j: jnp.broadcast_to(mod[:, j:j + 1], (b, s, d)).reshape(t, d)
    x2, h2, idx, wt, lrank, cnt = _post(x.reshape(t, d), o_mla.reshape(t, -1), o_gla.reshape(t, -1), rows(2), rows(4),
                                        rows(3), wts["w_o"], wts["g_ffn"], wts["w_r2"], wts["b_r"])
    return dict(x2=x2, h2=h2, idx=idx, wt=wt, lrank=lrank, cnt=cnt, gate_f=rows(5), lat=lat, kr=kr, s_fin=s_fin)


def kernel(x_prompt, x_sample, cache_mla_latent, cache_mla_krope, state_gla, c_prompt, c_sample, w_ada, b_ada, g_norm_mix, w_in, g_q_a, w_uq, g_kv_a, w_ukv, g_qk_q, g_qk_k, w_g2, b_g2, g_gla_out, w_o, g_norm_ffn, w_router, b_router, w_gu, b_gu, w_down, b_down):
    depth = w_ada.shape[0]
    assert depth == 1, "single-layer step"
    bp, sp, d = x_prompt.shape
    bs, ss, _ = x_sample.shape
    tp, tsm = bp * sp, bs * ss
    assert tp % ROUTE_TILE == 0 and tsm % ROUTE_TILE == 0, "token counts must be whole routing tiles"
    past = cache_mla_latent.shape[2]
    layer = lambda a: a.reshape(a.shape[1:])
    wts = _prep_weights(*[layer(a) for a in (g_norm_mix, w_in, g_q_a, w_uq, g_kv_a, w_ukv, g_qk_q, g_qk_k, w_g2, b_g2,
                                             g_gla_out, w_o, g_norm_ffn, w_router, b_router)])
    w_gu, b_gu, w_down, b_down = layer(w_gu), layer(b_gu), layer(w_down), layer(b_down)

    mod = _ada(jnp.concatenate([c_prompt, c_sample], axis=0), layer(w_ada), layer(b_ada)).reshape(bp + bs, 6, d)
    zero_state = jnp.zeros((bp, HEADS // 2, GLA_DV, LANES), F32)
    pr = _mixer(x_prompt, mod[:bp], 0, None, None, zero_state, wts)
    sa = _mixer(x_sample, mod[bp:], past, layer(cache_mla_latent), layer(cache_mla_krope),
                _state_to_pairs(layer(state_gla)), wts)

    idx = jnp.concatenate([pr["idx"], sa["idx"]], axis=1)
    lrank = jnp.concatenate([pr["lrank"], sa["lrank"]], axis=1)
    rt = _route_tables(idx, lrank, jnp.concatenate([pr["cnt"], sa["cnt"]], axis=0))
    ntp = tp // ROUTE_TILE
    lpos, table = rt["lpos"], rt["table"]
    xpad = _scatter(table, rt["tail"], lpos, pr["h2"], sa["h2"], rt["n_blocks"])
    ypad = _experts(rt["blk_e"], rt["n_valid"], rt["nxt_e"], rt["ord_e"], xpad, w_gu, b_gu, w_down, b_down)
    y_p = _combine(table[:ntp], lpos[:, :tp], pr["wt"], pr["x2"], pr["gate_f"], ypad).reshape(bp, sp, d)
    y_s = _combine(table[ntp:], lpos[:, tp:], sa["wt"], sa["x2"], sa["gate_f"], ypad).reshape(bs, ss, d)

    return (y_p, y_s,
            pr["lat"][None], pr["kr"][None], _state_from_pairs(pr["s_fin"])[None],
            sa["lat"][None], sa["kr"][None], _state_from_pairs(sa["s_fin"])[None])
```

```python
import functools

import numpy as np
import jax
import jax.numpy as jnp
from jax import lax
from jax.experimental import pallas as pl
from jax.experimental.pallas import tpu as pltpu

F32 = jnp.float32
BF16 = jnp.bfloat16
I32 = jnp.int32

CHUNK = 64
EPS = 1e-6
HEADS = 4
Q_LORA = 384
KV_LORA = 256
NOPE = 128
ROPE = 64
HALF = ROPE // 2
V_DIM = 128
QK = NOPE + ROPE
QK_PAD = 256
ROPE_THETA = 10000.0
GLA_DK = 64
GLA_DV = 128
GATE_RANK = 16
GATE_NORM = 16.0
N_EXPERTS = 32
TOP_K = 4
SWIGLU_LIMIT = 7.0
SWIGLU_ALPHA = 1.702
NEG = -1e30
LOG2_E = 1.4426950408889634

LANES = 128
SUBLANES = 8
BF16_ROWS = 16
TOKEN_TILE = 512
ADA_COLS = 1536
ROUTE_TILE = 512
RUN_CHUNK = BF16_ROWS
BIG_PIECE = 2 * RUN_CHUNK
SORT_ROWS = 512
CHUNK_UNROLL = 4
EXPERT_ROWS = 256
ATTN_TILE = 1024
MIB = 1024 * 1024
VMEM_BIG = 56 * MIB
VMEM_MID = 48 * MIB
VMEM_SMALL = 40 * MIB


def _cparams(sem, vmem=None):
    return pltpu.CompilerParams(dimension_semantics=sem, vmem_limit_bytes=vmem)


def _nt(a, b):
    return lax.dot_general(a, b, (((1,), (1,)), ((), ())), preferred_element_type=F32)


def _rms(x, width):
    return lax.rsqrt(jnp.sum(x * x, axis=-1, keepdims=True) * (1.0 / width) + EPS)


def _round_up(x, m):
    return ((x + m - 1) // m) * m


def _ada_body(c_ref, w_ref, b_ref, o_ref):
    c = c_ref[...]
    s = (c * jax.nn.sigmoid(c)).astype(BF16)
    o_ref[...] = jnp.dot(s, w_ref[...].astype(BF16), preferred_element_type=F32) + b_ref[...]


def _ada(c, w_ada, b_ada):
    r, d = c.shape
    n = w_ada.shape[1]
    tn = ADA_COLS if n % ADA_COLS == 0 else n
    return pl.pallas_call(
        _ada_body,
        grid=(n // tn,),
        in_specs=[pl.BlockSpec((r, d), lambda j: (0, 0)),
                  pl.BlockSpec((d, tn), lambda j: (0, j)),
                  pl.BlockSpec((1, tn), lambda j: (0, j))],
        out_specs=pl.BlockSpec((r, tn), lambda j: (0, j)),
        out_shape=jax.ShapeDtypeStruct((r, n), F32),
        compiler_params=_cparams(("arbitrary",), VMEM_SMALL),
        name="ada",
    )(c, w_ada, b_ada.reshape(1, n))


_SEG = dict(qa_kr=(0, 512), kva=(512, 768), gq=(768, 1024), gk=(1024, 1280),
            gv=(1280, 1792), gr=(1792, 2304), glr=(2304, 2432))
_W1_COLS = 2432
W1_PREP_COLS = 256


def _w1_source_columns():
    o_kva, o_kr = Q_LORA, Q_LORA + KV_LORA
    o_gq = o_kr + ROPE
    o_gk = o_gq + HEADS * GLA_DK
    o_gv = o_gk + HEADS * GLA_DK
    o_glr = o_gv + HEADS * GLA_DV
    o_gr = o_glr + GATE_RANK
    return [(0, Q_LORA), (o_kr, ROPE), (o_kr, ROPE), (o_kva, KV_LORA), (o_gq, HEADS * GLA_DK),
            (o_gk, HEADS * GLA_DK), (o_gv, HEADS * GLA_DV), (o_gr, HEADS * GLA_DV), (o_glr, GATE_RANK)]


def _w1_body(wt_ref, w1_ref):
    o = 0
    for start, width in _w1_source_columns():
        w1_ref[o:o + width, :] = wt_ref[start:start + width, :].astype(BF16)
        o += width
    w1_ref[o:, :] = jnp.zeros((_W1_COLS - o, w1_ref.shape[1]), BF16)


def _w1(w_in_t):
    n, d = w_in_t.shape
    return pl.pallas_call(
        _w1_body,
        grid=(d // W1_PREP_COLS,),
        in_specs=[pl.BlockSpec((n, W1_PREP_COLS), lambda i: (0, i))],
        out_specs=pl.BlockSpec((_W1_COLS, W1_PREP_COLS), lambda i: (0, i)),
        out_shape=jax.ShapeDtypeStruct((_W1_COLS, d), BF16),
        compiler_params=_cparams(("arbitrary",)),
        name="w1",
    )(w_in_t)


def _proj_body(pos0, ts, period, x_ref, sh_ref, sc_ref, gmix_ref, w1_ref, gqa_ref, wuq_ref, gkv_ref, gqk_ref,
               rope_ref, wg2_ref, bg2_ref,
               q_ref, lat_ref, kr_ref, gq_o, gk_o, gv_o, gl_o, gr_o, trig_scr):
    i = pl.program_id(1)
    x = x_ref[...]
    d = x.shape[-1]
    h = (x * _rms(x, d) * gmix_ref[...]) * (1.0 + sc_ref[...]) + sh_ref[...]
    hb = h.astype(BF16)

    def seg(name):
        a, b = _SEG[name]
        return _nt(hb, w1_ref[a:b, :])

    @pl.when((pl.program_id(0) == 0) & (i == 0))
    def _():
        row = lax.broadcasted_iota(I32, (ts, LANES), 0)
        if period is not None:
            row = row & (period - 1)
        row_ang = row.astype(F32) * rope_ref[0:1, :]
        trig_scr[0] = jnp.cos(row_ang)
        trig_scr[1] = jnp.sin(row_ang)

    tile_pos = pos0 + (i * ts if period is None else 0 * i)
    base_ang = jnp.broadcast_to(tile_pos.astype(F32) * rope_ref[0:1, :], (SUBLANES, LANES))
    cos_a, sin_a = jnp.cos(base_ang)[0:1, :], jnp.sin(base_ang)[0:1, :]
    cos = cos_a * trig_scr[0] - sin_a * trig_scr[1]
    sin = (sin_a * trig_scr[0] + cos_a * trig_scr[1]) * rope_ref[1:2, :]
    lane = lax.broadcasted_iota(I32, (ts, LANES), 1)
    first_half = (lane & HALF) == 0
    low64 = lane < ROPE

    def rope(v):
        partner = jnp.where(first_half, pltpu.roll(v, LANES - HALF, 1), pltpu.roll(v, HALF, 1))
        return v * cos + partner * sin

    qa_kr = seg("qa_kr")
    qa = qa_kr[:, 0:Q_LORA]
    qn = (qa * _rms(qa, Q_LORA) * gqa_ref[...]).astype(BF16)
    qf = jnp.dot(qn, wuq_ref[...], preferred_element_type=F32)
    rope_blocks = (rope(qf[:, 4 * NOPE:4 * NOPE + LANES]), rope(qf[:, 4 * NOPE + LANES:4 * NOPE + 2 * LANES]))
    for hd in range(HEADS):
        nope = qf[:, NOPE * hd:NOPE * (hd + 1)]
        blk = rope_blocks[hd // 2]
        if hd % 2:
            blk = pltpu.roll(blk, ROPE, 1)
        blk = jnp.where(low64, blk, 0.0)
        ss = jnp.sum(nope * nope, axis=-1, keepdims=True) + jnp.sum(blk * blk, axis=-1, keepdims=True)
        scl = lax.rsqrt(ss * (1.0 / QK) + EPS) * (QK ** -0.5 * LOG2_E)
        q_ref[hd, :, 0:NOPE] = (nope * scl * gqk_ref[0:1, :]).astype(BF16)
        q_ref[hd, :, NOPE:QK_PAD] = (blk * scl * gqk_ref[1:2, :]).astype(BF16)

    kva = seg("kva")
    lat_ref[...] = kva * _rms(kva, KV_LORA) * gkv_ref[...]
    kr_ref[...] = rope(qa_kr[:, Q_LORA:Q_LORA + LANES])[:, 0:ROPE]

    gq_o[...] = seg("gq") * (GLA_DK ** -0.5)
    gk_o[...] = seg("gk")
    gv_o[...] = seg("gv").astype(BF16)
    gr_o[...] = seg("gr")
    z = jnp.dot(seg("glr").astype(BF16), wg2_ref[...], preferred_element_type=F32) + bg2_ref[...]
    gl_o[...] = (jnp.minimum(z, 0.0) - jnp.log1p(jnp.exp(-jnp.abs(z)))) * (1.0 / GATE_NORM)


def _proj(x, shift, scale, pos0, wts, period=None):
    b, s, d = x.shape
    ts = min(s, TOKEN_TILE)
    assert period is None or (period & (period - 1) == 0 and ts % period == 0)
    row = lambda a: pl.BlockSpec(a.shape, lambda bi, i: (0,) * a.ndim)
    tok = lambda w: pl.BlockSpec((None, ts, w), lambda bi, i: (bi, i, 0))
    mod = tok(d) if shift.shape[1] == s and s > 1 else pl.BlockSpec((None, 1, d), lambda bi, i: (bi, 0, 0))
    small = [wts["g_mix"], wts["w1"], wts["g_qa"], wts["w_uq"], wts["g_kv"], wts["g_qk_q"], wts["rope"],
             wts["w_g2"], wts["b_g2"]]
    out_shape = (
        jax.ShapeDtypeStruct((b, HEADS, s, QK_PAD), BF16),
        jax.ShapeDtypeStruct((b, s, KV_LORA), F32),
        jax.ShapeDtypeStruct((b, s, ROPE), F32),
        jax.ShapeDtypeStruct((b, s, HEADS * GLA_DK), F32),
        jax.ShapeDtypeStruct((b, s, HEADS * GLA_DK), F32),
        jax.ShapeDtypeStruct((b, s, HEADS * GLA_DV), BF16),
        jax.ShapeDtypeStruct((b, s, HEADS * GLA_DK), F32),
        jax.ShapeDtypeStruct((b, s, HEADS * GLA_DV), F32),
    )
    out_specs = (
        pl.BlockSpec((None, HEADS, ts, QK_PAD), lambda bi, i: (bi, 0, i, 0)),
        tok(KV_LORA), tok(ROPE), tok(HEADS * GLA_DK), tok(HEADS * GLA_DK), tok(HEADS * GLA_DV),
        tok(HEADS * GLA_DK), tok(HEADS * GLA_DV),
    )
    return pl.pallas_call(
        functools.partial(_proj_body, pos0, ts, period),
        grid=(b, s // ts),
        in_specs=[tok(d), mod, mod] + [row(a) for a in small],
        out_specs=out_specs,
        out_shape=out_shape,
        scratch_shapes=[pltpu.VMEM((2, ts, LANES), F32)],
        compiler_params=_cparams(("arbitrary", "arbitrary"), VMEM_BIG),
        name="proj",
    )(x, shift, scale, *small)


def _key_rows(lat, kr, wk_ref, gk_ref, k_out):
    kn_all = jnp.dot(lat, wk_ref[...], preferred_element_type=F32)
    kr_ss = jnp.sum(kr * kr, axis=-1, keepdims=True)
    for hd in range(HEADS):
        kn = kn_all[:, NOPE * hd:NOPE * (hd + 1)]
        scl = lax.rsqrt((jnp.sum(kn * kn, axis=-1, keepdims=True) + kr_ss) * (1.0 / QK) + EPS)
        k_out[hd, :, 0:NOPE] = (kn * scl * gk_ref[0:1, :]).astype(BF16)
        k_out[hd, :, NOPE:QK] = (kr * scl * gk_ref[1:2, 0:ROPE]).astype(BF16)
        k_out[hd, :, QK:QK_PAD] = jnp.zeros((kr.shape[0], QK_PAD - QK), BF16)


def _kv_body(lat_ref, kr_ref, wk_ref, wv_ref, gk_ref, k_ref, v_ref):
    lat = lat_ref[...].astype(BF16)
    _key_rows(lat, kr_ref[...], wk_ref, gk_ref, k_ref)
    v_t = _nt(wv_ref[...], lat)
    for hd in range(HEADS):
        v_ref[hd] = v_t[V_DIM * hd:V_DIM * (hd + 1), :].astype(BF16)


def _kv(lat, kr, w_uk, w_uv_t, g_qk_k):
    b, s, _ = lat.shape
    ts = min(s, ATTN_TILE)
    return pl.pallas_call(
        _kv_body,
        grid=(b, s // ts),
        in_specs=[pl.BlockSpec((None, ts, KV_LORA), lambda bi, i: (bi, i, 0)),
                  pl.BlockSpec((None, ts, ROPE), lambda bi, i: (bi, i, 0)),
                  pl.BlockSpec(w_uk.shape, lambda bi, i: (0, 0)),
                  pl.BlockSpec(w_uv_t.shape, lambda bi, i: (0, 0)),
                  pl.BlockSpec(g_qk_k.shape, lambda bi, i: (0, 0))],
        out_specs=(pl.BlockSpec((None, HEADS, ts, QK_PAD), lambda bi, i: (bi, 0, i, 0)),
                   pl.BlockSpec((None, HEADS, None, V_DIM, ts), lambda bi, i: (bi, 0, i, 0, 0))),
        out_shape=(jax.ShapeDtypeStruct((b, HEADS, s, QK_PAD), BF16),
                   jax.ShapeDtypeStruct((b, HEADS, s // ts, V_DIM, ts), BF16)),
        compiler_params=_cparams(("arbitrary", "arbitrary")),
        name="kv",
    )(lat, kr, w_uk, w_uv_t, g_qk_k)


def _attn_prompt_body(t, q_ref, qn_ref, k_ref, vt_ref, o_ref, s_a, s_b):
    i = pl.program_id(2)

    def scores(q, j, buf):
        buf[...] = _nt(k_ref[pl.ds(pl.multiple_of(j * t, t), t), :], q)

    def consume(j, buf, carry, masked=False):
        m, l, acc = carry
        s = buf[...]
        if masked:
            visible = (lax.broadcasted_iota(I32, (t, t), 0) // CHUNK) <= (lax.broadcasted_iota(I32, (t, t), 1) // CHUNK)
            s = jnp.where(visible, s, NEG)
        m_new = jnp.maximum(m, jnp.max(s, axis=0, keepdims=True))
        alpha = jnp.exp2(m - m_new)
        p = jnp.exp2(s - m_new)
        l = alpha * l + jnp.sum(p, axis=0, keepdims=True)
        acc = alpha * acc + jnp.dot(vt_ref[j], p.astype(BF16), preferred_element_type=F32)
        return m_new, l, acc

    def run(first, second):
        q = q_ref[...]

        @pl.when(i == 0)
        def _():
            scores(q, 0, first)

        def pair(pp, carry):
            j = 2 * pp
            scores(q, j + 1, second)
            carry = consume(j, first, carry)
            scores(q, j + 2, first)
            return consume(j + 1, second, carry)

        def even_tail(carry):
            scores(qn_ref[...], 0, second)
            return consume(i, first, carry, masked=True)

        def odd_tail(carry):
            scores(q, i, second)
            carry = consume(i - 1, first, carry)
            scores(qn_ref[...], 0, first)
            return consume(i, second, carry, masked=True)

        carry = (jnp.full((1, t), NEG, F32), jnp.zeros((1, t), F32), jnp.zeros((V_DIM, t), F32))
        carry = lax.fori_loop(0, i // 2, pair, carry)
        _, l, acc = lax.cond(i % 2 == 1, odd_tail, even_tail, carry)
        o_ref[...] = (acc / l).T.astype(BF16)

    @pl.when(((i + 1) // 2) % 2 == 0)
    def _():
        run(s_a, s_b)

    @pl.when(((i + 1) // 2) % 2 == 1)
    def _():
        run(s_b, s_a)


def _attn_prompt(q, k, v_t):
    b, _, s, _ = q.shape
    t = v_t.shape[-1]
    nq = s // t
    return pl.pallas_call(
        functools.partial(_attn_prompt_body, t),
        grid=(b, HEADS, nq),
        in_specs=[pl.BlockSpec((None, None, t, QK_PAD), lambda bi, h, i: (bi, h, i, 0)),
                  pl.BlockSpec((None, None, t, QK_PAD), lambda bi, h, i: (bi, h, jnp.minimum(i + 1, nq - 1), 0)),
                  pl.BlockSpec((None, None, s, QK_PAD), lambda bi, h, i: (bi, h, 0, 0)),
                  pl.BlockSpec((None, None, nq, V_DIM, t), lambda bi, h, i: (bi, h, 0, 0, 0))],
        out_specs=pl.BlockSpec((None, t, V_DIM), lambda bi, h, i: (bi, i, h)),
        out_shape=jax.ShapeDtypeStruct((b, s, HEADS * V_DIM), BF16),
        scratch_shapes=[pltpu.VMEM((t, t), F32), pltpu.VMEM((t, t), F32)],
        compiler_params=_cparams(("arbitrary", "arbitrary", "arbitrary"), VMEM_BIG),
        name="attn_prompt",
    )(q, q, k, v_t)


def _attn_sample_body(past, sq, q_ref, plat_ref, pkr_ref, nlat_ref, nkr_ref, wk_ref, wv_ref, gk_ref, o_ref,
                      kp_scr, kn_scr):
    plat = plat_ref[...].astype(BF16)
    nlat = nlat_ref[...].astype(BF16)
    pkr_t = jnp.concatenate([pkr_ref[...], jnp.zeros((LANES - ROPE, past), F32)], axis=0)
    _key_rows(plat, pkr_t.T[:, 0:ROPE], wk_ref, gk_ref, kp_scr)
    _key_rows(nlat, nkr_ref[...], wk_ref, gk_ref, kn_scr)
    vp_t = _nt(wv_ref[...], plat).astype(BF16)
    vn_t = _nt(wv_ref[...], nlat).astype(BF16)
    hq = HEADS * sq
    col = lax.broadcasted_iota(I32, (sq, hq), 1)
    key_chunk = (past + lax.broadcasted_iota(I32, (sq, hq), 0)) // CHUNK
    qry_chunk = (past + col % sq) // CHUNK
    qrow = lax.broadcasted_iota(I32, (hq, 1), 0) // sq
    q_all = jnp.concatenate([q_ref[hd] for hd in range(HEADS)], axis=0)
    s_p = jnp.zeros((past, hq), F32)
    s_n = jnp.zeros((sq, hq), F32)
    for hd in range(HEADS):
        q_h = jnp.where(qrow == hd, q_all, jnp.zeros((), BF16))
        s_p = s_p + _nt(kp_scr[hd], q_h)
        s_n = s_n + _nt(kn_scr[hd], q_h)
    s_n = jnp.where(key_chunk <= qry_chunk, s_n, NEG)
    m = jnp.maximum(jnp.max(s_p, axis=0, keepdims=True), jnp.max(s_n, axis=0, keepdims=True))
    p_p = jnp.exp2(s_p - m)
    p_n = jnp.exp2(s_n - m)
    inv_l = 1.0 / (jnp.sum(p_p, axis=0, keepdims=True) + jnp.sum(p_n, axis=0, keepdims=True))
    p_p = p_p.astype(BF16)
    p_n = p_n.astype(BF16)
    lane_head = lax.broadcasted_iota(I32, (V_DIM, hq), 1) // sq
    o_t = jnp.zeros((V_DIM, hq), F32)
    for hd in range(HEADS):
        rows = slice(V_DIM * hd, V_DIM * (hd + 1))
        o_h = (jnp.dot(vp_t[rows, :], p_p, preferred_element_type=F32)
               + jnp.dot(vn_t[rows, :], p_n, preferred_element_type=F32))
        o_t = o_t + jnp.where(lane_head == hd, o_h, 0.0)
    o_all = (o_t * inv_l).T
    for hd in range(HEADS):
        o_ref[:, V_DIM * hd:V_DIM * (hd + 1)] = o_all[sq * hd:sq * (hd + 1), :].astype(BF16)


def _attn_sample(q, past_lat, past_kr, lat, kr, w_uk, w_uv_t, g_qk_k):
    b, _, sq, _ = q.shape
    past = past_lat.shape[1]
    rows = lambda n, w: pl.BlockSpec((None, n, w), lambda bi: (bi, 0, 0))
    full = lambda a: pl.BlockSpec(a.shape, lambda bi: (0,) * a.ndim)
    return pl.pallas_call(
        functools.partial(_attn_sample_body, past, sq),
        grid=(b,),
        in_specs=[pl.BlockSpec((None, HEADS, sq, QK_PAD), lambda bi: (bi, 0, 0, 0)),
                  rows(past, KV_LORA), rows(ROPE, past), rows(sq, KV_LORA), rows(sq, ROPE),
                  full(w_uk), full(w_uv_t), full(g_qk_k)],
        out_specs=rows(sq, HEADS * V_DIM),
        out_shape=jax.ShapeDtypeStruct((b, sq, HEADS * V_DIM), BF16),
        scratch_shapes=[pltpu.VMEM((HEADS, past, QK_PAD), BF16), pltpu.VMEM((HEADS, sq, QK_PAD), BF16)],
        compiler_params=_cparams(("arbitrary",), VMEM_SMALL),
        name="attn_sample",
    )(q, past_lat, jnp.swapaxes(past_kr, 1, 2), lat, kr, w_uk, w_uv_t, g_qk_k)


def _gla_masks(c, rows):
    idx = np.arange(rows)
    same = (idx // c)[:, None] == (idx // c)[None, :]
    le = same & (idx[None, :] <= idx[:, None])
    gt = same & (idx[None, :] > idx[:, None])
    return np.concatenate([le, gt], axis=0).astype(np.float32), int(np.log2(c))


def _level_exponents(b, g, c, level):
    n = c >> level
    rows = b.shape[0]
    row = lax.broadcasted_iota(I32, (rows, 1), 0)
    if n >= 8:
        split = b.reshape(rows // n, n, LANES)[:, n // 2 - 1:n // 2, :]
        split = jnp.broadcast_to(split, (rows // n, n, LANES)).reshape(rows, LANES)
        return jnp.where((row & (n // 2)) != 0, b - split, split - b)
    g_prev = pltpu.roll(g, 1, 0)
    g_next = pltpu.roll(g, rows - 1, 0)
    if n == 4:
        r = row & 3
        return jnp.where(r == 0, g_next, jnp.where(r == 1, 0.0, jnp.where(r == 2, g, g + g_prev)))
    assert n == 2
    return jnp.where((row & 1) != 0, g, 0.0)


def _gla_body(c, n_chunks, unit, levels, mall_ref, q_ref, k_ref, g_ref, v_ref, r_ref, s0_ref, gout_ref,
              o_ref, sfin_ref, st_scr):
    it = pl.program_id(1)

    @pl.when(it == 0)
    def _():
        st_scr[...] = s0_ref[...]

    ru = unit * c
    lane = lax.broadcasted_iota(I32, (ru, LANES), 1)
    head_lanes = (lane < GLA_DK, lane >= GLA_DK)
    st_lane_lo = lax.broadcasted_iota(I32, (GLA_DV, LANES), 1) < GLA_DK
    row = lax.broadcasted_iota(I32, (ru, 1), 0)
    ri = lax.broadcasted_iota(I32, (ru, ru), 0)
    ci = lax.broadcasted_iota(I32, (ru, ru), 1)
    mall = mall_ref[...]

    for un in range(n_chunks // unit):
        rows = slice(un * ru, (un + 1) * ru)
        for p in range(HEADS // 2):
            ls = slice(LANES * p, LANES * (p + 1))
            g = g_ref[rows, ls]
            q = q_ref[rows, ls]
            k = k_ref[rows, ls]
            g_hi = g.astype(BF16)
            g_lo = (g - g_hi.astype(F32)).astype(BF16)
            e2 = jnp.dot(mall, jnp.concatenate([g_hi, g_lo], axis=1), preferred_element_type=F32)
            e = e2[:, 0:LANES] + e2[:, LANES:2 * LANES]
            b = e[0:ru]
            eb = jnp.exp(b)
            qb = q * eb
            kd = (k * jnp.exp(e[ru:2 * ru])).astype(BF16)
            qs, ks = [q], [k.astype(BF16)]
            for l in range(levels):
                bottom = (row & (c >> (l + 1))) != 0
                decay = jnp.exp(_level_exponents(b, g, c, l))
                qs.append(jnp.where(bottom, q * decay, 0.0))
                ks.append(jnp.where(bottom, 0.0, k * decay).astype(BF16))
            states = [st_scr[p]]
            for j in range(unit):
                cr = slice(j * c, (j + 1) * c)
                upd = [lax.dot_general(v_ref[rows, GLA_DV * (2 * p + hh):GLA_DV * (2 * p + hh + 1)][cr, :], kd[cr, :],
                                       (((0,), (0,)), ((), ())), preferred_element_type=F32) for hh in range(2)]
                d_last = eb[j * c + c - 1:j * c + c, :]
                states.append(states[-1] * d_last + jnp.where(st_lane_lo, upd[0], upd[1]))
            st_scr[p] = states[-1]
            for hh in range(2):
                hd = 2 * p + hh
                sel = head_lanes[hh]
                a = jnp.where(ri == ci, _nt(jnp.where(sel, qs[0], 0.0).astype(BF16), ks[0]), 0.0)
                for l in range(levels):
                    pr = _nt(jnp.where(sel, qs[l + 1], 0.0).astype(BF16), ks[l + 1])
                    a = a + jnp.where((ri ^ ci) < (c >> l), pr, 0.0)
                vh = v_ref[rows, GLA_DV * hd:GLA_DV * (hd + 1)]
                qb_h = jnp.where(sel, qb, 0.0).astype(BF16)
                o_state = [_nt(qb_h[j * c:(j + 1) * c, :], states[j].astype(BF16)) for j in range(unit)]
                o = jnp.dot(a.astype(BF16), vh, preferred_element_type=F32) + jnp.concatenate(o_state, axis=0)
                on = o * _rms(o, GLA_DV) * gout_ref[...]
                r = r_ref[rows, GLA_DV * hd:GLA_DV * (hd + 1)]
                o_ref[rows, GLA_DV * hd:GLA_DV * (hd + 1)] = (on * (r * jax.nn.sigmoid(r))).astype(BF16)

    @pl.when(it == pl.num_programs(1) - 1)
    def _():
        sfin_ref[...] = st_scr[...]


def _gla(gq, gk, gl, gv, gr, s0, g_out):
    b, s, _ = gq.shape
    c = min(CHUNK, s)
    tile = min(s, 8 * c)
    unit = next(u for u in (4, 2, 1) if (tile // c) % u == 0)
    masks, levels = _gla_masks(c, unit * c)
    mall = jnp.asarray(masks, BF16)
    tok = lambda w: pl.BlockSpec((None, tile, w), lambda bi, i: (bi, i, 0))
    st_spec = pl.BlockSpec((None, HEADS // 2, GLA_DV, LANES), lambda bi, i: (bi, 0, 0, 0))
    return pl.pallas_call(
        functools.partial(_gla_body, c, tile // c, unit, levels),
        grid=(b, s // tile),
        in_specs=[pl.BlockSpec(mall.shape, lambda bi, i: (0, 0)),
                  tok(HEADS * GLA_DK), tok(HEADS * GLA_DK), tok(HEADS * GLA_DK), tok(HEADS * GLA_DV),
                  tok(HEADS * GLA_DV), st_spec, pl.BlockSpec(g_out.shape, lambda bi, i: (0, 0))],
        out_specs=(tok(HEADS * GLA_DV), st_spec),
        out_shape=(jax.ShapeDtypeStruct((b, s, HEADS * GLA_DV), BF16),
                   jax.ShapeDtypeStruct((b, HEADS // 2, GLA_DV, LANES), F32)),
        scratch_shapes=[pltpu.VMEM((HEADS // 2, GLA_DV, LANES), F32)],
        compiler_params=_cparams(("arbitrary", "arbitrary")),
        name="gla",
    )(mall, gq, gk, gl, gv, gr, s0, g_out)


def _state_to_pairs(s):
    b = s.shape[0]
    s = s.reshape(b, HEADS // 2, 2, GLA_DK, GLA_DV)
    return jnp.transpose(s, (0, 1, 4, 2, 3)).reshape(b, HEADS // 2, GLA_DV, 2 * GLA_DK)


def _state_from_pairs(s):
    b = s.shape[0]
    s = s.reshape(b, HEADS // 2, GLA_DV, 2, GLA_DK)
    return jnp.transpose(s, (0, 1, 3, 4, 2)).reshape(b, HEADS, GLA_DK, GLA_DV)


def _post_body(x_ref, om_ref, og_ref, gt_ref, sc_ref, sh_ref, wo_ref, gffn_ref, wr_ref, br_ref,
               x2_ref, h_ref, idx_ref, wt_ref, rank_ref, cnt_ref):
    half = om_ref.shape[-1]
    mix = (jnp.dot(om_ref[...], wo_ref[0:half, :], preferred_element_type=F32)
           + jnp.dot(og_ref[...], wo_ref[half:2 * half, :], preferred_element_type=F32))
    x2 = x_ref[...] + gt_ref[...] * mix
    x2_ref[...] = x2
    d = x2.shape[-1]
    h = (x2 * _rms(x2, d) * gffn_ref[...]) * (1.0 + sc_ref[...]) + sh_ref[...]
    h_hi = h.astype(BF16)
    h_ref[...] = h_hi
    h_lo = (h - h_hi.astype(F32)).astype(BF16)
    logits = _nt(wr_ref[0], h_hi) + _nt(wr_ref[0], h_lo) + _nt(wr_ref[1], h_hi) + br_ref[...]
    n_exp, tm = logits.shape
    eid = lax.broadcasted_iota(I32, (n_exp, tm), 0)
    vals, tops, ids = logits, [], []
    for _ in range(TOP_K):
        m = jnp.max(vals, axis=0, keepdims=True)
        sel = jnp.min(jnp.where(vals == m, eid, n_exp), axis=0, keepdims=True)
        tops.append(m)
        ids.append(sel)
        vals = jnp.where(eid == sel, -jnp.inf, vals)
    es = [jnp.exp(t - tops[0]) for t in tops]
    tot = es[0] + es[1] + es[2] + es[3]
    idx_ref[...] = jnp.concatenate(ids, axis=0)
    wt_ref[...] = jnp.concatenate([e / tot for e in es], axis=0)
    hits = [eid == sel for sel in ids]
    member = jnp.zeros((n_exp, tm), F32)
    for hk in hits:
        member = member + jnp.where(hk, 1.0, 0.0)
    before = lax.broadcasted_iota(I32, (tm, tm), 0) < lax.broadcasted_iota(I32, (tm, tm), 1)
    prefix = jnp.dot(member.astype(BF16), jnp.where(before, 1.0, 0.0).astype(BF16), preferred_element_type=F32)
    rank_ref[...] = jnp.concatenate(
        [jnp.sum(jnp.where(hk, prefix, 0.0), axis=0, keepdims=True) for hk in hits], axis=0).astype(I32)
    cnt_ref[...] = jnp.broadcast_to(jnp.sum(member, axis=1, keepdims=True), (n_exp, LANES)).astype(I32)


def _post(x, om, og, gate, scale, shift, w_o, g_ffn, w_r2, b_r):
    t, d = x.shape
    tm = ROUTE_TILE
    per_tok = gate.shape[0] == t
    mod = pl.BlockSpec((tm, d), lambda i: (i, 0)) if per_tok else pl.BlockSpec((1, d), lambda i: (0, 0))
    tok = lambda w: pl.BlockSpec((tm, w), lambda i: (i, 0))
    full = lambda a: pl.BlockSpec(a.shape, lambda i: (0,) * a.ndim)
    return pl.pallas_call(
        _post_body,
        grid=(t // tm,),
        in_specs=[tok(d), tok(om.shape[1]), tok(og.shape[1]), mod, mod, mod, full(w_o), full(g_ffn), full(w_r2),
                  full(b_r)],
        out_specs=(tok(d), tok(d),
                   pl.BlockSpec((TOP_K, tm), lambda i: (0, i)), pl.BlockSpec((TOP_K, tm), lambda i: (0, i)),
                   pl.BlockSpec((TOP_K, tm), lambda i: (0, i)),
                   pl.BlockSpec((None, N_EXPERTS, LANES), lambda i: (i, 0, 0))),
        out_shape=(jax.ShapeDtypeStruct((t, d), F32), jax.ShapeDtypeStruct((t, d), BF16),
                   jax.ShapeDtypeStruct((TOP_K, t), I32), jax.ShapeDtypeStruct((TOP_K, t), F32),
                   jax.ShapeDtypeStruct((TOP_K, t), I32), jax.ShapeDtypeStruct((t // tm, N_EXPERTS, LANES), I32)),
        compiler_params=_cparams(("arbitrary",), VMEM_SMALL),
        name="post",
    )(x, om, og, gate, scale, shift, w_o, g_ffn, w_r2, b_r)


def _max_tile_rows():
    return TOP_K * ROUTE_TILE + N_EXPERTS * (RUN_CHUNK - 1)


def _max_big_pieces():
    return _max_tile_rows() // BIG_PIECE


def _sort_capacity():
    return _round_up(_max_tile_rows(), SORT_ROWS)


class _Tab:
    def __init__(self):
        nb = _max_big_pieces()
        self.src_big, self.dst_big = 0, nb
        self.src_small, self.dst_small = 2 * nb, 2 * nb + N_EXPERTS
        self.n_big = 2 * nb + 2 * N_EXPERTS
        self.n_small, self.n_rows = self.n_big + 1, self.n_big + 2
        self.width = self.n_big + 3


def _route_tables(idx, lrank, cnt3):
    nt = cnt3.shape[0]
    t = idx.shape[1]
    cnt = cnt3[:, :, 0]
    run = _round_up(cnt, RUN_CHUNK)
    lo_end = jnp.cumsum(run, axis=1)
    lo = lo_end - run
    region = _round_up(jnp.sum(run, axis=0), EXPERT_ROWS)
    g_end = jnp.cumsum(region)
    run_dest = (g_end - region)[None, :] + jnp.cumsum(run, axis=0) - run
    n_ch = run // RUN_CHUNK
    big, small = n_ch // 2, n_ch % 2

    def piece_list(count, n_out):
        end = jnp.cumsum(count, axis=1)
        p = jnp.arange(n_out, dtype=I32)
        e = jnp.minimum(jnp.sum(end[:, None, :] <= p[None, :, None], axis=2), N_EXPERTS - 1)
        pick = e[:, :, None] == jnp.arange(N_EXPERTS, dtype=I32)[None, None, :]
        of_run = lambda a: jnp.sum(jnp.where(pick, a[:, None, :], 0), axis=2)
        return of_run, p[None, :] - of_run(end - count), end[:, -1]

    of_big, within_big, n_big = piece_list(big, _max_big_pieces())
    of_small, _, n_small = piece_list(small, N_EXPERTS)
    table = jnp.concatenate([
        of_big(lo) + BIG_PIECE * within_big, of_big(run_dest) + BIG_PIECE * within_big,
        of_small(lo + BIG_PIECE * big), of_small(run_dest + BIG_PIECE * big),
        n_big[:, None], n_small[:, None], lo_end[:, -1:]], axis=1).astype(I32)
    table = table.reshape(nt, 1, _Tab().width)
    eid =jnp.arange(N_EXPERTS, dtype=I32)[:, None]
    lo_tok = jnp.repeat(lo.T, ROUTE_TILE, axis=1)
    lpos = jnp.stack([jnp.sum(jnp.where(idx[k][None, :] == eid, lo_tok, 0), axis=0) for k in range(TOP_K)])
    lpos = (lpos + lrank).astype(I32)
    n_blocks = _round_up(t * TOP_K + nt * N_EXPERTS * (RUN_CHUNK - 1), EXPERT_ROWS) // EXPERT_ROWS + N_EXPERTS
    b_start = jnp.arange(n_blocks, dtype=I32) * EXPERT_ROWS
    blk_e = jnp.minimum(jnp.sum(g_end[None, :] <= b_start[:, None], axis=1), N_EXPERTS - 1).astype(I32)
    n_valid = (g_end[-1:] // EXPERT_ROWS).astype(I32)
    used = region > 0
    e_ids = jnp.arange(N_EXPERTS, dtype=I32)
    later_used = used[None, :] & (e_ids[None, :] > e_ids[:, None])
    nxt_e = jnp.min(jnp.where(later_used, e_ids[None, :], N_EXPERTS), axis=1)
    nxt_e = jnp.where(nxt_e < N_EXPERTS, nxt_e, -1).astype(I32)
    ord_e = (jnp.cumsum(used.astype(I32)) - 1).astype(I32)
    tail = jnp.concatenate([jnp.where(used, g_end - EXPERT_ROWS, -1), n_valid]).astype(I32)
    tail = tail.reshape(1, N_EXPERTS + 1)
    return dict(table=table, lpos=lpos, blk_e=blk_e, n_valid=n_valid, nxt_e=nxt_e, ord_e=ord_e, tail=tail,
                n_blocks=n_blocks)


def _chunk_copy(src, dst, sem):
    return pltpu.make_async_copy(src, dst, sem)


def _for_row_blocks(n_rows, body):
    full = n_rows // SORT_ROWS
    rem = n_rows - full * SORT_ROWS
    tail = pl.multiple_of(full * SORT_ROWS, SORT_ROWS)

    def whole(rb, carry):
        body(pl.multiple_of(rb * SORT_ROWS, SORT_ROWS), SORT_ROWS)
        return carry

    lax.fori_loop(0, full, whole, 0)

    @pl.when(rem > SORT_ROWS // 2)
    def _():
        body(tail, SORT_ROWS)

    @pl.when((rem > 0) & (rem <= SORT_ROWS // 2))
    def _():
        body(tail, SORT_ROWS // 2)


def _for_chunks(n, body):
    groups = n // CHUNK_UNROLL

    def group(g, carry):
        for u in range(CHUNK_UNROLL):
            body(g * CHUNK_UNROLL + u)
        return carry

    def single(c, carry):
        body(c)
        return carry

    lax.fori_loop(0, groups, group, 0)
    lax.fori_loop(groups * CHUNK_UNROLL, n, single, 0)


def _scatter_body(nt_a, n_blocks, tab_ref, prv_ref, tail_ref, lpos_ref, ha_ref, hb_ref, xout_ref, sorted_scr,
                  zero_scr, sems, zero_sem):
    i = pl.program_id(0)
    slot = i % 2
    tb = _Tab()
    tt = ha_ref.shape[0]

    @pl.when(i == 0)
    def _():
        zero_scr[...] = jnp.zeros_like(zero_scr)
        n_valid = tail_ref[0, N_EXPERTS]

        def block(start):
            return xout_ref.at[pl.ds(pl.multiple_of(start, EXPERT_ROWS), EXPERT_ROWS)]

        for e in range(N_EXPERTS):
            @pl.when(tail_ref[0, e] >= 0)
            def _():
                _chunk_copy(zero_scr, block(tail_ref[0, e]), zero_sem).start()

        def fill(b, carry):
            _chunk_copy(zero_scr, block(b * EXPERT_ROWS), zero_sem).start()
            return carry

        def fill_done(b, carry):
            _chunk_copy(zero_scr, block(0), zero_sem).wait()
            return carry

        lax.fori_loop(n_valid, n_blocks, fill, 0)
        for e in range(N_EXPERTS):
            @pl.when(tail_ref[0, e] >= 0)
            def _():
                _chunk_copy(zero_scr, block(0), zero_sem).wait()
        lax.fori_loop(n_valid, n_blocks, fill_done, 0)

    lp16 = lpos_ref[...].astype(jnp.int16)
    h = jnp.where(i < nt_a, ha_ref[...], hb_ref[...])

    def sort_block(r0, size):
        rid = (r0 + lax.broadcasted_iota(I32, (size, tt), 0)).astype(jnp.int16)
        onehot = jnp.zeros((size, tt), BF16)
        for k in range(TOP_K):
            onehot = onehot + jnp.where(lp16[k:k + 1, :] == rid, jnp.ones((), BF16), jnp.zeros((), BF16))
        sorted_scr[slot, pl.ds(r0, size), :] = jnp.dot(onehot, h, preferred_element_type=F32).astype(BF16)

    _for_row_blocks(tab_ref[0, tb.n_rows], sort_block)

    def piece(sl, src_row, dst_row, rows):
        src = sorted_scr.at[sl, pl.ds(pl.multiple_of(src_row, RUN_CHUNK), rows)]
        dst = xout_ref.at[pl.ds(pl.multiple_of(dst_row, RUN_CHUNK), rows)]
        return _chunk_copy(src, dst, sems.at[sl])

    def retire(tab, sl):
        _for_chunks(tab[0, tb.n_big], lambda p: piece(sl, 0, 0, BIG_PIECE).wait())
        _for_chunks(tab[0, tb.n_small], lambda p: piece(sl, 0, 0, RUN_CHUNK).wait())

    _for_chunks(tab_ref[0, tb.n_big], lambda p: piece(
        slot, tab_ref[0, tb.src_big + p], tab_ref[0, tb.dst_big + p], BIG_PIECE).start())
    _for_chunks(tab_ref[0, tb.n_small], lambda p: piece(
        slot, tab_ref[0, tb.src_small + p], tab_ref[0, tb.dst_small + p], RUN_CHUNK).start())

    @pl.when(i > 0)
    def _():
        retire(prv_ref, 1 - slot)

    @pl.when(i == pl.num_programs(0) - 1)
    def _():
        retire(tab_ref, slot)


def _scatter(table, tail, lpos, h_a, h_b, n_blocks):
    d = h_a.shape[1]
    nt_a, nt_b = h_a.shape[0] // ROUTE_TILE, h_b.shape[0] // ROUTE_TILE
    sort_cap = _sort_capacity()
    tab_spec = lambda f: pl.BlockSpec((None, 1, _Tab().width), lambda i: (f(i), 0, 0), memory_space=pltpu.SMEM)
    return pl.pallas_call(
        functools.partial(_scatter_body, nt_a, n_blocks),
        grid=(nt_a + nt_b,),
        in_specs=[tab_spec(lambda i: i), tab_spec(lambda i: jnp.maximum(i - 1, 0)),
                  pl.BlockSpec((1, N_EXPERTS + 1), lambda i: (0, 0), memory_space=pltpu.SMEM),
                  pl.BlockSpec((TOP_K, ROUTE_TILE), lambda i: (0, i)),
                  pl.BlockSpec((ROUTE_TILE, d), lambda i: (jnp.minimum(i, nt_a - 1), 0)),
                  pl.BlockSpec((ROUTE_TILE, d), lambda i: (jnp.maximum(i - nt_a, 0), 0))],
        out_specs=pl.BlockSpec(memory_space=pl.ANY),
        out_shape=jax.ShapeDtypeStruct((n_blocks * EXPERT_ROWS, d), BF16),
        scratch_shapes=[pltpu.VMEM((2, sort_cap, d), BF16), pltpu.VMEM((EXPERT_ROWS, d), BF16),
                        pltpu.SemaphoreType.DMA((2,)), pltpu.SemaphoreType.DMA(())],
        compiler_params=_cparams(("arbitrary",), VMEM_MID),
        name="scatter",
    )(table, table, tail, lpos, h_a, h_b)


def _experts_body(be_ref, nv_ref, nxt_ref, ord_ref, x_ref, wgu_hbm, bgu_ref, wd_hbm, bd_ref, y_ref,
                  wgu_f, wd_f, wgu_s, wd_s, sem_gu, sem_d):
    b = pl.program_id(0)
    e = be_ref[b]
    prev = be_ref[jnp.maximum(b - 1, 0)]
    valid = b < nv_ref[0]
    d_ff = wd_s.shape[0]
    slot = ord_ref[e] % 2

    def weights(expert, sl):
        return (pltpu.make_async_copy(wgu_hbm.at[expert], wgu_f.at[sl], sem_gu.at[sl]),
                pltpu.make_async_copy(wd_hbm.at[expert], wd_f.at[sl], sem_d.at[sl]))

    @pl.when(valid & ((b == 0) | (e != prev)))
    def _():
        @pl.when(b == 0)
        def _():
            for cp in weights(e, slot):
                cp.start()

        for cp in weights(e, slot):
            cp.wait()

        @pl.when(nxt_ref[e] >= 0)
        def _():
            for cp in weights(nxt_ref[e], 1 - slot):
                cp.start()

        wgu_s[...] = wgu_f[slot].astype(BF16)
        wd_s[...] = wd_f[slot].astype(BF16)

    @pl.when(valid)
    def _():
        gu = jnp.dot(x_ref[...], wgu_s[...], preferred_element_type=F32) + bgu_ref[...]
        gate = jnp.minimum(gu[:, 0:d_ff], SWIGLU_LIMIT)
        up = jnp.clip(gu[:, d_ff:2 * d_ff], -SWIGLU_LIMIT, SWIGLU_LIMIT)
        act = ((up + 1.0) * (gate * jax.nn.sigmoid(gate * SWIGLU_ALPHA))).astype(BF16)
        y_ref[...] = (jnp.dot(act, wd_s[...], preferred_element_type=F32) + bd_ref[...]).astype(BF16)

    @pl.when(jnp.logical_not(valid))
    def _():
        y_ref[...] = jnp.zeros_like(y_ref)


def _experts(blk_e, n_valid, nxt_e, ord_e, xpad, w_gu, b_gu, w_down, b_down):
    m, d = xpad.shape
    nb = m // EXPERT_ROWS
    n_exp, _, f2 = w_gu.shape
    d_ff = w_down.shape[1]
    last = lambda b, be, nv: jnp.minimum(b, nv[0] - 1)
    grid_spec = pltpu.PrefetchScalarGridSpec(
        num_scalar_prefetch=4,
        grid=(nb,),
        in_specs=[pl.BlockSpec((EXPERT_ROWS, d), lambda b, be, nv, nx, od: (last(b, be, nv), 0)),
                  pl.BlockSpec(memory_space=pl.ANY),
                  pl.BlockSpec((None, 1, f2), lambda b, be, nv, nx, od: (be[last(b, be, nv)], 0, 0)),
                  pl.BlockSpec(memory_space=pl.ANY),
                  pl.BlockSpec((None, 1, d), lambda b, be, nv, nx, od: (be[last(b, be, nv)], 0, 0))],
        out_specs=pl.BlockSpec((EXPERT_ROWS, d), lambda b, be, nv, nx, od: (b, 0)),
        scratch_shapes=[pltpu.VMEM((2, d, f2), F32), pltpu.VMEM((2, d_ff, d), F32),
                        pltpu.VMEM((d, f2), BF16), pltpu.VMEM((d_ff, d), BF16),
                        pltpu.SemaphoreType.DMA((2,)), pltpu.SemaphoreType.DMA((2,))],
    )
    return pl.pallas_call(
        _experts_body,
        grid_spec=grid_spec,
        out_shape=jax.ShapeDtypeStruct((m, d), BF16),
        compiler_params=_cparams(("arbitrary",), VMEM_BIG),
        name="experts",
    )(blk_e, n_valid, nxt_e, ord_e, xpad, w_gu, b_gu.reshape(n_exp, 1, f2), w_down, b_down.reshape(n_exp, 1, d))


def _combine_body(tab_ref, nxt_ref, lpos_ref, wt_ref, x2_ref, gt_ref, y_ref, o_ref, ysort_scr, sems):
    i = pl.program_id(0)
    slot = i % 2
    tb = _Tab()
    tt = x2_ref.shape[0]

    def piece(sl, sorted_row, expert_row, rows):
        src = y_ref.at[pl.ds(pl.multiple_of(expert_row, RUN_CHUNK), rows)]
        dst = ysort_scr.at[sl, pl.ds(pl.multiple_of(sorted_row, RUN_CHUNK), rows)]
        return _chunk_copy(src, dst, sems.at[sl])

    def fetch(tab, sl):
        _for_chunks(tab[0, tb.n_big], lambda p: piece(
            sl, tab[0, tb.src_big + p], tab[0, tb.dst_big + p], BIG_PIECE).start())
        _for_chunks(tab[0, tb.n_small], lambda p: piece(
            sl, tab[0, tb.src_small + p], tab[0, tb.dst_small + p], RUN_CHUNK).start())

    @pl.when(i == 0)
    def _():
        ysort_scr[...] = jnp.zeros_like(ysort_scr)
        fetch(tab_ref, 0)

    @pl.when(i + 1 < pl.num_programs(0))
    def _():
        fetch(nxt_ref, 1 - slot)

    _for_chunks(tab_ref[0, tb.n_big], lambda p: piece(slot, 0, 0, BIG_PIECE).wait())
    _for_chunks(tab_ref[0, tb.n_small], lambda p: piece(slot, 0, 0, RUN_CHUNK).wait())

    lp16 = lpos_ref[...].astype(jnp.int16)
    w = wt_ref[...].astype(BF16)
    o_ref[...] = x2_ref[...]
    gt = gt_ref[...]

    def gather_block(r0, size):
        rid = (r0 + lax.broadcasted_iota(I32, (size, tt), 0)).astype(jnp.int16)
        pw_t = jnp.zeros((size, tt), BF16)
        for k in range(TOP_K):
            pw_t = pw_t + jnp.where(lp16[k:k + 1, :] == rid, w[k:k + 1, :], jnp.zeros((), BF16))
        ys = ysort_scr[slot, pl.ds(r0, size), :]
        part = lax.dot_general(pw_t, ys, (((0,), (0,)), ((), ())), preferred_element_type=F32)
        o_ref[...] = o_ref[...] + gt * part

    _for_row_blocks(tab_ref[0, tb.n_rows], gather_block)


def _combine(table, lpos, wts, x2, gate, ypad):
    t, d = x2.shape
    nt = t // ROUTE_TILE
    per_tok = gate.shape[0] == t
    mod = pl.BlockSpec((ROUTE_TILE, d), lambda i: (i, 0)) if per_tok else pl.BlockSpec((1, d), lambda i: (0, 0))
    sort_cap = _sort_capacity()
    tab_spec = lambda f: pl.BlockSpec((None, 1, _Tab().width), lambda i: (f(i), 0, 0), memory_space=pltpu.SMEM)
    return pl.pallas_call(
        _combine_body,
        grid=(nt,),
        in_specs=[tab_spec(lambda i: i), tab_spec(lambda i: jnp.minimum(i + 1, nt - 1)),
                  pl.BlockSpec((TOP_K, ROUTE_TILE), lambda i: (0, i)),
                  pl.BlockSpec((TOP_K, ROUTE_TILE), lambda i: (0, i)),
                  pl.BlockSpec((ROUTE_TILE, d), lambda i: (i, 0)),
                  mod,
                  pl.BlockSpec(memory_space=pl.ANY)],
        out_specs=pl.BlockSpec((ROUTE_TILE, d), lambda i: (i, 0)),
        out_shape=jax.ShapeDtypeStruct((t, d), F32),
        scratch_shapes=[pltpu.VMEM((2, sort_cap, d), BF16), pltpu.SemaphoreType.DMA((2,))],
        compiler_params=_cparams(("arbitrary",), VMEM_MID),
        name="combine",
    )(table, table, lpos, wts, x2, gate, ypad)


def _prep_weights(g_norm_mix, w_in, g_q_a, w_uq, g_kv_a, w_ukv, g_qk_q, g_qk_k, w_g2, b_g2, g_gla_out, w_o,
                  g_norm_ffn, w_router, b_router):
    d = w_in.shape[0]
    w1 = _w1(w_in.T)
    wq = w_uq.reshape(Q_LORA, HEADS, QK)
    wuq = jnp.concatenate([wq[:, :, 0:NOPE].reshape(Q_LORA, HEADS * NOPE),
                           wq[:, :, NOPE:QK].reshape(Q_LORA, HEADS * ROPE)], axis=1).astype(BF16)
    wkv = w_ukv.reshape(KV_LORA, HEADS, NOPE + V_DIM)
    wuk = wkv[:, :, 0:NOPE].reshape(KV_LORA, HEADS * NOPE).astype(BF16)
    wuv_t = wkv[:, :, NOPE:].reshape(KV_LORA, HEADS * V_DIM).T.astype(BF16)
    pad_rope = lambda g: jnp.stack([g[0:NOPE], jnp.concatenate([g[NOPE:QK], jnp.zeros((QK_PAD - QK,), g.dtype)])])
    inv = ROPE_THETA ** (-jnp.arange(HALF, dtype=F32) / HALF)
    sign = jnp.concatenate([-jnp.ones((HALF,), F32), jnp.ones((HALF,), F32)])
    rope_tab = jnp.stack([jnp.tile(inv, LANES // HALF), jnp.tile(sign, LANES // ROPE)])
    wg2 = jnp.concatenate([w_g2, jnp.zeros((LANES - GATE_RANK, w_g2.shape[1]), w_g2.dtype)], axis=0).astype(BF16)
    wr_t = w_router.T
    wr_hi = wr_t.astype(BF16)
    wr_lo = (wr_t - wr_hi.astype(F32)).astype(BF16)
    return dict(
        g_mix=g_norm_mix.reshape(1, d), w1=w1, g_qa=g_q_a.reshape(1, -1), w_uq=wuq, g_kv=g_kv_a.reshape(1, -1),
        w_uk=wuk, w_uv_t=wuv_t, g_qk_q=pad_rope(g_qk_q), g_qk_k=pad_rope(g_qk_k), rope=rope_tab, w_g2=wg2,
        b_g2=b_g2.reshape(1, -1), g_out=g_gla_out.reshape(1, -1), w_o=w_o.astype(BF16),
        g_ffn=g_norm_ffn.reshape(1, d), w_r2=jnp.stack([wr_hi, wr_lo]), b_r=b_router.reshape(-1, 1))


def _mixer(x, mod, pos0, past_lat, past_kr, s0_pairs, wts):
    b, s, d = x.shape
    t = b * s
    if b > 1 and s & (s - 1) == 0 and min(t, TOKEN_TILE) % s == 0:
        per_tok = lambda j: jnp.broadcast_to(mod[:, j:j + 1], (b, s, d)).reshape(1, t, d)
        outs = _proj(x.reshape(1, t, d), per_tok(0), per_tok(1), pos0, wts, period=s)
        q = outs[0].reshape(HEADS, b, s, QK_PAD).transpose(1, 0, 2, 3)
        lat, kr, gq, gk, gv, gl, gr = [o.reshape(b, s, o.shape[-1]) for o in outs[1:]]
    else:
        q, lat, kr, gq, gk, gv, gl, gr = _proj(x, mod[:, 0:1], mod[:, 1:2], pos0, wts)
    kv_w = (wts["w_uk"], wts["w_uv_t"], wts["g_qk_k"])
    if past_lat is None:
        k_new, vt_new = _kv(lat, kr, *kv_w)
        o_mla = _attn_prompt(q, k_new, vt_new)
    else:
        o_mla = _attn_sample(q, past_lat, past_kr, lat, kr, *kv_w)
    o_gla, s_fin = _gla(gq, gk, gl, gv, gr, s0_pairs, wts["g_out"])
    if b == 1:
        rows = lambda j: mod[0, j:j + 1]
    else:
        rows = lambda j: jnp.broadcast_to(mod[:, j:j + 1], (b, s, d)).reshape(t, d)
    x2, h2, idx, wt, lrank, cnt = _post(x.reshape(t, d), o_mla.reshape(t, -1), o_gla.reshape(t, -1), rows(2), rows(4),
                                        rows(3), wts["w_o"], wts["g_ffn"], wts["w_r2"], wts["b_r"])
    return dict(x2=x2, h2=h2, idx=idx, wt=wt, lrank=lrank, cnt=cnt, gate_f=rows(5), lat=lat, kr=kr, s_fin=s_fin)


def kernel(x_prompt, x_sample, cache_mla_latent, cache_mla_krope, state_gla, c_prompt, c_sample, w_ada, b_ada, g_norm_mix, w_in, g_q_a, w_uq, g_kv_a, w_ukv, g_qk_q, g_qk_k, w_g2, b_g2, g_gla_out, w_o, g_norm_ffn, w_router, b_router, w_gu, b_gu, w_down, b_down):
    depth = w_ada.shape[0]
    assert depth == 1, "single-layer step"
    bp, sp, d = x_prompt.shape
    bs, ss, _ = x_sample.shape
    tp, tsm = bp * sp, bs * ss
    assert tp % ROUTE_TILE == 0 and tsm % ROUTE_TILE == 0, "token counts must be whole routing tiles"
    past = cache_mla_latent.shape[2]
    layer = lambda a: a.reshape(a.shape[1:])
    wts = _prep_weights(*[layer(a) for a in (g_norm_mix, w_in, g_q_a, w_uq, g_kv_a, w_ukv, g_qk_q, g_qk_k, w_g2, b_g2,
                                             g_gla_out, w_o, g_norm_ffn, w_router, b_router)])
    w_gu, b_gu, w_down, b_down = layer(w_gu), layer(b_gu), layer(w_down), layer(b_down)

    mod = _ada(jnp.concatenate([c_prompt, c_sample], axis=0), layer(w_ada), layer(b_ada)).reshape(bp + bs, 6, d)
    zero_state = jnp.zeros((bp, HEADS // 2, GLA_DV, LANES), F32)
    pr = _mixer(x_prompt, mod[:bp], 0, None, None, zero_state, wts)
    sa = _mixer(x_sample, mod[bp:], past, layer(cache_mla_latent), layer(cache_mla_krope),
                _state_to_pairs(layer(state_gla)), wts)

    idx = jnp.concatenate([pr["idx"], sa["idx"]], axis=1)
    lrank = jnp.concatenate([pr["lrank"], sa["lrank"]], axis=1)
    rt = _route_tables(idx, lrank, jnp.concatenate([pr["cnt"], sa["cnt"]], axis=0))
    ntp = tp // ROUTE_TILE
    lpos, table = rt["lpos"], rt["table"]
    xpad = _scatter(table, rt["tail"], lpos, pr["h2"], sa["h2"], rt["n_blocks"])
    ypad = _experts(rt["blk_e"], rt["n_valid"], rt["nxt_e"], rt["ord_e"], xpad, w_gu, b_gu, w_down, b_down)
    y_p = _combine(table[:ntp], lpos[:, :tp], pr["wt"], pr["x2"], pr["gate_f"], ypad).reshape(bp, sp, d)
    y_s = _combine(table[ntp:], lpos[:, tp:], sa["wt"], sa["x2"], sa["gate_f"], ypad).reshape(bs, ss, d)

    return (y_p, y_s,
            pr["lat"][None], pr["kr"][None], _state_from_pairs(pr["s_fin"])[None],
            sa["lat"][None], sa["kr"][None], _state_from_pairs(sa["s_fin"])[None])
```

```python
import functools

import numpy as np
import jax
import jax.numpy as jnp
from jax import lax
from jax.experimental import pallas as pl
from jax.experimental.pallas import tpu as pltpu

F32 = jnp.float32
BF16 = jnp.bfloat16
I32 = jnp.int32

CHUNK = 64
EPS = 1e-6
HEADS = 4
Q_LORA = 384
KV_LORA = 256
NOPE = 128
ROPE = 64
HALF = ROPE // 2
V_DIM = 128
QK = NOPE + ROPE
QK_PAD = 256
ROPE_THETA = 10000.0
GLA_DK = 64
GLA_DV = 128
GATE_RANK = 16
GATE_NORM = 16.0
N_EXPERTS = 32
TOP_K = 4
SWIGLU_LIMIT = 7.0
SWIGLU_ALPHA = 1.702
NEG = -1e30
LOG2_E = 1.4426950408889634

LANES = 128
SUBLANES = 8
BF16_ROWS = 16
TOKEN_TILE = 512
ADA_COLS = 1536
ROUTE_TILE = 512
RUN_CHUNK = BF16_ROWS
BIG_PIECE = 2 * RUN_CHUNK
SORT_ROWS = 512
CHUNK_UNROLL = 4
DMA_QUEUES = 2
EXPERT_ROWS = 256
ATTN_TILE = 1024
MIB = 1024 * 1024
VMEM_BIG = 56 * MIB
VMEM_MID = 48 * MIB
VMEM_SMALL = 40 * MIB


def _cparams(sem, vmem=None):
    return pltpu.CompilerParams(dimension_semantics=sem, vmem_limit_bytes=vmem)


def _nt(a, b):
    return lax.dot_general(a, b, (((1,), (1,)), ((), ())), preferred_element_type=F32)


def _rms(x, width):
    return lax.rsqrt(jnp.sum(x * x, axis=-1, keepdims=True) * (1.0 / width) + EPS)


def _round_up(x, m):
    return ((x + m - 1) // m) * m


def _ada_body(c_ref, w_ref, b_ref, o_ref):
    c = c_ref[...]
    s = (c * jax.nn.sigmoid(c)).astype(BF16)
    o_ref[...] = jnp.dot(s, w_ref[...].astype(BF16), preferred_element_type=F32) + b_ref[...]


def _ada(c, w_ada, b_ada):
    r, d = c.shape
    n = w_ada.shape[1]
    tn = ADA_COLS if n % ADA_COLS == 0 else n
    return pl.pallas_call(
        _ada_body,
        grid=(n // tn,),
        in_specs=[pl.BlockSpec((r, d), lambda j: (0, 0)),
                  pl.BlockSpec((d, tn), lambda j: (0, j)),
                  pl.BlockSpec((1, tn), lambda j: (0, j))],
        out_specs=pl.BlockSpec((r, tn), lambda j: (0, j)),
        out_shape=jax.ShapeDtypeStruct((r, n), F32),
        compiler_params=_cparams(("arbitrary",), VMEM_SMALL),
        name="ada",
    )(c, w_ada, b_ada.reshape(1, n))


_SEG = dict(qa_kr=(0, 512), kva=(512, 768), gq=(768, 1024), gk=(1024, 1280),
            gv=(1280, 1792), gr=(1792, 2304), glr=(2304, 2432))
_W1_COLS = 2432
W1_PREP_COLS = 256


def _w1_source_columns():
    o_kva, o_kr = Q_LORA, Q_LORA + KV_LORA
    o_gq = o_kr + ROPE
    o_gk = o_gq + HEADS * GLA_DK
    o_gv = o_gk + HEADS * GLA_DK
    o_glr = o_gv + HEADS * GLA_DV
    o_gr = o_glr + GATE_RANK
    return [(0, Q_LORA), (o_kr, ROPE), (o_kr, ROPE), (o_kva, KV_LORA), (o_gq, HEADS * GLA_DK),
            (o_gk, HEADS * GLA_DK), (o_gv, HEADS * GLA_DV), (o_gr, HEADS * GLA_DV), (o_glr, GATE_RANK)]


def _w1_body(wt_ref, w1_ref):
    o = 0
    for start, width in _w1_source_columns():
        w1_ref[o:o + width, :] = wt_ref[start:start + width, :].astype(BF16)
        o += width
    w1_ref[o:, :] = jnp.zeros((_W1_COLS - o, w1_ref.shape[1]), BF16)


def _w1(w_in_t):
    n, d = w_in_t.shape
    return pl.pallas_call(
        _w1_body,
        grid=(d // W1_PREP_COLS,),
        in_specs=[pl.BlockSpec((n, W1_PREP_COLS), lambda i: (0, i))],
        out_specs=pl.BlockSpec((_W1_COLS, W1_PREP_COLS), lambda i: (0, i)),
        out_shape=jax.ShapeDtypeStruct((_W1_COLS, d), BF16),
        compiler_params=_cparams(("arbitrary",)),
        name="w1",
    )(w_in_t)


def _proj_body(pos0, ts, period, x_ref, sh_ref, sc_ref, gmix_ref, w1_ref, gqa_ref, wuq_ref, gkv_ref, gqk_ref,
               rope_ref, wg2_ref, bg2_ref,
               q_ref, lat_ref, kr_ref, gq_o, gk_o, gv_o, gl_o, gr_o, trig_scr):
    i = pl.program_id(1)
    x = x_ref[...]
    d = x.shape[-1]
    h = (x * _rms(x, d) * gmix_ref[...]) * (1.0 + sc_ref[...]) + sh_ref[...]
    hb = h.astype(BF16)

    def seg(name):
        a, b = _SEG[name]
        return _nt(hb, w1_ref[a:b, :])

    @pl.when((pl.program_id(0) == 0) & (i == 0))
    def _():
        row = lax.broadcasted_iota(I32, (ts, LANES), 0)
        if period is not None:
            row = row & (period - 1)
        row_ang = row.astype(F32) * rope_ref[0:1, :]
        trig_scr[0] = jnp.cos(row_ang)
        trig_scr[1] = jnp.sin(row_ang)

    tile_pos = pos0 + (i * ts if period is None else 0 * i)
    base_ang = jnp.broadcast_to(tile_pos.astype(F32) * rope_ref[0:1, :], (SUBLANES, LANES))
    cos_a, sin_a = jnp.cos(base_ang)[0:1, :], jnp.sin(base_ang)[0:1, :]
    cos = cos_a * trig_scr[0] - sin_a * trig_scr[1]
    sin = (sin_a * trig_scr[0] + cos_a * trig_scr[1]) * rope_ref[1:2, :]
    lane = lax.broadcasted_iota(I32, (ts, LANES), 1)
    first_half = (lane & HALF) == 0
    low64 = lane < ROPE

    def rope(v):
        partner = jnp.where(first_half, pltpu.roll(v, LANES - HALF, 1), pltpu.roll(v, HALF, 1))
        return v * cos + partner * sin

    qa_kr = seg("qa_kr")
    qa = qa_kr[:, 0:Q_LORA]
    qn = (qa * _rms(qa, Q_LORA) * gqa_ref[...]).astype(BF16)
    qf = jnp.dot(qn, wuq_ref[...], preferred_element_type=F32)
    rope_blocks = (rope(qf[:, 4 * NOPE:4 * NOPE + LANES]), rope(qf[:, 4 * NOPE + LANES:4 * NOPE + 2 * LANES]))
    for hd in range(HEADS):
        nope = qf[:, NOPE * hd:NOPE * (hd + 1)]
        blk = rope_blocks[hd // 2]
        if hd % 2:
            blk = pltpu.roll(blk, ROPE, 1)
        blk = jnp.where(low64, blk, 0.0)
        ss = jnp.sum(nope * nope, axis=-1, keepdims=True) + jnp.sum(blk * blk, axis=-1, keepdims=True)
        scl = lax.rsqrt(ss * (1.0 / QK) + EPS) * (QK ** -0.5 * LOG2_E)
        q_ref[hd, :, 0:NOPE] = (nope * scl * gqk_ref[0:1, :]).astype(BF16)
        q_ref[hd, :, NOPE:QK_PAD] = (blk * scl * gqk_ref[1:2, :]).astype(BF16)

    kva = seg("kva")
    lat_ref[...] = kva * _rms(kva, KV_LORA) * gkv_ref[...]
    kr_ref[...] = rope(qa_kr[:, Q_LORA:Q_LORA + LANES])[:, 0:ROPE]

    gq_o[...] = seg("gq") * (GLA_DK ** -0.5)
    gk_o[...] = seg("gk")
    gv_o[...] = seg("gv").astype(BF16)
    gr_o[...] = seg("gr")
    z = jnp.dot(seg("glr").astype(BF16), wg2_ref[...], preferred_element_type=F32) + bg2_ref[...]
    gl_o[...] = (jnp.minimum(z, 0.0) - jnp.log1p(jnp.exp(-jnp.abs(z)))) * (1.0 / GATE_NORM)


def _proj(x, shift, scale, pos0, wts, period=None):
    b, s, d = x.shape
    ts = min(s, TOKEN_TILE)
    assert period is None or (period & (period - 1) == 0 and ts % period == 0)
    row = lambda a: pl.BlockSpec(a.shape, lambda bi, i: (0,) * a.ndim)
    tok = lambda w: pl.BlockSpec((None, ts, w), lambda bi, i: (bi, i, 0))
    mod = tok(d) if shift.shape[1] == s and s > 1 else pl.BlockSpec((None, 1, d), lambda bi, i: (bi, 0, 0))
    small = [wts["g_mix"], wts["w1"], wts["g_qa"], wts["w_uq"], wts["g_kv"], wts["g_qk_q"], wts["rope"],
             wts["w_g2"], wts["b_g2"]]
    out_shape = (
        jax.ShapeDtypeStruct((b, HEADS, s, QK_PAD), BF16),
        jax.ShapeDtypeStruct((b, s, KV_LORA), F32),
        jax.ShapeDtypeStruct((b, s, ROPE), F32),
        jax.ShapeDtypeStruct((b, s, HEADS * GLA_DK), F32),
        jax.ShapeDtypeStruct((b, s, HEADS * GLA_DK), F32),
        jax.ShapeDtypeStruct((b, s, HEADS * GLA_DV), BF16),
        jax.ShapeDtypeStruct((b, s, HEADS * GLA_DK), F32),
        jax.ShapeDtypeStruct((b, s, HEADS * GLA_DV), F32),
    )
    out_specs = (
        pl.BlockSpec((None, HEADS, ts, QK_PAD), lambda bi, i: (bi, 0, i, 0)),
        tok(KV_LORA), tok(ROPE), tok(HEADS * GLA_DK), tok(HEADS * GLA_DK), tok(HEADS * GLA_DV),
        tok(HEADS * GLA_DK), tok(HEADS * GLA_DV),
    )
    return pl.pallas_call(
        functools.partial(_proj_body, pos0, ts, period),
        grid=(b, s // ts),
        in_specs=[tok(d), mod, mod] + [row(a) for a in small],
        out_specs=out_specs,
        out_shape=out_shape,
        scratch_shapes=[pltpu.VMEM((2, ts, LANES), F32)],
        compiler_params=_cparams(("arbitrary", "arbitrary"), VMEM_BIG),
        name="proj",
    )(x, shift, scale, *small)


def _key_rows(lat, kr, wk_ref, gk_ref, k_out):
    kn_all = jnp.dot(lat, wk_ref[...], preferred_element_type=F32)
    kr_ss = jnp.sum(kr * kr, axis=-1, keepdims=True)
    for hd in range(HEADS):
        kn = kn_all[:, NOPE * hd:NOPE * (hd + 1)]
        scl = lax.rsqrt((jnp.sum(kn * kn, axis=-1, keepdims=True) + kr_ss) * (1.0 / QK) + EPS)
        k_out[hd, :, 0:NOPE] = (kn * scl * gk_ref[0:1, :]).astype(BF16)
        k_out[hd, :, NOPE:QK] = (kr * scl * gk_ref[1:2, 0:ROPE]).astype(BF16)
        k_out[hd, :, QK:QK_PAD] = jnp.zeros((kr.shape[0], QK_PAD - QK), BF16)


def _kv_body(lat_ref, kr_ref, wk_ref, wv_ref, gk_ref, k_ref, v_ref):
    lat = lat_ref[...].astype(BF16)
    _key_rows(lat, kr_ref[...], wk_ref, gk_ref, k_ref)
    v_t = _nt(wv_ref[...], lat)
    for hd in range(HEADS):
        v_ref[hd] = v_t[V_DIM * hd:V_DIM * (hd + 1), :].astype(BF16)


def _kv(lat, kr, w_uk, w_uv_t, g_qk_k):
    b, s, _ = lat.shape
    ts = min(s, ATTN_TILE)
    return pl.pallas_call(
        _kv_body,
        grid=(b, s // ts),
        in_specs=[pl.BlockSpec((None, ts, KV_LORA), lambda bi, i: (bi, i, 0)),
                  pl.BlockSpec((None, ts, ROPE), lambda bi, i: (bi, i, 0)),
                  pl.BlockSpec(w_uk.shape, lambda bi, i: (0, 0)),
                  pl.BlockSpec(w_uv_t.shape, lambda bi, i: (0, 0)),
                  pl.BlockSpec(g_qk_k.shape, lambda bi, i: (0, 0))],
        out_specs=(pl.BlockSpec((None, HEADS, ts, QK_PAD), lambda bi, i: (bi, 0, i, 0)),
                   pl.BlockSpec((None, HEADS, None, V_DIM, ts), lambda bi, i: (bi, 0, i, 0, 0))),
        out_shape=(jax.ShapeDtypeStruct((b, HEADS, s, QK_PAD), BF16),
                   jax.ShapeDtypeStruct((b, HEADS, s // ts, V_DIM, ts), BF16)),
        compiler_params=_cparams(("arbitrary", "arbitrary")),
        name="kv",
    )(lat, kr, w_uk, w_uv_t, g_qk_k)


def _attn_prompt_body(t, q_ref, qn_ref, k_ref, vt_ref, o_ref, s_a, s_b):
    i = pl.program_id(2)

    def scores(q, j, buf):
        buf[...] = _nt(k_ref[pl.ds(pl.multiple_of(j * t, t), t), :], q)

    def consume(j, buf, carry, masked=False):
        m, l, acc = carry
        s = buf[...]
        if masked:
            visible = (lax.broadcasted_iota(I32, (t, t), 0) // CHUNK) <= (lax.broadcasted_iota(I32, (t, t), 1) // CHUNK)
            s = jnp.where(visible, s, NEG)
        m_new = jnp.maximum(m, jnp.max(s, axis=0, keepdims=True))
        alpha = jnp.exp2(m - m_new)
        p = jnp.exp2(s - m_new)
        l = alpha * l + jnp.sum(p, axis=0, keepdims=True)
        acc = alpha * acc + jnp.dot(vt_ref[j], p.astype(BF16), preferred_element_type=F32)
        return m_new, l, acc

    def run(first, second):
        q = q_ref[...]

        @pl.when(i == 0)
        def _():
            scores(q, 0, first)

        def pair(pp, carry):
            j = 2 * pp
            scores(q, j + 1, second)
            carry = consume(j, first, carry)
            scores(q, j + 2, first)
            return consume(j + 1, second, carry)

        def even_tail(carry):
            scores(qn_ref[...], 0, second)
            return consume(i, first, carry, masked=True)

        def odd_tail(carry):
            scores(q, i, second)
            carry = consume(i - 1, first, carry)
            scores(qn_ref[...], 0, first)
            return consume(i, second, carry, masked=True)

        carry = (jnp.full((1, t), NEG, F32), jnp.zeros((1, t), F32), jnp.zeros((V_DIM, t), F32))
        carry = lax.fori_loop(0, i // 2, pair, carry)
        _, l, acc = lax.cond(i % 2 == 1, odd_tail, even_tail, carry)
        o_ref[...] = (acc / l).T.astype(BF16)

    @pl.when(((i + 1) // 2) % 2 == 0)
    def _():
        run(s_a, s_b)

    @pl.when(((i + 1) // 2) % 2 == 1)
    def _():
        run(s_b, s_a)


def _attn_prompt(q, k, v_t):
    b, _, s, _ = q.shape
    t = v_t.shape[-1]
    nq = s // t
    return pl.pallas_call(
        functools.partial(_attn_prompt_body, t),
        grid=(b, HEADS, nq),
        in_specs=[pl.BlockSpec((None, None, t, QK_PAD), lambda bi, h, i: (bi, h, i, 0)),
                  pl.BlockSpec((None, None, t, QK_PAD), lambda bi, h, i: (bi, h, jnp.minimum(i + 1, nq - 1), 0)),
                  pl.BlockSpec((None, None, s, QK_PAD), lambda bi, h, i: (bi, h, 0, 0)),
                  pl.BlockSpec((None, None, nq, V_DIM, t), lambda bi, h, i: (bi, h, 0, 0, 0))],
        out_specs=pl.BlockSpec((None, t, V_DIM), lambda bi, h, i: (bi, i, h)),
        out_shape=jax.ShapeDtypeStruct((b, s, HEADS * V_DIM), BF16),
        scratch_shapes=[pltpu.VMEM((t, t), F32), pltpu.VMEM((t, t), F32)],
        compiler_params=_cparams(("arbitrary", "arbitrary", "arbitrary"), VMEM_BIG),
        name="attn_prompt",
    )(q, q, k, v_t)


def _attn_sample_body(past, sq, q_ref, plat_ref, pkr_ref, nlat_ref, nkr_ref, wk_ref, wv_ref, gk_ref, o_ref,
                      kp_scr, kn_scr):
    plat = plat_ref[...].astype(BF16)
    nlat = nlat_ref[...].astype(BF16)
    _key_rows(plat, pkr_ref[...], wk_ref, gk_ref, kp_scr)
    _key_rows(nlat, nkr_ref[...], wk_ref, gk_ref, kn_scr)
    vp_t = _nt(wv_ref[...], plat).astype(BF16)
    vn_t = _nt(wv_ref[...], nlat).astype(BF16)
    hq = HEADS * sq
    col = lax.broadcasted_iota(I32, (sq, hq), 1)
    key_chunk = (past + lax.broadcasted_iota(I32, (sq, hq), 0)) // CHUNK
    qry_chunk = (past + col % sq) // CHUNK
    qrow = lax.broadcasted_iota(I32, (hq, 1), 0) // sq
    q_all = jnp.concatenate([q_ref[hd] for hd in range(HEADS)], axis=0)
    s_p = jnp.zeros((past, hq), F32)
    s_n = jnp.zeros((sq, hq), F32)
    for hd in range(HEADS):
        q_h = jnp.where(qrow == hd, q_all, jnp.zeros((), BF16))
        s_p = s_p + _nt(kp_scr[hd], q_h)
        s_n = s_n + _nt(kn_scr[hd], q_h)
    s_n = jnp.where(key_chunk <= qry_chunk, s_n, NEG)
    m = jnp.maximum(jnp.max(s_p, axis=0, keepdims=True), jnp.max(s_n, axis=0, keepdims=True))
    p_p = jnp.exp2(s_p - m)
    p_n = jnp.exp2(s_n - m)
    inv_l = 1.0 / (jnp.sum(p_p, axis=0, keepdims=True) + jnp.sum(p_n, axis=0, keepdims=True))
    p_p = p_p.astype(BF16)
    p_n = p_n.astype(BF16)
    lane_head = lax.broadcasted_iota(I32, (V_DIM, hq), 1) // sq
    o_t = jnp.zeros((V_DIM, hq), F32)
    for hd in range(HEADS):
        rows = slice(V_DIM * hd, V_DIM * (hd + 1))
        o_h = (jnp.dot(vp_t[rows, :], p_p, preferred_element_type=F32)
               + jnp.dot(vn_t[rows, :], p_n, preferred_element_type=F32))
        o_t = o_t + jnp.where(lane_head == hd, o_h, 0.0)
    o_all = (o_t * inv_l).T
    for hd in range(HEADS):
        o_ref[:, V_DIM * hd:V_DIM * (hd + 1)] = o_all[sq * hd:sq * (hd + 1), :].astype(BF16)


def _attn_sample(q, past_lat, past_kr, lat, kr, w_uk, w_uv_t, g_qk_k):
    b, _, sq, _ = q.shape
    past = past_lat.shape[1]
    rows = lambda n, w: pl.BlockSpec((None, n, w), lambda bi: (bi, 0, 0))
    full = lambda a: pl.BlockSpec(a.shape, lambda bi: (0,) * a.ndim)
    return pl.pallas_call(
        functools.partial(_attn_sample_body, past, sq),
        grid=(b,),
        in_specs=[pl.BlockSpec((None, HEADS, sq, QK_PAD), lambda bi: (bi, 0, 0, 0)),
                  rows(past, KV_LORA), rows(past, ROPE), rows(sq, KV_LORA), rows(sq, ROPE),
                  full(w_uk), full(w_uv_t), full(g_qk_k)],
        out_specs=rows(sq, HEADS * V_DIM),
        out_shape=jax.ShapeDtypeStruct((b, sq, HEADS * V_DIM), BF16),
        scratch_shapes=[pltpu.VMEM((HEADS, past, QK_PAD), BF16), pltpu.VMEM((HEADS, sq, QK_PAD), BF16)],
        compiler_params=_cparams(("arbitrary",), VMEM_SMALL),
        name="attn_sample",
    )(q, past_lat, past_kr, lat, kr, w_uk, w_uv_t, g_qk_k)


def _gla_masks(c, rows):
    idx = np.arange(rows)
    same = (idx // c)[:, None] == (idx // c)[None, :]
    le = same & (idx[None, :] <= idx[:, None])
    gt = same & (idx[None, :] > idx[:, None])
    return np.concatenate([le, gt], axis=0).astype(np.float32), int(np.log2(c))


def _level_exponents(b, g, c, level):
    n = c >> level
    rows = b.shape[0]
    row = lax.broadcasted_iota(I32, (rows, 1), 0)
    if n >= 8:
        split = b.reshape(rows // n, n, LANES)[:, n // 2 - 1:n // 2, :]
        split = jnp.broadcast_to(split, (rows // n, n, LANES)).reshape(rows, LANES)
        return jnp.where((row & (n // 2)) != 0, b - split, split - b)
    g_prev = pltpu.roll(g, 1, 0)
    g_next = pltpu.roll(g, rows - 1, 0)
    if n == 4:
        r = row & 3
        return jnp.where(r == 0, g_next, jnp.where(r == 1, 0.0, jnp.where(r == 2, g, g + g_prev)))
    assert n == 2
    return jnp.where((row & 1) != 0, g, 0.0)


def _gla_body(c, n_chunks, unit, levels, mall_ref, q_ref, k_ref, g_ref, v_ref, r_ref, s0_ref, gout_ref,
              o_ref, sfin_ref, st_scr):
    it = pl.program_id(1)

    @pl.when(it == 0)
    def _():
        st_scr[...] = s0_ref[...]

    ru = unit * c
    lane = lax.broadcasted_iota(I32, (ru, LANES), 1)
    head_lanes = (lane < GLA_DK, lane >= GLA_DK)
    st_lane_lo = lax.broadcasted_iota(I32, (GLA_DV, LANES), 1) < GLA_DK
    row = lax.broadcasted_iota(I32, (ru, 1), 0)
    ri = lax.broadcasted_iota(I32, (ru, ru), 0)
    ci = lax.broadcasted_iota(I32, (ru, ru), 1)
    mall = mall_ref[...]

    for un in range(n_chunks // unit):
        rows = slice(un * ru, (un + 1) * ru)
        for p in range(HEADS // 2):
            ls = slice(LANES * p, LANES * (p + 1))
            g = g_ref[rows, ls]
            q = q_ref[rows, ls]
            k = k_ref[rows, ls]
            g_hi = g.astype(BF16)
            g_lo = (g - g_hi.astype(F32)).astype(BF16)
            e2 = jnp.dot(mall, jnp.concatenate([g_hi, g_lo], axis=1), preferred_element_type=F32)
            e = e2[:, 0:LANES] + e2[:, LANES:2 * LANES]
            b = e[0:ru]
            eb = jnp.exp(b)
            qb = q * eb
            kd = (k * jnp.exp(e[ru:2 * ru])).astype(BF16)
            qs, ks = [q], [k.astype(BF16)]
            for l in range(levels):
                bottom = (row & (c >> (l + 1))) != 0
                decay = jnp.exp(_level_exponents(b, g, c, l))
                qs.append(jnp.where(bottom, q * decay, 0.0))
                ks.append(jnp.where(bottom, 0.0, k * decay).astype(BF16))
            states = [st_scr[p]]
            for j in range(unit):
                cr = slice(j * c, (j + 1) * c)
                upd = [lax.dot_general(v_ref[rows, GLA_DV * (2 * p + hh):GLA_DV * (2 * p + hh + 1)][cr, :], kd[cr, :],
                                       (((0,), (0,)), ((), ())), preferred_element_type=F32) for hh in range(2)]
                d_last = eb[j * c + c - 1:j * c + c, :]
                states.append(states[-1] * d_last + jnp.where(st_lane_lo, upd[0], upd[1]))
            st_scr[p] = states[-1]
            for hh in range(2):
                hd = 2 * p + hh
                sel = head_lanes[hh]
                a = jnp.where(ri == ci, _nt(jnp.where(sel, qs[0], 0.0).astype(BF16), ks[0]), 0.0)
                for l in range(levels):
                    pr = _nt(jnp.where(sel, qs[l + 1], 0.0).astype(BF16), ks[l + 1])
                    a = a + jnp.where((ri ^ ci) < (c >> l), pr, 0.0)
                vh = v_ref[rows, GLA_DV * hd:GLA_DV * (hd + 1)]
                qb_h = jnp.where(sel, qb, 0.0).astype(BF16)
                o_state = [_nt(qb_h[j * c:(j + 1) * c, :], states[j].astype(BF16)) for j in range(unit)]
                o = jnp.dot(a.astype(BF16), vh, preferred_element_type=F32) + jnp.concatenate(o_state, axis=0)
                on = o * _rms(o, GLA_DV) * gout_ref[...]
                r = r_ref[rows, GLA_DV * hd:GLA_DV * (hd + 1)]
                o_ref[rows, GLA_DV * hd:GLA_DV * (hd + 1)] = (on * (r * jax.nn.sigmoid(r))).astype(BF16)

    @pl.when(it == pl.num_programs(1) - 1)
    def _():
        sfin_ref[...] = st_scr[...]


def _gla(gq, gk, gl, gv, gr, s0, g_out):
    b, s, _ = gq.shape
    c = min(CHUNK, s)
    tile = min(s, 8 * c)
    unit = next(u for u in (4, 2, 1) if (tile // c) % u == 0)
    masks, levels = _gla_masks(c, unit * c)
    mall = jnp.asarray(masks, BF16)
    tok = lambda w: pl.BlockSpec((None, tile, w), lambda bi, i: (bi, i, 0))
    st_spec = pl.BlockSpec((None, HEADS // 2, GLA_DV, LANES), lambda bi, i: (bi, 0, 0, 0))
    return pl.pallas_call(
        functools.partial(_gla_body, c, tile // c, unit, levels),
        grid=(b, s // tile),
        in_specs=[pl.BlockSpec(mall.shape, lambda bi, i: (0, 0)),
                  tok(HEADS * GLA_DK), tok(HEADS * GLA_DK), tok(HEADS * GLA_DK), tok(HEADS * GLA_DV),
                  tok(HEADS * GLA_DV), st_spec, pl.BlockSpec(g_out.shape, lambda bi, i: (0, 0))],
        out_specs=(tok(HEADS * GLA_DV), st_spec),
        out_shape=(jax.ShapeDtypeStruct((b, s, HEADS * GLA_DV), BF16),
                   jax.ShapeDtypeStruct((b, HEADS // 2, GLA_DV, LANES), F32)),
        scratch_shapes=[pltpu.VMEM((HEADS // 2, GLA_DV, LANES), F32)],
        compiler_params=_cparams(("arbitrary", "arbitrary")),
        name="gla",
    )(mall, gq, gk, gl, gv, gr, s0, g_out)


def _state_to_pairs(s):
    b = s.shape[0]
    s = s.reshape(b, HEADS // 2, 2, GLA_DK, GLA_DV)
    return jnp.transpose(s, (0, 1, 4, 2, 3)).reshape(b, HEADS // 2, GLA_DV, 2 * GLA_DK)


def _state_from_pairs(s):
    b = s.shape[0]
    s = s.reshape(b, HEADS // 2, GLA_DV, 2, GLA_DK)
    return jnp.transpose(s, (0, 1, 3, 4, 2)).reshape(b, HEADS, GLA_DK, GLA_DV)


def _post_body(x_ref, om_ref, og_ref, gt_ref, sc_ref, sh_ref, wo_ref, gffn_ref, wr_ref, br_ref,
               x2_ref, h_ref, idx_ref, wt_ref, rank_ref, cnt_ref):
    half = om_ref.shape[-1]
    mix = (jnp.dot(om_ref[...], wo_ref[0:half, :], preferred_element_type=F32)
           + jnp.dot(og_ref[...], wo_ref[half:2 * half, :], preferred_element_type=F32))
    x2 = x_ref[...] + gt_ref[...] * mix
    x2_ref[...] = x2
    d = x2.shape[-1]
    h = (x2 * _rms(x2, d) * gffn_ref[...]) * (1.0 + sc_ref[...]) + sh_ref[...]
    h_hi = h.astype(BF16)
    h_ref[...] = h_hi
    h_lo = (h - h_hi.astype(F32)).astype(BF16)
    logits = _nt(wr_ref[0], h_hi) + _nt(wr_ref[0], h_lo) + _nt(wr_ref[1], h_hi) + br_ref[...]
    n_exp, tm = logits.shape
    eid = lax.broadcasted_iota(I32, (n_exp, tm), 0)
    vals, tops, ids = logits, [], []
    for _ in range(TOP_K):
        m = jnp.max(vals, axis=0, keepdims=True)
        sel = jnp.min(jnp.where(vals == m, eid, n_exp), axis=0, keepdims=True)
        tops.append(m)
        ids.append(sel)
        vals = jnp.where(eid == sel, -jnp.inf, vals)
    es = [jnp.exp(t - tops[0]) for t in tops]
    tot = es[0] + es[1] + es[2] + es[3]
    idx_ref[...] = jnp.concatenate(ids, axis=0)
    wt_ref[...] = jnp.concatenate([e / tot for e in es], axis=0)
    hits = [eid == sel for sel in ids]
    member = jnp.zeros((n_exp, tm), F32)
    for hk in hits:
        member = member + jnp.where(hk, 1.0, 0.0)
    before = lax.broadcasted_iota(I32, (tm, tm), 0) < lax.broadcasted_iota(I32, (tm, tm), 1)
    prefix = jnp.dot(member.astype(BF16), jnp.where(before, 1.0, 0.0).astype(BF16), preferred_element_type=F32)
    rank_ref[...] = jnp.concatenate(
        [jnp.sum(jnp.where(hk, prefix, 0.0), axis=0, keepdims=True) for hk in hits], axis=0).astype(I32)
    cnt_ref[...] = jnp.broadcast_to(jnp.sum(member, axis=1, keepdims=True), (n_exp, LANES)).astype(I32)


def _post(x, om, og, gate, scale, shift, w_o, g_ffn, w_r2, b_r):
    t, d = x.shape
    tm = ROUTE_TILE
    per_tok = gate.shape[0] == t
    mod = pl.BlockSpec((tm, d), lambda i: (i, 0)) if per_tok else pl.BlockSpec((1, d), lambda i: (0, 0))
    tok = lambda w: pl.BlockSpec((tm, w), lambda i: (i, 0))
    full = lambda a: pl.BlockSpec(a.shape, lambda i: (0,) * a.ndim)
    return pl.pallas_call(
        _post_body,
        grid=(t // tm,),
        in_specs=[tok(d), tok(om.shape[1]), tok(og.shape[1]), mod, mod, mod, full(w_o), full(g_ffn), full(w_r2),
                  full(b_r)],
        out_specs=(tok(d), tok(d),
                   pl.BlockSpec((TOP_K, tm), lambda i: (0, i)), pl.BlockSpec((TOP_K, tm), lambda i: (0, i)),
                   pl.BlockSpec((TOP_K, tm), lambda i: (0, i)),
                   pl.BlockSpec((None, N_EXPERTS, LANES), lambda i: (i, 0, 0))),
        out_shape=(jax.ShapeDtypeStruct((t, d), F32), jax.ShapeDtypeStruct((t, d), BF16),
                   jax.ShapeDtypeStruct((TOP_K, t), I32), jax.ShapeDtypeStruct((TOP_K, t), F32),
                   jax.ShapeDtypeStruct((TOP_K, t), I32), jax.ShapeDtypeStruct((t // tm, N_EXPERTS, LANES), I32)),
        compiler_params=_cparams(("arbitrary",), VMEM_SMALL),
        name="post",
    )(x, om, og, gate, scale, shift, w_o, g_ffn, w_r2, b_r)


def _max_tile_rows():
    return TOP_K * ROUTE_TILE + N_EXPERTS * (RUN_CHUNK - 1)


def _max_big_pieces():
    return _max_tile_rows() // BIG_PIECE


def _sort_capacity():
    return _round_up(_max_tile_rows(), SORT_ROWS)


class _Tab:
    def __init__(self):
        nb = _max_big_pieces()
        self.src_big, self.dst_big = 0, nb
        self.src_small, self.dst_small = 2 * nb, 2 * nb + N_EXPERTS
        self.n_big = 2 * nb + 2 * N_EXPERTS
        self.n_small, self.n_rows = self.n_big + 1, self.n_big + 2
        self.width = self.n_big + 3


def _route_tables(idx, lrank, cnt3):
    nt = cnt3.shape[0]
    t = idx.shape[1]
    cnt = cnt3[:, :, 0]
    run = _round_up(cnt, RUN_CHUNK)
    lo_end = jnp.cumsum(run, axis=1)
    lo = lo_end - run
    region = _round_up(jnp.sum(run, axis=0), EXPERT_ROWS)
    g_end = jnp.cumsum(region)
    run_dest = (g_end - region)[None, :] + jnp.cumsum(run, axis=0) - run
    n_ch = run // RUN_CHUNK
    big, small = n_ch // 2, n_ch % 2

    def piece_list(count, n_out):
        end = jnp.cumsum(count, axis=1)
        p = jnp.arange(n_out, dtype=I32)
        e = jnp.minimum(jnp.sum(end[:, None, :] <= p[None, :, None], axis=2), N_EXPERTS - 1)
        pick = e[:, :, None] == jnp.arange(N_EXPERTS, dtype=I32)[None, None, :]
        of_run = lambda a: jnp.sum(jnp.where(pick, a[:, None, :], 0), axis=2)
        return of_run, p[None, :] - of_run(end - count), end[:, -1]

    of_big, within_big, n_big = piece_list(big, _max_big_pieces())
    of_small, _, n_small = piece_list(small, N_EXPERTS)
    table = jnp.concatenate([
        of_big(lo) + BIG_PIECE * within_big, of_big(run_dest) + BIG_PIECE * within_big,
        of_small(lo + BIG_PIECE * big), of_small(run_dest + BIG_PIECE * big),
        n_big[:, None], n_small[:, None], lo_end[:, -1:]], axis=1).astype(I32)
    table = table.reshape(nt, 1, _Tab().width)
    eid =jnp.arange(N_EXPERTS, dtype=I32)[:, None]
    lo_tok = jnp.repeat(lo.T, ROUTE_TILE, axis=1)
    lpos = jnp.stack([jnp.sum(jnp.where(idx[k][None, :] == eid, lo_tok, 0), axis=0) for k in range(TOP_K)])
    lpos = (lpos + lrank).astype(I32)
    n_blocks = _round_up(t * TOP_K + nt * N_EXPERTS * (RUN_CHUNK - 1), EXPERT_ROWS) // EXPERT_ROWS + N_EXPERTS
    b_start = jnp.arange(n_blocks, dtype=I32) * EXPERT_ROWS
    blk_e = jnp.minimum(jnp.sum(g_end[None, :] <= b_start[:, None], axis=1), N_EXPERTS - 1).astype(I32)
    n_valid = (g_end[-1:] // EXPERT_ROWS).astype(I32)
    used = region > 0
    e_ids = jnp.arange(N_EXPERTS, dtype=I32)
    later_used = used[None, :] & (e_ids[None, :] > e_ids[:, None])
    nxt_e = jnp.min(jnp.where(later_used, e_ids[None, :], N_EXPERTS), axis=1)
    nxt_e = jnp.where(nxt_e < N_EXPERTS, nxt_e, -1).astype(I32)
    ord_e = (jnp.cumsum(used.astype(I32)) - 1).astype(I32)
    tail = jnp.concatenate([jnp.where(used, g_end - EXPERT_ROWS, -1), n_valid]).astype(I32)
    tail = tail.reshape(1, N_EXPERTS + 1)
    return dict(table=table, lpos=lpos, blk_e=blk_e, n_valid=n_valid, nxt_e=nxt_e, ord_e=ord_e, tail=tail,
                n_blocks=n_blocks)


def _chunk_copy(src, dst, sem):
    return pltpu.make_async_copy(src, dst, sem)


def _for_row_blocks(n_rows, body):
    full = n_rows // SORT_ROWS
    rem = n_rows - full * SORT_ROWS
    tail = pl.multiple_of(full * SORT_ROWS, SORT_ROWS)

    def whole(rb, carry):
        body(pl.multiple_of(rb * SORT_ROWS, SORT_ROWS), SORT_ROWS)
        return carry

    lax.fori_loop(0, full, whole, 0)

    @pl.when(rem > SORT_ROWS // 2)
    def _():
        body(tail, SORT_ROWS)

    @pl.when((rem > 0) & (rem <= SORT_ROWS // 2))
    def _():
        body(tail, SORT_ROWS // 2)


def _for_chunks(n, body):
    groups = n // CHUNK_UNROLL

    def group(g, carry):
        for u in range(CHUNK_UNROLL):
            body(g * CHUNK_UNROLL + u, u % DMA_QUEUES)
        return carry

    def single(c, carry):
        body(c, 0)
        return carry

    lax.fori_loop(0, groups, group, 0)
    lax.fori_loop(groups * CHUNK_UNROLL, n, single, 0)


def _scatter_body(nt_a, n_blocks, tab_ref, prv_ref, tail_ref, lpos_ref, ha_ref, hb_ref, xout_ref, sorted_scr,
                  zero_scr, sems, zero_sem):
    i = pl.program_id(0)
    slot = i % 2
    tb = _Tab()
    tt = ha_ref.shape[0]

    @pl.when(i == 0)
    def _():
        zero_scr[...] = jnp.zeros_like(zero_scr)
        n_valid = tail_ref[0, N_EXPERTS]

        def block(start):
            return xout_ref.at[pl.ds(pl.multiple_of(start, EXPERT_ROWS), EXPERT_ROWS)]

        for e in range(N_EXPERTS):
            @pl.when(tail_ref[0, e] >= 0)
            def _():
                _chunk_copy(zero_scr, block(tail_ref[0, e]), zero_sem).start()

        def fill(b, carry):
            _chunk_copy(zero_scr, block(b * EXPERT_ROWS), zero_sem).start()
            return carry

        def fill_done(b, carry):
            _chunk_copy(zero_scr, block(0), zero_sem).wait()
            return carry

        lax.fori_loop(n_valid, n_blocks, fill, 0)
        for e in range(N_EXPERTS):
            @pl.when(tail_ref[0, e] >= 0)
            def _():
                _chunk_copy(zero_scr, block(0), zero_sem).wait()
        lax.fori_loop(n_valid, n_blocks, fill_done, 0)

    lp16 = lpos_ref[...].astype(jnp.int16)
    h = jnp.where(i < nt_a, ha_ref[...], hb_ref[...])

    def sort_block(r0, size):
        rid = (r0 + lax.broadcasted_iota(I32, (size, tt), 0)).astype(jnp.int16)
        onehot = jnp.zeros((size, tt), BF16)
        for k in range(TOP_K):
            onehot = onehot + jnp.where(lp16[k:k + 1, :] == rid, jnp.ones((), BF16), jnp.zeros((), BF16))
        sorted_scr[slot, pl.ds(r0, size), :] = jnp.dot(onehot, h, preferred_element_type=F32).astype(BF16)

    _for_row_blocks(tab_ref[0, tb.n_rows], sort_block)

    def piece(sl, src_row, dst_row, rows):
        src = sorted_scr.at[sl, pl.ds(pl.multiple_of(src_row, RUN_CHUNK), rows)]
        dst = xout_ref.at[pl.ds(pl.multiple_of(dst_row, RUN_CHUNK), rows)]
        return _chunk_copy(src, dst, sems.at[sl])

    def retire(tab, sl):
        _for_chunks(tab[0, tb.n_big], lambda p, queue: piece(sl, 0, 0, BIG_PIECE).wait())
        _for_chunks(tab[0, tb.n_small], lambda p, queue: piece(sl, 0, 0, RUN_CHUNK).wait())

    _for_chunks(tab_ref[0, tb.n_big], lambda p, queue: piece(
        slot, tab_ref[0, tb.src_big + p], tab_ref[0, tb.dst_big + p], BIG_PIECE).start(priority=queue))
    _for_chunks(tab_ref[0, tb.n_small], lambda p, queue: piece(
        slot, tab_ref[0, tb.src_small + p], tab_ref[0, tb.dst_small + p], RUN_CHUNK).start(priority=queue))

    @pl.when(i > 0)
    def _():
        retire(prv_ref, 1 - slot)

    @pl.when(i == pl.num_programs(0) - 1)
    def _():
        retire(tab_ref, slot)


def _scatter(table, tail, lpos, h_a, h_b, n_blocks):
    d = h_a.shape[1]
    nt_a, nt_b = h_a.shape[0] // ROUTE_TILE, h_b.shape[0] // ROUTE_TILE
    sort_cap = _sort_capacity()
    tab_spec = lambda f: pl.BlockSpec((None, 1, _Tab().width), lambda i: (f(i), 0, 0), memory_space=pltpu.SMEM)
    return pl.pallas_call(
        functools.partial(_scatter_body, nt_a, n_blocks),
        grid=(nt_a + nt_b,),
        in_specs=[tab_spec(lambda i: i), tab_spec(lambda i: jnp.maximum(i - 1, 0)),
                  pl.BlockSpec((1, N_EXPERTS + 1), lambda i: (0, 0), memory_space=pltpu.SMEM),
                  pl.BlockSpec((TOP_K, ROUTE_TILE), lambda i: (0, i)),
                  pl.BlockSpec((ROUTE_TILE, d), lambda i: (jnp.minimum(i, nt_a - 1), 0)),
                  pl.BlockSpec((ROUTE_TILE, d), lambda i: (jnp.maximum(i - nt_a, 0), 0))],
        out_specs=pl.BlockSpec(memory_space=pl.ANY),
        out_shape=jax.ShapeDtypeStruct((n_blocks * EXPERT_ROWS, d), BF16),
        scratch_shapes=[pltpu.VMEM((2, sort_cap, d), BF16), pltpu.VMEM((EXPERT_ROWS, d), BF16),
                        pltpu.SemaphoreType.DMA((2,)), pltpu.SemaphoreType.DMA(())],
        compiler_params=_cparams(("arbitrary",), VMEM_MID),
        name="scatter",
    )(table, table, tail, lpos, h_a, h_b)


def _experts_body(be_ref, nv_ref, nxt_ref, ord_ref, x_ref, wgu_hbm, bgu_ref, wd_hbm, bd_ref, y_ref,
                  wgu_f, wd_f, wgu_s, wd_s, sem_gu, sem_d):
    b = pl.program_id(0)
    e = be_ref[b]
    prev = be_ref[jnp.maximum(b - 1, 0)]
    valid = b < nv_ref[0]
    d_ff = wd_s.shape[0]
    slot = ord_ref[e] % 2

    def weights(expert, sl):
        return (pltpu.make_async_copy(wgu_hbm.at[expert], wgu_f.at[sl], sem_gu.at[sl]),
                pltpu.make_async_copy(wd_hbm.at[expert], wd_f.at[sl], sem_d.at[sl]))

    @pl.when(valid & ((b == 0) | (e != prev)))
    def _():
        @pl.when(b == 0)
        def _():
            for cp in weights(e, slot):
                cp.start()

        for cp in weights(e, slot):
            cp.wait()

        @pl.when(nxt_ref[e] >= 0)
        def _():
            for cp in weights(nxt_ref[e], 1 - slot):
                cp.start()

        wgu_s[...] = wgu_f[slot].astype(BF16)
        wd_s[...] = wd_f[slot].astype(BF16)

    @pl.when(valid)
    def _():
        gu = jnp.dot(x_ref[...], wgu_s[...], preferred_element_type=F32) + bgu_ref[...]
        gate = jnp.minimum(gu[:, 0:d_ff], SWIGLU_LIMIT)
        up = jnp.clip(gu[:, d_ff:2 * d_ff], -SWIGLU_LIMIT, SWIGLU_LIMIT)
        act = ((up + 1.0) * (gate * jax.nn.sigmoid(gate * SWIGLU_ALPHA))).astype(BF16)
        y_ref[...] = (jnp.dot(act, wd_s[...], preferred_element_type=F32) + bd_ref[...]).astype(BF16)

    @pl.when(jnp.logical_not(valid))
    def _():
        y_ref[...] = jnp.zeros_like(y_ref)


def _experts(blk_e, n_valid, nxt_e, ord_e, xpad, w_gu, b_gu, w_down, b_down):
    m, d = xpad.shape
    nb = m // EXPERT_ROWS
    n_exp, _, f2 = w_gu.shape
    d_ff = w_down.shape[1]
    last = lambda b, be, nv: jnp.minimum(b, nv[0] - 1)
    grid_spec = pltpu.PrefetchScalarGridSpec(
        num_scalar_prefetch=4,
        grid=(nb,),
        in_specs=[pl.BlockSpec((EXPERT_ROWS, d), lambda b, be, nv, nx, od: (last(b, be, nv), 0)),
                  pl.BlockSpec(memory_space=pl.ANY),
                  pl.BlockSpec((None, 1, f2), lambda b, be, nv, nx, od: (be[last(b, be, nv)], 0, 0)),
                  pl.BlockSpec(memory_space=pl.ANY),
                  pl.BlockSpec((None, 1, d), lambda b, be, nv, nx, od: (be[last(b, be, nv)], 0, 0))],
        out_specs=pl.BlockSpec((EXPERT_ROWS, d), lambda b, be, nv, nx, od: (b, 0)),
        scratch_shapes=[pltpu.VMEM((2, d, f2), F32), pltpu.VMEM((2, d_ff, d), F32),
                        pltpu.VMEM((d, f2), BF16), pltpu.VMEM((d_ff, d), BF16),
                        pltpu.SemaphoreType.DMA((2,)), pltpu.SemaphoreType.DMA((2,))],
    )
    return pl.pallas_call(
        _experts_body,
        grid_spec=grid_spec,
        out_shape=jax.ShapeDtypeStruct((m, d), BF16),
        compiler_params=_cparams(("arbitrary",), VMEM_BIG),
        name="experts",
    )(blk_e, n_valid, nxt_e, ord_e, xpad, w_gu, b_gu.reshape(n_exp, 1, f2), w_down, b_down.reshape(n_exp, 1, d))


def _combine_body(tab_ref, nxt_ref, lpos_ref, wt_ref, x2_ref, gt_ref, y_ref, o_ref, ysort_scr, sems):
    i = pl.program_id(0)
    slot = i % 2
    tb = _Tab()
    tt = x2_ref.shape[0]

    def piece(sl, sorted_row, expert_row, rows):
        src = y_ref.at[pl.ds(pl.multiple_of(expert_row, RUN_CHUNK), rows)]
        dst = ysort_scr.at[sl, pl.ds(pl.multiple_of(sorted_row, RUN_CHUNK), rows)]
        return _chunk_copy(src, dst, sems.at[sl])

    def fetch(tab, sl):
        _for_chunks(tab[0, tb.n_big], lambda p, queue: piece(
            sl, tab[0, tb.src_big + p], tab[0, tb.dst_big + p], BIG_PIECE).start(priority=queue))
        _for_chunks(tab[0, tb.n_small], lambda p, queue: piece(
            sl, tab[0, tb.src_small + p], tab[0, tb.dst_small + p], RUN_CHUNK).start(priority=queue))

    @pl.when(i == 0)
    def _():
        ysort_scr[...] = jnp.zeros_like(ysort_scr)
        fetch(tab_ref, 0)

    @pl.when(i + 1 < pl.num_programs(0))
    def _():
        fetch(nxt_ref, 1 - slot)

    _for_chunks(tab_ref[0, tb.n_big], lambda p, queue: piece(slot, 0, 0, BIG_PIECE).wait())
    _for_chunks(tab_ref[0, tb.n_small], lambda p, queue: piece(slot, 0, 0, RUN_CHUNK).wait())

    lp16 = lpos_ref[...].astype(jnp.int16)
    w = wt_ref[...].astype(BF16)
    o_ref[...] = x2_ref[...]
    gt = gt_ref[...]

    def gather_block(r0, size):
        rid = (r0 + lax.broadcasted_iota(I32, (size, tt), 0)).astype(jnp.int16)
        pw_t = jnp.zeros((size, tt), BF16)
        for k in range(TOP_K):
            pw_t = pw_t + jnp.where(lp16[k:k + 1, :] == rid, w[k:k + 1, :], jnp.zeros((), BF16))
        ys = ysort_scr[slot, pl.ds(r0, size), :]
        part = lax.dot_general(pw_t, ys, (((0,), (0,)), ((), ())), preferred_element_type=F32)
        o_ref[...] = o_ref[...] + gt * part

    _for_row_blocks(tab_ref[0, tb.n_rows], gather_block)


def _combine(table, lpos, wts, x2, gate, ypad):
    t, d = x2.shape
    nt = t // ROUTE_TILE
    per_tok = gate.shape[0] == t
    mod = pl.BlockSpec((ROUTE_TILE, d), lambda i: (i, 0)) if per_tok else pl.BlockSpec((1, d), lambda i: (0, 0))
    sort_cap = _sort_capacity()
    tab_spec = lambda f: pl.BlockSpec((None, 1, _Tab().width), lambda i: (f(i), 0, 0), memory_space=pltpu.SMEM)
    return pl.pallas_call(
        _combine_body,
        grid=(nt,),
        in_specs=[tab_spec(lambda i: i), tab_spec(lambda i: jnp.minimum(i + 1, nt - 1)),
                  pl.BlockSpec((TOP_K, ROUTE_TILE), lambda i: (0, i)),
                  pl.BlockSpec((TOP_K, ROUTE_TILE), lambda i: (0, i)),
                  pl.BlockSpec((ROUTE_TILE, d), lambda i: (i, 0)),
                  mod,
                  pl.BlockSpec(memory_space=pl.ANY)],
        out_specs=pl.BlockSpec((ROUTE_TILE, d), lambda i: (i, 0)),
        out_shape=jax.ShapeDtypeStruct((t, d), F32),
        scratch_shapes=[pltpu.VMEM((2, sort_cap, d), BF16), pltpu.SemaphoreType.DMA((2,))],
        compiler_params=_cparams(("arbitrary",), VMEM_MID),
        name="combine",
    )(table, table, lpos, wts, x2, gate, ypad)


def _prep_weights(g_norm_mix, w_in, g_q_a, w_uq, g_kv_a, w_ukv, g_qk_q, g_qk_k, w_g2, b_g2, g_gla_out, w_o,
                  g_norm_ffn, w_router, b_router):
    d = w_in.shape[0]
    w1 = _w1(w_in.T)
    wq = w_uq.reshape(Q_LORA, HEADS, QK)
    wuq = jnp.concatenate([wq[:, :, 0:NOPE].reshape(Q_LORA, HEADS * NOPE),
                           wq[:, :, NOPE:QK].reshape(Q_LORA, HEADS * ROPE)], axis=1).astype(BF16)
    wkv = w_ukv.reshape(KV_LORA, HEADS, NOPE + V_DIM)
    wuk = wkv[:, :, 0:NOPE].reshape(KV_LORA, HEADS * NOPE).astype(BF16)
    wuv_t = wkv[:, :, NOPE:].reshape(KV_LORA, HEADS * V_DIM).T.astype(BF16)
    pad_rope = lambda g: jnp.stack([g[0:NOPE], jnp.concatenate([g[NOPE:QK], jnp.zeros((QK_PAD - QK,), g.dtype)])])
    inv = ROPE_THETA ** (-jnp.arange(HALF, dtype=F32) / HALF)
    sign = jnp.concatenate([-jnp.ones((HALF,), F32), jnp.ones((HALF,), F32)])
    rope_tab = jnp.stack([jnp.tile(inv, LANES // HALF), jnp.tile(sign, LANES // ROPE)])
    wg2 = jnp.concatenate([w_g2, jnp.zeros((LANES - GATE_RANK, w_g2.shape[1]), w_g2.dtype)], axis=0).astype(BF16)
    wr_t = w_router.T
    wr_hi = wr_t.astype(BF16)
    wr_lo = (wr_t - wr_hi.astype(F32)).astype(BF16)
    return dict(
        g_mix=g_norm_mix.reshape(1, d), w1=w1, g_qa=g_q_a.reshape(1, -1), w_uq=wuq, g_kv=g_kv_a.reshape(1, -1),
        w_uk=wuk, w_uv_t=wuv_t, g_qk_q=pad_rope(g_qk_q), g_qk_k=pad_rope(g_qk_k), rope=rope_tab, w_g2=wg2,
        b_g2=b_g2.reshape(1, -1), g_out=g_gla_out.reshape(1, -1), w_o=w_o.astype(BF16),
        g_ffn=g_norm_ffn.reshape(1, d), w_r2=jnp.stack([wr_hi, wr_lo]), b_r=b_router.reshape(-1, 1))


def _mixer(x, mod, pos0, past_lat, past_kr, s0_pairs, wts):
    b, s, d = x.shape
    t = b * s
    if b > 1 and s & (s - 1) == 0 and min(t, TOKEN_TILE) % s == 0:
        per_tok = lambda j: jnp.broadcast_to(mod[:, j:j + 1], (b, s, d)).reshape(1, t, d)
        outs = _proj(x.reshape(1, t, d), per_tok(0), per_tok(1), pos0, wts, period=s)
        q = outs[0].reshape(HEADS, b, s, QK_PAD).transpose(1, 0, 2, 3)
        lat, kr, gq, gk, gv, gl, gr = [o.reshape(b, s, o.shape[-1]) for o in outs[1:]]
    else:
        q, lat, kr, gq, gk, gv, gl, gr = _proj(x, mod[:, 0:1], mod[:, 1:2], pos0, wts)
    kv_w = (wts["w_uk"], wts["w_uv_t"], wts["g_qk_k"])
    if past_lat is None:
        k_new, vt_new = _kv(lat, kr, *kv_w)
        o_mla = _attn_prompt(q, k_new, vt_new)
    else:
        o_mla = _attn_sample(q, past_lat, past_kr, lat, kr, *kv_w)
    o_gla, s_fin = _gla(gq, gk, gl, gv, gr, s0_pairs, wts["g_out"])
    if b == 1:
        rows = lambda j: mod[0, j:j + 1]
    else:
        rows = lambda j: jnp.broadcast_to(mod[:, j:j + 1], (b, s, d)).reshape(t, d)
    x2, h2, idx, wt, lrank, cnt = _post(x.reshape(t, d), o_mla.reshape(t, -1), o_gla.reshape(t, -1), rows(2), rows(4),
                                        rows(3), wts["w_o"], wts["g_ffn"], wts["w_r2"], wts["b_r"])
    return dict(x2=x2, h2=h2, idx=idx, wt=wt, lrank=lrank, cnt=cnt, gate_f=rows(5), lat=lat, kr=kr, s_fin=s_fin)


def kernel(x_prompt, x_sample, cache_mla_latent, cache_mla_krope, state_gla, c_prompt, c_sample, w_ada, b_ada, g_norm_mix, w_in, g_q_a, w_uq, g_kv_a, w_ukv, g_qk_q, g_qk_k, w_g2, b_g2, g_gla_out, w_o, g_norm_ffn, w_router, b_router, w_gu, b_gu, w_down, b_down):
    depth = w_ada.shape[0]
    assert depth == 1, "single-layer step"
    bp, sp, d = x_prompt.shape
    bs, ss, _ = x_sample.shape
    tp, tsm = bp * sp, bs * ss
    assert tp % ROUTE_TILE == 0 and tsm % ROUTE_TILE == 0, "token counts must be whole routing tiles"
    past = cache_mla_latent.shape[2]
    layer = lambda a: a.reshape(a.shape[1:])
    wts = _prep_weights(*[layer(a) for a in (g_norm_mix, w_in, g_q_a, w_uq, g_kv_a, w_ukv, g_qk_q, g_qk_k, w_g2, b_g2,
                                             g_gla_out, w_o, g_norm_ffn, w_router, b_router)])
    w_gu, b_gu, w_down, b_down = layer(w_gu), layer(b_gu), layer(w_down), layer(b_down)

    mod = _ada(jnp.concatenate([c_prompt, c_sample], axis=0), layer(w_ada), layer(b_ada)).reshape(bp + bs, 6, d)
    zero_state = jnp.zeros((bp, HEADS // 2, GLA_DV, LANES), F32)
    pr = _mixer(x_prompt, mod[:bp], 0, None, None, zero_state, wts)
    sa = _mixer(x_sample, mod[bp:], past, layer(cache_mla_latent), layer(cache_mla_krope),
                _state_to_pairs(layer(state_gla)), wts)

    idx = jnp.concatenate([pr["idx"], sa["idx"]], axis=1)
    lrank = jnp.concatenate([pr["lrank"], sa["lrank"]], axis=1)
    rt = _route_tables(idx, lrank, jnp.concatenate([pr["cnt"], sa["cnt"]], axis=0))
    ntp = tp // ROUTE_TILE
    lpos, table = rt["lpos"], rt["table"]
    xpad = _scatter(table, rt["tail"], lpos, pr["h2"], sa["h2"], rt["n_blocks"])
    ypad = _experts(rt["blk_e"], rt["n_valid"], rt["nxt_e"], rt["ord_e"], xpad, w_gu, b_gu, w_down, b_down)
    y_p = _combine(table[:ntp], lpos[:, :tp], pr["wt"], pr["x2"], pr["gate_f"], ypad).reshape(bp, sp, d)
    y_s = _combine(table[ntp:], lpos[:, tp:], sa["wt"], sa["x2"], sa["gate_f"], ypad).reshape(bs, ss, d)

    return (y_p, y_s,
            pr["lat"][None], pr["kr"][None], _state_from_pairs(pr["s_fin"])[None],
            sa["lat"][None], sa["kr"][None], _state_from_pairs(sa["s_fin"])[None])
```

```python
import functools

import numpy as np
import jax
import jax.numpy as jnp
from jax import lax
from jax.experimental import pallas as pl
from jax.experimental.pallas import tpu as pltpu

F32 = jnp.float32
BF16 = jnp.bfloat16
I32 = jnp.int32

CHUNK = 64
EPS = 1e-6
HEADS = 4
Q_LORA = 384
KV_LORA = 256
NOPE = 128
ROPE = 64
HALF = ROPE // 2
V_DIM = 128
QK = NOPE + ROPE
QK_PAD = 256
ROPE_THETA = 10000.0
GLA_DK = 64
GLA_DV = 128
GATE_RANK = 16
GATE_NORM = 16.0
N_EXPERTS = 32
TOP_K = 4
SWIGLU_LIMIT = 7.0
SWIGLU_ALPHA = 1.702
NEG = -1e30
LOG2_E = 1.4426950408889634

LANES = 128
SUBLANES = 8
BF16_ROWS = 16
TOKEN_TILE = 512
ADA_COLS = 1536
ROUTE_TILE = 512
RUN_CHUNK = BF16_ROWS
PIECE_ROWS = (4 * RUN_CHUNK, 2 * RUN_CHUNK, RUN_CHUNK)
SORT_ROWS = 512
CHUNK_UNROLL = 4
EXPERT_ROWS = 256
ATTN_TILE = 1024
MIB = 1024 * 1024
VMEM_BIG = 56 * MIB
VMEM_MID = 48 * MIB
VMEM_SMALL = 40 * MIB


def _cparams(sem, vmem=None):
    return pltpu.CompilerParams(dimension_semantics=sem, vmem_limit_bytes=vmem)


def _nt(a, b):
    return lax.dot_general(a, b, (((1,), (1,)), ((), ())), preferred_element_type=F32)


def _rms(x, width):
    return lax.rsqrt(jnp.sum(x * x, axis=-1, keepdims=True) * (1.0 / width) + EPS)


def _round_up(x, m):
    return ((x + m - 1) // m) * m


def _ada_body(c_ref, w_ref, b_ref, o_ref):
    c = c_ref[...]
    s = (c * jax.nn.sigmoid(c)).astype(BF16)
    o_ref[...] = jnp.dot(s, w_ref[...].astype(BF16), preferred_element_type=F32) + b_ref[...]


def _ada(c, w_ada, b_ada):
    r, d = c.shape
    n = w_ada.shape[1]
    tn = ADA_COLS if n % ADA_COLS == 0 else n
    return pl.pallas_call(
        _ada_body,
        grid=(n // tn,),
        in_specs=[pl.BlockSpec((r, d), lambda j: (0, 0)),
                  pl.BlockSpec((d, tn), lambda j: (0, j)),
                  pl.BlockSpec((1, tn), lambda j: (0, j))],
        out_specs=pl.BlockSpec((r, tn), lambda j: (0, j)),
        out_shape=jax.ShapeDtypeStruct((r, n), F32),
        compiler_params=_cparams(("arbitrary",), VMEM_SMALL),
        name="ada",
    )(c, w_ada, b_ada.reshape(1, n))


_SEG = dict(qa_kr=(0, 512), kva=(512, 768), gq=(768, 1024), gk=(1024, 1280),
            gv=(1280, 1792), gr=(1792, 2304), glr=(2304, 2432))
_W1_COLS = 2432
W1_PREP_COLS = 256


def _w1_source_columns():
    o_kva, o_kr = Q_LORA, Q_LORA + KV_LORA
    o_gq = o_kr + ROPE
    o_gk = o_gq + HEADS * GLA_DK
    o_gv = o_gk + HEADS * GLA_DK
    o_glr = o_gv + HEADS * GLA_DV
    o_gr = o_glr + GATE_RANK
    return [(0, Q_LORA), (o_kr, ROPE), (o_kr, ROPE), (o_kva, KV_LORA), (o_gq, HEADS * GLA_DK),
            (o_gk, HEADS * GLA_DK), (o_gv, HEADS * GLA_DV), (o_gr, HEADS * GLA_DV), (o_glr, GATE_RANK)]


def _w1_body(wt_ref, w1_ref):
    o = 0
    for start, width in _w1_source_columns():
        w1_ref[o:o + width, :] = wt_ref[start:start + width, :].astype(BF16)
        o += width
    w1_ref[o:, :] = jnp.zeros((_W1_COLS - o, w1_ref.shape[1]), BF16)


def _w1(w_in_t):
    n, d = w_in_t.shape
    return pl.pallas_call(
        _w1_body,
        grid=(d // W1_PREP_COLS,),
        in_specs=[pl.BlockSpec((n, W1_PREP_COLS), lambda i: (0, i))],
        out_specs=pl.BlockSpec((_W1_COLS, W1_PREP_COLS), lambda i: (0, i)),
        out_shape=jax.ShapeDtypeStruct((_W1_COLS, d), BF16),
        compiler_params=_cparams(("arbitrary",)),
        name="w1",
    )(w_in_t)


def _proj_body(pos0, ts, period, x_ref, sh_ref, sc_ref, gmix_ref, w1_ref, gqa_ref, wuq_ref, gkv_ref, gqk_ref,
               rope_ref, wg2_ref, bg2_ref,
               q_ref, lat_ref, kr_ref, gq_o, gk_o, gv_o, gl_o, gr_o, trig_scr):
    i = pl.program_id(1)
    x = x_ref[...]
    d = x.shape[-1]
    h = (x * _rms(x, d) * gmix_ref[...]) * (1.0 + sc_ref[...]) + sh_ref[...]
    hb = h.astype(BF16)

    def seg(name):
        a, b = _SEG[name]
        return _nt(hb, w1_ref[a:b, :])

    @pl.when((pl.program_id(0) == 0) & (i == 0))
    def _():
        row = lax.broadcasted_iota(I32, (ts, LANES), 0)
        if period is not None:
            row = row & (period - 1)
        row_ang = row.astype(F32) * rope_ref[0:1, :]
        trig_scr[0] = jnp.cos(row_ang)
        trig_scr[1] = jnp.sin(row_ang)

    tile_pos = pos0 + (i * ts if period is None else 0 * i)
    base_ang = jnp.broadcast_to(tile_pos.astype(F32) * rope_ref[0:1, :], (SUBLANES, LANES))
    cos_a, sin_a = jnp.cos(base_ang)[0:1, :], jnp.sin(base_ang)[0:1, :]
    cos = cos_a * trig_scr[0] - sin_a * trig_scr[1]
    sin = (sin_a * trig_scr[0] + cos_a * trig_scr[1]) * rope_ref[1:2, :]
    lane = lax.broadcasted_iota(I32, (ts, LANES), 1)
    first_half = (lane & HALF) == 0
    low64 = lane < ROPE

    def rope(v):
        partner = jnp.where(first_half, pltpu.roll(v, LANES - HALF, 1), pltpu.roll(v, HALF, 1))
        return v * cos + partner * sin

    qa_kr = seg("qa_kr")
    qa = qa_kr[:, 0:Q_LORA]
    qn = (qa * _rms(qa, Q_LORA) * gqa_ref[...]).astype(BF16)
    qf = jnp.dot(qn, wuq_ref[...], preferred_element_type=F32)
    rope_blocks = (rope(qf[:, 4 * NOPE:4 * NOPE + LANES]), rope(qf[:, 4 * NOPE + LANES:4 * NOPE + 2 * LANES]))
    for hd in range(HEADS):
        nope = qf[:, NOPE * hd:NOPE * (hd + 1)]
        blk = rope_blocks[hd // 2]
        if hd % 2:
            blk = pltpu.roll(blk, ROPE, 1)
        blk = jnp.where(low64, blk, 0.0)
        ss = jnp.sum(nope * nope, axis=-1, keepdims=True) + jnp.sum(blk * blk, axis=-1, keepdims=True)
        scl = lax.rsqrt(ss * (1.0 / QK) + EPS) * (QK ** -0.5 * LOG2_E)
        q_ref[hd, :, 0:NOPE] = (nope * scl * gqk_ref[0:1, :]).astype(BF16)
        q_ref[hd, :, NOPE:QK_PAD] = (blk * scl * gqk_ref[1:2, :]).astype(BF16)

    kva = seg("kva")
    lat_ref[...] = kva * _rms(kva, KV_LORA) * gkv_ref[...]
    kr_ref[...] = rope(qa_kr[:, Q_LORA:Q_LORA + LANES])[:, 0:ROPE]

    gq_o[...] = seg("gq") * (GLA_DK ** -0.5)
    gk_o[...] = seg("gk")
    gv_o[...] = seg("gv").astype(BF16)
    gr_o[...] = seg("gr")
    z = jnp.dot(seg("glr").astype(BF16), wg2_ref[...], preferred_element_type=F32) + bg2_ref[...]
    gl_o[...] = (jnp.minimum(z, 0.0) - jnp.log1p(jnp.exp(-jnp.abs(z)))) * (1.0 / GATE_NORM)


def _proj(x, shift, scale, pos0, wts, period=None):
    b, s, d = x.shape
    ts = min(s, TOKEN_TILE)
    assert period is None or (period & (period - 1) == 0 and ts % period == 0)
    row = lambda a: pl.BlockSpec(a.shape, lambda bi, i: (0,) * a.ndim)
    tok = lambda w: pl.BlockSpec((None, ts, w), lambda bi, i: (bi, i, 0))
    mod = tok(d) if shift.shape[1] == s and s > 1 else pl.BlockSpec((None, 1, d), lambda bi, i: (bi, 0, 0))
    small = [wts["g_mix"], wts["w1"], wts["g_qa"], wts["w_uq"], wts["g_kv"], wts["g_qk_q"], wts["rope"],
             wts["w_g2"], wts["b_g2"]]
    out_shape = (
        jax.ShapeDtypeStruct((b, HEADS, s, QK_PAD), BF16),
        jax.ShapeDtypeStruct((b, s, KV_LORA), F32),
        jax.ShapeDtypeStruct((b, s, ROPE), F32),
        jax.ShapeDtypeStruct((b, s, HEADS * GLA_DK), F32),
        jax.ShapeDtypeStruct((b, s, HEADS * GLA_DK), F32),
        jax.ShapeDtypeStruct((b, s, HEADS * GLA_DV), BF16),
        jax.ShapeDtypeStruct((b, s, HEADS * GLA_DK), F32),
        jax.ShapeDtypeStruct((b, s, HEADS * GLA_DV), F32),
    )
    out_specs = (
        pl.BlockSpec((None, HEADS, ts, QK_PAD), lambda bi, i: (bi, 0, i, 0)),
        tok(KV_LORA), tok(ROPE), tok(HEADS * GLA_DK), tok(HEADS * GLA_DK), tok(HEADS * GLA_DV),
        tok(HEADS * GLA_DK), tok(HEADS * GLA_DV),
    )
    return pl.pallas_call(
        functools.partial(_proj_body, pos0, ts, period),
        grid=(b, s // ts),
        in_specs=[tok(d), mod, mod] + [row(a) for a in small],
        out_specs=out_specs,
        out_shape=out_shape,
        scratch_shapes=[pltpu.VMEM((2, ts, LANES), F32)],
        compiler_params=_cparams(("arbitrary", "arbitrary"), VMEM_BIG),
        name="proj",
    )(x, shift, scale, *small)


def _key_rows(lat, kr, wk_ref, gk_ref, k_out):
    kn_all = jnp.dot(lat, wk_ref[...], preferred_element_type=F32)
    kr_ss = jnp.sum(kr * kr, axis=-1, keepdims=True)
    for hd in range(HEADS):
        kn = kn_all[:, NOPE * hd:NOPE * (hd + 1)]
        scl = lax.rsqrt((jnp.sum(kn * kn, axis=-1, keepdims=True) + kr_ss) * (1.0 / QK) + EPS)
        k_out[hd, :, 0:NOPE] = (kn * scl * gk_ref[0:1, :]).astype(BF16)
        k_out[hd, :, NOPE:QK] = (kr * scl * gk_ref[1:2, 0:ROPE]).astype(BF16)
        k_out[hd, :, QK:QK_PAD] = jnp.zeros((kr.shape[0], QK_PAD - QK), BF16)


def _kv_body(lat_ref, kr_ref, wk_ref, wv_ref, gk_ref, k_ref, v_ref):
    lat = lat_ref[...].astype(BF16)
    _key_rows(lat, kr_ref[...], wk_ref, gk_ref, k_ref)
    v_t = _nt(wv_ref[...], lat)
    for hd in range(HEADS):
        v_ref[hd] = v_t[V_DIM * hd:V_DIM * (hd + 1), :].astype(BF16)


def _kv(lat, kr, w_uk, w_uv_t, g_qk_k):
    b, s, _ = lat.shape
    ts = min(s, ATTN_TILE)
    return pl.pallas_call(
        _kv_body,
        grid=(b, s // ts),
        in_specs=[pl.BlockSpec((None, ts, KV_LORA), lambda bi, i: (bi, i, 0)),
                  pl.BlockSpec((None, ts, ROPE), lambda bi, i: (bi, i, 0)),
                  pl.BlockSpec(w_uk.shape, lambda bi, i: (0, 0)),
                  pl.BlockSpec(w_uv_t.shape, lambda bi, i: (0, 0)),
                  pl.BlockSpec(g_qk_k.shape, lambda bi, i: (0, 0))],
        out_specs=(pl.BlockSpec((None, HEADS, ts, QK_PAD), lambda bi, i: (bi, 0, i, 0)),
                   pl.BlockSpec((None, HEADS, None, V_DIM, ts), lambda bi, i: (bi, 0, i, 0, 0))),
        out_shape=(jax.ShapeDtypeStruct((b, HEADS, s, QK_PAD), BF16),
                   jax.ShapeDtypeStruct((b, HEADS, s // ts, V_DIM, ts), BF16)),
        compiler_params=_cparams(("arbitrary", "arbitrary")),
        name="kv",
    )(lat, kr, w_uk, w_uv_t, g_qk_k)


def _attn_prompt_body(t, q_ref, qn_ref, k_ref, vt_ref, o_ref, s_a, s_b):
    i = pl.program_id(2)

    def scores(q, j, buf):
        buf[...] = _nt(k_ref[pl.ds(pl.multiple_of(j * t, t), t), :], q)

    def consume(j, buf, carry, masked=False):
        m, l, acc = carry
        s = buf[...]
        if masked:
            visible = (lax.broadcasted_iota(I32, (t, t), 0) // CHUNK) <= (lax.broadcasted_iota(I32, (t, t), 1) // CHUNK)
            s = jnp.where(visible, s, NEG)
        m_new = jnp.maximum(m, jnp.max(s, axis=0, keepdims=True))
        alpha = jnp.exp2(m - m_new)
        p = jnp.exp2(s - m_new)
        l = alpha * l + jnp.sum(p, axis=0, keepdims=True)
        acc = alpha * acc + jnp.dot(vt_ref[j], p.astype(BF16), preferred_element_type=F32)
        return m_new, l, acc

    def run(first, second):
        q = q_ref[...]

        @pl.when(i == 0)
        def _():
            scores(q, 0, first)

        def pair(pp, carry):
            j = 2 * pp
            scores(q, j + 1, second)
            carry = consume(j, first, carry)
            scores(q, j + 2, first)
            return consume(j + 1, second, carry)

        def even_tail(carry):
            scores(qn_ref[...], 0, second)
            return consume(i, first, carry, masked=True)

        def odd_tail(carry):
            scores(q, i, second)
            carry = consume(i - 1, first, carry)
            scores(qn_ref[...], 0, first)
            return consume(i, second, carry, masked=True)

        carry = (jnp.full((1, t), NEG, F32), jnp.zeros((1, t), F32), jnp.zeros((V_DIM, t), F32))
        carry = lax.fori_loop(0, i // 2, pair, carry)
        _, l, acc = lax.cond(i % 2 == 1, odd_tail, even_tail, carry)
        o_ref[...] = (acc / l).T.astype(BF16)

    @pl.when(((i + 1) // 2) % 2 == 0)
    def _():
        run(s_a, s_b)

    @pl.when(((i + 1) // 2) % 2 == 1)
    def _():
        run(s_b, s_a)


def _attn_prompt(q, k, v_t):
    b, _, s, _ = q.shape
    t = v_t.shape[-1]
    nq = s // t
    return pl.pallas_call(
        functools.partial(_attn_prompt_body, t),
        grid=(b, HEADS, nq),
        in_specs=[pl.BlockSpec((None, None, t, QK_PAD), lambda bi, h, i: (bi, h, i, 0)),
                  pl.BlockSpec((None, None, t, QK_PAD), lambda bi, h, i: (bi, h, jnp.minimum(i + 1, nq - 1), 0)),
                  pl.BlockSpec((None, None, s, QK_PAD), lambda bi, h, i: (bi, h, 0, 0)),
                  pl.BlockSpec((None, None, nq, V_DIM, t), lambda bi, h, i: (bi, h, 0, 0, 0))],
        out_specs=pl.BlockSpec((None, t, V_DIM), lambda bi, h, i: (bi, i, h)),
        out_shape=jax.ShapeDtypeStruct((b, s, HEADS * V_DIM), BF16),
        scratch_shapes=[pltpu.VMEM((t, t), F32), pltpu.VMEM((t, t), F32)],
        compiler_params=_cparams(("arbitrary", "arbitrary", "arbitrary"), VMEM_BIG),
        name="attn_prompt",
    )(q, q, k, v_t)


def _attn_sample_body(past, sq, q_ref, plat_ref, pkr_ref, nlat_ref, nkr_ref, wk_ref, wv_ref, gk_ref, o_ref,
                      kp_scr, kn_scr):
    plat = plat_ref[...].astype(BF16)
    nlat = nlat_ref[...].astype(BF16)
    _key_rows(plat, pkr_ref[...], wk_ref, gk_ref, kp_scr)
    _key_rows(nlat, nkr_ref[...], wk_ref, gk_ref, kn_scr)
    vp_t = _nt(wv_ref[...], plat).astype(BF16)
    vn_t = _nt(wv_ref[...], nlat).astype(BF16)
    hq = HEADS * sq
    col = lax.broadcasted_iota(I32, (sq, hq), 1)
    key_chunk = (past + lax.broadcasted_iota(I32, (sq, hq), 0)) // CHUNK
    qry_chunk = (past + col % sq) // CHUNK
    qrow = lax.broadcasted_iota(I32, (hq, 1), 0) // sq
    q_all = jnp.concatenate([q_ref[hd] for hd in range(HEADS)], axis=0)
    s_p = jnp.zeros((past, hq), F32)
    s_n = jnp.zeros((sq, hq), F32)
    for hd in range(HEADS):
        q_h = jnp.where(qrow == hd, q_all, jnp.zeros((), BF16))
        s_p = s_p + _nt(kp_scr[hd], q_h)
        s_n = s_n + _nt(kn_scr[hd], q_h)
    s_n = jnp.where(key_chunk <= qry_chunk, s_n, NEG)
    m = jnp.maximum(jnp.max(s_p, axis=0, keepdims=True), jnp.max(s_n, axis=0, keepdims=True))
    p_p = jnp.exp2(s_p - m)
    p_n = jnp.exp2(s_n - m)
    inv_l = 1.0 / (jnp.sum(p_p, axis=0, keepdims=True) + jnp.sum(p_n, axis=0, keepdims=True))
    p_p = p_p.astype(BF16)
    p_n = p_n.astype(BF16)
    lane_head = lax.broadcasted_iota(I32, (V_DIM, hq), 1) // sq
    o_t = jnp.zeros((V_DIM, hq), F32)
    for hd in range(HEADS):
        rows = slice(V_DIM * hd, V_DIM * (hd + 1))
        o_h = (jnp.dot(vp_t[rows, :], p_p, preferred_element_type=F32)
               + jnp.dot(vn_t[rows, :], p_n, preferred_element_type=F32))
        o_t = o_t + jnp.where(lane_head == hd, o_h, 0.0)
    o_all = (o_t * inv_l).T
    for hd in range(HEADS):
        o_ref[:, V_DIM * hd:V_DIM * (hd + 1)] = o_all[sq * hd:sq * (hd + 1), :].astype(BF16)


def _attn_sample(q, past_lat, past_kr, lat, kr, w_uk, w_uv_t, g_qk_k):
    b, _, sq, _ = q.shape
    past = past_lat.shape[1]
    rows = lambda n, w: pl.BlockSpec((None, n, w), lambda bi: (bi, 0, 0))
    full = lambda a: pl.BlockSpec(a.shape, lambda bi: (0,) * a.ndim)
    return pl.pallas_call(
        functools.partial(_attn_sample_body, past, sq),
        grid=(b,),
        in_specs=[pl.BlockSpec((None, HEADS, sq, QK_PAD), lambda bi: (bi, 0, 0, 0)),
                  rows(past, KV_LORA), rows(past, ROPE), rows(sq, KV_LORA), rows(sq, ROPE),
                  full(w_uk), full(w_uv_t), full(g_qk_k)],
        out_specs=rows(sq, HEADS * V_DIM),
        out_shape=jax.ShapeDtypeStruct((b, sq, HEADS * V_DIM), BF16),
        scratch_shapes=[pltpu.VMEM((HEADS, past, QK_PAD), BF16), pltpu.VMEM((HEADS, sq, QK_PAD), BF16)],
        compiler_params=_cparams(("arbitrary",), VMEM_SMALL),
        name="attn_sample",
    )(q, past_lat, past_kr, lat, kr, w_uk, w_uv_t, g_qk_k)


def _gla_masks(c, rows):
    idx = np.arange(rows)
    same = (idx // c)[:, None] == (idx // c)[None, :]
    le = same & (idx[None, :] <= idx[:, None])
    gt = same & (idx[None, :] > idx[:, None])
    return np.concatenate([le, gt], axis=0).astype(np.float32), int(np.log2(c))


def _level_exponents(b, g, c, level):
    n = c >> level
    rows = b.shape[0]
    row = lax.broadcasted_iota(I32, (rows, 1), 0)
    if n >= 8:
        split = b.reshape(rows // n, n, LANES)[:, n // 2 - 1:n // 2, :]
        split = jnp.broadcast_to(split, (rows // n, n, LANES)).reshape(rows, LANES)
        return jnp.where((row & (n // 2)) != 0, b - split, split - b)
    g_prev = pltpu.roll(g, 1, 0)
    g_next = pltpu.roll(g, rows - 1, 0)
    if n == 4:
        r = row & 3
        return jnp.where(r == 0, g_next, jnp.where(r == 1, 0.0, jnp.where(r == 2, g, g + g_prev)))
    assert n == 2
    return jnp.where((row & 1) != 0, g, 0.0)


def _gla_body(c, n_chunks, unit, levels, mall_ref, q_ref, k_ref, g_ref, v_ref, r_ref, s0_ref, gout_ref,
              o_ref, sfin_ref, st_scr):
    it = pl.program_id(1)

    @pl.when(it == 0)
    def _():
        st_scr[...] = s0_ref[...]

    ru = unit * c
    lane = lax.broadcasted_iota(I32, (ru, LANES), 1)
    head_lanes = (lane < GLA_DK, lane >= GLA_DK)
    st_lane_lo = lax.broadcasted_iota(I32, (GLA_DV, LANES), 1) < GLA_DK
    row = lax.broadcasted_iota(I32, (ru, 1), 0)
    ri = lax.broadcasted_iota(I32, (ru, ru), 0)
    ci = lax.broadcasted_iota(I32, (ru, ru), 1)
    mall = mall_ref[...]

    for un in range(n_chunks // unit):
        rows = slice(un * ru, (un + 1) * ru)
        for p in range(HEADS // 2):
            ls = slice(LANES * p, LANES * (p + 1))
            g = g_ref[rows, ls]
            q = q_ref[rows, ls]
            k = k_ref[rows, ls]
            g_hi = g.astype(BF16)
            g_lo = (g - g_hi.astype(F32)).astype(BF16)
            e2 = jnp.dot(mall, jnp.concatenate([g_hi, g_lo], axis=1), preferred_element_type=F32)
            e = e2[:, 0:LANES] + e2[:, LANES:2 * LANES]
            b = e[0:ru]
            eb = jnp.exp(b)
            qb = q * eb
            kd = (k * jnp.exp(e[ru:2 * ru])).astype(BF16)
            qs, ks = [q], [k.astype(BF16)]
            for l in range(levels):
                bottom = (row & (c >> (l + 1))) != 0
                decay = jnp.exp(_level_exponents(b, g, c, l))
                qs.append(jnp.where(bottom, q * decay, 0.0))
                ks.append(jnp.where(bottom, 0.0, k * decay).astype(BF16))
            states = [st_scr[p]]
            for j in range(unit):
                cr = slice(j * c, (j + 1) * c)
                upd = [lax.dot_general(v_ref[rows, GLA_DV * (2 * p + hh):GLA_DV * (2 * p + hh + 1)][cr, :], kd[cr, :],
                                       (((0,), (0,)), ((), ())), preferred_element_type=F32) for hh in range(2)]
                d_last = eb[j * c + c - 1:j * c + c, :]
                states.append(states[-1] * d_last + jnp.where(st_lane_lo, upd[0], upd[1]))
            st_scr[p] = states[-1]
            for hh in range(2):
                hd = 2 * p + hh
                sel = head_lanes[hh]
                a = jnp.where(ri == ci, _nt(jnp.where(sel, qs[0], 0.0).astype(BF16), ks[0]), 0.0)
                for l in range(levels):
                    pr = _nt(jnp.where(sel, qs[l + 1], 0.0).astype(BF16), ks[l + 1])
                    a = a + jnp.where((ri ^ ci) < (c >> l), pr, 0.0)
                vh = v_ref[rows, GLA_DV * hd:GLA_DV * (hd + 1)]
                qb_h = jnp.where(sel, qb, 0.0).astype(BF16)
                o_state = [_nt(qb_h[j * c:(j + 1) * c, :], states[j].astype(BF16)) for j in range(unit)]
                o = jnp.dot(a.astype(BF16), vh, preferred_element_type=F32) + jnp.concatenate(o_state, axis=0)
                on = o * _rms(o, GLA_DV) * gout_ref[...]
                r = r_ref[rows, GLA_DV * hd:GLA_DV * (hd + 1)]
                o_ref[rows, GLA_DV * hd:GLA_DV * (hd + 1)] = (on * (r * jax.nn.sigmoid(r))).astype(BF16)

    @pl.when(it == pl.num_programs(1) - 1)
    def _():
        sfin_ref[...] = st_scr[...]


def _gla(gq, gk, gl, gv, gr, s0, g_out):
    b, s, _ = gq.shape
    c = min(CHUNK, s)
    tile = min(s, 8 * c)
    unit = next(u for u in (4, 2, 1) if (tile // c) % u == 0)
    masks, levels = _gla_masks(c, unit * c)
    mall = jnp.asarray(masks, BF16)
    tok = lambda w: pl.BlockSpec((None, tile, w), lambda bi, i: (bi, i, 0))
    st_spec = pl.BlockSpec((None, HEADS // 2, GLA_DV, LANES), lambda bi, i: (bi, 0, 0, 0))
    return pl.pallas_call(
        functools.partial(_gla_body, c, tile // c, unit, levels),
        grid=(b, s // tile),
        in_specs=[pl.BlockSpec(mall.shape, lambda bi, i: (0, 0)),
                  tok(HEADS * GLA_DK), tok(HEADS * GLA_DK), tok(HEADS * GLA_DK), tok(HEADS * GLA_DV),
                  tok(HEADS * GLA_DV), st_spec, pl.BlockSpec(g_out.shape, lambda bi, i: (0, 0))],
        out_specs=(tok(HEADS * GLA_DV), st_spec),
        out_shape=(jax.ShapeDtypeStruct((b, s, HEADS * GLA_DV), BF16),
                   jax.ShapeDtypeStruct((b, HEADS // 2, GLA_DV, LANES), F32)),
        scratch_shapes=[pltpu.VMEM((HEADS // 2, GLA_DV, LANES), F32)],
        compiler_params=_cparams(("arbitrary", "arbitrary")),
        name="gla",
    )(mall, gq, gk, gl, gv, gr, s0, g_out)


def _state_to_pairs(s):
    b = s.shape[0]
    s = s.reshape(b, HEADS // 2, 2, GLA_DK, GLA_DV)
    return jnp.transpose(s, (0, 1, 4, 2, 3)).reshape(b, HEADS // 2, GLA_DV, 2 * GLA_DK)


def _state_from_pairs(s):
    b = s.shape[0]
    s = s.reshape(b, HEADS // 2, GLA_DV, 2, GLA_DK)
    return jnp.transpose(s, (0, 1, 3, 4, 2)).reshape(b, HEADS, GLA_DK, GLA_DV)


def _post_body(x_ref, om_ref, og_ref, gt_ref, sc_ref, sh_ref, wo_ref, gffn_ref, wr_ref, br_ref,
               x2_ref, h_ref, idx_ref, wt_ref, rank_ref, cnt_ref):
    half = om_ref.shape[-1]
    mix = (jnp.dot(om_ref[...], wo_ref[0:half, :], preferred_element_type=F32)
           + jnp.dot(og_ref[...], wo_ref[half:2 * half, :], preferred_element_type=F32))
    x2 = x_ref[...] + gt_ref[...] * mix
    x2_ref[...] = x2
    d = x2.shape[-1]
    h = (x2 * _rms(x2, d) * gffn_ref[...]) * (1.0 + sc_ref[...]) + sh_ref[...]
    h_hi = h.astype(BF16)
    h_ref[...] = h_hi
    h_lo = (h - h_hi.astype(F32)).astype(BF16)
    logits = _nt(wr_ref[0], h_hi) + _nt(wr_ref[0], h_lo) + _nt(wr_ref[1], h_hi) + br_ref[...]
    n_exp, tm = logits.shape
    eid = lax.broadcasted_iota(I32, (n_exp, tm), 0)
    vals, tops, ids = logits, [], []
    for _ in range(TOP_K):
        m = jnp.max(vals, axis=0, keepdims=True)
        sel = jnp.min(jnp.where(vals == m, eid, n_exp), axis=0, keepdims=True)
        tops.append(m)
        ids.append(sel)
        vals = jnp.where(eid == sel, -jnp.inf, vals)
    es = [jnp.exp(t - tops[0]) for t in tops]
    tot = es[0] + es[1] + es[2] + es[3]
    idx_ref[...] = jnp.concatenate(ids, axis=0)
    wt_ref[...] = jnp.concatenate([e / tot for e in es], axis=0)
    hits = [eid == sel for sel in ids]
    member = jnp.zeros((n_exp, tm), F32)
    for hk in hits:
        member = member + jnp.where(hk, 1.0, 0.0)
    before = lax.broadcasted_iota(I32, (tm, tm), 0) < lax.broadcasted_iota(I32, (tm, tm), 1)
    prefix = jnp.dot(member.astype(BF16), jnp.where(before, 1.0, 0.0).astype(BF16), preferred_element_type=F32)
    rank_ref[...] = jnp.concatenate(
        [jnp.sum(jnp.where(hk, prefix, 0.0), axis=0, keepdims=True) for hk in hits], axis=0).astype(I32)
    cnt_ref[...] = jnp.broadcast_to(jnp.sum(member, axis=1, keepdims=True), (n_exp, LANES)).astype(I32)


def _post(x, om, og, gate, scale, shift, w_o, g_ffn, w_r2, b_r):
    t, d = x.shape
    tm = ROUTE_TILE
    per_tok = gate.shape[0] == t
    mod = pl.BlockSpec((tm, d), lambda i: (i, 0)) if per_tok else pl.BlockSpec((1, d), lambda i: (0, 0))
    tok = lambda w: pl.BlockSpec((tm, w), lambda i: (i, 0))
    full = lambda a: pl.BlockSpec(a.shape, lambda i: (0,) * a.ndim)
    return pl.pallas_call(
        _post_body,
        grid=(t // tm,),
        in_specs=[tok(d), tok(om.shape[1]), tok(og.shape[1]), mod, mod, mod, full(w_o), full(g_ffn), full(w_r2),
                  full(b_r)],
        out_specs=(tok(d), tok(d),
                   pl.BlockSpec((TOP_K, tm), lambda i: (0, i)), pl.BlockSpec((TOP_K, tm), lambda i: (0, i)),
                   pl.BlockSpec((TOP_K, tm), lambda i: (0, i)),
                   pl.BlockSpec((None, N_EXPERTS, LANES), lambda i: (i, 0, 0))),
        out_shape=(jax.ShapeDtypeStruct((t, d), F32), jax.ShapeDtypeStruct((t, d), BF16),
                   jax.ShapeDtypeStruct((TOP_K, t), I32), jax.ShapeDtypeStruct((TOP_K, t), F32),
                   jax.ShapeDtypeStruct((TOP_K, t), I32), jax.ShapeDtypeStruct((t // tm, N_EXPERTS, LANES), I32)),
        compiler_params=_cparams(("arbitrary",), VMEM_SMALL),
        name="post",
    )(x, om, og, gate, scale, shift, w_o, g_ffn, w_r2, b_r)


def _max_tile_rows():
    return TOP_K * ROUTE_TILE + N_EXPERTS * (RUN_CHUNK - 1)


def _sort_capacity():
    return _round_up(_max_tile_rows(), SORT_ROWS)


class _Tab:
    def __init__(self):
        self.caps = [_max_tile_rows() // PIECE_ROWS[0]] + [N_EXPERTS] * (len(PIECE_ROWS) - 1)
        self.src, self.dst, o = [], [], 0
        for cap in self.caps:
            self.src.append(o)
            self.dst.append(o + cap)
            o += 2 * cap
        self.count = [o + k for k in range(len(PIECE_ROWS))]
        self.n_rows = o + len(PIECE_ROWS)
        self.width = self.n_rows + 1


def _route_tables(idx, lrank, cnt3):
    nt = cnt3.shape[0]
    t = idx.shape[1]
    cnt = cnt3[:, :, 0]
    run = _round_up(cnt, RUN_CHUNK)
    lo_end = jnp.cumsum(run, axis=1)
    lo = lo_end - run
    region = _round_up(jnp.sum(run, axis=0), EXPERT_ROWS)
    g_end = jnp.cumsum(region)
    run_dest = (g_end - region)[None, :] + jnp.cumsum(run, axis=0) - run
    def piece_list(count, n_out):
        end = jnp.cumsum(count, axis=1)
        p = jnp.arange(n_out, dtype=I32)
        e = jnp.minimum(jnp.sum(end[:, None, :] <= p[None, :, None], axis=2), N_EXPERTS - 1)
        pick = e[:, :, None] == jnp.arange(N_EXPERTS, dtype=I32)[None, None, :]
        of_run = lambda a: jnp.sum(jnp.where(pick, a[:, None, :], 0), axis=2)
        return of_run, p[None, :] - of_run(end - count), end[:, -1]

    done, lists, counts = jnp.zeros_like(run), [], []
    for rows, cap in zip(PIECE_ROWS, _Tab().caps):
        count = (run - done) // rows
        of_run, within, total = piece_list(count, cap)
        lists += [of_run(lo + done) + rows * within, of_run(run_dest + done) + rows * within]
        counts.append(total[:, None])
        done = done + count * rows
    table = jnp.concatenate(lists + counts + [lo_end[:, -1:]], axis=1).astype(I32)
    table = table.reshape(nt, 1, _Tab().width)
    eid =jnp.arange(N_EXPERTS, dtype=I32)[:, None]
    lo_tok = jnp.repeat(lo.T, ROUTE_TILE, axis=1)
    lpos = jnp.stack([jnp.sum(jnp.where(idx[k][None, :] == eid, lo_tok, 0), axis=0) for k in range(TOP_K)])
    lpos = (lpos + lrank).astype(I32)
    n_blocks = _round_up(t * TOP_K + nt * N_EXPERTS * (RUN_CHUNK - 1), EXPERT_ROWS) // EXPERT_ROWS + N_EXPERTS
    b_start = jnp.arange(n_blocks, dtype=I32) * EXPERT_ROWS
    blk_e = jnp.minimum(jnp.sum(g_end[None, :] <= b_start[:, None], axis=1), N_EXPERTS - 1).astype(I32)
    n_valid = (g_end[-1:] // EXPERT_ROWS).astype(I32)
    used = region > 0
    e_ids = jnp.arange(N_EXPERTS, dtype=I32)
    later_used = used[None, :] & (e_ids[None, :] > e_ids[:, None])
    nxt_e = jnp.min(jnp.where(later_used, e_ids[None, :], N_EXPERTS), axis=1)
    nxt_e = jnp.where(nxt_e < N_EXPERTS, nxt_e, -1).astype(I32)
    ord_e = (jnp.cumsum(used.astype(I32)) - 1).astype(I32)
    tail = jnp.concatenate([jnp.where(used, g_end - EXPERT_ROWS, -1), n_valid]).astype(I32)
    tail = tail.reshape(1, N_EXPERTS + 1)
    return dict(table=table, lpos=lpos, blk_e=blk_e, n_valid=n_valid, nxt_e=nxt_e, ord_e=ord_e, tail=tail,
                n_blocks=n_blocks)


def _chunk_copy(src, dst, sem):
    return pltpu.make_async_copy(src, dst, sem)


def _for_row_blocks(n_rows, body):
    full = n_rows // SORT_ROWS
    rem = n_rows - full * SORT_ROWS
    tail = pl.multiple_of(full * SORT_ROWS, SORT_ROWS)

    def whole(rb, carry):
        body(pl.multiple_of(rb * SORT_ROWS, SORT_ROWS), SORT_ROWS)
        return carry

    lax.fori_loop(0, full, whole, 0)

    @pl.when(rem > SORT_ROWS // 2)
    def _():
        body(tail, SORT_ROWS)

    @pl.when((rem > 0) & (rem <= SORT_ROWS // 2))
    def _():
        body(tail, SORT_ROWS // 2)


def _for_chunks(n, body):
    groups = n // CHUNK_UNROLL

    def group(g, carry):
        for u in range(CHUNK_UNROLL):
            body(g * CHUNK_UNROLL + u)
        return carry

    def single(c, carry):
        body(c)
        return carry

    lax.fori_loop(0, groups, group, 0)
    lax.fori_loop(groups * CHUNK_UNROLL, n, single, 0)


def _scatter_body(nt_a, n_blocks, tab_ref, prv_ref, tail_ref, lpos_ref, ha_ref, hb_ref, xout_ref, sorted_scr,
                  zero_scr, sems, zero_sem):
    i = pl.program_id(0)
    slot = i % 2
    tb = _Tab()
    tt = ha_ref.shape[0]

    @pl.when(i == 0)
    def _():
        zero_scr[...] = jnp.zeros_like(zero_scr)
        n_valid = tail_ref[0, N_EXPERTS]

        def block(start):
            return xout_ref.at[pl.ds(pl.multiple_of(start, EXPERT_ROWS), EXPERT_ROWS)]

        for e in range(N_EXPERTS):
            @pl.when(tail_ref[0, e] >= 0)
            def _():
                _chunk_copy(zero_scr, block(tail_ref[0, e]), zero_sem).start()

        def fill(b, carry):
            _chunk_copy(zero_scr, block(b * EXPERT_ROWS), zero_sem).start()
            return carry

        def fill_done(b, carry):
            _chunk_copy(zero_scr, block(0), zero_sem).wait()
            return carry

        lax.fori_loop(n_valid, n_blocks, fill, 0)
        for e in range(N_EXPERTS):
            @pl.when(tail_ref[0, e] >= 0)
            def _():
                _chunk_copy(zero_scr, block(0), zero_sem).wait()
        lax.fori_loop(n_valid, n_blocks, fill_done, 0)

    lp16 = lpos_ref[...].astype(jnp.int16)
    h = jnp.where(i < nt_a, ha_ref[...], hb_ref[...])

    def sort_block(r0, size):
        rid = (r0 + lax.broadcasted_iota(I32, (size, tt), 0)).astype(jnp.int16)
        onehot = jnp.zeros((size, tt), BF16)
        for k in range(TOP_K):
            onehot = onehot + jnp.where(lp16[k:k + 1, :] == rid, jnp.ones((), BF16), jnp.zeros((), BF16))
        sorted_scr[slot, pl.ds(r0, size), :] = jnp.dot(onehot, h, preferred_element_type=F32).astype(BF16)

    _for_row_blocks(tab_ref[0, tb.n_rows], sort_block)

    def piece(sl, src_row, dst_row, rows):
        src = sorted_scr.at[sl, pl.ds(pl.multiple_of(src_row, RUN_CHUNK), rows)]
        dst = xout_ref.at[pl.ds(pl.multiple_of(dst_row, RUN_CHUNK), rows)]
        return _chunk_copy(src, dst, sems.at[sl])

    def retire(tab, sl):
        for k, rows in enumerate(PIECE_ROWS):
            _for_chunks(tab[0, tb.count[k]], lambda p, rows=rows: piece(sl, 0, 0, rows).wait())

    for k, rows in enumerate(PIECE_ROWS):
        _for_chunks(tab_ref[0, tb.count[k]], lambda p, k=k, rows=rows: piece(
            slot, tab_ref[0, tb.src[k] + p], tab_ref[0, tb.dst[k] + p], rows).start())

    @pl.when(i > 0)
    def _():
        retire(prv_ref, 1 - slot)

    @pl.when(i == pl.num_programs(0) - 1)
    def _():
        retire(tab_ref, slot)


def _scatter(table, tail, lpos, h_a, h_b, n_blocks):
    d = h_a.shape[1]
    nt_a, nt_b = h_a.shape[0] // ROUTE_TILE, h_b.shape[0] // ROUTE_TILE
    sort_cap = _sort_capacity()
    tab_spec = lambda f: pl.BlockSpec((None, 1, _Tab().width), lambda i: (f(i), 0, 0), memory_space=pltpu.SMEM)
    return pl.pallas_call(
        functools.partial(_scatter_body, nt_a, n_blocks),
        grid=(nt_a + nt_b,),
        in_specs=[tab_spec(lambda i: i), tab_spec(lambda i: jnp.maximum(i - 1, 0)),
                  pl.BlockSpec((1, N_EXPERTS + 1), lambda i: (0, 0), memory_space=pltpu.SMEM),
                  pl.BlockSpec((TOP_K, ROUTE_TILE), lambda i: (0, i)),
                  pl.BlockSpec((ROUTE_TILE, d), lambda i: (jnp.minimum(i, nt_a - 1), 0)),
                  pl.BlockSpec((ROUTE_TILE, d), lambda i: (jnp.maximum(i - nt_a, 0), 0))],
        out_specs=pl.BlockSpec(memory_space=pl.ANY),
        out_shape=jax.ShapeDtypeStruct((n_blocks * EXPERT_ROWS, d), BF16),
        scratch_shapes=[pltpu.VMEM((2, sort_cap, d), BF16), pltpu.VMEM((EXPERT_ROWS, d), BF16),
                        pltpu.SemaphoreType.DMA((2,)), pltpu.SemaphoreType.DMA(())],
        compiler_params=_cparams(("arbitrary",), VMEM_MID),
        name="scatter",
    )(table, table, tail, lpos, h_a, h_b)


def _experts_body(be_ref, nv_ref, nxt_ref, ord_ref, x_ref, wgu_hbm, bgu_ref, wd_hbm, bd_ref, y_ref,
                  wgu_f, wd_f, wgu_s, wd_s, sem_gu, sem_d):
    b = pl.program_id(0)
    e = be_ref[b]
    prev = be_ref[jnp.maximum(b - 1, 0)]
    valid = b < nv_ref[0]
    d_ff = wd_s.shape[0]
    slot = ord_ref[e] % 2

    def weights(expert, sl):
        return (pltpu.make_async_copy(wgu_hbm.at[expert], wgu_f.at[sl], sem_gu.at[sl]),
                pltpu.make_async_copy(wd_hbm.at[expert], wd_f.at[sl], sem_d.at[sl]))

    @pl.when(valid & ((b == 0) | (e != prev)))
    def _():
        @pl.when(b == 0)
        def _():
            for cp in weights(e, slot):
                cp.start()

        for cp in weights(e, slot):
            cp.wait()

        @pl.when(nxt_ref[e] >= 0)
        def _():
            for cp in weights(nxt_ref[e], 1 - slot):
                cp.start()

        wgu_s[...] = wgu_f[slot].astype(BF16)
        wd_s[...] = wd_f[slot].astype(BF16)

    @pl.when(valid)
    def _():
        gu = jnp.dot(x_ref[...], wgu_s[...], preferred_element_type=F32) + bgu_ref[...]
        gate = jnp.minimum(gu[:, 0:d_ff], SWIGLU_LIMIT)
        up = jnp.clip(gu[:, d_ff:2 * d_ff], -SWIGLU_LIMIT, SWIGLU_LIMIT)
        act = ((up + 1.0) * (gate * jax.nn.sigmoid(gate * SWIGLU_ALPHA))).astype(BF16)
        y_ref[...] = (jnp.dot(act, wd_s[...], preferred_element_type=F32) + bd_ref[...]).astype(BF16)

    @pl.when(jnp.logical_not(valid))
    def _():
        y_ref[...] = jnp.zeros_like(y_ref)


def _experts(blk_e, n_valid, nxt_e, ord_e, xpad, w_gu, b_gu, w_down, b_down):
    m, d = xpad.shape
    nb = m // EXPERT_ROWS
    n_exp, _, f2 = w_gu.shape
    d_ff = w_down.shape[1]
    last = lambda b, be, nv: jnp.minimum(b, nv[0] - 1)
    grid_spec = pltpu.PrefetchScalarGridSpec(
        num_scalar_prefetch=4,
        grid=(nb,),
        in_specs=[pl.BlockSpec((EXPERT_ROWS, d), lambda b, be, nv, nx, od: (last(b, be, nv), 0)),
                  pl.BlockSpec(memory_space=pl.ANY),
                  pl.BlockSpec((None, 1, f2), lambda b, be, nv, nx, od: (be[last(b, be, nv)], 0, 0)),
                  pl.BlockSpec(memory_space=pl.ANY),
                  pl.BlockSpec((None, 1, d), lambda b, be, nv, nx, od: (be[last(b, be, nv)], 0, 0))],
        out_specs=pl.BlockSpec((EXPERT_ROWS, d), lambda b, be, nv, nx, od: (b, 0)),
        scratch_shapes=[pltpu.VMEM((2, d, f2), F32), pltpu.VMEM((2, d_ff, d), F32),
                        pltpu.VMEM((d, f2), BF16), pltpu.VMEM((d_ff, d), BF16),
                        pltpu.SemaphoreType.DMA((2,)), pltpu.SemaphoreType.DMA((2,))],
    )
    return pl.pallas_call(
        _experts_body,
        grid_spec=grid_spec,
        out_shape=jax.ShapeDtypeStruct((m, d), BF16),
        compiler_params=_cparams(("arbitrary",), VMEM_BIG),
        name="experts",
    )(blk_e, n_valid, nxt_e, ord_e, xpad, w_gu, b_gu.reshape(n_exp, 1, f2), w_down, b_down.reshape(n_exp, 1, d))


def _combine_body(tab_ref, nxt_ref, lpos_ref, wt_ref, x2_ref, gt_ref, y_ref, o_ref, ysort_scr, sems):
    i = pl.program_id(0)
    slot = i % 2
    tb = _Tab()
    tt = x2_ref.shape[0]

    def piece(sl, sorted_row, expert_row, rows):
        src = y_ref.at[pl.ds(pl.multiple_of(expert_row, RUN_CHUNK), rows)]
        dst = ysort_scr.at[sl, pl.ds(pl.multiple_of(sorted_row, RUN_CHUNK), rows)]
        return _chunk_copy(src, dst, sems.at[sl])

    def fetch(tab, sl):
        for k, rows in enumerate(PIECE_ROWS):
            _for_chunks(tab[0, tb.count[k]], lambda p, k=k, rows=rows: piece(
                sl, tab[0, tb.src[k] + p], tab[0, tb.dst[k] + p], rows).start())

    @pl.when(i == 0)
    def _():
        ysort_scr[...] = jnp.zeros_like(ysort_scr)
        fetch(tab_ref, 0)

    @pl.when(i + 1 < pl.num_programs(0))
    def _():
        fetch(nxt_ref, 1 - slot)

    for k, rows in enumerate(PIECE_ROWS):
        _for_chunks(tab_ref[0, tb.count[k]], lambda p, rows=rows: piece(slot, 0, 0, rows).wait())

    lp16 = lpos_ref[...].astype(jnp.int16)
    w = wt_ref[...].astype(BF16)
    o_ref[...] = x2_ref[...]
    gt = gt_ref[...]

    def gather_block(r0, size):
        rid = (r0 + lax.broadcasted_iota(I32, (size, tt), 0)).astype(jnp.int16)
        pw_t = jnp.zeros((size, tt), BF16)
        for k in range(TOP_K):
            pw_t = pw_t + jnp.where(lp16[k:k + 1, :] == rid, w[k:k + 1, :], jnp.zeros((), BF16))
        ys = ysort_scr[slot, pl.ds(r0, size), :]
        part = lax.dot_general(pw_t, ys, (((0,), (0,)), ((), ())), preferred_element_type=F32)
        o_ref[...] = o_ref[...] + gt * part

    _for_row_blocks(tab_ref[0, tb.n_rows], gather_block)


def _combine(table, lpos, wts, x2, gate, ypad):
    t, d = x2.shape
    nt = t // ROUTE_TILE
    per_tok = gate.shape[0] == t
    mod = pl.BlockSpec((ROUTE_TILE, d), lambda i: (i, 0)) if per_tok else pl.BlockSpec((1, d), lambda i: (0, 0))
    sort_cap = _sort_capacity()
    tab_spec = lambda f: pl.BlockSpec((None, 1, _Tab().width), lambda i: (f(i), 0, 0), memory_space=pltpu.SMEM)
    return pl.pallas_call(
        _combine_body,
        grid=(nt,),
        in_specs=[tab_spec(lambda i: i), tab_spec(lambda i: jnp.minimum(i + 1, nt - 1)),
                  pl.BlockSpec((TOP_K, ROUTE_TILE), lambda i: (0, i)),
                  pl.BlockSpec((TOP_K, ROUTE_TILE), lambda i: (0, i)),
                  pl.BlockSpec((ROUTE_TILE, d), lambda i: (i, 0)),
                  mod,
                  pl.BlockSpec(memory_space=pl.ANY)],
        out_specs=pl.BlockSpec((ROUTE_TILE, d), lambda i: (i, 0)),
        out_shape=jax.ShapeDtypeStruct((t, d), F32),
        scratch_shapes=[pltpu.VMEM((2, sort_cap, d), BF16), pltpu.SemaphoreType.DMA((2,))],
        compiler_params=_cparams(("arbitrary",), VMEM_MID),
        name="combine",
    )(table, table, lpos, wts, x2, gate, ypad)


def _prep_weights(g_norm_mix, w_in, g_q_a, w_uq, g_kv_a, w_ukv, g_qk_q, g_qk_k, w_g2, b_g2, g_gla_out, w_o,
                  g_norm_ffn, w_router, b_router):
    d = w_in.shape[0]
    w1 = _w1(w_in.T)
    wq = w_uq.reshape(Q_LORA, HEADS, QK)
    wuq = jnp.concatenate([wq[:, :, 0:NOPE].reshape(Q_LORA, HEADS * NOPE),
                           wq[:, :, NOPE:QK].reshape(Q_LORA, HEADS * ROPE)], axis=1).astype(BF16)
    wkv = w_ukv.reshape(KV_LORA, HEADS, NOPE + V_DIM)
    wuk = wkv[:, :, 0:NOPE].reshape(KV_LORA, HEADS * NOPE).astype(BF16)
    wuv_t = wkv[:, :, NOPE:].reshape(KV_LORA, HEADS * V_DIM).T.astype(BF16)
    pad_rope = lambda g: jnp.stack([g[0:NOPE], jnp.concatenate([g[NOPE:QK], jnp.zeros((QK_PAD - QK,), g.dtype)])])
    inv = ROPE_THETA ** (-jnp.arange(HALF, dtype=F32) / HALF)
    sign = jnp.concatenate([-jnp.ones((HALF,), F32), jnp.ones((HALF,), F32)])
    rope_tab = jnp.stack([jnp.tile(inv, LANES // HALF), jnp.tile(sign, LANES // ROPE)])
    wg2 = jnp.concatenate([w_g2, jnp.zeros((LANES - GATE_RANK, w_g2.shape[1]), w_g2.dtype)], axis=0).astype(BF16)
    wr_t = w_router.T
    wr_hi = wr_t.astype(BF16)
    wr_lo = (wr_t - wr_hi.astype(F32)).astype(BF16)
    return dict(
        g_mix=g_norm_mix.reshape(1, d), w1=w1, g_qa=g_q_a.reshape(1, -1), w_uq=wuq, g_kv=g_kv_a.reshape(1, -1),
        w_uk=wuk, w_uv_t=wuv_t, g_qk_q=pad_rope(g_qk_q), g_qk_k=pad_rope(g_qk_k), rope=rope_tab, w_g2=wg2,
        b_g2=b_g2.reshape(1, -1), g_out=g_gla_out.reshape(1, -1), w_o=w_o.astype(BF16),
        g_ffn=g_norm_ffn.reshape(1, d), w_r2=jnp.stack([wr_hi, wr_lo]), b_r=b_router.reshape(-1, 1))


def _mixer(x, mod, pos0, past_lat, past_kr, s0_pairs, wts):
    b, s, d = x.shape
    t = b * s
    if b > 1 and s & (s - 1) == 0 and min(t, TOKEN_TILE) % s == 0:
        per_tok = lambda j: jnp.broadcast_to(mod[:, j:j + 1], (b, s, d)).reshape(1, t, d)
        outs = _proj(x.reshape(1, t, d), per_tok(0), per_tok(1), pos0, wts, period=s)
        q = outs[0].reshape(HEADS, b, s, QK_PAD).transpose(1, 0, 2, 3)
        lat, kr, gq, gk, gv, gl, gr = [o.reshape(b, s, o.shape[-1]) for o in outs[1:]]
    else:
        q, lat, kr, gq, gk, gv, gl, gr = _proj(x, mod[:, 0:1], mod[:, 1:2], pos0, wts)
    kv_w = (wts["w_uk"], wts["w_uv_t"], wts["g_qk_k"])
    if past_lat is None:
        k_new, vt_new = _kv(lat, kr, *kv_w)
        o_mla = _attn_prompt(q, k_new, vt_new)
    else:
        o_mla = _attn_sample(q, past_lat, past_kr, lat, kr, *kv_w)
    o_gla, s_fin = _gla(gq, gk, gl, gv, gr, s0_pairs, wts["g_out"])
    if b == 1:
        rows = lambda j: mod[0, j:j + 1]
    else:
        rows = lambda j: jnp.broadcast_to(mod[:, j:j + 1], (b, s, d)).reshape(t, d)
    x2, h2, idx, wt, lrank, cnt = _post(x.reshape(t, d), o_mla.reshape(t, -1), o_gla.reshape(t, -1), rows(2), rows(4),
                                        rows(3), wts["w_o"], wts["g_ffn"], wts["w_r2"], wts["b_r"])
    return dict(x2=x2, h2=h2, idx=idx, wt=wt, lrank=lrank, cnt=cnt, gate_f=rows(5), lat=lat, kr=kr, s_fin=s_fin)


def kernel(x_prompt, x_sample, cache_mla_latent, cache_mla_krope, state_gla, c_prompt, c_sample, w_ada, b_ada, g_norm_mix, w_in, g_q_a, w_uq, g_kv_a, w_ukv, g_qk_q, g_qk_k, w_g2, b_g2, g_gla_out, w_o, g_norm_ffn, w_router, b_router, w_gu, b_gu, w_down, b_down):
    depth = w_ada.shape[0]
    assert depth == 1, "single-layer step"
    bp, sp, d = x_prompt.shape
    bs, ss, _ = x_sample.shape
    tp, tsm = bp * sp, bs * ss
    assert tp % ROUTE_TILE == 0 and tsm % ROUTE_TILE == 0, "token counts must be whole routing tiles"
    past = cache_mla_latent.shape[2]
    layer = lambda a: a.reshape(a.shape[1:])
    wts = _prep_weights(*[layer(a) for a in (g_norm_mix, w_in, g_q_a, w_uq, g_kv_a, w_ukv, g_qk_q, g_qk_k, w_g2, b_g2,
                                             g_gla_out, w_o, g_norm_ffn, w_router, b_router)])
    w_gu, b_gu, w_down, b_down = layer(w_gu), layer(b_gu), layer(w_down), layer(b_down)

    mod = _ada(jnp.concatenate([c_prompt, c_sample], axis=0), layer(w_ada), layer(b_ada)).reshape(bp + bs, 6, d)
    zero_state = jnp.zeros((bp, HEADS // 2, GLA_DV, LANES), F32)
    pr = _mixer(x_prompt, mod[:bp], 0, None, None, zero_state, wts)
    sa = _mixer(x_sample, mod[bp:], past, layer(cache_mla_latent), layer(cache_mla_krope),
                _state_to_pairs(layer(state_gla)), wts)

    idx = jnp.concatenate([pr["idx"], sa["idx"]], axis=1)
    lrank = jnp.concatenate([pr["lrank"], sa["lrank"]], axis=1)
    rt = _route_tables(idx, lrank, jnp.concatenate([pr["cnt"], sa["cnt"]], axis=0))
    ntp = tp // ROUTE_TILE
    lpos, table = rt["lpos"], rt["table"]
    xpad = _scatter(table, rt["tail"], lpos, pr["h2"], sa["h2"], rt["n_blocks"])
    ypad = _experts(rt["blk_e"], rt["n_valid"], rt["nxt_e"], rt["ord_e"], xpad, w_gu, b_gu, w_down, b_down)
    y_p = _combine(table[:ntp], lpos[:, :tp], pr["wt"], pr["x2"], pr["gate_f"], ypad).reshape(bp, sp, d)
    y_s = _combine(table[ntp:], lpos[:, tp:], sa["wt"], sa["x2"], sa["gate_f"], ypad).reshape(bs, ss, d)

    return (y_p, y_s,
            pr["lat"][None], pr["kr"][None], _state_from_pairs(pr["s_fin"])[None],
            sa["lat"][None], sa["kr"][None], _state_from_pairs(sa["s_fin"])[None])
```
